```python
import math
import jax, jax.numpy as jnp
from jax import lax
import numpy as np

D_MODEL = 2048
BATCH = 4
SEQ = 4096
DEPTH = 2

D_MIX = D_MODEL
A_HEADS = 8
A_HEAD_DIM = 128
A_WIDTH = A_HEADS * A_HEAD_DIM
KV_RANK = 256
IDX_HEADS = 16
IDX_DIM = 64
TOPK_MAX = 256
Q_BLOCK = 128
REL_BUCKETS = 32
REL_MAX_EXACT = REL_BUCKETS // 2
REL_MAX_DIST = 128
R_HEAD_DIM = 64
R_WIDTH = D_MIX - A_WIDTH
R_HEADS = R_WIDTH // R_HEAD_DIM
DECAY_LORA = 96
AAA_LORA = 96
GATE_LORA = 256
GN_EPS = 64e-5
D_FF = 5632
RMS_EPS = 1e-6
N_SUB = 3
A_COLS = (A_WIDTH, KV_RANK, IDX_HEADS * IDX_DIM, IDX_DIM, IDX_HEADS)
R_COLS = (R_WIDTH, R_WIDTH, R_WIDTH, DECAY_LORA, AAA_LORA, GATE_LORA)
A_PROJ = sum(A_COLS)
R_PROJ = sum(R_COLS)
P_IN = A_PROJ + R_PROJ

kernel_name = 'hybrid_dsa_rwkv7_macaron_adaln'


def _split(z, sizes):
    offs = np.cumsum(np.array(sizes))[:-1].tolist()
    return jnp.split(z, offs, axis=-1)


def rms_norm(x, g):
    xf = x.astype(jnp.float32)
    y = xf * lax.rsqrt(jnp.mean(xf * xf, axis=-1, keepdims=True) + RMS_EPS)
    return (y * g.astype(jnp.float32)).astype(x.dtype)


def adaln(h, g, shift, scale):
    return rms_norm(h, g) * (1 + scale[:, None, :]) + shift[:, None, :]


def swiglu(h, w_in, w_out):
    gu = h @ w_in
    g, u = jnp.split(gu, 2, axis=-1)
    return (jax.nn.silu(g) * u) @ w_out


def t5_bucket(dist):
    n = jnp.maximum(dist, 0)
    nf = jnp.maximum(n, 1).astype(jnp.float32)
    large = REL_MAX_EXACT + (jnp.log(nf / REL_MAX_EXACT) / math.log(REL_MAX_DIST / REL_MAX_EXACT)
                             * (REL_BUCKETS - REL_MAX_EXACT)).astype(jnp.int32)
    large = jnp.minimum(large, REL_BUCKETS - 1)
    return jnp.where(n < REL_MAX_EXACT, n, large)


def dsa_mixer(q, c_kv, q_idx, k_idx, w_idx, w_uk, w_uv, t5_bias):
    B, S = q.shape[0], q.shape[1]
    topk = min(TOPK_MAX, S // 4)
    nblk = S // Q_BLOCK
    scale = A_HEAD_DIM ** -0.5
    q_lat = jnp.einsum('bshd,rhd->bshr', q, w_uk)
    key_pos = jnp.arange(S, dtype=jnp.int32)

    def to_blocks(a):
        return jnp.moveaxis(a.reshape((B, nblk, Q_BLOCK) + a.shape[2:]), 1, 0)

    def block_fn(args):
        ql, qi, wi, start = args
        q_pos = start + jnp.arange(Q_BLOCK, dtype=jnp.int32)
        causal = key_pos[None, :] <= q_pos[:, None]
        rel = jax.nn.relu(jnp.einsum('bthd,bsd->bths', qi, k_idx))
        score = jnp.einsum('bths,bth->bts', rel, wi).astype(jnp.float32)
        score = jnp.where(causal[None], score, -jnp.inf)
        _, idx = lax.top_k(score, topk)
        valid = idx <= q_pos[None, :, None]
        c_sel = jax.vmap(lambda cb, ib: cb[ib])(c_kv, idx)
        logits = jnp.einsum('bthr,btkr->bhtk', ql, c_sel).astype(jnp.float32) * scale
        bias = t5_bias[t5_bucket(q_pos[None, :, None] - idx)]
        logits = logits + jnp.moveaxis(bias, -1, 1).astype(jnp.float32)
        logits = jnp.where(valid[:, None], logits, -jnp.inf)
        p = jax.nn.softmax(logits, axis=-1).astype(c_sel.dtype)
        o_lat = jnp.einsum('bhtk,btkr->bthr', p, c_sel)
        return jnp.einsum('bthr,rhd->bthd', o_lat, w_uv)

    starts = jnp.arange(nblk, dtype=jnp.int32) * Q_BLOCK
    out = lax.map(block_fn, (to_blocks(q_lat), to_blocks(q_idx), to_blocks(w_idx), starts))
    return jnp.moveaxis(out, 0, 1).reshape(B, S, A_WIDTH)


def rwkv7_mixer(zr, mu, w0, w2, a0, a2, g2, k_k, k_a, r_k, ln_w, ln_b):
    B, S = zr.shape[0], zr.shape[1]
    f32 = jnp.float32
    z_prev = jnp.pad(zr, ((0, 0), (1, 0), (0, 0)))[:, :-1]
    zm = zr + (z_prev - zr) * mu
    r, k, v, wl, al, gl = _split(zm, R_COLS)
    w = -jax.nn.softplus(-(w0 + jnp.tanh(wl) @ w2).astype(f32)) - 0.5
    decay = jnp.exp(-jnp.exp(w))
    a = jax.nn.sigmoid(a0 + al @ a2)
    g = jax.nn.sigmoid(gl) @ g2
    hd = lambda t: t.reshape(B, S, R_HEADS, R_HEAD_DIM).astype(f32)
    kk = hd(k * k_k)
    kk = kk / jnp.maximum(jnp.sqrt(jnp.sum(kk * kk, axis=-1, keepdims=True)), 1e-12)
    k = k * (1 + (a - 1) * k_a)
    r_h, k_h, v_h, a_h, w_h = hd(r), hd(k), hd(v), hd(a), hd(decay)

    def step(state, inp):
        r_t, w_t, k_t, v_t, kk_t, a_t = inp
        sa = jnp.einsum('bhij,bhj->bhi', state, -kk_t)
        state = (state * w_t[:, :, None, :] + sa[..., None] * (kk_t * a_t)[:, :, None, :]
                 + v_t[..., None] * k_t[:, :, None, :])
        return state, jnp.einsum('bhij,bhj->bhi', state, r_t)

    xs = tuple(jnp.moveaxis(t, 1, 0) for t in (r_h, w_h, k_h, v_h, kk, a_h))
    s0 = jnp.zeros((B, R_HEADS, R_HEAD_DIM, R_HEAD_DIM), f32)
    _, y = lax.scan(step, s0, xs)
    y = jnp.moveaxis(y, 0, 1)
    mean = jnp.mean(y, axis=-1, keepdims=True)
    var = jnp.mean((y - mean) ** 2, axis=-1, keepdims=True)
    y = ((y - mean) * lax.rsqrt(var + GN_EPS)).reshape(B, S, R_WIDTH)
    y = y * ln_w.astype(f32) + ln_b.astype(f32)
    bonus = jnp.sum(r_h * k_h * r_k.astype(f32), axis=-1, keepdims=True) * v_h
    y = (y + bonus.reshape(B, S, R_WIDTH)) * g.astype(f32)
    return y.astype(zr.dtype)


def setup_inputs(seed: int = 0) -> dict:
    key = jax.random.key(seed)
    ks = jax.random.split(key, 32)
    f32 = jnp.float32
    nrm = lambda k, shape, s: jax.random.normal(k, shape, f32) * s
    return {
        'x': nrm(ks[0], (BATCH, SEQ, D_MODEL), 1.0),
        'c': nrm(ks[1], (BATCH, D_MODEL), 1.0),
        't5_bias': nrm(ks[2], (REL_BUCKETS, A_HEADS), 0.5),
        'ada_w': nrm(ks[3], (DEPTH, D_MODEL, N_SUB * 3 * D_MODEL), 0.5 * D_MODEL ** -0.5),
        'ada_b': nrm(ks[4], (DEPTH, N_SUB * 3 * D_MODEL), 0.01),
        'norm_g': 1.0 + nrm(ks[5], (DEPTH, N_SUB, D_MODEL), 0.02),
        'ffn_w_in': nrm(ks[6], (DEPTH, 2, D_MODEL, 2 * D_FF), D_MODEL ** -0.5),
        'ffn_w_out': nrm(ks[7], (DEPTH, 2, D_FF, D_MODEL), D_FF ** -0.5),
        'w_in': nrm(ks[8], (DEPTH, D_MODEL, P_IN), D_MODEL ** -0.5),
        'ckv_norm_g': 1.0 + nrm(ks[9], (DEPTH, KV_RANK), 0.02),
        'w_uk': nrm(ks[10], (DEPTH, KV_RANK, A_HEADS, A_HEAD_DIM), KV_RANK ** -0.5),
        'w_uv': nrm(ks[11], (DEPTH, KV_RANK, A_HEADS, A_HEAD_DIM), KV_RANK ** -0.5),
        'rwkv_mu': jax.random.uniform(ks[12], (DEPTH, R_PROJ), f32, 0.0, 1.0),
        'rwkv_w0': nrm(ks[13], (DEPTH, R_WIDTH), 0.5),
        'rwkv_w2': nrm(ks[14], (DEPTH, DECAY_LORA, R_WIDTH), 0.5 * DECAY_LORA ** -0.5),
        'rwkv_a0': nrm(ks[15], (DEPTH, R_WIDTH), 0.5),
        'rwkv_a2': nrm(ks[16], (DEPTH, AAA_LORA, R_WIDTH), AAA_LORA ** -0.5),
        'rwkv_g2': nrm(ks[17], (DEPTH, GATE_LORA, R_WIDTH), GATE_LORA ** -0.5),
        'rwkv_k_k': 0.85 + nrm(ks[18], (DEPTH, R_WIDTH), 0.02),
        'rwkv_k_a': 1.0 + nrm(ks[19], (DEPTH, R_WIDTH), 0.02),
        'rwkv_r_k': nrm(ks[20], (DEPTH, R_HEADS, R_HEAD_DIM), 0.1),
        'rwkv_ln_w': 1.0 + nrm(ks[21], (DEPTH, R_WIDTH), 0.02),
        'rwkv_ln_b': nrm(ks[22], (DEPTH, R_WIDTH), 0.01),
        'w_out': nrm(ks[23], (DEPTH, D_MIX, D_MODEL), D_MIX ** -0.5),
        'final_norm_g': 1.0 + nrm(ks[24], (D_MODEL,), 0.02),
    }


def reference(x, c, t5_bias, ada_w, ada_b, norm_g, ffn_w_in, ffn_w_out, w_in, ckv_norm_g, w_uk, w_uv,
              rwkv_mu, rwkv_w0, rwkv_w2, rwkv_a0, rwkv_a2, rwkv_g2, rwkv_k_k, rwkv_k_a, rwkv_r_k,
              rwkv_ln_w, rwkv_ln_b, w_out, final_norm_g):
    B, S, D = x.shape
    c_act = jax.nn.silu(c)
    h = x
    for l in range(DEPTH):
        mod = (c_act @ ada_w[l] + ada_b[l]).reshape(B, N_SUB, 3, D)
        hn = adaln(h, norm_g[l, 0], mod[:, 0, 0], mod[:, 0, 1])
        h = h + 0.5 * mod[:, 0, 2, None, :] * swiglu(hn, ffn_w_in[l, 0], ffn_w_out[l, 0])
        hn = adaln(h, norm_g[l, 1], mod[:, 1, 0], mod[:, 1, 1])
        z = hn @ w_in[l]
        za, zr = z[..., :A_PROJ], z[..., A_PROJ:]
        q, ckv, qi, ki, wi = _split(za, A_COLS)
        ckv = rms_norm(ckv, ckv_norm_g[l])
        wi = wi * (IDX_HEADS * IDX_DIM) ** -0.5
        o_a = dsa_mixer(q.reshape(B, S, A_HEADS, A_HEAD_DIM), ckv,
                        qi.reshape(B, S, IDX_HEADS, IDX_DIM), ki, wi, w_uk[l], w_uv[l], t5_bias)
        o_r = rwkv7_mixer(zr, rwkv_mu[l], rwkv_w0[l], rwkv_w2[l], rwkv_a0[l], rwkv_a2[l], rwkv_g2[l],
                          rwkv_k_k[l], rwkv_k_a[l], rwkv_r_k[l], rwkv_ln_w[l], rwkv_ln_b[l])
        o = jnp.concatenate([o_a, o_r], axis=-1) @ w_out[l]
        h = h + mod[:, 1, 2, None, :] * o
        hn = adaln(h, norm_g[l, 2], mod[:, 2, 0], mod[:, 2, 1])
        h = h + 0.5 * mod[:, 2, 2, None, :] * swiglu(hn, ffn_w_in[l, 1], ffn_w_out[l, 1])
    return rms_norm(h, final_norm_g)
```

```python
import functools
import math

import jax
import jax.numpy as jnp
from jax import lax
from jax.experimental import pallas as pl
from jax.experimental.pallas import tpu as pltpu

F32 = jnp.float32
BF16 = jnp.bfloat16
I32 = jnp.int32

A_HEADS = 8
A_HEAD_DIM = 128
A_WIDTH = A_HEADS * A_HEAD_DIM
KV_RANK = 256
IDX_HEADS = 16
IDX_DIM = 64
TOPK_MAX = 256
REL_BUCKETS = 32
REL_MAX_EXACT = REL_BUCKETS // 2
REL_MAX_DIST = 128
R_HEAD_DIM = 64
R_WIDTH = 1024
R_HEADS = R_WIDTH // R_HEAD_DIM
DECAY_LORA = 96
AAA_LORA = 96
GATE_LORA = 256
GN_EPS = 64e-5
RMS_EPS = 1e-6
N_SUB = 3

LANES = 128
SUBLANES = 8
VMEM_LIMIT_BYTES = 56 * 1024 * 1024

LORA_PAD = 128
Z_Q = 0
Z_QI = Z_Q + A_WIDTH
Z_R = Z_QI + IDX_HEADS * IDX_DIM
Z_K = Z_R + R_WIDTH
Z_V = Z_K + R_WIDTH
Z_GL = Z_V + R_WIDTH
Z_CKV = Z_GL + GATE_LORA
Z_KIWI = Z_CKV + KV_RANK
Z_WL = Z_KIWI + LORA_PAD
Z_AL = Z_WL + LORA_PAD
Z_END = Z_AL + LORA_PAD
Z_WIDTH = 6144

DSA_TILE = 256
RWKV_CHUNK = 64
RWKV_TILE = 256
PAIR = 2 * R_HEAD_DIM

INT_MIN = -(2 ** 31)
KEY_NEG_INF = -2139095041


def _dot(a, b):
    return jnp.dot(a, b, preferred_element_type=F32)


def _dot_nt(a, b):
    return lax.dot_general(a, b, (((1,), (1,)), ((), ())), preferred_element_type=F32)


def _rms(x, g, eps):
    ms = jnp.mean(x * x, axis=-1, keepdims=True)
    return x * lax.rsqrt(ms + eps) * g


def _divisor_tile(n, pref):
    if n <= pref:
        return n
    t = (pref // LANES) * LANES
    while t > LANES and n % t:
        t -= LANES
    assert n % t == 0, (n, pref)
    return t


def _params(*sem):
    return pltpu.CompilerParams(dimension_semantics=sem, vmem_limit_bytes=VMEM_LIMIT_BYTES)


def _ada_kernel(c_ref, w_ref, b_ref, o_ref):
    c = c_ref[...]
    ca = (c * jax.nn.sigmoid(c)).astype(BF16)
    o_ref[0] = _dot(ca, w_ref[0].astype(BF16)) + b_ref[0]


def _ada_mod(c, ada_w, ada_b):
    depth, d, n = ada_w.shape
    b = c.shape[0]
    bp = -(-b // SUBLANES) * SUBLANES
    cp = jnp.pad(c, ((0, bp - b), (0, 0)))
    tn = _divisor_tile(n, 1024)
    out = pl.pallas_call(
        _ada_kernel,
        out_shape=jax.ShapeDtypeStruct((depth, bp, n), F32),
        grid=(depth, n // tn),
        in_specs=[
            pl.BlockSpec((bp, d), lambda l, j: (0, 0)),
            pl.BlockSpec((1, d, tn), lambda l, j: (l, 0, j)),
            pl.BlockSpec((1, 1, tn), lambda l, j: (l, 0, j)),
        ],
        out_specs=pl.BlockSpec((1, bp, tn), lambda l, j: (l, 0, j)),
        compiler_params=_params("arbitrary", "arbitrary"),
        name="ada_mod",
    )(cp, ada_w, ada_b.reshape(depth, 1, n))
    return out[:, :b]


def _ffn_kernel(h_ref, g_ref, shift_ref, scale_ref, gate_ref, wg_ref, wu_ref, wo_ref, fg_ref,
                o_ref, hn_ref, acc_ref, *, final_norm):
    f = pl.program_id(2)

    @pl.when(f == 0)
    def _():
        y = _rms(h_ref[0], g_ref[...], RMS_EPS)
        hn_ref[...] = (y * (1.0 + scale_ref[0]) + shift_ref[0]).astype(BF16)
        acc_ref[...] = jnp.zeros_like(acc_ref)

    hn = hn_ref[...]
    g = _dot(hn, wg_ref[...])
    u = _dot(hn, wu_ref[...])
    act = (g * jax.nn.sigmoid(g) * u).astype(BF16)
    acc_ref[...] += _dot(act, wo_ref[...])

    @pl.when(f == pl.num_programs(2) - 1)
    def _():
        out = h_ref[0] + 0.5 * gate_ref[0] * acc_ref[...]
        if final_norm:
            out = _rms(out, fg_ref[...], RMS_EPS)
        o_ref[0] = out


def _ffn(h, g, shift, scale, gate, w_in, w_out, final_g, *, final_norm):
    b, s, d = h.shape
    ff = w_out.shape[0]
    tm = _divisor_tile(s, 512)
    tf = _divisor_tile(ff, 512)
    nf = ff // tf
    vec = pl.BlockSpec((1, 1, d), lambda bi, i, f: (bi, 0, 0))
    row = pl.BlockSpec((1, d), lambda bi, i, f: (0, 0))
    tile = pl.BlockSpec((1, tm, d), lambda bi, i, f: (bi, i, 0))
    return pl.pallas_call(
        functools.partial(_ffn_kernel, final_norm=final_norm),
        out_shape=jax.ShapeDtypeStruct((b, s, d), F32),
        grid=(b, s // tm, nf),
        in_specs=[
            tile, row, vec, vec, vec,
            pl.BlockSpec((d, tf), lambda bi, i, f: (0, f)),
            pl.BlockSpec((d, tf), lambda bi, i, f: (0, nf + f)),
            pl.BlockSpec((tf, d), lambda bi, i, f: (f, 0)),
            row,
        ],
        out_specs=tile,
        scratch_shapes=[pltpu.VMEM((tm, d), BF16), pltpu.VMEM((tm, d), F32)],
        compiler_params=_params("parallel", "parallel", "arbitrary"),
        name="ffn",
    )(h, g, shift, scale, gate, w_in, w_in, w_out, final_g)


def _proj_kernel(h_ref, g_ref, shift_ref, scale_ref, w_ref, o_ref, hn_ref):
    @pl.when(pl.program_id(2) == 0)
    def _():
        y = _rms(h_ref[0], g_ref[...], RMS_EPS)
        hn_ref[...] = (y * (1.0 + scale_ref[0]) + shift_ref[0]).astype(BF16)

    o_ref[0] = _dot(hn_ref[...], w_ref[...])


def _proj(h, g, shift, scale, w):
    b, s, d = h.shape
    p = w.shape[1]
    tm = _divisor_tile(s, 512)
    tn = _divisor_tile(p, 1024)
    vec = pl.BlockSpec((1, 1, d), lambda bi, i, n: (bi, 0, 0))
    return pl.pallas_call(
        _proj_kernel,
        out_shape=jax.ShapeDtypeStruct((b, s, p), F32),
        grid=(b, s // tm, p // tn),
        in_specs=[
            pl.BlockSpec((1, tm, d), lambda bi, i, n: (bi, i, 0)),
            pl.BlockSpec((1, d), lambda bi, i, n: (0, 0)),
            vec, vec,
            pl.BlockSpec((d, tn), lambda bi, i, n: (0, n)),
        ],
        out_specs=pl.BlockSpec((1, tm, tn), lambda bi, i, n: (bi, i, n)),
        scratch_shapes=[pltpu.VMEM((tm, d), BF16)],
        compiler_params=_params("parallel", "parallel", "arbitrary"),
        name="proj",
    )(h, g, shift, scale, w)


def _outproj_kernel(oa_ref, or_ref, h_ref, gate_ref, wa_ref, wr_ref, o_ref):
    acc = _dot(oa_ref[0], wa_ref[...]) + _dot(or_ref[0], wr_ref[...])
    o_ref[0] = h_ref[0] + gate_ref[0] * acc


def _outproj(o_a, o_r, h, gate, w_out):
    b, s, d = h.shape
    tm = _divisor_tile(s, 512)
    tn = _divisor_tile(d, 1024)
    wa = o_a.shape[-1]
    wr = o_r.shape[-1]
    return pl.pallas_call(
        _outproj_kernel,
        out_shape=jax.ShapeDtypeStruct((b, s, d), F32),
        grid=(b, s // tm, d // tn),
        in_specs=[
            pl.BlockSpec((1, tm, wa), lambda bi, i, n: (bi, i, 0)),
            pl.BlockSpec((1, tm, wr), lambda bi, i, n: (bi, i, 0)),
            pl.BlockSpec((1, tm, tn), lambda bi, i, n: (bi, i, n)),
            pl.BlockSpec((1, 1, tn), lambda bi, i, n: (bi, 0, n)),
            pl.BlockSpec((wa, tn), lambda bi, i, n: (0, n)),
            pl.BlockSpec((wr, tn), lambda bi, i, n: (wa // wr, n)),
        ],
        out_specs=pl.BlockSpec((1, tm, tn), lambda bi, i, n: (bi, i, n)),
        compiler_params=_params("parallel", "parallel", "arbitrary"),
        name="outproj",
    )(o_a, o_r, h, gate, w_out, w_out)


def _dsa_prep_kernel(q_ref, qi_ref, ckv_ref, kiwi_ref, g_ref, qt_ref, qit_ref, ckv_o, ckvt_o, ki_o, wit_o):
    qt_ref[0] = q_ref[0].T.astype(BF16)
    qit_ref[0] = qi_ref[0].T.astype(BF16)
    cn = _rms(ckv_ref[0], g_ref[...], RMS_EPS)
    ckv_o[0] = cn.astype(BF16)
    ckvt_o[0, 0] = cn.T.astype(BF16)
    kw = kiwi_ref[0]
    ki_o[0] = kw[:, :IDX_DIM].astype(BF16)
    wit_o[0] = kw.T[IDX_DIM:IDX_DIM + IDX_HEADS, :] * (IDX_HEADS * IDX_DIM) ** -0.5


def _dsa_prep(z, ckv_g):
    b, s, _ = z.shape
    t = DSA_TILE
    nt = s // t

    def seg(width, off):
        return pl.BlockSpec((1, t, width), lambda bi, i: (bi, i, off // width))

    return pl.pallas_call(
        _dsa_prep_kernel,
        out_shape=(
            jax.ShapeDtypeStruct((b, A_WIDTH, s), BF16),
            jax.ShapeDtypeStruct((b, IDX_HEADS * IDX_DIM, s), BF16),
            jax.ShapeDtypeStruct((b, s, KV_RANK), BF16),
            jax.ShapeDtypeStruct((b, nt, KV_RANK, t), BF16),
            jax.ShapeDtypeStruct((b, s, IDX_DIM), BF16),
            jax.ShapeDtypeStruct((b, IDX_HEADS, s), F32),
        ),
        grid=(b, nt),
        in_specs=[
            seg(A_WIDTH, Z_Q), seg(IDX_HEADS * IDX_DIM, Z_QI), seg(KV_RANK, Z_CKV), seg(LORA_PAD, Z_KIWI),
            pl.BlockSpec((1, KV_RANK), lambda bi, i: (0, 0)),
        ],
        out_specs=(
            pl.BlockSpec((1, A_WIDTH, t), lambda bi, i: (bi, 0, i)),
            pl.BlockSpec((1, IDX_HEADS * IDX_DIM, t), lambda bi, i: (bi, 0, i)),
            pl.BlockSpec((1, t, KV_RANK), lambda bi, i: (bi, i, 0)),
            pl.BlockSpec((1, 1, KV_RANK, t), lambda bi, i: (bi, i, 0, 0)),
            pl.BlockSpec((1, t, IDX_DIM), lambda bi, i: (bi, i, 0)),
            pl.BlockSpec((1, IDX_HEADS, t), lambda bi, i: (bi, 0, i)),
        ),
        compiler_params=_params("parallel", "parallel"),
        name="dsa_prep",
    )(z, z, z, z, ckv_g)


def _bias_kernel(t5_ref, o_ref):
    t = DSA_TILE
    j = lax.broadcasted_iota(I32, (t, t), 0)
    i = lax.broadcasted_iota(I32, (t, t), 1)
    for didx in range(3):
        n = jnp.maximum(didx * t + i - j, 0)
        nf = jnp.maximum(n, 1).astype(F32)
        large = REL_MAX_EXACT + (jnp.log(nf / REL_MAX_EXACT) / math.log(REL_MAX_DIST / REL_MAX_EXACT)
                                 * (REL_BUCKETS - REL_MAX_EXACT)).astype(I32)
        large = jnp.minimum(large, REL_BUCKETS - 1)
        bucket = jnp.where(n < REL_MAX_EXACT, n, large)
        for h in range(A_HEADS):
            val = jnp.zeros((t, t), F32)
            for k in range(REL_BUCKETS):
                val = jnp.where(bucket == k, t5_ref[k, h], val)
            o_ref[didx, h] = val


def _bias_tiles(t5_bias):
    t = DSA_TILE
    return pl.pallas_call(
        _bias_kernel,
        out_shape=jax.ShapeDtypeStruct((3, A_HEADS, t, t), F32),
        in_specs=[pl.BlockSpec(memory_space=pltpu.SMEM)],
        out_specs=pl.BlockSpec(memory_space=pltpu.VMEM),
        compiler_params=pltpu.CompilerParams(vmem_limit_bytes=VMEM_LIMIT_BYTES),
        name="t5_bias_tiles",
    )(t5_bias)


def _dsa_kernel(qt_ref, qit_ref, ckv_ref, ckvt_ref, ki_ref, wit_ref, wuk_ref, wuvt_ref, bias_ref,
                o_ref, keys_ref, lg_ref, ot_ref, *, topk):
    t = DSA_TILE
    qb = pl.program_id(1)
    nk = qb + 1
    row = lax.broadcasted_iota(I32, (t, t), 0)
    col = lax.broadcasted_iota(I32, (t, t), 1)

    def idx_body(kc, carry):
        kic = ki_ref[0, pl.ds(pl.multiple_of(kc * t, t), t), :]
        acc = jnp.zeros((t, t), F32)
        for h in range(IDX_HEADS):
            rel = _dot(kic, qit_ref[0, h * IDX_DIM:(h + 1) * IDX_DIM, :])
            acc = acc + jnp.maximum(rel, 0.0) * wit_ref[0, h:h + 1, :]
        acc = acc + 0.0
        bits = pltpu.bitcast(acc, I32)
        key = bits ^ ((bits >> 31) & 0x7FFFFFFF)
        causal = (kc * t + row) <= (qb * t + col)
        keys_ref[pl.ds(pl.multiple_of(kc * t, t), t), :] = jnp.where(causal, key, KEY_NEG_INF)
        return carry

    lax.fori_loop(0, nk, idx_body, 0)

    def count_ge(cand):
        def body(kc, acc):
            k = keys_ref[pl.ds(pl.multiple_of(kc * t, t), t), :]
            m = jnp.where(k >= cand, 1, 0).astype(I32)
            return acc + jnp.sum(m.reshape(t // SUBLANES, SUBLANES, t), axis=0)

        acc = lax.fori_loop(0, nk, body, jnp.zeros((SUBLANES, t), I32))
        return jnp.sum(acc, axis=0, keepdims=True)

    thr = jnp.where(count_ge(jnp.zeros((1, t), I32)) >= topk, 0, INT_MIN).astype(I32)

    def bit_body(i, thr):
        cand = thr | (jnp.int32(1) << (30 - i))
        return jnp.where(count_ge(cand) >= topk, cand, thr)

    thr = lax.fori_loop(0, 31, bit_body, thr)
    thr = jnp.maximum(thr, KEY_NEG_INF + 1)

    scale = A_HEAD_DIM ** -0.5

    def head_body(h, carry):
        qh = qt_ref[0, pl.ds(pl.multiple_of(h * A_HEAD_DIM, A_HEAD_DIM), A_HEAD_DIM), :]
        qlat = (_dot(wuk_ref[h], qh) * scale).astype(BF16)

        def p1(kc, m8):
            sl = pl.ds(pl.multiple_of(kc * t, t), t)
            lg = _dot(ckv_ref[0, sl, :], qlat) + bias_ref[jnp.minimum(qb - kc, 2), h]
            lg = jnp.where(keys_ref[sl, :] >= thr, lg, -jnp.inf)
            lg_ref[sl, :] = lg
            return jnp.maximum(m8, jnp.max(lg.reshape(t // SUBLANES, SUBLANES, t), axis=0))

        m8 = lax.fori_loop(0, nk, p1, jnp.full((SUBLANES, t), -jnp.inf, F32))
        m = jnp.max(m8, axis=0, keepdims=True)

        def p2(kc, carry2):
            l8, acc = carry2
            sl = pl.ds(pl.multiple_of(kc * t, t), t)
            p = jnp.exp(lg_ref[sl, :] - m)
            l8 = l8 + jnp.sum(p.reshape(t // SUBLANES, SUBLANES, t), axis=0)
            acc = acc + _dot(ckvt_ref[0, kc], p.astype(BF16))
            return l8, acc

        l8, acc = lax.fori_loop(0, nk, p2, (jnp.zeros((SUBLANES, t), F32), jnp.zeros((KV_RANK, t), F32)))
        olat = (acc / jnp.sum(l8, axis=0, keepdims=True)).astype(BF16)
        ot_ref[pl.ds(pl.multiple_of(h * A_HEAD_DIM, A_HEAD_DIM), A_HEAD_DIM), :] = _dot(wuvt_ref[h], olat)
        return carry

    lax.fori_loop(0, A_HEADS, head_body, 0)
    o_ref[0] = ot_ref[...].T.astype(BF16)


def _dsa(qt, qit, ckv, ckvt, ki, wit, wuk, wuvt, bias):
    b, _, s = qt.shape
    t = DSA_TILE
    nt = s // t
    topk = min(TOPK_MAX, s // 4)
    return pl.pallas_call(
        functools.partial(_dsa_kernel, topk=topk),
        out_shape=jax.ShapeDtypeStruct((b, s, A_WIDTH), BF16),
        grid=(b, nt),
        in_specs=[
            pl.BlockSpec((1, A_WIDTH, t), lambda bi, i: (bi, 0, i)),
            pl.BlockSpec((1, IDX_HEADS * IDX_DIM, t), lambda bi, i: (bi, 0, i)),
            pl.BlockSpec((1, s, KV_RANK), lambda bi, i: (bi, 0, 0)),
            pl.BlockSpec((1, nt, KV_RANK, t), lambda bi, i: (bi, 0, 0, 0)),
            pl.BlockSpec((1, s, IDX_DIM), lambda bi, i: (bi, 0, 0)),
            pl.BlockSpec((1, IDX_HEADS, t), lambda bi, i: (bi, 0, i)),
            pl.BlockSpec((A_HEADS, KV_RANK, A_HEAD_DIM), lambda bi, i: (0, 0, 0)),
            pl.BlockSpec((A_HEADS, A_HEAD_DIM, KV_RANK), lambda bi, i: (0, 0, 0)),
            pl.BlockSpec((3, A_HEADS, t, t), lambda bi, i: (0, 0, 0, 0)),
        ],
        out_specs=pl.BlockSpec((1, t, A_WIDTH), lambda bi, i: (bi, i, 0)),
        scratch_shapes=[pltpu.VMEM((s, t), I32), pltpu.VMEM((s, t), F32), pltpu.VMEM((A_WIDTH, t), F32)],
        compiler_params=_params("parallel", "arbitrary"),
        name="dsa",
    )(qt, qit, ckv, ckvt, ki, wit, wuk, wuvt, bias)


def _split3(x):
    hi = x.astype(BF16)
    r1 = x - hi.astype(F32)
    mid = r1.astype(BF16)
    lo = (r1 - mid.astype(F32)).astype(BF16)
    return hi, mid, lo


def _rwkv_kernel(r_ref, k_ref, v_ref, gl_ref, wl_ref, al_ref,
                 mur_ref, muk_ref, muv_ref, mug_ref, muw_ref, mua_ref,
                 w0_ref, a0_ref, kk_ref, ka_ref, rk_ref, lnw_ref, lnb_ref,
                 w2_ref, a2_ref, g2_ref,
                 o_ref,
                 pr_ref, pk_ref, pv_ref, pg_ref, pw_ref, pa_ref, state_ref,
                 abar_ref, rbar_ref, bt_ref, kt_ref, bh_ref, kh_ref, vv_ref, pc_ref, rkv_ref, gg_ref):
    tl = RWKV_TILE
    c = RWKV_CHUNK
    nc = tl // c
    npair = R_WIDTH // PAIR

    @pl.when(pl.program_id(1) == 0)
    def _():
        state_ref[...] = jnp.zeros_like(state_ref)
        for ref in (pr_ref, pk_ref, pv_ref, pg_ref, pw_ref, pa_ref):
            ref[...] = jnp.zeros_like(ref)

    def shift_mix(x_ref, prev_ref, mu_ref):
        x = x_ref[0]
        first = lax.broadcasted_iota(I32, x.shape, 0) == 0
        xprev = jnp.where(first, prev_ref[...], pltpu.roll(x, 1, axis=0))
        prev_ref[...] = x[tl - 1:tl, :]
        return x + (xprev - x) * mu_ref[...]

    r = shift_mix(r_ref, pr_ref, mur_ref)
    k = shift_mix(k_ref, pk_ref, muk_ref)
    v = shift_mix(v_ref, pv_ref, muv_ref)
    gl = shift_mix(gl_ref, pg_ref, mug_ref)
    wl = shift_mix(wl_ref, pw_ref, muw_ref)
    al = shift_mix(al_ref, pa_ref, mua_ref)

    w_lin = w0_ref[...] + _dot(jnp.tanh(wl).astype(BF16), w2_ref[...])
    nx = -w_lin
    softplus = jnp.maximum(nx, 0.0) + jnp.log(1.0 + jnp.exp(-jnp.abs(nx)))
    ld = -jnp.exp(-softplus - 0.5)
    a = jax.nn.sigmoid(a0_ref[...] + _dot(al.astype(BF16), a2_ref[...]))
    gg_ref[...] = _dot(jax.nn.sigmoid(gl).astype(BF16), g2_ref[...])

    li = lax.broadcasted_iota(I32, (LANES, LANES), 0) // R_HEAD_DIM
    lj = lax.broadcasted_iota(I32, (LANES, LANES), 1) // R_HEAD_DIM
    head_ones = jnp.where(li == lj, 1.0, 0.0).astype(BF16)

    def head_sum(x):
        parts = []
        for j in range(R_WIDTH // LANES):
            hi, mid, lo = _split3(x[:, j * LANES:(j + 1) * LANES])
            parts.append(_dot(hi, head_ones) + _dot(mid, head_ones) + _dot(lo, head_ones))
        return jnp.concatenate(parts, axis=-1)

    kk = k * kk_ref[...]
    kk = kk / jnp.maximum(jnp.sqrt(head_sum(kk * kk)), 1e-12)
    k2 = k * (1.0 + (a - 1.0) * ka_ref[...])
    bb = kk * a

    ti = lax.broadcasted_iota(I32, (tl, tl), 0)
    tj = lax.broadcasted_iota(I32, (tl, tl), 1)
    tri = jnp.where((ti // c == tj // c) & (tj <= ti), 1.0, 0.0).astype(BF16)
    hi, mid, lo = _split3(ld)
    cum = _dot(tri, hi) + _dot(tri, mid) + _dot(tri, lo)
    cum_end = jnp.broadcast_to(cum.reshape(nc, c, R_WIDTH)[:, c - 1:c, :], (nc, c, R_WIDTH)).reshape(tl, R_WIDTH)
    e_in = jnp.exp(cum)
    e_neg = jnp.exp(-cum)
    abar_ref[...] = -kk * jnp.exp(cum - ld)
    rbar_ref[...] = r * e_in
    bt_ref[...] = bb * e_neg
    kt_ref[...] = k2 * e_neg
    e_out = jnp.exp(cum_end - cum)
    bh_ref[...] = bb * e_out
    kh_ref[...] = k2 * e_out
    vv_ref[...] = v
    pc_ref[...] = jnp.exp(cum_end)
    rkv_ref[...] = r * k2 * rk_ref[...]

    lane_head = lax.broadcasted_iota(I32, (2 * c, PAIR), 1) // R_HEAD_DIM
    row_head = lax.broadcasted_iota(I32, (2 * c, PAIR), 0) // c
    same = lane_head == row_head
    rt = lax.broadcasted_iota(I32, (2 * c, PAIR), 0) % c
    ct = lax.broadcasted_iota(I32, (2 * c, PAIR), 1) % c
    strict = same & (ct < rt)
    incl = same & (ct <= rt)
    eye = jnp.where(same & (ct == rt), 1.0, 0.0).astype(F32)

    def bd(x):
        return jnp.where(same, jnp.concatenate([x, x], axis=0), 0.0)

    def chunk_body(ci, carry):
        rows = pl.ds(pl.multiple_of(ci * c, c), c)
        for p in range(npair):
            lanes = slice(p * PAIR, (p + 1) * PAIR)
            abar = bd(abar_ref[rows, lanes]).astype(BF16)
            rbar = bd(rbar_ref[rows, lanes]).astype(BF16)
            bt = bd(bt_ref[rows, lanes]).astype(BF16)
            kt = bd(kt_ref[rows, lanes]).astype(BF16)
            v_pl = vv_ref[rows, lanes]
            vbd = bd(v_pl)
            vbd16 = vbd.astype(BF16)
            aa = _dot_nt(jnp.concatenate([abar, rbar], axis=0), jnp.concatenate([bt, kt], axis=0))
            a_ab = jnp.where(strict, aa[:2 * c, :PAIR], 0.0)
            a_ak = jnp.where(strict, aa[:2 * c, PAIR:], 0.0).astype(BF16)
            a_rb = jnp.where(incl, aa[2 * c:, :PAIR], 0.0).astype(BF16)
            a_rk = jnp.where(incl, aa[2 * c:, PAIR:], 0.0).astype(BF16)
            pw = a_ab.astype(BF16)
            tinv = eye + a_ab
            for _ in range(int(math.log2(c)) - 1):
                pw32 = _dot(pw, pw)
                pw = pw32.astype(BF16)
                tinv = tinv + _dot(tinv.astype(BF16), pw)
            akv = _dot(a_ak, vbd16)
            wu = _dot(tinv.astype(BF16), jnp.concatenate([abar, akv.astype(BF16)], axis=1))
            w_bd = wu[:, :PAIR]
            u_bd = wu[:, PAIR:]
            s_bd = state_ref[p]
            wr = _dot_nt(jnp.concatenate([w_bd.astype(BF16), rbar], axis=0), s_bd.astype(BF16))
            e_bd = wr[:2 * c] + u_bd
            ev = jnp.concatenate([e_bd, vbd], axis=0)
            y_bd = wr[2 * c:] + _dot(jnp.concatenate([a_rb, a_rk], axis=1), ev.astype(BF16))
            bk = jnp.concatenate([bd(bh_ref[rows, lanes]), bd(kh_ref[rows, lanes])], axis=0).astype(BF16)
            ds = _dot(ev.T.astype(BF16), bk)
            pc = pc_ref[rows, lanes][0:1, :]
            state_ref[p] = s_bd * pc + jnp.where(same, ds, 0.0)
            mean = jnp.sum(y_bd, axis=-1, keepdims=True) * (1.0 / R_HEAD_DIM)
            dev = jnp.where(same, y_bd - mean, 0.0)
            var = jnp.sum(dev * dev, axis=-1, keepdims=True) * (1.0 / R_HEAD_DIM)
            yn = dev * lax.rsqrt(var + GN_EPS)
            yn = yn[:c] + yn[c:]
            rkv = rkv_ref[rows, lanes]
            lane0 = lax.broadcasted_iota(I32, (c, PAIR), 1) < R_HEAD_DIM
            s0 = jnp.sum(jnp.where(lane0, rkv, 0.0), axis=-1, keepdims=True)
            s1 = jnp.sum(jnp.where(lane0, 0.0, rkv), axis=-1, keepdims=True)
            bonus = jnp.where(lane0, s0, s1) * v_pl
            out = (yn * lnw_ref[:, lanes] + lnb_ref[:, lanes] + bonus) * gg_ref[rows, lanes]
            o_ref[0, rows, lanes] = out.astype(BF16)
        return carry

    lax.fori_loop(0, nc, chunk_body, 0)


def _rwkv(z, prm):
    b, s, _ = z.shape
    tl = RWKV_TILE

    def seg(width, off):
        return pl.BlockSpec((1, tl, width), lambda bi, i: (bi, i, off // width))

    def full(arr):
        return pl.BlockSpec(arr.shape, lambda bi, i: (0,) * arr.ndim)

    names = ("mu_r", "mu_k", "mu_v", "mu_g", "mu_w", "mu_a", "w0", "a0", "k_k", "k_a", "r_k", "ln_w", "ln_b",
             "w2", "a2", "g2")
    consts = [prm[n] for n in names]
    wide = pltpu.VMEM((tl, R_WIDTH), F32)
    return pl.pallas_call(
        _rwkv_kernel,
        out_shape=jax.ShapeDtypeStruct((b, s, R_WIDTH), BF16),
        grid=(b, s // tl),
        in_specs=[seg(R_WIDTH, Z_R), seg(R_WIDTH, Z_K), seg(R_WIDTH, Z_V), seg(GATE_LORA, Z_GL),
                  seg(LORA_PAD, Z_WL), seg(LORA_PAD, Z_AL)] + [full(x) for x in consts],
        out_specs=pl.BlockSpec((1, tl, R_WIDTH), lambda bi, i: (bi, i, 0)),
        scratch_shapes=[
            pltpu.VMEM((1, R_WIDTH), F32), pltpu.VMEM((1, R_WIDTH), F32), pltpu.VMEM((1, R_WIDTH), F32),
            pltpu.VMEM((1, GATE_LORA), F32), pltpu.VMEM((1, LORA_PAD), F32), pltpu.VMEM((1, LORA_PAD), F32),
            pltpu.VMEM((R_WIDTH // PAIR, PAIR, PAIR), F32),
            wide, wide, wide, wide, wide, wide, wide, wide, wide, wide,
        ],
        compiler_params=_params("parallel", "arbitrary"),
        name="rwkv7",
    )(z, z, z, z, z, z, *consts)


def _pad_cols(w, width):
    return jnp.pad(w, ((0, 0), (0, width - w.shape[1])))


def _layout_w_in(w):
    offs = [0]
    for n in (A_WIDTH, KV_RANK, IDX_HEADS * IDX_DIM, IDX_DIM, IDX_HEADS,
              R_WIDTH, R_WIDTH, R_WIDTH, DECAY_LORA, AAA_LORA, GATE_LORA):
        offs.append(offs[-1] + n)
    q, ckv, qi, ki, wi, r, k, v, wl, al, gl = (w[:, offs[i]:offs[i + 1]] for i in range(11))
    cols = [q, qi, r, k, v, gl, ckv, _pad_cols(jnp.concatenate([ki, wi], axis=1), LORA_PAD),
            _pad_cols(wl, LORA_PAD), _pad_cols(al, LORA_PAD)]
    return _pad_cols(jnp.concatenate(cols, axis=1), Z_WIDTH).astype(BF16)


def _pad_rows(w, rows):
    return jnp.pad(w, ((0, rows - w.shape[0]), (0, 0)))


def _rwkv_params(l, mu, w0, w2, a0, a2, g2, k_k, k_a, r_k, ln_w, ln_b):
    m = mu[l]
    o = [0, R_WIDTH, 2 * R_WIDTH, 3 * R_WIDTH, 3 * R_WIDTH + DECAY_LORA, 3 * R_WIDTH + DECAY_LORA + AAA_LORA]
    row = lambda x: x.reshape(1, -1)
    return {
        "mu_r": row(m[o[0]:o[1]]), "mu_k": row(m[o[1]:o[2]]), "mu_v": row(m[o[2]:o[3]]),
        "mu_w": _pad_cols(row(m[o[3]:o[4]]), LORA_PAD), "mu_a": _pad_cols(row(m[o[4]:o[5]]), LORA_PAD),
        "mu_g": row(m[o[5]:]),
        "w0": row(w0[l]), "a0": row(a0[l]), "k_k": row(k_k[l]), "k_a": row(k_a[l]), "r_k": row(r_k[l]),
        "ln_w": row(ln_w[l]), "ln_b": row(ln_b[l]),
        "w2": _pad_rows(w2[l], LORA_PAD).astype(BF16), "a2": _pad_rows(a2[l], LORA_PAD).astype(BF16),
        "g2": g2[l].astype(BF16),
    }


def kernel(x, c, t5_bias, ada_w, ada_b, norm_g, ffn_w_in, ffn_w_out, w_in, ckv_norm_g, w_uk, w_uv, rwkv_mu, rwkv_w0, rwkv_w2, rwkv_a0, rwkv_a2, rwkv_g2, rwkv_k_k, rwkv_k_a, rwkv_r_k, rwkv_ln_w, rwkv_ln_b, w_out, final_norm_g):
    b, s, d = x.shape
    depth = ada_w.shape[0]
    assert s % DSA_TILE == 0 and s % RWKV_TILE == 0 and d == A_WIDTH + R_WIDTH
    mod = _ada_mod(c, ada_w, ada_b).reshape(depth, b, N_SUB, 3, 1, d)
    bias = _bias_tiles(t5_bias)
    final_g = final_norm_g.reshape(1, d)
    h = x
    for l in range(depth):
        shift = lambda i: mod[l, :, i, 0]
        scale = lambda i: mod[l, :, i, 1]
        gate = lambda i: mod[l, :, i, 2]
        g = lambda i: norm_g[l, i].reshape(1, d)
        h = _ffn(h, g(0), shift(0), scale(0), gate(0), ffn_w_in[l, 0].astype(BF16), ffn_w_out[l, 0].astype(BF16),
                 final_g, final_norm=False)
        z = _proj(h, g(1), shift(1), scale(1), _layout_w_in(w_in[l]))
        qt, qit, ckv, ckvt, ki, wit = _dsa_prep(z, ckv_norm_g[l].reshape(1, KV_RANK))
        wuk = jnp.transpose(w_uk[l], (1, 0, 2)).astype(BF16)
        wuvt = jnp.transpose(w_uv[l], (1, 2, 0)).astype(BF16)
        o_a = _dsa(qt, qit, ckv, ckvt, ki, wit, wuk, wuvt, bias)
        o_r = _rwkv(z, _rwkv_params(l, rwkv_mu, rwkv_w0, rwkv_w2, rwkv_a0, rwkv_a2, rwkv_g2, rwkv_k_k,
                                    rwkv_k_a, rwkv_r_k, rwkv_ln_w, rwkv_ln_b))
        h = _outproj(o_a, o_r, h, gate(1), w_out[l].astype(BF16))
        h = _ffn(h, g(2), shift(2), scale(2), gate(2), ffn_w_in[l, 1].astype(BF16), ffn_w_out[l, 1].astype(BF16),
                 final_g, final_norm=(l == depth - 1))
    return h
```

```python
import functools
import math

import jax
import jax.numpy as jnp
from jax import lax
from jax.experimental import pallas as pl
from jax.experimental.pallas import tpu as pltpu

F32 = jnp.float32
BF16 = jnp.bfloat16
I32 = jnp.int32

A_HEADS = 8
A_HEAD_DIM = 128
A_WIDTH = A_HEADS * A_HEAD_DIM
KV_RANK = 256
IDX_HEADS = 16
IDX_DIM = 64
TOPK_MAX = 256
REL_BUCKETS = 32
REL_MAX_EXACT = REL_BUCKETS // 2
REL_MAX_DIST = 128
R_HEAD_DIM = 64
R_WIDTH = 1024
R_HEADS = R_WIDTH // R_HEAD_DIM
DECAY_LORA = 96
AAA_LORA = 96
GATE_LORA = 256
GN_EPS = 64e-5
RMS_EPS = 1e-6
N_SUB = 3

LANES = 128
SUBLANES = 8
VMEM_LIMIT_BYTES = 56 * 1024 * 1024

LORA_PAD = 128
Z_Q = 0
Z_QI = Z_Q + A_WIDTH
Z_R = Z_QI + IDX_HEADS * IDX_DIM
Z_K = Z_R + R_WIDTH
Z_V = Z_K + R_WIDTH
Z_GL = Z_V + R_WIDTH
Z_CKV = Z_GL + GATE_LORA
Z_KIWI = Z_CKV + KV_RANK
Z_WL = Z_KIWI + LORA_PAD
Z_AL = Z_WL + LORA_PAD
Z_END = Z_AL + LORA_PAD
Z_WIDTH = 6144

DSA_TILE = 256
RWKV_CHUNK = 64
RWKV_TILE = 256
PAIR = 2 * R_HEAD_DIM

INT_MIN = -(2 ** 31)
KEY_NEG_INF = -2139095041
MASKED_LOGIT = -1e30


def _dot(a, b):
    return jnp.dot(a, b, preferred_element_type=F32)


def _dot_nt(a, b):
    return lax.dot_general(a, b, (((1,), (1,)), ((), ())), preferred_element_type=F32)


def _bdot(a, b):
    return lax.dot_general(a, b, (((2,), (1,)), ((0,), (0,))), preferred_element_type=F32)


def _bdot_nt(a, b):
    return lax.dot_general(a, b, (((2,), (2,)), ((0,), (0,))), preferred_element_type=F32)


def _rms(x, g, eps):
    ms = jnp.mean(x * x, axis=-1, keepdims=True)
    return x * lax.rsqrt(ms + eps) * g


def _divisor_tile(n, pref):
    if n <= pref:
        return n
    t = (pref // LANES) * LANES
    while t > LANES and n % t:
        t -= LANES
    assert n % t == 0, (n, pref)
    return t


def _params(*sem):
    return pltpu.CompilerParams(dimension_semantics=sem, vmem_limit_bytes=VMEM_LIMIT_BYTES)


def _ada_kernel(c_ref, w_ref, b_ref, o_ref):
    c = c_ref[...]
    ca = (c * jax.nn.sigmoid(c)).astype(BF16)
    o_ref[0] = _dot(ca, w_ref[0].astype(BF16)) + b_ref[0]


def _ada_mod(c, ada_w, ada_b):
    depth, d, n = ada_w.shape
    b = c.shape[0]
    bp = -(-b // SUBLANES) * SUBLANES
    cp = jnp.pad(c, ((0, bp - b), (0, 0)))
    tn = _divisor_tile(n, 1024)
    out = pl.pallas_call(
        _ada_kernel,
        out_shape=jax.ShapeDtypeStruct((depth, bp, n), F32),
        grid=(depth, n // tn),
        in_specs=[
            pl.BlockSpec((bp, d), lambda l, j: (0, 0)),
            pl.BlockSpec((1, d, tn), lambda l, j: (l, 0, j)),
            pl.BlockSpec((1, 1, tn), lambda l, j: (l, 0, j)),
        ],
        out_specs=pl.BlockSpec((1, bp, tn), lambda l, j: (l, 0, j)),
        compiler_params=_params("arbitrary", "arbitrary"),
        name="ada_mod",
    )(cp, ada_w, ada_b.reshape(depth, 1, n))
    return out[:, :b]


def _ffn_kernel(h_ref, g_ref, shift_ref, scale_ref, gate_ref, wg_ref, wu_ref, wo_ref, fg_ref,
                o_ref, hn_ref, acc_ref, *, final_norm):
    f = pl.program_id(2)

    @pl.when(f == 0)
    def _():
        y = _rms(h_ref[0], g_ref[...], RMS_EPS)
        hn_ref[...] = (y * (1.0 + scale_ref[0]) + shift_ref[0]).astype(BF16)
        acc_ref[...] = jnp.zeros_like(acc_ref)

    hn = hn_ref[...]
    g = _dot(hn, wg_ref[...])
    u = _dot(hn, wu_ref[...])
    act = (g * jax.nn.sigmoid(g) * u).astype(BF16)
    acc_ref[...] += _dot(act, wo_ref[...])

    @pl.when(f == pl.num_programs(2) - 1)
    def _():
        out = h_ref[0] + 0.5 * gate_ref[0] * acc_ref[...]
        if final_norm:
            out = _rms(out, fg_ref[...], RMS_EPS)
        o_ref[0] = out


def _ffn(h, g, shift, scale, gate, w_in, w_out, final_g, *, final_norm):
    b, s, d = h.shape
    ff = w_out.shape[0]
    tm = _divisor_tile(s, 512)
    tf = _divisor_tile(ff, 512)
    nf = ff // tf
    vec = pl.BlockSpec((1, 1, d), lambda bi, i, f: (bi, 0, 0))
    row = pl.BlockSpec((1, d), lambda bi, i, f: (0, 0))
    tile = pl.BlockSpec((1, tm, d), lambda bi, i, f: (bi, i, 0))
    return pl.pallas_call(
        functools.partial(_ffn_kernel, final_norm=final_norm),
        out_shape=jax.ShapeDtypeStruct((b, s, d), F32),
        grid=(b, s // tm, nf),
        in_specs=[
            tile, row, vec, vec, vec,
            pl.BlockSpec((d, tf), lambda bi, i, f: (0, f)),
            pl.BlockSpec((d, tf), lambda bi, i, f: (0, nf + f)),
            pl.BlockSpec((tf, d), lambda bi, i, f: (f, 0)),
            row,
        ],
        out_specs=tile,
        scratch_shapes=[pltpu.VMEM((tm, d), BF16), pltpu.VMEM((tm, d), F32)],
        compiler_params=_params("parallel", "parallel", "arbitrary"),
        name="ffn",
    )(h, g, shift, scale, gate, w_in, w_in, w_out, final_g)


def _proj_kernel(h_ref, g_ref, shift_ref, scale_ref, w_ref, o_ref, hn_ref):
    @pl.when(pl.program_id(2) == 0)
    def _():
        y = _rms(h_ref[0], g_ref[...], RMS_EPS)
        hn_ref[...] = (y * (1.0 + scale_ref[0]) + shift_ref[0]).astype(BF16)

    o_ref[0] = _dot(hn_ref[...], w_ref[...])


def _proj(h, g, shift, scale, w):
    b, s, d = h.shape
    p = w.shape[1]
    tm = _divisor_tile(s, 512)
    tn = _divisor_tile(p, 1024)
    vec = pl.BlockSpec((1, 1, d), lambda bi, i, n: (bi, 0, 0))
    return pl.pallas_call(
        _proj_kernel,
        out_shape=jax.ShapeDtypeStruct((b, s, p), F32),
        grid=(b, s // tm, p // tn),
        in_specs=[
            pl.BlockSpec((1, tm, d), lambda bi, i, n: (bi, i, 0)),
            pl.BlockSpec((1, d), lambda bi, i, n: (0, 0)),
            vec, vec,
            pl.BlockSpec((d, tn), lambda bi, i, n: (0, n)),
        ],
        out_specs=pl.BlockSpec((1, tm, tn), lambda bi, i, n: (bi, i, n)),
        scratch_shapes=[pltpu.VMEM((tm, d), BF16)],
        compiler_params=_params("parallel", "parallel", "arbitrary"),
        name="proj",
    )(h, g, shift, scale, w)


def _outproj_kernel(oa_ref, or_ref, h_ref, gate_ref, wa_ref, wr_ref, o_ref):
    acc = _dot(oa_ref[0], wa_ref[...]) + _dot(or_ref[0], wr_ref[...])
    o_ref[0] = h_ref[0] + gate_ref[0] * acc


def _outproj(o_a, o_r, h, gate, w_out):
    b, s, d = h.shape
    tm = _divisor_tile(s, 512)
    tn = _divisor_tile(d, 1024)
    wa = o_a.shape[-1]
    wr = o_r.shape[-1]
    return pl.pallas_call(
        _outproj_kernel,
        out_shape=jax.ShapeDtypeStruct((b, s, d), F32),
        grid=(b, s // tm, d // tn),
        in_specs=[
            pl.BlockSpec((1, tm, wa), lambda bi, i, n: (bi, i, 0)),
            pl.BlockSpec((1, tm, wr), lambda bi, i, n: (bi, i, 0)),
            pl.BlockSpec((1, tm, tn), lambda bi, i, n: (bi, i, n)),
            pl.BlockSpec((1, 1, tn), lambda bi, i, n: (bi, 0, n)),
            pl.BlockSpec((wa, tn), lambda bi, i, n: (0, n)),
            pl.BlockSpec((wr, tn), lambda bi, i, n: (wa // wr, n)),
        ],
        out_specs=pl.BlockSpec((1, tm, tn), lambda bi, i, n: (bi, i, n)),
        compiler_params=_params("parallel", "parallel", "arbitrary"),
        name="outproj",
    )(o_a, o_r, h, gate, w_out, w_out)


def _dsa_prep_kernel(q_ref, qi_ref, ckv_ref, kiwi_ref, g_ref, qt_ref, qit_ref, ckv_o, ckvt_o, ki_o, wit_o):
    qt_ref[0] = q_ref[0].T.astype(BF16)
    qit_ref[0] = qi_ref[0].T.astype(BF16)
    cn = _rms(ckv_ref[0], g_ref[...], RMS_EPS)
    ckv_o[0] = cn.astype(BF16)
    ckvt_o[0, 0] = cn.T.astype(BF16)
    kw = kiwi_ref[0]
    ki_o[0] = kw[:, :IDX_DIM].astype(BF16)
    wit_o[0] = kw.T[IDX_DIM:IDX_DIM + IDX_HEADS, :] * (IDX_HEADS * IDX_DIM) ** -0.5


def _dsa_prep(z, ckv_g):
    b, s, _ = z.shape
    t = DSA_TILE
    nt = s // t

    def seg(width, off):
        return pl.BlockSpec((1, t, width), lambda bi, i: (bi, i, off // width))

    return pl.pallas_call(
        _dsa_prep_kernel,
        out_shape=(
            jax.ShapeDtypeStruct((b, A_WIDTH, s), BF16),
            jax.ShapeDtypeStruct((b, IDX_HEADS * IDX_DIM, s), BF16),
            jax.ShapeDtypeStruct((b, s, KV_RANK), BF16),
            jax.ShapeDtypeStruct((b, nt, KV_RANK, t), BF16),
            jax.ShapeDtypeStruct((b, s, IDX_DIM), BF16),
            jax.ShapeDtypeStruct((b, IDX_HEADS, s), F32),
        ),
        grid=(b, nt),
        in_specs=[
            seg(A_WIDTH, Z_Q), seg(IDX_HEADS * IDX_DIM, Z_QI), seg(KV_RANK, Z_CKV), seg(LORA_PAD, Z_KIWI),
            pl.BlockSpec((1, KV_RANK), lambda bi, i: (0, 0)),
        ],
        out_specs=(
            pl.BlockSpec((1, A_WIDTH, t), lambda bi, i: (bi, 0, i)),
            pl.BlockSpec((1, IDX_HEADS * IDX_DIM, t), lambda bi, i: (bi, 0, i)),
            pl.BlockSpec((1, t, KV_RANK), lambda bi, i: (bi, i, 0)),
            pl.BlockSpec((1, 1, KV_RANK, t), lambda bi, i: (bi, i, 0, 0)),
            pl.BlockSpec((1, t, IDX_DIM), lambda bi, i: (bi, i, 0)),
            pl.BlockSpec((1, IDX_HEADS, t), lambda bi, i: (bi, 0, i)),
        ),
        compiler_params=_params("parallel", "parallel"),
        name="dsa_prep",
    )(z, z, z, z, ckv_g)


def _bias_kernel(t5_ref, o_ref):
    t = DSA_TILE
    j = lax.broadcasted_iota(I32, (t, t), 0)
    i = lax.broadcasted_iota(I32, (t, t), 1)
    for didx in range(3):
        n = jnp.maximum(didx * t + i - j, 0)
        nf = jnp.maximum(n, 1).astype(F32)
        large = REL_MAX_EXACT + (jnp.log(nf / REL_MAX_EXACT) / math.log(REL_MAX_DIST / REL_MAX_EXACT)
                                 * (REL_BUCKETS - REL_MAX_EXACT)).astype(I32)
        large = jnp.minimum(large, REL_BUCKETS - 1)
        bucket = jnp.where(n < REL_MAX_EXACT, n, large)
        for h in range(A_HEADS):
            val = jnp.zeros((t, t), F32)
            for k in range(REL_BUCKETS):
                val = jnp.where(bucket == k, t5_ref[k, h], val)
            o_ref[didx, :, h * t:(h + 1) * t] = val


def _bias_tiles(t5_bias):
    t = DSA_TILE
    return pl.pallas_call(
        _bias_kernel,
        out_shape=jax.ShapeDtypeStruct((3, t, A_HEADS * t), F32),
        in_specs=[pl.BlockSpec(memory_space=pltpu.SMEM)],
        out_specs=pl.BlockSpec(memory_space=pltpu.VMEM),
        compiler_params=pltpu.CompilerParams(vmem_limit_bytes=VMEM_LIMIT_BYTES),
        name="t5_bias_tiles",
    )(t5_bias)


def _dsa_kernel(qt_ref, qit_ref, ckv_ref, ckvt_ref, ki_ref, wit_ref, wuk_ref, wuvt_ref, bias_ref,
                o_ref, keys_ref, acc_ref, ot_ref, *, topk):
    t = DSA_TILE
    qb = pl.program_id(1)
    nk = qb + 1
    row = lax.broadcasted_iota(I32, (t, t), 0)
    col = lax.broadcasted_iota(I32, (t, t), 1)

    def idx_body(kc, carry):
        kic = ki_ref[0, pl.ds(pl.multiple_of(kc * t, t), t), :]
        acc = jnp.zeros((t, t), F32)
        for h in range(IDX_HEADS):
            rel = _dot(kic, qit_ref[0, h * IDX_DIM:(h + 1) * IDX_DIM, :])
            acc = acc + jnp.maximum(rel, 0.0) * wit_ref[0, h:h + 1, :]
        acc = acc + 0.0
        bits = pltpu.bitcast(acc, I32)
        key = bits ^ ((bits >> 31) & 0x7FFFFFFF)
        causal = (kc * t + row) <= (qb * t + col)
        keys_ref[pl.ds(pl.multiple_of(kc * t, t), t), :] = jnp.where(causal, key, KEY_NEG_INF)
        return carry

    lax.fori_loop(0, nk, idx_body, 0)

    def count_ge(cand):
        def body(kc, acc):
            k = keys_ref[pl.ds(pl.multiple_of(kc * t, t), t), :]
            m = jnp.where(k >= cand, 1, 0).astype(I32)
            return acc + jnp.sum(m.reshape(t // SUBLANES, SUBLANES, t), axis=0)

        acc = lax.fori_loop(0, nk, body, jnp.zeros((SUBLANES, t), I32))
        return jnp.sum(acc, axis=0, keepdims=True)

    thr = jnp.where(count_ge(jnp.zeros((1, t), I32)) >= topk, 0, INT_MIN).astype(I32)

    def bit_body(i, thr):
        cand = thr | (jnp.int32(1) << (30 - i))
        return jnp.where(count_ge(cand) >= topk, cand, thr)

    thr = lax.fori_loop(0, 31, bit_body, thr)
    thr = jnp.maximum(thr, KEY_NEG_INF + 1)

    scale = A_HEAD_DIM ** -0.5

    hw = A_HEADS * t
    qlat = jnp.concatenate(
        [(_dot(wuk_ref[h], qt_ref[0, h * A_HEAD_DIM:(h + 1) * A_HEAD_DIM, :]) * scale).astype(BF16)
         for h in range(A_HEADS)], axis=1)
    acc_ref[...] = jnp.zeros_like(acc_ref)

    def att_body(kc, carry):
        m, l = carry
        sl = pl.ds(pl.multiple_of(kc * t, t), t)
        raw = _dot(ckv_ref[0, sl, :], qlat)
        sel = keys_ref[sl, :] >= thr
        bsel = jnp.minimum(qb - kc, 2)
        lg = jnp.concatenate(
            [jnp.where(sel, raw[:, h * t:(h + 1) * t] + bias_ref[bsel, :, h * t:(h + 1) * t], MASKED_LOGIT)
             for h in range(A_HEADS)], axis=1)
        m_new = jnp.maximum(m, jnp.max(lg, axis=0, keepdims=True))
        alpha = jnp.exp(m - m_new)
        p = jnp.exp(lg - m_new)
        l = l * alpha + jnp.sum(p, axis=0, keepdims=True)
        acc_ref[...] = acc_ref[...] * alpha + _dot(ckvt_ref[0, kc], p.astype(BF16))
        return m_new, l

    m0 = jnp.full((1, hw), MASKED_LOGIT, F32)
    _, l = lax.fori_loop(0, nk, att_body, (m0, jnp.zeros((1, hw), F32)))
    olat = (acc_ref[...] / l).astype(BF16)
    for h in range(A_HEADS):
        ot_ref[h * A_HEAD_DIM:(h + 1) * A_HEAD_DIM, :] = _dot(wuvt_ref[h], olat[:, h * t:(h + 1) * t])
    o_ref[0] = ot_ref[...].T.astype(BF16)


def _dsa(qt, qit, ckv, ckvt, ki, wit, wuk, wuvt, bias):
    b, _, s = qt.shape
    t = DSA_TILE
    nt = s // t
    topk = min(TOPK_MAX, s // 4)
    return pl.pallas_call(
        functools.partial(_dsa_kernel, topk=topk),
        out_shape=jax.ShapeDtypeStruct((b, s, A_WIDTH), BF16),
        grid=(b, nt),
        in_specs=[
            pl.BlockSpec((1, A_WIDTH, t), lambda bi, i: (bi, 0, i)),
            pl.BlockSpec((1, IDX_HEADS * IDX_DIM, t), lambda bi, i: (bi, 0, i)),
            pl.BlockSpec((1, s, KV_RANK), lambda bi, i: (bi, 0, 0)),
            pl.BlockSpec((1, nt, KV_RANK, t), lambda bi, i: (bi, 0, 0, 0)),
            pl.BlockSpec((1, s, IDX_DIM), lambda bi, i: (bi, 0, 0)),
            pl.BlockSpec((1, IDX_HEADS, t), lambda bi, i: (bi, 0, i)),
            pl.BlockSpec((A_HEADS, KV_RANK, A_HEAD_DIM), lambda bi, i: (0, 0, 0)),
            pl.BlockSpec((A_HEADS, A_HEAD_DIM, KV_RANK), lambda bi, i: (0, 0, 0)),
            pl.BlockSpec((3, t, A_HEADS * t), lambda bi, i: (0, 0, 0)),
        ],
        out_specs=pl.BlockSpec((1, t, A_WIDTH), lambda bi, i: (bi, i, 0)),
        scratch_shapes=[pltpu.VMEM((s, t), I32), pltpu.VMEM((KV_RANK, A_HEADS * t), F32), pltpu.VMEM((A_WIDTH, t), F32)],
        compiler_params=_params("parallel", "arbitrary"),
        name="dsa",
    )(qt, qit, ckv, ckvt, ki, wit, wuk, wuvt, bias)


def _split3(x):
    hi = x.astype(BF16)
    r1 = x - hi.astype(F32)
    mid = r1.astype(BF16)
    lo = (r1 - mid.astype(F32)).astype(BF16)
    return hi, mid, lo


def _rwkv_kernel(r_ref, k_ref, v_ref, gl_ref, wl_ref, al_ref,
                 mur_ref, muk_ref, muv_ref, mug_ref, muw_ref, mua_ref,
                 w0_ref, a0_ref, kk_ref, ka_ref, rk_ref, lnw_ref, lnb_ref,
                 w2_ref, a2_ref, g2_ref,
                 o_ref,
                 pr_ref, pk_ref, pv_ref, pg_ref, pw_ref, pa_ref, state_ref,
                 abar_ref, rbar_ref, bt_ref, kt_ref, bh_ref, kh_ref, vv_ref, pc_ref, rkv_ref, gg_ref):
    tl = RWKV_TILE
    c = RWKV_CHUNK
    nc = tl // c
    npair = R_WIDTH // PAIR

    @pl.when(pl.program_id(1) == 0)
    def _():
        state_ref[...] = jnp.zeros_like(state_ref)
        for ref in (pr_ref, pk_ref, pv_ref, pg_ref, pw_ref, pa_ref):
            ref[...] = jnp.zeros_like(ref)

    def shift_mix(x_ref, prev_ref, mu_ref):
        x = x_ref[0]
        first = lax.broadcasted_iota(I32, x.shape, 0) == 0
        xprev = jnp.where(first, prev_ref[...], pltpu.roll(x, 1, axis=0))
        prev_ref[...] = x[tl - 1:tl, :]
        return x + (xprev - x) * mu_ref[...]

    r = shift_mix(r_ref, pr_ref, mur_ref)
    k = shift_mix(k_ref, pk_ref, muk_ref)
    v = shift_mix(v_ref, pv_ref, muv_ref)
    gl = shift_mix(gl_ref, pg_ref, mug_ref)
    wl = shift_mix(wl_ref, pw_ref, muw_ref)
    al = shift_mix(al_ref, pa_ref, mua_ref)

    w_lin = w0_ref[...] + _dot(jnp.tanh(wl).astype(BF16), w2_ref[...])
    nx = -w_lin
    softplus = jnp.maximum(nx, 0.0) + jnp.log(1.0 + jnp.exp(-jnp.abs(nx)))
    ld = -jnp.exp(-softplus - 0.5)
    a = jax.nn.sigmoid(a0_ref[...] + _dot(al.astype(BF16), a2_ref[...]))
    gg_ref[...] = _dot(jax.nn.sigmoid(gl).astype(BF16), g2_ref[...])

    li = lax.broadcasted_iota(I32, (LANES, LANES), 0) // R_HEAD_DIM
    lj = lax.broadcasted_iota(I32, (LANES, LANES), 1) // R_HEAD_DIM
    head_ones = jnp.where(li == lj, 1.0, 0.0).astype(BF16)

    def head_sum(x):
        parts = []
        for j in range(R_WIDTH // LANES):
            hi, mid, lo = _split3(x[:, j * LANES:(j + 1) * LANES])
            parts.append(_dot(hi, head_ones) + _dot(mid, head_ones) + _dot(lo, head_ones))
        return jnp.concatenate(parts, axis=-1)

    kk = k * kk_ref[...]
    kk = kk / jnp.maximum(jnp.sqrt(head_sum(kk * kk)), 1e-12)
    k2 = k * (1.0 + (a - 1.0) * ka_ref[...])
    bb = kk * a

    ti = lax.broadcasted_iota(I32, (tl, tl), 0)
    tj = lax.broadcasted_iota(I32, (tl, tl), 1)
    tri = jnp.where((ti // c == tj // c) & (tj <= ti), 1.0, 0.0).astype(BF16)
    hi, mid, lo = _split3(ld)
    cum = _dot(tri, hi) + _dot(tri, mid) + _dot(tri, lo)
    cum_end = jnp.broadcast_to(cum.reshape(nc, c, R_WIDTH)[:, c - 1:c, :], (nc, c, R_WIDTH)).reshape(tl, R_WIDTH)
    e_in = jnp.exp(cum)
    e_neg = jnp.exp(-cum)
    abar_ref[...] = -kk * jnp.exp(cum - ld)
    rbar_ref[...] = r * e_in
    bt_ref[...] = bb * e_neg
    kt_ref[...] = k2 * e_neg
    e_out = jnp.exp(cum_end - cum)
    bh_ref[...] = bb * e_out
    kh_ref[...] = k2 * e_out
    vv_ref[...] = v
    pc_ref[...] = jnp.exp(cum_end)
    rkv_ref[...] = r * k2 * rk_ref[...]

    lane_head = lax.broadcasted_iota(I32, (2 * c, PAIR), 1) // R_HEAD_DIM
    row_head = lax.broadcasted_iota(I32, (2 * c, PAIR), 0) // c
    same = lane_head == row_head
    rt = lax.broadcasted_iota(I32, (2 * c, PAIR), 0) % c
    ct = lax.broadcasted_iota(I32, (2 * c, PAIR), 1) % c
    strict = same & (ct < rt)
    incl = same & (ct <= rt)
    eye = jnp.where(same & (ct == rt), 1.0, 0.0).astype(F32)

    def bd(x):
        return jnp.where(same, jnp.concatenate([x, x], axis=1), 0.0)

    lane0 = lax.broadcasted_iota(I32, (c, PAIR), 1) < R_HEAD_DIM

    def pairs(ref, rows):
        return jnp.stack([ref[rows, p * PAIR:(p + 1) * PAIR] for p in range(npair)], axis=0)

    lnw = jnp.stack([lnw_ref[:, p * PAIR:(p + 1) * PAIR] for p in range(npair)], axis=0)
    lnb = jnp.stack([lnb_ref[:, p * PAIR:(p + 1) * PAIR] for p in range(npair)], axis=0)

    def chunk_body(ci, carry):
        rows = pl.ds(pl.multiple_of(ci * c, c), c)
        abar = bd(pairs(abar_ref, rows)).astype(BF16)
        rbar = bd(pairs(rbar_ref, rows)).astype(BF16)
        bt = bd(pairs(bt_ref, rows)).astype(BF16)
        kt = bd(pairs(kt_ref, rows)).astype(BF16)
        v_pl = pairs(vv_ref, rows)
        vbd = bd(v_pl)
        aa = _bdot_nt(jnp.concatenate([abar, rbar], axis=1), jnp.concatenate([bt, kt], axis=1))
        a_ab = jnp.where(strict, aa[:, :2 * c, :PAIR], 0.0)
        a_ak = jnp.where(strict, aa[:, :2 * c, PAIR:], 0.0).astype(BF16)
        a_rb = jnp.where(incl, aa[:, 2 * c:, :PAIR], 0.0).astype(BF16)
        a_rk = jnp.where(incl, aa[:, 2 * c:, PAIR:], 0.0).astype(BF16)
        pw = a_ab.astype(BF16)
        tinv = eye + a_ab
        for _ in range(int(math.log2(c)) - 1):
            pw = _bdot(pw, pw).astype(BF16)
            tinv = tinv + _bdot(tinv.astype(BF16), pw)
        akv = _bdot(a_ak, vbd.astype(BF16))
        wu = _bdot(tinv.astype(BF16), jnp.concatenate([abar, akv.astype(BF16)], axis=2))
        s_bd = state_ref[...]
        wr = _bdot_nt(jnp.concatenate([wu[:, :, :PAIR].astype(BF16), rbar], axis=1), s_bd.astype(BF16))
        e_bd = wr[:, :2 * c] + wu[:, :, PAIR:]
        ev = jnp.concatenate([e_bd, vbd], axis=1)
        y_bd = wr[:, 2 * c:] + _bdot(jnp.concatenate([a_rb, a_rk], axis=2), ev.astype(BF16))
        bk = jnp.concatenate([bd(pairs(bh_ref, rows)), bd(pairs(kh_ref, rows))], axis=1).astype(BF16)
        ds = _bdot(jnp.swapaxes(ev, 1, 2).astype(BF16), bk)
        state_ref[...] = s_bd * pairs(pc_ref, rows)[:, 0:1, :] + jnp.where(same, ds, 0.0)
        mean = jnp.sum(y_bd, axis=-1, keepdims=True) * (1.0 / R_HEAD_DIM)
        dev = jnp.where(same, y_bd - mean, 0.0)
        var = jnp.sum(dev * dev, axis=-1, keepdims=True) * (1.0 / R_HEAD_DIM)
        yn = dev * lax.rsqrt(var + GN_EPS)
        yn = yn[:, :c] + yn[:, c:]
        rkv = pairs(rkv_ref, rows)
        s0 = jnp.sum(jnp.where(lane0, rkv, 0.0), axis=-1, keepdims=True)
        s1 = jnp.sum(jnp.where(lane0, 0.0, rkv), axis=-1, keepdims=True)
        bonus = jnp.where(lane0, s0, s1) * v_pl
        out = ((yn * lnw + lnb + bonus) * pairs(gg_ref, rows)).astype(BF16)
        for p in range(npair):
            o_ref[0, rows, p * PAIR:(p + 1) * PAIR] = out[p]
        return carry

    lax.fori_loop(0, nc, chunk_body, 0)


def _rwkv(z, prm):
    b, s, _ = z.shape
    tl = RWKV_TILE

    def seg(width, off):
        return pl.BlockSpec((1, tl, width), lambda bi, i: (bi, i, off // width))

    def full(arr):
        return pl.BlockSpec(arr.shape, lambda bi, i: (0,) * arr.ndim)

    names = ("mu_r", "mu_k", "mu_v", "mu_g", "mu_w", "mu_a", "w0", "a0", "k_k", "k_a", "r_k", "ln_w", "ln_b",
             "w2", "a2", "g2")
    consts = [prm[n] for n in names]
    wide = pltpu.VMEM((tl, R_WIDTH), F32)
    return pl.pallas_call(
        _rwkv_kernel,
        out_shape=jax.ShapeDtypeStruct((b, s, R_WIDTH), BF16),
        grid=(b, s // tl),
        in_specs=[seg(R_WIDTH, Z_R), seg(R_WIDTH, Z_K), seg(R_WIDTH, Z_V), seg(GATE_LORA, Z_GL),
                  seg(LORA_PAD, Z_WL), seg(LORA_PAD, Z_AL)] + [full(x) for x in consts],
        out_specs=pl.BlockSpec((1, tl, R_WIDTH), lambda bi, i: (bi, i, 0)),
        scratch_shapes=[
            pltpu.VMEM((1, R_WIDTH), F32), pltpu.VMEM((1, R_WIDTH), F32), pltpu.VMEM((1, R_WIDTH), F32),
            pltpu.VMEM((1, GATE_LORA), F32), pltpu.VMEM((1, LORA_PAD), F32), pltpu.VMEM((1, LORA_PAD), F32),
            pltpu.VMEM((R_WIDTH // PAIR, PAIR, PAIR), F32),
            wide, wide, wide, wide, wide, wide, wide, wide, wide, wide,
        ],
        compiler_params=_params("parallel", "arbitrary"),
        name="rwkv7",
    )(z, z, z, z, z, z, *consts)


def _pad_cols(w, width):
    return jnp.pad(w, ((0, 0), (0, width - w.shape[1])))


def _layout_w_in(w):
    offs = [0]
    for n in (A_WIDTH, KV_RANK, IDX_HEADS * IDX_DIM, IDX_DIM, IDX_HEADS,
              R_WIDTH, R_WIDTH, R_WIDTH, DECAY_LORA, AAA_LORA, GATE_LORA):
        offs.append(offs[-1] + n)
    q, ckv, qi, ki, wi, r, k, v, wl, al, gl = (w[:, offs[i]:offs[i + 1]] for i in range(11))
    cols = [q, qi, r, k, v, gl, ckv, _pad_cols(jnp.concatenate([ki, wi], axis=1), LORA_PAD),
            _pad_cols(wl, LORA_PAD), _pad_cols(al, LORA_PAD)]
    return _pad_cols(jnp.concatenate(cols, axis=1), Z_WIDTH).astype(BF16)


def _pad_rows(w, rows):
    return jnp.pad(w, ((0, rows - w.shape[0]), (0, 0)))


def _rwkv_params(l, mu, w0, w2, a0, a2, g2, k_k, k_a, r_k, ln_w, ln_b):
    m = mu[l]
    o = [0, R_WIDTH, 2 * R_WIDTH, 3 * R_WIDTH, 3 * R_WIDTH + DECAY_LORA, 3 * R_WIDTH + DECAY_LORA + AAA_LORA]
    row = lambda x: x.reshape(1, -1)
    return {
        "mu_r": row(m[o[0]:o[1]]), "mu_k": row(m[o[1]:o[2]]), "mu_v": row(m[o[2]:o[3]]),
        "mu_w": _pad_cols(row(m[o[3]:o[4]]), LORA_PAD), "mu_a": _pad_cols(row(m[o[4]:o[5]]), LORA_PAD),
        "mu_g": row(m[o[5]:]),
        "w0": row(w0[l]), "a0": row(a0[l]), "k_k": row(k_k[l]), "k_a": row(k_a[l]), "r_k": row(r_k[l]),
        "ln_w": row(ln_w[l]), "ln_b": row(ln_b[l]),
        "w2": _pad_rows(w2[l], LORA_PAD).astype(BF16), "a2": _pad_rows(a2[l], LORA_PAD).astype(BF16),
        "g2": g2[l].astype(BF16),
    }


def kernel(x, c, t5_bias, ada_w, ada_b, norm_g, ffn_w_in, ffn_w_out, w_in, ckv_norm_g, w_uk, w_uv, rwkv_mu, rwkv_w0, rwkv_w2, rwkv_a0, rwkv_a2, rwkv_g2, rwkv_k_k, rwkv_k_a, rwkv_r_k, rwkv_ln_w, rwkv_ln_b, w_out, final_norm_g):
    b, s, d = x.shape
    depth = ada_w.shape[0]
    assert s % DSA_TILE == 0 and s % RWKV_TILE == 0 and d == A_WIDTH + R_WIDTH
    mod = _ada_mod(c, ada_w, ada_b).reshape(depth, b, N_SUB, 3, 1, d)
    bias = _bias_tiles(t5_bias)
    final_g = final_norm_g.reshape(1, d)
    h = x
    for l in range(depth):
        shift = lambda i: mod[l, :, i, 0]
        scale = lambda i: mod[l, :, i, 1]
        gate = lambda i: mod[l, :, i, 2]
        g = lambda i: norm_g[l, i].reshape(1, d)
        h = _ffn(h, g(0), shift(0), scale(0), gate(0), ffn_w_in[l, 0].astype(BF16), ffn_w_out[l, 0].astype(BF16),
                 final_g, final_norm=False)
        z = _proj(h, g(1), shift(1), scale(1), _layout_w_in(w_in[l]))
        qt, qit, ckv, ckvt, ki, wit = _dsa_prep(z, ckv_norm_g[l].reshape(1, KV_RANK))
        wuk = jnp.transpose(w_uk[l], (1, 0, 2)).astype(BF16)
        wuvt = jnp.transpose(w_uv[l], (1, 2, 0)).astype(BF16)
        o_a = _dsa(qt, qit, ckv, ckvt, ki, wit, wuk, wuvt, bias)
        o_r = _rwkv(z, _rwkv_params(l, rwkv_mu, rwkv_w0, rwkv_w2, rwkv_a0, rwkv_a2, rwkv_g2, rwkv_k_k,
                                    rwkv_k_a, rwkv_r_k, rwkv_ln_w, rwkv_ln_b))
        h = _outproj(o_a, o_r, h, gate(1), w_out[l].astype(BF16))
        h = _ffn(h, g(2), shift(2), scale(2), gate(2), ffn_w_in[l, 1].astype(BF16), ffn_w_out[l, 1].astype(BF16),
                 final_g, final_norm=(l == depth - 1))
    return h
```

```python
import functools
import math

import jax
import jax.numpy as jnp
from jax import lax
from jax.experimental import pallas as pl
from jax.experimental.pallas import tpu as pltpu

F32 = jnp.float32
BF16 = jnp.bfloat16
I32 = jnp.int32

A_HEADS = 8
A_HEAD_DIM = 128
A_WIDTH = A_HEADS * A_HEAD_DIM
KV_RANK = 256
IDX_HEADS = 16
IDX_DIM = 64
TOPK_MAX = 256
REL_BUCKETS = 32
REL_MAX_EXACT = REL_BUCKETS // 2
REL_MAX_DIST = 128
R_HEAD_DIM = 64
R_WIDTH = 1024
R_HEADS = R_WIDTH // R_HEAD_DIM
DECAY_LORA = 96
AAA_LORA = 96
GATE_LORA = 256
GN_EPS = 64e-5
RMS_EPS = 1e-6
N_SUB = 3

LANES = 128
SUBLANES = 8
VMEM_LIMIT_BYTES = 56 * 1024 * 1024

LORA_PAD = 128
Z_Q = 0
Z_QI = Z_Q + A_WIDTH
Z_R = Z_QI + IDX_HEADS * IDX_DIM
Z_K = Z_R + R_WIDTH
Z_V = Z_K + R_WIDTH
Z_GL = Z_V + R_WIDTH
Z_CKV = Z_GL + GATE_LORA
Z_KIWI = Z_CKV + KV_RANK
Z_WL = Z_KIWI + LORA_PAD
Z_AL = Z_WL + LORA_PAD
Z_END = Z_AL + LORA_PAD
Z_WIDTH = 6144

DSA_TILE = 256
RWKV_CHUNK = 64
RWKV_TILE = 256
PAIR = 2 * R_HEAD_DIM

INT_MIN = -(2 ** 31)
KEY_NEG_INF = -2139095041
MASKED_LOGIT = -1e30
LOG2E = math.log2(math.e)
FAR_DISTANCE = math.ceil(REL_MAX_EXACT * (REL_MAX_DIST / REL_MAX_EXACT)
                         ** ((REL_BUCKETS - 1 - REL_MAX_EXACT) / (REL_BUCKETS - REL_MAX_EXACT)))


def _dot(a, b):
    return jnp.dot(a, b, preferred_element_type=F32)


def _dot_nt(a, b):
    return lax.dot_general(a, b, (((1,), (1,)), ((), ())), preferred_element_type=F32)


def _bdot(a, b):
    return lax.dot_general(a, b, (((2,), (1,)), ((0,), (0,))), preferred_element_type=F32)


def _bdot_nt(a, b):
    return lax.dot_general(a, b, (((2,), (2,)), ((0,), (0,))), preferred_element_type=F32)


def _rms(x, g, eps):
    ms = jnp.mean(x * x, axis=-1, keepdims=True)
    return x * lax.rsqrt(ms + eps) * g


def _divisor_tile(n, pref):
    if n <= pref:
        return n
    t = (pref // LANES) * LANES
    while t > LANES and n % t:
        t -= LANES
    assert n % t == 0, (n, pref)
    return t


def _params(*sem):
    return pltpu.CompilerParams(dimension_semantics=sem, vmem_limit_bytes=VMEM_LIMIT_BYTES)


def _ada_kernel(c_ref, w_ref, b_ref, o_ref):
    c = c_ref[...]
    ca = (c * jax.nn.sigmoid(c)).astype(BF16)
    o_ref[0] = _dot(ca, w_ref[0].astype(BF16)) + b_ref[0]


def _ada_mod(c, ada_w, ada_b):
    depth, d, n = ada_w.shape
    b = c.shape[0]
    bp = -(-b // SUBLANES) * SUBLANES
    cp = jnp.pad(c, ((0, bp - b), (0, 0)))
    tn = _divisor_tile(n, 1024)
    out = pl.pallas_call(
        _ada_kernel,
        out_shape=jax.ShapeDtypeStruct((depth, bp, n), F32),
        grid=(depth, n // tn),
        in_specs=[
            pl.BlockSpec((bp, d), lambda l, j: (0, 0)),
            pl.BlockSpec((1, d, tn), lambda l, j: (l, 0, j)),
            pl.BlockSpec((1, 1, tn), lambda l, j: (l, 0, j)),
        ],
        out_specs=pl.BlockSpec((1, bp, tn), lambda l, j: (l, 0, j)),
        compiler_params=_params("arbitrary", "arbitrary"),
        name="ada_mod",
    )(cp, ada_w, ada_b.reshape(depth, 1, n))
    return out[:, :b]


def _ffn_kernel(h_ref, g_ref, shift_ref, scale_ref, gate_ref, wg_ref, wu_ref, wo_ref, fg_ref,
                o_ref, hn_ref, acc_ref, *, final_norm):
    f = pl.program_id(2)

    @pl.when(f == 0)
    def _():
        y = _rms(h_ref[0], g_ref[...], RMS_EPS)
        hn_ref[...] = (y * (1.0 + scale_ref[0]) + shift_ref[0]).astype(BF16)
        acc_ref[...] = jnp.zeros_like(acc_ref)

    hn = hn_ref[...]
    g = _dot(hn, wg_ref[...])
    u = _dot(hn, wu_ref[...])
    act = (g * jax.nn.sigmoid(g) * u).astype(BF16)
    acc_ref[...] += _dot(act, wo_ref[...])

    @pl.when(f == pl.num_programs(2) - 1)
    def _():
        out = h_ref[0] + 0.5 * gate_ref[0] * acc_ref[...]
        if final_norm:
            out = _rms(out, fg_ref[...], RMS_EPS)
        o_ref[0] = out


def _ffn(h, g, shift, scale, gate, w_in, w_out, final_g, *, final_norm):
    b, s, d = h.shape
    ff = w_out.shape[0]
    tm = _divisor_tile(s, 512)
    tf = _divisor_tile(ff, 512)
    nf = ff // tf
    vec = pl.BlockSpec((1, 1, d), lambda bi, i, f: (bi, 0, 0))
    row = pl.BlockSpec((1, d), lambda bi, i, f: (0, 0))
    tile = pl.BlockSpec((1, tm, d), lambda bi, i, f: (bi, i, 0))
    return pl.pallas_call(
        functools.partial(_ffn_kernel, final_norm=final_norm),
        out_shape=jax.ShapeDtypeStruct((b, s, d), F32),
        grid=(b, s // tm, nf),
        in_specs=[
            tile, row, vec, vec, vec,
            pl.BlockSpec((d, tf), lambda bi, i, f: (0, f)),
            pl.BlockSpec((d, tf), lambda bi, i, f: (0, nf + f)),
            pl.BlockSpec((tf, d), lambda bi, i, f: (f, 0)),
            row,
        ],
        out_specs=tile,
        scratch_shapes=[pltpu.VMEM((tm, d), BF16), pltpu.VMEM((tm, d), F32)],
        compiler_params=_params("parallel", "parallel", "arbitrary"),
        name="ffn",
    )(h, g, shift, scale, gate, w_in, w_in, w_out, final_g)


def _proj_kernel(h_ref, g_ref, shift_ref, scale_ref, w_ref, o_ref, hn_ref):
    @pl.when(pl.program_id(2) == 0)
    def _():
        y = _rms(h_ref[0], g_ref[...], RMS_EPS)
        hn_ref[...] = (y * (1.0 + scale_ref[0]) + shift_ref[0]).astype(BF16)

    o_ref[0] = _dot(hn_ref[...], w_ref[...])


def _proj(h, g, shift, scale, w):
    b, s, d = h.shape
    p = w.shape[1]
    tm = _divisor_tile(s, 1024)
    tn = _divisor_tile(p, 1024)
    vec = pl.BlockSpec((1, 1, d), lambda bi, i, n: (bi, 0, 0))
    return pl.pallas_call(
        _proj_kernel,
        out_shape=jax.ShapeDtypeStruct((b, s, p), F32),
        grid=(b, s // tm, p // tn),
        in_specs=[
            pl.BlockSpec((1, tm, d), lambda bi, i, n: (bi, i, 0)),
            pl.BlockSpec((1, d), lambda bi, i, n: (0, 0)),
            vec, vec,
            pl.BlockSpec((d, tn), lambda bi, i, n: (0, n)),
        ],
        out_specs=pl.BlockSpec((1, tm, tn), lambda bi, i, n: (bi, i, n)),
        scratch_shapes=[pltpu.VMEM((tm, d), BF16)],
        compiler_params=_params("parallel", "parallel", "arbitrary"),
        name="proj",
    )(h, g, shift, scale, w)


def _outproj_kernel(oa_ref, or_ref, h_ref, gate_ref, wa_ref, wr_ref, o_ref):
    acc = _dot(oa_ref[0], wa_ref[...]) + _dot(or_ref[0], wr_ref[...])
    o_ref[0] = h_ref[0] + gate_ref[0] * acc


def _outproj(o_a, o_r, h, gate, w_out):
    b, s, d = h.shape
    tm = _divisor_tile(s, 1024)
    tn = _divisor_tile(d, 1024)
    wa = o_a.shape[-1]
    wr = o_r.shape[-1]
    return pl.pallas_call(
        _outproj_kernel,
        out_shape=jax.ShapeDtypeStruct((b, s, d), F32),
        grid=(b, s // tm, d // tn),
        in_specs=[
            pl.BlockSpec((1, tm, wa), lambda bi, i, n: (bi, i, 0)),
            pl.BlockSpec((1, tm, wr), lambda bi, i, n: (bi, i, 0)),
            pl.BlockSpec((1, tm, tn), lambda bi, i, n: (bi, i, n)),
            pl.BlockSpec((1, 1, tn), lambda bi, i, n: (bi, 0, n)),
            pl.BlockSpec((wa, tn), lambda bi, i, n: (0, n)),
            pl.BlockSpec((wr, tn), lambda bi, i, n: (wa // wr, n)),
        ],
        out_specs=pl.BlockSpec((1, tm, tn), lambda bi, i, n: (bi, i, n)),
        compiler_params=_params("parallel", "parallel", "arbitrary"),
        name="outproj",
    )(o_a, o_r, h, gate, w_out, w_out)


def _dsa_prep_kernel(q_ref, qi_ref, ckv_ref, kiwi_ref, g_ref, qt_ref, qit_ref, ckv_o, ckvt_o, ki_o, wit_o):
    qt_ref[0] = q_ref[0].T.astype(BF16)
    qit_ref[0] = qi_ref[0].T.astype(BF16)
    cn = _rms(ckv_ref[0], g_ref[...], RMS_EPS)
    ckv_o[0] = cn.astype(BF16)
    ckvt_o[0, 0] = cn.T.astype(BF16)
    kw = kiwi_ref[0]
    ki_o[0] = kw[:, :IDX_DIM].astype(BF16)
    wit_o[0] = kw.T[IDX_DIM:IDX_DIM + IDX_HEADS, :] * (IDX_HEADS * IDX_DIM) ** -0.5


def _dsa_prep(z, ckv_g):
    b, s, _ = z.shape
    t = DSA_TILE
    nt = s // t

    def seg(width, off):
        return pl.BlockSpec((1, t, width), lambda bi, i: (bi, i, off // width))

    return pl.pallas_call(
        _dsa_prep_kernel,
        out_shape=(
            jax.ShapeDtypeStruct((b, A_WIDTH, s), BF16),
            jax.ShapeDtypeStruct((b, IDX_HEADS * IDX_DIM, s), BF16),
            jax.ShapeDtypeStruct((b, s, KV_RANK), BF16),
            jax.ShapeDtypeStruct((b, nt, KV_RANK, t), BF16),
            jax.ShapeDtypeStruct((b, s, IDX_DIM), BF16),
            jax.ShapeDtypeStruct((b, IDX_HEADS, s), F32),
        ),
        grid=(b, nt),
        in_specs=[
            seg(A_WIDTH, Z_Q), seg(IDX_HEADS * IDX_DIM, Z_QI), seg(KV_RANK, Z_CKV), seg(LORA_PAD, Z_KIWI),
            pl.BlockSpec((1, KV_RANK), lambda bi, i: (0, 0)),
        ],
        out_specs=(
            pl.BlockSpec((1, A_WIDTH, t), lambda bi, i: (bi, 0, i)),
            pl.BlockSpec((1, IDX_HEADS * IDX_DIM, t), lambda bi, i: (bi, 0, i)),
            pl.BlockSpec((1, t, KV_RANK), lambda bi, i: (bi, i, 0)),
            pl.BlockSpec((1, 1, KV_RANK, t), lambda bi, i: (bi, i, 0, 0)),
            pl.BlockSpec((1, t, IDX_DIM), lambda bi, i: (bi, i, 0)),
            pl.BlockSpec((1, IDX_HEADS, t), lambda bi, i: (bi, 0, i)),
        ),
        compiler_params=_params("parallel", "parallel"),
        name="dsa_prep",
    )(z, z, z, z, ckv_g)


def _bias_kernel(t5_ref, o_ref):
    t = DSA_TILE
    j = lax.broadcasted_iota(I32, (t, t), 0)
    i = lax.broadcasted_iota(I32, (t, t), 1)
    for didx in range(2):
        n = jnp.maximum(didx * t + i - j, 0)
        nf = jnp.maximum(n, 1).astype(F32)
        large = REL_MAX_EXACT + (jnp.log(nf / REL_MAX_EXACT) / math.log(REL_MAX_DIST / REL_MAX_EXACT)
                                 * (REL_BUCKETS - REL_MAX_EXACT)).astype(I32)
        large = jnp.minimum(large, REL_BUCKETS - 1)
        bucket = jnp.where(n < REL_MAX_EXACT, n, large)
        for h in range(A_HEADS):
            val = jnp.zeros((t, t), F32)
            for k in range(REL_BUCKETS):
                val = jnp.where(bucket == k, t5_ref[k, h], val)
            o_ref[didx, :, h * t:(h + 1) * t] = (val - t5_ref[REL_BUCKETS - 1, h]) * LOG2E


def _bias_tiles(t5_bias):
    t = DSA_TILE
    assert t + 1 >= FAR_DISTANCE
    return pl.pallas_call(
        _bias_kernel,
        out_shape=jax.ShapeDtypeStruct((2, t, A_HEADS * t), F32),
        in_specs=[pl.BlockSpec(memory_space=pltpu.SMEM)],
        out_specs=pl.BlockSpec(memory_space=pltpu.VMEM),
        compiler_params=pltpu.CompilerParams(vmem_limit_bytes=VMEM_LIMIT_BYTES),
        name="t5_bias_tiles",
    )(t5_bias)


def _dsa_kernel(qt_ref, qit_ref, ckv_ref, ckvt_ref, ki_ref, wit_ref, wuk_ref, wuvt_ref, bias_ref,
                o_ref, keys_ref, acc_ref, ot_ref, *, topk):
    t = DSA_TILE
    qb = pl.program_id(1)
    nk = qb + 1
    row = lax.broadcasted_iota(I32, (t, t), 0)
    col = lax.broadcasted_iota(I32, (t, t), 1)

    def idx_body(kc, carry):
        kic = ki_ref[0, pl.ds(pl.multiple_of(kc * t, t), t), :]
        acc = jnp.zeros((t, t), F32)
        for h in range(IDX_HEADS):
            rel = _dot(kic, qit_ref[0, h * IDX_DIM:(h + 1) * IDX_DIM, :])
            acc = acc + jnp.maximum(rel, 0.0) * wit_ref[0, h:h + 1, :]
        acc = acc + 0.0
        bits = pltpu.bitcast(acc, I32)
        key = bits ^ ((bits >> 31) & 0x7FFFFFFF)
        causal = (kc * t + row) <= (qb * t + col)
        keys_ref[pl.ds(pl.multiple_of(kc * t, t), t), :] = jnp.where(causal, key, KEY_NEG_INF)
        return carry

    lax.fori_loop(0, nk, idx_body, 0)

    def count_ge(cand):
        def body(kc, acc):
            k = keys_ref[pl.ds(pl.multiple_of(kc * t, t), t), :]
            m = jnp.where(k >= cand, 1, 0).astype(I32)
            return acc + jnp.sum(m.reshape(t // SUBLANES, SUBLANES, t), axis=0)

        acc = lax.fori_loop(0, nk, body, jnp.zeros((SUBLANES, t), I32))
        return jnp.sum(acc, axis=0, keepdims=True)

    thr = jnp.where(count_ge(jnp.zeros((1, t), I32)) >= topk, 0, INT_MIN).astype(I32)

    def bit_body(i, thr):
        cand = thr | (jnp.int32(1) << (30 - i))
        return jnp.where(count_ge(cand) >= topk, cand, thr)

    thr = lax.fori_loop(0, 31, bit_body, thr)
    thr = jnp.maximum(thr, KEY_NEG_INF + 1)

    scale = A_HEAD_DIM ** -0.5 * LOG2E

    hw = A_HEADS * t
    qlat = jnp.concatenate(
        [(_dot(wuk_ref[h], qt_ref[0, h * A_HEAD_DIM:(h + 1) * A_HEAD_DIM, :]) * scale).astype(BF16)
         for h in range(A_HEADS)], axis=1)
    acc_ref[...] = jnp.zeros_like(acc_ref)

    def att_body(kc, carry, *, near):
        m, l = carry
        sl = pl.ds(pl.multiple_of(kc * t, t), t)
        raw = _dot(ckv_ref[0, sl, :], qlat)
        sel = keys_ref[sl, :] >= thr
        parts = []
        for h in range(A_HEADS):
            lg_h = raw[:, h * t:(h + 1) * t]
            if near:
                lg_h = lg_h + bias_ref[qb - kc, :, h * t:(h + 1) * t]
            parts.append(jnp.where(sel, lg_h, MASKED_LOGIT))
        lg = jnp.concatenate(parts, axis=1)
        m_new = jnp.maximum(m, jnp.max(lg, axis=0, keepdims=True))
        alpha = jnp.exp2(m - m_new)
        p = jnp.exp2(lg - m_new)
        l = l * alpha + jnp.sum(p, axis=0, keepdims=True)
        acc_ref[...] = acc_ref[...] * alpha + _dot(ckvt_ref[0, kc], p.astype(BF16))
        return m_new, l

    n_far = jnp.maximum(qb - 1, 0)
    carry = (jnp.full((1, hw), MASKED_LOGIT, F32), jnp.zeros((1, hw), F32))
    carry = lax.fori_loop(0, n_far, functools.partial(att_body, near=False), carry)
    _, l = lax.fori_loop(n_far, nk, functools.partial(att_body, near=True), carry)
    olat = (acc_ref[...] / l).astype(BF16)
    for h in range(A_HEADS):
        ot_ref[h * A_HEAD_DIM:(h + 1) * A_HEAD_DIM, :] = _dot(wuvt_ref[h], olat[:, h * t:(h + 1) * t])
    o_ref[0] = ot_ref[...].T.astype(BF16)


def _dsa(qt, qit, ckv, ckvt, ki, wit, wuk, wuvt, bias):
    b, _, s = qt.shape
    t = DSA_TILE
    nt = s // t
    topk = min(TOPK_MAX, s // 4)
    return pl.pallas_call(
        functools.partial(_dsa_kernel, topk=topk),
        out_shape=jax.ShapeDtypeStruct((b, s, A_WIDTH), BF16),
        grid=(b, nt),
        in_specs=[
            pl.BlockSpec((1, A_WIDTH, t), lambda bi, i: (bi, 0, i)),
            pl.BlockSpec((1, IDX_HEADS * IDX_DIM, t), lambda bi, i: (bi, 0, i)),
            pl.BlockSpec((1, s, KV_RANK), lambda bi, i: (bi, 0, 0)),
            pl.BlockSpec((1, nt, KV_RANK, t), lambda bi, i: (bi, 0, 0, 0)),
            pl.BlockSpec((1, s, IDX_DIM), lambda bi, i: (bi, 0, 0)),
            pl.BlockSpec((1, IDX_HEADS, t), lambda bi, i: (bi, 0, i)),
            pl.BlockSpec((A_HEADS, KV_RANK, A_HEAD_DIM), lambda bi, i: (0, 0, 0)),
            pl.BlockSpec((A_HEADS, A_HEAD_DIM, KV_RANK), lambda bi, i: (0, 0, 0)),
            pl.BlockSpec((2, t, A_HEADS * t), lambda bi, i: (0, 0, 0)),
        ],
        out_specs=pl.BlockSpec((1, t, A_WIDTH), lambda bi, i: (bi, i, 0)),
        scratch_shapes=[pltpu.VMEM((s, t), I32), pltpu.VMEM((KV_RANK, A_HEADS * t), F32), pltpu.VMEM((A_WIDTH, t), F32)],
        compiler_params=_params("parallel", "arbitrary"),
        name="dsa",
    )(qt, qit, ckv, ckvt, ki, wit, wuk, wuvt, bias)


def _split3(x):
    hi = x.astype(BF16)
    r1 = x - hi.astype(F32)
    mid = r1.astype(BF16)
    lo = (r1 - mid.astype(F32)).astype(BF16)
    return hi, mid, lo


def _rwkv_kernel(r_ref, k_ref, v_ref, gl_ref, wl_ref, al_ref,
                 mur_ref, muk_ref, muv_ref, mug_ref, muw_ref, mua_ref,
                 w0_ref, a0_ref, kk_ref, ka_ref, rk_ref, lnw_ref, lnb_ref,
                 w2_ref, a2_ref, g2_ref,
                 o_ref,
                 pr_ref, pk_ref, pv_ref, pg_ref, pw_ref, pa_ref, state_ref,
                 abar_ref, rbar_ref, bt_ref, kt_ref, bh_ref, kh_ref, vv_ref, pc_ref, rkv_ref, gg_ref):
    tl = RWKV_TILE
    c = RWKV_CHUNK
    nc = tl // c
    npair = R_WIDTH // PAIR

    @pl.when(pl.program_id(1) == 0)
    def _():
        state_ref[...] = jnp.zeros_like(state_ref)
        for ref in (pr_ref, pk_ref, pv_ref, pg_ref, pw_ref, pa_ref):
            ref[...] = jnp.zeros_like(ref)

    def shift_mix(x_ref, prev_ref, mu_ref):
        x = x_ref[0]
        first = lax.broadcasted_iota(I32, x.shape, 0) == 0
        xprev = jnp.where(first, prev_ref[...], pltpu.roll(x, 1, axis=0))
        prev_ref[...] = x[tl - 1:tl, :]
        return x + (xprev - x) * mu_ref[...]

    r = shift_mix(r_ref, pr_ref, mur_ref)
    k = shift_mix(k_ref, pk_ref, muk_ref)
    v = shift_mix(v_ref, pv_ref, muv_ref)
    gl = shift_mix(gl_ref, pg_ref, mug_ref)
    wl = shift_mix(wl_ref, pw_ref, muw_ref)
    al = shift_mix(al_ref, pa_ref, mua_ref)

    w_lin = w0_ref[...] + _dot(jnp.tanh(wl).astype(BF16), w2_ref[...])
    nx = -w_lin
    softplus = jnp.maximum(nx, 0.0) + jnp.log(1.0 + jnp.exp(-jnp.abs(nx)))
    ld = -jnp.exp(-softplus - 0.5)
    a = jax.nn.sigmoid(a0_ref[...] + _dot(al.astype(BF16), a2_ref[...]))
    gg_ref[...] = _dot(jax.nn.sigmoid(gl).astype(BF16), g2_ref[...])

    li = lax.broadcasted_iota(I32, (LANES, LANES), 0) // R_HEAD_DIM
    lj = lax.broadcasted_iota(I32, (LANES, LANES), 1) // R_HEAD_DIM
    head_ones = jnp.where(li == lj, 1.0, 0.0).astype(BF16)

    def head_sum(x):
        parts = []
        for j in range(R_WIDTH // LANES):
            hi, mid, lo = _split3(x[:, j * LANES:(j + 1) * LANES])
            parts.append(_dot(hi, head_ones) + _dot(mid, head_ones) + _dot(lo, head_ones))
        return jnp.concatenate(parts, axis=-1)

    kk = k * kk_ref[...]
    kk = kk / jnp.maximum(jnp.sqrt(head_sum(kk * kk)), 1e-12)
    k2 = k * (1.0 + (a - 1.0) * ka_ref[...])
    bb = kk * a

    ti = lax.broadcasted_iota(I32, (tl, tl), 0)
    tj = lax.broadcasted_iota(I32, (tl, tl), 1)
    tri = jnp.where((ti // c == tj // c) & (tj <= ti), 1.0, 0.0).astype(BF16)
    hi, mid, lo = _split3(ld)
    cum = _dot(tri, hi) + _dot(tri, mid) + _dot(tri, lo)
    cum_end = jnp.broadcast_to(cum.reshape(nc, c, R_WIDTH)[:, c - 1:c, :], (nc, c, R_WIDTH)).reshape(tl, R_WIDTH)
    e_in = jnp.exp(cum)
    e_neg = jnp.exp(-cum)
    abar_ref[...] = -kk * jnp.exp(cum - ld)
    rbar_ref[...] = r * e_in
    bt_ref[...] = bb * e_neg
    kt_ref[...] = k2 * e_neg
    e_out = jnp.exp(cum_end - cum)
    bh_ref[...] = bb * e_out
    kh_ref[...] = k2 * e_out
    vv_ref[...] = v
    pc_ref[...] = jnp.exp(cum_end)
    rkv_ref[...] = r * k2 * rk_ref[...]

    lane_head = lax.broadcasted_iota(I32, (2 * c, PAIR), 1) // R_HEAD_DIM
    row_head = lax.broadcasted_iota(I32, (2 * c, PAIR), 0) // c
    same = lane_head == row_head
    rt = lax.broadcasted_iota(I32, (2 * c, PAIR), 0) % c
    ct = lax.broadcasted_iota(I32, (2 * c, PAIR), 1) % c
    strict = same & (ct < rt)
    incl = same & (ct <= rt)
    eye = jnp.where(same & (ct == rt), 1.0, 0.0).astype(F32)

    def bd(x):
        return jnp.where(same, jnp.concatenate([x, x], axis=1), 0.0)

    lane0 = lax.broadcasted_iota(I32, (c, PAIR), 1) < R_HEAD_DIM

    def pairs(ref, rows):
        return jnp.stack([ref[rows, p * PAIR:(p + 1) * PAIR] for p in range(npair)], axis=0)

    lnw = jnp.stack([lnw_ref[:, p * PAIR:(p + 1) * PAIR] for p in range(npair)], axis=0)
    lnb = jnp.stack([lnb_ref[:, p * PAIR:(p + 1) * PAIR] for p in range(npair)], axis=0)

    def chunk_body(ci, carry):
        rows = pl.ds(pl.multiple_of(ci * c, c), c)
        abar = bd(pairs(abar_ref, rows)).astype(BF16)
        rbar = bd(pairs(rbar_ref, rows)).astype(BF16)
        bt = bd(pairs(bt_ref, rows)).astype(BF16)
        kt = bd(pairs(kt_ref, rows)).astype(BF16)
        v_pl = pairs(vv_ref, rows)
        vbd = bd(v_pl)
        aa = _bdot_nt(jnp.concatenate([abar, rbar], axis=1), jnp.concatenate([bt, kt], axis=1))
        a_ab = jnp.where(strict, aa[:, :2 * c, :PAIR], 0.0)
        a_ak = jnp.where(strict, aa[:, :2 * c, PAIR:], 0.0).astype(BF16)
        a_rb = jnp.where(incl, aa[:, 2 * c:, :PAIR], 0.0).astype(BF16)
        a_rk = jnp.where(incl, aa[:, 2 * c:, PAIR:], 0.0).astype(BF16)
        pw = a_ab.astype(BF16)
        tinv = eye + a_ab
        for _ in range(int(math.log2(c)) - 1):
            pw = _bdot(pw, pw).astype(BF16)
            tinv = tinv + _bdot(tinv.astype(BF16), pw)
        akv = _bdot(a_ak, vbd.astype(BF16))
        wu = _bdot(tinv.astype(BF16), jnp.concatenate([abar, akv.astype(BF16)], axis=2))
        s_bd = state_ref[...]
        wr = _bdot_nt(jnp.concatenate([wu[:, :, :PAIR].astype(BF16), rbar], axis=1), s_bd.astype(BF16))
        e_bd = wr[:, :2 * c] + wu[:, :, PAIR:]
        ev = jnp.concatenate([e_bd, vbd], axis=1)
        y_bd = wr[:, 2 * c:] + _bdot(jnp.concatenate([a_rb, a_rk], axis=2), ev.astype(BF16))
        bk = jnp.concatenate([bd(pairs(bh_ref, rows)), bd(pairs(kh_ref, rows))], axis=1).astype(BF16)
        ds = _bdot(jnp.swapaxes(ev, 1, 2).astype(BF16), bk)
        state_ref[...] = s_bd * pairs(pc_ref, rows)[:, 0:1, :] + jnp.where(same, ds, 0.0)
        mean = jnp.sum(y_bd, axis=-1, keepdims=True) * (1.0 / R_HEAD_DIM)
        dev = jnp.where(same, y_bd - mean, 0.0)
        var = jnp.sum(dev * dev, axis=-1, keepdims=True) * (1.0 / R_HEAD_DIM)
        yn = dev * lax.rsqrt(var + GN_EPS)
        yn = yn[:, :c] + yn[:, c:]
        rkv = pairs(rkv_ref, rows)
        s0 = jnp.sum(jnp.where(lane0, rkv, 0.0), axis=-1, keepdims=True)
        s1 = jnp.sum(jnp.where(lane0, 0.0, rkv), axis=-1, keepdims=True)
        bonus = jnp.where(lane0, s0, s1) * v_pl
        out = ((yn * lnw + lnb + bonus) * pairs(gg_ref, rows)).astype(BF16)
        for p in range(npair):
            o_ref[0, rows, p * PAIR:(p + 1) * PAIR] = out[p]
        return carry

    lax.fori_loop(0, nc, chunk_body, 0)


def _rwkv(z, prm):
    b, s, _ = z.shape
    tl = RWKV_TILE

    def seg(width, off):
        return pl.BlockSpec((1, tl, width), lambda bi, i: (bi, i, off // width))

    def full(arr):
        return pl.BlockSpec(arr.shape, lambda bi, i: (0,) * arr.ndim)

    names = ("mu_r", "mu_k", "mu_v", "mu_g", "mu_w", "mu_a", "w0", "a0", "k_k", "k_a", "r_k", "ln_w", "ln_b",
             "w2", "a2", "g2")
    consts = [prm[n] for n in names]
    wide = pltpu.VMEM((tl, R_WIDTH), F32)
    return pl.pallas_call(
        _rwkv_kernel,
        out_shape=jax.ShapeDtypeStruct((b, s, R_WIDTH), BF16),
        grid=(b, s // tl),
        in_specs=[seg(R_WIDTH, Z_R), seg(R_WIDTH, Z_K), seg(R_WIDTH, Z_V), seg(GATE_LORA, Z_GL),
                  seg(LORA_PAD, Z_WL), seg(LORA_PAD, Z_AL)] + [full(x) for x in consts],
        out_specs=pl.BlockSpec((1, tl, R_WIDTH), lambda bi, i: (bi, i, 0)),
        scratch_shapes=[
            pltpu.VMEM((1, R_WIDTH), F32), pltpu.VMEM((1, R_WIDTH), F32), pltpu.VMEM((1, R_WIDTH), F32),
            pltpu.VMEM((1, GATE_LORA), F32), pltpu.VMEM((1, LORA_PAD), F32), pltpu.VMEM((1, LORA_PAD), F32),
            pltpu.VMEM((R_WIDTH // PAIR, PAIR, PAIR), F32),
            wide, wide, wide, wide, wide, wide, wide, wide, wide, wide,
        ],
        compiler_params=_params("parallel", "arbitrary"),
        name="rwkv7",
    )(z, z, z, z, z, z, *consts)


def _pad_cols(w, width):
    return jnp.pad(w, ((0, 0), (0, width - w.shape[1])))


def _layout_w_in(w):
    offs = [0]
    for n in (A_WIDTH, KV_RANK, IDX_HEADS * IDX_DIM, IDX_DIM, IDX_HEADS,
              R_WIDTH, R_WIDTH, R_WIDTH, DECAY_LORA, AAA_LORA, GATE_LORA):
        offs.append(offs[-1] + n)
    q, ckv, qi, ki, wi, r, k, v, wl, al, gl = (w[:, offs[i]:offs[i + 1]] for i in range(11))
    cols = [q, qi, r, k, v, gl, ckv, _pad_cols(jnp.concatenate([ki, wi], axis=1), LORA_PAD),
            _pad_cols(wl, LORA_PAD), _pad_cols(al, LORA_PAD)]
    return _pad_cols(jnp.concatenate(cols, axis=1), Z_WIDTH).astype(BF16)


def _pad_rows(w, rows):
    return jnp.pad(w, ((0, rows - w.shape[0]), (0, 0)))


def _rwkv_params(l, mu, w0, w2, a0, a2, g2, k_k, k_a, r_k, ln_w, ln_b):
    m = mu[l]
    o = [0, R_WIDTH, 2 * R_WIDTH, 3 * R_WIDTH, 3 * R_WIDTH + DECAY_LORA, 3 * R_WIDTH + DECAY_LORA + AAA_LORA]
    row = lambda x: x.reshape(1, -1)
    return {
        "mu_r": row(m[o[0]:o[1]]), "mu_k": row(m[o[1]:o[2]]), "mu_v": row(m[o[2]:o[3]]),
        "mu_w": _pad_cols(row(m[o[3]:o[4]]), LORA_PAD), "mu_a": _pad_cols(row(m[o[4]:o[5]]), LORA_PAD),
        "mu_g": row(m[o[5]:]),
        "w0": row(w0[l]), "a0": row(a0[l]), "k_k": row(k_k[l]), "k_a": row(k_a[l]), "r_k": row(r_k[l]),
        "ln_w": row(ln_w[l]), "ln_b": row(ln_b[l]),
        "w2": _pad_rows(w2[l], LORA_PAD).astype(BF16), "a2": _pad_rows(a2[l], LORA_PAD).astype(BF16),
        "g2": g2[l].astype(BF16),
    }


def kernel(x, c, t5_bias, ada_w, ada_b, norm_g, ffn_w_in, ffn_w_out, w_in, ckv_norm_g, w_uk, w_uv, rwkv_mu, rwkv_w0, rwkv_w2, rwkv_a0, rwkv_a2, rwkv_g2, rwkv_k_k, rwkv_k_a, rwkv_r_k, rwkv_ln_w, rwkv_ln_b, w_out, final_norm_g):
    b, s, d = x.shape
    depth = ada_w.shape[0]
    assert s % DSA_TILE == 0 and s % RWKV_TILE == 0 and d == A_WIDTH + R_WIDTH
    mod = _ada_mod(c, ada_w, ada_b).reshape(depth, b, N_SUB, 3, 1, d)
    bias = _bias_tiles(t5_bias)
    final_g = final_norm_g.reshape(1, d)
    h = x
    for l in range(depth):
        shift = lambda i: mod[l, :, i, 0]
        scale = lambda i: mod[l, :, i, 1]
        gate = lambda i: mod[l, :, i, 2]
        g = lambda i: norm_g[l, i].reshape(1, d)
        h = _ffn(h, g(0), shift(0), scale(0), gate(0), ffn_w_in[l, 0].astype(BF16), ffn_w_out[l, 0].astype(BF16),
                 final_g, final_norm=False)
        z = _proj(h, g(1), shift(1), scale(1), _layout_w_in(w_in[l]))
        qt, qit, ckv, ckvt, ki, wit = _dsa_prep(z, ckv_norm_g[l].reshape(1, KV_RANK))
        wuk = jnp.transpose(w_uk[l], (1, 0, 2)).astype(BF16)
        wuvt = jnp.transpose(w_uv[l], (1, 2, 0)).astype(BF16)
        o_a = _dsa(qt, qit, ckv, ckvt, ki, wit, wuk, wuvt, bias)
        o_r = _rwkv(z, _rwkv_params(l, rwkv_mu, rwkv_w0, rwkv_w2, rwkv_a0, rwkv_a2, rwkv_g2, rwkv_k_k,
                                    rwkv_k_a, rwkv_r_k, rwkv_ln_w, rwkv_ln_b))
        h = _outproj(o_a, o_r, h, gate(1), w_out[l].astype(BF16))
        h = _ffn(h, g(2), shift(2), scale(2), gate(2), ffn_w_in[l, 1].astype(BF16), ffn_w_out[l, 1].astype(BF16),
                 final_g, final_norm=(l == depth - 1))
    return h
```

```python
import functools
import math

import jax
import jax.numpy as jnp
from jax import lax
from jax.experimental import pallas as pl
from jax.experimental.pallas import tpu as pltpu

F32 = jnp.float32
BF16 = jnp.bfloat16
I32 = jnp.int32

A_HEADS = 8
A_HEAD_DIM = 128
A_WIDTH = A_HEADS * A_HEAD_DIM
KV_RANK = 256
IDX_HEADS = 16
IDX_DIM = 64
TOPK_MAX = 256
REL_BUCKETS = 32
REL_MAX_EXACT = REL_BUCKETS // 2
REL_MAX_DIST = 128
R_HEAD_DIM = 64
R_WIDTH = 1024
R_HEADS = R_WIDTH // R_HEAD_DIM
DECAY_LORA = 96
AAA_LORA = 96
GATE_LORA = 256
GN_EPS = 64e-5
RMS_EPS = 1e-6
N_SUB = 3

LANES = 128
SUBLANES = 8
VMEM_LIMIT_BYTES = 56 * 1024 * 1024

LORA_PAD = 128
Z_Q = 0
Z_QI = Z_Q + A_WIDTH
Z_R = Z_QI + IDX_HEADS * IDX_DIM
Z_K = Z_R + R_WIDTH
Z_V = Z_K + R_WIDTH
Z_GL = Z_V + R_WIDTH
Z_CKV = Z_GL + GATE_LORA
Z_KIWI = Z_CKV + KV_RANK
Z_WL = Z_KIWI + LORA_PAD
Z_AL = Z_WL + LORA_PAD
Z_END = Z_AL + LORA_PAD
Z_WIDTH = 6144

FFN_TOKEN_TILE = 1024
FFN_HIDDEN_TILE = 256
DSA_TILE = 256
RWKV_CHUNK = 64
RWKV_TILE = 256
PAIR = 2 * R_HEAD_DIM

INT_MIN = -(2 ** 31)
KEY_NEG_INF = -2139095041
MASKED_LOGIT = -1e30
LOG2E = math.log2(math.e)
FAR_DISTANCE = math.ceil(REL_MAX_EXACT * (REL_MAX_DIST / REL_MAX_EXACT)
                         ** ((REL_BUCKETS - 1 - REL_MAX_EXACT) / (REL_BUCKETS - REL_MAX_EXACT)))


def _dot(a, b):
    return jnp.dot(a, b, preferred_element_type=F32)


def _dot_nt(a, b):
    return lax.dot_general(a, b, (((1,), (1,)), ((), ())), preferred_element_type=F32)


def _bdot(a, b):
    return lax.dot_general(a, b, (((2,), (1,)), ((0,), (0,))), preferred_element_type=F32)


def _bdot_nt(a, b):
    return lax.dot_general(a, b, (((2,), (2,)), ((0,), (0,))), preferred_element_type=F32)


def _rms(x, g, eps):
    ms = jnp.mean(x * x, axis=-1, keepdims=True)
    return x * lax.rsqrt(ms + eps) * g


def _divisor_tile(n, pref):
    if n <= pref:
        return n
    t = (pref // LANES) * LANES
    while t > LANES and n % t:
        t -= LANES
    assert n % t == 0, (n, pref)
    return t


def _params(*sem):
    return pltpu.CompilerParams(dimension_semantics=sem, vmem_limit_bytes=VMEM_LIMIT_BYTES)


def _ada_kernel(c_ref, w_ref, b_ref, o_ref):
    c = c_ref[...]
    ca = (c * jax.nn.sigmoid(c)).astype(BF16)
    o_ref[0] = _dot(ca, w_ref[0].astype(BF16)) + b_ref[0]


def _ada_mod(c, ada_w, ada_b):
    depth, d, n = ada_w.shape
    b = c.shape[0]
    bp = -(-b // SUBLANES) * SUBLANES
    cp = jnp.pad(c, ((0, bp - b), (0, 0)))
    tn = _divisor_tile(n, 1024)
    out = pl.pallas_call(
        _ada_kernel,
        out_shape=jax.ShapeDtypeStruct((depth, bp, n), F32),
        grid=(depth, n // tn),
        in_specs=[
            pl.BlockSpec((bp, d), lambda l, j: (0, 0)),
            pl.BlockSpec((1, d, tn), lambda l, j: (l, 0, j)),
            pl.BlockSpec((1, 1, tn), lambda l, j: (l, 0, j)),
        ],
        out_specs=pl.BlockSpec((1, bp, tn), lambda l, j: (l, 0, j)),
        compiler_params=_params("arbitrary", "arbitrary"),
        name="ada_mod",
    )(cp, ada_w, ada_b.reshape(depth, 1, n))
    return out[:, :b]


def _ffn_kernel(h_ref, g_ref, shift_ref, scale_ref, gate_ref, wg_ref, wu_ref, wo_ref, fg_ref,
                o_ref, hn_ref, *, final_norm):
    f = pl.program_id(2)

    @pl.when(f == 0)
    def _():
        h = h_ref[0]
        y = _rms(h, g_ref[...], RMS_EPS)
        hn_ref[...] = (y * (1.0 + scale_ref[0]) + shift_ref[0]).astype(BF16)
        o_ref[0] = h

    hn = hn_ref[...]
    g = _dot(hn, wg_ref[...])
    u = _dot(hn, wu_ref[...])
    act = (g * jax.nn.sigmoid(g) * u).astype(BF16)
    o_ref[0] += (0.5 * gate_ref[0]) * _dot(act, wo_ref[...])

    if final_norm:
        @pl.when(f == pl.num_programs(2) - 1)
        def _():
            o_ref[0] = _rms(o_ref[0], fg_ref[...], RMS_EPS)


def _ffn(h, g, shift, scale, gate, w_in, w_out, final_g, *, final_norm):
    b, s, d = h.shape
    ff = w_out.shape[0]
    tm = _divisor_tile(s, FFN_TOKEN_TILE)
    tf = _divisor_tile(ff, FFN_HIDDEN_TILE)
    nf = ff // tf
    vec = pl.BlockSpec((1, 1, d), lambda bi, i, f: (bi, 0, 0))
    row = pl.BlockSpec((1, d), lambda bi, i, f: (0, 0))
    tile = pl.BlockSpec((1, tm, d), lambda bi, i, f: (bi, i, 0))
    return pl.pallas_call(
        functools.partial(_ffn_kernel, final_norm=final_norm),
        out_shape=jax.ShapeDtypeStruct((b, s, d), F32),
        grid=(b, s // tm, nf),
        in_specs=[
            tile, row, vec, vec, vec,
            pl.BlockSpec((d, tf), lambda bi, i, f: (0, f)),
            pl.BlockSpec((d, tf), lambda bi, i, f: (0, nf + f)),
            pl.BlockSpec((tf, d), lambda bi, i, f: (f, 0)),
            row,
        ],
        out_specs=tile,
        scratch_shapes=[pltpu.VMEM((tm, d), BF16)],
        compiler_params=_params("parallel", "parallel", "arbitrary"),
        name="ffn",
    )(h, g, shift, scale, gate, w_in, w_in, w_out, final_g)


def _proj_kernel(h_ref, g_ref, shift_ref, scale_ref, w_ref, o_ref, hn_ref):
    @pl.when(pl.program_id(2) == 0)
    def _():
        y = _rms(h_ref[0], g_ref[...], RMS_EPS)
        hn_ref[...] = (y * (1.0 + scale_ref[0]) + shift_ref[0]).astype(BF16)

    o_ref[0] = _dot(hn_ref[...], w_ref[...])


def _proj(h, g, shift, scale, w):
    b, s, d = h.shape
    p = w.shape[1]
    tm = _divisor_tile(s, 1024)
    tn = _divisor_tile(p, 1024)
    vec = pl.BlockSpec((1, 1, d), lambda bi, i, n: (bi, 0, 0))
    return pl.pallas_call(
        _proj_kernel,
        out_shape=jax.ShapeDtypeStruct((b, s, p), F32),
        grid=(b, s // tm, p // tn),
        in_specs=[
            pl.BlockSpec((1, tm, d), lambda bi, i, n: (bi, i, 0)),
            pl.BlockSpec((1, d), lambda bi, i, n: (0, 0)),
            vec, vec,
            pl.BlockSpec((d, tn), lambda bi, i, n: (0, n)),
        ],
        out_specs=pl.BlockSpec((1, tm, tn), lambda bi, i, n: (bi, i, n)),
        scratch_shapes=[pltpu.VMEM((tm, d), BF16)],
        compiler_params=_params("parallel", "parallel", "arbitrary"),
        name="proj",
    )(h, g, shift, scale, w)


def _outproj_kernel(oa_ref, or_ref, h_ref, gate_ref, wa_ref, wr_ref, o_ref):
    acc = _dot(oa_ref[0], wa_ref[...]) + _dot(or_ref[0], wr_ref[...])
    o_ref[0] = h_ref[0] + gate_ref[0] * acc


def _outproj(o_a, o_r, h, gate, w_out):
    b, s, d = h.shape
    tm = _divisor_tile(s, 1024)
    tn = _divisor_tile(d, 1024)
    wa = o_a.shape[-1]
    wr = o_r.shape[-1]
    return pl.pallas_call(
        _outproj_kernel,
        out_shape=jax.ShapeDtypeStruct((b, s, d), F32),
        grid=(b, s // tm, d // tn),
        in_specs=[
            pl.BlockSpec((1, tm, wa), lambda bi, i, n: (bi, i, 0)),
            pl.BlockSpec((1, tm, wr), lambda bi, i, n: (bi, i, 0)),
            pl.BlockSpec((1, tm, tn), lambda bi, i, n: (bi, i, n)),
            pl.BlockSpec((1, 1, tn), lambda bi, i, n: (bi, 0, n)),
            pl.BlockSpec((wa, tn), lambda bi, i, n: (0, n)),
            pl.BlockSpec((wr, tn), lambda bi, i, n: (wa // wr, n)),
        ],
        out_specs=pl.BlockSpec((1, tm, tn), lambda bi, i, n: (bi, i, n)),
        compiler_params=_params("parallel", "parallel", "arbitrary"),
        name="outproj",
    )(o_a, o_r, h, gate, w_out, w_out)


def _dsa_prep_kernel(q_ref, qi_ref, ckv_ref, kiwi_ref, g_ref, qt_ref, qit_ref, ckv_o, ckvt_o, ki_o, wit_o):
    qt_ref[0] = q_ref[0].T.astype(BF16)
    qit_ref[0] = qi_ref[0].T.astype(BF16)
    cn = _rms(ckv_ref[0], g_ref[...], RMS_EPS)
    ckv_o[0] = cn.astype(BF16)
    ckvt_o[0, 0] = cn.T.astype(BF16)
    kw = kiwi_ref[0]
    ki_o[0] = kw[:, :IDX_DIM].astype(BF16)
    wit_o[0] = kw.T[IDX_DIM:IDX_DIM + IDX_HEADS, :] * (IDX_HEADS * IDX_DIM) ** -0.5


def _dsa_prep(z, ckv_g):
    b, s, _ = z.shape
    t = DSA_TILE
    nt = s // t

    def seg(width, off):
        return pl.BlockSpec((1, t, width), lambda bi, i: (bi, i, off // width))

    return pl.pallas_call(
        _dsa_prep_kernel,
        out_shape=(
            jax.ShapeDtypeStruct((b, A_WIDTH, s), BF16),
            jax.ShapeDtypeStruct((b, IDX_HEADS * IDX_DIM, s), BF16),
            jax.ShapeDtypeStruct((b, s, KV_RANK), BF16),
            jax.ShapeDtypeStruct((b, nt, KV_RANK, t), BF16),
            jax.ShapeDtypeStruct((b, s, IDX_DIM), BF16),
            jax.ShapeDtypeStruct((b, IDX_HEADS, s), F32),
        ),
        grid=(b, nt),
        in_specs=[
            seg(A_WIDTH, Z_Q), seg(IDX_HEADS * IDX_DIM, Z_QI), seg(KV_RANK, Z_CKV), seg(LORA_PAD, Z_KIWI),
            pl.BlockSpec((1, KV_RANK), lambda bi, i: (0, 0)),
        ],
        out_specs=(
            pl.BlockSpec((1, A_WIDTH, t), lambda bi, i: (bi, 0, i)),
            pl.BlockSpec((1, IDX_HEADS * IDX_DIM, t), lambda bi, i: (bi, 0, i)),
            pl.BlockSpec((1, t, KV_RANK), lambda bi, i: (bi, i, 0)),
            pl.BlockSpec((1, 1, KV_RANK, t), lambda bi, i: (bi, i, 0, 0)),
            pl.BlockSpec((1, t, IDX_DIM), lambda bi, i: (bi, i, 0)),
            pl.BlockSpec((1, IDX_HEADS, t), lambda bi, i: (bi, 0, i)),
        ),
        compiler_params=_params("parallel", "parallel"),
        name="dsa_prep",
    )(z, z, z, z, ckv_g)


def _bias_kernel(t5_ref, o_ref):
    t = DSA_TILE
    j = lax.broadcasted_iota(I32, (t, t), 0)
    i = lax.broadcasted_iota(I32, (t, t), 1)
    for didx in range(2):
        n = jnp.maximum(didx * t + i - j, 0)
        nf = jnp.maximum(n, 1).astype(F32)
        large = REL_MAX_EXACT + (jnp.log(nf / REL_MAX_EXACT) / math.log(REL_MAX_DIST / REL_MAX_EXACT)
                                 * (REL_BUCKETS - REL_MAX_EXACT)).astype(I32)
        large = jnp.minimum(large, REL_BUCKETS - 1)
        bucket = jnp.where(n < REL_MAX_EXACT, n, large)
        for h in range(A_HEADS):
            val = jnp.zeros((t, t), F32)
            for k in range(REL_BUCKETS):
                val = jnp.where(bucket == k, t5_ref[k, h], val)
            o_ref[didx, :, h * t:(h + 1) * t] = (val - t5_ref[REL_BUCKETS - 1, h]) * LOG2E


def _bias_tiles(t5_bias):
    t = DSA_TILE
    assert t + 1 >= FAR_DISTANCE
    return pl.pallas_call(
        _bias_kernel,
        out_shape=jax.ShapeDtypeStruct((2, t, A_HEADS * t), F32),
        in_specs=[pl.BlockSpec(memory_space=pltpu.SMEM)],
        out_specs=pl.BlockSpec(memory_space=pltpu.VMEM),
        compiler_params=pltpu.CompilerParams(vmem_limit_bytes=VMEM_LIMIT_BYTES),
        name="t5_bias_tiles",
    )(t5_bias)


def _dsa_kernel(qt_ref, qit_ref, ckv_ref, ckvt_ref, ki_ref, wit_ref, wuk_ref, wuvt_ref, bias_ref,
                o_ref, keys_ref, acc_ref, ot_ref, *, topk, seq_len):
    t = DSA_TILE
    qb = pl.program_id(1)
    nk = qb + 1
    row = lax.broadcasted_iota(I32, (t, t), 0)
    col = lax.broadcasted_iota(I32, (t, t), 1)

    def idx_body(kc, carry):
        kic = ki_ref[0, pl.ds(pl.multiple_of(kc * t, t), t), :]
        acc = jnp.zeros((t, t), F32)
        for h in range(IDX_HEADS):
            rel = _dot(kic, qit_ref[0, h * IDX_DIM:(h + 1) * IDX_DIM, :])
            acc = acc + jnp.maximum(rel, 0.0) * wit_ref[0, h:h + 1, :]
        acc = acc + 0.0
        bits = pltpu.bitcast(acc, I32)
        key = bits ^ ((bits >> 31) & 0x7FFFFFFF)
        causal = (kc * t + row) <= (qb * t + col)
        keys_ref[pl.ds(pl.multiple_of(kc * t, t), t), :] = jnp.where(causal, key, KEY_NEG_INF)
        return carry

    lax.fori_loop(0, nk, idx_body, 0)

    def count(pred):
        def body(kc, acc):
            k = keys_ref[pl.ds(pl.multiple_of(kc * t, t), t), :]
            m = jnp.where(pred(k, kc), 1, 0).astype(I32)
            return acc + jnp.sum(m.reshape(t // SUBLANES, SUBLANES, t), axis=0)

        acc = lax.fori_loop(0, nk, body, jnp.zeros((SUBLANES, t), I32))
        return jnp.sum(acc, axis=0, keepdims=True)

    def count_ge(cand):
        return count(lambda k, kc: k >= cand)

    thr = jnp.where(count_ge(jnp.zeros((1, t), I32)) >= topk, 0, INT_MIN).astype(I32)

    def bit_body(i, thr):
        cand = thr | (jnp.int32(1) << (30 - i))
        return jnp.where(count_ge(cand) >= topk, cand, thr)

    thr = lax.fori_loop(0, 31, bit_body, thr)
    thr = jnp.maximum(thr, KEY_NEG_INF + 1)

    @pl.when(jnp.max(count_ge(thr)) > topk)
    def _():
        need = topk - count(lambda k, kc: k > thr)
        nbits = (seq_len - 1).bit_length()

        def pos_bit(i, y):
            cand = y | (jnp.int32(1) << (nbits - 1 - i))
            before = count(lambda k, kc: (k == thr) & ((kc * t + row) < cand))
            return jnp.where(before < need, cand, y)

        last_kept = lax.fori_loop(0, nbits, pos_bit, jnp.zeros((1, t), I32))

        def demote(kc, carry):
            sl = pl.ds(pl.multiple_of(kc * t, t), t)
            k = keys_ref[sl, :]
            keys_ref[sl, :] = jnp.where((k == thr) & ((kc * t + row) > last_kept), k - 1, k)
            return carry

        lax.fori_loop(0, nk, demote, 0)

    scale = A_HEAD_DIM ** -0.5 * LOG2E

    hw = A_HEADS * t
    qlat = jnp.concatenate(
        [(_dot(wuk_ref[h], qt_ref[0, h * A_HEAD_DIM:(h + 1) * A_HEAD_DIM, :]) * scale).astype(BF16)
         for h in range(A_HEADS)], axis=1)
    acc_ref[...] = jnp.zeros_like(acc_ref)

    def att_body(kc, carry, *, near):
        m, l = carry
        sl = pl.ds(pl.multiple_of(kc * t, t), t)
        raw = _dot(ckv_ref[0, sl, :], qlat)
        sel = keys_ref[sl, :] >= thr
        parts = []
        for h in range(A_HEADS):
            lg_h = raw[:, h * t:(h + 1) * t]
            if near:
                lg_h = lg_h + bias_ref[qb - kc, :, h * t:(h + 1) * t]
            parts.append(jnp.where(sel, lg_h, MASKED_LOGIT))
        lg = jnp.concatenate(parts, axis=1)
        m_new = jnp.maximum(m, jnp.max(lg, axis=0, keepdims=True))
        alpha = jnp.exp2(m - m_new)
        p = jnp.exp2(lg - m_new)
        l = l * alpha + jnp.sum(p, axis=0, keepdims=True)
        acc_ref[...] = acc_ref[...] * alpha + _dot(ckvt_ref[0, kc], p.astype(BF16))
        return m_new, l

    n_far = jnp.maximum(qb - 1, 0)
    carry = (jnp.full((1, hw), MASKED_LOGIT, F32), jnp.zeros((1, hw), F32))
    carry = lax.fori_loop(0, n_far, functools.partial(att_body, near=False), carry)
    _, l = lax.fori_loop(n_far, nk, functools.partial(att_body, near=True), carry)
    olat = (acc_ref[...] / l).astype(BF16)
    for h in range(A_HEADS):
        ot_ref[h * A_HEAD_DIM:(h + 1) * A_HEAD_DIM, :] = _dot(wuvt_ref[h], olat[:, h * t:(h + 1) * t])
    o_ref[0] = ot_ref[...].T.astype(BF16)


def _dsa(qt, qit, ckv, ckvt, ki, wit, wuk, wuvt, bias):
    b, _, s = qt.shape
    t = DSA_TILE
    nt = s // t
    topk = min(TOPK_MAX, s // 4)
    return pl.pallas_call(
        functools.partial(_dsa_kernel, topk=topk, seq_len=s),
        out_shape=jax.ShapeDtypeStruct((b, s, A_WIDTH), BF16),
        grid=(b, nt),
        in_specs=[
            pl.BlockSpec((1, A_WIDTH, t), lambda bi, i: (bi, 0, i)),
            pl.BlockSpec((1, IDX_HEADS * IDX_DIM, t), lambda bi, i: (bi, 0, i)),
            pl.BlockSpec((1, s, KV_RANK), lambda bi, i: (bi, 0, 0)),
            pl.BlockSpec((1, nt, KV_RANK, t), lambda bi, i: (bi, 0, 0, 0)),
            pl.BlockSpec((1, s, IDX_DIM), lambda bi, i: (bi, 0, 0)),
            pl.BlockSpec((1, IDX_HEADS, t), lambda bi, i: (bi, 0, i)),
            pl.BlockSpec((A_HEADS, KV_RANK, A_HEAD_DIM), lambda bi, i: (0, 0, 0)),
            pl.BlockSpec((A_HEADS, A_HEAD_DIM, KV_RANK), lambda bi, i: (0, 0, 0)),
            pl.BlockSpec((2, t, A_HEADS * t), lambda bi, i: (0, 0, 0)),
        ],
        out_specs=pl.BlockSpec((1, t, A_WIDTH), lambda bi, i: (bi, i, 0)),
        scratch_shapes=[pltpu.VMEM((s, t), I32), pltpu.VMEM((KV_RANK, A_HEADS * t), F32), pltpu.VMEM((A_WIDTH, t), F32)],
        compiler_params=_params("parallel", "arbitrary"),
        name="dsa",
    )(qt, qit, ckv, ckvt, ki, wit, wuk, wuvt, bias)


def _split3(x):
    hi = x.astype(BF16)
    r1 = x - hi.astype(F32)
    mid = r1.astype(BF16)
    lo = (r1 - mid.astype(F32)).astype(BF16)
    return hi, mid, lo


def _rwkv_kernel(r_ref, k_ref, v_ref, gl_ref, wl_ref, al_ref,
                 mur_ref, muk_ref, muv_ref, mug_ref, muw_ref, mua_ref,
                 w0_ref, a0_ref, kk_ref, ka_ref, rk_ref, lnw_ref, lnb_ref,
                 w2_ref, a2_ref, g2_ref,
                 o_ref,
                 pr_ref, pk_ref, pv_ref, pg_ref, pw_ref, pa_ref, state_ref,
                 abar_ref, rbar_ref, bt_ref, kt_ref, bh_ref, kh_ref, vv_ref, pc_ref, rkv_ref, gg_ref):
    tl = RWKV_TILE
    c = RWKV_CHUNK
    nc = tl // c
    npair = R_WIDTH // PAIR

    @pl.when(pl.program_id(1) == 0)
    def _():
        state_ref[...] = jnp.zeros_like(state_ref)
        for ref in (pr_ref, pk_ref, pv_ref, pg_ref, pw_ref, pa_ref):
            ref[...] = jnp.zeros_like(ref)

    def shift_mix(x_ref, prev_ref, mu_ref):
        x = x_ref[0]
        first = lax.broadcasted_iota(I32, x.shape, 0) == 0
        xprev = jnp.where(first, prev_ref[...], pltpu.roll(x, 1, axis=0))
        prev_ref[...] = x[tl - 1:tl, :]
        return x + (xprev - x) * mu_ref[...]

    r = shift_mix(r_ref, pr_ref, mur_ref)
    k = shift_mix(k_ref, pk_ref, muk_ref)
    v = shift_mix(v_ref, pv_ref, muv_ref)
    gl = shift_mix(gl_ref, pg_ref, mug_ref)
    wl = shift_mix(wl_ref, pw_ref, muw_ref)
    al = shift_mix(al_ref, pa_ref, mua_ref)

    w_lin = w0_ref[...] + _dot(jnp.tanh(wl).astype(BF16), w2_ref[...])
    nx = -w_lin
    softplus = jnp.maximum(nx, 0.0) + jnp.log(1.0 + jnp.exp(-jnp.abs(nx)))
    ld = -jnp.exp(-softplus - 0.5)
    a = jax.nn.sigmoid(a0_ref[...] + _dot(al.astype(BF16), a2_ref[...]))
    gg_ref[...] = _dot(jax.nn.sigmoid(gl).astype(BF16), g2_ref[...])

    li = lax.broadcasted_iota(I32, (LANES, LANES), 0) // R_HEAD_DIM
    lj = lax.broadcasted_iota(I32, (LANES, LANES), 1) // R_HEAD_DIM
    head_ones = jnp.where(li == lj, 1.0, 0.0).astype(BF16)

    def head_sum(x):
        parts = []
        for j in range(R_WIDTH // LANES):
            hi, mid, lo = _split3(x[:, j * LANES:(j + 1) * LANES])
            parts.append(_dot(hi, head_ones) + _dot(mid, head_ones) + _dot(lo, head_ones))
        return jnp.concatenate(parts, axis=-1)

    kk = k * kk_ref[...]
    kk = kk / jnp.maximum(jnp.sqrt(head_sum(kk * kk)), 1e-12)
    k2 = k * (1.0 + (a - 1.0) * ka_ref[...])
    bb = kk * a

    ti = lax.broadcasted_iota(I32, (tl, tl), 0)
    tj = lax.broadcasted_iota(I32, (tl, tl), 1)
    tri = jnp.where((ti // c == tj // c) & (tj <= ti), 1.0, 0.0).astype(BF16)
    hi, mid, lo = _split3(ld)
    cum = _dot(tri, hi) + _dot(tri, mid) + _dot(tri, lo)
    cum_end = jnp.broadcast_to(cum.reshape(nc, c, R_WIDTH)[:, c - 1:c, :], (nc, c, R_WIDTH)).reshape(tl, R_WIDTH)
    e_in = jnp.exp(cum)
    e_neg = jnp.exp(-cum)
    abar_ref[...] = -kk * jnp.exp(cum - ld)
    rbar_ref[...] = r * e_in
    bt_ref[...] = bb * e_neg
    kt_ref[...] = k2 * e_neg
    e_out = jnp.exp(cum_end - cum)
    bh_ref[...] = bb * e_out
    kh_ref[...] = k2 * e_out
    vv_ref[...] = v
    pc_ref[...] = jnp.exp(cum_end)
    rkv_ref[...] = r * k2 * rk_ref[...]

    lane_head = lax.broadcasted_iota(I32, (2 * c, PAIR), 1) // R_HEAD_DIM
    row_head = lax.broadcasted_iota(I32, (2 * c, PAIR), 0) // c
    same = lane_head == row_head
    rt = lax.broadcasted_iota(I32, (2 * c, PAIR), 0) % c
    ct = lax.broadcasted_iota(I32, (2 * c, PAIR), 1) % c
    strict = same & (ct < rt)
    incl = same & (ct <= rt)
    eye = jnp.where(same & (ct == rt), 1.0, 0.0).astype(F32)

    def bd(x):
        return jnp.where(same, jnp.concatenate([x, x], axis=1), 0.0)

    lane0 = lax.broadcasted_iota(I32, (c, PAIR), 1) < R_HEAD_DIM

    def pairs(ref, rows):
        return jnp.stack([ref[rows, p * PAIR:(p + 1) * PAIR] for p in range(npair)], axis=0)

    lnw = jnp.stack([lnw_ref[:, p * PAIR:(p + 1) * PAIR] for p in range(npair)], axis=0)
    lnb = jnp.stack([lnb_ref[:, p * PAIR:(p + 1) * PAIR] for p in range(npair)], axis=0)

    def chunk_body(ci, carry):
        rows = pl.ds(pl.multiple_of(ci * c, c), c)
        abar = bd(pairs(abar_ref, rows)).astype(BF16)
        rbar = bd(pairs(rbar_ref, rows)).astype(BF16)
        bt = bd(pairs(bt_ref, rows)).astype(BF16)
        kt = bd(pairs(kt_ref, rows)).astype(BF16)
        v_pl = pairs(vv_ref, rows)
        vbd = bd(v_pl)
        aa = _bdot_nt(jnp.concatenate([abar, rbar], axis=1), jnp.concatenate([bt, kt], axis=1))
        a_ab = jnp.where(strict, aa[:, :2 * c, :PAIR], 0.0)
        a_ak = jnp.where(strict, aa[:, :2 * c, PAIR:], 0.0).astype(BF16)
        a_rb = jnp.where(incl, aa[:, 2 * c:, :PAIR], 0.0).astype(BF16)
        a_rk = jnp.where(incl, aa[:, 2 * c:, PAIR:], 0.0).astype(BF16)
        pw = a_ab.astype(BF16)
        tinv = eye + a_ab
        for _ in range(int(math.log2(c)) - 1):
            pw = _bdot(pw, pw).astype(BF16)
            tinv = tinv + _bdot(tinv.astype(BF16), pw)
        akv = _bdot(a_ak, vbd.astype(BF16))
        wu = _bdot(tinv.astype(BF16), jnp.concatenate([abar, akv.astype(BF16)], axis=2))
        s_bd = state_ref[...]
        wr = _bdot_nt(jnp.concatenate([wu[:, :, :PAIR].astype(BF16), rbar], axis=1), s_bd.astype(BF16))
        e_bd = wr[:, :2 * c] + wu[:, :, PAIR:]
        ev = jnp.concatenate([e_bd, vbd], axis=1)
        y_bd = wr[:, 2 * c:] + _bdot(jnp.concatenate([a_rb, a_rk], axis=2), ev.astype(BF16))
        bk = jnp.concatenate([bd(pairs(bh_ref, rows)), bd(pairs(kh_ref, rows))], axis=1).astype(BF16)
        ds = _bdot(jnp.swapaxes(ev, 1, 2).astype(BF16), bk)
        state_ref[...] = s_bd * pairs(pc_ref, rows)[:, 0:1, :] + jnp.where(same, ds, 0.0)
        mean = jnp.sum(y_bd, axis=-1, keepdims=True) * (1.0 / R_HEAD_DIM)
        dev = jnp.where(same, y_bd - mean, 0.0)
        var = jnp.sum(dev * dev, axis=-1, keepdims=True) * (1.0 / R_HEAD_DIM)
        yn = dev * lax.rsqrt(var + GN_EPS)
        yn = yn[:, :c] + yn[:, c:]
        rkv = pairs(rkv_ref, rows)
        s0 = jnp.sum(jnp.where(lane0, rkv, 0.0), axis=-1, keepdims=True)
        s1 = jnp.sum(jnp.where(lane0, 0.0, rkv), axis=-1, keepdims=True)
        bonus = jnp.where(lane0, s0, s1) * v_pl
        out = ((yn * lnw + lnb + bonus) * pairs(gg_ref, rows)).astype(BF16)
        for p in range(npair):
            o_ref[0, rows, p * PAIR:(p + 1) * PAIR] = out[p]
        return carry

    lax.fori_loop(0, nc, chunk_body, 0)


def _rwkv(z, prm):
    b, s, _ = z.shape
    tl = RWKV_TILE

    def seg(width, off):
        return pl.BlockSpec((1, tl, width), lambda bi, i: (bi, i, off // width))

    def full(arr):
        return pl.BlockSpec(arr.shape, lambda bi, i: (0,) * arr.ndim)

    names = ("mu_r", "mu_k", "mu_v", "mu_g", "mu_w", "mu_a", "w0", "a0", "k_k", "k_a", "r_k", "ln_w", "ln_b",
             "w2", "a2", "g2")
    consts = [prm[n] for n in names]
    wide = pltpu.VMEM((tl, R_WIDTH), F32)
    return pl.pallas_call(
        _rwkv_kernel,
        out_shape=jax.ShapeDtypeStruct((b, s, R_WIDTH), BF16),
        grid=(b, s // tl),
        in_specs=[seg(R_WIDTH, Z_R), seg(R_WIDTH, Z_K), seg(R_WIDTH, Z_V), seg(GATE_LORA, Z_GL),
                  seg(LORA_PAD, Z_WL), seg(LORA_PAD, Z_AL)] + [full(x) for x in consts],
        out_specs=pl.BlockSpec((1, tl, R_WIDTH), lambda bi, i: (bi, i, 0)),
        scratch_shapes=[
            pltpu.VMEM((1, R_WIDTH), F32), pltpu.VMEM((1, R_WIDTH), F32), pltpu.VMEM((1, R_WIDTH), F32),
            pltpu.VMEM((1, GATE_LORA), F32), pltpu.VMEM((1, LORA_PAD), F32), pltpu.VMEM((1, LORA_PAD), F32),
            pltpu.VMEM((R_WIDTH // PAIR, PAIR, PAIR), F32),
            wide, wide, wide, wide, wide, wide, wide, wide, wide, wide,
        ],
        compiler_params=_params("parallel", "arbitrary"),
        name="rwkv7",
    )(z, z, z, z, z, z, *consts)


def _pad_cols(w, width):
    return jnp.pad(w, ((0, 0), (0, width - w.shape[1])))


def _layout_w_in(w):
    offs = [0]
    for n in (A_WIDTH, KV_RANK, IDX_HEADS * IDX_DIM, IDX_DIM, IDX_HEADS,
              R_WIDTH, R_WIDTH, R_WIDTH, DECAY_LORA, AAA_LORA, GATE_LORA):
        offs.append(offs[-1] + n)
    q, ckv, qi, ki, wi, r, k, v, wl, al, gl = (w[:, offs[i]:offs[i + 1]] for i in range(11))
    cols = [q, qi, r, k, v, gl, ckv, _pad_cols(jnp.concatenate([ki, wi], axis=1), LORA_PAD),
            _pad_cols(wl, LORA_PAD), _pad_cols(al, LORA_PAD)]
    return _pad_cols(jnp.concatenate(cols, axis=1), Z_WIDTH).astype(BF16)


def _pad_rows(w, rows):
    return jnp.pad(w, ((0, rows - w.shape[0]), (0, 0)))


def _rwkv_params(l, mu, w0, w2, a0, a2, g2, k_k, k_a, r_k, ln_w, ln_b):
    m = mu[l]
    o = [0, R_WIDTH, 2 * R_WIDTH, 3 * R_WIDTH, 3 * R_WIDTH + DECAY_LORA, 3 * R_WIDTH + DECAY_LORA + AAA_LORA]
    row = lambda x: x.reshape(1, -1)
    return {
        "mu_r": row(m[o[0]:o[1]]), "mu_k": row(m[o[1]:o[2]]), "mu_v": row(m[o[2]:o[3]]),
        "mu_w": _pad_cols(row(m[o[3]:o[4]]), LORA_PAD), "mu_a": _pad_cols(row(m[o[4]:o[5]]), LORA_PAD),
        "mu_g": row(m[o[5]:]),
        "w0": row(w0[l]), "a0": row(a0[l]), "k_k": row(k_k[l]), "k_a": row(k_a[l]), "r_k": row(r_k[l]),
        "ln_w": row(ln_w[l]), "ln_b": row(ln_b[l]),
        "w2": _pad_rows(w2[l], LORA_PAD).astype(BF16), "a2": _pad_rows(a2[l], LORA_PAD).astype(BF16),
        "g2": g2[l].astype(BF16),
    }


def kernel(x, c, t5_bias, ada_w, ada_b, norm_g, ffn_w_in, ffn_w_out, w_in, ckv_norm_g, w_uk, w_uv, rwkv_mu, rwkv_w0, rwkv_w2, rwkv_a0, rwkv_a2, rwkv_g2, rwkv_k_k, rwkv_k_a, rwkv_r_k, rwkv_ln_w, rwkv_ln_b, w_out, final_norm_g):
    b, s, d = x.shape
    depth = ada_w.shape[0]
    assert s % DSA_TILE == 0 and s % RWKV_TILE == 0 and d == A_WIDTH + R_WIDTH
    mod = _ada_mod(c, ada_w, ada_b).reshape(depth, b, N_SUB, 3, 1, d)
    bias = _bias_tiles(t5_bias)
    final_g = final_norm_g.reshape(1, d)
    h = x
    for l in range(depth):
        shift = lambda i: mod[l, :, i, 0]
        scale = lambda i: mod[l, :, i, 1]
        gate = lambda i: mod[l, :, i, 2]
        g = lambda i: norm_g[l, i].reshape(1, d)
        h = _ffn(h, g(0), shift(0), scale(0), gate(0), ffn_w_in[l, 0].astype(BF16), ffn_w_out[l, 0].astype(BF16),
                 final_g, final_norm=False)
        z = _proj(h, g(1), shift(1), scale(1), _layout_w_in(w_in[l]))
        qt, qit, ckv, ckvt, ki, wit = _dsa_prep(z, ckv_norm_g[l].reshape(1, KV_RANK))
        wuk = jnp.transpose(w_uk[l], (1, 0, 2)).astype(BF16)
        wuvt = jnp.transpose(w_uv[l], (1, 2, 0)).astype(BF16)
        o_a = _dsa(qt, qit, ckv, ckvt, ki, wit, wuk, wuvt, bias)
        o_r = _rwkv(z, _rwkv_params(l, rwkv_mu, rwkv_w0, rwkv_w2, rwkv_a0, rwkv_a2, rwkv_g2, rwkv_k_k,
                                    rwkv_k_a, rwkv_r_k, rwkv_ln_w, rwkv_ln_b))
        h = _outproj(o_a, o_r, h, gate(1), w_out[l].astype(BF16))
        h = _ffn(h, g(2), shift(2), scale(2), gate(2), ffn_w_in[l, 1].astype(BF16), ffn_w_out[l, 1].astype(BF16),
                 final_g, final_norm=(l == depth - 1))
    return h
```

```python
import functools
import math

import jax
import jax.numpy as jnp
from jax import lax
from jax.experimental import pallas as pl
from jax.experimental.pallas import tpu as pltpu

F32 = jnp.float32
BF16 = jnp.bfloat16
I32 = jnp.int32
I16 = jnp.int16

A_HEADS = 8
A_HEAD_DIM = 128
A_WIDTH = A_HEADS * A_HEAD_DIM
KV_RANK = 256
IDX_HEADS = 16
IDX_DIM = 64
TOPK_MAX = 256
REL_BUCKETS = 32
REL_MAX_EXACT = REL_BUCKETS // 2
REL_MAX_DIST = 128
R_HEAD_DIM = 64
R_WIDTH = 1024
R_HEADS = R_WIDTH // R_HEAD_DIM
DECAY_LORA = 96
AAA_LORA = 96
GATE_LORA = 256
GN_EPS = 64e-5
RMS_EPS = 1e-6
N_SUB = 3

LANES = 128
SUBLANES = 8
VMEM_LIMIT_BYTES = 56 * 1024 * 1024

LORA_PAD = 128
Z_Q = 0
Z_QI = Z_Q + A_WIDTH
Z_R = Z_QI + IDX_HEADS * IDX_DIM
Z_K = Z_R + R_WIDTH
Z_V = Z_K + R_WIDTH
Z_GL = Z_V + R_WIDTH
Z_CKV = Z_GL + GATE_LORA
Z_KIWI = Z_CKV + KV_RANK
Z_WL = Z_KIWI + LORA_PAD
Z_AL = Z_WL + LORA_PAD
Z_END = Z_AL + LORA_PAD
Z_WIDTH = 6144

FFN_TOKEN_TILE = 1024
FFN_HIDDEN_TILE = 256
DSA_TILE = 256
RWKV_CHUNK = 64
RWKV_TILE = 256
PAIR = 2 * R_HEAD_DIM

I16_MIN = -(2 ** 15)
PACKED_ROWS = 2 * SUBLANES
BF16_EXACT_INT = 256
KEY_NEG_INF = -2139095041
MASKED_LOGIT = -1e30
LOG2E = math.log2(math.e)
FAR_DISTANCE = math.ceil(REL_MAX_EXACT * (REL_MAX_DIST / REL_MAX_EXACT)
                         ** ((REL_BUCKETS - 1 - REL_MAX_EXACT) / (REL_BUCKETS - REL_MAX_EXACT)))


def _dot(a, b):
    return jnp.dot(a, b, preferred_element_type=F32)


def _dot_nt(a, b):
    return lax.dot_general(a, b, (((1,), (1,)), ((), ())), preferred_element_type=F32)


def _bdot(a, b):
    return lax.dot_general(a, b, (((2,), (1,)), ((0,), (0,))), preferred_element_type=F32)


def _bdot_nt(a, b):
    return lax.dot_general(a, b, (((2,), (2,)), ((0,), (0,))), preferred_element_type=F32)


def _rms(x, g, eps):
    ms = jnp.mean(x * x, axis=-1, keepdims=True)
    return x * lax.rsqrt(ms + eps) * g


def _divisor_tile(n, pref):
    if n <= pref:
        return n
    t = (pref // LANES) * LANES
    while t > LANES and n % t:
        t -= LANES
    assert n % t == 0, (n, pref)
    return t


def _params(*sem):
    return pltpu.CompilerParams(dimension_semantics=sem, vmem_limit_bytes=VMEM_LIMIT_BYTES)


def _ada_kernel(c_ref, w_ref, b_ref, o_ref):
    c = c_ref[...]
    ca = (c * jax.nn.sigmoid(c)).astype(BF16)
    o_ref[0] = _dot(ca, w_ref[0].astype(BF16)) + b_ref[0]


def _ada_mod(c, ada_w, ada_b):
    depth, d, n = ada_w.shape
    b = c.shape[0]
    bp = -(-b // SUBLANES) * SUBLANES
    cp = jnp.pad(c, ((0, bp - b), (0, 0)))
    tn = _divisor_tile(n, 1024)
    out = pl.pallas_call(
        _ada_kernel,
        out_shape=jax.ShapeDtypeStruct((depth, bp, n), F32),
        grid=(depth, n // tn),
        in_specs=[
            pl.BlockSpec((bp, d), lambda l, j: (0, 0)),
            pl.BlockSpec((1, d, tn), lambda l, j: (l, 0, j)),
            pl.BlockSpec((1, 1, tn), lambda l, j: (l, 0, j)),
        ],
        out_specs=pl.BlockSpec((1, bp, tn), lambda l, j: (l, 0, j)),
        compiler_params=_params("arbitrary", "arbitrary"),
        name="ada_mod",
    )(cp, ada_w, ada_b.reshape(depth, 1, n))
    return out[:, :b]


def _ffn_kernel(h_ref, g_ref, shift_ref, scale_ref, gate_ref, wg_ref, wu_ref, wo_ref, fg_ref,
                o_ref, hn_ref, *, final_norm):
    f = pl.program_id(2)

    @pl.when(f == 0)
    def _():
        h = h_ref[0]
        y = _rms(h, g_ref[...], RMS_EPS)
        hn_ref[...] = (y * (1.0 + scale_ref[0]) + shift_ref[0]).astype(BF16)
        o_ref[0] = h

    hn = hn_ref[...]
    g = _dot(hn, wg_ref[...])
    u = _dot(hn, wu_ref[...])
    act = (g * jax.nn.sigmoid(g) * u).astype(BF16)
    o_ref[0] += (0.5 * gate_ref[0]) * _dot(act, wo_ref[...])

    if final_norm:
        @pl.when(f == pl.num_programs(2) - 1)
        def _():
            o_ref[0] = _rms(o_ref[0], fg_ref[...], RMS_EPS)


def _ffn(h, g, shift, scale, gate, w_in, w_out, final_g, *, final_norm):
    b, s, d = h.shape
    ff = w_out.shape[0]
    tm = _divisor_tile(s, FFN_TOKEN_TILE)
    tf = _divisor_tile(ff, FFN_HIDDEN_TILE)
    nf = ff // tf
    vec = pl.BlockSpec((1, 1, d), lambda bi, i, f: (bi, 0, 0))
    row = pl.BlockSpec((1, d), lambda bi, i, f: (0, 0))
    tile = pl.BlockSpec((1, tm, d), lambda bi, i, f: (bi, i, 0))
    return pl.pallas_call(
        functools.partial(_ffn_kernel, final_norm=final_norm),
        out_shape=jax.ShapeDtypeStruct((b, s, d), F32),
        grid=(b, s // tm, nf),
        in_specs=[
            tile, row, vec, vec, vec,
            pl.BlockSpec((d, tf), lambda bi, i, f: (0, f)),
            pl.BlockSpec((d, tf), lambda bi, i, f: (0, nf + f)),
            pl.BlockSpec((tf, d), lambda bi, i, f: (f, 0)),
            row,
        ],
        out_specs=tile,
        scratch_shapes=[pltpu.VMEM((tm, d), BF16)],
        compiler_params=_params("parallel", "parallel", "arbitrary"),
        name="ffn",
    )(h, g, shift, scale, gate, w_in, w_in, w_out, final_g)


def _proj_kernel(h_ref, g_ref, shift_ref, scale_ref, w_ref, o_ref, hn_ref):
    @pl.when(pl.program_id(2) == 0)
    def _():
        y = _rms(h_ref[0], g_ref[...], RMS_EPS)
        hn_ref[...] = (y * (1.0 + scale_ref[0]) + shift_ref[0]).astype(BF16)

    o_ref[0] = _dot(hn_ref[...], w_ref[...])


def _proj(h, g, shift, scale, w):
    b, s, d = h.shape
    p = w.shape[1]
    tm = _divisor_tile(s, 1024)
    tn = _divisor_tile(p, 1024)
    vec = pl.BlockSpec((1, 1, d), lambda bi, i, n: (bi, 0, 0))
    return pl.pallas_call(
        _proj_kernel,
        out_shape=jax.ShapeDtypeStruct((b, s, p), F32),
        grid=(b, s // tm, p // tn),
        in_specs=[
            pl.BlockSpec((1, tm, d), lambda bi, i, n: (bi, i, 0)),
            pl.BlockSpec((1, d), lambda bi, i, n: (0, 0)),
            vec, vec,
            pl.BlockSpec((d, tn), lambda bi, i, n: (0, n)),
        ],
        out_specs=pl.BlockSpec((1, tm, tn), lambda bi, i, n: (bi, i, n)),
        scratch_shapes=[pltpu.VMEM((tm, d), BF16)],
        compiler_params=_params("parallel", "parallel", "arbitrary"),
        name="proj",
    )(h, g, shift, scale, w)


def _outproj_kernel(oa_ref, or_ref, h_ref, gate_ref, wa_ref, wr_ref, o_ref):
    acc = _dot(oa_ref[0], wa_ref[...]) + _dot(or_ref[0], wr_ref[...])
    o_ref[0] = h_ref[0] + gate_ref[0] * acc


def _outproj(o_a, o_r, h, gate, w_out):
    b, s, d = h.shape
    tm = _divisor_tile(s, 1024)
    tn = _divisor_tile(d, 1024)
    wa = o_a.shape[-1]
    wr = o_r.shape[-1]
    return pl.pallas_call(
        _outproj_kernel,
        out_shape=jax.ShapeDtypeStruct((b, s, d), F32),
        grid=(b, s // tm, d // tn),
        in_specs=[
            pl.BlockSpec((1, tm, wa), lambda bi, i, n: (bi, i, 0)),
            pl.BlockSpec((1, tm, wr), lambda bi, i, n: (bi, i, 0)),
            pl.BlockSpec((1, tm, tn), lambda bi, i, n: (bi, i, n)),
            pl.BlockSpec((1, 1, tn), lambda bi, i, n: (bi, 0, n)),
            pl.BlockSpec((wa, tn), lambda bi, i, n: (0, n)),
            pl.BlockSpec((wr, tn), lambda bi, i, n: (wa // wr, n)),
        ],
        out_specs=pl.BlockSpec((1, tm, tn), lambda bi, i, n: (bi, i, n)),
        compiler_params=_params("parallel", "parallel", "arbitrary"),
        name="outproj",
    )(o_a, o_r, h, gate, w_out, w_out)


def _dsa_prep_kernel(q_ref, qi_ref, ckv_ref, kiwi_ref, g_ref, qt_ref, qit_ref, ckv_o, ckvt_o, ki_o, wit_o):
    qt_ref[0] = q_ref[0].T.astype(BF16)
    qit_ref[0] = qi_ref[0].T.astype(BF16)
    cn = _rms(ckv_ref[0], g_ref[...], RMS_EPS)
    ckv_o[0] = cn.astype(BF16)
    ckvt_o[0, 0] = cn.T.astype(BF16)
    kw = kiwi_ref[0]
    ki_o[0] = kw[:, :IDX_DIM].astype(BF16)
    wit_o[0] = kw.T[IDX_DIM:IDX_DIM + IDX_HEADS, :] * (IDX_HEADS * IDX_DIM) ** -0.5


def _dsa_prep(z, ckv_g):
    b, s, _ = z.shape
    t = DSA_TILE
    nt = s // t

    def seg(width, off):
        return pl.BlockSpec((1, t, width), lambda bi, i: (bi, i, off // width))

    return pl.pallas_call(
        _dsa_prep_kernel,
        out_shape=(
            jax.ShapeDtypeStruct((b, A_WIDTH, s), BF16),
            jax.ShapeDtypeStruct((b, IDX_HEADS * IDX_DIM, s), BF16),
            jax.ShapeDtypeStruct((b, s, KV_RANK), BF16),
            jax.ShapeDtypeStruct((b, nt, KV_RANK, t), BF16),
            jax.ShapeDtypeStruct((b, s, IDX_DIM), BF16),
            jax.ShapeDtypeStruct((b, IDX_HEADS, s), F32),
        ),
        grid=(b, nt),
        in_specs=[
            seg(A_WIDTH, Z_Q), seg(IDX_HEADS * IDX_DIM, Z_QI), seg(KV_RANK, Z_CKV), seg(LORA_PAD, Z_KIWI),
            pl.BlockSpec((1, KV_RANK), lambda bi, i: (0, 0)),
        ],
        out_specs=(
            pl.BlockSpec((1, A_WIDTH, t), lambda bi, i: (bi, 0, i)),
            pl.BlockSpec((1, IDX_HEADS * IDX_DIM, t), lambda bi, i: (bi, 0, i)),
            pl.BlockSpec((1, t, KV_RANK), lambda bi, i: (bi, i, 0)),
            pl.BlockSpec((1, 1, KV_RANK, t), lambda bi, i: (bi, i, 0, 0)),
            pl.BlockSpec((1, t, IDX_DIM), lambda bi, i: (bi, i, 0)),
            pl.BlockSpec((1, IDX_HEADS, t), lambda bi, i: (bi, 0, i)),
        ),
        compiler_params=_params("parallel", "parallel"),
        name="dsa_prep",
    )(z, z, z, z, ckv_g)


def _bias_kernel(t5_ref, o_ref):
    t = DSA_TILE
    j = lax.broadcasted_iota(I32, (t, t), 0)
    i = lax.broadcasted_iota(I32, (t, t), 1)
    for didx in range(2):
        n = jnp.maximum(didx * t + i - j, 0)
        nf = jnp.maximum(n, 1).astype(F32)
        large = REL_MAX_EXACT + (jnp.log(nf / REL_MAX_EXACT) / math.log(REL_MAX_DIST / REL_MAX_EXACT)
                                 * (REL_BUCKETS - REL_MAX_EXACT)).astype(I32)
        large = jnp.minimum(large, REL_BUCKETS - 1)
        bucket = jnp.where(n < REL_MAX_EXACT, n, large)
        for h in range(A_HEADS):
            val = jnp.zeros((t, t), F32)
            for k in range(REL_BUCKETS):
                val = jnp.where(bucket == k, t5_ref[k, h], val)
            o_ref[didx, :, h * t:(h + 1) * t] = (val - t5_ref[REL_BUCKETS - 1, h]) * LOG2E


def _bias_tiles(t5_bias):
    t = DSA_TILE
    assert t + 1 >= FAR_DISTANCE
    return pl.pallas_call(
        _bias_kernel,
        out_shape=jax.ShapeDtypeStruct((2, t, A_HEADS * t), F32),
        in_specs=[pl.BlockSpec(memory_space=pltpu.SMEM)],
        out_specs=pl.BlockSpec(memory_space=pltpu.VMEM),
        compiler_params=pltpu.CompilerParams(vmem_limit_bytes=VMEM_LIMIT_BYTES),
        name="t5_bias_tiles",
    )(t5_bias)


def _dsa_kernel(qt_ref, qit_ref, ckv_ref, ckvt_ref, ki_ref, wit_ref, wuk_ref, wuvt_ref, bias_ref,
                o_ref, keys_ref, khi_ref, klo_ref, acc_ref, ot_ref, *, topk, seq_len):
    t = DSA_TILE
    qb = pl.program_id(1)
    nk = qb + 1
    row = lax.broadcasted_iota(I32, (t, t), 0)
    col = lax.broadcasted_iota(I32, (t, t), 1)

    def idx_body(kc, carry):
        kic = ki_ref[0, pl.ds(pl.multiple_of(kc * t, t), t), :]
        acc = jnp.zeros((t, t), F32)
        for h in range(IDX_HEADS):
            rel = _dot(kic, qit_ref[0, h * IDX_DIM:(h + 1) * IDX_DIM, :])
            acc = acc + jnp.maximum(rel, 0.0) * wit_ref[0, h:h + 1, :]
        acc = acc + 0.0
        bits = pltpu.bitcast(acc, I32)
        key = bits ^ ((bits >> 31) & 0x7FFFFFFF)
        causal = (kc * t + row) <= (qb * t + col)
        key = jnp.where(causal, key, KEY_NEG_INF)
        sl = pl.ds(pl.multiple_of(kc * t, t), t)
        keys_ref[sl, :] = key
        khi_ref[sl, :] = (key >> 16).astype(I16)
        klo_ref[sl, :] = ((key & 0xFFFF) + I16_MIN).astype(I16)
        return carry

    lax.fori_loop(0, nk, idx_body, 0)

    def count16(ref, cand):
        cand16 = cand.astype(I16)

        def body(kc, acc):
            x = ref[pl.ds(pl.multiple_of(kc * t, t), t), :]
            ones = jnp.where(x >= cand16, jnp.asarray(1, BF16), jnp.asarray(0, BF16))
            ones = ones.reshape(t // PACKED_ROWS, PACKED_ROWS, t)
            for i in range(t // PACKED_ROWS):
                acc = acc + ones[i]
            return acc

        acc = lax.fori_loop(0, nk, body, jnp.zeros((PACKED_ROWS, t), BF16))
        return jnp.sum(acc.astype(F32), axis=0, keepdims=True).astype(I32)

    def search16(ref, base):
        def accept(cand, cur):
            return jnp.where(base + count16(ref, cand) >= topk, cand, cur)

        v = accept(jnp.zeros((1, t), I32), jnp.full((1, t), I16_MIN, I32))
        return lax.fori_loop(0, 15, lambda i, v: accept(v | (jnp.int32(1) << (14 - i)), v), v)

    hi = search16(khi_ref, 0)
    above = count16(khi_ref, jnp.minimum(hi + 1, -I16_MIN - 1))
    hi16 = hi.astype(I16)

    def keep_low(kc, carry):
        sl = pl.ds(pl.multiple_of(kc * t, t), t)
        klo_ref[sl, :] = jnp.where(khi_ref[sl, :] == hi16, klo_ref[sl, :], jnp.asarray(I16_MIN, I16))
        return carry

    lax.fori_loop(0, nk, keep_low, 0)
    lo = search16(klo_ref, above)
    thr = jnp.maximum((hi << 16) | (lo - I16_MIN), KEY_NEG_INF + 1)

    def count(pred):
        def body(kc, acc):
            k = keys_ref[pl.ds(pl.multiple_of(kc * t, t), t), :]
            m = jnp.where(pred(k, kc), 1, 0).astype(I32)
            return acc + jnp.sum(m.reshape(t // SUBLANES, SUBLANES, t), axis=0)

        acc = lax.fori_loop(0, nk, body, jnp.zeros((SUBLANES, t), I32))
        return jnp.sum(acc, axis=0, keepdims=True)

    @pl.when(jnp.max(count(lambda k, kc: k >= thr)) > topk)
    def _():
        need = topk - count(lambda k, kc: k > thr)
        nbits = (seq_len - 1).bit_length()

        def pos_bit(i, y):
            cand = y | (jnp.int32(1) << (nbits - 1 - i))
            before = count(lambda k, kc: (k == thr) & ((kc * t + row) < cand))
            return jnp.where(before < need, cand, y)

        last_kept = lax.fori_loop(0, nbits, pos_bit, jnp.zeros((1, t), I32))

        def demote(kc, carry):
            sl = pl.ds(pl.multiple_of(kc * t, t), t)
            k = keys_ref[sl, :]
            keys_ref[sl, :] = jnp.where((k == thr) & ((kc * t + row) > last_kept), k - 1, k)
            return carry

        lax.fori_loop(0, nk, demote, 0)

    scale = A_HEAD_DIM ** -0.5 * LOG2E

    hw = A_HEADS * t
    qlat = jnp.concatenate(
        [(_dot(wuk_ref[h], qt_ref[0, h * A_HEAD_DIM:(h + 1) * A_HEAD_DIM, :]) * scale).astype(BF16)
         for h in range(A_HEADS)], axis=1)
    acc_ref[...] = jnp.zeros_like(acc_ref)

    def att_body(kc, carry, *, near):
        m, l = carry
        sl = pl.ds(pl.multiple_of(kc * t, t), t)
        raw = _dot(ckv_ref[0, sl, :], qlat)
        sel = keys_ref[sl, :] >= thr
        parts = []
        for h in range(A_HEADS):
            lg_h = raw[:, h * t:(h + 1) * t]
            if near:
                lg_h = lg_h + bias_ref[qb - kc, :, h * t:(h + 1) * t]
            parts.append(jnp.where(sel, lg_h, MASKED_LOGIT))
        lg = jnp.concatenate(parts, axis=1)
        m_new = jnp.maximum(m, jnp.max(lg, axis=0, keepdims=True))
        alpha = jnp.exp2(m - m_new)
        p = jnp.exp2(lg - m_new)
        l = l * alpha + jnp.sum(p, axis=0, keepdims=True)
        acc_ref[...] = acc_ref[...] * alpha + _dot(ckvt_ref[0, kc], p.astype(BF16))
        return m_new, l

    n_far = jnp.maximum(qb - 1, 0)
    carry = (jnp.full((1, hw), MASKED_LOGIT, F32), jnp.zeros((1, hw), F32))
    carry = lax.fori_loop(0, n_far, functools.partial(att_body, near=False), carry)
    _, l = lax.fori_loop(n_far, nk, functools.partial(att_body, near=True), carry)
    olat = (acc_ref[...] / l).astype(BF16)
    for h in range(A_HEADS):
        ot_ref[h * A_HEAD_DIM:(h + 1) * A_HEAD_DIM, :] = _dot(wuvt_ref[h], olat[:, h * t:(h + 1) * t])
    o_ref[0] = ot_ref[...].T.astype(BF16)


def _dsa(qt, qit, ckv, ckvt, ki, wit, wuk, wuvt, bias):
    b, _, s = qt.shape
    t = DSA_TILE
    nt = s // t
    topk = min(TOPK_MAX, s // 4)
    assert s // PACKED_ROWS <= BF16_EXACT_INT
    return pl.pallas_call(
        functools.partial(_dsa_kernel, topk=topk, seq_len=s),
        out_shape=jax.ShapeDtypeStruct((b, s, A_WIDTH), BF16),
        grid=(b, nt),
        in_specs=[
            pl.BlockSpec((1, A_WIDTH, t), lambda bi, i: (bi, 0, i)),
            pl.BlockSpec((1, IDX_HEADS * IDX_DIM, t), lambda bi, i: (bi, 0, i)),
            pl.BlockSpec((1, s, KV_RANK), lambda bi, i: (bi, 0, 0)),
            pl.BlockSpec((1, nt, KV_RANK, t), lambda bi, i: (bi, 0, 0, 0)),
            pl.BlockSpec((1, s, IDX_DIM), lambda bi, i: (bi, 0, 0)),
            pl.BlockSpec((1, IDX_HEADS, t), lambda bi, i: (bi, 0, i)),
            pl.BlockSpec((A_HEADS, KV_RANK, A_HEAD_DIM), lambda bi, i: (0, 0, 0)),
            pl.BlockSpec((A_HEADS, A_HEAD_DIM, KV_RANK), lambda bi, i: (0, 0, 0)),
            pl.BlockSpec((2, t, A_HEADS * t), lambda bi, i: (0, 0, 0)),
        ],
        out_specs=pl.BlockSpec((1, t, A_WIDTH), lambda bi, i: (bi, i, 0)),
        scratch_shapes=[pltpu.VMEM((s, t), I32), pltpu.VMEM((s, t), I16), pltpu.VMEM((s, t), I16),
                        pltpu.VMEM((KV_RANK, A_HEADS * t), F32), pltpu.VMEM((A_WIDTH, t), F32)],
        compiler_params=_params("parallel", "arbitrary"),
        name="dsa",
    )(qt, qit, ckv, ckvt, ki, wit, wuk, wuvt, bias)


def _split3(x):
    hi = x.astype(BF16)
    r1 = x - hi.astype(F32)
    mid = r1.astype(BF16)
    lo = (r1 - mid.astype(F32)).astype(BF16)
    return hi, mid, lo


def _rwkv_kernel(r_ref, k_ref, v_ref, gl_ref, wl_ref, al_ref,
                 mur_ref, muk_ref, muv_ref, mug_ref, muw_ref, mua_ref,
                 w0_ref, a0_ref, kk_ref, ka_ref, rk_ref, lnw_ref, lnb_ref,
                 w2_ref, a2_ref, g2_ref,
                 o_ref,
                 pr_ref, pk_ref, pv_ref, pg_ref, pw_ref, pa_ref, state_ref,
                 abar_ref, rbar_ref, bt_ref, kt_ref, bh_ref, kh_ref, vv_ref, pc_ref, rkv_ref, gg_ref):
    tl = RWKV_TILE
    c = RWKV_CHUNK
    nc = tl // c
    npair = R_WIDTH // PAIR

    @pl.when(pl.program_id(1) == 0)
    def _():
        state_ref[...] = jnp.zeros_like(state_ref)
        for ref in (pr_ref, pk_ref, pv_ref, pg_ref, pw_ref, pa_ref):
            ref[...] = jnp.zeros_like(ref)

    def shift_mix(x_ref, prev_ref, mu_ref):
        x = x_ref[0]
        first = lax.broadcasted_iota(I32, x.shape, 0) == 0
        xprev = jnp.where(first, prev_ref[...], pltpu.roll(x, 1, axis=0))
        prev_ref[...] = x[tl - 1:tl, :]
        return x + (xprev - x) * mu_ref[...]

    r = shift_mix(r_ref, pr_ref, mur_ref)
    k = shift_mix(k_ref, pk_ref, muk_ref)
    v = shift_mix(v_ref, pv_ref, muv_ref)
    gl = shift_mix(gl_ref, pg_ref, mug_ref)
    wl = shift_mix(wl_ref, pw_ref, muw_ref)
    al = shift_mix(al_ref, pa_ref, mua_ref)

    w_lin = w0_ref[...] + _dot(jnp.tanh(wl).astype(BF16), w2_ref[...])
    nx = -w_lin
    softplus = jnp.maximum(nx, 0.0) + jnp.log(1.0 + jnp.exp(-jnp.abs(nx)))
    ld = -jnp.exp(-softplus - 0.5)
    a = jax.nn.sigmoid(a0_ref[...] + _dot(al.astype(BF16), a2_ref[...]))
    gg_ref[...] = _dot(jax.nn.sigmoid(gl).astype(BF16), g2_ref[...])

    li = lax.broadcasted_iota(I32, (LANES, LANES), 0) // R_HEAD_DIM
    lj = lax.broadcasted_iota(I32, (LANES, LANES), 1) // R_HEAD_DIM
    head_ones = jnp.where(li == lj, 1.0, 0.0).astype(BF16)

    def head_sum(x):
        parts = []
        for j in range(R_WIDTH // LANES):
            hi, mid, lo = _split3(x[:, j * LANES:(j + 1) * LANES])
            parts.append(_dot(hi, head_ones) + _dot(mid, head_ones) + _dot(lo, head_ones))
        return jnp.concatenate(parts, axis=-1)

    kk = k * kk_ref[...]
    kk = kk / jnp.maximum(jnp.sqrt(head_sum(kk * kk)), 1e-12)
    k2 = k * (1.0 + (a - 1.0) * ka_ref[...])
    bb = kk * a

    ti = lax.broadcasted_iota(I32, (tl, tl), 0)
    tj = lax.broadcasted_iota(I32, (tl, tl), 1)
    tri = jnp.where((ti // c == tj // c) & (tj <= ti), 1.0, 0.0).astype(BF16)
    hi, mid, lo = _split3(ld)
    cum = _dot(tri, hi) + _dot(tri, mid) + _dot(tri, lo)
    cum_end = jnp.broadcast_to(cum.reshape(nc, c, R_WIDTH)[:, c - 1:c, :], (nc, c, R_WIDTH)).reshape(tl, R_WIDTH)
    e_in = jnp.exp(cum)
    e_neg = jnp.exp(-cum)
    abar_ref[...] = -kk * jnp.exp(cum - ld)
    rbar_ref[...] = r * e_in
    bt_ref[...] = bb * e_neg
    kt_ref[...] = k2 * e_neg
    e_out = jnp.exp(cum_end - cum)
    bh_ref[...] = bb * e_out
    kh_ref[...] = k2 * e_out
    vv_ref[...] = v
    pc_ref[...] = jnp.exp(cum_end)
    rkv_ref[...] = r * k2 * rk_ref[...]

    lane_head = lax.broadcasted_iota(I32, (2 * c, PAIR), 1) // R_HEAD_DIM
    row_head = lax.broadcasted_iota(I32, (2 * c, PAIR), 0) // c
    same = lane_head == row_head
    rt = lax.broadcasted_iota(I32, (2 * c, PAIR), 0) % c
    ct = lax.broadcasted_iota(I32, (2 * c, PAIR), 1) % c
    strict = same & (ct < rt)
    incl = same & (ct <= rt)
    eye = jnp.where(same & (ct == rt), 1.0, 0.0).astype(F32)

    def bd(x):
        return jnp.where(same, jnp.concatenate([x, x], axis=1), 0.0)

    lane0 = lax.broadcasted_iota(I32, (c, PAIR), 1) < R_HEAD_DIM

    def pairs(ref, rows):
        return jnp.stack([ref[rows, p * PAIR:(p + 1) * PAIR] for p in range(npair)], axis=0)

    lnw = jnp.stack([lnw_ref[:, p * PAIR:(p + 1) * PAIR] for p in range(npair)], axis=0)
    lnb = jnp.stack([lnb_ref[:, p * PAIR:(p + 1) * PAIR] for p in range(npair)], axis=0)

    def chunk_body(ci, carry):
        rows = pl.ds(pl.multiple_of(ci * c, c), c)
        abar = bd(pairs(abar_ref, rows)).astype(BF16)
        rbar = bd(pairs(rbar_ref, rows)).astype(BF16)
        bt = bd(pairs(bt_ref, rows)).astype(BF16)
        kt = bd(pairs(kt_ref, rows)).astype(BF16)
        v_pl = pairs(vv_ref, rows)
        vbd = bd(v_pl)
        aa = _bdot_nt(jnp.concatenate([abar, rbar], axis=1), jnp.concatenate([bt, kt], axis=1))
        a_ab = jnp.where(strict, aa[:, :2 * c, :PAIR], 0.0)
        a_ak = jnp.where(strict, aa[:, :2 * c, PAIR:], 0.0).astype(BF16)
        a_rb = jnp.where(incl, aa[:, 2 * c:, :PAIR], 0.0).astype(BF16)
        a_rk = jnp.where(incl, aa[:, 2 * c:, PAIR:], 0.0).astype(BF16)
        pw = a_ab.astype(BF16)
        tinv = eye + a_ab
        for _ in range(int(math.log2(c)) - 1):
            pw = _bdot(pw, pw).astype(BF16)
            tinv = tinv + _bdot(tinv.astype(BF16), pw)
        akv = _bdot(a_ak, vbd.astype(BF16))
        wu = _bdot(tinv.astype(BF16), jnp.concatenate([abar, akv.astype(BF16)], axis=2))
        s_bd = state_ref[...]
        wr = _bdot_nt(jnp.concatenate([wu[:, :, :PAIR].astype(BF16), rbar], axis=1), s_bd.astype(BF16))
        e_bd = wr[:, :2 * c] + wu[:, :, PAIR:]
        ev = jnp.concatenate([e_bd, vbd], axis=1)
        y_bd = wr[:, 2 * c:] + _bdot(jnp.concatenate([a_rb, a_rk], axis=2), ev.astype(BF16))
        bk = jnp.concatenate([bd(pairs(bh_ref, rows)), bd(pairs(kh_ref, rows))], axis=1).astype(BF16)
        ds = _bdot(jnp.swapaxes(ev, 1, 2).astype(BF16), bk)
        state_ref[...] = s_bd * pairs(pc_ref, rows)[:, 0:1, :] + jnp.where(same, ds, 0.0)
        mean = jnp.sum(y_bd, axis=-1, keepdims=True) * (1.0 / R_HEAD_DIM)
        dev = jnp.where(same, y_bd - mean, 0.0)
        var = jnp.sum(dev * dev, axis=-1, keepdims=True) * (1.0 / R_HEAD_DIM)
        yn = dev * lax.rsqrt(var + GN_EPS)
        yn = yn[:, :c] + yn[:, c:]
        rkv = pairs(rkv_ref, rows)
        s0 = jnp.sum(jnp.where(lane0, rkv, 0.0), axis=-1, keepdims=True)
        s1 = jnp.sum(jnp.where(lane0, 0.0, rkv), axis=-1, keepdims=True)
        bonus = jnp.where(lane0, s0, s1) * v_pl
        out = ((yn * lnw + lnb + bonus) * pairs(gg_ref, rows)).astype(BF16)
        for p in range(npair):
            o_ref[0, rows, p * PAIR:(p + 1) * PAIR] = out[p]
        return carry

    lax.fori_loop(0, nc, chunk_body, 0)


def _rwkv(z, prm):
    b, s, _ = z.shape
    tl = RWKV_TILE

    def seg(width, off):
        return pl.BlockSpec((1, tl, width), lambda bi, i: (bi, i, off // width))

    def full(arr):
        return pl.BlockSpec(arr.shape, lambda bi, i: (0,) * arr.ndim)

    names = ("mu_r", "mu_k", "mu_v", "mu_g", "mu_w", "mu_a", "w0", "a0", "k_k", "k_a", "r_k", "ln_w", "ln_b",
             "w2", "a2", "g2")
    consts = [prm[n] for n in names]
    wide = pltpu.VMEM((tl, R_WIDTH), F32)
    return pl.pallas_call(
        _rwkv_kernel,
        out_shape=jax.ShapeDtypeStruct((b, s, R_WIDTH), BF16),
        grid=(b, s // tl),
        in_specs=[seg(R_WIDTH, Z_R), seg(R_WIDTH, Z_K), seg(R_WIDTH, Z_V), seg(GATE_LORA, Z_GL),
                  seg(LORA_PAD, Z_WL), seg(LORA_PAD, Z_AL)] + [full(x) for x in consts],
        out_specs=pl.BlockSpec((1, tl, R_WIDTH), lambda bi, i: (bi, i, 0)),
        scratch_shapes=[
            pltpu.VMEM((1, R_WIDTH), F32), pltpu.VMEM((1, R_WIDTH), F32), pltpu.VMEM((1, R_WIDTH), F32),
            pltpu.VMEM((1, GATE_LORA), F32), pltpu.VMEM((1, LORA_PAD), F32), pltpu.VMEM((1, LORA_PAD), F32),
            pltpu.VMEM((R_WIDTH // PAIR, PAIR, PAIR), F32),
            wide, wide, wide, wide, wide, wide, wide, wide, wide, wide,
        ],
        compiler_params=_params("parallel", "arbitrary"),
        name="rwkv7",
    )(z, z, z, z, z, z, *consts)


def _pad_cols(w, width):
    return jnp.pad(w, ((0, 0), (0, width - w.shape[1])))


def _layout_w_in(w):
    offs = [0]
    for n in (A_WIDTH, KV_RANK, IDX_HEADS * IDX_DIM, IDX_DIM, IDX_HEADS,
              R_WIDTH, R_WIDTH, R_WIDTH, DECAY_LORA, AAA_LORA, GATE_LORA):
        offs.append(offs[-1] + n)
    q, ckv, qi, ki, wi, r, k, v, wl, al, gl = (w[:, offs[i]:offs[i + 1]] for i in range(11))
    cols = [q, qi, r, k, v, gl, ckv, _pad_cols(jnp.concatenate([ki, wi], axis=1), LORA_PAD),
            _pad_cols(wl, LORA_PAD), _pad_cols(al, LORA_PAD)]
    return _pad_cols(jnp.concatenate(cols, axis=1), Z_WIDTH).astype(BF16)


def _pad_rows(w, rows):
    return jnp.pad(w, ((0, rows - w.shape[0]), (0, 0)))


def _rwkv_params(l, mu, w0, w2, a0, a2, g2, k_k, k_a, r_k, ln_w, ln_b):
    m = mu[l]
    o = [0, R_WIDTH, 2 * R_WIDTH, 3 * R_WIDTH, 3 * R_WIDTH + DECAY_LORA, 3 * R_WIDTH + DECAY_LORA + AAA_LORA]
    row = lambda x: x.reshape(1, -1)
    return {
        "mu_r": row(m[o[0]:o[1]]), "mu_k": row(m[o[1]:o[2]]), "mu_v": row(m[o[2]:o[3]]),
        "mu_w": _pad_cols(row(m[o[3]:o[4]]), LORA_PAD), "mu_a": _pad_cols(row(m[o[4]:o[5]]), LORA_PAD),
        "mu_g": row(m[o[5]:]),
        "w0": row(w0[l]), "a0": row(a0[l]), "k_k": row(k_k[l]), "k_a": row(k_a[l]), "r_k": row(r_k[l]),
        "ln_w": row(ln_w[l]), "ln_b": row(ln_b[l]),
        "w2": _pad_rows(w2[l], LORA_PAD).astype(BF16), "a2": _pad_rows(a2[l], LORA_PAD).astype(BF16),
        "g2": g2[l].astype(BF16),
    }


def kernel(x, c, t5_bias, ada_w, ada_b, norm_g, ffn_w_in, ffn_w_out, w_in, ckv_norm_g, w_uk, w_uv, rwkv_mu, rwkv_w0, rwkv_w2, rwkv_a0, rwkv_a2, rwkv_g2, rwkv_k_k, rwkv_k_a, rwkv_r_k, rwkv_ln_w, rwkv_ln_b, w_out, final_norm_g):
    b, s, d = x.shape
    depth = ada_w.shape[0]
    assert s % DSA_TILE == 0 and s % RWKV_TILE == 0 and d == A_WIDTH + R_WIDTH
    mod = _ada_mod(c, ada_w, ada_b).reshape(depth, b, N_SUB, 3, 1, d)
    bias = _bias_tiles(t5_bias)
    final_g = final_norm_g.reshape(1, d)
    h = x
    for l in range(depth):
        shift = lambda i: mod[l, :, i, 0]
        scale = lambda i: mod[l, :, i, 1]
        gate = lambda i: mod[l, :, i, 2]
        g = lambda i: norm_g[l, i].reshape(1, d)
        h = _ffn(h, g(0), shift(0), scale(0), gate(0), ffn_w_in[l, 0].astype(BF16), ffn_w_out[l, 0].astype(BF16),
                 final_g, final_norm=False)
        z = _proj(h, g(1), shift(1), scale(1), _layout_w_in(w_in[l]))
        qt, qit, ckv, ckvt, ki, wit = _dsa_prep(z, ckv_norm_g[l].reshape(1, KV_RANK))
        wuk = jnp.transpose(w_uk[l], (1, 0, 2)).astype(BF16)
        wuvt = jnp.transpose(w_uv[l], (1, 2, 0)).astype(BF16)
        o_a = _dsa(qt, qit, ckv, ckvt, ki, wit, wuk, wuvt, bias)
        o_r = _rwkv(z, _rwkv_params(l, rwkv_mu, rwkv_w0, rwkv_w2, rwkv_a0, rwkv_a2, rwkv_g2, rwkv_k_k,
                                    rwkv_k_a, rwkv_r_k, rwkv_ln_w, rwkv_ln_b))
        h = _outproj(o_a, o_r, h, gate(1), w_out[l].astype(BF16))
        h = _ffn(h, g(2), shift(2), scale(2), gate(2), ffn_w_in[l, 1].astype(BF16), ffn_w_out[l, 1].astype(BF16),
                 final_g, final_norm=(l == depth - 1))
    return h
```

```python
import functools
import math

import jax
import jax.numpy as jnp
from jax import lax
from jax.experimental import pallas as pl
from jax.experimental.pallas import tpu as pltpu

F32 = jnp.float32
BF16 = jnp.bfloat16
I32 = jnp.int32
I16 = jnp.int16

A_HEADS = 8
A_HEAD_DIM = 128
A_WIDTH = A_HEADS * A_HEAD_DIM
KV_RANK = 256
IDX_HEADS = 16
IDX_DIM = 64
TOPK_MAX = 256
REL_BUCKETS = 32
REL_MAX_EXACT = REL_BUCKETS // 2
REL_MAX_DIST = 128
R_HEAD_DIM = 64
R_WIDTH = 1024
R_HEADS = R_WIDTH // R_HEAD_DIM
DECAY_LORA = 96
AAA_LORA = 96
GATE_LORA = 256
GN_EPS = 64e-5
RMS_EPS = 1e-6
N_SUB = 3

LANES = 128
SUBLANES = 8
VMEM_LIMIT_BYTES = 56 * 1024 * 1024

LORA_PAD = 128
Z_Q = 0
Z_QI = Z_Q + A_WIDTH
Z_R = Z_QI + IDX_HEADS * IDX_DIM
Z_K = Z_R + R_WIDTH
Z_V = Z_K + R_WIDTH
Z_GL = Z_V + R_WIDTH
Z_CKV = Z_GL + GATE_LORA
Z_KIWI = Z_CKV + KV_RANK
Z_WL = Z_KIWI + LORA_PAD
Z_AL = Z_WL + LORA_PAD
Z_END = Z_AL + LORA_PAD
Z_WIDTH = 6144

FFN_TOKEN_TILE = 1024
FFN_HIDDEN_TILE = 256
DSA_TILE = 256
RWKV_CHUNK = 64
RWKV_TILE = 256
PAIR = 2 * R_HEAD_DIM

I16_MIN = -(2 ** 15)
PACKED_ROWS = 2 * SUBLANES
BF16_EXACT_INT = 256
KEY_NEG_INF = -2139095041
MASKED_LOGIT = -1e30
LOG2E = math.log2(math.e)
FAR_DISTANCE = math.ceil(REL_MAX_EXACT * (REL_MAX_DIST / REL_MAX_EXACT)
                         ** ((REL_BUCKETS - 1 - REL_MAX_EXACT) / (REL_BUCKETS - REL_MAX_EXACT)))


def _dot(a, b):
    return jnp.dot(a, b, preferred_element_type=F32)


def _dot_nt(a, b):
    return lax.dot_general(a, b, (((1,), (1,)), ((), ())), preferred_element_type=F32)


def _bdot(a, b):
    return lax.dot_general(a, b, (((2,), (1,)), ((0,), (0,))), preferred_element_type=F32)


def _bdot_nt(a, b):
    return lax.dot_general(a, b, (((2,), (2,)), ((0,), (0,))), preferred_element_type=F32)


def _rms(x, g, eps):
    ms = jnp.mean(x * x, axis=-1, keepdims=True)
    return x * lax.rsqrt(ms + eps) * g


def _divisor_tile(n, pref):
    if n <= pref:
        return n
    t = (pref // LANES) * LANES
    while t > LANES and n % t:
        t -= LANES
    assert n % t == 0, (n, pref)
    return t


def _params(*sem):
    return pltpu.CompilerParams(dimension_semantics=sem, vmem_limit_bytes=VMEM_LIMIT_BYTES)


def _ada_kernel(c_ref, w_ref, b_ref, o_ref):
    c = c_ref[...]
    ca = (c * jax.nn.sigmoid(c)).astype(BF16)
    o_ref[0] = _dot(ca, w_ref[0].astype(BF16)) + b_ref[0]


def _ada_mod(c, ada_w, ada_b):
    depth, d, n = ada_w.shape
    b = c.shape[0]
    bp = -(-b // SUBLANES) * SUBLANES
    cp = jnp.pad(c, ((0, bp - b), (0, 0)))
    tn = _divisor_tile(n, 1024)
    out = pl.pallas_call(
        _ada_kernel,
        out_shape=jax.ShapeDtypeStruct((depth, bp, n), F32),
        grid=(depth, n // tn),
        in_specs=[
            pl.BlockSpec((bp, d), lambda l, j: (0, 0)),
            pl.BlockSpec((1, d, tn), lambda l, j: (l, 0, j)),
            pl.BlockSpec((1, 1, tn), lambda l, j: (l, 0, j)),
        ],
        out_specs=pl.BlockSpec((1, bp, tn), lambda l, j: (l, 0, j)),
        compiler_params=_params("arbitrary", "arbitrary"),
        name="ada_mod",
    )(cp, ada_w, ada_b.reshape(depth, 1, n))
    return out[:, :b]


def _ffn_kernel(h_ref, g_ref, shift_ref, scale_ref, gate_ref, wg_ref, wu_ref, wo_ref, fg_ref,
                o_ref, hn_ref, *, final_norm):
    f = pl.program_id(2)

    @pl.when(f == 0)
    def _():
        h = h_ref[0]
        y = _rms(h, g_ref[...], RMS_EPS)
        hn_ref[...] = (y * (1.0 + scale_ref[0]) + shift_ref[0]).astype(BF16)
        o_ref[0] = h

    hn = hn_ref[...]
    g = _dot(hn, wg_ref[...])
    u = _dot(hn, wu_ref[...])
    act = (g * jax.nn.sigmoid(g) * u).astype(BF16)
    o_ref[0] += (0.5 * gate_ref[0]) * _dot(act, wo_ref[...])

    if final_norm:
        @pl.when(f == pl.num_programs(2) - 1)
        def _():
            o_ref[0] = _rms(o_ref[0], fg_ref[...], RMS_EPS)


def _ffn(h, g, shift, scale, gate, w_in, w_out, final_g, *, final_norm):
    b, s, d = h.shape
    ff = w_out.shape[0]
    tm = _divisor_tile(s, FFN_TOKEN_TILE)
    tf = _divisor_tile(ff, FFN_HIDDEN_TILE)
    nf = ff // tf
    vec = pl.BlockSpec((1, 1, d), lambda bi, i, f: (bi, 0, 0))
    row = pl.BlockSpec((1, d), lambda bi, i, f: (0, 0))
    tile = pl.BlockSpec((1, tm, d), lambda bi, i, f: (bi, i, 0))
    return pl.pallas_call(
        functools.partial(_ffn_kernel, final_norm=final_norm),
        out_shape=jax.ShapeDtypeStruct((b, s, d), F32),
        grid=(b, s // tm, nf),
        in_specs=[
            tile, row, vec, vec, vec,
            pl.BlockSpec((d, tf), lambda bi, i, f: (0, f)),
            pl.BlockSpec((d, tf), lambda bi, i, f: (0, nf + f)),
            pl.BlockSpec((tf, d), lambda bi, i, f: (f, 0)),
            row,
        ],
        out_specs=tile,
        scratch_shapes=[pltpu.VMEM((tm, d), BF16)],
        compiler_params=_params("parallel", "parallel", "arbitrary"),
        name="ffn",
    )(h, g, shift, scale, gate, w_in, w_in, w_out, final_g)


def _proj_kernel(h_ref, g_ref, shift_ref, scale_ref, w_ref, o_ref, hn_ref):
    @pl.when(pl.program_id(2) == 0)
    def _():
        y = _rms(h_ref[0], g_ref[...], RMS_EPS)
        hn_ref[...] = (y * (1.0 + scale_ref[0]) + shift_ref[0]).astype(BF16)

    o_ref[0] = _dot(hn_ref[...], w_ref[...])


def _proj(h, g, shift, scale, w):
    b, s, d = h.shape
    p = w.shape[1]
    tm = _divisor_tile(s, 1024)
    tn = _divisor_tile(p, 1024)
    vec = pl.BlockSpec((1, 1, d), lambda bi, i, n: (bi, 0, 0))
    return pl.pallas_call(
        _proj_kernel,
        out_shape=jax.ShapeDtypeStruct((b, s, p), F32),
        grid=(b, s // tm, p // tn),
        in_specs=[
            pl.BlockSpec((1, tm, d), lambda bi, i, n: (bi, i, 0)),
            pl.BlockSpec((1, d), lambda bi, i, n: (0, 0)),
            vec, vec,
            pl.BlockSpec((d, tn), lambda bi, i, n: (0, n)),
        ],
        out_specs=pl.BlockSpec((1, tm, tn), lambda bi, i, n: (bi, i, n)),
        scratch_shapes=[pltpu.VMEM((tm, d), BF16)],
        compiler_params=_params("parallel", "parallel", "arbitrary"),
        name="proj",
    )(h, g, shift, scale, w)


def _outproj_kernel(oa_ref, or_ref, h_ref, gate_ref, wa_ref, wr_ref, o_ref):
    acc = _dot(oa_ref[0], wa_ref[...]) + _dot(or_ref[0], wr_ref[...])
    o_ref[0] = h_ref[0] + gate_ref[0] * acc


def _outproj(o_a, o_r, h, gate, w_out):
    b, s, d = h.shape
    tm = _divisor_tile(s, 1024)
    tn = _divisor_tile(d, 1024)
    wa = o_a.shape[-1]
    wr = o_r.shape[-1]
    return pl.pallas_call(
        _outproj_kernel,
        out_shape=jax.ShapeDtypeStruct((b, s, d), F32),
        grid=(b, s // tm, d // tn),
        in_specs=[
            pl.BlockSpec((1, tm, wa), lambda bi, i, n: (bi, i, 0)),
            pl.BlockSpec((1, tm, wr), lambda bi, i, n: (bi, i, 0)),
            pl.BlockSpec((1, tm, tn), lambda bi, i, n: (bi, i, n)),
            pl.BlockSpec((1, 1, tn), lambda bi, i, n: (bi, 0, n)),
            pl.BlockSpec((wa, tn), lambda bi, i, n: (0, n)),
            pl.BlockSpec((wr, tn), lambda bi, i, n: (wa // wr, n)),
        ],
        out_specs=pl.BlockSpec((1, tm, tn), lambda bi, i, n: (bi, i, n)),
        compiler_params=_params("parallel", "parallel", "arbitrary"),
        name="outproj",
    )(o_a, o_r, h, gate, w_out, w_out)


def _dsa_prep_kernel(q_ref, qi_ref, ckv_ref, kiwi_ref, g_ref, qt_ref, qit_ref, ckv_o, ckvt_o, ki_o, wit_o):
    qt_ref[0] = q_ref[0].T.astype(BF16)
    qit_ref[0] = qi_ref[0].T.astype(BF16)
    cn = _rms(ckv_ref[0], g_ref[...], RMS_EPS)
    ckv_o[0] = cn.astype(BF16)
    ckvt_o[0, 0] = cn.T.astype(BF16)
    kw = kiwi_ref[0]
    ki_o[0] = kw[:, :IDX_DIM].astype(BF16)
    wit_o[0] = kw.T[IDX_DIM:IDX_DIM + IDX_HEADS, :] * (IDX_HEADS * IDX_DIM) ** -0.5


def _dsa_prep(z, ckv_g):
    b, s, _ = z.shape
    t = DSA_TILE
    nt = s // t

    def seg(width, off):
        return pl.BlockSpec((1, t, width), lambda bi, i: (bi, i, off // width))

    return pl.pallas_call(
        _dsa_prep_kernel,
        out_shape=(
            jax.ShapeDtypeStruct((b, A_WIDTH, s), BF16),
            jax.ShapeDtypeStruct((b, IDX_HEADS * IDX_DIM, s), BF16),
            jax.ShapeDtypeStruct((b, s, KV_RANK), BF16),
            jax.ShapeDtypeStruct((b, nt, KV_RANK, t), BF16),
            jax.ShapeDtypeStruct((b, s, IDX_DIM), BF16),
            jax.ShapeDtypeStruct((b, IDX_HEADS, s), F32),
        ),
        grid=(b, nt),
        in_specs=[
            seg(A_WIDTH, Z_Q), seg(IDX_HEADS * IDX_DIM, Z_QI), seg(KV_RANK, Z_CKV), seg(LORA_PAD, Z_KIWI),
            pl.BlockSpec((1, KV_RANK), lambda bi, i: (0, 0)),
        ],
        out_specs=(
            pl.BlockSpec((1, A_WIDTH, t), lambda bi, i: (bi, 0, i)),
            pl.BlockSpec((1, IDX_HEADS * IDX_DIM, t), lambda bi, i: (bi, 0, i)),
            pl.BlockSpec((1, t, KV_RANK), lambda bi, i: (bi, i, 0)),
            pl.BlockSpec((1, 1, KV_RANK, t), lambda bi, i: (bi, i, 0, 0)),
            pl.BlockSpec((1, t, IDX_DIM), lambda bi, i: (bi, i, 0)),
            pl.BlockSpec((1, IDX_HEADS, t), lambda bi, i: (bi, 0, i)),
        ),
        compiler_params=_params("parallel", "parallel"),
        name="dsa_prep",
    )(z, z, z, z, ckv_g)


def _bias_kernel(t5_ref, o_ref):
    t = DSA_TILE
    j = lax.broadcasted_iota(I32, (t, t), 0)
    i = lax.broadcasted_iota(I32, (t, t), 1)
    for didx in range(2):
        n = jnp.maximum(didx * t + i - j, 0)
        nf = jnp.maximum(n, 1).astype(F32)
        large = REL_MAX_EXACT + (jnp.log(nf / REL_MAX_EXACT) / math.log(REL_MAX_DIST / REL_MAX_EXACT)
                                 * (REL_BUCKETS - REL_MAX_EXACT)).astype(I32)
        large = jnp.minimum(large, REL_BUCKETS - 1)
        bucket = jnp.where(n < REL_MAX_EXACT, n, large)
        for h in range(A_HEADS):
            val = jnp.zeros((t, t), F32)
            for k in range(REL_BUCKETS):
                val = jnp.where(bucket == k, t5_ref[k, h], val)
            o_ref[didx, :, h * t:(h + 1) * t] = (val - t5_ref[REL_BUCKETS - 1, h]) * LOG2E


def _bias_tiles(t5_bias):
    t = DSA_TILE
    assert t + 1 >= FAR_DISTANCE
    return pl.pallas_call(
        _bias_kernel,
        out_shape=jax.ShapeDtypeStruct((2, t, A_HEADS * t), F32),
        in_specs=[pl.BlockSpec(memory_space=pltpu.SMEM)],
        out_specs=pl.BlockSpec(memory_space=pltpu.VMEM),
        compiler_params=pltpu.CompilerParams(vmem_limit_bytes=VMEM_LIMIT_BYTES),
        name="t5_bias_tiles",
    )(t5_bias)


def _dsa_kernel(qt_ref, qit_ref, ckv_ref, ckvt_ref, ki_ref, wit_ref, wuk_ref, wuvt_ref, bias_ref,
                o_ref, keys_ref, khi_ref, klo_ref, acc_ref, ot_ref, *, topk, seq_len):
    t = DSA_TILE
    qb = pl.program_id(1)
    nk = qb + 1
    row = lax.broadcasted_iota(I32, (t, t), 0)
    col = lax.broadcasted_iota(I32, (t, t), 1)

    def idx_body(kc, carry):
        kic = ki_ref[0, pl.ds(pl.multiple_of(kc * t, t), t), :]
        acc = jnp.zeros((t, t), F32)
        for h in range(IDX_HEADS):
            rel = _dot(kic, qit_ref[0, h * IDX_DIM:(h + 1) * IDX_DIM, :])
            acc = acc + jnp.maximum(rel, 0.0) * wit_ref[0, h:h + 1, :]
        acc = acc + 0.0
        bits = pltpu.bitcast(acc, I32)
        key = bits ^ ((bits >> 31) & 0x7FFFFFFF)
        causal = (kc * t + row) <= (qb * t + col)
        key = jnp.where(causal, key, KEY_NEG_INF)
        sl = pl.ds(pl.multiple_of(kc * t, t), t)
        keys_ref[sl, :] = key
        khi_ref[sl, :] = (key >> 16).astype(I16)
        klo_ref[sl, :] = ((key & 0xFFFF) + I16_MIN).astype(I16)
        return carry

    lax.fori_loop(0, nk, idx_body, 0)

    def count16(ref, cand):
        cand16 = cand.astype(I16)

        def body(kc, acc):
            x = ref[pl.ds(pl.multiple_of(kc * t, t), t), :]
            ones = jnp.where(x >= cand16, jnp.asarray(1, BF16), jnp.asarray(0, BF16))
            ones = ones.reshape(t // PACKED_ROWS, PACKED_ROWS, t)
            for i in range(t // PACKED_ROWS):
                acc = acc + ones[i]
            return acc

        acc = lax.fori_loop(0, nk // 2, lambda i, acc: body(2 * i + 1, body(2 * i, acc)),
                            jnp.zeros((PACKED_ROWS, t), BF16))
        acc = lax.fori_loop(2 * (nk // 2), nk, body, acc)
        return jnp.sum(acc.astype(F32), axis=0, keepdims=True).astype(I32)

    def search16(ref, base):
        def accept(cand, cur):
            return jnp.where(base + count16(ref, cand) >= topk, cand, cur)

        v = accept(jnp.zeros((1, t), I32), jnp.full((1, t), I16_MIN, I32))
        return lax.fori_loop(0, 15, lambda i, v: accept(v | (jnp.int32(1) << (14 - i)), v), v)

    hi = search16(khi_ref, 0)
    above = count16(khi_ref, jnp.minimum(hi + 1, -I16_MIN - 1))
    hi16 = hi.astype(I16)

    def keep_low(kc, carry):
        sl = pl.ds(pl.multiple_of(kc * t, t), t)
        klo_ref[sl, :] = jnp.where(khi_ref[sl, :] == hi16, klo_ref[sl, :], jnp.asarray(I16_MIN, I16))
        return carry

    lax.fori_loop(0, nk, keep_low, 0)
    lo = search16(klo_ref, above)
    thr = jnp.maximum((hi << 16) | (lo - I16_MIN), KEY_NEG_INF + 1)

    def count(pred):
        def body(kc, acc):
            k = keys_ref[pl.ds(pl.multiple_of(kc * t, t), t), :]
            m = jnp.where(pred(k, kc), 1, 0).astype(I32)
            return acc + jnp.sum(m.reshape(t // SUBLANES, SUBLANES, t), axis=0)

        acc = lax.fori_loop(0, nk, body, jnp.zeros((SUBLANES, t), I32))
        return jnp.sum(acc, axis=0, keepdims=True)

    @pl.when(jnp.max(count(lambda k, kc: k >= thr)) > topk)
    def _():
        need = topk - count(lambda k, kc: k > thr)
        nbits = (seq_len - 1).bit_length()

        def pos_bit(i, y):
            cand = y | (jnp.int32(1) << (nbits - 1 - i))
            before = count(lambda k, kc: (k == thr) & ((kc * t + row) < cand))
            return jnp.where(before < need, cand, y)

        last_kept = lax.fori_loop(0, nbits, pos_bit, jnp.zeros((1, t), I32))

        def demote(kc, carry):
            sl = pl.ds(pl.multiple_of(kc * t, t), t)
            k = keys_ref[sl, :]
            keys_ref[sl, :] = jnp.where((k == thr) & ((kc * t + row) > last_kept), k - 1, k)
            return carry

        lax.fori_loop(0, nk, demote, 0)

    scale = A_HEAD_DIM ** -0.5 * LOG2E

    hw = A_HEADS * t
    qlat = jnp.concatenate(
        [(_dot(wuk_ref[h], qt_ref[0, h * A_HEAD_DIM:(h + 1) * A_HEAD_DIM, :]) * scale).astype(BF16)
         for h in range(A_HEADS)], axis=1)
    acc_ref[...] = jnp.zeros_like(acc_ref)

    def att_body(kc, carry, *, near):
        m, l = carry
        sl = pl.ds(pl.multiple_of(kc * t, t), t)
        raw = _dot(ckv_ref[0, sl, :], qlat)
        sel = keys_ref[sl, :] >= thr
        parts = []
        for h in range(A_HEADS):
            lg_h = raw[:, h * t:(h + 1) * t]
            if near:
                lg_h = lg_h + bias_ref[qb - kc, :, h * t:(h + 1) * t]
            parts.append(jnp.where(sel, lg_h, MASKED_LOGIT))
        lg = jnp.concatenate(parts, axis=1)
        m_new = jnp.maximum(m, jnp.max(lg, axis=0, keepdims=True))
        alpha = jnp.exp2(m - m_new)
        p = jnp.exp2(lg - m_new)
        l = l * alpha + jnp.sum(p, axis=0, keepdims=True)
        acc_ref[...] = acc_ref[...] * alpha + _dot(ckvt_ref[0, kc], p.astype(BF16))
        return m_new, l

    n_far = jnp.maximum(qb - 1, 0)
    carry = (jnp.full((1, hw), MASKED_LOGIT, F32), jnp.zeros((1, hw), F32))
    carry = lax.fori_loop(0, n_far, functools.partial(att_body, near=False), carry)
    _, l = lax.fori_loop(n_far, nk, functools.partial(att_body, near=True), carry)
    olat = (acc_ref[...] / l).astype(BF16)
    for h in range(A_HEADS):
        ot_ref[h * A_HEAD_DIM:(h + 1) * A_HEAD_DIM, :] = _dot(wuvt_ref[h], olat[:, h * t:(h + 1) * t])
    o_ref[0] = ot_ref[...].T.astype(BF16)


def _dsa(qt, qit, ckv, ckvt, ki, wit, wuk, wuvt, bias):
    b, _, s = qt.shape
    t = DSA_TILE
    nt = s // t
    topk = min(TOPK_MAX, s // 4)
    assert s // PACKED_ROWS <= BF16_EXACT_INT
    return pl.pallas_call(
        functools.partial(_dsa_kernel, topk=topk, seq_len=s),
        out_shape=jax.ShapeDtypeStruct((b, s, A_WIDTH), BF16),
        grid=(b, nt),
        in_specs=[
            pl.BlockSpec((1, A_WIDTH, t), lambda bi, i: (bi, 0, i)),
            pl.BlockSpec((1, IDX_HEADS * IDX_DIM, t), lambda bi, i: (bi, 0, i)),
            pl.BlockSpec((1, s, KV_RANK), lambda bi, i: (bi, 0, 0)),
            pl.BlockSpec((1, nt, KV_RANK, t), lambda bi, i: (bi, 0, 0, 0)),
            pl.BlockSpec((1, s, IDX_DIM), lambda bi, i: (bi, 0, 0)),
            pl.BlockSpec((1, IDX_HEADS, t), lambda bi, i: (bi, 0, i)),
            pl.BlockSpec((A_HEADS, KV_RANK, A_HEAD_DIM), lambda bi, i: (0, 0, 0)),
            pl.BlockSpec((A_HEADS, A_HEAD_DIM, KV_RANK), lambda bi, i: (0, 0, 0)),
            pl.BlockSpec((2, t, A_HEADS * t), lambda bi, i: (0, 0, 0)),
        ],
        out_specs=pl.BlockSpec((1, t, A_WIDTH), lambda bi, i: (bi, i, 0)),
        scratch_shapes=[pltpu.VMEM((s, t), I32), pltpu.VMEM((s, t), I16), pltpu.VMEM((s, t), I16),
                        pltpu.VMEM((KV_RANK, A_HEADS * t), F32), pltpu.VMEM((A_WIDTH, t), F32)],
        compiler_params=_params("parallel", "arbitrary"),
        name="dsa",
    )(qt, qit, ckv, ckvt, ki, wit, wuk, wuvt, bias)


def _split3(x):
    hi = x.astype(BF16)
    r1 = x - hi.astype(F32)
    mid = r1.astype(BF16)
    lo = (r1 - mid.astype(F32)).astype(BF16)
    return hi, mid, lo


def _rwkv_kernel(r_ref, k_ref, v_ref, gl_ref, wl_ref, al_ref,
                 mur_ref, muk_ref, muv_ref, mug_ref, muw_ref, mua_ref,
                 w0_ref, a0_ref, kk_ref, ka_ref, rk_ref, lnw_ref, lnb_ref,
                 w2_ref, a2_ref, g2_ref,
                 o_ref,
                 pr_ref, pk_ref, pv_ref, pg_ref, pw_ref, pa_ref, state_ref):
    tl = RWKV_TILE
    c = RWKV_CHUNK
    nc = tl // c
    npair = R_WIDTH // PAIR

    @pl.when(pl.program_id(1) == 0)
    def _():
        state_ref[...] = jnp.zeros_like(state_ref)
        for ref in (pr_ref, pk_ref, pv_ref, pg_ref, pw_ref, pa_ref):
            ref[...] = jnp.zeros_like(ref)

    def shift_mix(x_ref, prev_ref, mu_ref):
        x = x_ref[0]
        first = lax.broadcasted_iota(I32, x.shape, 0) == 0
        xprev = jnp.where(first, prev_ref[...], pltpu.roll(x, 1, axis=0))
        prev_ref[...] = x[tl - 1:tl, :]
        return x + (xprev - x) * mu_ref[...]

    r = shift_mix(r_ref, pr_ref, mur_ref)
    k = shift_mix(k_ref, pk_ref, muk_ref)
    v = shift_mix(v_ref, pv_ref, muv_ref)
    gl = shift_mix(gl_ref, pg_ref, mug_ref)
    wl = shift_mix(wl_ref, pw_ref, muw_ref)
    al = shift_mix(al_ref, pa_ref, mua_ref)

    w_lin = w0_ref[...] + _dot(jnp.tanh(wl).astype(BF16), w2_ref[...])
    nx = -w_lin
    softplus = jnp.maximum(nx, 0.0) + jnp.log(1.0 + jnp.exp(-jnp.abs(nx)))
    ld = -jnp.exp(-softplus - 0.5)
    a = jax.nn.sigmoid(a0_ref[...] + _dot(al.astype(BF16), a2_ref[...]))
    gate = _dot(jax.nn.sigmoid(gl).astype(BF16), g2_ref[...])

    li = lax.broadcasted_iota(I32, (LANES, LANES), 0) // R_HEAD_DIM
    lj = lax.broadcasted_iota(I32, (LANES, LANES), 1) // R_HEAD_DIM
    head_ones = jnp.where(li == lj, 1.0, 0.0).astype(BF16)

    def head_sum(x):
        parts = []
        for j in range(R_WIDTH // LANES):
            hi, mid, lo = _split3(x[:, j * LANES:(j + 1) * LANES])
            parts.append(_dot(hi, head_ones) + _dot(mid, head_ones) + _dot(lo, head_ones))
        return jnp.concatenate(parts, axis=-1)

    kk = k * kk_ref[...]
    kk = kk / jnp.maximum(jnp.sqrt(head_sum(kk * kk)), 1e-12)
    k2 = k * (1.0 + (a - 1.0) * ka_ref[...])
    bb = kk * a

    ti = lax.broadcasted_iota(I32, (tl, tl), 0)
    tj = lax.broadcasted_iota(I32, (tl, tl), 1)
    tri = jnp.where((ti // c == tj // c) & (tj <= ti), 1.0, 0.0).astype(BF16)
    hi, mid, lo = _split3(ld)
    cum = _dot(tri, hi) + _dot(tri, mid) + _dot(tri, lo)
    cum_end = jnp.broadcast_to(cum.reshape(nc, c, R_WIDTH)[:, c - 1:c, :], (nc, c, R_WIDTH)).reshape(tl, R_WIDTH)
    e_in = jnp.exp(cum)
    e_neg = jnp.exp(-cum)
    e_out = jnp.exp(cum_end - cum)

    nb = nc * npair

    def to_pairs(x):
        x3 = x.reshape(nc, c, R_WIDTH)
        xs = jnp.stack([x3[:, :, p * PAIR:(p + 1) * PAIR] for p in range(npair)], axis=1)
        return xs.reshape(nb, c, PAIR)

    lane_head = lax.broadcasted_iota(I32, (2 * c, PAIR), 1) // R_HEAD_DIM
    row_head = lax.broadcasted_iota(I32, (2 * c, PAIR), 0) // c
    same = lane_head == row_head
    rt = lax.broadcasted_iota(I32, (2 * c, PAIR), 0) % c
    ct = lax.broadcasted_iota(I32, (2 * c, PAIR), 1) % c
    strict = same & (ct < rt)
    incl = same & (ct <= rt)
    eye = jnp.where(same & (ct == rt), 1.0, 0.0).astype(F32)

    def bd(x):
        return jnp.where(same, jnp.concatenate([x, x], axis=1), 0.0)

    lane0 = lax.broadcasted_iota(I32, (c, PAIR), 1) < R_HEAD_DIM

    abar = bd(to_pairs(-kk * jnp.exp(cum - ld))).astype(BF16)
    rbar = bd(to_pairs(r * e_in)).astype(BF16)
    bt = bd(to_pairs(bb * e_neg)).astype(BF16)
    kt = bd(to_pairs(k2 * e_neg)).astype(BF16)
    bk = jnp.concatenate([bd(to_pairs(bb * e_out)), bd(to_pairs(k2 * e_out))], axis=1).astype(BF16)
    v_pl = to_pairs(v)
    vbd = bd(v_pl)
    decay_end = to_pairs(jnp.exp(cum_end))[:, 0:1, :]
    aa = _bdot_nt(jnp.concatenate([abar, rbar], axis=1), jnp.concatenate([bt, kt], axis=1))
    a_ab = jnp.where(strict, aa[:, :2 * c, :PAIR], 0.0)
    a_ak = jnp.where(strict, aa[:, :2 * c, PAIR:], 0.0).astype(BF16)
    a_rb = jnp.where(incl, aa[:, 2 * c:, :PAIR], 0.0).astype(BF16)
    a_rk = jnp.where(incl, aa[:, 2 * c:, PAIR:], 0.0).astype(BF16)
    pw = a_ab.astype(BF16)
    tinv = eye + a_ab
    for _ in range(int(math.log2(c)) - 1):
        pw = _bdot(pw, pw).astype(BF16)
        tinv = tinv + _bdot(tinv.astype(BF16), pw)
    akv = _bdot(a_ak, vbd.astype(BF16))
    wu = _bdot(tinv.astype(BF16), jnp.concatenate([abar, akv.astype(BF16)], axis=2))
    w16 = wu[:, :, :PAIR].astype(BF16)
    u_bd = wu[:, :, PAIR:]

    s_bd = state_ref[...]
    e_parts, rs_parts = [], []
    for ci in range(nc):
        sl = slice(ci * npair, (ci + 1) * npair)
        wr = _bdot_nt(jnp.concatenate([w16[sl], rbar[sl]], axis=1), s_bd.astype(BF16))
        e_bd = wr[:, :2 * c] + u_bd[sl]
        e_parts.append(e_bd)
        rs_parts.append(wr[:, 2 * c:])
        ev = jnp.concatenate([e_bd, vbd[sl]], axis=1)
        ds = _bdot(jnp.swapaxes(ev, 1, 2).astype(BF16), bk[sl])
        s_bd = s_bd * decay_end[sl] + jnp.where(same, ds, 0.0)
    state_ref[...] = s_bd

    ev_all = jnp.concatenate([jnp.concatenate(e_parts, axis=0), vbd], axis=1).astype(BF16)
    y_bd = jnp.concatenate(rs_parts, axis=0) + _bdot(jnp.concatenate([a_rb, a_rk], axis=2), ev_all)
    mean = jnp.sum(y_bd, axis=-1, keepdims=True) * (1.0 / R_HEAD_DIM)
    dev = jnp.where(same, y_bd - mean, 0.0)
    var = jnp.sum(dev * dev, axis=-1, keepdims=True) * (1.0 / R_HEAD_DIM)
    yn = dev * lax.rsqrt(var + GN_EPS)
    yn = yn[:, :c] + yn[:, c:]
    rkv = to_pairs(r * k2 * rk_ref[...])
    s0 = jnp.sum(jnp.where(lane0, rkv, 0.0), axis=-1, keepdims=True)
    s1 = jnp.sum(jnp.where(lane0, 0.0, rkv), axis=-1, keepdims=True)
    bonus = jnp.where(lane0, s0, s1) * v_pl
    def per_pair(row_ref):
        rows = [row_ref[:, p * PAIR:(p + 1) * PAIR] for p in range(npair)]
        return jnp.stack(rows * nc, axis=0)

    out = ((yn * per_pair(lnw_ref) + per_pair(lnb_ref) + bonus) * to_pairs(gate)).astype(BF16)
    for ci in range(nc):
        for p in range(npair):
            o_ref[0, ci * c:(ci + 1) * c, p * PAIR:(p + 1) * PAIR] = out[ci * npair + p]


def _rwkv(z, prm):
    b, s, _ = z.shape
    tl = RWKV_TILE

    def seg(width, off):
        return pl.BlockSpec((1, tl, width), lambda bi, i: (bi, i, off // width))

    def full(arr):
        return pl.BlockSpec(arr.shape, lambda bi, i: (0,) * arr.ndim)

    names = ("mu_r", "mu_k", "mu_v", "mu_g", "mu_w", "mu_a", "w0", "a0", "k_k", "k_a", "r_k", "ln_w", "ln_b",
             "w2", "a2", "g2")
    consts = [prm[n] for n in names]
    return pl.pallas_call(
        _rwkv_kernel,
        out_shape=jax.ShapeDtypeStruct((b, s, R_WIDTH), BF16),
        grid=(b, s // tl),
        in_specs=[seg(R_WIDTH, Z_R), seg(R_WIDTH, Z_K), seg(R_WIDTH, Z_V), seg(GATE_LORA, Z_GL),
                  seg(LORA_PAD, Z_WL), seg(LORA_PAD, Z_AL)] + [full(x) for x in consts],
        out_specs=pl.BlockSpec((1, tl, R_WIDTH), lambda bi, i: (bi, i, 0)),
        scratch_shapes=[
            pltpu.VMEM((1, R_WIDTH), F32), pltpu.VMEM((1, R_WIDTH), F32), pltpu.VMEM((1, R_WIDTH), F32),
            pltpu.VMEM((1, GATE_LORA), F32), pltpu.VMEM((1, LORA_PAD), F32), pltpu.VMEM((1, LORA_PAD), F32),
            pltpu.VMEM((R_WIDTH // PAIR, PAIR, PAIR), F32),
        ],
        compiler_params=_params("parallel", "arbitrary"),
        name="rwkv7",
    )(z, z, z, z, z, z, *consts)


def _pad_cols(w, width):
    return jnp.pad(w, ((0, 0), (0, width - w.shape[1])))


def _layout_w_in(w):
    offs = [0]
    for n in (A_WIDTH, KV_RANK, IDX_HEADS * IDX_DIM, IDX_DIM, IDX_HEADS,
              R_WIDTH, R_WIDTH, R_WIDTH, DECAY_LORA, AAA_LORA, GATE_LORA):
        offs.append(offs[-1] + n)
    q, ckv, qi, ki, wi, r, k, v, wl, al, gl = (w[:, offs[i]:offs[i + 1]] for i in range(11))
    cols = [q, qi, r, k, v, gl, ckv, _pad_cols(jnp.concatenate([ki, wi], axis=1), LORA_PAD),
            _pad_cols(wl, LORA_PAD), _pad_cols(al, LORA_PAD)]
    return _pad_cols(jnp.concatenate(cols, axis=1), Z_WIDTH).astype(BF16)


def _pad_rows(w, rows):
    return jnp.pad(w, ((0, rows - w.shape[0]), (0, 0)))


def _rwkv_params(l, mu, w0, w2, a0, a2, g2, k_k, k_a, r_k, ln_w, ln_b):
    m = mu[l]
    o = [0, R_WIDTH, 2 * R_WIDTH, 3 * R_WIDTH, 3 * R_WIDTH + DECAY_LORA, 3 * R_WIDTH + DECAY_LORA + AAA_LORA]
    row = lambda x: x.reshape(1, -1)
    return {
        "mu_r": row(m[o[0]:o[1]]), "mu_k": row(m[o[1]:o[2]]), "mu_v": row(m[o[2]:o[3]]),
        "mu_w": _pad_cols(row(m[o[3]:o[4]]), LORA_PAD), "mu_a": _pad_cols(row(m[o[4]:o[5]]), LORA_PAD),
        "mu_g": row(m[o[5]:]),
        "w0": row(w0[l]), "a0": row(a0[l]), "k_k": row(k_k[l]), "k_a": row(k_a[l]), "r_k": row(r_k[l]),
        "ln_w": row(ln_w[l]), "ln_b": row(ln_b[l]),
        "w2": _pad_rows(w2[l], LORA_PAD).astype(BF16), "a2": _pad_rows(a2[l], LORA_PAD).astype(BF16),
        "g2": g2[l].astype(BF16),
    }


def kernel(x, c, t5_bias, ada_w, ada_b, norm_g, ffn_w_in, ffn_w_out, w_in, ckv_norm_g, w_uk, w_uv, rwkv_mu, rwkv_w0, rwkv_w2, rwkv_a0, rwkv_a2, rwkv_g2, rwkv_k_k, rwkv_k_a, rwkv_r_k, rwkv_ln_w, rwkv_ln_b, w_out, final_norm_g):
    b, s, d = x.shape
    depth = ada_w.shape[0]
    assert s % DSA_TILE == 0 and s % RWKV_TILE == 0 and d == A_WIDTH + R_WIDTH
    mod = _ada_mod(c, ada_w, ada_b).reshape(depth, b, N_SUB, 3, 1, d)
    bias = _bias_tiles(t5_bias)
    final_g = final_norm_g.reshape(1, d)
    h = x
    for l in range(depth):
        shift = lambda i: mod[l, :, i, 0]
        scale = lambda i: mod[l, :, i, 1]
        gate = lambda i: mod[l, :, i, 2]
        g = lambda i: norm_g[l, i].reshape(1, d)
        h = _ffn(h, g(0), shift(0), scale(0), gate(0), ffn_w_in[l, 0].astype(BF16), ffn_w_out[l, 0].astype(BF16),
                 final_g, final_norm=False)
        z = _proj(h, g(1), shift(1), scale(1), _layout_w_in(w_in[l]))
        qt, qit, ckv, ckvt, ki, wit = _dsa_prep(z, ckv_norm_g[l].reshape(1, KV_RANK))
        wuk = jnp.transpose(w_uk[l], (1, 0, 2)).astype(BF16)
        wuvt = jnp.transpose(w_uv[l], (1, 2, 0)).astype(BF16)
        o_a = _dsa(qt, qit, ckv, ckvt, ki, wit, wuk, wuvt, bias)
        o_r = _rwkv(z, _rwkv_params(l, rwkv_mu, rwkv_w0, rwkv_w2, rwkv_a0, rwkv_a2, rwkv_g2, rwkv_k_k,
                                    rwkv_k_a, rwkv_r_k, rwkv_ln_w, rwkv_ln_b))
        h = _outproj(o_a, o_r, h, gate(1), w_out[l].astype(BF16))
        h = _ffn(h, g(2), shift(2), scale(2), gate(2), ffn_w_in[l, 1].astype(BF16), ffn_w_out[l, 1].astype(BF16),
                 final_g, final_norm=(l == depth - 1))
    return h
```

```python
import functools
import math

import jax
import jax.numpy as jnp
from jax import lax
from jax.experimental import pallas as pl
from jax.experimental.pallas import tpu as pltpu

F32 = jnp.float32
BF16 = jnp.bfloat16
I32 = jnp.int32
I16 = jnp.int16

A_HEADS = 8
A_HEAD_DIM = 128
A_WIDTH = A_HEADS * A_HEAD_DIM
KV_RANK = 256
IDX_HEADS = 16
IDX_DIM = 64
TOPK_MAX = 256
REL_BUCKETS = 32
REL_MAX_EXACT = REL_BUCKETS // 2
REL_MAX_DIST = 128
R_HEAD_DIM = 64
R_WIDTH = 1024
R_HEADS = R_WIDTH // R_HEAD_DIM
DECAY_LORA = 96
AAA_LORA = 96
GATE_LORA = 256
GN_EPS = 64e-5
RMS_EPS = 1e-6
N_SUB = 3

LANES = 128
SUBLANES = 8
VMEM_LIMIT_BYTES = 56 * 1024 * 1024

LORA_PAD = 128
Z_Q = 0
Z_QI = Z_Q + A_WIDTH
Z_R = Z_QI + IDX_HEADS * IDX_DIM
Z_K = Z_R + R_WIDTH
Z_V = Z_K + R_WIDTH
Z_GL = Z_V + R_WIDTH
Z_CKV = Z_GL + GATE_LORA
Z_KIWI = Z_CKV + KV_RANK
Z_WL = Z_KIWI + LORA_PAD
Z_AL = Z_WL + LORA_PAD
Z_END = Z_AL + LORA_PAD
Z_WIDTH = 6144

FFN_TOKEN_TILE = 1024
FFN_HIDDEN_TILE = 256
DSA_TILE = 256
RWKV_CHUNK = 64
RWKV_TILE = 256
PAIR = 2 * R_HEAD_DIM

I16_MIN = -(2 ** 15)
PACKED_ROWS = 2 * SUBLANES
SUM_ROWS = PACKED_ROWS
BF16_EXACT_INT = 256
KEY_NEG_INF = -2139095041
MASKED_LOGIT = -(2.0 ** 100)
LOG2E = math.log2(math.e)
FAR_DISTANCE = math.ceil(REL_MAX_EXACT * (REL_MAX_DIST / REL_MAX_EXACT)
                         ** ((REL_BUCKETS - 1 - REL_MAX_EXACT) / (REL_BUCKETS - REL_MAX_EXACT)))


def _dot(a, b):
    return jnp.dot(a, b, preferred_element_type=F32)


def _dot_nt(a, b):
    return lax.dot_general(a, b, (((1,), (1,)), ((), ())), preferred_element_type=F32)


def _bdot(a, b):
    return lax.dot_general(a, b, (((2,), (1,)), ((0,), (0,))), preferred_element_type=F32)


def _bdot_nt(a, b):
    return lax.dot_general(a, b, (((2,), (2,)), ((0,), (0,))), preferred_element_type=F32)


def _rms(x, g, eps):
    ms = jnp.mean(x * x, axis=-1, keepdims=True)
    return x * lax.rsqrt(ms + eps) * g


def _divisor_tile(n, pref):
    if n <= pref:
        return n
    t = (pref // LANES) * LANES
    while t > LANES and n % t:
        t -= LANES
    assert n % t == 0, (n, pref)
    return t


def _params(*sem):
    return pltpu.CompilerParams(dimension_semantics=sem, vmem_limit_bytes=VMEM_LIMIT_BYTES)


def _ada_kernel(c_ref, w_ref, b_ref, o_ref):
    c = c_ref[...]
    ca = (c * jax.nn.sigmoid(c)).astype(BF16)
    o_ref[0] = _dot(ca, w_ref[0].astype(BF16)) + b_ref[0]


def _ada_mod(c, ada_w, ada_b):
    depth, d, n = ada_w.shape
    b = c.shape[0]
    bp = -(-b // SUBLANES) * SUBLANES
    cp = jnp.pad(c, ((0, bp - b), (0, 0)))
    tn = _divisor_tile(n, 1024)
    out = pl.pallas_call(
        _ada_kernel,
        out_shape=jax.ShapeDtypeStruct((depth, bp, n), F32),
        grid=(depth, n // tn),
        in_specs=[
            pl.BlockSpec((bp, d), lambda l, j: (0, 0)),
            pl.BlockSpec((1, d, tn), lambda l, j: (l, 0, j)),
            pl.BlockSpec((1, 1, tn), lambda l, j: (l, 0, j)),
        ],
        out_specs=pl.BlockSpec((1, bp, tn), lambda l, j: (l, 0, j)),
        compiler_params=_params("arbitrary", "arbitrary"),
        name="ada_mod",
    )(cp, ada_w, ada_b.reshape(depth, 1, n))
    return out[:, :b]


def _ffn_kernel(h_ref, g_ref, shift_ref, scale_ref, gate_ref, wg_ref, wu_ref, wo_ref, fg_ref,
                o_ref, hn_ref, *, final_norm):
    f = pl.program_id(2)

    @pl.when(f == 0)
    def _():
        h = h_ref[0]
        y = _rms(h, g_ref[...], RMS_EPS)
        hn_ref[...] = (y * (1.0 + scale_ref[0]) + shift_ref[0]).astype(BF16)
        o_ref[0] = h

    hn = hn_ref[...]
    g = _dot(hn, wg_ref[...])
    u = _dot(hn, wu_ref[...])
    act = (g * jax.nn.sigmoid(g) * u).astype(BF16)
    o_ref[0] += (0.5 * gate_ref[0]) * _dot(act, wo_ref[...])

    if final_norm:
        @pl.when(f == pl.num_programs(2) - 1)
        def _():
            o_ref[0] = _rms(o_ref[0], fg_ref[...], RMS_EPS)


def _ffn(h, g, shift, scale, gate, w_in, w_out, final_g, *, final_norm):
    b, s, d = h.shape
    ff = w_out.shape[0]
    tm = _divisor_tile(s, FFN_TOKEN_TILE)
    tf = _divisor_tile(ff, FFN_HIDDEN_TILE)
    nf = ff // tf
    vec = pl.BlockSpec((1, 1, d), lambda bi, i, f: (bi, 0, 0))
    row = pl.BlockSpec((1, d), lambda bi, i, f: (0, 0))
    tile = pl.BlockSpec((1, tm, d), lambda bi, i, f: (bi, i, 0))
    return pl.pallas_call(
        functools.partial(_ffn_kernel, final_norm=final_norm),
        out_shape=jax.ShapeDtypeStruct((b, s, d), F32),
        grid=(b, s // tm, nf),
        in_specs=[
            tile, row, vec, vec, vec,
            pl.BlockSpec((d, tf), lambda bi, i, f: (0, f)),
            pl.BlockSpec((d, tf), lambda bi, i, f: (0, nf + f)),
            pl.BlockSpec((tf, d), lambda bi, i, f: (f, 0)),
            row,
        ],
        out_specs=tile,
        scratch_shapes=[pltpu.VMEM((tm, d), BF16)],
        compiler_params=_params("parallel", "parallel", "arbitrary"),
        name="ffn",
    )(h, g, shift, scale, gate, w_in, w_in, w_out, final_g)


def _proj_kernel(h_ref, g_ref, shift_ref, scale_ref, w_ref, o_ref, hn_ref):
    @pl.when(pl.program_id(2) == 0)
    def _():
        y = _rms(h_ref[0], g_ref[...], RMS_EPS)
        hn_ref[...] = (y * (1.0 + scale_ref[0]) + shift_ref[0]).astype(BF16)

    o_ref[0] = _dot(hn_ref[...], w_ref[...])


def _proj(h, g, shift, scale, w):
    b, s, d = h.shape
    p = w.shape[1]
    tm = _divisor_tile(s, 1024)
    tn = _divisor_tile(p, 1024)
    vec = pl.BlockSpec((1, 1, d), lambda bi, i, n: (bi, 0, 0))
    return pl.pallas_call(
        _proj_kernel,
        out_shape=jax.ShapeDtypeStruct((b, s, p), F32),
        grid=(b, s // tm, p // tn),
        in_specs=[
            pl.BlockSpec((1, tm, d), lambda bi, i, n: (bi, i, 0)),
            pl.BlockSpec((1, d), lambda bi, i, n: (0, 0)),
            vec, vec,
            pl.BlockSpec((d, tn), lambda bi, i, n: (0, n)),
        ],
        out_specs=pl.BlockSpec((1, tm, tn), lambda bi, i, n: (bi, i, n)),
        scratch_shapes=[pltpu.VMEM((tm, d), BF16)],
        compiler_params=_params("parallel", "parallel", "arbitrary"),
        name="proj",
    )(h, g, shift, scale, w)


def _outproj_kernel(oa_ref, or_ref, h_ref, gate_ref, wa_ref, wr_ref, o_ref):
    acc = _dot(oa_ref[0], wa_ref[...]) + _dot(or_ref[0], wr_ref[...])
    o_ref[0] = h_ref[0] + gate_ref[0] * acc


def _outproj(o_a, o_r, h, gate, w_out):
    b, s, d = h.shape
    tm = _divisor_tile(s, 1024)
    tn = _divisor_tile(d, 1024)
    wa = o_a.shape[-1]
    wr = o_r.shape[-1]
    return pl.pallas_call(
        _outproj_kernel,
        out_shape=jax.ShapeDtypeStruct((b, s, d), F32),
        grid=(b, s // tm, d // tn),
        in_specs=[
            pl.BlockSpec((1, tm, wa), lambda bi, i, n: (bi, i, 0)),
            pl.BlockSpec((1, tm, wr), lambda bi, i, n: (bi, i, 0)),
            pl.BlockSpec((1, tm, tn), lambda bi, i, n: (bi, i, n)),
            pl.BlockSpec((1, 1, tn), lambda bi, i, n: (bi, 0, n)),
            pl.BlockSpec((wa, tn), lambda bi, i, n: (0, n)),
            pl.BlockSpec((wr, tn), lambda bi, i, n: (wa // wr, n)),
        ],
        out_specs=pl.BlockSpec((1, tm, tn), lambda bi, i, n: (bi, i, n)),
        compiler_params=_params("parallel", "parallel", "arbitrary"),
        name="outproj",
    )(o_a, o_r, h, gate, w_out, w_out)


def _dsa_prep_kernel(q_ref, qi_ref, ckv_ref, kiwi_ref, g_ref, qt_ref, qit_ref, ckv_o, ckvt_o, ki_o, wit_o):
    qt_ref[0] = q_ref[0].T.astype(BF16)
    qit_ref[0] = qi_ref[0].T.astype(BF16)
    cn = _rms(ckv_ref[0], g_ref[...], RMS_EPS)
    ckv_o[0] = cn.astype(BF16)
    ckvt_o[0, 0] = jnp.concatenate([cn.T.astype(BF16), jnp.ones((SUM_ROWS, cn.shape[0]), BF16)], axis=0)
    kw = kiwi_ref[0]
    ki_o[0] = kw[:, :IDX_DIM].astype(BF16)
    wit_o[0] = kw.T[IDX_DIM:IDX_DIM + IDX_HEADS, :] * (IDX_HEADS * IDX_DIM) ** -0.5


def _dsa_prep(z, ckv_g):
    b, s, _ = z.shape
    t = DSA_TILE
    nt = s // t

    def seg(width, off):
        return pl.BlockSpec((1, t, width), lambda bi, i: (bi, i, off // width))

    return pl.pallas_call(
        _dsa_prep_kernel,
        out_shape=(
            jax.ShapeDtypeStruct((b, A_WIDTH, s), BF16),
            jax.ShapeDtypeStruct((b, IDX_HEADS * IDX_DIM, s), BF16),
            jax.ShapeDtypeStruct((b, s, KV_RANK), BF16),
            jax.ShapeDtypeStruct((b, nt, KV_RANK + SUM_ROWS, t), BF16),
            jax.ShapeDtypeStruct((b, s, IDX_DIM), BF16),
            jax.ShapeDtypeStruct((b, IDX_HEADS, s), F32),
        ),
        grid=(b, nt),
        in_specs=[
            seg(A_WIDTH, Z_Q), seg(IDX_HEADS * IDX_DIM, Z_QI), seg(KV_RANK, Z_CKV), seg(LORA_PAD, Z_KIWI),
            pl.BlockSpec((1, KV_RANK), lambda bi, i: (0, 0)),
        ],
        out_specs=(
            pl.BlockSpec((1, A_WIDTH, t), lambda bi, i: (bi, 0, i)),
            pl.BlockSpec((1, IDX_HEADS * IDX_DIM, t), lambda bi, i: (bi, 0, i)),
            pl.BlockSpec((1, t, KV_RANK), lambda bi, i: (bi, i, 0)),
            pl.BlockSpec((1, 1, KV_RANK + SUM_ROWS, t), lambda bi, i: (bi, i, 0, 0)),
            pl.BlockSpec((1, t, IDX_DIM), lambda bi, i: (bi, i, 0)),
            pl.BlockSpec((1, IDX_HEADS, t), lambda bi, i: (bi, 0, i)),
        ),
        compiler_params=_params("parallel", "parallel"),
        name="dsa_prep",
    )(z, z, z, z, ckv_g)


def _bias_kernel(t5_ref, o_ref):
    t = DSA_TILE
    j = lax.broadcasted_iota(I32, (t, t), 0)
    i = lax.broadcasted_iota(I32, (t, t), 1)
    for didx in range(2):
        n = jnp.maximum(didx * t + i - j, 0)
        nf = jnp.maximum(n, 1).astype(F32)
        large = REL_MAX_EXACT + (jnp.log(nf / REL_MAX_EXACT) / math.log(REL_MAX_DIST / REL_MAX_EXACT)
                                 * (REL_BUCKETS - REL_MAX_EXACT)).astype(I32)
        large = jnp.minimum(large, REL_BUCKETS - 1)
        bucket = jnp.where(n < REL_MAX_EXACT, n, large)
        for h in range(A_HEADS):
            val = jnp.zeros((t, t), F32)
            for k in range(REL_BUCKETS):
                val = jnp.where(bucket == k, t5_ref[k, h], val)
            o_ref[didx, :, h * t:(h + 1) * t] = (val - t5_ref[REL_BUCKETS - 1, h]) * LOG2E


def _bias_tiles(t5_bias):
    t = DSA_TILE
    assert t + 1 >= FAR_DISTANCE
    return pl.pallas_call(
        _bias_kernel,
        out_shape=jax.ShapeDtypeStruct((2, t, A_HEADS * t), F32),
        in_specs=[pl.BlockSpec(memory_space=pltpu.SMEM)],
        out_specs=pl.BlockSpec(memory_space=pltpu.VMEM),
        compiler_params=pltpu.CompilerParams(vmem_limit_bytes=VMEM_LIMIT_BYTES),
        name="t5_bias_tiles",
    )(t5_bias)


def _dsa_kernel(qt_ref, qit_ref, ckv_ref, ckvt_ref, ki_ref, wit_ref, wuk_ref, wuvt_ref, bias_ref,
                o_ref, keys_ref, khi_ref, klo_ref, acc_ref, ot_ref, *, topk, seq_len):
    t = DSA_TILE
    qb = pl.program_id(1)
    nk = qb + 1
    row = lax.broadcasted_iota(I32, (t, t), 0)
    col = lax.broadcasted_iota(I32, (t, t), 1)

    def idx_body(kc, carry):
        kic = ki_ref[0, pl.ds(pl.multiple_of(kc * t, t), t), :]
        acc = jnp.zeros((t, t), F32)
        for h in range(IDX_HEADS):
            rel = _dot(kic, qit_ref[0, h * IDX_DIM:(h + 1) * IDX_DIM, :])
            acc = acc + jnp.maximum(rel, 0.0) * wit_ref[0, h:h + 1, :]
        acc = acc + 0.0
        bits = pltpu.bitcast(acc, I32)
        key = bits ^ ((bits >> 31) & 0x7FFFFFFF)
        causal = (kc * t + row) <= (qb * t + col)
        key = jnp.where(causal, key, KEY_NEG_INF)
        sl = pl.ds(pl.multiple_of(kc * t, t), t)
        keys_ref[sl, :] = key
        khi_ref[sl, :] = (key >> 16).astype(I16)
        klo_ref[sl, :] = ((key & 0xFFFF) + I16_MIN).astype(I16)
        return carry

    lax.fori_loop(0, nk, idx_body, 0)

    def count16(ref, cand):
        cand16 = cand.astype(I16)

        def body(kc, acc):
            x = ref[pl.ds(pl.multiple_of(kc * t, t), t), :]
            ones = jnp.where(x >= cand16, jnp.asarray(1, BF16), jnp.asarray(0, BF16))
            ones = ones.reshape(t // PACKED_ROWS, PACKED_ROWS, t)
            for i in range(t // PACKED_ROWS):
                acc = acc + ones[i]
            return acc

        acc = lax.fori_loop(0, nk // 2, lambda i, acc: body(2 * i + 1, body(2 * i, acc)),
                            jnp.zeros((PACKED_ROWS, t), BF16))
        acc = lax.fori_loop(2 * (nk // 2), nk, body, acc)
        return jnp.sum(acc.astype(F32), axis=0, keepdims=True).astype(I32)

    def search16(ref, base):
        def accept(cand, cur):
            return jnp.where(base + count16(ref, cand) >= topk, cand, cur)

        v = accept(jnp.zeros((1, t), I32), jnp.full((1, t), I16_MIN, I32))
        return lax.fori_loop(0, 15, lambda i, v: accept(v | (jnp.int32(1) << (14 - i)), v), v)

    hi = search16(khi_ref, 0)
    above = count16(khi_ref, jnp.minimum(hi + 1, -I16_MIN - 1))
    hi16 = hi.astype(I16)

    def keep_low(kc, carry):
        sl = pl.ds(pl.multiple_of(kc * t, t), t)
        klo_ref[sl, :] = jnp.where(khi_ref[sl, :] == hi16, klo_ref[sl, :], jnp.asarray(I16_MIN, I16))
        return carry

    lax.fori_loop(0, nk, keep_low, 0)
    lo = search16(klo_ref, above)
    thr = jnp.maximum((hi << 16) | (lo - I16_MIN), KEY_NEG_INF + 1)

    def count(pred):
        def body(kc, acc):
            k = keys_ref[pl.ds(pl.multiple_of(kc * t, t), t), :]
            m = jnp.where(pred(k, kc), 1, 0).astype(I32)
            return acc + jnp.sum(m.reshape(t // SUBLANES, SUBLANES, t), axis=0)

        acc = lax.fori_loop(0, nk, body, jnp.zeros((SUBLANES, t), I32))
        return jnp.sum(acc, axis=0, keepdims=True)

    @pl.when(jnp.max(count(lambda k, kc: k >= thr)) > topk)
    def _():
        need = topk - count(lambda k, kc: k > thr)
        nbits = (seq_len - 1).bit_length()

        def pos_bit(i, y):
            cand = y | (jnp.int32(1) << (nbits - 1 - i))
            before = count(lambda k, kc: (k == thr) & ((kc * t + row) < cand))
            return jnp.where(before < need, cand, y)

        last_kept = lax.fori_loop(0, nbits, pos_bit, jnp.zeros((1, t), I32))

        def demote(kc, carry):
            sl = pl.ds(pl.multiple_of(kc * t, t), t)
            k = keys_ref[sl, :]
            keys_ref[sl, :] = jnp.where((k == thr) & ((kc * t + row) > last_kept), k - 1, k)
            return carry

        lax.fori_loop(0, nk, demote, 0)

    scale = A_HEAD_DIM ** -0.5 * LOG2E

    hw = A_HEADS * t
    qlat = jnp.concatenate(
        [(_dot(wuk_ref[h], qt_ref[0, h * A_HEAD_DIM:(h + 1) * A_HEAD_DIM, :]) * scale).astype(BF16)
         for h in range(A_HEADS)], axis=1)
    acc_ref[...] = jnp.zeros_like(acc_ref)

    def att_body(kc, m, *, near):
        sl = pl.ds(pl.multiple_of(kc * t, t), t)
        raw = _dot(ckv_ref[0, sl, :], qlat)
        sel = keys_ref[sl, :] >= thr
        parts = []
        for h in range(A_HEADS):
            lg_h = raw[:, h * t:(h + 1) * t]
            if near:
                lg_h = lg_h + bias_ref[qb - kc, :, h * t:(h + 1) * t]
            parts.append(jnp.where(sel, lg_h, MASKED_LOGIT).astype(BF16))
        lg = jnp.concatenate(parts, axis=1)
        m_new = jnp.maximum(m, jnp.max(lg, axis=0, keepdims=True).astype(F32))
        p = jnp.exp2(lg - m_new.astype(BF16))
        acc_ref[...] = acc_ref[...] * jnp.exp2(m - m_new) + _dot(ckvt_ref[0, kc], p)
        return m_new

    n_far = jnp.maximum(qb - 1, 0)
    m = lax.fori_loop(0, n_far, functools.partial(att_body, near=False), jnp.full((1, hw), MASKED_LOGIT, F32))
    lax.fori_loop(n_far, nk, functools.partial(att_body, near=True), m)
    olat = (acc_ref[:KV_RANK, :] / acc_ref[KV_RANK:KV_RANK + 1, :]).astype(BF16)
    for h in range(A_HEADS):
        ot_ref[h * A_HEAD_DIM:(h + 1) * A_HEAD_DIM, :] = _dot(wuvt_ref[h], olat[:, h * t:(h + 1) * t])
    o_ref[0] = ot_ref[...].T.astype(BF16)


def _dsa(qt, qit, ckv, ckvt, ki, wit, wuk, wuvt, bias):
    b, _, s = qt.shape
    t = DSA_TILE
    nt = s // t
    topk = min(TOPK_MAX, s // 4)
    assert s // PACKED_ROWS <= BF16_EXACT_INT
    return pl.pallas_call(
        functools.partial(_dsa_kernel, topk=topk, seq_len=s),
        out_shape=jax.ShapeDtypeStruct((b, s, A_WIDTH), BF16),
        grid=(b, nt),
        in_specs=[
            pl.BlockSpec((1, A_WIDTH, t), lambda bi, i: (bi, 0, i)),
            pl.BlockSpec((1, IDX_HEADS * IDX_DIM, t), lambda bi, i: (bi, 0, i)),
            pl.BlockSpec((1, s, KV_RANK), lambda bi, i: (bi, 0, 0)),
            pl.BlockSpec((1, nt, KV_RANK + SUM_ROWS, t), lambda bi, i: (bi, 0, 0, 0)),
            pl.BlockSpec((1, s, IDX_DIM), lambda bi, i: (bi, 0, 0)),
            pl.BlockSpec((1, IDX_HEADS, t), lambda bi, i: (bi, 0, i)),
            pl.BlockSpec((A_HEADS, KV_RANK, A_HEAD_DIM), lambda bi, i: (0, 0, 0)),
            pl.BlockSpec((A_HEADS, A_HEAD_DIM, KV_RANK), lambda bi, i: (0, 0, 0)),
            pl.BlockSpec((2, t, A_HEADS * t), lambda bi, i: (0, 0, 0)),
        ],
        out_specs=pl.BlockSpec((1, t, A_WIDTH), lambda bi, i: (bi, i, 0)),
        scratch_shapes=[pltpu.VMEM((s, t), I32), pltpu.VMEM((s, t), I16), pltpu.VMEM((s, t), I16),
                        pltpu.VMEM((KV_RANK + SUM_ROWS, A_HEADS * t), F32), pltpu.VMEM((A_WIDTH, t), F32)],
        compiler_params=_params("parallel", "arbitrary"),
        name="dsa",
    )(qt, qit, ckv, ckvt, ki, wit, wuk, wuvt, bias)


def _split3(x):
    hi = x.astype(BF16)
    r1 = x - hi.astype(F32)
    mid = r1.astype(BF16)
    lo = (r1 - mid.astype(F32)).astype(BF16)
    return hi, mid, lo


def _rwkv_kernel(r_ref, k_ref, v_ref, gl_ref, wl_ref, al_ref,
                 mur_ref, muk_ref, muv_ref, mug_ref, muw_ref, mua_ref,
                 w0_ref, a0_ref, kk_ref, ka_ref, rk_ref, lnw_ref, lnb_ref,
                 w2_ref, a2_ref, g2_ref,
                 o_ref,
                 pr_ref, pk_ref, pv_ref, pg_ref, pw_ref, pa_ref, state_ref):
    tl = RWKV_TILE
    c = RWKV_CHUNK
    nc = tl // c
    npair = R_WIDTH // PAIR

    @pl.when(pl.program_id(1) == 0)
    def _():
        state_ref[...] = jnp.zeros_like(state_ref)
        for ref in (pr_ref, pk_ref, pv_ref, pg_ref, pw_ref, pa_ref):
            ref[...] = jnp.zeros_like(ref)

    def shift_mix(x_ref, prev_ref, mu_ref):
        x = x_ref[0]
        first = lax.broadcasted_iota(I32, x.shape, 0) == 0
        xprev = jnp.where(first, prev_ref[...], pltpu.roll(x, 1, axis=0))
        prev_ref[...] = x[tl - 1:tl, :]
        return x + (xprev - x) * mu_ref[...]

    r = shift_mix(r_ref, pr_ref, mur_ref)
    k = shift_mix(k_ref, pk_ref, muk_ref)
    v = shift_mix(v_ref, pv_ref, muv_ref)
    gl = shift_mix(gl_ref, pg_ref, mug_ref)
    wl = shift_mix(wl_ref, pw_ref, muw_ref)
    al = shift_mix(al_ref, pa_ref, mua_ref)

    w_lin = w0_ref[...] + _dot(jnp.tanh(wl).astype(BF16), w2_ref[...])
    nx = -w_lin
    softplus = jnp.maximum(nx, 0.0) + jnp.log(1.0 + jnp.exp(-jnp.abs(nx)))
    ld = -jnp.exp(-softplus - 0.5)
    a = jax.nn.sigmoid(a0_ref[...] + _dot(al.astype(BF16), a2_ref[...]))
    gate = _dot(jax.nn.sigmoid(gl).astype(BF16), g2_ref[...])

    li = lax.broadcasted_iota(I32, (LANES, LANES), 0) // R_HEAD_DIM
    lj = lax.broadcasted_iota(I32, (LANES, LANES), 1) // R_HEAD_DIM
    head_ones = jnp.where(li == lj, 1.0, 0.0).astype(BF16)

    def head_sum(x):
        parts = []
        for j in range(R_WIDTH // LANES):
            hi, mid, lo = _split3(x[:, j * LANES:(j + 1) * LANES])
            parts.append(_dot(hi, head_ones) + _dot(mid, head_ones) + _dot(lo, head_ones))
        return jnp.concatenate(parts, axis=-1)

    kk = k * kk_ref[...]
    kk = kk / jnp.maximum(jnp.sqrt(head_sum(kk * kk)), 1e-12)
    k2 = k * (1.0 + (a - 1.0) * ka_ref[...])
    bb = kk * a

    ti = lax.broadcasted_iota(I32, (tl, tl), 0)
    tj = lax.broadcasted_iota(I32, (tl, tl), 1)
    tri = jnp.where((ti // c == tj // c) & (tj <= ti), 1.0, 0.0).astype(BF16)
    hi, mid, lo = _split3(ld)
    cum = _dot(tri, hi) + _dot(tri, mid) + _dot(tri, lo)
    cum_end = jnp.broadcast_to(cum.reshape(nc, c, R_WIDTH)[:, c - 1:c, :], (nc, c, R_WIDTH)).reshape(tl, R_WIDTH)
    e_in = jnp.exp(cum)
    e_neg = jnp.exp(-cum)
    e_out = jnp.exp(cum_end - cum)

    nb = nc * npair

    def to_pairs(x):
        x3 = x.reshape(nc, c, R_WIDTH)
        xs = jnp.stack([x3[:, :, p * PAIR:(p + 1) * PAIR] for p in range(npair)], axis=1)
        return xs.reshape(nb, c, PAIR)

    lane_head = lax.broadcasted_iota(I32, (2 * c, PAIR), 1) // R_HEAD_DIM
    row_head = lax.broadcasted_iota(I32, (2 * c, PAIR), 0) // c
    same = lane_head == row_head
    rt = lax.broadcasted_iota(I32, (2 * c, PAIR), 0) % c
    ct = lax.broadcasted_iota(I32, (2 * c, PAIR), 1) % c
    strict = same & (ct < rt)
    incl = same & (ct <= rt)
    eye = jnp.where(same & (ct == rt), 1.0, 0.0).astype(F32)

    def bd(x):
        return jnp.where(same, jnp.concatenate([x, x], axis=1), 0.0)

    lane0 = lax.broadcasted_iota(I32, (c, PAIR), 1) < R_HEAD_DIM

    abar = bd(to_pairs(-kk * jnp.exp(cum - ld))).astype(BF16)
    rbar = bd(to_pairs(r * e_in)).astype(BF16)
    bt = bd(to_pairs(bb * e_neg)).astype(BF16)
    kt = bd(to_pairs(k2 * e_neg)).astype(BF16)
    bk = jnp.concatenate([bd(to_pairs(bb * e_out)), bd(to_pairs(k2 * e_out))], axis=1).astype(BF16)
    v_pl = to_pairs(v)
    vbd = bd(v_pl)
    decay_end = to_pairs(jnp.exp(cum_end))[:, 0:1, :]
    aa = _bdot_nt(jnp.concatenate([abar, rbar], axis=1), jnp.concatenate([bt, kt], axis=1))
    a_ab = jnp.where(strict, aa[:, :2 * c, :PAIR], 0.0)
    a_ak = jnp.where(strict, aa[:, :2 * c, PAIR:], 0.0).astype(BF16)
    a_rb = jnp.where(incl, aa[:, 2 * c:, :PAIR], 0.0).astype(BF16)
    a_rk = jnp.where(incl, aa[:, 2 * c:, PAIR:], 0.0).astype(BF16)
    pw = a_ab.astype(BF16)
    tinv = eye + a_ab
    for _ in range(int(math.log2(c)) - 1):
        pw = _bdot(pw, pw).astype(BF16)
        tinv = tinv + _bdot(tinv.astype(BF16), pw)
    akv = _bdot(a_ak, vbd.astype(BF16))
    wu = _bdot(tinv.astype(BF16), jnp.concatenate([abar, akv.astype(BF16)], axis=2))
    w16 = wu[:, :, :PAIR].astype(BF16)
    u_bd = wu[:, :, PAIR:]

    s_bd = state_ref[...]
    e_parts, rs_parts = [], []
    for ci in range(nc):
        sl = slice(ci * npair, (ci + 1) * npair)
        wr = _bdot_nt(jnp.concatenate([w16[sl], rbar[sl]], axis=1), s_bd.astype(BF16))
        e_bd = wr[:, :2 * c] + u_bd[sl]
        e_parts.append(e_bd)
        rs_parts.append(wr[:, 2 * c:])
        ev = jnp.concatenate([e_bd, vbd[sl]], axis=1)
        ds = _bdot(jnp.swapaxes(ev, 1, 2).astype(BF16), bk[sl])
        s_bd = s_bd * decay_end[sl] + jnp.where(same, ds, 0.0)
    state_ref[...] = s_bd

    ev_all = jnp.concatenate([jnp.concatenate(e_parts, axis=0), vbd], axis=1).astype(BF16)
    y_bd = jnp.concatenate(rs_parts, axis=0) + _bdot(jnp.concatenate([a_rb, a_rk], axis=2), ev_all)
    mean = jnp.sum(y_bd, axis=-1, keepdims=True) * (1.0 / R_HEAD_DIM)
    dev = jnp.where(same, y_bd - mean, 0.0)
    var = jnp.sum(dev * dev, axis=-1, keepdims=True) * (1.0 / R_HEAD_DIM)
    yn = dev * lax.rsqrt(var + GN_EPS)
    yn = yn[:, :c] + yn[:, c:]
    rkv = to_pairs(r * k2 * rk_ref[...])
    s0 = jnp.sum(jnp.where(lane0, rkv, 0.0), axis=-1, keepdims=True)
    s1 = jnp.sum(jnp.where(lane0, 0.0, rkv), axis=-1, keepdims=True)
    bonus = jnp.where(lane0, s0, s1) * v_pl
    def per_pair(row_ref):
        rows = [row_ref[:, p * PAIR:(p + 1) * PAIR] for p in range(npair)]
        return jnp.stack(rows * nc, axis=0)

    out = ((yn * per_pair(lnw_ref) + per_pair(lnb_ref) + bonus) * to_pairs(gate)).astype(BF16)
    for ci in range(nc):
        for p in range(npair):
            o_ref[0, ci * c:(ci + 1) * c, p * PAIR:(p + 1) * PAIR] = out[ci * npair + p]


def _rwkv(z, prm):
    b, s, _ = z.shape
    tl = RWKV_TILE

    def seg(width, off):
        return pl.BlockSpec((1, tl, width), lambda bi, i: (bi, i, off // width))

    def full(arr):
        return pl.BlockSpec(arr.shape, lambda bi, i: (0,) * arr.ndim)

    names = ("mu_r", "mu_k", "mu_v", "mu_g", "mu_w", "mu_a", "w0", "a0", "k_k", "k_a", "r_k", "ln_w", "ln_b",
             "w2", "a2", "g2")
    consts = [prm[n] for n in names]
    return pl.pallas_call(
        _rwkv_kernel,
        out_shape=jax.ShapeDtypeStruct((b, s, R_WIDTH), BF16),
        grid=(b, s // tl),
        in_specs=[seg(R_WIDTH, Z_R), seg(R_WIDTH, Z_K), seg(R_WIDTH, Z_V), seg(GATE_LORA, Z_GL),
                  seg(LORA_PAD, Z_WL), seg(LORA_PAD, Z_AL)] + [full(x) for x in consts],
        out_specs=pl.BlockSpec((1, tl, R_WIDTH), lambda bi, i: (bi, i, 0)),
        scratch_shapes=[
            pltpu.VMEM((1, R_WIDTH), F32), pltpu.VMEM((1, R_WIDTH), F32), pltpu.VMEM((1, R_WIDTH), F32),
            pltpu.VMEM((1, GATE_LORA), F32), pltpu.VMEM((1, LORA_PAD), F32), pltpu.VMEM((1, LORA_PAD), F32),
            pltpu.VMEM((R_WIDTH // PAIR, PAIR, PAIR), F32),
        ],
        compiler_params=_params("parallel", "arbitrary"),
        name="rwkv7",
    )(z, z, z, z, z, z, *consts)


def _pad_cols(w, width):
    return jnp.pad(w, ((0, 0), (0, width - w.shape[1])))


def _layout_w_in(w):
    offs = [0]
    for n in (A_WIDTH, KV_RANK, IDX_HEADS * IDX_DIM, IDX_DIM, IDX_HEADS,
              R_WIDTH, R_WIDTH, R_WIDTH, DECAY_LORA, AAA_LORA, GATE_LORA):
        offs.append(offs[-1] + n)
    q, ckv, qi, ki, wi, r, k, v, wl, al, gl = (w[:, offs[i]:offs[i + 1]] for i in range(11))
    cols = [q, qi, r, k, v, gl, ckv, _pad_cols(jnp.concatenate([ki, wi], axis=1), LORA_PAD),
            _pad_cols(wl, LORA_PAD), _pad_cols(al, LORA_PAD)]
    return _pad_cols(jnp.concatenate(cols, axis=1), Z_WIDTH).astype(BF16)


def _pad_rows(w, rows):
    return jnp.pad(w, ((0, rows - w.shape[0]), (0, 0)))


def _rwkv_params(l, mu, w0, w2, a0, a2, g2, k_k, k_a, r_k, ln_w, ln_b):
    m = mu[l]
    o = [0, R_WIDTH, 2 * R_WIDTH, 3 * R_WIDTH, 3 * R_WIDTH + DECAY_LORA, 3 * R_WIDTH + DECAY_LORA + AAA_LORA]
    row = lambda x: x.reshape(1, -1)
    return {
        "mu_r": row(m[o[0]:o[1]]), "mu_k": row(m[o[1]:o[2]]), "mu_v": row(m[o[2]:o[3]]),
        "mu_w": _pad_cols(row(m[o[3]:o[4]]), LORA_PAD), "mu_a": _pad_cols(row(m[o[4]:o[5]]), LORA_PAD),
        "mu_g": row(m[o[5]:]),
        "w0": row(w0[l]), "a0": row(a0[l]), "k_k": row(k_k[l]), "k_a": row(k_a[l]), "r_k": row(r_k[l]),
        "ln_w": row(ln_w[l]), "ln_b": row(ln_b[l]),
        "w2": _pad_rows(w2[l], LORA_PAD).astype(BF16), "a2": _pad_rows(a2[l], LORA_PAD).astype(BF16),
        "g2": g2[l].astype(BF16),
    }


def kernel(x, c, t5_bias, ada_w, ada_b, norm_g, ffn_w_in, ffn_w_out, w_in, ckv_norm_g, w_uk, w_uv, rwkv_mu, rwkv_w0, rwkv_w2, rwkv_a0, rwkv_a2, rwkv_g2, rwkv_k_k, rwkv_k_a, rwkv_r_k, rwkv_ln_w, rwkv_ln_b, w_out, final_norm_g):
    b, s, d = x.shape
    depth = ada_w.shape[0]
    assert s % DSA_TILE == 0 and s % RWKV_TILE == 0 and d == A_WIDTH + R_WIDTH
    mod = _ada_mod(c, ada_w, ada_b).reshape(depth, b, N_SUB, 3, 1, d)
    bias = _bias_tiles(t5_bias)
    final_g = final_norm_g.reshape(1, d)
    h = x
    for l in range(depth):
        shift = lambda i: mod[l, :, i, 0]
        scale = lambda i: mod[l, :, i, 1]
        gate = lambda i: mod[l, :, i, 2]
        g = lambda i: norm_g[l, i].reshape(1, d)
        h = _ffn(h, g(0), shift(0), scale(0), gate(0), ffn_w_in[l, 0].astype(BF16), ffn_w_out[l, 0].astype(BF16),
                 final_g, final_norm=False)
        z = _proj(h, g(1), shift(1), scale(1), _layout_w_in(w_in[l]))
        qt, qit, ckv, ckvt, ki, wit = _dsa_prep(z, ckv_norm_g[l].reshape(1, KV_RANK))
        wuk = jnp.transpose(w_uk[l], (1, 0, 2)).astype(BF16)
        wuvt = jnp.transpose(w_uv[l], (1, 2, 0)).astype(BF16)
        o_a = _dsa(qt, qit, ckv, ckvt, ki, wit, wuk, wuvt, bias)
        o_r = _rwkv(z, _rwkv_params(l, rwkv_mu, rwkv_w0, rwkv_w2, rwkv_a0, rwkv_a2, rwkv_g2, rwkv_k_k,
                                    rwkv_k_a, rwkv_r_k, rwkv_ln_w, rwkv_ln_b))
        h = _outproj(o_a, o_r, h, gate(1), w_out[l].astype(BF16))
        h = _ffn(h, g(2), shift(2), scale(2), gate(2), ffn_w_in[l, 1].astype(BF16), ffn_w_out[l, 1].astype(BF16),
                 final_g, final_norm=(l == depth - 1))
    return h
```

```python
import functools
import math

import jax
import jax.numpy as jnp
from jax import lax
from jax.experimental import pallas as pl
from jax.experimental.pallas import tpu as pltpu

F32 = jnp.float32
BF16 = jnp.bfloat16
I32 = jnp.int32
I16 = jnp.int16

A_HEADS = 8
A_HEAD_DIM = 128
A_WIDTH = A_HEADS * A_HEAD_DIM
KV_RANK = 256
IDX_HEADS = 16
IDX_DIM = 64
TOPK_MAX = 256
REL_BUCKETS = 32
REL_MAX_EXACT = REL_BUCKETS // 2
REL_MAX_DIST = 128
R_HEAD_DIM = 64
R_WIDTH = 1024
R_HEADS = R_WIDTH // R_HEAD_DIM
DECAY_LORA = 96
AAA_LORA = 96
GATE_LORA = 256
GN_EPS = 64e-5
RMS_EPS = 1e-6
N_SUB = 3

LANES = 128
SUBLANES = 8
VMEM_LIMIT_BYTES = 56 * 1024 * 1024

LORA_PAD = 128
Z_Q = 0
Z_QI = Z_Q + A_WIDTH
Z_R = Z_QI + IDX_HEADS * IDX_DIM
Z_K = Z_R + R_WIDTH
Z_V = Z_K + R_WIDTH
Z_GL = Z_V + R_WIDTH
Z_CKV = Z_GL + GATE_LORA
Z_KIWI = Z_CKV + KV_RANK
Z_WL = Z_KIWI + LORA_PAD
Z_AL = Z_WL + LORA_PAD
Z_END = Z_AL + LORA_PAD
Z_WIDTH = 6144

FFN_TOKEN_TILE = 1024
FFN_HIDDEN_TILE = 256
ADALN_SLAB_ROWS = 16
ADALN_UNROLL = 8
DSA_TILE = 256
RWKV_CHUNK = 64
RWKV_TILE = 256
PAIR = 2 * R_HEAD_DIM

I16_MIN = -(2 ** 15)
PACKED_ROWS = 2 * SUBLANES
SUM_ROWS = PACKED_ROWS
BF16_EXACT_INT = 256
KEY_NEG_INF = -2139095041
MASKED_LOGIT = -(2.0 ** 100)
LOG2E = math.log2(math.e)
FAR_DISTANCE = math.ceil(REL_MAX_EXACT * (REL_MAX_DIST / REL_MAX_EXACT)
                         ** ((REL_BUCKETS - 1 - REL_MAX_EXACT) / (REL_BUCKETS - REL_MAX_EXACT)))


def _dot(a, b):
    return jnp.dot(a, b, preferred_element_type=F32)


def _dot_nt(a, b):
    return lax.dot_general(a, b, (((1,), (1,)), ((), ())), preferred_element_type=F32)


def _bdot(a, b):
    return lax.dot_general(a, b, (((2,), (1,)), ((0,), (0,))), preferred_element_type=F32)


def _bdot_nt(a, b):
    return lax.dot_general(a, b, (((2,), (2,)), ((0,), (0,))), preferred_element_type=F32)


def _rms(x, g, eps):
    ms = jnp.mean(x * x, axis=-1, keepdims=True)
    return x * lax.rsqrt(ms + eps) * g


def _divisor_tile(n, pref):
    if n <= pref:
        return n
    t = (pref // LANES) * LANES
    while t > LANES and n % t:
        t -= LANES
    assert n % t == 0, (n, pref)
    return t


def _params(*sem):
    return pltpu.CompilerParams(dimension_semantics=sem, vmem_limit_bytes=VMEM_LIMIT_BYTES)


def _ada_kernel(c_ref, w_ref, b_ref, o_ref):
    c = c_ref[...]
    ca = (c * jax.nn.sigmoid(c)).astype(BF16)
    o_ref[0] = _dot(ca, w_ref[0].astype(BF16)) + b_ref[0]


def _ada_mod(c, ada_w, ada_b):
    depth, d, n = ada_w.shape
    b = c.shape[0]
    bp = -(-b // SUBLANES) * SUBLANES
    cp = jnp.pad(c, ((0, bp - b), (0, 0)))
    tn = _divisor_tile(n, 1024)
    out = pl.pallas_call(
        _ada_kernel,
        out_shape=jax.ShapeDtypeStruct((depth, bp, n), F32),
        grid=(depth, n // tn),
        in_specs=[
            pl.BlockSpec((bp, d), lambda l, j: (0, 0)),
            pl.BlockSpec((1, d, tn), lambda l, j: (l, 0, j)),
            pl.BlockSpec((1, 1, tn), lambda l, j: (l, 0, j)),
        ],
        out_specs=pl.BlockSpec((1, bp, tn), lambda l, j: (l, 0, j)),
        compiler_params=_params("arbitrary", "arbitrary"),
        name="ada_mod",
    )(cp, ada_w, ada_b.reshape(depth, 1, n))
    return out[:, :b]


def _adaln_rows(h_ref, g_ref, scale_ref, shift_ref, hn_ref, copy_ref=None):
    rows = h_ref.shape[1]
    g = g_ref[...]
    mul = 1.0 + scale_ref[0]
    add = shift_ref[0]

    def slab(i, carry):
        sl = pl.ds(pl.multiple_of(i * ADALN_SLAB_ROWS, ADALN_SLAB_ROWS), ADALN_SLAB_ROWS)
        h = h_ref[0, sl, :]
        hn_ref[sl, :] = (_rms(h, g, RMS_EPS) * mul + add).astype(BF16)
        if copy_ref is not None:
            copy_ref[0, sl, :] = h
        return carry

    lax.fori_loop(0, rows // ADALN_SLAB_ROWS, slab, 0, unroll=ADALN_UNROLL)


def _ffn_kernel(h_ref, g_ref, shift_ref, scale_ref, gate_ref, wg_ref, wu_ref, wo_ref, fg_ref,
                o_ref, hn_ref, *, final_norm):
    f = pl.program_id(2)

    @pl.when(f == 0)
    def _():
        _adaln_rows(h_ref, g_ref, scale_ref, shift_ref, hn_ref, o_ref)

    hn = hn_ref[...]
    g = _dot(hn, wg_ref[...])
    u = _dot(hn, wu_ref[...])
    act = (g * jax.nn.sigmoid(g) * u).astype(BF16)
    o_ref[0] += (0.5 * gate_ref[0]) * _dot(act, wo_ref[...])

    if final_norm:
        @pl.when(f == pl.num_programs(2) - 1)
        def _():
            fg = fg_ref[...]

            def slab(i, carry):
                sl = pl.ds(pl.multiple_of(i * ADALN_SLAB_ROWS, ADALN_SLAB_ROWS), ADALN_SLAB_ROWS)
                o_ref[0, sl, :] = _rms(o_ref[0, sl, :], fg, RMS_EPS)
                return carry

            lax.fori_loop(0, o_ref.shape[1] // ADALN_SLAB_ROWS, slab, 0, unroll=ADALN_UNROLL)


def _ffn(h, g, shift, scale, gate, w_in, w_out, final_g, *, final_norm):
    b, s, d = h.shape
    ff = w_out.shape[0]
    tm = _divisor_tile(s, FFN_TOKEN_TILE)
    tf = _divisor_tile(ff, FFN_HIDDEN_TILE)
    nf = ff // tf
    vec = pl.BlockSpec((1, 1, d), lambda bi, i, f: (bi, 0, 0))
    row = pl.BlockSpec((1, d), lambda bi, i, f: (0, 0))
    tile = pl.BlockSpec((1, tm, d), lambda bi, i, f: (bi, i, 0))
    return pl.pallas_call(
        functools.partial(_ffn_kernel, final_norm=final_norm),
        out_shape=jax.ShapeDtypeStruct((b, s, d), F32),
        grid=(b, s // tm, nf),
        in_specs=[
            tile, row, vec, vec, vec,
            pl.BlockSpec((d, tf), lambda bi, i, f: (0, f)),
            pl.BlockSpec((d, tf), lambda bi, i, f: (0, nf + f)),
            pl.BlockSpec((tf, d), lambda bi, i, f: (f, 0)),
            row,
        ],
        out_specs=tile,
        scratch_shapes=[pltpu.VMEM((tm, d), BF16)],
        compiler_params=_params("parallel", "parallel", "arbitrary"),
        name="ffn",
    )(h, g, shift, scale, gate, w_in, w_in, w_out, final_g)


def _proj_kernel(h_ref, g_ref, shift_ref, scale_ref, w_ref, o_ref, hn_ref):
    @pl.when(pl.program_id(2) == 0)
    def _():
        _adaln_rows(h_ref, g_ref, scale_ref, shift_ref, hn_ref)

    o_ref[0] = _dot(hn_ref[...], w_ref[...])


def _proj(h, g, shift, scale, w):
    b, s, d = h.shape
    p = w.shape[1]
    tm = _divisor_tile(s, 1024)
    tn = _divisor_tile(p, 1024)
    vec = pl.BlockSpec((1, 1, d), lambda bi, i, n: (bi, 0, 0))
    return pl.pallas_call(
        _proj_kernel,
        out_shape=jax.ShapeDtypeStruct((b, s, p), F32),
        grid=(b, s // tm, p // tn),
        in_specs=[
            pl.BlockSpec((1, tm, d), lambda bi, i, n: (bi, i, 0)),
            pl.BlockSpec((1, d), lambda bi, i, n: (0, 0)),
            vec, vec,
            pl.BlockSpec((d, tn), lambda bi, i, n: (0, n)),
        ],
        out_specs=pl.BlockSpec((1, tm, tn), lambda bi, i, n: (bi, i, n)),
        scratch_shapes=[pltpu.VMEM((tm, d), BF16)],
        compiler_params=_params("parallel", "parallel", "arbitrary"),
        name="proj",
    )(h, g, shift, scale, w)


def _outproj_kernel(oa_ref, or_ref, h_ref, gate_ref, wa_ref, wr_ref, o_ref):
    acc = _dot(oa_ref[0], wa_ref[...]) + _dot(or_ref[0], wr_ref[...])
    o_ref[0] = h_ref[0] + gate_ref[0] * acc


def _outproj(o_a, o_r, h, gate, w_out):
    b, s, d = h.shape
    tm = _divisor_tile(s, 1024)
    tn = _divisor_tile(d, 1024)
    wa = o_a.shape[-1]
    wr = o_r.shape[-1]
    return pl.pallas_call(
        _outproj_kernel,
        out_shape=jax.ShapeDtypeStruct((b, s, d), F32),
        grid=(b, s // tm, d // tn),
        in_specs=[
            pl.BlockSpec((1, tm, wa), lambda bi, i, n: (bi, i, 0)),
            pl.BlockSpec((1, tm, wr), lambda bi, i, n: (bi, i, 0)),
            pl.BlockSpec((1, tm, tn), lambda bi, i, n: (bi, i, n)),
            pl.BlockSpec((1, 1, tn), lambda bi, i, n: (bi, 0, n)),
            pl.BlockSpec((wa, tn), lambda bi, i, n: (0, n)),
            pl.BlockSpec((wr, tn), lambda bi, i, n: (wa // wr, n)),
        ],
        out_specs=pl.BlockSpec((1, tm, tn), lambda bi, i, n: (bi, i, n)),
        compiler_params=_params("parallel", "parallel", "arbitrary"),
        name="outproj",
    )(o_a, o_r, h, gate, w_out, w_out)


def _dsa_prep_kernel(q_ref, qi_ref, ckv_ref, kiwi_ref, g_ref, qt_ref, qit_ref, ckv_o, ckvt_o, ki_o, wit_o):
    qt_ref[0] = q_ref[0].T.astype(BF16)
    qit_ref[0] = qi_ref[0].T.astype(BF16)
    cn = _rms(ckv_ref[0], g_ref[...], RMS_EPS)
    ckv_o[0] = cn.astype(BF16)
    ckvt_o[0, 0] = jnp.concatenate([cn.T.astype(BF16), jnp.ones((SUM_ROWS, cn.shape[0]), BF16)], axis=0)
    kw = kiwi_ref[0]
    ki_o[0] = kw[:, :IDX_DIM].astype(BF16)
    wit_o[0] = kw.T[IDX_DIM:IDX_DIM + IDX_HEADS, :] * (IDX_HEADS * IDX_DIM) ** -0.5


def _dsa_prep(z, ckv_g):
    b, s, _ = z.shape
    t = DSA_TILE
    nt = s // t

    def seg(width, off):
        return pl.BlockSpec((1, t, width), lambda bi, i: (bi, i, off // width))

    return pl.pallas_call(
        _dsa_prep_kernel,
        out_shape=(
            jax.ShapeDtypeStruct((b, A_WIDTH, s), BF16),
            jax.ShapeDtypeStruct((b, IDX_HEADS * IDX_DIM, s), BF16),
            jax.ShapeDtypeStruct((b, s, KV_RANK), BF16),
            jax.ShapeDtypeStruct((b, nt, KV_RANK + SUM_ROWS, t), BF16),
            jax.ShapeDtypeStruct((b, s, IDX_DIM), BF16),
            jax.ShapeDtypeStruct((b, IDX_HEADS, s), F32),
        ),
        grid=(b, nt),
        in_specs=[
            seg(A_WIDTH, Z_Q), seg(IDX_HEADS * IDX_DIM, Z_QI), seg(KV_RANK, Z_CKV), seg(LORA_PAD, Z_KIWI),
            pl.BlockSpec((1, KV_RANK), lambda bi, i: (0, 0)),
        ],
        out_specs=(
            pl.BlockSpec((1, A_WIDTH, t), lambda bi, i: (bi, 0, i)),
            pl.BlockSpec((1, IDX_HEADS * IDX_DIM, t), lambda bi, i: (bi, 0, i)),
            pl.BlockSpec((1, t, KV_RANK), lambda bi, i: (bi, i, 0)),
            pl.BlockSpec((1, 1, KV_RANK + SUM_ROWS, t), lambda bi, i: (bi, i, 0, 0)),
            pl.BlockSpec((1, t, IDX_DIM), lambda bi, i: (bi, i, 0)),
            pl.BlockSpec((1, IDX_HEADS, t), lambda bi, i: (bi, 0, i)),
        ),
        compiler_params=_params("parallel", "parallel"),
        name="dsa_prep",
    )(z, z, z, z, ckv_g)


def _bias_kernel(t5_ref, o_ref):
    t = DSA_TILE
    j = lax.broadcasted_iota(I32, (t, t), 0)
    i = lax.broadcasted_iota(I32, (t, t), 1)
    for didx in range(2):
        n = jnp.maximum(didx * t + i - j, 0)
        nf = jnp.maximum(n, 1).astype(F32)
        large = REL_MAX_EXACT + (jnp.log(nf / REL_MAX_EXACT) / math.log(REL_MAX_DIST / REL_MAX_EXACT)
                                 * (REL_BUCKETS - REL_MAX_EXACT)).astype(I32)
        large = jnp.minimum(large, REL_BUCKETS - 1)
        bucket = jnp.where(n < REL_MAX_EXACT, n, large)
        for h in range(A_HEADS):
            val = jnp.zeros((t, t), F32)
            for k in range(REL_BUCKETS):
                val = jnp.where(bucket == k, t5_ref[k, h], val)
            o_ref[didx, :, h * t:(h + 1) * t] = (val - t5_ref[REL_BUCKETS - 1, h]) * LOG2E


def _bias_tiles(t5_bias):
    t = DSA_TILE
    assert t + 1 >= FAR_DISTANCE
    return pl.pallas_call(
        _bias_kernel,
        out_shape=jax.ShapeDtypeStruct((2, t, A_HEADS * t), F32),
        in_specs=[pl.BlockSpec(memory_space=pltpu.SMEM)],
        out_specs=pl.BlockSpec(memory_space=pltpu.VMEM),
        compiler_params=pltpu.CompilerParams(vmem_limit_bytes=VMEM_LIMIT_BYTES),
        name="t5_bias_tiles",
    )(t5_bias)


def _dsa_kernel(qt_ref, qit_ref, ckv_ref, ckvt_ref, ki_ref, wit_ref, wuk_ref, wuvt_ref, bias_ref,
                o_ref, keys_ref, khi_ref, klo_ref, acc_ref, ot_ref, *, topk, seq_len):
    t = DSA_TILE
    qb = pl.program_id(1)
    nk = qb + 1
    row = lax.broadcasted_iota(I32, (t, t), 0)
    col = lax.broadcasted_iota(I32, (t, t), 1)

    def idx_body(kc, carry):
        kic = ki_ref[0, pl.ds(pl.multiple_of(kc * t, t), t), :]
        acc = jnp.zeros((t, t), F32)
        for h in range(IDX_HEADS):
            rel = _dot(kic, qit_ref[0, h * IDX_DIM:(h + 1) * IDX_DIM, :])
            acc = acc + jnp.maximum(rel, 0.0) * wit_ref[0, h:h + 1, :]
        acc = acc + 0.0
        bits = pltpu.bitcast(acc, I32)
        key = bits ^ ((bits >> 31) & 0x7FFFFFFF)
        causal = (kc * t + row) <= (qb * t + col)
        key = jnp.where(causal, key, KEY_NEG_INF)
        sl = pl.ds(pl.multiple_of(kc * t, t), t)
        keys_ref[sl, :] = key
        khi_ref[sl, :] = (key >> 16).astype(I16)
        klo_ref[sl, :] = ((key & 0xFFFF) + I16_MIN).astype(I16)
        return carry

    lax.fori_loop(0, nk, idx_body, 0)

    def count16(ref, cand):
        cand16 = cand.astype(I16)

        def body(kc, acc):
            x = ref[pl.ds(pl.multiple_of(kc * t, t), t), :]
            ones = jnp.where(x >= cand16, jnp.asarray(1, BF16), jnp.asarray(0, BF16))
            ones = ones.reshape(t // PACKED_ROWS, PACKED_ROWS, t)
            for i in range(t // PACKED_ROWS):
                acc = acc + ones[i]
            return acc

        acc = lax.fori_loop(0, nk // 2, lambda i, acc: body(2 * i + 1, body(2 * i, acc)),
                            jnp.zeros((PACKED_ROWS, t), BF16))
        acc = lax.fori_loop(2 * (nk // 2), nk, body, acc)
        return jnp.sum(acc.astype(F32), axis=0, keepdims=True).astype(I32)

    def search16(ref, base):
        def accept(cand, cur):
            return jnp.where(base + count16(ref, cand) >= topk, cand, cur)

        v = accept(jnp.zeros((1, t), I32), jnp.full((1, t), I16_MIN, I32))
        return lax.fori_loop(0, 15, lambda i, v: accept(v | (jnp.int32(1) << (14 - i)), v), v)

    hi = search16(khi_ref, 0)
    above = count16(khi_ref, jnp.minimum(hi + 1, -I16_MIN - 1))
    hi16 = hi.astype(I16)

    def keep_low(kc, carry):
        sl = pl.ds(pl.multiple_of(kc * t, t), t)
        klo_ref[sl, :] = jnp.where(khi_ref[sl, :] == hi16, klo_ref[sl, :], jnp.asarray(I16_MIN, I16))
        return carry

    lax.fori_loop(0, nk, keep_low, 0)
    lo = search16(klo_ref, above)
    thr = jnp.maximum((hi << 16) | (lo - I16_MIN), KEY_NEG_INF + 1)

    def count(pred):
        def body(kc, acc):
            k = keys_ref[pl.ds(pl.multiple_of(kc * t, t), t), :]
            m = jnp.where(pred(k, kc), 1, 0).astype(I32)
            return acc + jnp.sum(m.reshape(t // SUBLANES, SUBLANES, t), axis=0)

        acc = lax.fori_loop(0, nk, body, jnp.zeros((SUBLANES, t), I32))
        return jnp.sum(acc, axis=0, keepdims=True)

    @pl.when(jnp.max(count(lambda k, kc: k >= thr)) > topk)
    def _():
        need = topk - count(lambda k, kc: k > thr)
        nbits = (seq_len - 1).bit_length()

        def pos_bit(i, y):
            cand = y | (jnp.int32(1) << (nbits - 1 - i))
            before = count(lambda k, kc: (k == thr) & ((kc * t + row) < cand))
            return jnp.where(before < need, cand, y)

        last_kept = lax.fori_loop(0, nbits, pos_bit, jnp.zeros((1, t), I32))

        def demote(kc, carry):
            sl = pl.ds(pl.multiple_of(kc * t, t), t)
            k = keys_ref[sl, :]
            keys_ref[sl, :] = jnp.where((k == thr) & ((kc * t + row) > last_kept), k - 1, k)
            return carry

        lax.fori_loop(0, nk, demote, 0)

    scale = A_HEAD_DIM ** -0.5 * LOG2E

    hw = A_HEADS * t
    qlat = jnp.concatenate(
        [(_dot(wuk_ref[h], qt_ref[0, h * A_HEAD_DIM:(h + 1) * A_HEAD_DIM, :]) * scale).astype(BF16)
         for h in range(A_HEADS)], axis=1)
    acc_ref[...] = jnp.zeros_like(acc_ref)

    def att_body(kc, m, *, near):
        sl = pl.ds(pl.multiple_of(kc * t, t), t)
        raw = _dot(ckv_ref[0, sl, :], qlat)
        sel = keys_ref[sl, :] >= thr
        parts = []
        for h in range(A_HEADS):
            lg_h = raw[:, h * t:(h + 1) * t]
            if near:
                lg_h = lg_h + bias_ref[qb - kc, :, h * t:(h + 1) * t]
            parts.append(jnp.where(sel, lg_h, MASKED_LOGIT).astype(BF16))
        lg = jnp.concatenate(parts, axis=1)
        m_new = jnp.maximum(m, jnp.max(lg, axis=0, keepdims=True).astype(F32))
        p = jnp.exp2(lg - m_new.astype(BF16))
        acc_ref[...] = acc_ref[...] * jnp.exp2(m - m_new) + _dot(ckvt_ref[0, kc], p)
        return m_new

    n_far = jnp.maximum(qb - 1, 0)
    m = lax.fori_loop(0, n_far, functools.partial(att_body, near=False), jnp.full((1, hw), MASKED_LOGIT, F32))
    lax.fori_loop(n_far, nk, functools.partial(att_body, near=True), m)
    olat = (acc_ref[:KV_RANK, :] / acc_ref[KV_RANK:KV_RANK + 1, :]).astype(BF16)
    for h in range(A_HEADS):
        ot_ref[h * A_HEAD_DIM:(h + 1) * A_HEAD_DIM, :] = _dot(wuvt_ref[h], olat[:, h * t:(h + 1) * t])
    o_ref[0] = ot_ref[...].T.astype(BF16)


def _dsa(qt, qit, ckv, ckvt, ki, wit, wuk, wuvt, bias):
    b, _, s = qt.shape
    t = DSA_TILE
    nt = s // t
    topk = min(TOPK_MAX, s // 4)
    assert s // PACKED_ROWS <= BF16_EXACT_INT
    return pl.pallas_call(
        functools.partial(_dsa_kernel, topk=topk, seq_len=s),
        out_shape=jax.ShapeDtypeStruct((b, s, A_WIDTH), BF16),
        grid=(b, nt),
        in_specs=[
            pl.BlockSpec((1, A_WIDTH, t), lambda bi, i: (bi, 0, i)),
            pl.BlockSpec((1, IDX_HEADS * IDX_DIM, t), lambda bi, i: (bi, 0, i)),
            pl.BlockSpec((1, s, KV_RANK), lambda bi, i: (bi, 0, 0)),
            pl.BlockSpec((1, nt, KV_RANK + SUM_ROWS, t), lambda bi, i: (bi, 0, 0, 0)),
            pl.BlockSpec((1, s, IDX_DIM), lambda bi, i: (bi, 0, 0)),
            pl.BlockSpec((1, IDX_HEADS, t), lambda bi, i: (bi, 0, i)),
            pl.BlockSpec((A_HEADS, KV_RANK, A_HEAD_DIM), lambda bi, i: (0, 0, 0)),
            pl.BlockSpec((A_HEADS, A_HEAD_DIM, KV_RANK), lambda bi, i: (0, 0, 0)),
            pl.BlockSpec((2, t, A_HEADS * t), lambda bi, i: (0, 0, 0)),
        ],
        out_specs=pl.BlockSpec((1, t, A_WIDTH), lambda bi, i: (bi, i, 0)),
        scratch_shapes=[pltpu.VMEM((s, t), I32), pltpu.VMEM((s, t), I16), pltpu.VMEM((s, t), I16),
                        pltpu.VMEM((KV_RANK + SUM_ROWS, A_HEADS * t), F32), pltpu.VMEM((A_WIDTH, t), F32)],
        compiler_params=_params("parallel", "arbitrary"),
        name="dsa",
    )(qt, qit, ckv, ckvt, ki, wit, wuk, wuvt, bias)


def _split3(x):
    hi = x.astype(BF16)
    r1 = x - hi.astype(F32)
    mid = r1.astype(BF16)
    lo = (r1 - mid.astype(F32)).astype(BF16)
    return hi, mid, lo


def _rwkv_kernel(r_ref, k_ref, v_ref, gl_ref, wl_ref, al_ref,
                 mur_ref, muk_ref, muv_ref, mug_ref, muw_ref, mua_ref,
                 w0_ref, a0_ref, kk_ref, ka_ref, rk_ref, lnw_ref, lnb_ref,
                 w2_ref, a2_ref, g2_ref,
                 o_ref,
                 pr_ref, pk_ref, pv_ref, pg_ref, pw_ref, pa_ref, state_ref):
    tl = RWKV_TILE
    c = RWKV_CHUNK
    nc = tl // c
    npair = R_WIDTH // PAIR

    @pl.when(pl.program_id(1) == 0)
    def _():
        state_ref[...] = jnp.zeros_like(state_ref)
        for ref in (pr_ref, pk_ref, pv_ref, pg_ref, pw_ref, pa_ref):
            ref[...] = jnp.zeros_like(ref)

    def shift_mix(x_ref, prev_ref, mu_ref):
        x = x_ref[0]
        first = lax.broadcasted_iota(I32, x.shape, 0) == 0
        xprev = jnp.where(first, prev_ref[...], pltpu.roll(x, 1, axis=0))
        prev_ref[...] = x[tl - 1:tl, :]
        return x + (xprev - x) * mu_ref[...]

    r = shift_mix(r_ref, pr_ref, mur_ref)
    k = shift_mix(k_ref, pk_ref, muk_ref)
    v = shift_mix(v_ref, pv_ref, muv_ref)
    gl = shift_mix(gl_ref, pg_ref, mug_ref)
    wl = shift_mix(wl_ref, pw_ref, muw_ref)
    al = shift_mix(al_ref, pa_ref, mua_ref)

    w_lin = w0_ref[...] + _dot(jnp.tanh(wl).astype(BF16), w2_ref[...])
    nx = -w_lin
    softplus = jnp.maximum(nx, 0.0) + jnp.log(1.0 + jnp.exp(-jnp.abs(nx)))
    ld = -jnp.exp(-softplus - 0.5)
    a = jax.nn.sigmoid(a0_ref[...] + _dot(al.astype(BF16), a2_ref[...]))
    gate = _dot(jax.nn.sigmoid(gl).astype(BF16), g2_ref[...])

    li = lax.broadcasted_iota(I32, (LANES, LANES), 0) // R_HEAD_DIM
    lj = lax.broadcasted_iota(I32, (LANES, LANES), 1) // R_HEAD_DIM
    head_ones = jnp.where(li == lj, 1.0, 0.0).astype(BF16)

    def head_sum(x):
        parts = []
        for j in range(R_WIDTH // LANES):
            hi, mid, lo = _split3(x[:, j * LANES:(j + 1) * LANES])
            parts.append(_dot(hi, head_ones) + _dot(mid, head_ones) + _dot(lo, head_ones))
        return jnp.concatenate(parts, axis=-1)

    kk = k * kk_ref[...]
    kk = kk / jnp.maximum(jnp.sqrt(head_sum(kk * kk)), 1e-12)
    k2 = k * (1.0 + (a - 1.0) * ka_ref[...])
    bb = kk * a

    ti = lax.broadcasted_iota(I32, (tl, tl), 0)
    tj = lax.broadcasted_iota(I32, (tl, tl), 1)
    tri = jnp.where((ti // c == tj // c) & (tj <= ti), 1.0, 0.0).astype(BF16)
    hi, mid, lo = _split3(ld)
    cum = _dot(tri, hi) + _dot(tri, mid) + _dot(tri, lo)
    cum_end = jnp.broadcast_to(cum.reshape(nc, c, R_WIDTH)[:, c - 1:c, :], (nc, c, R_WIDTH)).reshape(tl, R_WIDTH)
    e_in = jnp.exp(cum)
    e_neg = jnp.exp(-cum)
    e_out = jnp.exp(cum_end - cum)

    nb = nc * npair

    def to_pairs(x):
        x3 = x.reshape(nc, c, R_WIDTH)
        xs = jnp.stack([x3[:, :, p * PAIR:(p + 1) * PAIR] for p in range(npair)], axis=1)
        return xs.reshape(nb, c, PAIR)

    lane_head = lax.broadcasted_iota(I32, (2 * c, PAIR), 1) // R_HEAD_DIM
    row_head = lax.broadcasted_iota(I32, (2 * c, PAIR), 0) // c
    same = lane_head == row_head
    rt = lax.broadcasted_iota(I32, (2 * c, PAIR), 0) % c
    ct = lax.broadcasted_iota(I32, (2 * c, PAIR), 1) % c
    strict = same & (ct < rt)
    incl = same & (ct <= rt)
    eye = jnp.where(same & (ct == rt), 1.0, 0.0).astype(F32)

    def bd(x):
        return jnp.where(same, jnp.concatenate([x, x], axis=1), 0.0)

    lane0 = lax.broadcasted_iota(I32, (c, PAIR), 1) < R_HEAD_DIM

    abar = bd(to_pairs(-kk * jnp.exp(cum - ld))).astype(BF16)
    rbar = bd(to_pairs(r * e_in)).astype(BF16)
    bt = bd(to_pairs(bb * e_neg)).astype(BF16)
    kt = bd(to_pairs(k2 * e_neg)).astype(BF16)
    bk = jnp.concatenate([bd(to_pairs(bb * e_out)), bd(to_pairs(k2 * e_out))], axis=1).astype(BF16)
    v_pl = to_pairs(v)
    vbd = bd(v_pl)
    decay_end = to_pairs(jnp.exp(cum_end))[:, 0:1, :]
    aa = _bdot_nt(jnp.concatenate([abar, rbar], axis=1), jnp.concatenate([bt, kt], axis=1))
    a_ab = jnp.where(strict, aa[:, :2 * c, :PAIR], 0.0)
    a_ak = jnp.where(strict, aa[:, :2 * c, PAIR:], 0.0).astype(BF16)
    a_rb = jnp.where(incl, aa[:, 2 * c:, :PAIR], 0.0).astype(BF16)
    a_rk = jnp.where(incl, aa[:, 2 * c:, PAIR:], 0.0).astype(BF16)
    pw = a_ab.astype(BF16)
    tinv = eye + a_ab
    for _ in range(int(math.log2(c)) - 1):
        pw = _bdot(pw, pw).astype(BF16)
        tinv = tinv + _bdot(tinv.astype(BF16), pw)
    akv = _bdot(a_ak, vbd.astype(BF16))
    wu = _bdot(tinv.astype(BF16), jnp.concatenate([abar, akv.astype(BF16)], axis=2))
    w16 = wu[:, :, :PAIR].astype(BF16)
    u_bd = wu[:, :, PAIR:]

    s_bd = state_ref[...]
    e_parts, rs_parts = [], []
    for ci in range(nc):
        sl = slice(ci * npair, (ci + 1) * npair)
        wr = _bdot_nt(jnp.concatenate([w16[sl], rbar[sl]], axis=1), s_bd.astype(BF16))
        e_bd = wr[:, :2 * c] + u_bd[sl]
        e_parts.append(e_bd)
        rs_parts.append(wr[:, 2 * c:])
        ev = jnp.concatenate([e_bd, vbd[sl]], axis=1)
        ds = _bdot(jnp.swapaxes(ev, 1, 2).astype(BF16), bk[sl])
        s_bd = s_bd * decay_end[sl] + jnp.where(same, ds, 0.0)
    state_ref[...] = s_bd

    ev_all = jnp.concatenate([jnp.concatenate(e_parts, axis=0), vbd], axis=1).astype(BF16)
    y_bd = jnp.concatenate(rs_parts, axis=0) + _bdot(jnp.concatenate([a_rb, a_rk], axis=2), ev_all)
    mean = jnp.sum(y_bd, axis=-1, keepdims=True) * (1.0 / R_HEAD_DIM)
    dev = jnp.where(same, y_bd - mean, 0.0)
    var = jnp.sum(dev * dev, axis=-1, keepdims=True) * (1.0 / R_HEAD_DIM)
    yn = dev * lax.rsqrt(var + GN_EPS)
    yn = yn[:, :c] + yn[:, c:]
    rkv = to_pairs(r * k2 * rk_ref[...])
    s0 = jnp.sum(jnp.where(lane0, rkv, 0.0), axis=-1, keepdims=True)
    s1 = jnp.sum(jnp.where(lane0, 0.0, rkv), axis=-1, keepdims=True)
    bonus = jnp.where(lane0, s0, s1) * v_pl
    def per_pair(row_ref):
        rows = [row_ref[:, p * PAIR:(p + 1) * PAIR] for p in range(npair)]
        return jnp.stack(rows * nc, axis=0)

    out = ((yn * per_pair(lnw_ref) + per_pair(lnb_ref) + bonus) * to_pairs(gate)).astype(BF16)
    for ci in range(nc):
        for p in range(npair):
            o_ref[0, ci * c:(ci + 1) * c, p * PAIR:(p + 1) * PAIR] = out[ci * npair + p]


def _rwkv(z, prm):
    b, s, _ = z.shape
    tl = RWKV_TILE

    def seg(width, off):
        return pl.BlockSpec((1, tl, width), lambda bi, i: (bi, i, off // width))

    def full(arr):
        return pl.BlockSpec(arr.shape, lambda bi, i: (0,) * arr.ndim)

    names = ("mu_r", "mu_k", "mu_v", "mu_g", "mu_w", "mu_a", "w0", "a0", "k_k", "k_a", "r_k", "ln_w", "ln_b",
             "w2", "a2", "g2")
    consts = [prm[n] for n in names]
    return pl.pallas_call(
        _rwkv_kernel,
        out_shape=jax.ShapeDtypeStruct((b, s, R_WIDTH), BF16),
        grid=(b, s // tl),
        in_specs=[seg(R_WIDTH, Z_R), seg(R_WIDTH, Z_K), seg(R_WIDTH, Z_V), seg(GATE_LORA, Z_GL),
                  seg(LORA_PAD, Z_WL), seg(LORA_PAD, Z_AL)] + [full(x) for x in consts],
        out_specs=pl.BlockSpec((1, tl, R_WIDTH), lambda bi, i: (bi, i, 0)),
        scratch_shapes=[
            pltpu.VMEM((1, R_WIDTH), F32), pltpu.VMEM((1, R_WIDTH), F32), pltpu.VMEM((1, R_WIDTH), F32),
            pltpu.VMEM((1, GATE_LORA), F32), pltpu.VMEM((1, LORA_PAD), F32), pltpu.VMEM((1, LORA_PAD), F32),
            pltpu.VMEM((R_WIDTH // PAIR, PAIR, PAIR), F32),
        ],
        compiler_params=_params("parallel", "arbitrary"),
        name="rwkv7",
    )(z, z, z, z, z, z, *consts)


def _pad_cols(w, width):
    return jnp.pad(w, ((0, 0), (0, width - w.shape[1])))


def _layout_w_in(w):
    offs = [0]
    for n in (A_WIDTH, KV_RANK, IDX_HEADS * IDX_DIM, IDX_DIM, IDX_HEADS,
              R_WIDTH, R_WIDTH, R_WIDTH, DECAY_LORA, AAA_LORA, GATE_LORA):
        offs.append(offs[-1] + n)
    q, ckv, qi, ki, wi, r, k, v, wl, al, gl = (w[:, offs[i]:offs[i + 1]] for i in range(11))
    cols = [q, qi, r, k, v, gl, ckv, _pad_cols(jnp.concatenate([ki, wi], axis=1), LORA_PAD),
            _pad_cols(wl, LORA_PAD), _pad_cols(al, LORA_PAD)]
    return _pad_cols(jnp.concatenate(cols, axis=1), Z_WIDTH).astype(BF16)


def _pad_rows(w, rows):
    return jnp.pad(w, ((0, rows - w.shape[0]), (0, 0)))


def _rwkv_params(l, mu, w0, w2, a0, a2, g2, k_k, k_a, r_k, ln_w, ln_b):
    m = mu[l]
    o = [0, R_WIDTH, 2 * R_WIDTH, 3 * R_WIDTH, 3 * R_WIDTH + DECAY_LORA, 3 * R_WIDTH + DECAY_LORA + AAA_LORA]
    row = lambda x: x.reshape(1, -1)
    return {
        "mu_r": row(m[o[0]:o[1]]), "mu_k": row(m[o[1]:o[2]]), "mu_v": row(m[o[2]:o[3]]),
        "mu_w": _pad_cols(row(m[o[3]:o[4]]), LORA_PAD), "mu_a": _pad_cols(row(m[o[4]:o[5]]), LORA_PAD),
        "mu_g": row(m[o[5]:]),
        "w0": row(w0[l]), "a0": row(a0[l]), "k_k": row(k_k[l]), "k_a": row(k_a[l]), "r_k": row(r_k[l]),
        "ln_w": row(ln_w[l]), "ln_b": row(ln_b[l]),
        "w2": _pad_rows(w2[l], LORA_PAD).astype(BF16), "a2": _pad_rows(a2[l], LORA_PAD).astype(BF16),
        "g2": g2[l].astype(BF16),
    }


def kernel(x, c, t5_bias, ada_w, ada_b, norm_g, ffn_w_in, ffn_w_out, w_in, ckv_norm_g, w_uk, w_uv, rwkv_mu, rwkv_w0, rwkv_w2, rwkv_a0, rwkv_a2, rwkv_g2, rwkv_k_k, rwkv_k_a, rwkv_r_k, rwkv_ln_w, rwkv_ln_b, w_out, final_norm_g):
    b, s, d = x.shape
    depth = ada_w.shape[0]
    assert s % DSA_TILE == 0 and s % RWKV_TILE == 0 and d == A_WIDTH + R_WIDTH
    mod = _ada_mod(c, ada_w, ada_b).reshape(depth, b, N_SUB, 3, 1, d)
    bias = _bias_tiles(t5_bias)
    final_g = final_norm_g.reshape(1, d)
    h = x
    for l in range(depth):
        shift = lambda i: mod[l, :, i, 0]
        scale = lambda i: mod[l, :, i, 1]
        gate = lambda i: mod[l, :, i, 2]
        g = lambda i: norm_g[l, i].reshape(1, d)
        h = _ffn(h, g(0), shift(0), scale(0), gate(0), ffn_w_in[l, 0].astype(BF16), ffn_w_out[l, 0].astype(BF16),
                 final_g, final_norm=False)
        z = _proj(h, g(1), shift(1), scale(1), _layout_w_in(w_in[l]))
        qt, qit, ckv, ckvt, ki, wit = _dsa_prep(z, ckv_norm_g[l].reshape(1, KV_RANK))
        wuk = jnp.transpose(w_uk[l], (1, 0, 2)).astype(BF16)
        wuvt = jnp.transpose(w_uv[l], (1, 2, 0)).astype(BF16)
        o_a = _dsa(qt, qit, ckv, ckvt, ki, wit, wuk, wuvt, bias)
        o_r = _rwkv(z, _rwkv_params(l, rwkv_mu, rwkv_w0, rwkv_w2, rwkv_a0, rwkv_a2, rwkv_g2, rwkv_k_k,
                                    rwkv_k_a, rwkv_r_k, rwkv_ln_w, rwkv_ln_b))
        h = _outproj(o_a, o_r, h, gate(1), w_out[l].astype(BF16))
        h = _ffn(h, g(2), shift(2), scale(2), gate(2), ffn_w_in[l, 1].astype(BF16), ffn_w_out[l, 1].astype(BF16),
                 final_g, final_norm=(l == depth - 1))
    return h
```

```python
import functools
import math

import jax
import jax.numpy as jnp
from jax import lax
from jax.experimental import pallas as pl
from jax.experimental.pallas import tpu as pltpu

F32 = jnp.float32
BF16 = jnp.bfloat16
I32 = jnp.int32
I16 = jnp.int16

A_HEADS = 8
A_HEAD_DIM = 128
A_WIDTH = A_HEADS * A_HEAD_DIM
KV_RANK = 256
IDX_HEADS = 16
IDX_DIM = 64
TOPK_MAX = 256
REL_BUCKETS = 32
REL_MAX_EXACT = REL_BUCKETS // 2
REL_MAX_DIST = 128
R_HEAD_DIM = 64
R_WIDTH = 1024
R_HEADS = R_WIDTH // R_HEAD_DIM
DECAY_LORA = 96
AAA_LORA = 96
GATE_LORA = 256
GN_EPS = 64e-5
RMS_EPS = 1e-6
N_SUB = 3

LANES = 128
SUBLANES = 8
VMEM_LIMIT_BYTES = 56 * 1024 * 1024

LORA_PAD = 128
Z_Q = 0
Z_QI = Z_Q + A_WIDTH
Z_R = Z_QI + IDX_HEADS * IDX_DIM
Z_K = Z_R + R_WIDTH
Z_V = Z_K + R_WIDTH
Z_GL = Z_V + R_WIDTH
Z_CKV = Z_GL + GATE_LORA
Z_KIWI = Z_CKV + KV_RANK
Z_WL = Z_KIWI + LORA_PAD
Z_AL = Z_WL + LORA_PAD
Z_END = Z_AL + LORA_PAD
Z_WIDTH = 6144

FFN_TOKEN_TILE = 1024
FFN_HIDDEN_TILE = 256
ADALN_SLAB_ROWS = 16
ADALN_UNROLL = 8
DSA_TILE = 256
RWKV_CHUNK = 64
RWKV_TILE = 256
PAIR = 2 * R_HEAD_DIM

I16_MIN = -(2 ** 15)
PACKED_ROWS = 2 * SUBLANES
SUM_ROWS = PACKED_ROWS
BF16_EXACT_INT = 256
COUNT_CHAINS = 4
KEY_NEG_INF = -2139095041
MASKED_LOGIT = -(2.0 ** 100)
LOG2E = math.log2(math.e)
FAR_DISTANCE = math.ceil(REL_MAX_EXACT * (REL_MAX_DIST / REL_MAX_EXACT)
                         ** ((REL_BUCKETS - 1 - REL_MAX_EXACT) / (REL_BUCKETS - REL_MAX_EXACT)))


def _dot(a, b):
    return jnp.dot(a, b, preferred_element_type=F32)


def _dot_nt(a, b):
    return lax.dot_general(a, b, (((1,), (1,)), ((), ())), preferred_element_type=F32)


def _bdot(a, b):
    return lax.dot_general(a, b, (((2,), (1,)), ((0,), (0,))), preferred_element_type=F32)


def _bdot_nt(a, b):
    return lax.dot_general(a, b, (((2,), (2,)), ((0,), (0,))), preferred_element_type=F32)


def _rms(x, g, eps):
    ms = jnp.mean(x * x, axis=-1, keepdims=True)
    return x * lax.rsqrt(ms + eps) * g


def _divisor_tile(n, pref):
    if n <= pref:
        return n
    t = (pref // LANES) * LANES
    while t > LANES and n % t:
        t -= LANES
    assert n % t == 0, (n, pref)
    return t


def _params(*sem):
    return pltpu.CompilerParams(dimension_semantics=sem, vmem_limit_bytes=VMEM_LIMIT_BYTES)


def _ada_kernel(c_ref, w_ref, b_ref, o_ref):
    c = c_ref[...]
    ca = (c * jax.nn.sigmoid(c)).astype(BF16)
    o_ref[0] = _dot(ca, w_ref[0].astype(BF16)) + b_ref[0]


def _ada_mod(c, ada_w, ada_b):
    depth, d, n = ada_w.shape
    b = c.shape[0]
    bp = -(-b // SUBLANES) * SUBLANES
    cp = jnp.pad(c, ((0, bp - b), (0, 0)))
    tn = _divisor_tile(n, 1024)
    out = pl.pallas_call(
        _ada_kernel,
        out_shape=jax.ShapeDtypeStruct((depth, bp, n), F32),
        grid=(depth, n // tn),
        in_specs=[
            pl.BlockSpec((bp, d), lambda l, j: (0, 0)),
            pl.BlockSpec((1, d, tn), lambda l, j: (l, 0, j)),
            pl.BlockSpec((1, 1, tn), lambda l, j: (l, 0, j)),
        ],
        out_specs=pl.BlockSpec((1, bp, tn), lambda l, j: (l, 0, j)),
        compiler_params=_params("arbitrary", "arbitrary"),
        name="ada_mod",
    )(cp, ada_w, ada_b.reshape(depth, 1, n))
    return out[:, :b]


def _adaln_rows(h_ref, g_ref, scale_ref, shift_ref, hn_ref, copy_ref=None):
    rows = h_ref.shape[1]
    g = g_ref[...]
    mul = 1.0 + scale_ref[0]
    add = shift_ref[0]

    def slab(i, carry):
        sl = pl.ds(pl.multiple_of(i * ADALN_SLAB_ROWS, ADALN_SLAB_ROWS), ADALN_SLAB_ROWS)
        h = h_ref[0, sl, :]
        hn_ref[sl, :] = (_rms(h, g, RMS_EPS) * mul + add).astype(BF16)
        if copy_ref is not None:
            copy_ref[0, sl, :] = h
        return carry

    lax.fori_loop(0, rows // ADALN_SLAB_ROWS, slab, 0, unroll=ADALN_UNROLL)


def _ffn_kernel(h_ref, g_ref, shift_ref, scale_ref, gate_ref, wg_ref, wu_ref, wo_ref, fg_ref,
                o_ref, hn_ref, *, final_norm):
    f = pl.program_id(2)

    @pl.when(f == 0)
    def _():
        _adaln_rows(h_ref, g_ref, scale_ref, shift_ref, hn_ref, o_ref)

    hn = hn_ref[...]
    g = _dot(hn, wg_ref[...])
    u = _dot(hn, wu_ref[...])
    act = (g * jax.nn.sigmoid(g) * u).astype(BF16)
    o_ref[0] += (0.5 * gate_ref[0]) * _dot(act, wo_ref[...])

    if final_norm:
        @pl.when(f == pl.num_programs(2) - 1)
        def _():
            o_ref[0] = _rms(o_ref[0], fg_ref[...], RMS_EPS)


def _ffn(h, g, shift, scale, gate, w_in, w_out, final_g, *, final_norm):
    b, s, d = h.shape
    ff = w_out.shape[0]
    tm = _divisor_tile(s, FFN_TOKEN_TILE)
    tf = _divisor_tile(ff, FFN_HIDDEN_TILE)
    nf = ff // tf
    vec = pl.BlockSpec((1, 1, d), lambda bi, i, f: (bi, 0, 0))
    row = pl.BlockSpec((1, d), lambda bi, i, f: (0, 0))
    tile = pl.BlockSpec((1, tm, d), lambda bi, i, f: (bi, i, 0))
    return pl.pallas_call(
        functools.partial(_ffn_kernel, final_norm=final_norm),
        out_shape=jax.ShapeDtypeStruct((b, s, d), F32),
        grid=(b, s // tm, nf),
        in_specs=[
            tile, row, vec, vec, vec,
            pl.BlockSpec((d, tf), lambda bi, i, f: (0, f)),
            pl.BlockSpec((d, tf), lambda bi, i, f: (0, nf + f)),
            pl.BlockSpec((tf, d), lambda bi, i, f: (f, 0)),
            row,
        ],
        out_specs=tile,
        scratch_shapes=[pltpu.VMEM((tm, d), BF16)],
        compiler_params=_params("parallel", "parallel", "arbitrary"),
        name="ffn",
    )(h, g, shift, scale, gate, w_in, w_in, w_out, final_g)


def _proj_kernel(h_ref, g_ref, shift_ref, scale_ref, w_ref, o_ref, hn_ref):
    @pl.when(pl.program_id(2) == 0)
    def _():
        _adaln_rows(h_ref, g_ref, scale_ref, shift_ref, hn_ref)

    o_ref[0] = _dot(hn_ref[...], w_ref[...])


def _proj(h, g, shift, scale, w):
    b, s, d = h.shape
    p = w.shape[1]
    tm = _divisor_tile(s, 1024)
    tn = _divisor_tile(p, 1024)
    vec = pl.BlockSpec((1, 1, d), lambda bi, i, n: (bi, 0, 0))
    return pl.pallas_call(
        _proj_kernel,
        out_shape=jax.ShapeDtypeStruct((b, s, p), F32),
        grid=(b, s // tm, p // tn),
        in_specs=[
            pl.BlockSpec((1, tm, d), lambda bi, i, n: (bi, i, 0)),
            pl.BlockSpec((1, d), lambda bi, i, n: (0, 0)),
            vec, vec,
            pl.BlockSpec((d, tn), lambda bi, i, n: (0, n)),
        ],
        out_specs=pl.BlockSpec((1, tm, tn), lambda bi, i, n: (bi, i, n)),
        scratch_shapes=[pltpu.VMEM((tm, d), BF16)],
        compiler_params=_params("parallel", "parallel", "arbitrary"),
        name="proj",
    )(h, g, shift, scale, w)


def _outproj_kernel(oa_ref, or_ref, h_ref, gate_ref, wa_ref, wr_ref, o_ref):
    acc = _dot(oa_ref[0], wa_ref[...]) + _dot(or_ref[0], wr_ref[...])
    o_ref[0] = h_ref[0] + gate_ref[0] * acc


def _outproj(o_a, o_r, h, gate, w_out):
    b, s, d = h.shape
    tm = _divisor_tile(s, 1024)
    tn = _divisor_tile(d, 1024)
    wa = o_a.shape[-1]
    wr = o_r.shape[-1]
    return pl.pallas_call(
        _outproj_kernel,
        out_shape=jax.ShapeDtypeStruct((b, s, d), F32),
        grid=(b, s // tm, d // tn),
        in_specs=[
            pl.BlockSpec((1, tm, wa), lambda bi, i, n: (bi, i, 0)),
            pl.BlockSpec((1, tm, wr), lambda bi, i, n: (bi, i, 0)),
            pl.BlockSpec((1, tm, tn), lambda bi, i, n: (bi, i, n)),
            pl.BlockSpec((1, 1, tn), lambda bi, i, n: (bi, 0, n)),
            pl.BlockSpec((wa, tn), lambda bi, i, n: (0, n)),
            pl.BlockSpec((wr, tn), lambda bi, i, n: (wa // wr, n)),
        ],
        out_specs=pl.BlockSpec((1, tm, tn), lambda bi, i, n: (bi, i, n)),
        compiler_params=_params("parallel", "parallel", "arbitrary"),
        name="outproj",
    )(o_a, o_r, h, gate, w_out, w_out)


def _dsa_prep_kernel(q_ref, qi_ref, ckv_ref, kiwi_ref, g_ref, qt_ref, qit_ref, ckv_o, ckvt_o, ki_o, wit_o):
    qt_ref[0] = q_ref[0].T.astype(BF16)
    qit_ref[0] = qi_ref[0].T.astype(BF16)
    cn = _rms(ckv_ref[0], g_ref[...], RMS_EPS)
    ckv_o[0] = cn.astype(BF16)
    ckvt_o[0, 0] = jnp.concatenate([cn.T.astype(BF16), jnp.ones((SUM_ROWS, cn.shape[0]), BF16)], axis=0)
    kw = kiwi_ref[0]
    ki_o[0] = kw[:, :IDX_DIM].astype(BF16)
    wit_o[0] = kw.T[IDX_DIM:IDX_DIM + IDX_HEADS, :] * (IDX_HEADS * IDX_DIM) ** -0.5


def _dsa_prep(z, ckv_g):
    b, s, _ = z.shape
    t = DSA_TILE
    nt = s // t

    def seg(width, off):
        return pl.BlockSpec((1, t, width), lambda bi, i: (bi, i, off // width))

    return pl.pallas_call(
        _dsa_prep_kernel,
        out_shape=(
            jax.ShapeDtypeStruct((b, A_WIDTH, s), BF16),
            jax.ShapeDtypeStruct((b, IDX_HEADS * IDX_DIM, s), BF16),
            jax.ShapeDtypeStruct((b, s, KV_RANK), BF16),
            jax.ShapeDtypeStruct((b, nt, KV_RANK + SUM_ROWS, t), BF16),
            jax.ShapeDtypeStruct((b, s, IDX_DIM), BF16),
            jax.ShapeDtypeStruct((b, IDX_HEADS, s), F32),
        ),
        grid=(b, nt),
        in_specs=[
            seg(A_WIDTH, Z_Q), seg(IDX_HEADS * IDX_DIM, Z_QI), seg(KV_RANK, Z_CKV), seg(LORA_PAD, Z_KIWI),
            pl.BlockSpec((1, KV_RANK), lambda bi, i: (0, 0)),
        ],
        out_specs=(
            pl.BlockSpec((1, A_WIDTH, t), lambda bi, i: (bi, 0, i)),
            pl.BlockSpec((1, IDX_HEADS * IDX_DIM, t), lambda bi, i: (bi, 0, i)),
            pl.BlockSpec((1, t, KV_RANK), lambda bi, i: (bi, i, 0)),
            pl.BlockSpec((1, 1, KV_RANK + SUM_ROWS, t), lambda bi, i: (bi, i, 0, 0)),
            pl.BlockSpec((1, t, IDX_DIM), lambda bi, i: (bi, i, 0)),
            pl.BlockSpec((1, IDX_HEADS, t), lambda bi, i: (bi, 0, i)),
        ),
        compiler_params=_params("parallel", "parallel"),
        name="dsa_prep",
    )(z, z, z, z, ckv_g)


def _bias_kernel(t5_ref, o_ref):
    t = DSA_TILE
    j = lax.broadcasted_iota(I32, (t, t), 0)
    i = lax.broadcasted_iota(I32, (t, t), 1)
    for didx in range(2):
        n = jnp.maximum(didx * t + i - j, 0)
        nf = jnp.maximum(n, 1).astype(F32)
        large = REL_MAX_EXACT + (jnp.log(nf / REL_MAX_EXACT) / math.log(REL_MAX_DIST / REL_MAX_EXACT)
                                 * (REL_BUCKETS - REL_MAX_EXACT)).astype(I32)
        large = jnp.minimum(large, REL_BUCKETS - 1)
        bucket = jnp.where(n < REL_MAX_EXACT, n, large)
        for h in range(A_HEADS):
            val = jnp.zeros((t, t), F32)
            for k in range(REL_BUCKETS):
                val = jnp.where(bucket == k, t5_ref[k, h], val)
            o_ref[didx, :, h * t:(h + 1) * t] = (val - t5_ref[REL_BUCKETS - 1, h]) * LOG2E


def _bias_tiles(t5_bias):
    t = DSA_TILE
    assert t + 1 >= FAR_DISTANCE
    return pl.pallas_call(
        _bias_kernel,
        out_shape=jax.ShapeDtypeStruct((2, t, A_HEADS * t), F32),
        in_specs=[pl.BlockSpec(memory_space=pltpu.SMEM)],
        out_specs=pl.BlockSpec(memory_space=pltpu.VMEM),
        compiler_params=pltpu.CompilerParams(vmem_limit_bytes=VMEM_LIMIT_BYTES),
        name="t5_bias_tiles",
    )(t5_bias)


def _dsa_kernel(qt_ref, qit_ref, ckv_ref, ckvt_ref, ki_ref, wit_ref, wuk_ref, wuvt_ref, bias_ref,
                o_ref, keys_ref, khi_ref, klo_ref, acc_ref, ot_ref, *, topk, seq_len):
    t = DSA_TILE
    qb = pl.program_id(1)
    nk = qb + 1
    row = lax.broadcasted_iota(I32, (t, t), 0)
    col = lax.broadcasted_iota(I32, (t, t), 1)

    def idx_body(kc, carry):
        kic = ki_ref[0, pl.ds(pl.multiple_of(kc * t, t), t), :]
        acc = jnp.zeros((t, t), F32)
        for h in range(IDX_HEADS):
            rel = _dot(kic, qit_ref[0, h * IDX_DIM:(h + 1) * IDX_DIM, :])
            acc = acc + jnp.maximum(rel, 0.0) * wit_ref[0, h:h + 1, :]
        acc = acc + 0.0
        bits = pltpu.bitcast(acc, I32)
        key = bits ^ ((bits >> 31) & 0x7FFFFFFF)
        causal = (kc * t + row) <= (qb * t + col)
        key = jnp.where(causal, key, KEY_NEG_INF)
        sl = pl.ds(pl.multiple_of(kc * t, t), t)
        keys_ref[sl, :] = key
        khi_ref[sl, :] = (key >> 16).astype(I16)
        klo_ref[sl, :] = ((key & 0xFFFF) + I16_MIN).astype(I16)
        return carry

    lax.fori_loop(0, nk, idx_body, 0)

    def count16(ref, cand):
        cand16 = cand.astype(I16)

        def body(kc, accs):
            x = ref[pl.ds(pl.multiple_of(kc * t, t), t), :]
            ones = jnp.where(x >= cand16, jnp.asarray(1, BF16), jnp.asarray(0, BF16))
            ones = ones.reshape(t // PACKED_ROWS, PACKED_ROWS, t)
            accs = list(accs)
            for i in range(t // PACKED_ROWS):
                accs[i % COUNT_CHAINS] = accs[i % COUNT_CHAINS] + ones[i]
            return tuple(accs)

        accs = (jnp.zeros((PACKED_ROWS, t), BF16),) * COUNT_CHAINS
        accs = lax.fori_loop(0, nk // 2, lambda i, a: body(2 * i + 1, body(2 * i, a)), accs)
        accs = lax.fori_loop(2 * (nk // 2), nk, body, accs)
        total = sum(a.astype(F32) for a in accs)
        return jnp.sum(total, axis=0, keepdims=True).astype(I32)

    def search16(ref, base):
        def accept(cand, cur):
            return jnp.where(base + count16(ref, cand) >= topk, cand, cur)

        v = accept(jnp.zeros((1, t), I32), jnp.full((1, t), I16_MIN, I32))
        return lax.fori_loop(0, 15, lambda i, v: accept(v | (jnp.int32(1) << (14 - i)), v), v)

    hi = search16(khi_ref, 0)
    above = count16(khi_ref, jnp.minimum(hi + 1, -I16_MIN - 1))
    hi16 = hi.astype(I16)

    def keep_low(kc, carry):
        sl = pl.ds(pl.multiple_of(kc * t, t), t)
        klo_ref[sl, :] = jnp.where(khi_ref[sl, :] == hi16, klo_ref[sl, :], jnp.asarray(I16_MIN, I16))
        return carry

    lax.fori_loop(0, nk, keep_low, 0)
    lo = search16(klo_ref, above)
    thr = jnp.maximum((hi << 16) | (lo - I16_MIN), KEY_NEG_INF + 1)

    def count(pred):
        def body(kc, acc):
            k = keys_ref[pl.ds(pl.multiple_of(kc * t, t), t), :]
            m = jnp.where(pred(k, kc), 1, 0).astype(I32)
            return acc + jnp.sum(m.reshape(t // SUBLANES, SUBLANES, t), axis=0)

        acc = lax.fori_loop(0, nk, body, jnp.zeros((SUBLANES, t), I32))
        return jnp.sum(acc, axis=0, keepdims=True)

    @pl.when(jnp.max(count(lambda k, kc: k >= thr)) > topk)
    def _():
        need = topk - count(lambda k, kc: k > thr)
        nbits = (seq_len - 1).bit_length()

        def pos_bit(i, y):
            cand = y | (jnp.int32(1) << (nbits - 1 - i))
            before = count(lambda k, kc: (k == thr) & ((kc * t + row) < cand))
            return jnp.where(before < need, cand, y)

        last_kept = lax.fori_loop(0, nbits, pos_bit, jnp.zeros((1, t), I32))

        def demote(kc, carry):
            sl = pl.ds(pl.multiple_of(kc * t, t), t)
            k = keys_ref[sl, :]
            keys_ref[sl, :] = jnp.where((k == thr) & ((kc * t + row) > last_kept), k - 1, k)
            return carry

        lax.fori_loop(0, nk, demote, 0)

    scale = A_HEAD_DIM ** -0.5 * LOG2E

    hw = A_HEADS * t
    qlat = jnp.concatenate(
        [(_dot(wuk_ref[h], qt_ref[0, h * A_HEAD_DIM:(h + 1) * A_HEAD_DIM, :]) * scale).astype(BF16)
         for h in range(A_HEADS)], axis=1)
    acc_ref[...] = jnp.zeros_like(acc_ref)

    def att_body(kc, m, *, near):
        sl = pl.ds(pl.multiple_of(kc * t, t), t)
        raw = _dot(ckv_ref[0, sl, :], qlat)
        sel = keys_ref[sl, :] >= thr
        parts = []
        for h in range(A_HEADS):
            lg_h = raw[:, h * t:(h + 1) * t]
            if near:
                lg_h = lg_h + bias_ref[qb - kc, :, h * t:(h + 1) * t]
            parts.append(jnp.where(sel, lg_h, MASKED_LOGIT).astype(BF16))
        lg = jnp.concatenate(parts, axis=1)
        m_new = jnp.maximum(m, jnp.max(lg, axis=0, keepdims=True).astype(F32))
        p = jnp.exp2(lg - m_new.astype(BF16))
        acc_ref[...] = acc_ref[...] * jnp.exp2(m - m_new) + _dot(ckvt_ref[0, kc], p)
        return m_new

    n_far = jnp.maximum(qb - 1, 0)
    m = lax.fori_loop(0, n_far, functools.partial(att_body, near=False), jnp.full((1, hw), MASKED_LOGIT, F32))
    lax.fori_loop(n_far, nk, functools.partial(att_body, near=True), m)
    olat = (acc_ref[:KV_RANK, :] / acc_ref[KV_RANK:KV_RANK + 1, :]).astype(BF16)
    for h in range(A_HEADS):
        ot_ref[h * A_HEAD_DIM:(h + 1) * A_HEAD_DIM, :] = _dot(wuvt_ref[h], olat[:, h * t:(h + 1) * t])
    o_ref[0] = ot_ref[...].T.astype(BF16)


def _dsa(qt, qit, ckv, ckvt, ki, wit, wuk, wuvt, bias):
    b, _, s = qt.shape
    t = DSA_TILE
    nt = s // t
    topk = min(TOPK_MAX, s // 4)
    assert s // PACKED_ROWS <= BF16_EXACT_INT
    return pl.pallas_call(
        functools.partial(_dsa_kernel, topk=topk, seq_len=s),
        out_shape=jax.ShapeDtypeStruct((b, s, A_WIDTH), BF16),
        grid=(b, nt),
        in_specs=[
            pl.BlockSpec((1, A_WIDTH, t), lambda bi, i: (bi, 0, i)),
            pl.BlockSpec((1, IDX_HEADS * IDX_DIM, t), lambda bi, i: (bi, 0, i)),
            pl.BlockSpec((1, s, KV_RANK), lambda bi, i: (bi, 0, 0)),
            pl.BlockSpec((1, nt, KV_RANK + SUM_ROWS, t), lambda bi, i: (bi, 0, 0, 0)),
            pl.BlockSpec((1, s, IDX_DIM), lambda bi, i: (bi, 0, 0)),
            pl.BlockSpec((1, IDX_HEADS, t), lambda bi, i: (bi, 0, i)),
            pl.BlockSpec((A_HEADS, KV_RANK, A_HEAD_DIM), lambda bi, i: (0, 0, 0)),
            pl.BlockSpec((A_HEADS, A_HEAD_DIM, KV_RANK), lambda bi, i: (0, 0, 0)),
            pl.BlockSpec((2, t, A_HEADS * t), lambda bi, i: (0, 0, 0)),
        ],
        out_specs=pl.BlockSpec((1, t, A_WIDTH), lambda bi, i: (bi, i, 0)),
        scratch_shapes=[pltpu.VMEM((s, t), I32), pltpu.VMEM((s, t), I16), pltpu.VMEM((s, t), I16),
                        pltpu.VMEM((KV_RANK + SUM_ROWS, A_HEADS * t), F32), pltpu.VMEM((A_WIDTH, t), F32)],
        compiler_params=_params("parallel", "arbitrary"),
        name="dsa",
    )(qt, qit, ckv, ckvt, ki, wit, wuk, wuvt, bias)


def _split3(x):
    hi = x.astype(BF16)
    r1 = x - hi.astype(F32)
    mid = r1.astype(BF16)
    lo = (r1 - mid.astype(F32)).astype(BF16)
    return hi, mid, lo


def _rwkv_kernel(r_ref, k_ref, v_ref, gl_ref, wl_ref, al_ref,
                 mur_ref, muk_ref, muv_ref, mug_ref, muw_ref, mua_ref,
                 w0_ref, a0_ref, kk_ref, ka_ref, rk_ref, lnw_ref, lnb_ref,
                 w2_ref, a2_ref, g2_ref,
                 o_ref,
                 pr_ref, pk_ref, pv_ref, pg_ref, pw_ref, pa_ref, state_ref):
    tl = RWKV_TILE
    c = RWKV_CHUNK
    nc = tl // c
    npair = R_WIDTH // PAIR

    @pl.when(pl.program_id(1) == 0)
    def _():
        state_ref[...] = jnp.zeros_like(state_ref)
        for ref in (pr_ref, pk_ref, pv_ref, pg_ref, pw_ref, pa_ref):
            ref[...] = jnp.zeros_like(ref)

    def shift_mix(x_ref, prev_ref, mu_ref):
        x = x_ref[0]
        first = lax.broadcasted_iota(I32, x.shape, 0) == 0
        xprev = jnp.where(first, prev_ref[...], pltpu.roll(x, 1, axis=0))
        prev_ref[...] = x[tl - 1:tl, :]
        return x + (xprev - x) * mu_ref[...]

    r = shift_mix(r_ref, pr_ref, mur_ref)
    k = shift_mix(k_ref, pk_ref, muk_ref)
    v = shift_mix(v_ref, pv_ref, muv_ref)
    gl = shift_mix(gl_ref, pg_ref, mug_ref)
    wl = shift_mix(wl_ref, pw_ref, muw_ref)
    al = shift_mix(al_ref, pa_ref, mua_ref)

    w_lin = w0_ref[...] + _dot(jnp.tanh(wl).astype(BF16), w2_ref[...])
    nx = -w_lin
    softplus = jnp.maximum(nx, 0.0) + jnp.log(1.0 + jnp.exp(-jnp.abs(nx)))
    ld = -jnp.exp(-softplus - 0.5)
    a = jax.nn.sigmoid(a0_ref[...] + _dot(al.astype(BF16), a2_ref[...]))
    gate = _dot(jax.nn.sigmoid(gl).astype(BF16), g2_ref[...])

    li = lax.broadcasted_iota(I32, (LANES, LANES), 0) // R_HEAD_DIM
    lj = lax.broadcasted_iota(I32, (LANES, LANES), 1) // R_HEAD_DIM
    head_ones = jnp.where(li == lj, 1.0, 0.0).astype(BF16)

    def head_sum(x):
        parts = []
        for j in range(R_WIDTH // LANES):
            hi, mid, lo = _split3(x[:, j * LANES:(j + 1) * LANES])
            parts.append(_dot(hi, head_ones) + _dot(mid, head_ones) + _dot(lo, head_ones))
        return jnp.concatenate(parts, axis=-1)

    kk = k * kk_ref[...]
    kk = kk / jnp.maximum(jnp.sqrt(head_sum(kk * kk)), 1e-12)
    k2 = k * (1.0 + (a - 1.0) * ka_ref[...])
    bb = kk * a

    ti = lax.broadcasted_iota(I32, (tl, tl), 0)
    tj = lax.broadcasted_iota(I32, (tl, tl), 1)
    tri = jnp.where((ti // c == tj // c) & (tj <= ti), 1.0, 0.0).astype(BF16)
    hi, mid, lo = _split3(ld)
    cum = _dot(tri, hi) + _dot(tri, mid) + _dot(tri, lo)
    cum_end = jnp.broadcast_to(cum.reshape(nc, c, R_WIDTH)[:, c - 1:c, :], (nc, c, R_WIDTH)).reshape(tl, R_WIDTH)
    e_in = jnp.exp(cum)
    e_neg = jnp.exp(-cum)
    e_out = jnp.exp(cum_end - cum)

    nb = nc * npair

    def to_pairs(x):
        x3 = x.reshape(nc, c, R_WIDTH)
        xs = jnp.stack([x3[:, :, p * PAIR:(p + 1) * PAIR] for p in range(npair)], axis=1)
        return xs.reshape(nb, c, PAIR)

    lane_head = lax.broadcasted_iota(I32, (2 * c, PAIR), 1) // R_HEAD_DIM
    row_head = lax.broadcasted_iota(I32, (2 * c, PAIR), 0) // c
    same = lane_head == row_head
    rt = lax.broadcasted_iota(I32, (2 * c, PAIR), 0) % c
    ct = lax.broadcasted_iota(I32, (2 * c, PAIR), 1) % c
    strict = same & (ct < rt)
    incl = same & (ct <= rt)
    eye = jnp.where(same & (ct == rt), 1.0, 0.0).astype(F32)

    def bd(x):
        return jnp.where(same, jnp.concatenate([x, x], axis=1), 0.0)

    lane0 = lax.broadcasted_iota(I32, (c, PAIR), 1) < R_HEAD_DIM

    abar = bd(to_pairs(-kk * jnp.exp(cum - ld))).astype(BF16)
    rbar = bd(to_pairs(r * e_in)).astype(BF16)
    bt = bd(to_pairs(bb * e_neg)).astype(BF16)
    kt = bd(to_pairs(k2 * e_neg)).astype(BF16)
    bk = jnp.concatenate([bd(to_pairs(bb * e_out)), bd(to_pairs(k2 * e_out))], axis=1).astype(BF16)
    v_pl = to_pairs(v)
    vbd = bd(v_pl)
    decay_end = to_pairs(jnp.exp(cum_end))[:, 0:1, :]
    aa = _bdot_nt(jnp.concatenate([abar, rbar], axis=1), jnp.concatenate([bt, kt], axis=1))
    a_ab = jnp.where(strict, aa[:, :2 * c, :PAIR], 0.0)
    a_ak = jnp.where(strict, aa[:, :2 * c, PAIR:], 0.0).astype(BF16)
    a_rb = jnp.where(incl, aa[:, 2 * c:, :PAIR], 0.0).astype(BF16)
    a_rk = jnp.where(incl, aa[:, 2 * c:, PAIR:], 0.0).astype(BF16)
    pw = a_ab.astype(BF16)
    tinv = eye + a_ab
    for _ in range(int(math.log2(c)) - 1):
        pw = _bdot(pw, pw).astype(BF16)
        tinv = tinv + _bdot(tinv.astype(BF16), pw)
    akv = _bdot(a_ak, vbd.astype(BF16))
    wu = _bdot(tinv.astype(BF16), jnp.concatenate([abar, akv.astype(BF16)], axis=2))
    w16 = wu[:, :, :PAIR].astype(BF16)
    u_bd = wu[:, :, PAIR:]

    s_bd = state_ref[...]
    e_parts, rs_parts = [], []
    for ci in range(nc):
        sl = slice(ci * npair, (ci + 1) * npair)
        wr = _bdot_nt(jnp.concatenate([w16[sl], rbar[sl]], axis=1), s_bd.astype(BF16))
        e_bd = wr[:, :2 * c] + u_bd[sl]
        e_parts.append(e_bd)
        rs_parts.append(wr[:, 2 * c:])
        ev = jnp.concatenate([e_bd, vbd[sl]], axis=1)
        ds = _bdot(jnp.swapaxes(ev, 1, 2).astype(BF16), bk[sl])
        s_bd = s_bd * decay_end[sl] + jnp.where(same, ds, 0.0)
    state_ref[...] = s_bd

    ev_all = jnp.concatenate([jnp.concatenate(e_parts, axis=0), vbd], axis=1).astype(BF16)
    y_bd = jnp.concatenate(rs_parts, axis=0) + _bdot(jnp.concatenate([a_rb, a_rk], axis=2), ev_all)
    mean = jnp.sum(y_bd, axis=-1, keepdims=True) * (1.0 / R_HEAD_DIM)
    dev = jnp.where(same, y_bd - mean, 0.0)
    var = jnp.sum(dev * dev, axis=-1, keepdims=True) * (1.0 / R_HEAD_DIM)
    yn = dev * lax.rsqrt(var + GN_EPS)
    yn = yn[:, :c] + yn[:, c:]
    rkv = to_pairs(r * k2 * rk_ref[...])
    s0 = jnp.sum(jnp.where(lane0, rkv, 0.0), axis=-1, keepdims=True)
    s1 = jnp.sum(jnp.where(lane0, 0.0, rkv), axis=-1, keepdims=True)
    bonus = jnp.where(lane0, s0, s1) * v_pl
    def per_pair(row_ref):
        rows = [row_ref[:, p * PAIR:(p + 1) * PAIR] for p in range(npair)]
        return jnp.stack(rows * nc, axis=0)

    out = ((yn * per_pair(lnw_ref) + per_pair(lnb_ref) + bonus) * to_pairs(gate)).astype(BF16)
    for ci in range(nc):
        for p in range(npair):
            o_ref[0, ci * c:(ci + 1) * c, p * PAIR:(p + 1) * PAIR] = out[ci * npair + p]


def _rwkv(z, prm):
    b, s, _ = z.shape
    tl = RWKV_TILE

    def seg(width, off):
        return pl.BlockSpec((1, tl, width), lambda bi, i: (bi, i, off // width))

    def full(arr):
        return pl.BlockSpec(arr.shape, lambda bi, i: (0,) * arr.ndim)

    names = ("mu_r", "mu_k", "mu_v", "mu_g", "mu_w", "mu_a", "w0", "a0", "k_k", "k_a", "r_k", "ln_w", "ln_b",
             "w2", "a2", "g2")
    consts = [prm[n] for n in names]
    return pl.pallas_call(
        _rwkv_kernel,
        out_shape=jax.ShapeDtypeStruct((b, s, R_WIDTH), BF16),
        grid=(b, s // tl),
        in_specs=[seg(R_WIDTH, Z_R), seg(R_WIDTH, Z_K), seg(R_WIDTH, Z_V), seg(GATE_LORA, Z_GL),
                  seg(LORA_PAD, Z_WL), seg(LORA_PAD, Z_AL)] + [full(x) for x in consts],
        out_specs=pl.BlockSpec((1, tl, R_WIDTH), lambda bi, i: (bi, i, 0)),
        scratch_shapes=[
            pltpu.VMEM((1, R_WIDTH), F32), pltpu.VMEM((1, R_WIDTH), F32), pltpu.VMEM((1, R_WIDTH), F32),
            pltpu.VMEM((1, GATE_LORA), F32), pltpu.VMEM((1, LORA_PAD), F32), pltpu.VMEM((1, LORA_PAD), F32),
            pltpu.VMEM((R_WIDTH // PAIR, PAIR, PAIR), F32),
        ],
        compiler_params=_params("parallel", "arbitrary"),
        name="rwkv7",
    )(z, z, z, z, z, z, *consts)


def _pad_cols(w, width):
    return jnp.pad(w, ((0, 0), (0, width - w.shape[1])))


def _layout_w_in(w):
    offs = [0]
    for n in (A_WIDTH, KV_RANK, IDX_HEADS * IDX_DIM, IDX_DIM, IDX_HEADS,
              R_WIDTH, R_WIDTH, R_WIDTH, DECAY_LORA, AAA_LORA, GATE_LORA):
        offs.append(offs[-1] + n)
    q, ckv, qi, ki, wi, r, k, v, wl, al, gl = (w[:, offs[i]:offs[i + 1]] for i in range(11))
    cols = [q, qi, r, k, v, gl, ckv, _pad_cols(jnp.concatenate([ki, wi], axis=1), LORA_PAD),
            _pad_cols(wl, LORA_PAD), _pad_cols(al, LORA_PAD)]
    return _pad_cols(jnp.concatenate(cols, axis=1), Z_WIDTH).astype(BF16)


def _pad_rows(w, rows):
    return jnp.pad(w, ((0, rows - w.shape[0]), (0, 0)))


def _rwkv_params(l, mu, w0, w2, a0, a2, g2, k_k, k_a, r_k, ln_w, ln_b):
    m = mu[l]
    o = [0, R_WIDTH, 2 * R_WIDTH, 3 * R_WIDTH, 3 * R_WIDTH + DECAY_LORA, 3 * R_WIDTH + DECAY_LORA + AAA_LORA]
    row = lambda x: x.reshape(1, -1)
    return {
        "mu_r": row(m[o[0]:o[1]]), "mu_k": row(m[o[1]:o[2]]), "mu_v": row(m[o[2]:o[3]]),
        "mu_w": _pad_cols(row(m[o[3]:o[4]]), LORA_PAD), "mu_a": _pad_cols(row(m[o[4]:o[5]]), LORA_PAD),
        "mu_g": row(m[o[5]:]),
        "w0": row(w0[l]), "a0": row(a0[l]), "k_k": row(k_k[l]), "k_a": row(k_a[l]), "r_k": row(r_k[l]),
        "ln_w": row(ln_w[l]), "ln_b": row(ln_b[l]),
        "w2": _pad_rows(w2[l], LORA_PAD).astype(BF16), "a2": _pad_rows(a2[l], LORA_PAD).astype(BF16),
        "g2": g2[l].astype(BF16),
    }


def kernel(x, c, t5_bias, ada_w, ada_b, norm_g, ffn_w_in, ffn_w_out, w_in, ckv_norm_g, w_uk, w_uv, rwkv_mu, rwkv_w0, rwkv_w2, rwkv_a0, rwkv_a2, rwkv_g2, rwkv_k_k, rwkv_k_a, rwkv_r_k, rwkv_ln_w, rwkv_ln_b, w_out, final_norm_g):
    b, s, d = x.shape
    depth = ada_w.shape[0]
    assert s % DSA_TILE == 0 and s % RWKV_TILE == 0 and d == A_WIDTH + R_WIDTH
    mod = _ada_mod(c, ada_w, ada_b).reshape(depth, b, N_SUB, 3, 1, d)
    bias = _bias_tiles(t5_bias)
    final_g = final_norm_g.reshape(1, d)
    h = x
    for l in range(depth):
        shift = lambda i: mod[l, :, i, 0]
        scale = lambda i: mod[l, :, i, 1]
        gate = lambda i: mod[l, :, i, 2]
        g = lambda i: norm_g[l, i].reshape(1, d)
        h = _ffn(h, g(0), shift(0), scale(0), gate(0), ffn_w_in[l, 0].astype(BF16), ffn_w_out[l, 0].astype(BF16),
                 final_g, final_norm=False)
        z = _proj(h, g(1), shift(1), scale(1), _layout_w_in(w_in[l]))
        qt, qit, ckv, ckvt, ki, wit = _dsa_prep(z, ckv_norm_g[l].reshape(1, KV_RANK))
        wuk = jnp.transpose(w_uk[l], (1, 0, 2)).astype(BF16)
        wuvt = jnp.transpose(w_uv[l], (1, 2, 0)).astype(BF16)
        o_a = _dsa(qt, qit, ckv, ckvt, ki, wit, wuk, wuvt, bias)
        o_r = _rwkv(z, _rwkv_params(l, rwkv_mu, rwkv_w0, rwkv_w2, rwkv_a0, rwkv_a2, rwkv_g2, rwkv_k_k,
                                    rwkv_k_a, rwkv_r_k, rwkv_ln_w, rwkv_ln_b))
        h = _outproj(o_a, o_r, h, gate(1), w_out[l].astype(BF16))
        h = _ffn(h, g(2), shift(2), scale(2), gate(2), ffn_w_in[l, 1].astype(BF16), ffn_w_out[l, 1].astype(BF16),
                 final_g, final_norm=(l == depth - 1))
    return h
```

```python
import functools
import math

import jax
import jax.numpy as jnp
from jax import lax
from jax.experimental import pallas as pl
from jax.experimental.pallas import tpu as pltpu

F32 = jnp.float32
BF16 = jnp.bfloat16
I32 = jnp.int32
I16 = jnp.int16

A_HEADS = 8
A_HEAD_DIM = 128
A_WIDTH = A_HEADS * A_HEAD_DIM
KV_RANK = 256
IDX_HEADS = 16
IDX_DIM = 64
TOPK_MAX = 256
REL_BUCKETS = 32
REL_MAX_EXACT = REL_BUCKETS // 2
REL_MAX_DIST = 128
R_HEAD_DIM = 64
R_WIDTH = 1024
R_HEADS = R_WIDTH // R_HEAD_DIM
DECAY_LORA = 96
AAA_LORA = 96
GATE_LORA = 256
GN_EPS = 64e-5
RMS_EPS = 1e-6
N_SUB = 3

LANES = 128
SUBLANES = 8
VMEM_LIMIT_BYTES = 56 * 1024 * 1024

LORA_PAD = 128
Z_Q = 0
Z_QI = Z_Q + A_WIDTH
Z_R = Z_QI + IDX_HEADS * IDX_DIM
Z_K = Z_R + R_WIDTH
Z_V = Z_K + R_WIDTH
Z_GL = Z_V + R_WIDTH
Z_CKV = Z_GL + GATE_LORA
Z_KIWI = Z_CKV + KV_RANK
Z_WL = Z_KIWI + LORA_PAD
Z_AL = Z_WL + LORA_PAD
Z_END = Z_AL + LORA_PAD
Z_WIDTH = 6144

FFN_TOKEN_TILE = 1024
FFN_HIDDEN_TILE = 256
ADALN_SLAB_ROWS = 16
ADALN_UNROLL = 8
DSA_TILE = 256
RWKV_CHUNK = 64
RWKV_TILE = 256
PAIR = 2 * R_HEAD_DIM

I16_MIN = -(2 ** 15)
PACKED_ROWS = 2 * SUBLANES
SUM_ROWS = PACKED_ROWS
BF16_EXACT_INT = 256
COUNT_CHAINS = 4
KEY_NEG_INF = -2139095041
MASKED_LOGIT = -(2.0 ** 100)
LOG2E = math.log2(math.e)
FAR_DISTANCE = math.ceil(REL_MAX_EXACT * (REL_MAX_DIST / REL_MAX_EXACT)
                         ** ((REL_BUCKETS - 1 - REL_MAX_EXACT) / (REL_BUCKETS - REL_MAX_EXACT)))


def _dot(a, b):
    return jnp.dot(a, b, preferred_element_type=F32)


def _dot_nt(a, b):
    return lax.dot_general(a, b, (((1,), (1,)), ((), ())), preferred_element_type=F32)


def _bdot(a, b):
    return lax.dot_general(a, b, (((2,), (1,)), ((0,), (0,))), preferred_element_type=F32)


def _bdot_nt(a, b):
    return lax.dot_general(a, b, (((2,), (2,)), ((0,), (0,))), preferred_element_type=F32)


def _rms(x, g, eps):
    ms = jnp.mean(x * x, axis=-1, keepdims=True)
    return x * lax.rsqrt(ms + eps) * g


def _divisor_tile(n, pref):
    if n <= pref:
        return n
    t = (pref // LANES) * LANES
    while t > LANES and n % t:
        t -= LANES
    assert n % t == 0, (n, pref)
    return t


def _params(*sem):
    return pltpu.CompilerParams(dimension_semantics=sem, vmem_limit_bytes=VMEM_LIMIT_BYTES)


def _ada_kernel(c_ref, w_ref, b_ref, o_ref):
    c = c_ref[...]
    ca = (c * jax.nn.sigmoid(c)).astype(BF16)
    o_ref[0] = _dot(ca, w_ref[0].astype(BF16)) + b_ref[0]


def _ada_mod(c, ada_w, ada_b):
    depth, d, n = ada_w.shape
    b = c.shape[0]
    bp = -(-b // SUBLANES) * SUBLANES
    cp = jnp.pad(c, ((0, bp - b), (0, 0)))
    tn = _divisor_tile(n, 1024)
    out = pl.pallas_call(
        _ada_kernel,
        out_shape=jax.ShapeDtypeStruct((depth, bp, n), F32),
        grid=(depth, n // tn),
        in_specs=[
            pl.BlockSpec((bp, d), lambda l, j: (0, 0)),
            pl.BlockSpec((1, d, tn), lambda l, j: (l, 0, j)),
            pl.BlockSpec((1, 1, tn), lambda l, j: (l, 0, j)),
        ],
        out_specs=pl.BlockSpec((1, bp, tn), lambda l, j: (l, 0, j)),
        compiler_params=_params("arbitrary", "arbitrary"),
        name="ada_mod",
    )(cp, ada_w, ada_b.reshape(depth, 1, n))
    return out[:, :b]


def _adaln_rows(h_ref, g_ref, scale_ref, shift_ref, hn_ref, copy_ref=None):
    rows = h_ref.shape[1]
    g = g_ref[...]
    mul = 1.0 + scale_ref[0]
    add = shift_ref[0]

    def slab(i, carry):
        sl = pl.ds(pl.multiple_of(i * ADALN_SLAB_ROWS, ADALN_SLAB_ROWS), ADALN_SLAB_ROWS)
        h = h_ref[0, sl, :]
        hn_ref[sl, :] = (_rms(h, g, RMS_EPS) * mul + add).astype(BF16)
        if copy_ref is not None:
            copy_ref[0, sl, :] = h
        return carry

    lax.fori_loop(0, rows // ADALN_SLAB_ROWS, slab, 0, unroll=ADALN_UNROLL)


def _ffn_kernel(h_ref, g_ref, shift_ref, scale_ref, gate_ref, wg_ref, wu_ref, wo_ref, fg_ref,
                o_ref, hn_ref, *, final_norm):
    f = pl.program_id(2)

    @pl.when(f == 0)
    def _():
        _adaln_rows(h_ref, g_ref, scale_ref, shift_ref, hn_ref, o_ref)

    hn = hn_ref[...]
    g = _dot(hn, wg_ref[...])
    u = _dot(hn, wu_ref[...])
    act = (g * jax.nn.sigmoid(g) * u).astype(BF16)
    o_ref[0] += (0.5 * gate_ref[0]) * _dot(act, wo_ref[...])

    if final_norm:
        @pl.when(f == pl.num_programs(2) - 1)
        def _():
            o_ref[0] = _rms(o_ref[0], fg_ref[...], RMS_EPS)


def _ffn(h, g, shift, scale, gate, w_in, w_out, final_g, *, final_norm):
    b, s, d = h.shape
    ff = w_out.shape[0]
    tm = _divisor_tile(s, FFN_TOKEN_TILE)
    tf = _divisor_tile(ff, FFN_HIDDEN_TILE)
    nf = ff // tf
    vec = pl.BlockSpec((1, 1, d), lambda bi, i, f: (bi, 0, 0))
    row = pl.BlockSpec((1, d), lambda bi, i, f: (0, 0))
    tile = pl.BlockSpec((1, tm, d), lambda bi, i, f: (bi, i, 0))
    return pl.pallas_call(
        functools.partial(_ffn_kernel, final_norm=final_norm),
        out_shape=jax.ShapeDtypeStruct((b, s, d), F32),
        grid=(b, s // tm, nf),
        in_specs=[
            tile, row, vec, vec, vec,
            pl.BlockSpec((d, tf), lambda bi, i, f: (0, f)),
            pl.BlockSpec((d, tf), lambda bi, i, f: (0, nf + f)),
            pl.BlockSpec((tf, d), lambda bi, i, f: (f, 0)),
            row,
        ],
        out_specs=tile,
        scratch_shapes=[pltpu.VMEM((tm, d), BF16)],
        compiler_params=_params("parallel", "parallel", "arbitrary"),
        name="ffn",
    )(h, g, shift, scale, gate, w_in, w_in, w_out, final_g)


def _proj_kernel(h_ref, g_ref, shift_ref, scale_ref, w_ref, o_ref, hn_ref):
    @pl.when(pl.program_id(2) == 0)
    def _():
        _adaln_rows(h_ref, g_ref, scale_ref, shift_ref, hn_ref)

    o_ref[0] = _dot(hn_ref[...], w_ref[...])


def _proj(h, g, shift, scale, w):
    b, s, d = h.shape
    p = w.shape[1]
    tm = _divisor_tile(s, 1024)
    tn = _divisor_tile(p, 1024)
    vec = pl.BlockSpec((1, 1, d), lambda bi, i, n: (bi, 0, 0))
    return pl.pallas_call(
        _proj_kernel,
        out_shape=jax.ShapeDtypeStruct((b, s, p), F32),
        grid=(b, s // tm, p // tn),
        in_specs=[
            pl.BlockSpec((1, tm, d), lambda bi, i, n: (bi, i, 0)),
            pl.BlockSpec((1, d), lambda bi, i, n: (0, 0)),
            vec, vec,
            pl.BlockSpec((d, tn), lambda bi, i, n: (0, n)),
        ],
        out_specs=pl.BlockSpec((1, tm, tn), lambda bi, i, n: (bi, i, n)),
        scratch_shapes=[pltpu.VMEM((tm, d), BF16)],
        compiler_params=_params("parallel", "parallel", "arbitrary"),
        name="proj",
    )(h, g, shift, scale, w)


def _outproj_kernel(oa_ref, or_ref, h_ref, gate_ref, wa_ref, wr_ref, o_ref):
    acc = _dot(oa_ref[0], wa_ref[...]) + _dot(or_ref[0], wr_ref[...])
    o_ref[0] = h_ref[0] + gate_ref[0] * acc


def _outproj(o_a, o_r, h, gate, w_out):
    b, s, d = h.shape
    tm = _divisor_tile(s, 1024)
    tn = _divisor_tile(d, 1024)
    wa = o_a.shape[-1]
    wr = o_r.shape[-1]
    return pl.pallas_call(
        _outproj_kernel,
        out_shape=jax.ShapeDtypeStruct((b, s, d), F32),
        grid=(b, s // tm, d // tn),
        in_specs=[
            pl.BlockSpec((1, tm, wa), lambda bi, i, n: (bi, i, 0)),
            pl.BlockSpec((1, tm, wr), lambda bi, i, n: (bi, i, 0)),
            pl.BlockSpec((1, tm, tn), lambda bi, i, n: (bi, i, n)),
            pl.BlockSpec((1, 1, tn), lambda bi, i, n: (bi, 0, n)),
            pl.BlockSpec((wa, tn), lambda bi, i, n: (0, n)),
            pl.BlockSpec((wr, tn), lambda bi, i, n: (wa // wr, n)),
        ],
        out_specs=pl.BlockSpec((1, tm, tn), lambda bi, i, n: (bi, i, n)),
        compiler_params=_params("parallel", "parallel", "arbitrary"),
        name="outproj",
    )(o_a, o_r, h, gate, w_out, w_out)


def _dsa_prep_kernel(q_ref, qi_ref, ckv_ref, kiwi_ref, g_ref, qt_ref, qit_ref, ckv_o, ckvt_o, ki_o, wit_o):
    qt_ref[0] = q_ref[0].T.astype(BF16)
    qit_ref[0] = qi_ref[0].T.astype(BF16)
    cn = _rms(ckv_ref[0], g_ref[...], RMS_EPS)
    ckv_o[0] = cn.astype(BF16)
    ckvt_o[0, 0] = jnp.concatenate([cn.T.astype(BF16), jnp.ones((SUM_ROWS, cn.shape[0]), BF16)], axis=0)
    kw = kiwi_ref[0]
    ki_o[0] = kw[:, :IDX_DIM].astype(BF16)
    wit_o[0] = kw.T[IDX_DIM:IDX_DIM + IDX_HEADS, :] * (IDX_HEADS * IDX_DIM) ** -0.5


def _dsa_prep(z, ckv_g):
    b, s, _ = z.shape
    t = DSA_TILE
    nt = s // t

    def seg(width, off):
        return pl.BlockSpec((1, t, width), lambda bi, i: (bi, i, off // width))

    return pl.pallas_call(
        _dsa_prep_kernel,
        out_shape=(
            jax.ShapeDtypeStruct((b, A_WIDTH, s), BF16),
            jax.ShapeDtypeStruct((b, IDX_HEADS * IDX_DIM, s), BF16),
            jax.ShapeDtypeStruct((b, s, KV_RANK), BF16),
            jax.ShapeDtypeStruct((b, nt, KV_RANK + SUM_ROWS, t), BF16),
            jax.ShapeDtypeStruct((b, s, IDX_DIM), BF16),
            jax.ShapeDtypeStruct((b, IDX_HEADS, s), F32),
        ),
        grid=(b, nt),
        in_specs=[
            seg(A_WIDTH, Z_Q), seg(IDX_HEADS * IDX_DIM, Z_QI), seg(KV_RANK, Z_CKV), seg(LORA_PAD, Z_KIWI),
            pl.BlockSpec((1, KV_RANK), lambda bi, i: (0, 0)),
        ],
        out_specs=(
            pl.BlockSpec((1, A_WIDTH, t), lambda bi, i: (bi, 0, i)),
            pl.BlockSpec((1, IDX_HEADS * IDX_DIM, t), lambda bi, i: (bi, 0, i)),
            pl.BlockSpec((1, t, KV_RANK), lambda bi, i: (bi, i, 0)),
            pl.BlockSpec((1, 1, KV_RANK + SUM_ROWS, t), lambda bi, i: (bi, i, 0, 0)),
            pl.BlockSpec((1, t, IDX_DIM), lambda bi, i: (bi, i, 0)),
            pl.BlockSpec((1, IDX_HEADS, t), lambda bi, i: (bi, 0, i)),
        ),
        compiler_params=_params("parallel", "parallel"),
        name="dsa_prep",
    )(z, z, z, z, ckv_g)


def _bias_kernel(t5_ref, o_ref):
    t = DSA_TILE
    j = lax.broadcasted_iota(I32, (t, t), 0)
    i = lax.broadcasted_iota(I32, (t, t), 1)
    for didx in range(2):
        n = jnp.maximum(didx * t + i - j, 0)
        nf = jnp.maximum(n, 1).astype(F32)
        large = REL_MAX_EXACT + (jnp.log(nf / REL_MAX_EXACT) / math.log(REL_MAX_DIST / REL_MAX_EXACT)
                                 * (REL_BUCKETS - REL_MAX_EXACT)).astype(I32)
        large = jnp.minimum(large, REL_BUCKETS - 1)
        bucket = jnp.where(n < REL_MAX_EXACT, n, large)
        for h in range(A_HEADS):
            val = jnp.zeros((t, t), F32)
            for k in range(REL_BUCKETS):
                val = jnp.where(bucket == k, t5_ref[k, h], val)
            o_ref[didx, :, h * t:(h + 1) * t] = (val - t5_ref[REL_BUCKETS - 1, h]) * LOG2E


def _bias_tiles(t5_bias):
    t = DSA_TILE
    assert t + 1 >= FAR_DISTANCE
    return pl.pallas_call(
        _bias_kernel,
        out_shape=jax.ShapeDtypeStruct((2, t, A_HEADS * t), F32),
        in_specs=[pl.BlockSpec(memory_space=pltpu.SMEM)],
        out_specs=pl.BlockSpec(memory_space=pltpu.VMEM),
        compiler_params=pltpu.CompilerParams(vmem_limit_bytes=VMEM_LIMIT_BYTES),
        name="t5_bias_tiles",
    )(t5_bias)


def _dsa_kernel(qt_ref, qit_ref, ckv_ref, ckvt_ref, ki_ref, wit_ref, wuk_ref, wuvt_ref, bias_ref,
                o_ref, keys_ref, khi_ref, klo_ref, acc_ref, ot_ref, *, topk, seq_len):
    t = DSA_TILE
    qb = pl.program_id(1)
    nk = qb + 1
    row = lax.broadcasted_iota(I32, (t, t), 0)
    col = lax.broadcasted_iota(I32, (t, t), 1)

    def idx_body(kc, carry):
        kic = ki_ref[0, pl.ds(pl.multiple_of(kc * t, t), t), :]
        acc = jnp.zeros((t, t), F32)
        for h in range(IDX_HEADS):
            rel = _dot(kic, qit_ref[0, h * IDX_DIM:(h + 1) * IDX_DIM, :])
            acc = acc + jnp.maximum(rel, 0.0) * wit_ref[0, h:h + 1, :]
        acc = acc + 0.0
        bits = pltpu.bitcast(acc, I32)
        key = bits ^ ((bits >> 31) & 0x7FFFFFFF)
        causal = (kc * t + row) <= (qb * t + col)
        key = jnp.where(causal, key, KEY_NEG_INF)
        sl = pl.ds(pl.multiple_of(kc * t, t), t)
        keys_ref[sl, :] = key
        khi_ref[sl, :] = (key >> 16).astype(I16)
        klo_ref[sl, :] = ((key & 0xFFFF) + I16_MIN).astype(I16)
        return carry

    lax.fori_loop(0, nk, idx_body, 0)

    def count16(ref, cand):
        cand16 = cand.astype(I16)

        def body(kc, accs):
            x = ref[pl.ds(pl.multiple_of(kc * t, t), t), :]
            ones = jnp.where(x >= cand16, jnp.asarray(1, BF16), jnp.asarray(0, BF16))
            ones = ones.reshape(t // PACKED_ROWS, PACKED_ROWS, t)
            accs = list(accs)
            for i in range(t // PACKED_ROWS):
                accs[i % COUNT_CHAINS] = accs[i % COUNT_CHAINS] + ones[i]
            return tuple(accs)

        accs = (jnp.zeros((PACKED_ROWS, t), BF16),) * COUNT_CHAINS
        accs = lax.fori_loop(0, nk // 2, lambda i, a: body(2 * i + 1, body(2 * i, a)), accs)
        accs = lax.fori_loop(2 * (nk // 2), nk, body, accs)
        total = sum(a.astype(F32) for a in accs)
        return jnp.sum(total, axis=0, keepdims=True).astype(I32)

    def search16(ref, base):
        def accept(cand, cur):
            return jnp.where(base + count16(ref, cand) >= topk, cand, cur)

        v = accept(jnp.zeros((1, t), I32), jnp.full((1, t), I16_MIN, I32))
        return lax.fori_loop(0, 15, lambda i, v: accept(v | (jnp.int32(1) << (14 - i)), v), v)

    hi = search16(khi_ref, 0)
    above = count16(khi_ref, jnp.minimum(hi + 1, -I16_MIN - 1))
    hi16 = hi.astype(I16)

    def keep_low(kc, carry):
        sl = pl.ds(pl.multiple_of(kc * t, t), t)
        klo_ref[sl, :] = jnp.where(khi_ref[sl, :] == hi16, klo_ref[sl, :], jnp.asarray(I16_MIN, I16))
        return carry

    lax.fori_loop(0, nk, keep_low, 0)
    lo = search16(klo_ref, above)
    thr = jnp.maximum((hi << 16) | (lo - I16_MIN), KEY_NEG_INF + 1)

    def count(pred):
        def body(kc, acc):
            k = keys_ref[pl.ds(pl.multiple_of(kc * t, t), t), :]
            m = jnp.where(pred(k, kc), 1, 0).astype(I32)
            return acc + jnp.sum(m.reshape(t // SUBLANES, SUBLANES, t), axis=0)

        acc = lax.fori_loop(0, nk, body, jnp.zeros((SUBLANES, t), I32))
        return jnp.sum(acc, axis=0, keepdims=True)

    @pl.when(jnp.max(count(lambda k, kc: k >= thr)) > topk)
    def _():
        need = topk - count(lambda k, kc: k > thr)
        nbits = (seq_len - 1).bit_length()

        def pos_bit(i, y):
            cand = y | (jnp.int32(1) << (nbits - 1 - i))
            before = count(lambda k, kc: (k == thr) & ((kc * t + row) < cand))
            return jnp.where(before < need, cand, y)

        last_kept = lax.fori_loop(0, nbits, pos_bit, jnp.zeros((1, t), I32))

        def demote(kc, carry):
            sl = pl.ds(pl.multiple_of(kc * t, t), t)
            k = keys_ref[sl, :]
            keys_ref[sl, :] = jnp.where((k == thr) & ((kc * t + row) > last_kept), k - 1, k)
            return carry

        lax.fori_loop(0, nk, demote, 0)

    scale = A_HEAD_DIM ** -0.5 * LOG2E

    hw = A_HEADS * t
    qlat = jnp.concatenate(
        [(_dot(wuk_ref[h], qt_ref[0, h * A_HEAD_DIM:(h + 1) * A_HEAD_DIM, :]) * scale).astype(BF16)
         for h in range(A_HEADS)], axis=1)
    acc_ref[...] = jnp.zeros_like(acc_ref)

    def att_body(kc, m, *, near):
        sl = pl.ds(pl.multiple_of(kc * t, t), t)
        raw = _dot(ckv_ref[0, sl, :], qlat)
        sel = keys_ref[sl, :] >= thr
        parts = []
        for h in range(A_HEADS):
            lg_h = raw[:, h * t:(h + 1) * t]
            if near:
                lg_h = lg_h + bias_ref[qb - kc, :, h * t:(h + 1) * t]
            parts.append(jnp.where(sel, lg_h, MASKED_LOGIT).astype(BF16))
        lg = jnp.concatenate(parts, axis=1)
        m_new = jnp.maximum(m, jnp.max(lg, axis=0, keepdims=True).astype(F32))
        p = jnp.exp2(lg - m_new.astype(BF16))
        acc_ref[...] = acc_ref[...] * jnp.exp2(m - m_new) + _dot(ckvt_ref[0, kc], p)
        return m_new

    n_far = jnp.maximum(qb - 1, 0)
    m = lax.fori_loop(0, n_far, functools.partial(att_body, near=False), jnp.full((1, hw), MASKED_LOGIT, F32))
    lax.fori_loop(n_far, nk, functools.partial(att_body, near=True), m)
    olat = (acc_ref[:KV_RANK, :] / acc_ref[KV_RANK:KV_RANK + 1, :]).astype(BF16)
    for h in range(A_HEADS):
        ot_ref[h * A_HEAD_DIM:(h + 1) * A_HEAD_DIM, :] = _dot(wuvt_ref[h], olat[:, h * t:(h + 1) * t])
    o_ref[0] = ot_ref[...].T.astype(BF16)


def _dsa(qt, qit, ckv, ckvt, ki, wit, wuk, wuvt, bias):
    b, _, s = qt.shape
    t = DSA_TILE
    nt = s // t
    topk = min(TOPK_MAX, s // 4)
    assert s // PACKED_ROWS <= BF16_EXACT_INT
    return pl.pallas_call(
        functools.partial(_dsa_kernel, topk=topk, seq_len=s),
        out_shape=jax.ShapeDtypeStruct((b, s, A_WIDTH), BF16),
        grid=(b, nt),
        in_specs=[
            pl.BlockSpec((1, A_WIDTH, t), lambda bi, i: (bi, 0, i)),
            pl.BlockSpec((1, IDX_HEADS * IDX_DIM, t), lambda bi, i: (bi, 0, i)),
            pl.BlockSpec((1, s, KV_RANK), lambda bi, i: (bi, 0, 0)),
            pl.BlockSpec((1, nt, KV_RANK + SUM_ROWS, t), lambda bi, i: (bi, 0, 0, 0)),
            pl.BlockSpec((1, s, IDX_DIM), lambda bi, i: (bi, 0, 0)),
            pl.BlockSpec((1, IDX_HEADS, t), lambda bi, i: (bi, 0, i)),
            pl.BlockSpec((A_HEADS, KV_RANK, A_HEAD_DIM), lambda bi, i: (0, 0, 0)),
            pl.BlockSpec((A_HEADS, A_HEAD_DIM, KV_RANK), lambda bi, i: (0, 0, 0)),
            pl.BlockSpec((2, t, A_HEADS * t), lambda bi, i: (0, 0, 0)),
        ],
        out_specs=pl.BlockSpec((1, t, A_WIDTH), lambda bi, i: (bi, i, 0)),
        scratch_shapes=[pltpu.VMEM((s, t), I32), pltpu.VMEM((s, t), I16), pltpu.VMEM((s, t), I16),
                        pltpu.VMEM((KV_RANK + SUM_ROWS, A_HEADS * t), F32), pltpu.VMEM((A_WIDTH, t), F32)],
        compiler_params=_params("parallel", "arbitrary"),
        name="dsa",
    )(qt, qit, ckv, ckvt, ki, wit, wuk, wuvt, bias)


def _split3(x):
    hi = x.astype(BF16)
    r1 = x - hi.astype(F32)
    mid = r1.astype(BF16)
    lo = (r1 - mid.astype(F32)).astype(BF16)
    return hi, mid, lo


def _rwkv_kernel(r_ref, k_ref, v_ref, gl_ref, wl_ref, al_ref,
                 mur_ref, muk_ref, muv_ref, mug_ref, muw_ref, mua_ref,
                 w0_ref, a0_ref, kk_ref, ka_ref, rk_ref, lnw_ref, lnb_ref,
                 w2_ref, a2_ref, g2_ref,
                 o_ref,
                 pr_ref, pk_ref, pv_ref, pg_ref, pw_ref, pa_ref, state_ref):
    tl = RWKV_TILE
    c = RWKV_CHUNK
    nc = tl // c
    npair = R_WIDTH // PAIR

    @pl.when(pl.program_id(1) == 0)
    def _():
        state_ref[...] = jnp.zeros_like(state_ref)
        for ref in (pr_ref, pk_ref, pv_ref, pg_ref, pw_ref, pa_ref):
            ref[...] = jnp.zeros_like(ref)

    def shift_mix(x_ref, prev_ref, mu_ref):
        x = x_ref[0]
        first = lax.broadcasted_iota(I32, x.shape, 0) == 0
        xprev = jnp.where(first, prev_ref[...], pltpu.roll(x, 1, axis=0))
        prev_ref[...] = x[tl - 1:tl, :]
        return x + (xprev - x) * mu_ref[...]

    r = shift_mix(r_ref, pr_ref, mur_ref)
    k = shift_mix(k_ref, pk_ref, muk_ref)
    v = shift_mix(v_ref, pv_ref, muv_ref)
    gl = shift_mix(gl_ref, pg_ref, mug_ref)
    wl = shift_mix(wl_ref, pw_ref, muw_ref)
    al = shift_mix(al_ref, pa_ref, mua_ref)

    w_lin = w0_ref[...] + _dot(jnp.tanh(wl).astype(BF16), w2_ref[...])
    nx = -w_lin
    softplus = jnp.maximum(nx, 0.0) + jnp.log(1.0 + jnp.exp(-jnp.abs(nx)))
    ld = -jnp.exp(-softplus - 0.5)
    a = jax.nn.sigmoid(a0_ref[...] + _dot(al.astype(BF16), a2_ref[...]))
    gate = _dot(jax.nn.sigmoid(gl).astype(BF16), g2_ref[...])

    li = lax.broadcasted_iota(I32, (LANES, LANES), 0) // R_HEAD_DIM
    lj = lax.broadcasted_iota(I32, (LANES, LANES), 1) // R_HEAD_DIM
    head_ones = jnp.where(li == lj, 1.0, 0.0).astype(BF16)

    def head_sum(x):
        parts = []
        for j in range(R_WIDTH // LANES):
            hi, mid, lo = _split3(x[:, j * LANES:(j + 1) * LANES])
            parts.append(_dot(hi, head_ones) + _dot(mid, head_ones) + _dot(lo, head_ones))
        return jnp.concatenate(parts, axis=-1)

    kk = k * kk_ref[...]
    kk = kk / jnp.maximum(jnp.sqrt(head_sum(kk * kk)), 1e-12)
    k2 = k * (1.0 + (a - 1.0) * ka_ref[...])
    bb = kk * a

    ti = lax.broadcasted_iota(I32, (tl, tl), 0)
    tj = lax.broadcasted_iota(I32, (tl, tl), 1)
    tri = jnp.where((ti // c == tj // c) & (tj <= ti), 1.0, 0.0).astype(BF16)
    hi, mid, lo = _split3(ld)
    cum = _dot(tri, hi) + _dot(tri, mid) + _dot(tri, lo)
    cum_end = jnp.broadcast_to(cum.reshape(nc, c, R_WIDTH)[:, c - 1:c, :], (nc, c, R_WIDTH)).reshape(tl, R_WIDTH)
    e_in = jnp.exp(cum)
    e_neg = jnp.exp(-cum)
    e_out = jnp.exp(cum_end - cum)

    nb = nc * npair

    def to_pairs(x):
        x3 = x.reshape(nc, c, R_WIDTH)
        xs = jnp.stack([x3[:, :, p * PAIR:(p + 1) * PAIR] for p in range(npair)], axis=1)
        return xs.reshape(nb, c, PAIR)

    lane_head = lax.broadcasted_iota(I32, (2 * c, PAIR), 1) // R_HEAD_DIM
    row_head = lax.broadcasted_iota(I32, (2 * c, PAIR), 0) // c
    same = lane_head == row_head
    rt = lax.broadcasted_iota(I32, (2 * c, PAIR), 0) % c
    ct = lax.broadcasted_iota(I32, (2 * c, PAIR), 1) % c
    strict = same & (ct < rt)
    incl = same & (ct <= rt)
    eye = jnp.where(same & (ct == rt), 1.0, 0.0).astype(F32)

    def bd(x):
        return jnp.where(same, jnp.concatenate([x, x], axis=1), 0.0)

    lane0 = lax.broadcasted_iota(I32, (c, PAIR), 1) < R_HEAD_DIM

    abar = bd(to_pairs(-kk * jnp.exp(cum - ld))).astype(BF16)
    rbar = bd(to_pairs(r * e_in)).astype(BF16)
    bt = bd(to_pairs(bb * e_neg)).astype(BF16)
    kt = bd(to_pairs(k2 * e_neg)).astype(BF16)
    bk = jnp.concatenate([bd(to_pairs(bb * e_out)), bd(to_pairs(k2 * e_out))], axis=1).astype(BF16)
    v_pl = to_pairs(v)
    vbd = bd(v_pl)
    decay_end = to_pairs(jnp.exp(cum_end))[:, 0:1, :]
    aa = _bdot_nt(jnp.concatenate([abar, rbar], axis=1), jnp.concatenate([bt, kt], axis=1))
    a_ab = jnp.where(strict, aa[:, :2 * c, :PAIR], 0.0)
    a_ak = jnp.where(strict, aa[:, :2 * c, PAIR:], 0.0).astype(BF16)
    a_rb = jnp.where(incl, aa[:, 2 * c:, :PAIR], 0.0).astype(BF16)
    a_rk = jnp.where(incl, aa[:, 2 * c:, PAIR:], 0.0).astype(BF16)
    pw = a_ab.astype(BF16)
    tinv = eye + a_ab
    for _ in range(int(math.log2(c)) - 1):
        pw = _bdot(pw, pw).astype(BF16)
        tinv = tinv + _bdot(tinv.astype(BF16), pw)
    akv = _bdot(a_ak, vbd.astype(BF16))
    wu = _bdot(tinv.astype(BF16), jnp.concatenate([abar, akv.astype(BF16)], axis=2))
    w16 = wu[:, :, :PAIR].astype(BF16)
    u_bd = wu[:, :, PAIR:]

    s_bd = state_ref[...]
    e_parts, rs_parts = [], []
    for ci in range(nc):
        sl = slice(ci * npair, (ci + 1) * npair)
        wr = _bdot_nt(jnp.concatenate([w16[sl], rbar[sl]], axis=1), s_bd.astype(BF16))
        e_bd = wr[:, :2 * c] + u_bd[sl]
        e_parts.append(e_bd)
        rs_parts.append(wr[:, 2 * c:])
        ev = jnp.concatenate([e_bd, vbd[sl]], axis=1)
        ds = _bdot(jnp.swapaxes(ev, 1, 2).astype(BF16), bk[sl])
        s_bd = s_bd * decay_end[sl] + jnp.where(same, ds, 0.0)
    state_ref[...] = s_bd

    ev_all = jnp.concatenate([jnp.concatenate(e_parts, axis=0), vbd], axis=1).astype(BF16)
    y_bd = jnp.concatenate(rs_parts, axis=0) + _bdot(jnp.concatenate([a_rb, a_rk], axis=2), ev_all)
    mean = jnp.sum(y_bd, axis=-1, keepdims=True) * (1.0 / R_HEAD_DIM)
    dev = jnp.where(same, y_bd - mean, 0.0)
    var = jnp.sum(dev * dev, axis=-1, keepdims=True) * (1.0 / R_HEAD_DIM)
    yn = dev * lax.rsqrt(var + GN_EPS)
    yn = yn[:, :c] + yn[:, c:]
    rkv = to_pairs(r * k2 * rk_ref[...])
    s0 = jnp.sum(jnp.where(lane0, rkv, 0.0), axis=-1, keepdims=True)
    s1 = jnp.sum(jnp.where(lane0, 0.0, rkv), axis=-1, keepdims=True)
    bonus = jnp.where(lane0, s0, s1) * v_pl
    def per_pair(row_ref):
        rows = [row_ref[:, p * PAIR:(p + 1) * PAIR] for p in range(npair)]
        return jnp.stack(rows * nc, axis=0)

    out = ((yn * per_pair(lnw_ref) + per_pair(lnb_ref) + bonus) * to_pairs(gate)).astype(BF16)
    for ci in range(nc):
        for p in range(npair):
            o_ref[0, ci * c:(ci + 1) * c, p * PAIR:(p + 1) * PAIR] = out[ci * npair + p]


def _rwkv(z, prm):
    b, s, _ = z.shape
    tl = RWKV_TILE

    def seg(width, off):
        return pl.BlockSpec((1, tl, width), lambda bi, i: (bi, i, off // width))

    def full(arr):
        return pl.BlockSpec(arr.shape, lambda bi, i: (0,) * arr.ndim)

    names = ("mu_r", "mu_k", "mu_v", "mu_g", "mu_w", "mu_a", "w0", "a0", "k_k", "k_a", "r_k", "ln_w", "ln_b",
             "w2", "a2", "g2")
    consts = [prm[n] for n in names]
    return pl.pallas_call(
        _rwkv_kernel,
        out_shape=jax.ShapeDtypeStruct((b, s, R_WIDTH), BF16),
        grid=(b, s // tl),
        in_specs=[seg(R_WIDTH, Z_R), seg(R_WIDTH, Z_K), seg(R_WIDTH, Z_V), seg(GATE_LORA, Z_GL),
                  seg(LORA_PAD, Z_WL), seg(LORA_PAD, Z_AL)] + [full(x) for x in consts],
        out_specs=pl.BlockSpec((1, tl, R_WIDTH), lambda bi, i: (bi, i, 0)),
        scratch_shapes=[
            pltpu.VMEM((1, R_WIDTH), F32), pltpu.VMEM((1, R_WIDTH), F32), pltpu.VMEM((1, R_WIDTH), F32),
            pltpu.VMEM((1, GATE_LORA), F32), pltpu.VMEM((1, LORA_PAD), F32), pltpu.VMEM((1, LORA_PAD), F32),
            pltpu.VMEM((R_WIDTH // PAIR, PAIR, PAIR), F32),
        ],
        compiler_params=_params("parallel", "arbitrary"),
        name="rwkv7",
    )(z, z, z, z, z, z, *consts)


def _pad_cols(w, width):
    return jnp.pad(w, ((0, 0), (0, width - w.shape[1])))


def _w_in_layout_kernel(w_ref, o_ref):
    w = w_ref[0]
    rows = w.shape[0]
    offs = [0]
    for n in (A_WIDTH, KV_RANK, IDX_HEADS * IDX_DIM, IDX_DIM + IDX_HEADS,
              3 * R_WIDTH, DECAY_LORA, AAA_LORA, GATE_LORA):
        offs.append(offs[-1] + n)
    q, ckv, qi, kiwi, rkv, wl, al, gl = (w[:, offs[i]:offs[i + 1]] for i in range(8))

    def padded(x, width):
        return jnp.concatenate([x, jnp.zeros((rows, width - x.shape[1]), x.dtype)], axis=1)

    cols = [q, qi, rkv, gl, ckv, padded(kiwi, LORA_PAD), padded(wl, LORA_PAD), padded(al, LORA_PAD),
            jnp.zeros((rows, Z_WIDTH - Z_END), w.dtype)]
    o_ref[0] = jnp.concatenate(cols, axis=1).astype(BF16)


def _layout_w_in(w_in):
    depth, d, p = w_in.shape
    tr = _divisor_tile(d, 256)
    return pl.pallas_call(
        _w_in_layout_kernel,
        out_shape=jax.ShapeDtypeStruct((depth, d, Z_WIDTH), BF16),
        grid=(depth, d // tr),
        in_specs=[pl.BlockSpec((1, tr, p), lambda l, i: (l, i, 0))],
        out_specs=pl.BlockSpec((1, tr, Z_WIDTH), lambda l, i: (l, i, 0)),
        compiler_params=_params("parallel", "parallel"),
        name="w_in_layout",
    )(w_in)


def _pad_rows(w, rows):
    return jnp.pad(w, ((0, rows - w.shape[0]), (0, 0)))


def _rwkv_params(l, mu, w0, w2, a0, a2, g2, k_k, k_a, r_k, ln_w, ln_b):
    m = mu[l]
    o = [0, R_WIDTH, 2 * R_WIDTH, 3 * R_WIDTH, 3 * R_WIDTH + DECAY_LORA, 3 * R_WIDTH + DECAY_LORA + AAA_LORA]
    row = lambda x: x.reshape(1, -1)
    return {
        "mu_r": row(m[o[0]:o[1]]), "mu_k": row(m[o[1]:o[2]]), "mu_v": row(m[o[2]:o[3]]),
        "mu_w": _pad_cols(row(m[o[3]:o[4]]), LORA_PAD), "mu_a": _pad_cols(row(m[o[4]:o[5]]), LORA_PAD),
        "mu_g": row(m[o[5]:]),
        "w0": row(w0[l]), "a0": row(a0[l]), "k_k": row(k_k[l]), "k_a": row(k_a[l]), "r_k": row(r_k[l]),
        "ln_w": row(ln_w[l]), "ln_b": row(ln_b[l]),
        "w2": _pad_rows(w2[l], LORA_PAD).astype(BF16), "a2": _pad_rows(a2[l], LORA_PAD).astype(BF16),
        "g2": g2[l].astype(BF16),
    }


def kernel(x, c, t5_bias, ada_w, ada_b, norm_g, ffn_w_in, ffn_w_out, w_in, ckv_norm_g, w_uk, w_uv, rwkv_mu, rwkv_w0, rwkv_w2, rwkv_a0, rwkv_a2, rwkv_g2, rwkv_k_k, rwkv_k_a, rwkv_r_k, rwkv_ln_w, rwkv_ln_b, w_out, final_norm_g):
    b, s, d = x.shape
    depth = ada_w.shape[0]
    assert s % DSA_TILE == 0 and s % RWKV_TILE == 0 and d == A_WIDTH + R_WIDTH
    mod = _ada_mod(c, ada_w, ada_b).reshape(depth, b, N_SUB, 3, 1, d)
    bias = _bias_tiles(t5_bias)
    final_g = final_norm_g.reshape(1, d)
    w_in_z = _layout_w_in(w_in)
    h = x
    for l in range(depth):
        shift = lambda i: mod[l, :, i, 0]
        scale = lambda i: mod[l, :, i, 1]
        gate = lambda i: mod[l, :, i, 2]
        g = lambda i: norm_g[l, i].reshape(1, d)
        h = _ffn(h, g(0), shift(0), scale(0), gate(0), ffn_w_in[l, 0].astype(BF16), ffn_w_out[l, 0].astype(BF16),
                 final_g, final_norm=False)
        z = _proj(h, g(1), shift(1), scale(1), w_in_z[l])
        qt, qit, ckv, ckvt, ki, wit = _dsa_prep(z, ckv_norm_g[l].reshape(1, KV_RANK))
        wuk = jnp.transpose(w_uk[l], (1, 0, 2)).astype(BF16)
        wuvt = jnp.transpose(w_uv[l], (1, 2, 0)).astype(BF16)
        o_a = _dsa(qt, qit, ckv, ckvt, ki, wit, wuk, wuvt, bias)
        o_r = _rwkv(z, _rwkv_params(l, rwkv_mu, rwkv_w0, rwkv_w2, rwkv_a0, rwkv_a2, rwkv_g2, rwkv_k_k,
                                    rwkv_k_a, rwkv_r_k, rwkv_ln_w, rwkv_ln_b))
        h = _outproj(o_a, o_r, h, gate(1), w_out[l].astype(BF16))
        h = _ffn(h, g(2), shift(2), scale(2), gate(2), ffn_w_in[l, 1].astype(BF16), ffn_w_out[l, 1].astype(BF16),
                 final_g, final_norm=(l == depth - 1))
    return h
```

```python
import functools
import math

import jax
import jax.numpy as jnp
from jax import lax
from jax.experimental import pallas as pl
from jax.experimental.pallas import tpu as pltpu

F32 = jnp.float32
BF16 = jnp.bfloat16
I32 = jnp.int32
I16 = jnp.int16

A_HEADS = 8
A_HEAD_DIM = 128
A_WIDTH = A_HEADS * A_HEAD_DIM
KV_RANK = 256
IDX_HEADS = 16
IDX_DIM = 64
TOPK_MAX = 256
REL_BUCKETS = 32
REL_MAX_EXACT = REL_BUCKETS // 2
REL_MAX_DIST = 128
R_HEAD_DIM = 64
R_WIDTH = 1024
R_HEADS = R_WIDTH // R_HEAD_DIM
DECAY_LORA = 96
AAA_LORA = 96
GATE_LORA = 256
GN_EPS = 64e-5
RMS_EPS = 1e-6
N_SUB = 3

LANES = 128
SUBLANES = 8
VMEM_LIMIT_BYTES = 56 * 1024 * 1024

LORA_PAD = 128
Z_Q = 0
Z_QI = Z_Q + A_WIDTH
Z_R = Z_QI + IDX_HEADS * IDX_DIM
Z_K = Z_R + R_WIDTH
Z_V = Z_K + R_WIDTH
Z_GL = Z_V + R_WIDTH
Z_CKV = Z_GL + GATE_LORA
Z_KIWI = Z_CKV + KV_RANK
Z_WL = Z_KIWI + LORA_PAD
Z_AL = Z_WL + LORA_PAD
Z_END = Z_AL + LORA_PAD
Z_WIDTH = 6144

FFN_TOKEN_TILE = 1024
FFN_HIDDEN_TILE = 256
ADALN_SLAB_ROWS = 16
ADALN_UNROLL = 8
DSA_TILE = 256
RWKV_CHUNK = 64
RWKV_TILE = 256
PAIR = 2 * R_HEAD_DIM

I16_MIN = -(2 ** 15)
PACKED_ROWS = 2 * SUBLANES
SUM_ROWS = PACKED_ROWS
BF16_EXACT_INT = 256
COUNT_CHAINS = 4
KEY_NEG_INF = -2139095041
MASKED_LOGIT = -(2.0 ** 100)
LOG2E = math.log2(math.e)
FAR_DISTANCE = math.ceil(REL_MAX_EXACT * (REL_MAX_DIST / REL_MAX_EXACT)
                         ** ((REL_BUCKETS - 1 - REL_MAX_EXACT) / (REL_BUCKETS - REL_MAX_EXACT)))


def _dot(a, b):
    return jnp.dot(a, b, preferred_element_type=F32)


def _dot_nt(a, b):
    return lax.dot_general(a, b, (((1,), (1,)), ((), ())), preferred_element_type=F32)


def _bdot(a, b):
    return lax.dot_general(a, b, (((2,), (1,)), ((0,), (0,))), preferred_element_type=F32)


def _bdot_nt(a, b):
    return lax.dot_general(a, b, (((2,), (2,)), ((0,), (0,))), preferred_element_type=F32)


def _rms(x, g, eps):
    ms = jnp.mean(x * x, axis=-1, keepdims=True)
    return x * lax.rsqrt(ms + eps) * g


def _divisor_tile(n, pref):
    if n <= pref:
        return n
    t = (pref // LANES) * LANES
    while t > LANES and n % t:
        t -= LANES
    assert n % t == 0, (n, pref)
    return t


def _params(*sem):
    return pltpu.CompilerParams(dimension_semantics=sem, vmem_limit_bytes=VMEM_LIMIT_BYTES)


def _ada_kernel(c_ref, w_ref, b_ref, o_ref):
    c = c_ref[...]
    ca = (c * jax.nn.sigmoid(c)).astype(BF16)
    o_ref[0] = _dot(ca, w_ref[0].astype(BF16)) + b_ref[0]


def _ada_mod(c, ada_w, ada_b):
    depth, d, n = ada_w.shape
    b = c.shape[0]
    bp = -(-b // SUBLANES) * SUBLANES
    cp = jnp.pad(c, ((0, bp - b), (0, 0)))
    tn = _divisor_tile(n, 1024)
    out = pl.pallas_call(
        _ada_kernel,
        out_shape=jax.ShapeDtypeStruct((depth, bp, n), F32),
        grid=(depth, n // tn),
        in_specs=[
            pl.BlockSpec((bp, d), lambda l, j: (0, 0)),
            pl.BlockSpec((1, d, tn), lambda l, j: (l, 0, j)),
            pl.BlockSpec((1, 1, tn), lambda l, j: (l, 0, j)),
        ],
        out_specs=pl.BlockSpec((1, bp, tn), lambda l, j: (l, 0, j)),
        compiler_params=_params("arbitrary", "arbitrary"),
        name="ada_mod",
    )(cp, ada_w, ada_b.reshape(depth, 1, n))
    return out[:, :b]


def _adaln_rows(h_ref, g_ref, scale_ref, shift_ref, hn_ref, copy_ref=None):
    rows = h_ref.shape[1]
    g = g_ref[...]
    mul = 1.0 + scale_ref[0]
    add = shift_ref[0]

    def slab(i, carry):
        sl = pl.ds(pl.multiple_of(i * ADALN_SLAB_ROWS, ADALN_SLAB_ROWS), ADALN_SLAB_ROWS)
        h = h_ref[0, sl, :]
        hn_ref[sl, :] = (_rms(h, g, RMS_EPS) * mul + add).astype(BF16)
        if copy_ref is not None:
            copy_ref[0, sl, :] = h
        return carry

    lax.fori_loop(0, rows // ADALN_SLAB_ROWS, slab, 0, unroll=ADALN_UNROLL)


def _ffn_kernel(h_ref, g_ref, shift_ref, scale_ref, gate_ref, wg_ref, wu_ref, wo_ref, fg_ref,
                o_ref, hn_ref, *, final_norm):
    f = pl.program_id(2)

    @pl.when(f == 0)
    def _():
        _adaln_rows(h_ref, g_ref, scale_ref, shift_ref, hn_ref, o_ref)

    hn = hn_ref[...]
    g = _dot(hn, wg_ref[...])
    u = _dot(hn, wu_ref[...])
    act = (g * jax.nn.sigmoid(g) * u).astype(BF16)
    o_ref[0] += (0.5 * gate_ref[0]) * _dot(act, wo_ref[...])

    if final_norm:
        @pl.when(f == pl.num_programs(2) - 1)
        def _():
            o_ref[0] = _rms(o_ref[0], fg_ref[...], RMS_EPS)


def _ffn(h, g, shift, scale, gate, w_in, w_out, layer, which, final_g, *, final_norm):
    b, s, d = h.shape
    ff = w_out.shape[2]
    tm = _divisor_tile(s, FFN_TOKEN_TILE)
    tf = _divisor_tile(ff, FFN_HIDDEN_TILE)
    nf = ff // tf
    vec = pl.BlockSpec((1, 1, d), lambda bi, i, f: (bi, 0, 0))
    row = pl.BlockSpec((1, d), lambda bi, i, f: (0, 0))
    tile = pl.BlockSpec((1, tm, d), lambda bi, i, f: (bi, i, 0))
    return pl.pallas_call(
        functools.partial(_ffn_kernel, final_norm=final_norm),
        out_shape=jax.ShapeDtypeStruct((b, s, d), F32),
        grid=(b, s // tm, nf),
        in_specs=[
            tile, row, vec, vec, vec,
            pl.BlockSpec((None, None, d, tf), lambda bi, i, f: (layer, which, 0, f)),
            pl.BlockSpec((None, None, d, tf), lambda bi, i, f: (layer, which, 0, nf + f)),
            pl.BlockSpec((None, None, tf, d), lambda bi, i, f: (layer, which, f, 0)),
            row,
        ],
        out_specs=tile,
        scratch_shapes=[pltpu.VMEM((tm, d), BF16)],
        compiler_params=_params("parallel", "parallel", "arbitrary"),
        name="ffn",
    )(h, g, shift, scale, gate, w_in, w_in, w_out, final_g)


def _proj_kernel(h_ref, g_ref, shift_ref, scale_ref, w_ref, o_ref, hn_ref):
    @pl.when(pl.program_id(2) == 0)
    def _():
        _adaln_rows(h_ref, g_ref, scale_ref, shift_ref, hn_ref)

    o_ref[0] = _dot(hn_ref[...], w_ref[...])


def _proj(h, g, shift, scale, w, layer):
    b, s, d = h.shape
    p = w.shape[2]
    tm = _divisor_tile(s, 1024)
    tn = _divisor_tile(p, 1024)
    vec = pl.BlockSpec((1, 1, d), lambda bi, i, n: (bi, 0, 0))
    return pl.pallas_call(
        _proj_kernel,
        out_shape=jax.ShapeDtypeStruct((b, s, p), F32),
        grid=(b, s // tm, p // tn),
        in_specs=[
            pl.BlockSpec((1, tm, d), lambda bi, i, n: (bi, i, 0)),
            pl.BlockSpec((1, d), lambda bi, i, n: (0, 0)),
            vec, vec,
            pl.BlockSpec((None, d, tn), lambda bi, i, n: (layer, 0, n)),
        ],
        out_specs=pl.BlockSpec((1, tm, tn), lambda bi, i, n: (bi, i, n)),
        scratch_shapes=[pltpu.VMEM((tm, d), BF16)],
        compiler_params=_params("parallel", "parallel", "arbitrary"),
        name="proj",
    )(h, g, shift, scale, w)


def _outproj_kernel(oa_ref, or_ref, h_ref, gate_ref, wa_ref, wr_ref, o_ref):
    acc = _dot(oa_ref[0], wa_ref[...]) + _dot(or_ref[0], wr_ref[...])
    o_ref[0] = h_ref[0] + gate_ref[0] * acc


def _outproj(o_a, o_r, h, gate, w_out, layer):
    b, s, d = h.shape
    tm = _divisor_tile(s, 1024)
    tn = _divisor_tile(d, 1024)
    wa = o_a.shape[-1]
    wr = o_r.shape[-1]
    return pl.pallas_call(
        _outproj_kernel,
        out_shape=jax.ShapeDtypeStruct((b, s, d), F32),
        grid=(b, s // tm, d // tn),
        in_specs=[
            pl.BlockSpec((1, tm, wa), lambda bi, i, n: (bi, i, 0)),
            pl.BlockSpec((1, tm, wr), lambda bi, i, n: (bi, i, 0)),
            pl.BlockSpec((1, tm, tn), lambda bi, i, n: (bi, i, n)),
            pl.BlockSpec((1, 1, tn), lambda bi, i, n: (bi, 0, n)),
            pl.BlockSpec((None, wa, tn), lambda bi, i, n: (layer, 0, n)),
            pl.BlockSpec((None, wr, tn), lambda bi, i, n: (layer, wa // wr, n)),
        ],
        out_specs=pl.BlockSpec((1, tm, tn), lambda bi, i, n: (bi, i, n)),
        compiler_params=_params("parallel", "parallel", "arbitrary"),
        name="outproj",
    )(o_a, o_r, h, gate, w_out, w_out)


def _dsa_prep_kernel(q_ref, qi_ref, ckv_ref, kiwi_ref, g_ref, qt_ref, qit_ref, ckv_o, ckvt_o, ki_o, wit_o):
    qt_ref[0] = q_ref[0].T.astype(BF16)
    qit_ref[0] = qi_ref[0].T.astype(BF16)
    cn = _rms(ckv_ref[0], g_ref[...], RMS_EPS)
    ckv_o[0] = cn.astype(BF16)
    ckvt_o[0, 0] = jnp.concatenate([cn.T.astype(BF16), jnp.ones((SUM_ROWS, cn.shape[0]), BF16)], axis=0)
    kw = kiwi_ref[0]
    ki_o[0] = kw[:, :IDX_DIM].astype(BF16)
    wit_o[0] = kw.T[IDX_DIM:IDX_DIM + IDX_HEADS, :] * (IDX_HEADS * IDX_DIM) ** -0.5


def _dsa_prep(z, ckv_g):
    b, s, _ = z.shape
    t = DSA_TILE
    nt = s // t

    def seg(width, off):
        return pl.BlockSpec((1, t, width), lambda bi, i: (bi, i, off // width))

    return pl.pallas_call(
        _dsa_prep_kernel,
        out_shape=(
            jax.ShapeDtypeStruct((b, A_WIDTH, s), BF16),
            jax.ShapeDtypeStruct((b, IDX_HEADS * IDX_DIM, s), BF16),
            jax.ShapeDtypeStruct((b, s, KV_RANK), BF16),
            jax.ShapeDtypeStruct((b, nt, KV_RANK + SUM_ROWS, t), BF16),
            jax.ShapeDtypeStruct((b, s, IDX_DIM), BF16),
            jax.ShapeDtypeStruct((b, IDX_HEADS, s), F32),
        ),
        grid=(b, nt),
        in_specs=[
            seg(A_WIDTH, Z_Q), seg(IDX_HEADS * IDX_DIM, Z_QI), seg(KV_RANK, Z_CKV), seg(LORA_PAD, Z_KIWI),
            pl.BlockSpec((1, KV_RANK), lambda bi, i: (0, 0)),
        ],
        out_specs=(
            pl.BlockSpec((1, A_WIDTH, t), lambda bi, i: (bi, 0, i)),
            pl.BlockSpec((1, IDX_HEADS * IDX_DIM, t), lambda bi, i: (bi, 0, i)),
            pl.BlockSpec((1, t, KV_RANK), lambda bi, i: (bi, i, 0)),
            pl.BlockSpec((1, 1, KV_RANK + SUM_ROWS, t), lambda bi, i: (bi, i, 0, 0)),
            pl.BlockSpec((1, t, IDX_DIM), lambda bi, i: (bi, i, 0)),
            pl.BlockSpec((1, IDX_HEADS, t), lambda bi, i: (bi, 0, i)),
        ),
        compiler_params=_params("parallel", "parallel"),
        name="dsa_prep",
    )(z, z, z, z, ckv_g)


def _bias_kernel(t5_ref, o_ref):
    t = DSA_TILE
    j = lax.broadcasted_iota(I32, (t, t), 0)
    i = lax.broadcasted_iota(I32, (t, t), 1)
    for didx in range(2):
        n = jnp.maximum(didx * t + i - j, 0)
        nf = jnp.maximum(n, 1).astype(F32)
        large = REL_MAX_EXACT + (jnp.log(nf / REL_MAX_EXACT) / math.log(REL_MAX_DIST / REL_MAX_EXACT)
                                 * (REL_BUCKETS - REL_MAX_EXACT)).astype(I32)
        large = jnp.minimum(large, REL_BUCKETS - 1)
        bucket = jnp.where(n < REL_MAX_EXACT, n, large)
        for h in range(A_HEADS):
            val = jnp.zeros((t, t), F32)
            for k in range(REL_BUCKETS):
                val = jnp.where(bucket == k, t5_ref[k, h], val)
            o_ref[didx, :, h * t:(h + 1) * t] = (val - t5_ref[REL_BUCKETS - 1, h]) * LOG2E


def _bias_tiles(t5_bias):
    t = DSA_TILE
    assert t + 1 >= FAR_DISTANCE
    return pl.pallas_call(
        _bias_kernel,
        out_shape=jax.ShapeDtypeStruct((2, t, A_HEADS * t), F32),
        in_specs=[pl.BlockSpec(memory_space=pltpu.SMEM)],
        out_specs=pl.BlockSpec(memory_space=pltpu.VMEM),
        compiler_params=pltpu.CompilerParams(vmem_limit_bytes=VMEM_LIMIT_BYTES),
        name="t5_bias_tiles",
    )(t5_bias)


def _dsa_kernel(qt_ref, qit_ref, ckv_ref, ckvt_ref, ki_ref, wit_ref, wuk_ref, wuvt_ref, bias_ref,
                o_ref, keys_ref, khi_ref, klo_ref, acc_ref, ot_ref, *, topk, seq_len):
    t = DSA_TILE
    qb = pl.program_id(1)
    nk = qb + 1
    row = lax.broadcasted_iota(I32, (t, t), 0)
    col = lax.broadcasted_iota(I32, (t, t), 1)

    def idx_body(kc, carry):
        kic = ki_ref[0, pl.ds(pl.multiple_of(kc * t, t), t), :]
        acc = jnp.zeros((t, t), F32)
        for h in range(IDX_HEADS):
            rel = _dot(kic, qit_ref[0, h * IDX_DIM:(h + 1) * IDX_DIM, :])
            acc = acc + jnp.maximum(rel, 0.0) * wit_ref[0, h:h + 1, :]
        acc = acc + 0.0
        bits = pltpu.bitcast(acc, I32)
        key = bits ^ ((bits >> 31) & 0x7FFFFFFF)
        causal = (kc * t + row) <= (qb * t + col)
        key = jnp.where(causal, key, KEY_NEG_INF)
        sl = pl.ds(pl.multiple_of(kc * t, t), t)
        keys_ref[sl, :] = key
        khi_ref[sl, :] = (key >> 16).astype(I16)
        klo_ref[sl, :] = ((key & 0xFFFF) + I16_MIN).astype(I16)
        return carry

    lax.fori_loop(0, nk, idx_body, 0)

    def count16(ref, cand):
        cand16 = cand.astype(I16)

        def body(kc, accs):
            x = ref[pl.ds(pl.multiple_of(kc * t, t), t), :]
            ones = jnp.where(x >= cand16, jnp.asarray(1, BF16), jnp.asarray(0, BF16))
            ones = ones.reshape(t // PACKED_ROWS, PACKED_ROWS, t)
            accs = list(accs)
            for i in range(t // PACKED_ROWS):
                accs[i % COUNT_CHAINS] = accs[i % COUNT_CHAINS] + ones[i]
            return tuple(accs)

        accs = (jnp.zeros((PACKED_ROWS, t), BF16),) * COUNT_CHAINS
        accs = lax.fori_loop(0, nk // 2, lambda i, a: body(2 * i + 1, body(2 * i, a)), accs)
        accs = lax.fori_loop(2 * (nk // 2), nk, body, accs)
        total = sum(a.astype(F32) for a in accs)
        return jnp.sum(total, axis=0, keepdims=True).astype(I32)

    def search16(ref, base):
        def accept(cand, cur):
            return jnp.where(base + count16(ref, cand) >= topk, cand, cur)

        v = accept(jnp.zeros((1, t), I32), jnp.full((1, t), I16_MIN, I32))
        return lax.fori_loop(0, 15, lambda i, v: accept(v | (jnp.int32(1) << (14 - i)), v), v)

    hi = search16(khi_ref, 0)
    above = count16(khi_ref, jnp.minimum(hi + 1, -I16_MIN - 1))
    hi16 = hi.astype(I16)

    def keep_low(kc, carry):
        sl = pl.ds(pl.multiple_of(kc * t, t), t)
        klo_ref[sl, :] = jnp.where(khi_ref[sl, :] == hi16, klo_ref[sl, :], jnp.asarray(I16_MIN, I16))
        return carry

    lax.fori_loop(0, nk, keep_low, 0)
    lo = search16(klo_ref, above)
    thr = jnp.maximum((hi << 16) | (lo - I16_MIN), KEY_NEG_INF + 1)

    def count(pred):
        def body(kc, acc):
            k = keys_ref[pl.ds(pl.multiple_of(kc * t, t), t), :]
            m = jnp.where(pred(k, kc), 1, 0).astype(I32)
            return acc + jnp.sum(m.reshape(t // SUBLANES, SUBLANES, t), axis=0)

        acc = lax.fori_loop(0, nk, body, jnp.zeros((SUBLANES, t), I32))
        return jnp.sum(acc, axis=0, keepdims=True)

    @pl.when(jnp.max(count(lambda k, kc: k >= thr)) > topk)
    def _():
        need = topk - count(lambda k, kc: k > thr)
        nbits = (seq_len - 1).bit_length()

        def pos_bit(i, y):
            cand = y | (jnp.int32(1) << (nbits - 1 - i))
            before = count(lambda k, kc: (k == thr) & ((kc * t + row) < cand))
            return jnp.where(before < need, cand, y)

        last_kept = lax.fori_loop(0, nbits, pos_bit, jnp.zeros((1, t), I32))

        def demote(kc, carry):
            sl = pl.ds(pl.multiple_of(kc * t, t), t)
            k = keys_ref[sl, :]
            keys_ref[sl, :] = jnp.where((k == thr) & ((kc * t + row) > last_kept), k - 1, k)
            return carry

        lax.fori_loop(0, nk, demote, 0)

    scale = A_HEAD_DIM ** -0.5 * LOG2E

    hw = A_HEADS * t
    qlat = jnp.concatenate(
        [(_dot(wuk_ref[h], qt_ref[0, h * A_HEAD_DIM:(h + 1) * A_HEAD_DIM, :]) * scale).astype(BF16)
         for h in range(A_HEADS)], axis=1)
    acc_ref[...] = jnp.zeros_like(acc_ref)

    def att_body(kc, m, *, near):
        sl = pl.ds(pl.multiple_of(kc * t, t), t)
        raw = _dot(ckv_ref[0, sl, :], qlat)
        sel = keys_ref[sl, :] >= thr
        parts = []
        for h in range(A_HEADS):
            lg_h = raw[:, h * t:(h + 1) * t]
            if near:
                lg_h = lg_h + bias_ref[qb - kc, :, h * t:(h + 1) * t]
            parts.append(jnp.where(sel, lg_h, MASKED_LOGIT).astype(BF16))
        lg = jnp.concatenate(parts, axis=1)
        m_new = jnp.maximum(m, jnp.max(lg, axis=0, keepdims=True).astype(F32))
        p = jnp.exp2(lg - m_new.astype(BF16))
        acc_ref[...] = acc_ref[...] * jnp.exp2(m - m_new) + _dot(ckvt_ref[0, kc], p)
        return m_new

    n_far = jnp.maximum(qb - 1, 0)
    m = lax.fori_loop(0, n_far, functools.partial(att_body, near=False), jnp.full((1, hw), MASKED_LOGIT, F32))
    lax.fori_loop(n_far, nk, functools.partial(att_body, near=True), m)
    olat = (acc_ref[:KV_RANK, :] / acc_ref[KV_RANK:KV_RANK + 1, :]).astype(BF16)
    for h in range(A_HEADS):
        ot_ref[h * A_HEAD_DIM:(h + 1) * A_HEAD_DIM, :] = _dot(wuvt_ref[h], olat[:, h * t:(h + 1) * t])
    o_ref[0] = ot_ref[...].T.astype(BF16)


def _dsa(qt, qit, ckv, ckvt, ki, wit, wuk, wuvt, bias):
    b, _, s = qt.shape
    t = DSA_TILE
    nt = s // t
    topk = min(TOPK_MAX, s // 4)
    assert s // PACKED_ROWS <= BF16_EXACT_INT
    return pl.pallas_call(
        functools.partial(_dsa_kernel, topk=topk, seq_len=s),
        out_shape=jax.ShapeDtypeStruct((b, s, A_WIDTH), BF16),
        grid=(b, nt),
        in_specs=[
            pl.BlockSpec((1, A_WIDTH, t), lambda bi, i: (bi, 0, i)),
            pl.BlockSpec((1, IDX_HEADS * IDX_DIM, t), lambda bi, i: (bi, 0, i)),
            pl.BlockSpec((1, s, KV_RANK), lambda bi, i: (bi, 0, 0)),
            pl.BlockSpec((1, nt, KV_RANK + SUM_ROWS, t), lambda bi, i: (bi, 0, 0, 0)),
            pl.BlockSpec((1, s, IDX_DIM), lambda bi, i: (bi, 0, 0)),
            pl.BlockSpec((1, IDX_HEADS, t), lambda bi, i: (bi, 0, i)),
            pl.BlockSpec((A_HEADS, KV_RANK, A_HEAD_DIM), lambda bi, i: (0, 0, 0)),
            pl.BlockSpec((A_HEADS, A_HEAD_DIM, KV_RANK), lambda bi, i: (0, 0, 0)),
            pl.BlockSpec((2, t, A_HEADS * t), lambda bi, i: (0, 0, 0)),
        ],
        out_specs=pl.BlockSpec((1, t, A_WIDTH), lambda bi, i: (bi, i, 0)),
        scratch_shapes=[pltpu.VMEM((s, t), I32), pltpu.VMEM((s, t), I16), pltpu.VMEM((s, t), I16),
                        pltpu.VMEM((KV_RANK + SUM_ROWS, A_HEADS * t), F32), pltpu.VMEM((A_WIDTH, t), F32)],
        compiler_params=_params("parallel", "arbitrary"),
        name="dsa",
    )(qt, qit, ckv, ckvt, ki, wit, wuk, wuvt, bias)


def _split3(x):
    hi = x.astype(BF16)
    r1 = x - hi.astype(F32)
    mid = r1.astype(BF16)
    lo = (r1 - mid.astype(F32)).astype(BF16)
    return hi, mid, lo


def _rwkv_kernel(r_ref, k_ref, v_ref, gl_ref, wl_ref, al_ref,
                 mur_ref, muk_ref, muv_ref, mug_ref, muw_ref, mua_ref,
                 w0_ref, a0_ref, kk_ref, ka_ref, rk_ref, lnw_ref, lnb_ref,
                 w2_ref, a2_ref, g2_ref,
                 o_ref,
                 pr_ref, pk_ref, pv_ref, pg_ref, pw_ref, pa_ref, state_ref):
    tl = RWKV_TILE
    c = RWKV_CHUNK
    nc = tl // c
    npair = R_WIDTH // PAIR

    @pl.when(pl.program_id(1) == 0)
    def _():
        state_ref[...] = jnp.zeros_like(state_ref)
        for ref in (pr_ref, pk_ref, pv_ref, pg_ref, pw_ref, pa_ref):
            ref[...] = jnp.zeros_like(ref)

    def shift_mix(x_ref, prev_ref, mu_ref):
        x = x_ref[0]
        first = lax.broadcasted_iota(I32, x.shape, 0) == 0
        xprev = jnp.where(first, prev_ref[...], pltpu.roll(x, 1, axis=0))
        prev_ref[...] = x[tl - 1:tl, :]
        return x + (xprev - x) * mu_ref[...]

    r = shift_mix(r_ref, pr_ref, mur_ref)
    k = shift_mix(k_ref, pk_ref, muk_ref)
    v = shift_mix(v_ref, pv_ref, muv_ref)
    gl = shift_mix(gl_ref, pg_ref, mug_ref)
    wl = shift_mix(wl_ref, pw_ref, muw_ref)
    al = shift_mix(al_ref, pa_ref, mua_ref)

    w_lin = w0_ref[...] + _dot(jnp.tanh(wl).astype(BF16), w2_ref[...])
    nx = -w_lin
    softplus = jnp.maximum(nx, 0.0) + jnp.log(1.0 + jnp.exp(-jnp.abs(nx)))
    ld = -jnp.exp(-softplus - 0.5)
    a = jax.nn.sigmoid(a0_ref[...] + _dot(al.astype(BF16), a2_ref[...]))
    gate = _dot(jax.nn.sigmoid(gl).astype(BF16), g2_ref[...])

    li = lax.broadcasted_iota(I32, (LANES, LANES), 0) // R_HEAD_DIM
    lj = lax.broadcasted_iota(I32, (LANES, LANES), 1) // R_HEAD_DIM
    head_ones = jnp.where(li == lj, 1.0, 0.0).astype(BF16)

    def head_sum(x):
        parts = []
        for j in range(R_WIDTH // LANES):
            hi, mid, lo = _split3(x[:, j * LANES:(j + 1) * LANES])
            parts.append(_dot(hi, head_ones) + _dot(mid, head_ones) + _dot(lo, head_ones))
        return jnp.concatenate(parts, axis=-1)

    kk = k * kk_ref[...]
    kk = kk / jnp.maximum(jnp.sqrt(head_sum(kk * kk)), 1e-12)
    k2 = k * (1.0 + (a - 1.0) * ka_ref[...])
    bb = kk * a

    ti = lax.broadcasted_iota(I32, (tl, tl), 0)
    tj = lax.broadcasted_iota(I32, (tl, tl), 1)
    tri = jnp.where((ti // c == tj // c) & (tj <= ti), 1.0, 0.0).astype(BF16)
    hi, mid, lo = _split3(ld)
    cum = _dot(tri, hi) + _dot(tri, mid) + _dot(tri, lo)
    cum_end = jnp.broadcast_to(cum.reshape(nc, c, R_WIDTH)[:, c - 1:c, :], (nc, c, R_WIDTH)).reshape(tl, R_WIDTH)
    e_in = jnp.exp(cum)
    e_neg = jnp.exp(-cum)
    e_out = jnp.exp(cum_end - cum)

    nb = nc * npair

    def to_pairs(x):
        x3 = x.reshape(nc, c, R_WIDTH)
        xs = jnp.stack([x3[:, :, p * PAIR:(p + 1) * PAIR] for p in range(npair)], axis=1)
        return xs.reshape(nb, c, PAIR)

    lane_head = lax.broadcasted_iota(I32, (2 * c, PAIR), 1) // R_HEAD_DIM
    row_head = lax.broadcasted_iota(I32, (2 * c, PAIR), 0) // c
    same = lane_head == row_head
    rt = lax.broadcasted_iota(I32, (2 * c, PAIR), 0) % c
    ct = lax.broadcasted_iota(I32, (2 * c, PAIR), 1) % c
    strict = same & (ct < rt)
    incl = same & (ct <= rt)
    eye = jnp.where(same & (ct == rt), 1.0, 0.0).astype(F32)

    def bd(x):
        return jnp.where(same, jnp.concatenate([x, x], axis=1), 0.0)

    lane0 = lax.broadcasted_iota(I32, (c, PAIR), 1) < R_HEAD_DIM

    abar = bd(to_pairs(-kk * jnp.exp(cum - ld))).astype(BF16)
    rbar = bd(to_pairs(r * e_in)).astype(BF16)
    bt = bd(to_pairs(bb * e_neg)).astype(BF16)
    kt = bd(to_pairs(k2 * e_neg)).astype(BF16)
    bk = jnp.concatenate([bd(to_pairs(bb * e_out)), bd(to_pairs(k2 * e_out))], axis=1).astype(BF16)
    v_pl = to_pairs(v)
    vbd = bd(v_pl)
    decay_end = to_pairs(jnp.exp(cum_end))[:, 0:1, :]
    aa = _bdot_nt(jnp.concatenate([abar, rbar], axis=1), jnp.concatenate([bt, kt], axis=1))
    a_ab = jnp.where(strict, aa[:, :2 * c, :PAIR], 0.0)
    a_ak = jnp.where(strict, aa[:, :2 * c, PAIR:], 0.0).astype(BF16)
    a_rb = jnp.where(incl, aa[:, 2 * c:, :PAIR], 0.0).astype(BF16)
    a_rk = jnp.where(incl, aa[:, 2 * c:, PAIR:], 0.0).astype(BF16)
    pw = a_ab.astype(BF16)
    tinv = eye + a_ab
    for _ in range(int(math.log2(c)) - 1):
        pw = _bdot(pw, pw).astype(BF16)
        tinv = tinv + _bdot(tinv.astype(BF16), pw)
    akv = _bdot(a_ak, vbd.astype(BF16))
    wu = _bdot(tinv.astype(BF16), jnp.concatenate([abar, akv.astype(BF16)], axis=2))
    w16 = wu[:, :, :PAIR].astype(BF16)
    u_bd = wu[:, :, PAIR:]

    s_bd = state_ref[...]
    e_parts, rs_parts = [], []
    for ci in range(nc):
        sl = slice(ci * npair, (ci + 1) * npair)
        wr = _bdot_nt(jnp.concatenate([w16[sl], rbar[sl]], axis=1), s_bd.astype(BF16))
        e_bd = wr[:, :2 * c] + u_bd[sl]
        e_parts.append(e_bd)
        rs_parts.append(wr[:, 2 * c:])
        ev = jnp.concatenate([e_bd, vbd[sl]], axis=1)
        ds = _bdot(jnp.swapaxes(ev, 1, 2).astype(BF16), bk[sl])
        s_bd = s_bd * decay_end[sl] + jnp.where(same, ds, 0.0)
    state_ref[...] = s_bd

    ev_all = jnp.concatenate([jnp.concatenate(e_parts, axis=0), vbd], axis=1).astype(BF16)
    y_bd = jnp.concatenate(rs_parts, axis=0) + _bdot(jnp.concatenate([a_rb, a_rk], axis=2), ev_all)
    mean = jnp.sum(y_bd, axis=-1, keepdims=True) * (1.0 / R_HEAD_DIM)
    dev = jnp.where(same, y_bd - mean, 0.0)
    var = jnp.sum(dev * dev, axis=-1, keepdims=True) * (1.0 / R_HEAD_DIM)
    yn = dev * lax.rsqrt(var + GN_EPS)
    yn = yn[:, :c] + yn[:, c:]
    rkv = to_pairs(r * k2 * rk_ref[...])
    s0 = jnp.sum(jnp.where(lane0, rkv, 0.0), axis=-1, keepdims=True)
    s1 = jnp.sum(jnp.where(lane0, 0.0, rkv), axis=-1, keepdims=True)
    bonus = jnp.where(lane0, s0, s1) * v_pl
    def per_pair(row_ref):
        rows = [row_ref[:, p * PAIR:(p + 1) * PAIR] for p in range(npair)]
        return jnp.stack(rows * nc, axis=0)

    out = ((yn * per_pair(lnw_ref) + per_pair(lnb_ref) + bonus) * to_pairs(gate)).astype(BF16)
    for ci in range(nc):
        for p in range(npair):
            o_ref[0, ci * c:(ci + 1) * c, p * PAIR:(p + 1) * PAIR] = out[ci * npair + p]


def _rwkv(z, prm):
    b, s, _ = z.shape
    tl = RWKV_TILE

    def seg(width, off):
        return pl.BlockSpec((1, tl, width), lambda bi, i: (bi, i, off // width))

    def full(arr):
        return pl.BlockSpec(arr.shape, lambda bi, i: (0,) * arr.ndim)

    names = ("mu_r", "mu_k", "mu_v", "mu_g", "mu_w", "mu_a", "w0", "a0", "k_k", "k_a", "r_k", "ln_w", "ln_b",
             "w2", "a2", "g2")
    consts = [prm[n] for n in names]
    return pl.pallas_call(
        _rwkv_kernel,
        out_shape=jax.ShapeDtypeStruct((b, s, R_WIDTH), BF16),
        grid=(b, s // tl),
        in_specs=[seg(R_WIDTH, Z_R), seg(R_WIDTH, Z_K), seg(R_WIDTH, Z_V), seg(GATE_LORA, Z_GL),
                  seg(LORA_PAD, Z_WL), seg(LORA_PAD, Z_AL)] + [full(x) for x in consts],
        out_specs=pl.BlockSpec((1, tl, R_WIDTH), lambda bi, i: (bi, i, 0)),
        scratch_shapes=[
            pltpu.VMEM((1, R_WIDTH), F32), pltpu.VMEM((1, R_WIDTH), F32), pltpu.VMEM((1, R_WIDTH), F32),
            pltpu.VMEM((1, GATE_LORA), F32), pltpu.VMEM((1, LORA_PAD), F32), pltpu.VMEM((1, LORA_PAD), F32),
            pltpu.VMEM((R_WIDTH // PAIR, PAIR, PAIR), F32),
        ],
        compiler_params=_params("parallel", "arbitrary"),
        name="rwkv7",
    )(z, z, z, z, z, z, *consts)


def _pad_cols(w, width):
    return jnp.pad(w, ((0, 0), (0, width - w.shape[1])))


def _w_in_layout_kernel(wt_ref, o_ref):
    wt = wt_ref[0]
    cols = wt.shape[1]
    offs = [0]
    for n in (A_WIDTH, KV_RANK, IDX_HEADS * IDX_DIM, IDX_DIM + IDX_HEADS,
              3 * R_WIDTH, DECAY_LORA, AAA_LORA, GATE_LORA):
        offs.append(offs[-1] + n)
    q, ckv, qi, kiwi, rkv, wl, al, gl = (wt[offs[i]:offs[i + 1], :] for i in range(8))

    def padded(x, height):
        return jnp.concatenate([x, jnp.zeros((height - x.shape[0], cols), x.dtype)], axis=0)

    rows = [q, qi, rkv, gl, ckv, padded(kiwi, LORA_PAD), padded(wl, LORA_PAD), padded(al, LORA_PAD),
            jnp.zeros((Z_WIDTH - Z_END, cols), wt.dtype)]
    o_ref[0] = jnp.concatenate(rows, axis=0).T.astype(BF16)


def _layout_w_in(w_in):
    depth, d, p = w_in.shape
    tc = _divisor_tile(d, 256)
    return pl.pallas_call(
        _w_in_layout_kernel,
        out_shape=jax.ShapeDtypeStruct((depth, d, Z_WIDTH), BF16),
        grid=(depth, d // tc),
        in_specs=[pl.BlockSpec((1, p, tc), lambda l, i: (l, 0, i))],
        out_specs=pl.BlockSpec((1, tc, Z_WIDTH), lambda l, i: (l, i, 0)),
        compiler_params=_params("parallel", "parallel"),
        name="w_in_layout",
    )(jnp.swapaxes(w_in, 1, 2))


def _pad_rows(w, rows):
    return jnp.pad(w, ((0, rows - w.shape[0]), (0, 0)))


def _rwkv_params(l, mu, w0, w2, a0, a2, g2, k_k, k_a, r_k, ln_w, ln_b):
    m = mu[l]
    o = [0, R_WIDTH, 2 * R_WIDTH, 3 * R_WIDTH, 3 * R_WIDTH + DECAY_LORA, 3 * R_WIDTH + DECAY_LORA + AAA_LORA]
    row = lambda x: x.reshape(1, -1)
    return {
        "mu_r": row(m[o[0]:o[1]]), "mu_k": row(m[o[1]:o[2]]), "mu_v": row(m[o[2]:o[3]]),
        "mu_w": _pad_cols(row(m[o[3]:o[4]]), LORA_PAD), "mu_a": _pad_cols(row(m[o[4]:o[5]]), LORA_PAD),
        "mu_g": row(m[o[5]:]),
        "w0": row(w0[l]), "a0": row(a0[l]), "k_k": row(k_k[l]), "k_a": row(k_a[l]), "r_k": row(r_k[l]),
        "ln_w": row(ln_w[l]), "ln_b": row(ln_b[l]),
        "w2": _pad_rows(w2[l], LORA_PAD).astype(BF16), "a2": _pad_rows(a2[l], LORA_PAD).astype(BF16),
        "g2": g2[l].astype(BF16),
    }


def kernel(x, c, t5_bias, ada_w, ada_b, norm_g, ffn_w_in, ffn_w_out, w_in, ckv_norm_g, w_uk, w_uv, rwkv_mu, rwkv_w0, rwkv_w2, rwkv_a0, rwkv_a2, rwkv_g2, rwkv_k_k, rwkv_k_a, rwkv_r_k, rwkv_ln_w, rwkv_ln_b, w_out, final_norm_g):
    b, s, d = x.shape
    depth = ada_w.shape[0]
    assert s % DSA_TILE == 0 and s % RWKV_TILE == 0 and d == A_WIDTH + R_WIDTH
    mod = _ada_mod(c, ada_w, ada_b).reshape(depth, b, N_SUB, 3, 1, d)
    bias = _bias_tiles(t5_bias)
    final_g = final_norm_g.reshape(1, d)
    w_in_z = _layout_w_in(w_in)
    ffn_in16 = ffn_w_in.astype(BF16)
    ffn_out16 = ffn_w_out.astype(BF16)
    w_out16 = w_out.astype(BF16)
    h = x
    for l in range(depth):
        shift = lambda i: mod[l, :, i, 0]
        scale = lambda i: mod[l, :, i, 1]
        gate = lambda i: mod[l, :, i, 2]
        g = lambda i: norm_g[l, i].reshape(1, d)
        h = _ffn(h, g(0), shift(0), scale(0), gate(0), ffn_in16, ffn_out16, l, 0, final_g, final_norm=False)
        z = _proj(h, g(1), shift(1), scale(1), w_in_z, l)
        qt, qit, ckv, ckvt, ki, wit = _dsa_prep(z, ckv_norm_g[l].reshape(1, KV_RANK))
        wuk = jnp.transpose(w_uk[l], (1, 0, 2)).astype(BF16)
        wuvt = jnp.transpose(w_uv[l], (1, 2, 0)).astype(BF16)
        o_a = _dsa(qt, qit, ckv, ckvt, ki, wit, wuk, wuvt, bias)
        o_r = _rwkv(z, _rwkv_params(l, rwkv_mu, rwkv_w0, rwkv_w2, rwkv_a0, rwkv_a2, rwkv_g2, rwkv_k_k,
                                    rwkv_k_a, rwkv_r_k, rwkv_ln_w, rwkv_ln_b))
        h = _outproj(o_a, o_r, h, gate(1), w_out16, l)
        h = _ffn(h, g(2), shift(2), scale(2), gate(2), ffn_in16, ffn_out16, l, 1, final_g,
                 final_norm=(l == depth - 1))
    return h
```

```python
import functools
import math

import jax
import jax.numpy as jnp
from jax import lax
from jax.experimental import pallas as pl
from jax.experimental.pallas import tpu as pltpu

F32 = jnp.float32
BF16 = jnp.bfloat16
I32 = jnp.int32
I16 = jnp.int16

A_HEADS = 8
A_HEAD_DIM = 128
A_WIDTH = A_HEADS * A_HEAD_DIM
KV_RANK = 256
IDX_HEADS = 16
IDX_DIM = 64
TOPK_MAX = 256
REL_BUCKETS = 32
REL_MAX_EXACT = REL_BUCKETS // 2
REL_MAX_DIST = 128
R_HEAD_DIM = 64
R_WIDTH = 1024
R_HEADS = R_WIDTH // R_HEAD_DIM
DECAY_LORA = 96
AAA_LORA = 96
GATE_LORA = 256
GN_EPS = 64e-5
RMS_EPS = 1e-6
N_SUB = 3

LANES = 128
SUBLANES = 8
VMEM_LIMIT_BYTES = 56 * 1024 * 1024

LORA_PAD = 128
Z_Q = 0
Z_QI = Z_Q + A_WIDTH
Z_R = Z_QI + IDX_HEADS * IDX_DIM
Z_K = Z_R + R_WIDTH
Z_V = Z_K + R_WIDTH
Z_GL = Z_V + R_WIDTH
Z_CKV = Z_GL + GATE_LORA
Z_KIWI = Z_CKV + KV_RANK
Z_WL = Z_KIWI + LORA_PAD
Z_AL = Z_WL + LORA_PAD
Z_END = Z_AL + LORA_PAD
Z_WIDTH = 6144

FFN_TOKEN_TILE = 1024
FFN_HIDDEN_TILE = 256
ADALN_SLAB_ROWS = 16
ADALN_UNROLL = 8
DSA_TILE = 256
RWKV_CHUNK = 64
RWKV_TILE = 256
PAIR = 2 * R_HEAD_DIM

I16_MIN = -(2 ** 15)
PACKED_ROWS = 2 * SUBLANES
SUM_ROWS = PACKED_ROWS
BF16_EXACT_INT = 256
COUNT_CHAINS = 4
KEY_NEG_INF = -2139095041
MASKED_LOGIT = -(2.0 ** 100)
LOG2E = math.log2(math.e)
FAR_DISTANCE = math.ceil(REL_MAX_EXACT * (REL_MAX_DIST / REL_MAX_EXACT)
                         ** ((REL_BUCKETS - 1 - REL_MAX_EXACT) / (REL_BUCKETS - REL_MAX_EXACT)))


def _dot(a, b):
    return jnp.dot(a, b, preferred_element_type=F32)


def _dot_nt(a, b):
    return lax.dot_general(a, b, (((1,), (1,)), ((), ())), preferred_element_type=F32)


def _bdot(a, b):
    return lax.dot_general(a, b, (((2,), (1,)), ((0,), (0,))), preferred_element_type=F32)


def _bdot_nt(a, b):
    return lax.dot_general(a, b, (((2,), (2,)), ((0,), (0,))), preferred_element_type=F32)


def _rms(x, g, eps):
    ms = jnp.mean(x * x, axis=-1, keepdims=True)
    return x * lax.rsqrt(ms + eps) * g


def _divisor_tile(n, pref):
    if n <= pref:
        return n
    t = (pref // LANES) * LANES
    while t > LANES and n % t:
        t -= LANES
    assert n % t == 0, (n, pref)
    return t


def _params(*sem):
    return pltpu.CompilerParams(dimension_semantics=sem, vmem_limit_bytes=VMEM_LIMIT_BYTES)


def _ada_kernel(c_ref, w_ref, b_ref, o_ref):
    c = c_ref[...]
    ca = (c * jax.nn.sigmoid(c)).astype(BF16)
    o_ref[0] = _dot(ca, w_ref[0].astype(BF16)) + b_ref[0]


def _ada_mod(c, ada_w, ada_b):
    depth, d, n = ada_w.shape
    b = c.shape[0]
    bp = -(-b // SUBLANES) * SUBLANES
    cp = jnp.pad(c, ((0, bp - b), (0, 0)))
    tn = _divisor_tile(n, 1024)
    out = pl.pallas_call(
        _ada_kernel,
        out_shape=jax.ShapeDtypeStruct((depth, bp, n), F32),
        grid=(depth, n // tn),
        in_specs=[
            pl.BlockSpec((bp, d), lambda l, j: (0, 0)),
            pl.BlockSpec((1, d, tn), lambda l, j: (l, 0, j)),
            pl.BlockSpec((1, 1, tn), lambda l, j: (l, 0, j)),
        ],
        out_specs=pl.BlockSpec((1, bp, tn), lambda l, j: (l, 0, j)),
        compiler_params=_params("arbitrary", "arbitrary"),
        name="ada_mod",
    )(cp, ada_w, ada_b.reshape(depth, 1, n))
    return out[:, :b]


def _adaln_rows(h_ref, g_ref, scale_ref, shift_ref, hn_ref, copy_ref=None):
    rows = h_ref.shape[1]
    g = g_ref[...]
    mul = 1.0 + scale_ref[0]
    add = shift_ref[0]

    def slab(i, carry):
        sl = pl.ds(pl.multiple_of(i * ADALN_SLAB_ROWS, ADALN_SLAB_ROWS), ADALN_SLAB_ROWS)
        h = h_ref[0, sl, :]
        hn_ref[sl, :] = (_rms(h, g, RMS_EPS) * mul + add).astype(BF16)
        if copy_ref is not None:
            copy_ref[0, sl, :] = h
        return carry

    lax.fori_loop(0, rows // ADALN_SLAB_ROWS, slab, 0, unroll=ADALN_UNROLL)


def _ffn_kernel(h_ref, g_ref, shift_ref, scale_ref, gate_ref, wg_ref, wu_ref, wo_ref, fg_ref,
                o_ref, hn_ref, *, final_norm):
    f = pl.program_id(2)

    @pl.when(f == 0)
    def _():
        _adaln_rows(h_ref, g_ref, scale_ref, shift_ref, hn_ref, o_ref)

    hn = hn_ref[...]
    g = _dot(hn, wg_ref[...].astype(BF16))
    u = _dot(hn, wu_ref[...].astype(BF16))
    act = (g * jax.nn.sigmoid(g) * u).astype(BF16)
    o_ref[0] += (0.5 * gate_ref[0]) * _dot(act, wo_ref[...].astype(BF16))

    if final_norm:
        @pl.when(f == pl.num_programs(2) - 1)
        def _():
            o_ref[0] = _rms(o_ref[0], fg_ref[...], RMS_EPS)


def _ffn(h, g, shift, scale, gate, w_in, w_out, layer, which, final_g, *, final_norm):
    b, s, d = h.shape
    ff = w_out.shape[2]
    tm = _divisor_tile(s, FFN_TOKEN_TILE)
    tf = _divisor_tile(ff, FFN_HIDDEN_TILE)
    nf = ff // tf
    vec = pl.BlockSpec((1, 1, d), lambda bi, i, f: (bi, 0, 0))
    row = pl.BlockSpec((1, d), lambda bi, i, f: (0, 0))
    tile = pl.BlockSpec((1, tm, d), lambda bi, i, f: (bi, i, 0))
    return pl.pallas_call(
        functools.partial(_ffn_kernel, final_norm=final_norm),
        out_shape=jax.ShapeDtypeStruct((b, s, d), F32),
        grid=(b, s // tm, nf),
        in_specs=[
            tile, row, vec, vec, vec,
            pl.BlockSpec((None, None, d, tf), lambda bi, i, f: (layer, which, 0, f)),
            pl.BlockSpec((None, None, d, tf), lambda bi, i, f: (layer, which, 0, nf + f)),
            pl.BlockSpec((None, None, tf, d), lambda bi, i, f: (layer, which, f, 0)),
            row,
        ],
        out_specs=tile,
        scratch_shapes=[pltpu.VMEM((tm, d), BF16)],
        compiler_params=_params("parallel", "parallel", "arbitrary"),
        name="ffn",
    )(h, g, shift, scale, gate, w_in, w_in, w_out, final_g)


def _proj_kernel(h_ref, g_ref, shift_ref, scale_ref, w_ref, o_ref, hn_ref):
    @pl.when(pl.program_id(2) == 0)
    def _():
        _adaln_rows(h_ref, g_ref, scale_ref, shift_ref, hn_ref)

    o_ref[0] = _dot(hn_ref[...], w_ref[...])


def _proj(h, g, shift, scale, w, layer):
    b, s, d = h.shape
    p = w.shape[2]
    tm = _divisor_tile(s, 1024)
    tn = _divisor_tile(p, 1024)
    vec = pl.BlockSpec((1, 1, d), lambda bi, i, n: (bi, 0, 0))
    return pl.pallas_call(
        _proj_kernel,
        out_shape=jax.ShapeDtypeStruct((b, s, p), F32),
        grid=(b, s // tm, p // tn),
        in_specs=[
            pl.BlockSpec((1, tm, d), lambda bi, i, n: (bi, i, 0)),
            pl.BlockSpec((1, d), lambda bi, i, n: (0, 0)),
            vec, vec,
            pl.BlockSpec((None, d, tn), lambda bi, i, n: (layer, 0, n)),
        ],
        out_specs=pl.BlockSpec((1, tm, tn), lambda bi, i, n: (bi, i, n)),
        scratch_shapes=[pltpu.VMEM((tm, d), BF16)],
        compiler_params=_params("parallel", "parallel", "arbitrary"),
        name="proj",
    )(h, g, shift, scale, w)


def _outproj_kernel(oa_ref, or_ref, h_ref, gate_ref, wa_ref, wr_ref, o_ref):
    acc = _dot(oa_ref[0], wa_ref[...]) + _dot(or_ref[0], wr_ref[...])
    o_ref[0] = h_ref[0] + gate_ref[0] * acc


def _outproj(o_a, o_r, h, gate, w_out, layer):
    b, s, d = h.shape
    tm = _divisor_tile(s, 1024)
    tn = _divisor_tile(d, 1024)
    wa = o_a.shape[-1]
    wr = o_r.shape[-1]
    return pl.pallas_call(
        _outproj_kernel,
        out_shape=jax.ShapeDtypeStruct((b, s, d), F32),
        grid=(b, s // tm, d // tn),
        in_specs=[
            pl.BlockSpec((1, tm, wa), lambda bi, i, n: (bi, i, 0)),
            pl.BlockSpec((1, tm, wr), lambda bi, i, n: (bi, i, 0)),
            pl.BlockSpec((1, tm, tn), lambda bi, i, n: (bi, i, n)),
            pl.BlockSpec((1, 1, tn), lambda bi, i, n: (bi, 0, n)),
            pl.BlockSpec((None, wa, tn), lambda bi, i, n: (layer, 0, n)),
            pl.BlockSpec((None, wr, tn), lambda bi, i, n: (layer, wa // wr, n)),
        ],
        out_specs=pl.BlockSpec((1, tm, tn), lambda bi, i, n: (bi, i, n)),
        compiler_params=_params("parallel", "parallel", "arbitrary"),
        name="outproj",
    )(o_a, o_r, h, gate, w_out, w_out)


def _dsa_prep_kernel(q_ref, qi_ref, ckv_ref, kiwi_ref, g_ref, qt_ref, qit_ref, ckv_o, ckvt_o, ki_o, wit_o):
    qt_ref[0] = q_ref[0].T.astype(BF16)
    qit_ref[0] = qi_ref[0].T.astype(BF16)
    cn = _rms(ckv_ref[0], g_ref[...], RMS_EPS)
    ckv_o[0] = cn.astype(BF16)
    ckvt_o[0, 0] = jnp.concatenate([cn.T.astype(BF16), jnp.ones((SUM_ROWS, cn.shape[0]), BF16)], axis=0)
    kw = kiwi_ref[0]
    ki_o[0] = kw[:, :IDX_DIM].astype(BF16)
    wit_o[0] = kw.T[IDX_DIM:IDX_DIM + IDX_HEADS, :] * (IDX_HEADS * IDX_DIM) ** -0.5


def _dsa_prep(z, ckv_g):
    b, s, _ = z.shape
    t = DSA_TILE
    nt = s // t

    def seg(width, off):
        return pl.BlockSpec((1, t, width), lambda bi, i: (bi, i, off // width))

    return pl.pallas_call(
        _dsa_prep_kernel,
        out_shape=(
            jax.ShapeDtypeStruct((b, A_WIDTH, s), BF16),
            jax.ShapeDtypeStruct((b, IDX_HEADS * IDX_DIM, s), BF16),
            jax.ShapeDtypeStruct((b, s, KV_RANK), BF16),
            jax.ShapeDtypeStruct((b, nt, KV_RANK + SUM_ROWS, t), BF16),
            jax.ShapeDtypeStruct((b, s, IDX_DIM), BF16),
            jax.ShapeDtypeStruct((b, IDX_HEADS, s), F32),
        ),
        grid=(b, nt),
        in_specs=[
            seg(A_WIDTH, Z_Q), seg(IDX_HEADS * IDX_DIM, Z_QI), seg(KV_RANK, Z_CKV), seg(LORA_PAD, Z_KIWI),
            pl.BlockSpec((1, KV_RANK), lambda bi, i: (0, 0)),
        ],
        out_specs=(
            pl.BlockSpec((1, A_WIDTH, t), lambda bi, i: (bi, 0, i)),
            pl.BlockSpec((1, IDX_HEADS * IDX_DIM, t), lambda bi, i: (bi, 0, i)),
            pl.BlockSpec((1, t, KV_RANK), lambda bi, i: (bi, i, 0)),
            pl.BlockSpec((1, 1, KV_RANK + SUM_ROWS, t), lambda bi, i: (bi, i, 0, 0)),
            pl.BlockSpec((1, t, IDX_DIM), lambda bi, i: (bi, i, 0)),
            pl.BlockSpec((1, IDX_HEADS, t), lambda bi, i: (bi, 0, i)),
        ),
        compiler_params=_params("parallel", "parallel"),
        name="dsa_prep",
    )(z, z, z, z, ckv_g)


def _bias_kernel(t5_ref, o_ref):
    t = DSA_TILE
    j = lax.broadcasted_iota(I32, (t, t), 0)
    i = lax.broadcasted_iota(I32, (t, t), 1)
    for didx in range(2):
        n = jnp.maximum(didx * t + i - j, 0)
        nf = jnp.maximum(n, 1).astype(F32)
        large = REL_MAX_EXACT + (jnp.log(nf / REL_MAX_EXACT) / math.log(REL_MAX_DIST / REL_MAX_EXACT)
                                 * (REL_BUCKETS - REL_MAX_EXACT)).astype(I32)
        large = jnp.minimum(large, REL_BUCKETS - 1)
        bucket = jnp.where(n < REL_MAX_EXACT, n, large)
        for h in range(A_HEADS):
            val = jnp.zeros((t, t), F32)
            for k in range(REL_BUCKETS):
                val = jnp.where(bucket == k, t5_ref[k, h], val)
            o_ref[didx, :, h * t:(h + 1) * t] = (val - t5_ref[REL_BUCKETS - 1, h]) * LOG2E


def _bias_tiles(t5_bias):
    t = DSA_TILE
    assert t + 1 >= FAR_DISTANCE
    return pl.pallas_call(
        _bias_kernel,
        out_shape=jax.ShapeDtypeStruct((2, t, A_HEADS * t), F32),
        in_specs=[pl.BlockSpec(memory_space=pltpu.SMEM)],
        out_specs=pl.BlockSpec(memory_space=pltpu.VMEM),
        compiler_params=pltpu.CompilerParams(vmem_limit_bytes=VMEM_LIMIT_BYTES),
        name="t5_bias_tiles",
    )(t5_bias)


def _dsa_kernel(qt_ref, qit_ref, ckv_ref, ckvt_ref, ki_ref, wit_ref, wuk_ref, wuvt_ref, bias_ref,
                o_ref, keys_ref, khi_ref, klo_ref, acc_ref, ot_ref, *, topk, seq_len):
    t = DSA_TILE
    qb = pl.program_id(1)
    nk = qb + 1
    row = lax.broadcasted_iota(I32, (t, t), 0)
    col = lax.broadcasted_iota(I32, (t, t), 1)

    def idx_body(kc, carry):
        kic = ki_ref[0, pl.ds(pl.multiple_of(kc * t, t), t), :]
        acc = jnp.zeros((t, t), F32)
        for h in range(IDX_HEADS):
            rel = _dot(kic, qit_ref[0, h * IDX_DIM:(h + 1) * IDX_DIM, :])
            acc = acc + jnp.maximum(rel, 0.0) * wit_ref[0, h:h + 1, :]
        acc = acc + 0.0
        bits = pltpu.bitcast(acc, I32)
        key = bits ^ ((bits >> 31) & 0x7FFFFFFF)
        causal = (kc * t + row) <= (qb * t + col)
        key = jnp.where(causal, key, KEY_NEG_INF)
        sl = pl.ds(pl.multiple_of(kc * t, t), t)
        keys_ref[sl, :] = key
        khi_ref[sl, :] = (key >> 16).astype(I16)
        klo_ref[sl, :] = ((key & 0xFFFF) + I16_MIN).astype(I16)
        return carry

    lax.fori_loop(0, nk, idx_body, 0)

    def count16(ref, cand):
        cand16 = cand.astype(I16)

        def body(kc, accs):
            x = ref[pl.ds(pl.multiple_of(kc * t, t), t), :]
            ones = jnp.where(x >= cand16, jnp.asarray(1, BF16), jnp.asarray(0, BF16))
            ones = ones.reshape(t // PACKED_ROWS, PACKED_ROWS, t)
            accs = list(accs)
            for i in range(t // PACKED_ROWS):
                accs[i % COUNT_CHAINS] = accs[i % COUNT_CHAINS] + ones[i]
            return tuple(accs)

        accs = (jnp.zeros((PACKED_ROWS, t), BF16),) * COUNT_CHAINS
        accs = lax.fori_loop(0, nk // 2, lambda i, a: body(2 * i + 1, body(2 * i, a)), accs)
        accs = lax.fori_loop(2 * (nk // 2), nk, body, accs)
        total = sum(a.astype(F32) for a in accs)
        return jnp.sum(total, axis=0, keepdims=True).astype(I32)

    def search16(ref, base):
        def accept(cand, cur):
            return jnp.where(base + count16(ref, cand) >= topk, cand, cur)

        v = accept(jnp.zeros((1, t), I32), jnp.full((1, t), I16_MIN, I32))
        return lax.fori_loop(0, 15, lambda i, v: accept(v | (jnp.int32(1) << (14 - i)), v), v)

    hi = search16(khi_ref, 0)
    above = count16(khi_ref, jnp.minimum(hi + 1, -I16_MIN - 1))
    hi16 = hi.astype(I16)

    def keep_low(kc, carry):
        sl = pl.ds(pl.multiple_of(kc * t, t), t)
        klo_ref[sl, :] = jnp.where(khi_ref[sl, :] == hi16, klo_ref[sl, :], jnp.asarray(I16_MIN, I16))
        return carry

    lax.fori_loop(0, nk, keep_low, 0)
    lo = search16(klo_ref, above)
    thr = jnp.maximum((hi << 16) | (lo - I16_MIN), KEY_NEG_INF + 1)

    def count(pred):
        def body(kc, acc):
            k = keys_ref[pl.ds(pl.multiple_of(kc * t, t), t), :]
            m = jnp.where(pred(k, kc), 1, 0).astype(I32)
            return acc + jnp.sum(m.reshape(t // SUBLANES, SUBLANES, t), axis=0)

        acc = lax.fori_loop(0, nk, body, jnp.zeros((SUBLANES, t), I32))
        return jnp.sum(acc, axis=0, keepdims=True)

    @pl.when(jnp.max(count(lambda k, kc: k >= thr)) > topk)
    def _():
        need = topk - count(lambda k, kc: k > thr)
        nbits = (seq_len - 1).bit_length()

        def pos_bit(i, y):
            cand = y | (jnp.int32(1) << (nbits - 1 - i))
            before = count(lambda k, kc: (k == thr) & ((kc * t + row) < cand))
            return jnp.where(before < need, cand, y)

        last_kept = lax.fori_loop(0, nbits, pos_bit, jnp.zeros((1, t), I32))

        def demote(kc, carry):
            sl = pl.ds(pl.multiple_of(kc * t, t), t)
            k = keys_ref[sl, :]
            keys_ref[sl, :] = jnp.where((k == thr) & ((kc * t + row) > last_kept), k - 1, k)
            return carry

        lax.fori_loop(0, nk, demote, 0)

    scale = A_HEAD_DIM ** -0.5 * LOG2E

    hw = A_HEADS * t
    qlat = jnp.concatenate(
        [(_dot(wuk_ref[h], qt_ref[0, h * A_HEAD_DIM:(h + 1) * A_HEAD_DIM, :]) * scale).astype(BF16)
         for h in range(A_HEADS)], axis=1)
    acc_ref[...] = jnp.zeros_like(acc_ref)

    def att_body(kc, m, *, near):
        sl = pl.ds(pl.multiple_of(kc * t, t), t)
        raw = _dot(ckv_ref[0, sl, :], qlat)
        sel = keys_ref[sl, :] >= thr
        parts = []
        for h in range(A_HEADS):
            lg_h = raw[:, h * t:(h + 1) * t]
            if near:
                lg_h = lg_h + bias_ref[qb - kc, :, h * t:(h + 1) * t]
            parts.append(jnp.where(sel, lg_h, MASKED_LOGIT).astype(BF16))
        lg = jnp.concatenate(parts, axis=1)
        m_new = jnp.maximum(m, jnp.max(lg, axis=0, keepdims=True).astype(F32))
        p = jnp.exp2(lg - m_new.astype(BF16))
        acc_ref[...] = acc_ref[...] * jnp.exp2(m - m_new) + _dot(ckvt_ref[0, kc], p)
        return m_new

    n_far = jnp.maximum(qb - 1, 0)
    m = lax.fori_loop(0, n_far, functools.partial(att_body, near=False), jnp.full((1, hw), MASKED_LOGIT, F32))
    lax.fori_loop(n_far, nk, functools.partial(att_body, near=True), m)
    olat = (acc_ref[:KV_RANK, :] / acc_ref[KV_RANK:KV_RANK + 1, :]).astype(BF16)
    for h in range(A_HEADS):
        ot_ref[h * A_HEAD_DIM:(h + 1) * A_HEAD_DIM, :] = _dot(wuvt_ref[h], olat[:, h * t:(h + 1) * t])
    o_ref[0] = ot_ref[...].T.astype(BF16)


def _dsa(qt, qit, ckv, ckvt, ki, wit, wuk, wuvt, bias):
    b, _, s = qt.shape
    t = DSA_TILE
    nt = s // t
    topk = min(TOPK_MAX, s // 4)
    assert s // PACKED_ROWS <= BF16_EXACT_INT
    return pl.pallas_call(
        functools.partial(_dsa_kernel, topk=topk, seq_len=s),
        out_shape=jax.ShapeDtypeStruct((b, s, A_WIDTH), BF16),
        grid=(b, nt),
        in_specs=[
            pl.BlockSpec((1, A_WIDTH, t), lambda bi, i: (bi, 0, i)),
            pl.BlockSpec((1, IDX_HEADS * IDX_DIM, t), lambda bi, i: (bi, 0, i)),
            pl.BlockSpec((1, s, KV_RANK), lambda bi, i: (bi, 0, 0)),
            pl.BlockSpec((1, nt, KV_RANK + SUM_ROWS, t), lambda bi, i: (bi, 0, 0, 0)),
            pl.BlockSpec((1, s, IDX_DIM), lambda bi, i: (bi, 0, 0)),
            pl.BlockSpec((1, IDX_HEADS, t), lambda bi, i: (bi, 0, i)),
            pl.BlockSpec((A_HEADS, KV_RANK, A_HEAD_DIM), lambda bi, i: (0, 0, 0)),
            pl.BlockSpec((A_HEADS, A_HEAD_DIM, KV_RANK), lambda bi, i: (0, 0, 0)),
            pl.BlockSpec((2, t, A_HEADS * t), lambda bi, i: (0, 0, 0)),
        ],
        out_specs=pl.BlockSpec((1, t, A_WIDTH), lambda bi, i: (bi, i, 0)),
        scratch_shapes=[pltpu.VMEM((s, t), I32), pltpu.VMEM((s, t), I16), pltpu.VMEM((s, t), I16),
                        pltpu.VMEM((KV_RANK + SUM_ROWS, A_HEADS * t), F32), pltpu.VMEM((A_WIDTH, t), F32)],
        compiler_params=_params("parallel", "arbitrary"),
        name="dsa",
    )(qt, qit, ckv, ckvt, ki, wit, wuk, wuvt, bias)


def _split3(x):
    hi = x.astype(BF16)
    r1 = x - hi.astype(F32)
    mid = r1.astype(BF16)
    lo = (r1 - mid.astype(F32)).astype(BF16)
    return hi, mid, lo


def _rwkv_kernel(r_ref, k_ref, v_ref, gl_ref, wl_ref, al_ref,
                 mur_ref, muk_ref, muv_ref, mug_ref, muw_ref, mua_ref,
                 w0_ref, a0_ref, kk_ref, ka_ref, rk_ref, lnw_ref, lnb_ref,
                 w2_ref, a2_ref, g2_ref,
                 o_ref,
                 pr_ref, pk_ref, pv_ref, pg_ref, pw_ref, pa_ref, state_ref):
    tl = RWKV_TILE
    c = RWKV_CHUNK
    nc = tl // c
    npair = R_WIDTH // PAIR

    @pl.when(pl.program_id(1) == 0)
    def _():
        state_ref[...] = jnp.zeros_like(state_ref)
        for ref in (pr_ref, pk_ref, pv_ref, pg_ref, pw_ref, pa_ref):
            ref[...] = jnp.zeros_like(ref)

    def shift_mix(x_ref, prev_ref, mu_ref):
        x = x_ref[0]
        first = lax.broadcasted_iota(I32, x.shape, 0) == 0
        xprev = jnp.where(first, prev_ref[...], pltpu.roll(x, 1, axis=0))
        prev_ref[...] = x[tl - 1:tl, :]
        return x + (xprev - x) * mu_ref[...]

    r = shift_mix(r_ref, pr_ref, mur_ref)
    k = shift_mix(k_ref, pk_ref, muk_ref)
    v = shift_mix(v_ref, pv_ref, muv_ref)
    gl = shift_mix(gl_ref, pg_ref, mug_ref)
    wl = shift_mix(wl_ref, pw_ref, muw_ref)
    al = shift_mix(al_ref, pa_ref, mua_ref)

    w_lin = w0_ref[...] + _dot(jnp.tanh(wl).astype(BF16), w2_ref[...])
    nx = -w_lin
    softplus = jnp.maximum(nx, 0.0) + jnp.log(1.0 + jnp.exp(-jnp.abs(nx)))
    ld = -jnp.exp(-softplus - 0.5)
    a = jax.nn.sigmoid(a0_ref[...] + _dot(al.astype(BF16), a2_ref[...]))
    gate = _dot(jax.nn.sigmoid(gl).astype(BF16), g2_ref[...])

    li = lax.broadcasted_iota(I32, (LANES, LANES), 0) // R_HEAD_DIM
    lj = lax.broadcasted_iota(I32, (LANES, LANES), 1) // R_HEAD_DIM
    head_ones = jnp.where(li == lj, 1.0, 0.0).astype(BF16)

    def head_sum(x):
        parts = []
        for j in range(R_WIDTH // LANES):
            hi, mid, lo = _split3(x[:, j * LANES:(j + 1) * LANES])
            parts.append(_dot(hi, head_ones) + _dot(mid, head_ones) + _dot(lo, head_ones))
        return jnp.concatenate(parts, axis=-1)

    kk = k * kk_ref[...]
    kk = kk / jnp.maximum(jnp.sqrt(head_sum(kk * kk)), 1e-12)
    k2 = k * (1.0 + (a - 1.0) * ka_ref[...])
    bb = kk * a

    ti = lax.broadcasted_iota(I32, (tl, tl), 0)
    tj = lax.broadcasted_iota(I32, (tl, tl), 1)
    tri = jnp.where((ti // c == tj // c) & (tj <= ti), 1.0, 0.0).astype(BF16)
    hi, mid, lo = _split3(ld)
    cum = _dot(tri, hi) + _dot(tri, mid) + _dot(tri, lo)
    cum_end = jnp.broadcast_to(cum.reshape(nc, c, R_WIDTH)[:, c - 1:c, :], (nc, c, R_WIDTH)).reshape(tl, R_WIDTH)
    e_in = jnp.exp(cum)
    e_neg = jnp.exp(-cum)
    e_out = jnp.exp(cum_end - cum)

    nb = nc * npair

    def to_pairs(x):
        x3 = x.reshape(nc, c, R_WIDTH)
        xs = jnp.stack([x3[:, :, p * PAIR:(p + 1) * PAIR] for p in range(npair)], axis=1)
        return xs.reshape(nb, c, PAIR)

    lane_head = lax.broadcasted_iota(I32, (2 * c, PAIR), 1) // R_HEAD_DIM
    row_head = lax.broadcasted_iota(I32, (2 * c, PAIR), 0) // c
    same = lane_head == row_head
    rt = lax.broadcasted_iota(I32, (2 * c, PAIR), 0) % c
    ct = lax.broadcasted_iota(I32, (2 * c, PAIR), 1) % c
    strict = same & (ct < rt)
    incl = same & (ct <= rt)
    eye = jnp.where(same & (ct == rt), 1.0, 0.0).astype(F32)

    def bd(x):
        return jnp.where(same, jnp.concatenate([x, x], axis=1), 0.0)

    lane0 = lax.broadcasted_iota(I32, (c, PAIR), 1) < R_HEAD_DIM

    abar = bd(to_pairs(-kk * jnp.exp(cum - ld))).astype(BF16)
    rbar = bd(to_pairs(r * e_in)).astype(BF16)
    bt = bd(to_pairs(bb * e_neg)).astype(BF16)
    kt = bd(to_pairs(k2 * e_neg)).astype(BF16)
    bk = jnp.concatenate([bd(to_pairs(bb * e_out)), bd(to_pairs(k2 * e_out))], axis=1).astype(BF16)
    v_pl = to_pairs(v)
    vbd = bd(v_pl)
    decay_end = to_pairs(jnp.exp(cum_end))[:, 0:1, :]
    aa = _bdot_nt(jnp.concatenate([abar, rbar], axis=1), jnp.concatenate([bt, kt], axis=1))
    a_ab = jnp.where(strict, aa[:, :2 * c, :PAIR], 0.0)
    a_ak = jnp.where(strict, aa[:, :2 * c, PAIR:], 0.0).astype(BF16)
    a_rb = jnp.where(incl, aa[:, 2 * c:, :PAIR], 0.0).astype(BF16)
    a_rk = jnp.where(incl, aa[:, 2 * c:, PAIR:], 0.0).astype(BF16)
    pw = a_ab.astype(BF16)
    tinv = eye + a_ab
    for _ in range(int(math.log2(c)) - 1):
        pw = _bdot(pw, pw).astype(BF16)
        tinv = tinv + _bdot(tinv.astype(BF16), pw)
    akv = _bdot(a_ak, vbd.astype(BF16))
    wu = _bdot(tinv.astype(BF16), jnp.concatenate([abar, akv.astype(BF16)], axis=2))
    w16 = wu[:, :, :PAIR].astype(BF16)
    u_bd = wu[:, :, PAIR:]

    s_bd = state_ref[...]
    e_parts, rs_parts = [], []
    for ci in range(nc):
        sl = slice(ci * npair, (ci + 1) * npair)
        wr = _bdot_nt(jnp.concatenate([w16[sl], rbar[sl]], axis=1), s_bd.astype(BF16))
        e_bd = wr[:, :2 * c] + u_bd[sl]
        e_parts.append(e_bd)
        rs_parts.append(wr[:, 2 * c:])
        ev = jnp.concatenate([e_bd, vbd[sl]], axis=1)
        ds = _bdot(jnp.swapaxes(ev, 1, 2).astype(BF16), bk[sl])
        s_bd = s_bd * decay_end[sl] + jnp.where(same, ds, 0.0)
    state_ref[...] = s_bd

    ev_all = jnp.concatenate([jnp.concatenate(e_parts, axis=0), vbd], axis=1).astype(BF16)
    y_bd = jnp.concatenate(rs_parts, axis=0) + _bdot(jnp.concatenate([a_rb, a_rk], axis=2), ev_all)
    mean = jnp.sum(y_bd, axis=-1, keepdims=True) * (1.0 / R_HEAD_DIM)
    dev = jnp.where(same, y_bd - mean, 0.0)
    var = jnp.sum(dev * dev, axis=-1, keepdims=True) * (1.0 / R_HEAD_DIM)
    yn = dev * lax.rsqrt(var + GN_EPS)
    yn = yn[:, :c] + yn[:, c:]
    rkv = to_pairs(r * k2 * rk_ref[...])
    s0 = jnp.sum(jnp.where(lane0, rkv, 0.0), axis=-1, keepdims=True)
    s1 = jnp.sum(jnp.where(lane0, 0.0, rkv), axis=-1, keepdims=True)
    bonus = jnp.where(lane0, s0, s1) * v_pl
    def per_pair(row_ref):
        rows = [row_ref[:, p * PAIR:(p + 1) * PAIR] for p in range(npair)]
        return jnp.stack(rows * nc, axis=0)

    out = ((yn * per_pair(lnw_ref) + per_pair(lnb_ref) + bonus) * to_pairs(gate)).astype(BF16)
    for ci in range(nc):
        for p in range(npair):
            o_ref[0, ci * c:(ci + 1) * c, p * PAIR:(p + 1) * PAIR] = out[ci * npair + p]


def _rwkv(z, prm):
    b, s, _ = z.shape
    tl = RWKV_TILE

    def seg(width, off):
        return pl.BlockSpec((1, tl, width), lambda bi, i: (bi, i, off // width))

    def full(arr):
        return pl.BlockSpec(arr.shape, lambda bi, i: (0,) * arr.ndim)

    names = ("mu_r", "mu_k", "mu_v", "mu_g", "mu_w", "mu_a", "w0", "a0", "k_k", "k_a", "r_k", "ln_w", "ln_b",
             "w2", "a2", "g2")
    consts = [prm[n] for n in names]
    return pl.pallas_call(
        _rwkv_kernel,
        out_shape=jax.ShapeDtypeStruct((b, s, R_WIDTH), BF16),
        grid=(b, s // tl),
        in_specs=[seg(R_WIDTH, Z_R), seg(R_WIDTH, Z_K), seg(R_WIDTH, Z_V), seg(GATE_LORA, Z_GL),
                  seg(LORA_PAD, Z_WL), seg(LORA_PAD, Z_AL)] + [full(x) for x in consts],
        out_specs=pl.BlockSpec((1, tl, R_WIDTH), lambda bi, i: (bi, i, 0)),
        scratch_shapes=[
            pltpu.VMEM((1, R_WIDTH), F32), pltpu.VMEM((1, R_WIDTH), F32), pltpu.VMEM((1, R_WIDTH), F32),
            pltpu.VMEM((1, GATE_LORA), F32), pltpu.VMEM((1, LORA_PAD), F32), pltpu.VMEM((1, LORA_PAD), F32),
            pltpu.VMEM((R_WIDTH // PAIR, PAIR, PAIR), F32),
        ],
        compiler_params=_params("parallel", "arbitrary"),
        name="rwkv7",
    )(z, z, z, z, z, z, *consts)


def _pad_cols(w, width):
    return jnp.pad(w, ((0, 0), (0, width - w.shape[1])))


def _w_in_layout_kernel(wt_ref, o_ref):
    wt = wt_ref[0]
    cols = wt.shape[1]
    offs = [0]
    for n in (A_WIDTH, KV_RANK, IDX_HEADS * IDX_DIM, IDX_DIM + IDX_HEADS,
              3 * R_WIDTH, DECAY_LORA, AAA_LORA, GATE_LORA):
        offs.append(offs[-1] + n)
    q, ckv, qi, kiwi, rkv, wl, al, gl = (wt[offs[i]:offs[i + 1], :] for i in range(8))

    def padded(x, height):
        return jnp.concatenate([x, jnp.zeros((height - x.shape[0], cols), x.dtype)], axis=0)

    rows = [q, qi, rkv, gl, ckv, padded(kiwi, LORA_PAD), padded(wl, LORA_PAD), padded(al, LORA_PAD),
            jnp.zeros((Z_WIDTH - Z_END, cols), wt.dtype)]
    o_ref[0] = jnp.concatenate(rows, axis=0).T.astype(BF16)


def _layout_w_in(w_in):
    depth, d, p = w_in.shape
    tc = _divisor_tile(d, 256)
    return pl.pallas_call(
        _w_in_layout_kernel,
        out_shape=jax.ShapeDtypeStruct((depth, d, Z_WIDTH), BF16),
        grid=(depth, d // tc),
        in_specs=[pl.BlockSpec((1, p, tc), lambda l, i: (l, 0, i))],
        out_specs=pl.BlockSpec((1, tc, Z_WIDTH), lambda l, i: (l, i, 0)),
        compiler_params=_params("parallel", "parallel"),
        name="w_in_layout",
    )(jnp.swapaxes(w_in, 1, 2))


def _pad_rows(w, rows):
    return jnp.pad(w, ((0, rows - w.shape[0]), (0, 0)))


def _rwkv_params(l, mu, w0, w2, a0, a2, g2, k_k, k_a, r_k, ln_w, ln_b):
    m = mu[l]
    o = [0, R_WIDTH, 2 * R_WIDTH, 3 * R_WIDTH, 3 * R_WIDTH + DECAY_LORA, 3 * R_WIDTH + DECAY_LORA + AAA_LORA]
    row = lambda x: x.reshape(1, -1)
    return {
        "mu_r": row(m[o[0]:o[1]]), "mu_k": row(m[o[1]:o[2]]), "mu_v": row(m[o[2]:o[3]]),
        "mu_w": _pad_cols(row(m[o[3]:o[4]]), LORA_PAD), "mu_a": _pad_cols(row(m[o[4]:o[5]]), LORA_PAD),
        "mu_g": row(m[o[5]:]),
        "w0": row(w0[l]), "a0": row(a0[l]), "k_k": row(k_k[l]), "k_a": row(k_a[l]), "r_k": row(r_k[l]),
        "ln_w": row(ln_w[l]), "ln_b": row(ln_b[l]),
        "w2": _pad_rows(w2[l], LORA_PAD).astype(BF16), "a2": _pad_rows(a2[l], LORA_PAD).astype(BF16),
        "g2": g2[l].astype(BF16),
    }


def kernel(x, c, t5_bias, ada_w, ada_b, norm_g, ffn_w_in, ffn_w_out, w_in, ckv_norm_g, w_uk, w_uv, rwkv_mu, rwkv_w0, rwkv_w2, rwkv_a0, rwkv_a2, rwkv_g2, rwkv_k_k, rwkv_k_a, rwkv_r_k, rwkv_ln_w, rwkv_ln_b, w_out, final_norm_g):
    b, s, d = x.shape
    depth = ada_w.shape[0]
    assert s % DSA_TILE == 0 and s % RWKV_TILE == 0 and d == A_WIDTH + R_WIDTH
    mod = _ada_mod(c, ada_w, ada_b).reshape(depth, b, N_SUB, 3, 1, d)
    bias = _bias_tiles(t5_bias)
    final_g = final_norm_g.reshape(1, d)
    w_in_z = _layout_w_in(w_in)
    ffn_in16 = ffn_w_in
    ffn_out16 = ffn_w_out
    w_out16 = w_out.astype(BF16)
    h = x
    for l in range(depth):
        shift = lambda i: mod[l, :, i, 0]
        scale = lambda i: mod[l, :, i, 1]
        gate = lambda i: mod[l, :, i, 2]
        g = lambda i: norm_g[l, i].reshape(1, d)
        h = _ffn(h, g(0), shift(0), scale(0), gate(0), ffn_in16, ffn_out16, l, 0, final_g, final_norm=False)
        z = _proj(h, g(1), shift(1), scale(1), w_in_z, l)
        qt, qit, ckv, ckvt, ki, wit = _dsa_prep(z, ckv_norm_g[l].reshape(1, KV_RANK))
        wuk = jnp.transpose(w_uk[l], (1, 0, 2)).astype(BF16)
        wuvt = jnp.transpose(w_uv[l], (1, 2, 0)).astype(BF16)
        o_a = _dsa(qt, qit, ckv, ckvt, ki, wit, wuk, wuvt, bias)
        o_r = _rwkv(z, _rwkv_params(l, rwkv_mu, rwkv_w0, rwkv_w2, rwkv_a0, rwkv_a2, rwkv_g2, rwkv_k_k,
                                    rwkv_k_a, rwkv_r_k, rwkv_ln_w, rwkv_ln_b))
        h = _outproj(o_a, o_r, h, gate(1), w_out16, l)
        h = _ffn(h, g(2), shift(2), scale(2), gate(2), ffn_in16, ffn_out16, l, 1, final_g,
                 final_norm=(l == depth - 1))
    return h
```

```python
import functools
import math

import jax
import jax.numpy as jnp
from jax import lax
from jax.experimental import pallas as pl
from jax.experimental.pallas import tpu as pltpu

F32 = jnp.float32
BF16 = jnp.bfloat16
I32 = jnp.int32
I16 = jnp.int16

A_HEADS = 8
A_HEAD_DIM = 128
A_WIDTH = A_HEADS * A_HEAD_DIM
KV_RANK = 256
IDX_HEADS = 16
IDX_DIM = 64
TOPK_MAX = 256
REL_BUCKETS = 32
REL_MAX_EXACT = REL_BUCKETS // 2
REL_MAX_DIST = 128
R_HEAD_DIM = 64
R_WIDTH = 1024
DECAY_LORA = 96
AAA_LORA = 96
GATE_LORA = 256
GN_EPS = 64e-5
RMS_EPS = 1e-6
N_SUB = 3

LANES = 128
SUBLANES = 8
VMEM_LIMIT_BYTES = 56 * 1024 * 1024

LORA_PAD = 128
Z_Q = 0
Z_QI = Z_Q + A_WIDTH
Z_R = Z_QI + IDX_HEADS * IDX_DIM
Z_K = Z_R + R_WIDTH
Z_V = Z_K + R_WIDTH
Z_GL = Z_V + R_WIDTH
Z_CKV = Z_GL + GATE_LORA
Z_KIWI = Z_CKV + KV_RANK
Z_WL = Z_KIWI + LORA_PAD
Z_AL = Z_WL + LORA_PAD
Z_END = Z_AL + LORA_PAD
Z_WIDTH = 6144

FFN_TOKEN_TILE = 1024
FFN_HIDDEN_TILE = 256
ADALN_SLAB_ROWS = 16
ADALN_UNROLL = 8
DSA_TILE = 256
RWKV_CHUNK = 64
RWKV_TILE = 256
PAIR = 2 * R_HEAD_DIM

I16_MIN = -(2 ** 15)
PACKED_ROWS = 2 * SUBLANES
SUM_ROWS = PACKED_ROWS
BF16_EXACT_INT = 256
COUNT_CHAINS = 4
KEY_NEG_INF = -2139095041
MASKED_LOGIT = -(2.0 ** 100)
LOG2E = math.log2(math.e)
FAR_DISTANCE = math.ceil(REL_MAX_EXACT * (REL_MAX_DIST / REL_MAX_EXACT)
                         ** ((REL_BUCKETS - 1 - REL_MAX_EXACT) / (REL_BUCKETS - REL_MAX_EXACT)))


def _dot(a, b):
    return jnp.dot(a, b, preferred_element_type=F32)


def _bdot(a, b):
    return lax.dot_general(a, b, (((2,), (1,)), ((0,), (0,))), preferred_element_type=F32)


def _bdot_nt(a, b):
    return lax.dot_general(a, b, (((2,), (2,)), ((0,), (0,))), preferred_element_type=F32)


def _rms(x, g, eps):
    ms = jnp.mean(x * x, axis=-1, keepdims=True)
    return x * lax.rsqrt(ms + eps) * g


def _divisor_tile(n, pref):
    if n <= pref:
        return n
    t = (pref // LANES) * LANES
    while t > LANES and n % t:
        t -= LANES
    assert n % t == 0, (n, pref)
    return t


def _params(*sem):
    return pltpu.CompilerParams(dimension_semantics=sem, vmem_limit_bytes=VMEM_LIMIT_BYTES)


def _ada_kernel(c_ref, w_ref, b_ref, o_ref):
    c = c_ref[...]
    ca = (c * jax.nn.sigmoid(c)).astype(BF16)
    o_ref[0] = _dot(ca, w_ref[0].astype(BF16)) + b_ref[0]


def _ada_mod(c, ada_w, ada_b):
    depth, d, n = ada_w.shape
    b = c.shape[0]
    bp = -(-b // SUBLANES) * SUBLANES
    cp = jnp.pad(c, ((0, bp - b), (0, 0)))
    tn = _divisor_tile(n, 1024)
    out = pl.pallas_call(
        _ada_kernel,
        out_shape=jax.ShapeDtypeStruct((depth, bp, n), F32),
        grid=(depth, n // tn),
        in_specs=[
            pl.BlockSpec((bp, d), lambda l, j: (0, 0)),
            pl.BlockSpec((1, d, tn), lambda l, j: (l, 0, j)),
            pl.BlockSpec((1, 1, tn), lambda l, j: (l, 0, j)),
        ],
        out_specs=pl.BlockSpec((1, bp, tn), lambda l, j: (l, 0, j)),
        compiler_params=_params("arbitrary", "arbitrary"),
        name="ada_mod",
    )(cp, ada_w, ada_b.reshape(depth, 1, n))
    return out[:, :b]


def _adaln_rows(h_ref, g_ref, scale_ref, shift_ref, hn_ref, copy_ref=None):
    rows = h_ref.shape[1]
    g = g_ref[...]
    mul = 1.0 + scale_ref[0]
    add = shift_ref[0]

    def slab(i, carry):
        sl = pl.ds(pl.multiple_of(i * ADALN_SLAB_ROWS, ADALN_SLAB_ROWS), ADALN_SLAB_ROWS)
        h = h_ref[0, sl, :]
        hn_ref[sl, :] = (_rms(h, g, RMS_EPS) * mul + add).astype(BF16)
        if copy_ref is not None:
            copy_ref[0, sl, :] = h
        return carry

    lax.fori_loop(0, rows // ADALN_SLAB_ROWS, slab, 0, unroll=ADALN_UNROLL)


def _ffn_kernel(h_ref, g_ref, shift_ref, scale_ref, gate_ref, wg_ref, wu_ref, wo_ref, fg_ref,
                o_ref, hn_ref, *, final_norm):
    f = pl.program_id(2)

    @pl.when(f == 0)
    def _():
        _adaln_rows(h_ref, g_ref, scale_ref, shift_ref, hn_ref, o_ref)

    hn = hn_ref[...]
    g = _dot(hn, wg_ref[...].astype(BF16))
    u = _dot(hn, wu_ref[...].astype(BF16))
    act = (g * jax.nn.sigmoid(g) * u).astype(BF16)
    o_ref[0] += (0.5 * gate_ref[0]) * _dot(act, wo_ref[...].astype(BF16))

    if final_norm:
        @pl.when(f == pl.num_programs(2) - 1)
        def _():
            o_ref[0] = _rms(o_ref[0], fg_ref[...], RMS_EPS)


def _ffn(h, g, shift, scale, gate, w_in, w_out, layer, which, final_g, *, final_norm):
    b, s, d = h.shape
    ff = w_out.shape[2]
    tm = _divisor_tile(s, FFN_TOKEN_TILE)
    tf = _divisor_tile(ff, FFN_HIDDEN_TILE)
    nf = ff // tf
    vec = pl.BlockSpec((1, 1, d), lambda bi, i, f: (bi, 0, 0))
    row = pl.BlockSpec((1, d), lambda bi, i, f: (0, 0))
    tile = pl.BlockSpec((1, tm, d), lambda bi, i, f: (bi, i, 0))
    return pl.pallas_call(
        functools.partial(_ffn_kernel, final_norm=final_norm),
        out_shape=jax.ShapeDtypeStruct((b, s, d), F32),
        grid=(b, s // tm, nf),
        in_specs=[
            tile, row, vec, vec, vec,
            pl.BlockSpec((None, None, d, tf), lambda bi, i, f: (layer, which, 0, f)),
            pl.BlockSpec((None, None, d, tf), lambda bi, i, f: (layer, which, 0, nf + f)),
            pl.BlockSpec((None, None, tf, d), lambda bi, i, f: (layer, which, f, 0)),
            row,
        ],
        out_specs=tile,
        scratch_shapes=[pltpu.VMEM((tm, d), BF16)],
        compiler_params=_params("parallel", "parallel", "arbitrary"),
        name="ffn",
    )(h, g, shift, scale, gate, w_in, w_in, w_out, final_g)


def _proj_kernel(h_ref, g_ref, shift_ref, scale_ref, w_ref, o_ref, hn_ref):
    @pl.when(pl.program_id(2) == 0)
    def _():
        _adaln_rows(h_ref, g_ref, scale_ref, shift_ref, hn_ref)

    o_ref[0] = _dot(hn_ref[...], w_ref[...])


def _proj(h, g, shift, scale, w, layer):
    b, s, d = h.shape
    p = w.shape[2]
    tm = _divisor_tile(s, 1024)
    tn = _divisor_tile(p, 1024)
    vec = pl.BlockSpec((1, 1, d), lambda bi, i, n: (bi, 0, 0))
    return pl.pallas_call(
        _proj_kernel,
        out_shape=jax.ShapeDtypeStruct((b, s, p), F32),
        grid=(b, s // tm, p // tn),
        in_specs=[
            pl.BlockSpec((1, tm, d), lambda bi, i, n: (bi, i, 0)),
            pl.BlockSpec((1, d), lambda bi, i, n: (0, 0)),
            vec, vec,
            pl.BlockSpec((None, d, tn), lambda bi, i, n: (layer, 0, n)),
        ],
        out_specs=pl.BlockSpec((1, tm, tn), lambda bi, i, n: (bi, i, n)),
        scratch_shapes=[pltpu.VMEM((tm, d), BF16)],
        compiler_params=_params("parallel", "parallel", "arbitrary"),
        name="proj",
    )(h, g, shift, scale, w)


def _outproj_kernel(oa_ref, or_ref, h_ref, gate_ref, wa_ref, wr_ref, o_ref):
    acc = _dot(oa_ref[0], wa_ref[...].astype(BF16)) + _dot(or_ref[0], wr_ref[...].astype(BF16))
    o_ref[0] = h_ref[0] + gate_ref[0] * acc


def _outproj(o_a, o_r, h, gate, w_out, layer):
    b, s, d = h.shape
    tm = _divisor_tile(s, 1024)
    tn = _divisor_tile(d, 1024)
    wa = o_a.shape[-1]
    wr = o_r.shape[-1]
    return pl.pallas_call(
        _outproj_kernel,
        out_shape=jax.ShapeDtypeStruct((b, s, d), F32),
        grid=(b, s // tm, d // tn),
        in_specs=[
            pl.BlockSpec((1, tm, wa), lambda bi, i, n: (bi, i, 0)),
            pl.BlockSpec((1, tm, wr), lambda bi, i, n: (bi, i, 0)),
            pl.BlockSpec((1, tm, tn), lambda bi, i, n: (bi, i, n)),
            pl.BlockSpec((1, 1, tn), lambda bi, i, n: (bi, 0, n)),
            pl.BlockSpec((None, wa, tn), lambda bi, i, n: (layer, 0, n)),
            pl.BlockSpec((None, wr, tn), lambda bi, i, n: (layer, wa // wr, n)),
        ],
        out_specs=pl.BlockSpec((1, tm, tn), lambda bi, i, n: (bi, i, n)),
        compiler_params=_params("parallel", "parallel", "arbitrary"),
        name="outproj",
    )(o_a, o_r, h, gate, w_out, w_out)


def _dsa_prep_kernel(q_ref, qi_ref, ckv_ref, kiwi_ref, g_ref, qt_ref, qit_ref, ckv_o, ckvt_o, ki_o, wit_o):
    qt_ref[0] = q_ref[0].T.astype(BF16)
    qit_ref[0] = qi_ref[0].T.astype(BF16)
    cn = _rms(ckv_ref[0], g_ref[...], RMS_EPS)
    ckv_o[0] = cn.astype(BF16)
    ckvt_o[0, 0] = jnp.concatenate([cn.T.astype(BF16), jnp.ones((SUM_ROWS, cn.shape[0]), BF16)], axis=0)
    kw = kiwi_ref[0]
    ki_o[0] = kw[:, :IDX_DIM].astype(BF16)
    wit_o[0] = kw.T[IDX_DIM:IDX_DIM + IDX_HEADS, :] * (IDX_HEADS * IDX_DIM) ** -0.5


def _dsa_prep(z, ckv_g):
    b, s, _ = z.shape
    t = DSA_TILE
    nt = s // t

    def seg(width, off):
        return pl.BlockSpec((1, t, width), lambda bi, i: (bi, i, off // width))

    return pl.pallas_call(
        _dsa_prep_kernel,
        out_shape=(
            jax.ShapeDtypeStruct((b, A_WIDTH, s), BF16),
            jax.ShapeDtypeStruct((b, IDX_HEADS * IDX_DIM, s), BF16),
            jax.ShapeDtypeStruct((b, s, KV_RANK), BF16),
            jax.ShapeDtypeStruct((b, nt, KV_RANK + SUM_ROWS, t), BF16),
            jax.ShapeDtypeStruct((b, s, IDX_DIM), BF16),
            jax.ShapeDtypeStruct((b, IDX_HEADS, s), F32),
        ),
        grid=(b, nt),
        in_specs=[
            seg(A_WIDTH, Z_Q), seg(IDX_HEADS * IDX_DIM, Z_QI), seg(KV_RANK, Z_CKV), seg(LORA_PAD, Z_KIWI),
            pl.BlockSpec((1, KV_RANK), lambda bi, i: (0, 0)),
        ],
        out_specs=(
            pl.BlockSpec((1, A_WIDTH, t), lambda bi, i: (bi, 0, i)),
            pl.BlockSpec((1, IDX_HEADS * IDX_DIM, t), lambda bi, i: (bi, 0, i)),
            pl.BlockSpec((1, t, KV_RANK), lambda bi, i: (bi, i, 0)),
            pl.BlockSpec((1, 1, KV_RANK + SUM_ROWS, t), lambda bi, i: (bi, i, 0, 0)),
            pl.BlockSpec((1, t, IDX_DIM), lambda bi, i: (bi, i, 0)),
            pl.BlockSpec((1, IDX_HEADS, t), lambda bi, i: (bi, 0, i)),
        ),
        compiler_params=_params("parallel", "parallel"),
        name="dsa_prep",
    )(z, z, z, z, ckv_g)


def _bias_kernel(t5_ref, o_ref):
    t = DSA_TILE
    j = lax.broadcasted_iota(I32, (t, t), 0)
    i = lax.broadcasted_iota(I32, (t, t), 1)
    for didx in range(2):
        n = jnp.maximum(didx * t + i - j, 0)
        nf = jnp.maximum(n, 1).astype(F32)
        large = REL_MAX_EXACT + (jnp.log(nf / REL_MAX_EXACT) / math.log(REL_MAX_DIST / REL_MAX_EXACT)
                                 * (REL_BUCKETS - REL_MAX_EXACT)).astype(I32)
        large = jnp.minimum(large, REL_BUCKETS - 1)
        bucket = jnp.where(n < REL_MAX_EXACT, n, large)
        for h in range(A_HEADS):
            val = jnp.zeros((t, t), F32)
            for k in range(REL_BUCKETS):
                val = jnp.where(bucket == k, t5_ref[k, h], val)
            o_ref[didx, :, h * t:(h + 1) * t] = (val - t5_ref[REL_BUCKETS - 1, h]) * LOG2E


def _bias_tiles(t5_bias):
    t = DSA_TILE
    assert t + 1 >= FAR_DISTANCE
    return pl.pallas_call(
        _bias_kernel,
        out_shape=jax.ShapeDtypeStruct((2, t, A_HEADS * t), F32),
        in_specs=[pl.BlockSpec(memory_space=pltpu.SMEM)],
        out_specs=pl.BlockSpec(memory_space=pltpu.VMEM),
        compiler_params=pltpu.CompilerParams(vmem_limit_bytes=VMEM_LIMIT_BYTES),
        name="t5_bias_tiles",
    )(t5_bias)


def _dsa_kernel(qt_ref, qit_ref, ckv_ref, ckvt_ref, ki_ref, wit_ref, wuk_ref, wuvt_ref, bias_ref,
                o_ref, keys_ref, khi_ref, klo_ref, acc_ref, ot_ref, *, topk, seq_len):
    t = DSA_TILE
    qb = pl.program_id(1)
    nk = qb + 1
    row = lax.broadcasted_iota(I32, (t, t), 0)
    col = lax.broadcasted_iota(I32, (t, t), 1)

    def idx_body(kc, carry):
        kic = ki_ref[0, pl.ds(pl.multiple_of(kc * t, t), t), :]
        acc = jnp.zeros((t, t), F32)
        for h in range(IDX_HEADS):
            rel = _dot(kic, qit_ref[0, h * IDX_DIM:(h + 1) * IDX_DIM, :])
            acc = acc + jnp.maximum(rel, 0.0) * wit_ref[0, h:h + 1, :]
        acc = acc + 0.0
        bits = pltpu.bitcast(acc, I32)
        key = bits ^ ((bits >> 31) & 0x7FFFFFFF)
        causal = (kc * t + row) <= (qb * t + col)
        key = jnp.where(causal, key, KEY_NEG_INF)
        sl = pl.ds(pl.multiple_of(kc * t, t), t)
        keys_ref[sl, :] = key
        khi_ref[sl, :] = (key >> 16).astype(I16)
        klo_ref[sl, :] = ((key & 0xFFFF) + I16_MIN).astype(I16)
        return carry

    lax.fori_loop(0, nk, idx_body, 0)

    def count16(ref, cand):
        cand16 = cand.astype(I16)

        def body(kc, accs):
            x = ref[pl.ds(pl.multiple_of(kc * t, t), t), :]
            ones = jnp.where(x >= cand16, jnp.asarray(1, BF16), jnp.asarray(0, BF16))
            ones = ones.reshape(t // PACKED_ROWS, PACKED_ROWS, t)
            accs = list(accs)
            for i in range(t // PACKED_ROWS):
                accs[i % COUNT_CHAINS] = accs[i % COUNT_CHAINS] + ones[i]
            return tuple(accs)

        accs = (jnp.zeros((PACKED_ROWS, t), BF16),) * COUNT_CHAINS
        accs = lax.fori_loop(0, nk // 2, lambda i, a: body(2 * i + 1, body(2 * i, a)), accs)
        accs = lax.fori_loop(2 * (nk // 2), nk, body, accs)
        total = sum(a.astype(F32) for a in accs)
        return jnp.sum(total, axis=0, keepdims=True).astype(I32)

    def search16(ref, base):
        def accept(cand, cur):
            return jnp.where(base + count16(ref, cand) >= topk, cand, cur)

        v = accept(jnp.zeros((1, t), I32), jnp.full((1, t), I16_MIN, I32))
        return lax.fori_loop(0, 15, lambda i, v: accept(v | (jnp.int32(1) << (14 - i)), v), v)

    hi = search16(khi_ref, 0)
    above = count16(khi_ref, jnp.minimum(hi + 1, -I16_MIN - 1))
    hi16 = hi.astype(I16)

    def keep_low(kc, carry):
        sl = pl.ds(pl.multiple_of(kc * t, t), t)
        klo_ref[sl, :] = jnp.where(khi_ref[sl, :] == hi16, klo_ref[sl, :], jnp.asarray(I16_MIN, I16))
        return carry

    lax.fori_loop(0, nk, keep_low, 0)
    lo = search16(klo_ref, above)
    thr = jnp.maximum((hi << 16) | (lo - I16_MIN), KEY_NEG_INF + 1)

    def count(pred):
        def body(kc, acc):
            k = keys_ref[pl.ds(pl.multiple_of(kc * t, t), t), :]
            m = jnp.where(pred(k, kc), 1, 0).astype(I32)
            return acc + jnp.sum(m.reshape(t // SUBLANES, SUBLANES, t), axis=0)

        acc = lax.fori_loop(0, nk, body, jnp.zeros((SUBLANES, t), I32))
        return jnp.sum(acc, axis=0, keepdims=True)

    @pl.when(jnp.max(count(lambda k, kc: k >= thr)) > topk)
    def _():
        need = topk - count(lambda k, kc: k > thr)
        nbits = (seq_len - 1).bit_length()

        def pos_bit(i, y):
            cand = y | (jnp.int32(1) << (nbits - 1 - i))
            before = count(lambda k, kc: (k == thr) & ((kc * t + row) < cand))
            return jnp.where(before < need, cand, y)

        last_kept = lax.fori_loop(0, nbits, pos_bit, jnp.zeros((1, t), I32))

        def demote(kc, carry):
            sl = pl.ds(pl.multiple_of(kc * t, t), t)
            k = keys_ref[sl, :]
            keys_ref[sl, :] = jnp.where((k == thr) & ((kc * t + row) > last_kept), k - 1, k)
            return carry

        lax.fori_loop(0, nk, demote, 0)

    scale = A_HEAD_DIM ** -0.5 * LOG2E

    hw = A_HEADS * t
    qlat = jnp.concatenate(
        [(_dot(wuk_ref[h], qt_ref[0, h * A_HEAD_DIM:(h + 1) * A_HEAD_DIM, :]) * scale).astype(BF16)
         for h in range(A_HEADS)], axis=1)
    acc_ref[...] = jnp.zeros_like(acc_ref)

    def att_body(kc, m, *, near):
        sl = pl.ds(pl.multiple_of(kc * t, t), t)
        raw = _dot(ckv_ref[0, sl, :], qlat)
        sel = keys_ref[sl, :] >= thr
        parts = []
        for h in range(A_HEADS):
            lg_h = raw[:, h * t:(h + 1) * t]
            if near:
                lg_h = lg_h + bias_ref[qb - kc, :, h * t:(h + 1) * t]
            parts.append(jnp.where(sel, lg_h, MASKED_LOGIT).astype(BF16))
        lg = jnp.concatenate(parts, axis=1)
        m_new = jnp.maximum(m, jnp.max(lg, axis=0, keepdims=True).astype(F32))
        p = jnp.exp2(lg - m_new.astype(BF16))
        acc_ref[...] = acc_ref[...] * jnp.exp2(m - m_new) + _dot(ckvt_ref[0, kc], p)
        return m_new

    n_far = jnp.maximum(qb - 1, 0)
    m = lax.fori_loop(0, n_far, functools.partial(att_body, near=False), jnp.full((1, hw), MASKED_LOGIT, F32))
    lax.fori_loop(n_far, nk, functools.partial(att_body, near=True), m)
    olat = (acc_ref[:KV_RANK, :] / acc_ref[KV_RANK:KV_RANK + 1, :]).astype(BF16)
    for h in range(A_HEADS):
        ot_ref[h * A_HEAD_DIM:(h + 1) * A_HEAD_DIM, :] = _dot(wuvt_ref[h], olat[:, h * t:(h + 1) * t])
    o_ref[0] = ot_ref[...].T.astype(BF16)


def _dsa(qt, qit, ckv, ckvt, ki, wit, wuk, wuvt, bias):
    b, _, s = qt.shape
    t = DSA_TILE
    nt = s // t
    topk = min(TOPK_MAX, s // 4)
    assert s // PACKED_ROWS <= BF16_EXACT_INT
    return pl.pallas_call(
        functools.partial(_dsa_kernel, topk=topk, seq_len=s),
        out_shape=jax.ShapeDtypeStruct((b, s, A_WIDTH), BF16),
        grid=(b, nt),
        in_specs=[
            pl.BlockSpec((1, A_WIDTH, t), lambda bi, i: (bi, 0, i)),
            pl.BlockSpec((1, IDX_HEADS * IDX_DIM, t), lambda bi, i: (bi, 0, i)),
            pl.BlockSpec((1, s, KV_RANK), lambda bi, i: (bi, 0, 0)),
            pl.BlockSpec((1, nt, KV_RANK + SUM_ROWS, t), lambda bi, i: (bi, 0, 0, 0)),
            pl.BlockSpec((1, s, IDX_DIM), lambda bi, i: (bi, 0, 0)),
            pl.BlockSpec((1, IDX_HEADS, t), lambda bi, i: (bi, 0, i)),
            pl.BlockSpec((A_HEADS, KV_RANK, A_HEAD_DIM), lambda bi, i: (0, 0, 0)),
            pl.BlockSpec((A_HEADS, A_HEAD_DIM, KV_RANK), lambda bi, i: (0, 0, 0)),
            pl.BlockSpec((2, t, A_HEADS * t), lambda bi, i: (0, 0, 0)),
        ],
        out_specs=pl.BlockSpec((1, t, A_WIDTH), lambda bi, i: (bi, i, 0)),
        scratch_shapes=[pltpu.VMEM((s, t), I32), pltpu.VMEM((s, t), I16), pltpu.VMEM((s, t), I16),
                        pltpu.VMEM((KV_RANK + SUM_ROWS, A_HEADS * t), F32), pltpu.VMEM((A_WIDTH, t), F32)],
        compiler_params=_params("parallel", "arbitrary"),
        name="dsa",
    )(qt, qit, ckv, ckvt, ki, wit, wuk, wuvt, bias)


def _split3(x):
    hi = x.astype(BF16)
    r1 = x - hi.astype(F32)
    mid = r1.astype(BF16)
    lo = (r1 - mid.astype(F32)).astype(BF16)
    return hi, mid, lo


def _rwkv_kernel(r_ref, k_ref, v_ref, gl_ref, wl_ref, al_ref,
                 mur_ref, muk_ref, muv_ref, mug_ref, muw_ref, mua_ref,
                 w0_ref, a0_ref, kk_ref, ka_ref, rk_ref, lnw_ref, lnb_ref,
                 w2_ref, a2_ref, g2_ref,
                 o_ref,
                 pr_ref, pk_ref, pv_ref, pg_ref, pw_ref, pa_ref, state_ref):
    tl = RWKV_TILE
    c = RWKV_CHUNK
    nc = tl // c
    npair = R_WIDTH // PAIR

    @pl.when(pl.program_id(1) == 0)
    def _():
        state_ref[...] = jnp.zeros_like(state_ref)
        for ref in (pr_ref, pk_ref, pv_ref, pg_ref, pw_ref, pa_ref):
            ref[...] = jnp.zeros_like(ref)

    def shift_mix(x_ref, prev_ref, mu_ref):
        x = x_ref[0]
        first = lax.broadcasted_iota(I32, x.shape, 0) == 0
        xprev = jnp.where(first, prev_ref[...], pltpu.roll(x, 1, axis=0))
        prev_ref[...] = x[tl - 1:tl, :]
        return x + (xprev - x) * mu_ref[...]

    r = shift_mix(r_ref, pr_ref, mur_ref)
    k = shift_mix(k_ref, pk_ref, muk_ref)
    v = shift_mix(v_ref, pv_ref, muv_ref)
    gl = shift_mix(gl_ref, pg_ref, mug_ref)
    wl = shift_mix(wl_ref, pw_ref, muw_ref)
    al = shift_mix(al_ref, pa_ref, mua_ref)

    w_lin = w0_ref[...] + _dot(jnp.tanh(wl).astype(BF16), w2_ref[...])
    nx = -w_lin
    softplus = jnp.maximum(nx, 0.0) + jnp.log(1.0 + jnp.exp(-jnp.abs(nx)))
    ld = -jnp.exp(-softplus - 0.5)
    a = jax.nn.sigmoid(a0_ref[...] + _dot(al.astype(BF16), a2_ref[...]))
    gate = _dot(jax.nn.sigmoid(gl).astype(BF16), g2_ref[...])

    li = lax.broadcasted_iota(I32, (LANES, LANES), 0) // R_HEAD_DIM
    lj = lax.broadcasted_iota(I32, (LANES, LANES), 1) // R_HEAD_DIM
    head_ones = jnp.where(li == lj, 1.0, 0.0).astype(BF16)

    def head_sum(x):
        parts = []
        for j in range(R_WIDTH // LANES):
            hi, mid, lo = _split3(x[:, j * LANES:(j + 1) * LANES])
            parts.append(_dot(hi, head_ones) + _dot(mid, head_ones) + _dot(lo, head_ones))
        return jnp.concatenate(parts, axis=-1)

    kk = k * kk_ref[...]
    kk = kk / jnp.maximum(jnp.sqrt(head_sum(kk * kk)), 1e-12)
    k2 = k * (1.0 + (a - 1.0) * ka_ref[...])
    bb = kk * a

    ti = lax.broadcasted_iota(I32, (tl, tl), 0)
    tj = lax.broadcasted_iota(I32, (tl, tl), 1)
    tri = jnp.where((ti // c == tj // c) & (tj <= ti), 1.0, 0.0).astype(BF16)
    hi, mid, lo = _split3(ld)
    cum = _dot(tri, hi) + _dot(tri, mid) + _dot(tri, lo)
    cum_end = jnp.broadcast_to(cum.reshape(nc, c, R_WIDTH)[:, c - 1:c, :], (nc, c, R_WIDTH)).reshape(tl, R_WIDTH)
    e_in = jnp.exp(cum)
    e_neg = jnp.exp(-cum)
    e_out = jnp.exp(cum_end - cum)

    nb = nc * npair

    def to_pairs(x):
        x3 = x.reshape(nc, c, R_WIDTH)
        xs = jnp.stack([x3[:, :, p * PAIR:(p + 1) * PAIR] for p in range(npair)], axis=1)
        return xs.reshape(nb, c, PAIR)

    lane_head = lax.broadcasted_iota(I32, (2 * c, PAIR), 1) // R_HEAD_DIM
    row_head = lax.broadcasted_iota(I32, (2 * c, PAIR), 0) // c
    same = lane_head == row_head
    rt = lax.broadcasted_iota(I32, (2 * c, PAIR), 0) % c
    ct = lax.broadcasted_iota(I32, (2 * c, PAIR), 1) % c
    strict = same & (ct < rt)
    incl = same & (ct <= rt)
    eye = jnp.where(same & (ct == rt), 1.0, 0.0).astype(F32)

    def bd(x):
        return jnp.where(same, jnp.concatenate([x, x], axis=1), 0.0)

    lane0 = lax.broadcasted_iota(I32, (c, PAIR), 1) < R_HEAD_DIM

    abar = bd(to_pairs(-kk * jnp.exp(cum - ld))).astype(BF16)
    rbar = bd(to_pairs(r * e_in)).astype(BF16)
    bt = bd(to_pairs(bb * e_neg)).astype(BF16)
    kt = bd(to_pairs(k2 * e_neg)).astype(BF16)
    bk = jnp.concatenate([bd(to_pairs(bb * e_out)), bd(to_pairs(k2 * e_out))], axis=1).astype(BF16)
    v_pl = to_pairs(v)
    vbd = bd(v_pl)
    decay_end = to_pairs(jnp.exp(cum_end))[:, 0:1, :]
    aa = _bdot_nt(jnp.concatenate([abar, rbar], axis=1), jnp.concatenate([bt, kt], axis=1))
    a_ab = jnp.where(strict, aa[:, :2 * c, :PAIR], 0.0)
    a_ak = jnp.where(strict, aa[:, :2 * c, PAIR:], 0.0).astype(BF16)
    a_rb = jnp.where(incl, aa[:, 2 * c:, :PAIR], 0.0).astype(BF16)
    a_rk = jnp.where(incl, aa[:, 2 * c:, PAIR:], 0.0).astype(BF16)
    pw = a_ab.astype(BF16)
    tinv = eye + a_ab
    for _ in range(int(math.log2(c)) - 1):
        pw = _bdot(pw, pw).astype(BF16)
        tinv = tinv + _bdot(tinv.astype(BF16), pw)
    akv = _bdot(a_ak, vbd.astype(BF16))
    wu = _bdot(tinv.astype(BF16), jnp.concatenate([abar, akv.astype(BF16)], axis=2))
    w16 = wu[:, :, :PAIR].astype(BF16)
    u_bd = wu[:, :, PAIR:]

    s_bd = state_ref[...]
    e_parts, rs_parts = [], []
    for ci in range(nc):
        sl = slice(ci * npair, (ci + 1) * npair)
        wr = _bdot_nt(jnp.concatenate([w16[sl], rbar[sl]], axis=1), s_bd.astype(BF16))
        e_bd = wr[:, :2 * c] + u_bd[sl]
        e_parts.append(e_bd)
        rs_parts.append(wr[:, 2 * c:])
        ev = jnp.concatenate([e_bd, vbd[sl]], axis=1)
        ds = _bdot(jnp.swapaxes(ev, 1, 2).astype(BF16), bk[sl])
        s_bd = s_bd * decay_end[sl] + jnp.where(same, ds, 0.0)
    state_ref[...] = s_bd

    ev_all = jnp.concatenate([jnp.concatenate(e_parts, axis=0), vbd], axis=1).astype(BF16)
    y_bd = jnp.concatenate(rs_parts, axis=0) + _bdot(jnp.concatenate([a_rb, a_rk], axis=2), ev_all)
    mean = jnp.sum(y_bd, axis=-1, keepdims=True) * (1.0 / R_HEAD_DIM)
    dev = jnp.where(same, y_bd - mean, 0.0)
    var = jnp.sum(dev * dev, axis=-1, keepdims=True) * (1.0 / R_HEAD_DIM)
    yn = dev * lax.rsqrt(var + GN_EPS)
    yn = yn[:, :c] + yn[:, c:]
    rkv = to_pairs(r * k2 * rk_ref[...])
    s0 = jnp.sum(jnp.where(lane0, rkv, 0.0), axis=-1, keepdims=True)
    s1 = jnp.sum(jnp.where(lane0, 0.0, rkv), axis=-1, keepdims=True)
    bonus = jnp.where(lane0, s0, s1) * v_pl
    def per_pair(row_ref):
        rows = [row_ref[:, p * PAIR:(p + 1) * PAIR] for p in range(npair)]
        return jnp.stack(rows * nc, axis=0)

    out = ((yn * per_pair(lnw_ref) + per_pair(lnb_ref) + bonus) * to_pairs(gate)).astype(BF16)
    for ci in range(nc):
        for p in range(npair):
            o_ref[0, ci * c:(ci + 1) * c, p * PAIR:(p + 1) * PAIR] = out[ci * npair + p]


def _rwkv(z, prm):
    b, s, _ = z.shape
    tl = RWKV_TILE

    def seg(width, off):
        return pl.BlockSpec((1, tl, width), lambda bi, i: (bi, i, off // width))

    def full(arr):
        return pl.BlockSpec(arr.shape, lambda bi, i: (0,) * arr.ndim)

    names = ("mu_r", "mu_k", "mu_v", "mu_g", "mu_w", "mu_a", "w0", "a0", "k_k", "k_a", "r_k", "ln_w", "ln_b",
             "w2", "a2", "g2")
    consts = [prm[n] for n in names]
    return pl.pallas_call(
        _rwkv_kernel,
        out_shape=jax.ShapeDtypeStruct((b, s, R_WIDTH), BF16),
        grid=(b, s // tl),
        in_specs=[seg(R_WIDTH, Z_R), seg(R_WIDTH, Z_K), seg(R_WIDTH, Z_V), seg(GATE_LORA, Z_GL),
                  seg(LORA_PAD, Z_WL), seg(LORA_PAD, Z_AL)] + [full(x) for x in consts],
        out_specs=pl.BlockSpec((1, tl, R_WIDTH), lambda bi, i: (bi, i, 0)),
        scratch_shapes=[
            pltpu.VMEM((1, R_WIDTH), F32), pltpu.VMEM((1, R_WIDTH), F32), pltpu.VMEM((1, R_WIDTH), F32),
            pltpu.VMEM((1, GATE_LORA), F32), pltpu.VMEM((1, LORA_PAD), F32), pltpu.VMEM((1, LORA_PAD), F32),
            pltpu.VMEM((R_WIDTH // PAIR, PAIR, PAIR), F32),
        ],
        compiler_params=_params("parallel", "arbitrary"),
        name="rwkv7",
    )(z, z, z, z, z, z, *consts)


def _pad_cols(w, width):
    return jnp.pad(w, ((0, 0), (0, width - w.shape[1])))


def _w_in_layout_kernel(wt_ref, o_ref):
    wt = wt_ref[0]
    cols = wt.shape[1]
    offs = [0]
    for n in (A_WIDTH, KV_RANK, IDX_HEADS * IDX_DIM, IDX_DIM + IDX_HEADS,
              3 * R_WIDTH, DECAY_LORA, AAA_LORA, GATE_LORA):
        offs.append(offs[-1] + n)
    q, ckv, qi, kiwi, rkv, wl, al, gl = (wt[offs[i]:offs[i + 1], :] for i in range(8))

    def padded(x, height):
        return jnp.concatenate([x, jnp.zeros((height - x.shape[0], cols), x.dtype)], axis=0)

    rows = [q, qi, rkv, gl, ckv, padded(kiwi, LORA_PAD), padded(wl, LORA_PAD), padded(al, LORA_PAD),
            jnp.zeros((Z_WIDTH - Z_END, cols), wt.dtype)]
    o_ref[0] = jnp.concatenate(rows, axis=0).T.astype(BF16)


def _layout_w_in(w_in):
    depth, d, p = w_in.shape
    tc = _divisor_tile(d, 256)
    return pl.pallas_call(
        _w_in_layout_kernel,
        out_shape=jax.ShapeDtypeStruct((depth, d, Z_WIDTH), BF16),
        grid=(depth, d // tc),
        in_specs=[pl.BlockSpec((1, p, tc), lambda l, i: (l, 0, i))],
        out_specs=pl.BlockSpec((1, tc, Z_WIDTH), lambda l, i: (l, i, 0)),
        compiler_params=_params("parallel", "parallel"),
        name="w_in_layout",
    )(jnp.swapaxes(w_in, 1, 2))


def _pad_rows(w, rows):
    return jnp.pad(w, ((0, rows - w.shape[0]), (0, 0)))


def _rwkv_params(l, mu, w0, w2, a0, a2, g2, k_k, k_a, r_k, ln_w, ln_b):
    m = mu[l]
    o = [0, R_WIDTH, 2 * R_WIDTH, 3 * R_WIDTH, 3 * R_WIDTH + DECAY_LORA, 3 * R_WIDTH + DECAY_LORA + AAA_LORA]
    row = lambda x: x.reshape(1, -1)
    return {
        "mu_r": row(m[o[0]:o[1]]), "mu_k": row(m[o[1]:o[2]]), "mu_v": row(m[o[2]:o[3]]),
        "mu_w": _pad_cols(row(m[o[3]:o[4]]), LORA_PAD), "mu_a": _pad_cols(row(m[o[4]:o[5]]), LORA_PAD),
        "mu_g": row(m[o[5]:]),
        "w0": row(w0[l]), "a0": row(a0[l]), "k_k": row(k_k[l]), "k_a": row(k_a[l]), "r_k": row(r_k[l]),
        "ln_w": row(ln_w[l]), "ln_b": row(ln_b[l]),
        "w2": _pad_rows(w2[l], LORA_PAD).astype(BF16), "a2": _pad_rows(a2[l], LORA_PAD).astype(BF16),
        "g2": g2[l].astype(BF16),
    }


def kernel(x, c, t5_bias, ada_w, ada_b, norm_g, ffn_w_in, ffn_w_out, w_in, ckv_norm_g, w_uk, w_uv, rwkv_mu, rwkv_w0, rwkv_w2, rwkv_a0, rwkv_a2, rwkv_g2, rwkv_k_k, rwkv_k_a, rwkv_r_k, rwkv_ln_w, rwkv_ln_b, w_out, final_norm_g):
    b, s, d = x.shape
    depth = ada_w.shape[0]
    assert s % DSA_TILE == 0 and s % RWKV_TILE == 0 and d == A_WIDTH + R_WIDTH
    mod = _ada_mod(c, ada_w, ada_b).reshape(depth, b, N_SUB, 3, 1, d)
    bias = _bias_tiles(t5_bias)
    final_g = final_norm_g.reshape(1, d)
    w_in_z = _layout_w_in(w_in)
    h = x
    for l in range(depth):
        shift = lambda i: mod[l, :, i, 0]
        scale = lambda i: mod[l, :, i, 1]
        gate = lambda i: mod[l, :, i, 2]
        g = lambda i: norm_g[l, i].reshape(1, d)
        h = _ffn(h, g(0), shift(0), scale(0), gate(0), ffn_w_in, ffn_w_out, l, 0, final_g, final_norm=False)
        z = _proj(h, g(1), shift(1), scale(1), w_in_z, l)
        qt, qit, ckv, ckvt, ki, wit = _dsa_prep(z, ckv_norm_g[l].reshape(1, KV_RANK))
        wuk = jnp.transpose(w_uk[l], (1, 0, 2)).astype(BF16)
        wuvt = jnp.transpose(w_uv[l], (1, 2, 0)).astype(BF16)
        o_a = _dsa(qt, qit, ckv, ckvt, ki, wit, wuk, wuvt, bias)
        o_r = _rwkv(z, _rwkv_params(l, rwkv_mu, rwkv_w0, rwkv_w2, rwkv_a0, rwkv_a2, rwkv_g2, rwkv_k_k,
                                    rwkv_k_a, rwkv_r_k, rwkv_ln_w, rwkv_ln_b))
        h = _outproj(o_a, o_r, h, gate(1), w_out, l)
        h = _ffn(h, g(2), shift(2), scale(2), gate(2), ffn_w_in, ffn_w_out, l, 1, final_g,
                 final_norm=(l == depth - 1))
    return h
```

```python
import functools
import math

import jax
import jax.numpy as jnp
from jax import lax
from jax.experimental import pallas as pl
from jax.experimental.pallas import tpu as pltpu

F32 = jnp.float32
BF16 = jnp.bfloat16
I32 = jnp.int32
I16 = jnp.int16

A_HEADS = 8
A_HEAD_DIM = 128
A_WIDTH = A_HEADS * A_HEAD_DIM
KV_RANK = 256
IDX_HEADS = 16
IDX_DIM = 64
TOPK_MAX = 256
REL_BUCKETS = 32
REL_MAX_EXACT = REL_BUCKETS // 2
REL_MAX_DIST = 128
R_HEAD_DIM = 64
R_WIDTH = 1024
DECAY_LORA = 96
AAA_LORA = 96
GATE_LORA = 256
GN_EPS = 64e-5
RMS_EPS = 1e-6
N_SUB = 3

LANES = 128
SUBLANES = 8
VMEM_LIMIT_BYTES = 56 * 1024 * 1024

LORA_PAD = 128
Z_Q = 0
Z_QI = Z_Q + A_WIDTH
Z_R = Z_QI + IDX_HEADS * IDX_DIM
Z_K = Z_R + R_WIDTH
Z_V = Z_K + R_WIDTH
Z_GL = Z_V + R_WIDTH
Z_CKV = Z_GL + GATE_LORA
Z_KIWI = Z_CKV + KV_RANK
Z_WL = Z_KIWI + LORA_PAD
Z_AL = Z_WL + LORA_PAD
Z_END = Z_AL + LORA_PAD
Z_WIDTH = 6144

FFN_TOKEN_TILE = 1024
FFN_HIDDEN_TILE = 256
ADALN_SLAB_ROWS = 16
ADALN_UNROLL = 8
DSA_TILE = 256
RWKV_CHUNK = 64
RWKV_TILE = 256
PAIR = 2 * R_HEAD_DIM

I16_MIN = -(2 ** 15)
PACKED_ROWS = 2 * SUBLANES
SUM_ROWS = PACKED_ROWS
BF16_EXACT_INT = 256
COUNT_CHAINS = 4
KEY_NEG_INF = -2139095041
MASKED_LOGIT = -(2.0 ** 100)
LOG2E = math.log2(math.e)
FAR_DISTANCE = math.ceil(REL_MAX_EXACT * (REL_MAX_DIST / REL_MAX_EXACT)
                         ** ((REL_BUCKETS - 1 - REL_MAX_EXACT) / (REL_BUCKETS - REL_MAX_EXACT)))


def _dot(a, b):
    return jnp.dot(a, b, preferred_element_type=F32)


def _bdot(a, b):
    return lax.dot_general(a, b, (((2,), (1,)), ((0,), (0,))), preferred_element_type=F32)


def _bdot_nt(a, b):
    return lax.dot_general(a, b, (((2,), (2,)), ((0,), (0,))), preferred_element_type=F32)


def _rms(x, g, eps):
    ms = jnp.mean(x * x, axis=-1, keepdims=True)
    return x * lax.rsqrt(ms + eps) * g


def _divisor_tile(n, pref):
    if n <= pref:
        return n
    t = (pref // LANES) * LANES
    while t > LANES and n % t:
        t -= LANES
    assert n % t == 0, (n, pref)
    return t


def _params(*sem):
    return pltpu.CompilerParams(dimension_semantics=sem, vmem_limit_bytes=VMEM_LIMIT_BYTES)


def _ada_kernel(c_ref, w_ref, b_ref, o_ref):
    c = c_ref[...]
    ca = (c * jax.nn.sigmoid(c)).astype(BF16)
    o_ref[0] = _dot(ca, w_ref[0].astype(BF16)) + b_ref[0]


def _ada_mod(c, ada_w, ada_b):
    depth, d, n = ada_w.shape
    b = c.shape[0]
    bp = -(-b // SUBLANES) * SUBLANES
    cp = jnp.pad(c, ((0, bp - b), (0, 0)))
    tn = _divisor_tile(n, 1024)
    out = pl.pallas_call(
        _ada_kernel,
        out_shape=jax.ShapeDtypeStruct((depth, bp, n), F32),
        grid=(depth, n // tn),
        in_specs=[
            pl.BlockSpec((bp, d), lambda l, j: (0, 0)),
            pl.BlockSpec((1, d, tn), lambda l, j: (l, 0, j)),
            pl.BlockSpec((1, 1, tn), lambda l, j: (l, 0, j)),
        ],
        out_specs=pl.BlockSpec((1, bp, tn), lambda l, j: (l, 0, j)),
        compiler_params=_params("arbitrary", "arbitrary"),
        name="ada_mod",
    )(cp, ada_w, ada_b.reshape(depth, 1, n))
    return out[:, :b]


def _adaln_rows(h_ref, g_ref, scale_ref, shift_ref, hn_ref, copy_ref=None):
    rows = h_ref.shape[1]
    g = g_ref[...]
    mul = 1.0 + scale_ref[0]
    add = shift_ref[0]

    def slab(i, carry):
        sl = pl.ds(pl.multiple_of(i * ADALN_SLAB_ROWS, ADALN_SLAB_ROWS), ADALN_SLAB_ROWS)
        h = h_ref[0, sl, :]
        hn_ref[sl, :] = (_rms(h, g, RMS_EPS) * mul + add).astype(BF16)
        if copy_ref is not None:
            copy_ref[0, sl, :] = h
        return carry

    lax.fori_loop(0, rows // ADALN_SLAB_ROWS, slab, 0, unroll=ADALN_UNROLL)


def _ffn_kernel(h_ref, g_ref, shift_ref, scale_ref, gate_ref, wg_ref, wu_ref, wo_ref, fg_ref,
                o_ref, hn_ref, *, final_norm):
    f = pl.program_id(2)

    @pl.when(f == 0)
    def _():
        _adaln_rows(h_ref, g_ref, scale_ref, shift_ref, hn_ref, o_ref)

    hn = hn_ref[...]
    g = _dot(hn, wg_ref[...].astype(BF16))
    u = _dot(hn, wu_ref[...].astype(BF16))
    act = (g * jax.nn.sigmoid(g) * u).astype(BF16)
    o_ref[0] += (0.5 * gate_ref[0]) * _dot(act, wo_ref[...].astype(BF16))

    if final_norm:
        @pl.when(f == pl.num_programs(2) - 1)
        def _():
            o_ref[0] = _rms(o_ref[0], fg_ref[...], RMS_EPS)


def _ffn(h, g, shift, scale, gate, w_in, w_out, layer, which, final_g, *, final_norm):
    b, s, d = h.shape
    ff = w_out.shape[2]
    tm = _divisor_tile(s, FFN_TOKEN_TILE)
    tf = _divisor_tile(ff, FFN_HIDDEN_TILE)
    nf = ff // tf
    vec = pl.BlockSpec((1, 1, d), lambda bi, i, f: (bi, 0, 0))
    row = pl.BlockSpec((1, d), lambda bi, i, f: (0, 0))
    tile = pl.BlockSpec((1, tm, d), lambda bi, i, f: (bi, i, 0))
    return pl.pallas_call(
        functools.partial(_ffn_kernel, final_norm=final_norm),
        out_shape=jax.ShapeDtypeStruct((b, s, d), F32),
        grid=(b, s // tm, nf),
        in_specs=[
            tile, row, vec, vec, vec,
            pl.BlockSpec((None, None, d, tf), lambda bi, i, f: (layer, which, 0, f)),
            pl.BlockSpec((None, None, d, tf), lambda bi, i, f: (layer, which, 0, nf + f)),
            pl.BlockSpec((None, None, tf, d), lambda bi, i, f: (layer, which, f, 0)),
            row,
        ],
        out_specs=tile,
        scratch_shapes=[pltpu.VMEM((tm, d), BF16)],
        compiler_params=_params("parallel", "parallel", "arbitrary"),
        name="ffn",
    )(h, g, shift, scale, gate, w_in, w_in, w_out, final_g)


def _proj_kernel(h_ref, g_ref, shift_ref, scale_ref, w_ref, o_ref, hn_ref):
    @pl.when(pl.program_id(2) == 0)
    def _():
        _adaln_rows(h_ref, g_ref, scale_ref, shift_ref, hn_ref)

    o_ref[0] = _dot(hn_ref[...], w_ref[...])


def _proj(h, g, shift, scale, w, layer):
    b, s, d = h.shape
    p = w.shape[2]
    tm = _divisor_tile(s, 1024)
    tn = _divisor_tile(p, 1024)
    vec = pl.BlockSpec((1, 1, d), lambda bi, i, n: (bi, 0, 0))
    return pl.pallas_call(
        _proj_kernel,
        out_shape=jax.ShapeDtypeStruct((b, s, p), F32),
        grid=(b, s // tm, p // tn),
        in_specs=[
            pl.BlockSpec((1, tm, d), lambda bi, i, n: (bi, i, 0)),
            pl.BlockSpec((1, d), lambda bi, i, n: (0, 0)),
            vec, vec,
            pl.BlockSpec((None, d, tn), lambda bi, i, n: (layer, 0, n)),
        ],
        out_specs=pl.BlockSpec((1, tm, tn), lambda bi, i, n: (bi, i, n)),
        scratch_shapes=[pltpu.VMEM((tm, d), BF16)],
        compiler_params=_params("parallel", "parallel", "arbitrary"),
        name="proj",
    )(h, g, shift, scale, w)


def _outproj_kernel(oa_ref, or_ref, h_ref, gate_ref, wa_ref, wr_ref, o_ref):
    acc = _dot(oa_ref[0], wa_ref[...]) + _dot(or_ref[0], wr_ref[...])
    o_ref[0] = h_ref[0] + gate_ref[0] * acc


def _outproj(o_a, o_r, h, gate, w_out, layer):
    b, s, d = h.shape
    tm = _divisor_tile(s, 512)
    tn = d
    wa = o_a.shape[-1]
    wr = o_r.shape[-1]
    return pl.pallas_call(
        _outproj_kernel,
        out_shape=jax.ShapeDtypeStruct((b, s, d), F32),
        grid=(b, s // tm, d // tn),
        in_specs=[
            pl.BlockSpec((1, tm, wa), lambda bi, i, n: (bi, i, 0)),
            pl.BlockSpec((1, tm, wr), lambda bi, i, n: (bi, i, 0)),
            pl.BlockSpec((1, tm, tn), lambda bi, i, n: (bi, i, n)),
            pl.BlockSpec((1, 1, tn), lambda bi, i, n: (bi, 0, n)),
            pl.BlockSpec((None, wa, tn), lambda bi, i, n: (layer, 0, n)),
            pl.BlockSpec((None, wr, tn), lambda bi, i, n: (layer, wa // wr, n)),
        ],
        out_specs=pl.BlockSpec((1, tm, tn), lambda bi, i, n: (bi, i, n)),
        compiler_params=_params("parallel", "parallel", "arbitrary"),
        name="outproj",
    )(o_a, o_r, h, gate, w_out, w_out)


def _dsa_prep_kernel(q_ref, qi_ref, ckv_ref, kiwi_ref, g_ref, qt_ref, qit_ref, ckv_o, ckvt_o, ki_o, wit_o):
    qt_ref[0] = q_ref[0].T.astype(BF16)
    qit_ref[0] = qi_ref[0].T.astype(BF16)
    cn = _rms(ckv_ref[0], g_ref[...], RMS_EPS)
    ckv_o[0] = cn.astype(BF16)
    ckvt_o[0, 0] = jnp.concatenate([cn.T.astype(BF16), jnp.ones((SUM_ROWS, cn.shape[0]), BF16)], axis=0)
    kw = kiwi_ref[0]
    ki_o[0] = kw[:, :IDX_DIM].astype(BF16)
    wit_o[0] = kw.T[IDX_DIM:IDX_DIM + IDX_HEADS, :] * (IDX_HEADS * IDX_DIM) ** -0.5


def _dsa_prep(z, ckv_g):
    b, s, _ = z.shape
    t = DSA_TILE
    nt = s // t

    def seg(width, off):
        return pl.BlockSpec((1, t, width), lambda bi, i: (bi, i, off // width))

    return pl.pallas_call(
        _dsa_prep_kernel,
        out_shape=(
            jax.ShapeDtypeStruct((b, A_WIDTH, s), BF16),
            jax.ShapeDtypeStruct((b, IDX_HEADS * IDX_DIM, s), BF16),
            jax.ShapeDtypeStruct((b, s, KV_RANK), BF16),
            jax.ShapeDtypeStruct((b, nt, KV_RANK + SUM_ROWS, t), BF16),
            jax.ShapeDtypeStruct((b, s, IDX_DIM), BF16),
            jax.ShapeDtypeStruct((b, IDX_HEADS, s), F32),
        ),
        grid=(b, nt),
        in_specs=[
            seg(A_WIDTH, Z_Q), seg(IDX_HEADS * IDX_DIM, Z_QI), seg(KV_RANK, Z_CKV), seg(LORA_PAD, Z_KIWI),
            pl.BlockSpec((1, KV_RANK), lambda bi, i: (0, 0)),
        ],
        out_specs=(
            pl.BlockSpec((1, A_WIDTH, t), lambda bi, i: (bi, 0, i)),
            pl.BlockSpec((1, IDX_HEADS * IDX_DIM, t), lambda bi, i: (bi, 0, i)),
            pl.BlockSpec((1, t, KV_RANK), lambda bi, i: (bi, i, 0)),
            pl.BlockSpec((1, 1, KV_RANK + SUM_ROWS, t), lambda bi, i: (bi, i, 0, 0)),
            pl.BlockSpec((1, t, IDX_DIM), lambda bi, i: (bi, i, 0)),
            pl.BlockSpec((1, IDX_HEADS, t), lambda bi, i: (bi, 0, i)),
        ),
        compiler_params=_params("parallel", "parallel"),
        name="dsa_prep",
    )(z, z, z, z, ckv_g)


def _bias_kernel(t5_ref, o_ref):
    t = DSA_TILE
    j = lax.broadcasted_iota(I32, (t, t), 0)
    i = lax.broadcasted_iota(I32, (t, t), 1)
    for didx in range(2):
        n = jnp.maximum(didx * t + i - j, 0)
        nf = jnp.maximum(n, 1).astype(F32)
        large = REL_MAX_EXACT + (jnp.log(nf / REL_MAX_EXACT) / math.log(REL_MAX_DIST / REL_MAX_EXACT)
                                 * (REL_BUCKETS - REL_MAX_EXACT)).astype(I32)
        large = jnp.minimum(large, REL_BUCKETS - 1)
        bucket = jnp.where(n < REL_MAX_EXACT, n, large)
        for h in range(A_HEADS):
            val = jnp.zeros((t, t), F32)
            for k in range(REL_BUCKETS):
                val = jnp.where(bucket == k, t5_ref[k, h], val)
            o_ref[didx, :, h * t:(h + 1) * t] = (val - t5_ref[REL_BUCKETS - 1, h]) * LOG2E


def _bias_tiles(t5_bias):
    t = DSA_TILE
    assert t + 1 >= FAR_DISTANCE
    return pl.pallas_call(
        _bias_kernel,
        out_shape=jax.ShapeDtypeStruct((2, t, A_HEADS * t), F32),
        in_specs=[pl.BlockSpec(memory_space=pltpu.SMEM)],
        out_specs=pl.BlockSpec(memory_space=pltpu.VMEM),
        compiler_params=pltpu.CompilerParams(vmem_limit_bytes=VMEM_LIMIT_BYTES),
        name="t5_bias_tiles",
    )(t5_bias)


def _dsa_kernel(qt_ref, qit_ref, ckv_ref, ckvt_ref, ki_ref, wit_ref, wuk_ref, wuvt_ref, bias_ref,
                o_ref, keys_ref, khi_ref, klo_ref, acc_ref, ot_ref, *, topk, seq_len):
    t = DSA_TILE
    qb = pl.program_id(1)
    nk = qb + 1
    row = lax.broadcasted_iota(I32, (t, t), 0)
    col = lax.broadcasted_iota(I32, (t, t), 1)

    def idx_body(kc, carry):
        kic = ki_ref[0, pl.ds(pl.multiple_of(kc * t, t), t), :]
        acc = jnp.zeros((t, t), F32)
        for h in range(IDX_HEADS):
            rel = _dot(kic, qit_ref[0, h * IDX_DIM:(h + 1) * IDX_DIM, :])
            acc = acc + jnp.maximum(rel, 0.0) * wit_ref[0, h:h + 1, :]
        acc = acc + 0.0
        bits = pltpu.bitcast(acc, I32)
        key = bits ^ ((bits >> 31) & 0x7FFFFFFF)
        causal = (kc * t + row) <= (qb * t + col)
        key = jnp.where(causal, key, KEY_NEG_INF)
        sl = pl.ds(pl.multiple_of(kc * t, t), t)
        keys_ref[sl, :] = key
        khi_ref[sl, :] = (key >> 16).astype(I16)
        klo_ref[sl, :] = ((key & 0xFFFF) + I16_MIN).astype(I16)
        return carry

    lax.fori_loop(0, nk, idx_body, 0)

    def count16(ref, cand):
        cand16 = cand.astype(I16)

        def body(kc, accs):
            x = ref[pl.ds(pl.multiple_of(kc * t, t), t), :]
            ones = jnp.where(x >= cand16, jnp.asarray(1, BF16), jnp.asarray(0, BF16))
            ones = ones.reshape(t // PACKED_ROWS, PACKED_ROWS, t)
            accs = list(accs)
            for i in range(t // PACKED_ROWS):
                accs[i % COUNT_CHAINS] = accs[i % COUNT_CHAINS] + ones[i]
            return tuple(accs)

        accs = (jnp.zeros((PACKED_ROWS, t), BF16),) * COUNT_CHAINS
        accs = lax.fori_loop(0, nk // 2, lambda i, a: body(2 * i + 1, body(2 * i, a)), accs)
        accs = lax.fori_loop(2 * (nk // 2), nk, body, accs)
        total = sum(a.astype(F32) for a in accs)
        return jnp.sum(total, axis=0, keepdims=True).astype(I32)

    def search16(ref, base):
        def accept(cand, cur):
            return jnp.where(base + count16(ref, cand) >= topk, cand, cur)

        v = accept(jnp.zeros((1, t), I32), jnp.full((1, t), I16_MIN, I32))
        return lax.fori_loop(0, 15, lambda i, v: accept(v | (jnp.int32(1) << (14 - i)), v), v)

    hi = search16(khi_ref, 0)
    above = count16(khi_ref, jnp.minimum(hi + 1, -I16_MIN - 1))
    hi16 = hi.astype(I16)

    def keep_low(kc, carry):
        sl = pl.ds(pl.multiple_of(kc * t, t), t)
        klo_ref[sl, :] = jnp.where(khi_ref[sl, :] == hi16, klo_ref[sl, :], jnp.asarray(I16_MIN, I16))
        return carry

    lax.fori_loop(0, nk, keep_low, 0)
    lo = search16(klo_ref, above)
    thr = jnp.maximum((hi << 16) | (lo - I16_MIN), KEY_NEG_INF + 1)

    def count(pred):
        def body(kc, acc):
            k = keys_ref[pl.ds(pl.multiple_of(kc * t, t), t), :]
            m = jnp.where(pred(k, kc), 1, 0).astype(I32)
            return acc + jnp.sum(m.reshape(t // SUBLANES, SUBLANES, t), axis=0)

        acc = lax.fori_loop(0, nk, body, jnp.zeros((SUBLANES, t), I32))
        return jnp.sum(acc, axis=0, keepdims=True)

    @pl.when(jnp.max(count(lambda k, kc: k >= thr)) > topk)
    def _():
        need = topk - count(lambda k, kc: k > thr)
        nbits = (seq_len - 1).bit_length()

        def pos_bit(i, y):
            cand = y | (jnp.int32(1) << (nbits - 1 - i))
            before = count(lambda k, kc: (k == thr) & ((kc * t + row) < cand))
            return jnp.where(before < need, cand, y)

        last_kept = lax.fori_loop(0, nbits, pos_bit, jnp.zeros((1, t), I32))

        def demote(kc, carry):
            sl = pl.ds(pl.multiple_of(kc * t, t), t)
            k = keys_ref[sl, :]
            keys_ref[sl, :] = jnp.where((k == thr) & ((kc * t + row) > last_kept), k - 1, k)
            return carry

        lax.fori_loop(0, nk, demote, 0)

    scale = A_HEAD_DIM ** -0.5 * LOG2E

    hw = A_HEADS * t
    qlat = jnp.concatenate(
        [(_dot(wuk_ref[h], qt_ref[0, h * A_HEAD_DIM:(h + 1) * A_HEAD_DIM, :]) * scale).astype(BF16)
         for h in range(A_HEADS)], axis=1)
    acc_ref[...] = jnp.zeros_like(acc_ref)

    def att_body(kc, m, *, near):
        sl = pl.ds(pl.multiple_of(kc * t, t), t)
        raw = _dot(ckv_ref[0, sl, :], qlat)
        sel = keys_ref[sl, :] >= thr
        parts = []
        for h in range(A_HEADS):
            lg_h = raw[:, h * t:(h + 1) * t]
            if near:
                lg_h = lg_h + bias_ref[qb - kc, :, h * t:(h + 1) * t]
            parts.append(jnp.where(sel, lg_h, MASKED_LOGIT).astype(BF16))
        lg = jnp.concatenate(parts, axis=1)
        m_new = jnp.maximum(m, jnp.max(lg, axis=0, keepdims=True).astype(F32))
        p = jnp.exp2(lg - m_new.astype(BF16))
        acc_ref[...] = acc_ref[...] * jnp.exp2(m - m_new) + _dot(ckvt_ref[0, kc], p)
        return m_new

    n_far = jnp.maximum(qb - 1, 0)
    m = lax.fori_loop(0, n_far, functools.partial(att_body, near=False), jnp.full((1, hw), MASKED_LOGIT, F32))
    lax.fori_loop(n_far, nk, functools.partial(att_body, near=True), m)
    olat = (acc_ref[:KV_RANK, :] / acc_ref[KV_RANK:KV_RANK + 1, :]).astype(BF16)
    for h in range(A_HEADS):
        ot_ref[h * A_HEAD_DIM:(h + 1) * A_HEAD_DIM, :] = _dot(wuvt_ref[h], olat[:, h * t:(h + 1) * t])
    o_ref[0] = ot_ref[...].T.astype(BF16)


def _dsa(qt, qit, ckv, ckvt, ki, wit, wuk, wuvt, bias):
    b, _, s = qt.shape
    t = DSA_TILE
    nt = s // t
    topk = min(TOPK_MAX, s // 4)
    assert s // PACKED_ROWS <= BF16_EXACT_INT
    return pl.pallas_call(
        functools.partial(_dsa_kernel, topk=topk, seq_len=s),
        out_shape=jax.ShapeDtypeStruct((b, s, A_WIDTH), BF16),
        grid=(b, nt),
        in_specs=[
            pl.BlockSpec((1, A_WIDTH, t), lambda bi, i: (bi, 0, i)),
            pl.BlockSpec((1, IDX_HEADS * IDX_DIM, t), lambda bi, i: (bi, 0, i)),
            pl.BlockSpec((1, s, KV_RANK), lambda bi, i: (bi, 0, 0)),
            pl.BlockSpec((1, nt, KV_RANK + SUM_ROWS, t), lambda bi, i: (bi, 0, 0, 0)),
            pl.BlockSpec((1, s, IDX_DIM), lambda bi, i: (bi, 0, 0)),
            pl.BlockSpec((1, IDX_HEADS, t), lambda bi, i: (bi, 0, i)),
            pl.BlockSpec((A_HEADS, KV_RANK, A_HEAD_DIM), lambda bi, i: (0, 0, 0)),
            pl.BlockSpec((A_HEADS, A_HEAD_DIM, KV_RANK), lambda bi, i: (0, 0, 0)),
            pl.BlockSpec((2, t, A_HEADS * t), lambda bi, i: (0, 0, 0)),
        ],
        out_specs=pl.BlockSpec((1, t, A_WIDTH), lambda bi, i: (bi, i, 0)),
        scratch_shapes=[pltpu.VMEM((s, t), I32), pltpu.VMEM((s, t), I16), pltpu.VMEM((s, t), I16),
                        pltpu.VMEM((KV_RANK + SUM_ROWS, A_HEADS * t), F32), pltpu.VMEM((A_WIDTH, t), F32)],
        compiler_params=_params("parallel", "arbitrary"),
        name="dsa",
    )(qt, qit, ckv, ckvt, ki, wit, wuk, wuvt, bias)


def _split3(x):
    hi = x.astype(BF16)
    r1 = x - hi.astype(F32)
    mid = r1.astype(BF16)
    lo = (r1 - mid.astype(F32)).astype(BF16)
    return hi, mid, lo


def _rwkv_kernel(r_ref, k_ref, v_ref, gl_ref, wl_ref, al_ref,
                 mur_ref, muk_ref, muv_ref, mug_ref, muw_ref, mua_ref,
                 w0_ref, a0_ref, kk_ref, ka_ref, rk_ref, lnw_ref, lnb_ref,
                 w2_ref, a2_ref, g2_ref,
                 o_ref,
                 pr_ref, pk_ref, pv_ref, pg_ref, pw_ref, pa_ref, state_ref):
    tl = RWKV_TILE
    c = RWKV_CHUNK
    nc = tl // c
    npair = R_WIDTH // PAIR

    @pl.when(pl.program_id(1) == 0)
    def _():
        state_ref[...] = jnp.zeros_like(state_ref)
        for ref in (pr_ref, pk_ref, pv_ref, pg_ref, pw_ref, pa_ref):
            ref[...] = jnp.zeros_like(ref)

    def shift_mix(x_ref, prev_ref, mu_ref):
        x = x_ref[0]
        first = lax.broadcasted_iota(I32, x.shape, 0) == 0
        xprev = jnp.where(first, prev_ref[...], pltpu.roll(x, 1, axis=0))
        prev_ref[...] = x[tl - 1:tl, :]
        return x + (xprev - x) * mu_ref[...]

    r = shift_mix(r_ref, pr_ref, mur_ref)
    k = shift_mix(k_ref, pk_ref, muk_ref)
    v = shift_mix(v_ref, pv_ref, muv_ref)
    gl = shift_mix(gl_ref, pg_ref, mug_ref)
    wl = shift_mix(wl_ref, pw_ref, muw_ref)
    al = shift_mix(al_ref, pa_ref, mua_ref)

    w_lin = w0_ref[...] + _dot(jnp.tanh(wl).astype(BF16), w2_ref[...])
    nx = -w_lin
    softplus = jnp.maximum(nx, 0.0) + jnp.log(1.0 + jnp.exp(-jnp.abs(nx)))
    ld = -jnp.exp(-softplus - 0.5)
    a = jax.nn.sigmoid(a0_ref[...] + _dot(al.astype(BF16), a2_ref[...]))
    gate = _dot(jax.nn.sigmoid(gl).astype(BF16), g2_ref[...])

    li = lax.broadcasted_iota(I32, (LANES, LANES), 0) // R_HEAD_DIM
    lj = lax.broadcasted_iota(I32, (LANES, LANES), 1) // R_HEAD_DIM
    head_ones = jnp.where(li == lj, 1.0, 0.0).astype(BF16)

    def head_sum(x):
        parts = []
        for j in range(R_WIDTH // LANES):
            hi, mid, lo = _split3(x[:, j * LANES:(j + 1) * LANES])
            parts.append(_dot(hi, head_ones) + _dot(mid, head_ones) + _dot(lo, head_ones))
        return jnp.concatenate(parts, axis=-1)

    kk = k * kk_ref[...]
    kk = kk / jnp.maximum(jnp.sqrt(head_sum(kk * kk)), 1e-12)
    k2 = k * (1.0 + (a - 1.0) * ka_ref[...])
    bb = kk * a

    ti = lax.broadcasted_iota(I32, (tl, tl), 0)
    tj = lax.broadcasted_iota(I32, (tl, tl), 1)
    tri = jnp.where((ti // c == tj // c) & (tj <= ti), 1.0, 0.0).astype(BF16)
    hi, mid, lo = _split3(ld)
    cum = _dot(tri, hi) + _dot(tri, mid) + _dot(tri, lo)
    cum_end = jnp.broadcast_to(cum.reshape(nc, c, R_WIDTH)[:, c - 1:c, :], (nc, c, R_WIDTH)).reshape(tl, R_WIDTH)
    e_in = jnp.exp(cum)
    e_neg = jnp.exp(-cum)
    e_out = jnp.exp(cum_end - cum)

    nb = nc * npair

    def to_pairs(x):
        x3 = x.reshape(nc, c, R_WIDTH)
        xs = jnp.stack([x3[:, :, p * PAIR:(p + 1) * PAIR] for p in range(npair)], axis=1)
        return xs.reshape(nb, c, PAIR)

    lane_head = lax.broadcasted_iota(I32, (2 * c, PAIR), 1) // R_HEAD_DIM
    row_head = lax.broadcasted_iota(I32, (2 * c, PAIR), 0) // c
    same = lane_head == row_head
    rt = lax.broadcasted_iota(I32, (2 * c, PAIR), 0) % c
    ct = lax.broadcasted_iota(I32, (2 * c, PAIR), 1) % c
    strict = same & (ct < rt)
    incl = same & (ct <= rt)
    eye = jnp.where(same & (ct == rt), 1.0, 0.0).astype(F32)

    def bd(x):
        return jnp.where(same, jnp.concatenate([x, x], axis=1), 0.0)

    lane0 = lax.broadcasted_iota(I32, (c, PAIR), 1) < R_HEAD_DIM

    abar = bd(to_pairs(-kk * jnp.exp(cum - ld))).astype(BF16)
    rbar = bd(to_pairs(r * e_in)).astype(BF16)
    bt = bd(to_pairs(bb * e_neg)).astype(BF16)
    kt = bd(to_pairs(k2 * e_neg)).astype(BF16)
    bk = jnp.concatenate([bd(to_pairs(bb * e_out)), bd(to_pairs(k2 * e_out))], axis=1).astype(BF16)
    v_pl = to_pairs(v)
    vbd = bd(v_pl)
    decay_end = to_pairs(jnp.exp(cum_end))[:, 0:1, :]
    aa = _bdot_nt(jnp.concatenate([abar, rbar], axis=1), jnp.concatenate([bt, kt], axis=1))
    a_ab = jnp.where(strict, aa[:, :2 * c, :PAIR], 0.0)
    a_ak = jnp.where(strict, aa[:, :2 * c, PAIR:], 0.0).astype(BF16)
    a_rb = jnp.where(incl, aa[:, 2 * c:, :PAIR], 0.0).astype(BF16)
    a_rk = jnp.where(incl, aa[:, 2 * c:, PAIR:], 0.0).astype(BF16)
    pw = a_ab.astype(BF16)
    tinv = eye + a_ab
    for _ in range(int(math.log2(c)) - 1):
        pw = _bdot(pw, pw).astype(BF16)
        tinv = tinv + _bdot(tinv.astype(BF16), pw)
    akv = _bdot(a_ak, vbd.astype(BF16))
    wu = _bdot(tinv.astype(BF16), jnp.concatenate([abar, akv.astype(BF16)], axis=2))
    w16 = wu[:, :, :PAIR].astype(BF16)
    u_bd = wu[:, :, PAIR:]

    s_bd = state_ref[...]
    e_parts, rs_parts = [], []
    for ci in range(nc):
        sl = slice(ci * npair, (ci + 1) * npair)
        wr = _bdot_nt(jnp.concatenate([w16[sl], rbar[sl]], axis=1), s_bd.astype(BF16))
        e_bd = wr[:, :2 * c] + u_bd[sl]
        e_parts.append(e_bd)
        rs_parts.append(wr[:, 2 * c:])
        ev = jnp.concatenate([e_bd, vbd[sl]], axis=1)
        ds = _bdot(jnp.swapaxes(ev, 1, 2).astype(BF16), bk[sl])
        s_bd = s_bd * decay_end[sl] + jnp.where(same, ds, 0.0)
    state_ref[...] = s_bd

    ev_all = jnp.concatenate([jnp.concatenate(e_parts, axis=0), vbd], axis=1).astype(BF16)
    y_bd = jnp.concatenate(rs_parts, axis=0) + _bdot(jnp.concatenate([a_rb, a_rk], axis=2), ev_all)
    mean = jnp.sum(y_bd, axis=-1, keepdims=True) * (1.0 / R_HEAD_DIM)
    dev = jnp.where(same, y_bd - mean, 0.0)
    var = jnp.sum(dev * dev, axis=-1, keepdims=True) * (1.0 / R_HEAD_DIM)
    yn = dev * lax.rsqrt(var + GN_EPS)
    yn = yn[:, :c] + yn[:, c:]
    rkv = to_pairs(r * k2 * rk_ref[...])
    s0 = jnp.sum(jnp.where(lane0, rkv, 0.0), axis=-1, keepdims=True)
    s1 = jnp.sum(jnp.where(lane0, 0.0, rkv), axis=-1, keepdims=True)
    bonus = jnp.where(lane0, s0, s1) * v_pl
    def per_pair(row_ref):
        rows = [row_ref[:, p * PAIR:(p + 1) * PAIR] for p in range(npair)]
        return jnp.stack(rows * nc, axis=0)

    out = ((yn * per_pair(lnw_ref) + per_pair(lnb_ref) + bonus) * to_pairs(gate)).astype(BF16)
    for ci in range(nc):
        for p in range(npair):
            o_ref[0, ci * c:(ci + 1) * c, p * PAIR:(p + 1) * PAIR] = out[ci * npair + p]


def _rwkv(z, prm):
    b, s, _ = z.shape
    tl = RWKV_TILE

    def seg(width, off):
        return pl.BlockSpec((1, tl, width), lambda bi, i: (bi, i, off // width))

    def full(arr):
        return pl.BlockSpec(arr.shape, lambda bi, i: (0,) * arr.ndim)

    names = ("mu_r", "mu_k", "mu_v", "mu_g", "mu_w", "mu_a", "w0", "a0", "k_k", "k_a", "r_k", "ln_w", "ln_b",
             "w2", "a2", "g2")
    consts = [prm[n] for n in names]
    return pl.pallas_call(
        _rwkv_kernel,
        out_shape=jax.ShapeDtypeStruct((b, s, R_WIDTH), BF16),
        grid=(b, s // tl),
        in_specs=[seg(R_WIDTH, Z_R), seg(R_WIDTH, Z_K), seg(R_WIDTH, Z_V), seg(GATE_LORA, Z_GL),
                  seg(LORA_PAD, Z_WL), seg(LORA_PAD, Z_AL)] + [full(x) for x in consts],
        out_specs=pl.BlockSpec((1, tl, R_WIDTH), lambda bi, i: (bi, i, 0)),
        scratch_shapes=[
            pltpu.VMEM((1, R_WIDTH), F32), pltpu.VMEM((1, R_WIDTH), F32), pltpu.VMEM((1, R_WIDTH), F32),
            pltpu.VMEM((1, GATE_LORA), F32), pltpu.VMEM((1, LORA_PAD), F32), pltpu.VMEM((1, LORA_PAD), F32),
            pltpu.VMEM((R_WIDTH // PAIR, PAIR, PAIR), F32),
        ],
        compiler_params=_params("parallel", "arbitrary"),
        name="rwkv7",
    )(z, z, z, z, z, z, *consts)


def _pad_cols(w, width):
    return jnp.pad(w, ((0, 0), (0, width - w.shape[1])))


def _w_in_layout_kernel(wt_ref, o_ref):
    wt = wt_ref[0]
    cols = wt.shape[1]
    offs = [0]
    for n in (A_WIDTH, KV_RANK, IDX_HEADS * IDX_DIM, IDX_DIM + IDX_HEADS,
              3 * R_WIDTH, DECAY_LORA, AAA_LORA, GATE_LORA):
        offs.append(offs[-1] + n)
    q, ckv, qi, kiwi, rkv, wl, al, gl = (wt[offs[i]:offs[i + 1], :] for i in range(8))

    def padded(x, height):
        return jnp.concatenate([x, jnp.zeros((height - x.shape[0], cols), x.dtype)], axis=0)

    rows = [q, qi, rkv, gl, ckv, padded(kiwi, LORA_PAD), padded(wl, LORA_PAD), padded(al, LORA_PAD),
            jnp.zeros((Z_WIDTH - Z_END, cols), wt.dtype)]
    o_ref[0] = jnp.concatenate(rows, axis=0).T.astype(BF16)


def _layout_w_in(w_in):
    depth, d, p = w_in.shape
    tc = _divisor_tile(d, 256)
    return pl.pallas_call(
        _w_in_layout_kernel,
        out_shape=jax.ShapeDtypeStruct((depth, d, Z_WIDTH), BF16),
        grid=(depth, d // tc),
        in_specs=[pl.BlockSpec((1, p, tc), lambda l, i: (l, 0, i))],
        out_specs=pl.BlockSpec((1, tc, Z_WIDTH), lambda l, i: (l, i, 0)),
        compiler_params=_params("parallel", "parallel"),
        name="w_in_layout",
    )(jnp.swapaxes(w_in, 1, 2))


def _pad_rows(w, rows):
    return jnp.pad(w, ((0, rows - w.shape[0]), (0, 0)))


def _rwkv_params(l, mu, w0, w2, a0, a2, g2, k_k, k_a, r_k, ln_w, ln_b):
    m = mu[l]
    o = [0, R_WIDTH, 2 * R_WIDTH, 3 * R_WIDTH, 3 * R_WIDTH + DECAY_LORA, 3 * R_WIDTH + DECAY_LORA + AAA_LORA]
    row = lambda x: x.reshape(1, -1)
    return {
        "mu_r": row(m[o[0]:o[1]]), "mu_k": row(m[o[1]:o[2]]), "mu_v": row(m[o[2]:o[3]]),
        "mu_w": _pad_cols(row(m[o[3]:o[4]]), LORA_PAD), "mu_a": _pad_cols(row(m[o[4]:o[5]]), LORA_PAD),
        "mu_g": row(m[o[5]:]),
        "w0": row(w0[l]), "a0": row(a0[l]), "k_k": row(k_k[l]), "k_a": row(k_a[l]), "r_k": row(r_k[l]),
        "ln_w": row(ln_w[l]), "ln_b": row(ln_b[l]),
        "w2": _pad_rows(w2[l], LORA_PAD).astype(BF16), "a2": _pad_rows(a2[l], LORA_PAD).astype(BF16),
        "g2": g2[l].astype(BF16),
    }


def kernel(x, c, t5_bias, ada_w, ada_b, norm_g, ffn_w_in, ffn_w_out, w_in, ckv_norm_g, w_uk, w_uv, rwkv_mu, rwkv_w0, rwkv_w2, rwkv_a0, rwkv_a2, rwkv_g2, rwkv_k_k, rwkv_k_a, rwkv_r_k, rwkv_ln_w, rwkv_ln_b, w_out, final_norm_g):
    b, s, d = x.shape
    depth = ada_w.shape[0]
    assert s % DSA_TILE == 0 and s % RWKV_TILE == 0 and d == A_WIDTH + R_WIDTH
    mod = _ada_mod(c, ada_w, ada_b).reshape(depth, b, N_SUB, 3, 1, d)
    bias = _bias_tiles(t5_bias)
    final_g = final_norm_g.reshape(1, d)
    w_in_z = _layout_w_in(w_in)
    w_out16 = w_out.astype(BF16)
    h = x
    for l in range(depth):
        shift = lambda i: mod[l, :, i, 0]
        scale = lambda i: mod[l, :, i, 1]
        gate = lambda i: mod[l, :, i, 2]
        g = lambda i: norm_g[l, i].reshape(1, d)
        h = _ffn(h, g(0), shift(0), scale(0), gate(0), ffn_w_in, ffn_w_out, l, 0, final_g, final_norm=False)
        z = _proj(h, g(1), shift(1), scale(1), w_in_z, l)
        qt, qit, ckv, ckvt, ki, wit = _dsa_prep(z, ckv_norm_g[l].reshape(1, KV_RANK))
        wuk = jnp.transpose(w_uk[l], (1, 0, 2)).astype(BF16)
        wuvt = jnp.transpose(w_uv[l], (1, 2, 0)).astype(BF16)
        o_a = _dsa(qt, qit, ckv, ckvt, ki, wit, wuk, wuvt, bias)
        o_r = _rwkv(z, _rwkv_params(l, rwkv_mu, rwkv_w0, rwkv_w2, rwkv_a0, rwkv_a2, rwkv_g2, rwkv_k_k,
                                    rwkv_k_a, rwkv_r_k, rwkv_ln_w, rwkv_ln_b))
        h = _outproj(o_a, o_r, h, gate(1), w_out16, l)
        h = _ffn(h, g(2), shift(2), scale(2), gate(2), ffn_w_in, ffn_w_out, l, 1, final_g,
                 final_norm=(l == depth - 1))
    return h
```

```python
import functools
import math

import jax
import jax.numpy as jnp
from jax import lax
from jax.experimental import pallas as pl
from jax.experimental.pallas import tpu as pltpu

F32 = jnp.float32
BF16 = jnp.bfloat16
I32 = jnp.int32
I16 = jnp.int16

A_HEADS = 8
A_HEAD_DIM = 128
A_WIDTH = A_HEADS * A_HEAD_DIM
KV_RANK = 256
IDX_HEADS = 16
IDX_DIM = 64
TOPK_MAX = 256
REL_BUCKETS = 32
REL_MAX_EXACT = REL_BUCKETS // 2
REL_MAX_DIST = 128
R_HEAD_DIM = 64
R_WIDTH = 1024
DECAY_LORA = 96
AAA_LORA = 96
GATE_LORA = 256
GN_EPS = 64e-5
RMS_EPS = 1e-6
N_SUB = 3

LANES = 128
SUBLANES = 8
VMEM_LIMIT_BYTES = 56 * 1024 * 1024

LORA_PAD = 128
Z_Q = 0
Z_QI = Z_Q + A_WIDTH
Z_R = Z_QI + IDX_HEADS * IDX_DIM
Z_K = Z_R + R_WIDTH
Z_V = Z_K + R_WIDTH
Z_GL = Z_V + R_WIDTH
Z_CKV = Z_GL + GATE_LORA
Z_KIWI = Z_CKV + KV_RANK
Z_WL = Z_KIWI + LORA_PAD
Z_AL = Z_WL + LORA_PAD
Z_END = Z_AL + LORA_PAD
Z_WIDTH = 6144

FFN_TOKEN_TILE = 1024
FFN_HIDDEN_TILE = 256
ADALN_SLAB_ROWS = 16
ADALN_UNROLL = 8
DSA_TILE = 256
RWKV_CHUNK = 64
RWKV_TILE = 256
PAIR = 2 * R_HEAD_DIM

I16_MIN = -(2 ** 15)
PACKED_ROWS = 2 * SUBLANES
SUM_ROWS = PACKED_ROWS
BF16_EXACT_INT = 256
COUNT_CHAINS = 4
KEY_NEG_INF = -2139095041
MASKED_LOGIT = -(2.0 ** 100)
LOG2E = math.log2(math.e)
FAR_DISTANCE = math.ceil(REL_MAX_EXACT * (REL_MAX_DIST / REL_MAX_EXACT)
                         ** ((REL_BUCKETS - 1 - REL_MAX_EXACT) / (REL_BUCKETS - REL_MAX_EXACT)))


def _dot(a, b):
    return jnp.dot(a, b, preferred_element_type=F32)


def _bdot(a, b):
    return lax.dot_general(a, b, (((2,), (1,)), ((0,), (0,))), preferred_element_type=F32)


def _bdot_nt(a, b):
    return lax.dot_general(a, b, (((2,), (2,)), ((0,), (0,))), preferred_element_type=F32)


def _rms(x, g, eps):
    ms = jnp.mean(x * x, axis=-1, keepdims=True)
    return x * lax.rsqrt(ms + eps) * g


def _divisor_tile(n, pref):
    if n <= pref:
        return n
    t = (pref // LANES) * LANES
    while t > LANES and n % t:
        t -= LANES
    assert n % t == 0, (n, pref)
    return t


def _params(*sem):
    return pltpu.CompilerParams(dimension_semantics=sem, vmem_limit_bytes=VMEM_LIMIT_BYTES)


def _ada_kernel(c_ref, w_ref, b_ref, o_ref):
    c = c_ref[...]
    ca = (c * jax.nn.sigmoid(c)).astype(BF16)
    o_ref[0] = _dot(ca, w_ref[0].astype(BF16)) + b_ref[0]


def _ada_mod(c, ada_w, ada_b):
    depth, d, n = ada_w.shape
    b = c.shape[0]
    bp = -(-b // SUBLANES) * SUBLANES
    cp = jnp.pad(c, ((0, bp - b), (0, 0)))
    tn = _divisor_tile(n, 1024)
    out = pl.pallas_call(
        _ada_kernel,
        out_shape=jax.ShapeDtypeStruct((depth, bp, n), F32),
        grid=(depth, n // tn),
        in_specs=[
            pl.BlockSpec((bp, d), lambda l, j: (0, 0)),
            pl.BlockSpec((1, d, tn), lambda l, j: (l, 0, j)),
            pl.BlockSpec((1, 1, tn), lambda l, j: (l, 0, j)),
        ],
        out_specs=pl.BlockSpec((1, bp, tn), lambda l, j: (l, 0, j)),
        compiler_params=_params("arbitrary", "arbitrary"),
        name="ada_mod",
    )(cp, ada_w, ada_b.reshape(depth, 1, n))
    return out[:, :b]


def _adaln_rows(h_ref, g_ref, scale_ref, shift_ref, hn_ref, copy_ref=None):
    rows = h_ref.shape[1]
    g = g_ref[...]
    mul = 1.0 + scale_ref[0]
    add = shift_ref[0]

    def slab(i, carry):
        sl = pl.ds(pl.multiple_of(i * ADALN_SLAB_ROWS, ADALN_SLAB_ROWS), ADALN_SLAB_ROWS)
        h = h_ref[0, sl, :]
        hn_ref[sl, :] = (_rms(h, g, RMS_EPS) * mul + add).astype(BF16)
        if copy_ref is not None:
            copy_ref[0, sl, :] = h
        return carry

    lax.fori_loop(0, rows // ADALN_SLAB_ROWS, slab, 0, unroll=ADALN_UNROLL)


def _ffn_kernel(h_ref, g_ref, shift_ref, scale_ref, gate_ref, wg_ref, wu_ref, wo_ref, fg_ref,
                o_ref, hn_ref, *, final_norm):
    f = pl.program_id(2)

    @pl.when(f == 0)
    def _():
        _adaln_rows(h_ref, g_ref, scale_ref, shift_ref, hn_ref, o_ref)

    hn = hn_ref[...]
    g = _dot(hn, wg_ref[...].astype(BF16))
    u = _dot(hn, wu_ref[...].astype(BF16))
    act = (g * jax.nn.sigmoid(g) * u).astype(BF16)
    o_ref[0] += (0.5 * gate_ref[0]) * _dot(act, wo_ref[...].astype(BF16))

    if final_norm:
        @pl.when(f == pl.num_programs(2) - 1)
        def _():
            o_ref[0] = _rms(o_ref[0], fg_ref[...], RMS_EPS)


def _ffn(h, g, shift, scale, gate, w_in, w_out, layer, which, final_g, *, final_norm):
    b, s, d = h.shape
    ff = w_out.shape[2]
    tm = _divisor_tile(s, FFN_TOKEN_TILE)
    tf = _divisor_tile(ff, FFN_HIDDEN_TILE)
    nf = ff // tf
    vec = pl.BlockSpec((1, 1, d), lambda bi, i, f: (bi, 0, 0))
    row = pl.BlockSpec((1, d), lambda bi, i, f: (0, 0))
    tile = pl.BlockSpec((1, tm, d), lambda bi, i, f: (bi, i, 0))
    return pl.pallas_call(
        functools.partial(_ffn_kernel, final_norm=final_norm),
        out_shape=jax.ShapeDtypeStruct((b, s, d), F32),
        grid=(b, s // tm, nf),
        in_specs=[
            tile, row, vec, vec, vec,
            pl.BlockSpec((None, None, d, tf), lambda bi, i, f: (layer, which, 0, f)),
            pl.BlockSpec((None, None, d, tf), lambda bi, i, f: (layer, which, 0, nf + f)),
            pl.BlockSpec((None, None, tf, d), lambda bi, i, f: (layer, which, f, 0)),
            row,
        ],
        out_specs=tile,
        scratch_shapes=[pltpu.VMEM((tm, d), BF16)],
        compiler_params=_params("parallel", "parallel", "arbitrary"),
        name="ffn",
    )(h, g, shift, scale, gate, w_in, w_in, w_out, final_g)


def _proj_kernel(h_ref, g_ref, shift_ref, scale_ref, w_ref, o_ref, hn_ref):
    @pl.when(pl.program_id(2) == 0)
    def _():
        _adaln_rows(h_ref, g_ref, scale_ref, shift_ref, hn_ref)

    o_ref[0] = _dot(hn_ref[...], w_ref[...])


def _proj(h, g, shift, scale, w, layer):
    b, s, d = h.shape
    p = w.shape[2]
    tm = _divisor_tile(s, 1024)
    tn = _divisor_tile(p, 1024)
    vec = pl.BlockSpec((1, 1, d), lambda bi, i, n: (bi, 0, 0))
    return pl.pallas_call(
        _proj_kernel,
        out_shape=jax.ShapeDtypeStruct((b, s, p), F32),
        grid=(b, s // tm, p // tn),
        in_specs=[
            pl.BlockSpec((1, tm, d), lambda bi, i, n: (bi, i, 0)),
            pl.BlockSpec((1, d), lambda bi, i, n: (0, 0)),
            vec, vec,
            pl.BlockSpec((None, d, tn), lambda bi, i, n: (layer, 0, n)),
        ],
        out_specs=pl.BlockSpec((1, tm, tn), lambda bi, i, n: (bi, i, n)),
        scratch_shapes=[pltpu.VMEM((tm, d), BF16)],
        compiler_params=_params("parallel", "parallel", "arbitrary"),
        name="proj",
    )(h, g, shift, scale, w)


def _outproj_kernel(oa_ref, or_ref, h_ref, gate_ref, wa_ref, wr_ref, o_ref):
    acc = _dot(oa_ref[0], wa_ref[...]) + _dot(or_ref[0], wr_ref[...])
    o_ref[0] = h_ref[0] + gate_ref[0] * acc


def _outproj(o_a, o_r, h, gate, w_out, layer):
    b, s, d = h.shape
    tm = _divisor_tile(s, 512)
    tn = d
    wa = o_a.shape[-1]
    wr = o_r.shape[-1]
    return pl.pallas_call(
        _outproj_kernel,
        out_shape=jax.ShapeDtypeStruct((b, s, d), F32),
        grid=(b, s // tm, d // tn),
        in_specs=[
            pl.BlockSpec((1, tm, wa), lambda bi, i, n: (bi, i, 0)),
            pl.BlockSpec((1, tm, wr), lambda bi, i, n: (bi, i, 0)),
            pl.BlockSpec((1, tm, tn), lambda bi, i, n: (bi, i, n)),
            pl.BlockSpec((1, 1, tn), lambda bi, i, n: (bi, 0, n)),
            pl.BlockSpec((None, wa, tn), lambda bi, i, n: (layer, 0, n)),
            pl.BlockSpec((None, wr, tn), lambda bi, i, n: (layer, wa // wr, n)),
        ],
        out_specs=pl.BlockSpec((1, tm, tn), lambda bi, i, n: (bi, i, n)),
        compiler_params=_params("parallel", "parallel", "arbitrary"),
        name="outproj",
    )(o_a, o_r, h, gate, w_out, w_out)


def _dsa_prep_kernel(q_ref, qi_ref, ckv_ref, kiwi_ref, g_ref, qt_ref, qit_ref, ckv_o, ckvt_o, ki_o, wit_o):
    qt_ref[0] = q_ref[0].T.astype(BF16)
    qit_ref[0] = qi_ref[0].T.astype(BF16)
    cn = _rms(ckv_ref[0], g_ref[...], RMS_EPS)
    ckv_o[0] = cn.astype(BF16)
    ckvt_o[0, 0] = jnp.concatenate([cn.T.astype(BF16), jnp.ones((SUM_ROWS, cn.shape[0]), BF16)], axis=0)
    kw = kiwi_ref[0]
    ki_o[0] = kw[:, :IDX_DIM].astype(BF16)
    wit_o[0] = kw.T[IDX_DIM:IDX_DIM + IDX_HEADS, :] * (IDX_HEADS * IDX_DIM) ** -0.5


def _dsa_prep(z, ckv_g):
    b, s, _ = z.shape
    t = DSA_TILE
    nt = s // t

    def seg(width, off):
        return pl.BlockSpec((1, t, width), lambda bi, i: (bi, i, off // width))

    return pl.pallas_call(
        _dsa_prep_kernel,
        out_shape=(
            jax.ShapeDtypeStruct((b, A_WIDTH, s), BF16),
            jax.ShapeDtypeStruct((b, IDX_HEADS * IDX_DIM, s), BF16),
            jax.ShapeDtypeStruct((b, s, KV_RANK), BF16),
            jax.ShapeDtypeStruct((b, nt, KV_RANK + SUM_ROWS, t), BF16),
            jax.ShapeDtypeStruct((b, s, IDX_DIM), BF16),
            jax.ShapeDtypeStruct((b, IDX_HEADS, s), F32),
        ),
        grid=(b, nt),
        in_specs=[
            seg(A_WIDTH, Z_Q), seg(IDX_HEADS * IDX_DIM, Z_QI), seg(KV_RANK, Z_CKV), seg(LORA_PAD, Z_KIWI),
            pl.BlockSpec((1, KV_RANK), lambda bi, i: (0, 0)),
        ],
        out_specs=(
            pl.BlockSpec((1, A_WIDTH, t), lambda bi, i: (bi, 0, i)),
            pl.BlockSpec((1, IDX_HEADS * IDX_DIM, t), lambda bi, i: (bi, 0, i)),
            pl.BlockSpec((1, t, KV_RANK), lambda bi, i: (bi, i, 0)),
            pl.BlockSpec((1, 1, KV_RANK + SUM_ROWS, t), lambda bi, i: (bi, i, 0, 0)),
            pl.BlockSpec((1, t, IDX_DIM), lambda bi, i: (bi, i, 0)),
            pl.BlockSpec((1, IDX_HEADS, t), lambda bi, i: (bi, 0, i)),
        ),
        compiler_params=_params("parallel", "parallel"),
        name="dsa_prep",
    )(z, z, z, z, ckv_g)


def _bias_kernel(t5_ref, o_ref):
    t = DSA_TILE
    j = lax.broadcasted_iota(I32, (t, t), 0)
    i = lax.broadcasted_iota(I32, (t, t), 1)
    for didx in range(2):
        n = jnp.maximum(didx * t + i - j, 0)
        nf = jnp.maximum(n, 1).astype(F32)
        large = REL_MAX_EXACT + (jnp.log(nf / REL_MAX_EXACT) / math.log(REL_MAX_DIST / REL_MAX_EXACT)
                                 * (REL_BUCKETS - REL_MAX_EXACT)).astype(I32)
        large = jnp.minimum(large, REL_BUCKETS - 1)
        bucket = jnp.where(n < REL_MAX_EXACT, n, large)
        for h in range(A_HEADS):
            val = jnp.zeros((t, t), F32)
            for k in range(REL_BUCKETS):
                val = jnp.where(bucket == k, t5_ref[k, h], val)
            o_ref[(1 - didx) * t:(2 - didx) * t, h * t:(h + 1) * t] = (val - t5_ref[REL_BUCKETS - 1, h]) * LOG2E


def _bias_tiles(t5_bias):
    t = DSA_TILE
    assert t + 1 >= FAR_DISTANCE
    return pl.pallas_call(
        _bias_kernel,
        out_shape=jax.ShapeDtypeStruct((2 * t, A_HEADS * t), F32),
        in_specs=[pl.BlockSpec(memory_space=pltpu.SMEM)],
        out_specs=pl.BlockSpec(memory_space=pltpu.VMEM),
        compiler_params=pltpu.CompilerParams(vmem_limit_bytes=VMEM_LIMIT_BYTES),
        name="t5_bias_tiles",
    )(t5_bias)


def _dsa_kernel(qt_ref, qit_ref, ckv_ref, ckvt_ref, ki_ref, wit_ref, wuk_ref, wuvt_ref, bias_ref,
                o_ref, keys_ref, khi_ref, klo_ref, acc_ref, ot_ref, *, topk, seq_len):
    t = DSA_TILE
    qb = pl.program_id(1)
    nk = qb + 1
    row = lax.broadcasted_iota(I32, (t, t), 0)
    col = lax.broadcasted_iota(I32, (t, t), 1)

    def idx_rows(start, rows, diagonal):
        sl = pl.ds(pl.multiple_of(start, rows), rows)
        kic = ki_ref[0, sl, :]
        acc = jnp.zeros((rows, t), F32)
        for h in range(IDX_HEADS):
            rel = _dot(kic, qit_ref[0, h * IDX_DIM:(h + 1) * IDX_DIM, :])
            acc = acc + jnp.maximum(rel, 0.0) * wit_ref[0, h:h + 1, :]
        acc = acc + 0.0
        bits = pltpu.bitcast(acc, I32)
        key = bits ^ ((bits >> 31) & 0x7FFFFFFF)
        if diagonal:
            key = jnp.where(row <= col, key, KEY_NEG_INF)
        keys_ref[sl, :] = key
        khi_ref[sl, :] = (key >> 16).astype(I16)
        klo_ref[sl, :] = ((key & 0xFFFF) + I16_MIN).astype(I16)

    def idx_pair(i, carry):
        idx_rows(2 * i * t, 2 * t, False)
        return carry

    def idx_single(kc, carry):
        idx_rows(kc * t, t, False)
        return carry

    lax.fori_loop(0, qb // 2, idx_pair, 0)
    lax.fori_loop(2 * (qb // 2), qb, idx_single, 0)
    idx_rows(qb * t, t, True)

    def count16(ref, cand):
        cand16 = cand.astype(I16)

        def body(kc, accs):
            x = ref[pl.ds(pl.multiple_of(kc * t, t), t), :]
            ones = jnp.where(x >= cand16, jnp.asarray(1, BF16), jnp.asarray(0, BF16))
            ones = ones.reshape(t // PACKED_ROWS, PACKED_ROWS, t)
            accs = list(accs)
            for i in range(t // PACKED_ROWS):
                accs[i % COUNT_CHAINS] = accs[i % COUNT_CHAINS] + ones[i]
            return tuple(accs)

        accs = (jnp.zeros((PACKED_ROWS, t), BF16),) * COUNT_CHAINS
        accs = lax.fori_loop(0, nk // 2, lambda i, a: body(2 * i + 1, body(2 * i, a)), accs)
        accs = lax.fori_loop(2 * (nk // 2), nk, body, accs)
        total = sum(a.astype(F32) for a in accs)
        return jnp.sum(total, axis=0, keepdims=True).astype(I32)

    def search16(ref, base):
        def accept(cand, cur):
            return jnp.where(base + count16(ref, cand) >= topk, cand, cur)

        v = accept(jnp.zeros((1, t), I32), jnp.full((1, t), I16_MIN, I32))
        return lax.fori_loop(0, 15, lambda i, v: accept(v | (jnp.int32(1) << (14 - i)), v), v)

    hi = search16(khi_ref, 0)
    above = count16(khi_ref, jnp.minimum(hi + 1, -I16_MIN - 1))
    hi16 = hi.astype(I16)

    def keep_low(kc, carry):
        sl = pl.ds(pl.multiple_of(kc * t, t), t)
        klo_ref[sl, :] = jnp.where(khi_ref[sl, :] == hi16, klo_ref[sl, :], jnp.asarray(I16_MIN, I16))
        return carry

    lax.fori_loop(0, nk, keep_low, 0)
    lo = search16(klo_ref, above)
    thr = jnp.maximum((hi << 16) | (lo - I16_MIN), KEY_NEG_INF + 1)

    def count(pred):
        def body(kc, acc):
            k = keys_ref[pl.ds(pl.multiple_of(kc * t, t), t), :]
            m = jnp.where(pred(k, kc), 1, 0).astype(I32)
            return acc + jnp.sum(m.reshape(t // SUBLANES, SUBLANES, t), axis=0)

        acc = lax.fori_loop(0, nk, body, jnp.zeros((SUBLANES, t), I32))
        return jnp.sum(acc, axis=0, keepdims=True)

    @pl.when(jnp.max(count(lambda k, kc: k >= thr)) > topk)
    def _():
        need = topk - count(lambda k, kc: k > thr)
        nbits = (seq_len - 1).bit_length()

        def pos_bit(i, y):
            cand = y | (jnp.int32(1) << (nbits - 1 - i))
            before = count(lambda k, kc: (k == thr) & ((kc * t + row) < cand))
            return jnp.where(before < need, cand, y)

        last_kept = lax.fori_loop(0, nbits, pos_bit, jnp.zeros((1, t), I32))

        def demote(kc, carry):
            sl = pl.ds(pl.multiple_of(kc * t, t), t)
            k = keys_ref[sl, :]
            keys_ref[sl, :] = jnp.where((k == thr) & ((kc * t + row) > last_kept), k - 1, k)
            return carry

        lax.fori_loop(0, nk, demote, 0)

    scale = A_HEAD_DIM ** -0.5 * LOG2E

    hw = A_HEADS * t
    qlat = jnp.concatenate(
        [(_dot(wuk_ref[h], qt_ref[0, h * A_HEAD_DIM:(h + 1) * A_HEAD_DIM, :]) * scale).astype(BF16)
         for h in range(A_HEADS)], axis=1)
    acc_ref[...] = jnp.zeros_like(acc_ref)

    def att_chunks(kc, m, *, chunks, bias_row=None):
        rows = chunks * t
        sl = pl.ds(pl.multiple_of(kc * t, t), rows)
        raw = _dot(ckv_ref[0, sl, :], qlat)
        sel = keys_ref[sl, :] >= thr
        parts = []
        for h in range(A_HEADS):
            lg_h = raw[:, h * t:(h + 1) * t]
            if bias_row is not None:
                lg_h = lg_h + bias_ref[bias_row:bias_row + rows, h * t:(h + 1) * t]
            parts.append(jnp.where(sel, lg_h, MASKED_LOGIT).astype(BF16))
        lg = jnp.concatenate(parts, axis=1)
        m_new = jnp.maximum(m, jnp.max(lg, axis=0, keepdims=True).astype(F32))
        p = jnp.exp2(lg - m_new.astype(BF16))
        acc = acc_ref[...] * jnp.exp2(m - m_new)
        for j in range(chunks):
            acc = acc + _dot(ckvt_ref[0, kc + j], p[j * t:(j + 1) * t])
        acc_ref[...] = acc
        return m_new

    n_far = jnp.maximum(qb - 1, 0)
    has_prev = jnp.minimum(qb, 1)
    m = jnp.full((1, hw), MASKED_LOGIT, F32)
    m = lax.fori_loop(0, n_far // 2, lambda i, m: att_chunks(2 * i, m, chunks=2), m)
    m = lax.fori_loop(2 * (n_far // 2), n_far, functools.partial(att_chunks, chunks=1), m)
    m = lax.fori_loop(0, has_prev, lambda i, m: att_chunks(qb - 1, m, chunks=2, bias_row=0), m)
    lax.fori_loop(0, 1 - has_prev, lambda i, m: att_chunks(qb, m, chunks=1, bias_row=t), m)
    olat = (acc_ref[:KV_RANK, :] / acc_ref[KV_RANK:KV_RANK + 1, :]).astype(BF16)
    for h in range(A_HEADS):
        ot_ref[h * A_HEAD_DIM:(h + 1) * A_HEAD_DIM, :] = _dot(wuvt_ref[h], olat[:, h * t:(h + 1) * t])
    o_ref[0] = ot_ref[...].T.astype(BF16)


def _dsa(qt, qit, ckv, ckvt, ki, wit, wuk, wuvt, bias):
    b, _, s = qt.shape
    t = DSA_TILE
    nt = s // t
    topk = min(TOPK_MAX, s // 4)
    assert s // PACKED_ROWS <= BF16_EXACT_INT
    return pl.pallas_call(
        functools.partial(_dsa_kernel, topk=topk, seq_len=s),
        out_shape=jax.ShapeDtypeStruct((b, s, A_WIDTH), BF16),
        grid=(b, nt),
        in_specs=[
            pl.BlockSpec((1, A_WIDTH, t), lambda bi, i: (bi, 0, i)),
            pl.BlockSpec((1, IDX_HEADS * IDX_DIM, t), lambda bi, i: (bi, 0, i)),
            pl.BlockSpec((1, s, KV_RANK), lambda bi, i: (bi, 0, 0)),
            pl.BlockSpec((1, nt, KV_RANK + SUM_ROWS, t), lambda bi, i: (bi, 0, 0, 0)),
            pl.BlockSpec((1, s, IDX_DIM), lambda bi, i: (bi, 0, 0)),
            pl.BlockSpec((1, IDX_HEADS, t), lambda bi, i: (bi, 0, i)),
            pl.BlockSpec((A_HEADS, KV_RANK, A_HEAD_DIM), lambda bi, i: (0, 0, 0)),
            pl.BlockSpec((A_HEADS, A_HEAD_DIM, KV_RANK), lambda bi, i: (0, 0, 0)),
            pl.BlockSpec((2 * t, A_HEADS * t), lambda bi, i: (0, 0)),
        ],
        out_specs=pl.BlockSpec((1, t, A_WIDTH), lambda bi, i: (bi, i, 0)),
        scratch_shapes=[pltpu.VMEM((s, t), I32), pltpu.VMEM((s, t), I16), pltpu.VMEM((s, t), I16),
                        pltpu.VMEM((KV_RANK + SUM_ROWS, A_HEADS * t), F32), pltpu.VMEM((A_WIDTH, t), F32)],
        compiler_params=_params("parallel", "arbitrary"),
        name="dsa",
    )(qt, qit, ckv, ckvt, ki, wit, wuk, wuvt, bias)


def _split3(x):
    hi = x.astype(BF16)
    r1 = x - hi.astype(F32)
    mid = r1.astype(BF16)
    lo = (r1 - mid.astype(F32)).astype(BF16)
    return hi, mid, lo


def _rwkv_kernel(r_ref, k_ref, v_ref, gl_ref, wl_ref, al_ref,
                 mur_ref, muk_ref, muv_ref, mug_ref, muw_ref, mua_ref,
                 w0_ref, a0_ref, kk_ref, ka_ref, rk_ref, lnw_ref, lnb_ref,
                 w2_ref, a2_ref, g2_ref,
                 o_ref,
                 pr_ref, pk_ref, pv_ref, pg_ref, pw_ref, pa_ref, state_ref):
    tl = RWKV_TILE
    c = RWKV_CHUNK
    nc = tl // c
    npair = R_WIDTH // PAIR

    @pl.when(pl.program_id(1) == 0)
    def _():
        state_ref[...] = jnp.zeros_like(state_ref)
        for ref in (pr_ref, pk_ref, pv_ref, pg_ref, pw_ref, pa_ref):
            ref[...] = jnp.zeros_like(ref)

    def shift_mix(x_ref, prev_ref, mu_ref):
        x = x_ref[0]
        first = lax.broadcasted_iota(I32, x.shape, 0) == 0
        xprev = jnp.where(first, prev_ref[...], pltpu.roll(x, 1, axis=0))
        prev_ref[...] = x[tl - 1:tl, :]
        return x + (xprev - x) * mu_ref[...]

    r = shift_mix(r_ref, pr_ref, mur_ref)
    k = shift_mix(k_ref, pk_ref, muk_ref)
    v = shift_mix(v_ref, pv_ref, muv_ref)
    gl = shift_mix(gl_ref, pg_ref, mug_ref)
    wl = shift_mix(wl_ref, pw_ref, muw_ref)
    al = shift_mix(al_ref, pa_ref, mua_ref)

    w_lin = w0_ref[...] + _dot(jnp.tanh(wl).astype(BF16), w2_ref[...])
    nx = -w_lin
    softplus = jnp.maximum(nx, 0.0) + jnp.log(1.0 + jnp.exp(-jnp.abs(nx)))
    ld = -jnp.exp(-softplus - 0.5)
    a = jax.nn.sigmoid(a0_ref[...] + _dot(al.astype(BF16), a2_ref[...]))
    gate = _dot(jax.nn.sigmoid(gl).astype(BF16), g2_ref[...])

    li = lax.broadcasted_iota(I32, (LANES, LANES), 0) // R_HEAD_DIM
    lj = lax.broadcasted_iota(I32, (LANES, LANES), 1) // R_HEAD_DIM
    head_ones = jnp.where(li == lj, 1.0, 0.0).astype(BF16)

    def head_sum(x):
        parts = []
        for j in range(R_WIDTH // LANES):
            hi, mid, lo = _split3(x[:, j * LANES:(j + 1) * LANES])
            parts.append(_dot(hi, head_ones) + _dot(mid, head_ones) + _dot(lo, head_ones))
        return jnp.concatenate(parts, axis=-1)

    kk = k * kk_ref[...]
    kk = kk / jnp.maximum(jnp.sqrt(head_sum(kk * kk)), 1e-12)
    k2 = k * (1.0 + (a - 1.0) * ka_ref[...])
    bb = kk * a

    ti = lax.broadcasted_iota(I32, (tl, tl), 0)
    tj = lax.broadcasted_iota(I32, (tl, tl), 1)
    tri = jnp.where((ti // c == tj // c) & (tj <= ti), 1.0, 0.0).astype(BF16)
    hi, mid, lo = _split3(ld)
    cum = _dot(tri, hi) + _dot(tri, mid) + _dot(tri, lo)
    cum_end = jnp.broadcast_to(cum.reshape(nc, c, R_WIDTH)[:, c - 1:c, :], (nc, c, R_WIDTH)).reshape(tl, R_WIDTH)
    e_in = jnp.exp(cum)
    e_neg = jnp.exp(-cum)
    e_out = jnp.exp(cum_end - cum)

    nb = nc * npair

    def to_pairs(x):
        x3 = x.reshape(nc, c, R_WIDTH)
        xs = jnp.stack([x3[:, :, p * PAIR:(p + 1) * PAIR] for p in range(npair)], axis=1)
        return xs.reshape(nb, c, PAIR)

    lane_head = lax.broadcasted_iota(I32, (2 * c, PAIR), 1) // R_HEAD_DIM
    row_head = lax.broadcasted_iota(I32, (2 * c, PAIR), 0) // c
    same = lane_head == row_head
    rt = lax.broadcasted_iota(I32, (2 * c, PAIR), 0) % c
    ct = lax.broadcasted_iota(I32, (2 * c, PAIR), 1) % c
    strict = same & (ct < rt)
    incl = same & (ct <= rt)
    eye = jnp.where(same & (ct == rt), 1.0, 0.0).astype(F32)

    def bd(x):
        return jnp.where(same, jnp.concatenate([x, x], axis=1), 0.0)

    lane0 = lax.broadcasted_iota(I32, (c, PAIR), 1) < R_HEAD_DIM

    abar = bd(to_pairs(-kk * jnp.exp(cum - ld))).astype(BF16)
    rbar = bd(to_pairs(r * e_in)).astype(BF16)
    bt = bd(to_pairs(bb * e_neg)).astype(BF16)
    kt = bd(to_pairs(k2 * e_neg)).astype(BF16)
    bk = jnp.concatenate([bd(to_pairs(bb * e_out)), bd(to_pairs(k2 * e_out))], axis=1).astype(BF16)
    v_pl = to_pairs(v)
    vbd = bd(v_pl)
    decay_end = to_pairs(jnp.exp(cum_end))[:, 0:1, :]
    aa = _bdot_nt(jnp.concatenate([abar, rbar], axis=1), jnp.concatenate([bt, kt], axis=1))
    a_ab = jnp.where(strict, aa[:, :2 * c, :PAIR], 0.0)
    a_ak = jnp.where(strict, aa[:, :2 * c, PAIR:], 0.0).astype(BF16)
    a_rb = jnp.where(incl, aa[:, 2 * c:, :PAIR], 0.0).astype(BF16)
    a_rk = jnp.where(incl, aa[:, 2 * c:, PAIR:], 0.0).astype(BF16)
    pw = a_ab.astype(BF16)
    tinv = eye + a_ab
    for _ in range(int(math.log2(c)) - 1):
        pw = _bdot(pw, pw).astype(BF16)
        tinv = tinv + _bdot(tinv.astype(BF16), pw)
    akv = _bdot(a_ak, vbd.astype(BF16))
    wu = _bdot(tinv.astype(BF16), jnp.concatenate([abar, akv.astype(BF16)], axis=2))
    w16 = wu[:, :, :PAIR].astype(BF16)
    u_bd = wu[:, :, PAIR:]

    s_bd = state_ref[...]
    e_parts, rs_parts = [], []
    for ci in range(nc):
        sl = slice(ci * npair, (ci + 1) * npair)
        wr = _bdot_nt(jnp.concatenate([w16[sl], rbar[sl]], axis=1), s_bd.astype(BF16))
        e_bd = wr[:, :2 * c] + u_bd[sl]
        e_parts.append(e_bd)
        rs_parts.append(wr[:, 2 * c:])
        ev = jnp.concatenate([e_bd, vbd[sl]], axis=1)
        ds = _bdot(jnp.swapaxes(ev, 1, 2).astype(BF16), bk[sl])
        s_bd = s_bd * decay_end[sl] + jnp.where(same, ds, 0.0)
    state_ref[...] = s_bd

    ev_all = jnp.concatenate([jnp.concatenate(e_parts, axis=0), vbd], axis=1).astype(BF16)
    y_bd = jnp.concatenate(rs_parts, axis=0) + _bdot(jnp.concatenate([a_rb, a_rk], axis=2), ev_all)
    mean = jnp.sum(y_bd, axis=-1, keepdims=True) * (1.0 / R_HEAD_DIM)
    dev = jnp.where(same, y_bd - mean, 0.0)
    var = jnp.sum(dev * dev, axis=-1, keepdims=True) * (1.0 / R_HEAD_DIM)
    yn = dev * lax.rsqrt(var + GN_EPS)
    yn = yn[:, :c] + yn[:, c:]
    rkv = to_pairs(r * k2 * rk_ref[...])
    s0 = jnp.sum(jnp.where(lane0, rkv, 0.0), axis=-1, keepdims=True)
    s1 = jnp.sum(jnp.where(lane0, 0.0, rkv), axis=-1, keepdims=True)
    bonus = jnp.where(lane0, s0, s1) * v_pl
    def per_pair(row_ref):
        rows = [row_ref[:, p * PAIR:(p + 1) * PAIR] for p in range(npair)]
        return jnp.stack(rows * nc, axis=0)

    out = ((yn * per_pair(lnw_ref) + per_pair(lnb_ref) + bonus) * to_pairs(gate)).astype(BF16)
    for ci in range(nc):
        for p in range(npair):
            o_ref[0, ci * c:(ci + 1) * c, p * PAIR:(p + 1) * PAIR] = out[ci * npair + p]


def _rwkv(z, prm):
    b, s, _ = z.shape
    tl = RWKV_TILE

    def seg(width, off):
        return pl.BlockSpec((1, tl, width), lambda bi, i: (bi, i, off // width))

    def full(arr):
        return pl.BlockSpec(arr.shape, lambda bi, i: (0,) * arr.ndim)

    names = ("mu_r", "mu_k", "mu_v", "mu_g", "mu_w", "mu_a", "w0", "a0", "k_k", "k_a", "r_k", "ln_w", "ln_b",
             "w2", "a2", "g2")
    consts = [prm[n] for n in names]
    return pl.pallas_call(
        _rwkv_kernel,
        out_shape=jax.ShapeDtypeStruct((b, s, R_WIDTH), BF16),
        grid=(b, s // tl),
        in_specs=[seg(R_WIDTH, Z_R), seg(R_WIDTH, Z_K), seg(R_WIDTH, Z_V), seg(GATE_LORA, Z_GL),
                  seg(LORA_PAD, Z_WL), seg(LORA_PAD, Z_AL)] + [full(x) for x in consts],
        out_specs=pl.BlockSpec((1, tl, R_WIDTH), lambda bi, i: (bi, i, 0)),
        scratch_shapes=[
            pltpu.VMEM((1, R_WIDTH), F32), pltpu.VMEM((1, R_WIDTH), F32), pltpu.VMEM((1, R_WIDTH), F32),
            pltpu.VMEM((1, GATE_LORA), F32), pltpu.VMEM((1, LORA_PAD), F32), pltpu.VMEM((1, LORA_PAD), F32),
            pltpu.VMEM((R_WIDTH // PAIR, PAIR, PAIR), F32),
        ],
        compiler_params=_params("parallel", "arbitrary"),
        name="rwkv7",
    )(z, z, z, z, z, z, *consts)


def _pad_cols(w, width):
    return jnp.pad(w, ((0, 0), (0, width - w.shape[1])))


def _w_in_layout_kernel(wt_ref, o_ref):
    wt = wt_ref[0]
    cols = wt.shape[1]
    offs = [0]
    for n in (A_WIDTH, KV_RANK, IDX_HEADS * IDX_DIM, IDX_DIM + IDX_HEADS,
              3 * R_WIDTH, DECAY_LORA, AAA_LORA, GATE_LORA):
        offs.append(offs[-1] + n)
    q, ckv, qi, kiwi, rkv, wl, al, gl = (wt[offs[i]:offs[i + 1], :] for i in range(8))

    def padded(x, height):
        return jnp.concatenate([x, jnp.zeros((height - x.shape[0], cols), x.dtype)], axis=0)

    rows = [q, qi, rkv, gl, ckv, padded(kiwi, LORA_PAD), padded(wl, LORA_PAD), padded(al, LORA_PAD),
            jnp.zeros((Z_WIDTH - Z_END, cols), wt.dtype)]
    o_ref[0] = jnp.concatenate(rows, axis=0).T.astype(BF16)


def _layout_w_in(w_in):
    depth, d, p = w_in.shape
    tc = _divisor_tile(d, 256)
    return pl.pallas_call(
        _w_in_layout_kernel,
        out_shape=jax.ShapeDtypeStruct((depth, d, Z_WIDTH), BF16),
        grid=(depth, d // tc),
        in_specs=[pl.BlockSpec((1, p, tc), lambda l, i: (l, 0, i))],
        out_specs=pl.BlockSpec((1, tc, Z_WIDTH), lambda l, i: (l, i, 0)),
        compiler_params=_params("parallel", "parallel"),
        name="w_in_layout",
    )(jnp.swapaxes(w_in, 1, 2))


def _pad_rows(w, rows):
    return jnp.pad(w, ((0, rows - w.shape[0]), (0, 0)))


def _rwkv_params(l, mu, w0, w2, a0, a2, g2, k_k, k_a, r_k, ln_w, ln_b):
    m = mu[l]
    o = [0, R_WIDTH, 2 * R_WIDTH, 3 * R_WIDTH, 3 * R_WIDTH + DECAY_LORA, 3 * R_WIDTH + DECAY_LORA + AAA_LORA]
    row = lambda x: x.reshape(1, -1)
    return {
        "mu_r": row(m[o[0]:o[1]]), "mu_k": row(m[o[1]:o[2]]), "mu_v": row(m[o[2]:o[3]]),
        "mu_w": _pad_cols(row(m[o[3]:o[4]]), LORA_PAD), "mu_a": _pad_cols(row(m[o[4]:o[5]]), LORA_PAD),
        "mu_g": row(m[o[5]:]),
        "w0": row(w0[l]), "a0": row(a0[l]), "k_k": row(k_k[l]), "k_a": row(k_a[l]), "r_k": row(r_k[l]),
        "ln_w": row(ln_w[l]), "ln_b": row(ln_b[l]),
        "w2": _pad_rows(w2[l], LORA_PAD).astype(BF16), "a2": _pad_rows(a2[l], LORA_PAD).astype(BF16),
        "g2": g2[l].astype(BF16),
    }


def kernel(x, c, t5_bias, ada_w, ada_b, norm_g, ffn_w_in, ffn_w_out, w_in, ckv_norm_g, w_uk, w_uv, rwkv_mu, rwkv_w0, rwkv_w2, rwkv_a0, rwkv_a2, rwkv_g2, rwkv_k_k, rwkv_k_a, rwkv_r_k, rwkv_ln_w, rwkv_ln_b, w_out, final_norm_g):
    b, s, d = x.shape
    depth = ada_w.shape[0]
    assert s % DSA_TILE == 0 and s % RWKV_TILE == 0 and d == A_WIDTH + R_WIDTH
    mod = _ada_mod(c, ada_w, ada_b).reshape(depth, b, N_SUB, 3, 1, d)
    bias = _bias_tiles(t5_bias)
    final_g = final_norm_g.reshape(1, d)
    w_in_z = _layout_w_in(w_in)
    w_out16 = w_out.astype(BF16)
    h = x
    for l in range(depth):
        shift = lambda i: mod[l, :, i, 0]
        scale = lambda i: mod[l, :, i, 1]
        gate = lambda i: mod[l, :, i, 2]
        g = lambda i: norm_g[l, i].reshape(1, d)
        h = _ffn(h, g(0), shift(0), scale(0), gate(0), ffn_w_in, ffn_w_out, l, 0, final_g, final_norm=False)
        z = _proj(h, g(1), shift(1), scale(1), w_in_z, l)
        qt, qit, ckv, ckvt, ki, wit = _dsa_prep(z, ckv_norm_g[l].reshape(1, KV_RANK))
        wuk = jnp.transpose(w_uk[l], (1, 0, 2)).astype(BF16)
        wuvt = jnp.transpose(w_uv[l], (1, 2, 0)).astype(BF16)
        o_a = _dsa(qt, qit, ckv, ckvt, ki, wit, wuk, wuvt, bias)
        o_r = _rwkv(z, _rwkv_params(l, rwkv_mu, rwkv_w0, rwkv_w2, rwkv_a0, rwkv_a2, rwkv_g2, rwkv_k_k,
                                    rwkv_k_a, rwkv_r_k, rwkv_ln_w, rwkv_ln_b))
        h = _outproj(o_a, o_r, h, gate(1), w_out16, l)
        h = _ffn(h, g(2), shift(2), scale(2), gate(2), ffn_w_in, ffn_w_out, l, 1, final_g,
                 final_norm=(l == depth - 1))
    return h
```

```python
import functools
import math

import jax
import jax.numpy as jnp
from jax import lax
from jax.experimental import pallas as pl
from jax.experimental.pallas import tpu as pltpu

F32 = jnp.float32
BF16 = jnp.bfloat16
I32 = jnp.int32
I16 = jnp.int16

A_HEADS = 8
A_HEAD_DIM = 128
A_WIDTH = A_HEADS * A_HEAD_DIM
KV_RANK = 256
IDX_HEADS = 16
IDX_DIM = 64
TOPK_MAX = 256
REL_BUCKETS = 32
REL_MAX_EXACT = REL_BUCKETS // 2
REL_MAX_DIST = 128
R_HEAD_DIM = 64
R_WIDTH = 1024
DECAY_LORA = 96
AAA_LORA = 96
GATE_LORA = 256
GN_EPS = 64e-5
RMS_EPS = 1e-6
N_SUB = 3

LANES = 128
SUBLANES = 8
VMEM_LIMIT_BYTES = 56 * 1024 * 1024

LORA_PAD = 128
Z_Q = 0
Z_QI = Z_Q + A_WIDTH
Z_R = Z_QI + IDX_HEADS * IDX_DIM
Z_K = Z_R + R_WIDTH
Z_V = Z_K + R_WIDTH
Z_GL = Z_V + R_WIDTH
Z_CKV = Z_GL + GATE_LORA
Z_KIWI = Z_CKV + KV_RANK
Z_WL = Z_KIWI + LORA_PAD
Z_AL = Z_WL + LORA_PAD
Z_END = Z_AL + LORA_PAD
Z_WIDTH = 6144

FFN_TOKEN_TILE = 1024
FFN_HIDDEN_TILE = 256
ADALN_SLAB_ROWS = 16
ADALN_UNROLL = 8
DSA_TILE = 256
RWKV_CHUNK = 64
RWKV_TILE = 256
PAIR = 2 * R_HEAD_DIM

I16_MIN = -(2 ** 15)
PACKED_ROWS = 2 * SUBLANES
SUM_ROWS = PACKED_ROWS
BF16_EXACT_INT = 256
COUNT_CHAINS = 4
KEY_NEG_INF = -2139095041
MASKED_LOGIT = -(2.0 ** 100)
LOG2E = math.log2(math.e)
FAR_DISTANCE = math.ceil(REL_MAX_EXACT * (REL_MAX_DIST / REL_MAX_EXACT)
                         ** ((REL_BUCKETS - 1 - REL_MAX_EXACT) / (REL_BUCKETS - REL_MAX_EXACT)))


def _dot(a, b):
    return jnp.dot(a, b, preferred_element_type=F32)


def _bdot(a, b):
    return lax.dot_general(a, b, (((2,), (1,)), ((0,), (0,))), preferred_element_type=F32)


def _bdot_nt(a, b):
    return lax.dot_general(a, b, (((2,), (2,)), ((0,), (0,))), preferred_element_type=F32)


def _rms(x, g, eps):
    ms = jnp.mean(x * x, axis=-1, keepdims=True)
    return x * lax.rsqrt(ms + eps) * g


def _divisor_tile(n, pref):
    if n <= pref:
        return n
    t = (pref // LANES) * LANES
    while t > LANES and n % t:
        t -= LANES
    assert n % t == 0, (n, pref)
    return t


def _params(*sem):
    return pltpu.CompilerParams(dimension_semantics=sem, vmem_limit_bytes=VMEM_LIMIT_BYTES)


def _ada_kernel(c_ref, w_ref, b_ref, o_ref):
    c = c_ref[...]
    ca = (c * jax.nn.sigmoid(c)).astype(BF16)
    o_ref[0] = _dot(ca, w_ref[0].astype(BF16)) + b_ref[0]


def _ada_mod(c, ada_w, ada_b):
    depth, d, n = ada_w.shape
    b = c.shape[0]
    bp = -(-b // SUBLANES) * SUBLANES
    cp = jnp.pad(c, ((0, bp - b), (0, 0)))
    tn = _divisor_tile(n, 1024)
    out = pl.pallas_call(
        _ada_kernel,
        out_shape=jax.ShapeDtypeStruct((depth, bp, n), F32),
        grid=(depth, n // tn),
        in_specs=[
            pl.BlockSpec((bp, d), lambda l, j: (0, 0)),
            pl.BlockSpec((1, d, tn), lambda l, j: (l, 0, j)),
            pl.BlockSpec((1, 1, tn), lambda l, j: (l, 0, j)),
        ],
        out_specs=pl.BlockSpec((1, bp, tn), lambda l, j: (l, 0, j)),
        compiler_params=_params("arbitrary", "arbitrary"),
        name="ada_mod",
    )(cp, ada_w, ada_b.reshape(depth, 1, n))
    return out[:, :b]


def _adaln_rows(h_ref, g_ref, scale_ref, shift_ref, hn_ref, copy_ref=None):
    rows = h_ref.shape[1]
    g = g_ref[...]
    mul = 1.0 + scale_ref[0]
    add = shift_ref[0]

    def slab(i, carry):
        sl = pl.ds(pl.multiple_of(i * ADALN_SLAB_ROWS, ADALN_SLAB_ROWS), ADALN_SLAB_ROWS)
        h = h_ref[0, sl, :]
        hn_ref[sl, :] = (_rms(h, g, RMS_EPS) * mul + add).astype(BF16)
        if copy_ref is not None:
            copy_ref[0, sl, :] = h
        return carry

    lax.fori_loop(0, rows // ADALN_SLAB_ROWS, slab, 0, unroll=ADALN_UNROLL)


def _ffn_kernel(h_ref, g_ref, shift_ref, scale_ref, gate_ref, wg_ref, wu_ref, wo_ref, fg_ref,
                o_ref, hn_ref, *, final_norm):
    f = pl.program_id(2)

    @pl.when(f == 0)
    def _():
        _adaln_rows(h_ref, g_ref, scale_ref, shift_ref, hn_ref, o_ref)

    hn = hn_ref[...]
    g = _dot(hn, wg_ref[...].astype(BF16))
    u = _dot(hn, wu_ref[...].astype(BF16))
    act = (g * jax.nn.sigmoid(g) * u).astype(BF16)
    o_ref[0] += (0.5 * gate_ref[0]) * _dot(act, wo_ref[...].astype(BF16))

    if final_norm:
        @pl.when(f == pl.num_programs(2) - 1)
        def _():
            o_ref[0] = _rms(o_ref[0], fg_ref[...], RMS_EPS)


def _ffn(h, g, shift, scale, gate, w_in, w_out, layer, which, final_g, *, final_norm):
    b, s, d = h.shape
    ff = w_out.shape[2]
    tm = _divisor_tile(s, FFN_TOKEN_TILE)
    tf = _divisor_tile(ff, FFN_HIDDEN_TILE)
    nf = ff // tf
    vec = pl.BlockSpec((1, 1, d), lambda bi, i, f: (bi, 0, 0))
    row = pl.BlockSpec((1, d), lambda bi, i, f: (0, 0))
    tile = pl.BlockSpec((1, tm, d), lambda bi, i, f: (bi, i, 0))
    return pl.pallas_call(
        functools.partial(_ffn_kernel, final_norm=final_norm),
        out_shape=jax.ShapeDtypeStruct((b, s, d), F32),
        grid=(b, s // tm, nf),
        in_specs=[
            tile, row, vec, vec, vec,
            pl.BlockSpec((None, None, d, tf), lambda bi, i, f: (layer, which, 0, f)),
            pl.BlockSpec((None, None, d, tf), lambda bi, i, f: (layer, which, 0, nf + f)),
            pl.BlockSpec((None, None, tf, d), lambda bi, i, f: (layer, which, f, 0)),
            row,
        ],
        out_specs=tile,
        scratch_shapes=[pltpu.VMEM((tm, d), BF16)],
        compiler_params=_params("parallel", "parallel", "arbitrary"),
        name="ffn",
    )(h, g, shift, scale, gate, w_in, w_in, w_out, final_g)


def _proj_kernel(h_ref, g_ref, shift_ref, scale_ref, w_ref, o_ref, hn_ref):
    @pl.when(pl.program_id(2) == 0)
    def _():
        _adaln_rows(h_ref, g_ref, scale_ref, shift_ref, hn_ref)

    o_ref[0] = _dot(hn_ref[...], w_ref[...])


def _proj(h, g, shift, scale, w, layer):
    b, s, d = h.shape
    p = w.shape[2]
    tm = _divisor_tile(s, 1024)
    tn = _divisor_tile(p, 1024)
    vec = pl.BlockSpec((1, 1, d), lambda bi, i, n: (bi, 0, 0))
    return pl.pallas_call(
        _proj_kernel,
        out_shape=jax.ShapeDtypeStruct((b, s, p), F32),
        grid=(b, s // tm, p // tn),
        in_specs=[
            pl.BlockSpec((1, tm, d), lambda bi, i, n: (bi, i, 0)),
            pl.BlockSpec((1, d), lambda bi, i, n: (0, 0)),
            vec, vec,
            pl.BlockSpec((None, d, tn), lambda bi, i, n: (layer, 0, n)),
        ],
        out_specs=pl.BlockSpec((1, tm, tn), lambda bi, i, n: (bi, i, n)),
        scratch_shapes=[pltpu.VMEM((tm, d), BF16)],
        compiler_params=_params("parallel", "parallel", "arbitrary"),
        name="proj",
    )(h, g, shift, scale, w)


def _outproj_kernel(oa_ref, or_ref, h_ref, gate_ref, wa_ref, wr_ref, o_ref):
    acc = _dot(oa_ref[0], wa_ref[...]) + _dot(or_ref[0], wr_ref[...])
    o_ref[0] = h_ref[0] + gate_ref[0] * acc


def _outproj(o_a, o_r, h, gate, w_out, layer):
    b, s, d = h.shape
    tm = _divisor_tile(s, 512)
    tn = d
    wa = o_a.shape[-1]
    wr = o_r.shape[-1]
    return pl.pallas_call(
        _outproj_kernel,
        out_shape=jax.ShapeDtypeStruct((b, s, d), F32),
        grid=(b, s // tm, d // tn),
        in_specs=[
            pl.BlockSpec((1, tm, wa), lambda bi, i, n: (bi, i, 0)),
            pl.BlockSpec((1, tm, wr), lambda bi, i, n: (bi, i, 0)),
            pl.BlockSpec((1, tm, tn), lambda bi, i, n: (bi, i, n)),
            pl.BlockSpec((1, 1, tn), lambda bi, i, n: (bi, 0, n)),
            pl.BlockSpec((None, wa, tn), lambda bi, i, n: (layer, 0, n)),
            pl.BlockSpec((None, wr, tn), lambda bi, i, n: (layer, wa // wr, n)),
        ],
        out_specs=pl.BlockSpec((1, tm, tn), lambda bi, i, n: (bi, i, n)),
        compiler_params=_params("parallel", "parallel", "arbitrary"),
        name="outproj",
    )(o_a, o_r, h, gate, w_out, w_out)


def _dsa_prep_kernel(q_ref, qi_ref, ckv_ref, kiwi_ref, g_ref, qt_ref, qit_ref, ckv_o, ckvt_o, ki_o, wit_o):
    qt_ref[0] = q_ref[0].T.astype(BF16)
    qit_ref[0] = qi_ref[0].T.astype(BF16)
    cn = _rms(ckv_ref[0], g_ref[...], RMS_EPS)
    ckv_o[0] = cn.astype(BF16)
    ckvt_o[0, 0] = jnp.concatenate([cn.T.astype(BF16), jnp.ones((SUM_ROWS, cn.shape[0]), BF16)], axis=0)
    kw = kiwi_ref[0]
    ki_o[0] = kw[:, :IDX_DIM].astype(BF16)
    wit_o[0] = kw.T[IDX_DIM:IDX_DIM + IDX_HEADS, :] * (IDX_HEADS * IDX_DIM) ** -0.5


def _dsa_prep(z, ckv_g):
    b, s, _ = z.shape
    t = DSA_TILE
    nt = s // t

    def seg(width, off):
        return pl.BlockSpec((1, t, width), lambda bi, i: (bi, i, off // width))

    return pl.pallas_call(
        _dsa_prep_kernel,
        out_shape=(
            jax.ShapeDtypeStruct((b, A_WIDTH, s), BF16),
            jax.ShapeDtypeStruct((b, IDX_HEADS * IDX_DIM, s), BF16),
            jax.ShapeDtypeStruct((b, s, KV_RANK), BF16),
            jax.ShapeDtypeStruct((b, nt, KV_RANK + SUM_ROWS, t), BF16),
            jax.ShapeDtypeStruct((b, s, IDX_DIM), BF16),
            jax.ShapeDtypeStruct((b, IDX_HEADS, s), F32),
        ),
        grid=(b, nt),
        in_specs=[
            seg(A_WIDTH, Z_Q), seg(IDX_HEADS * IDX_DIM, Z_QI), seg(KV_RANK, Z_CKV), seg(LORA_PAD, Z_KIWI),
            pl.BlockSpec((1, KV_RANK), lambda bi, i: (0, 0)),
        ],
        out_specs=(
            pl.BlockSpec((1, A_WIDTH, t), lambda bi, i: (bi, 0, i)),
            pl.BlockSpec((1, IDX_HEADS * IDX_DIM, t), lambda bi, i: (bi, 0, i)),
            pl.BlockSpec((1, t, KV_RANK), lambda bi, i: (bi, i, 0)),
            pl.BlockSpec((1, 1, KV_RANK + SUM_ROWS, t), lambda bi, i: (bi, i, 0, 0)),
            pl.BlockSpec((1, t, IDX_DIM), lambda bi, i: (bi, i, 0)),
            pl.BlockSpec((1, IDX_HEADS, t), lambda bi, i: (bi, 0, i)),
        ),
        compiler_params=_params("parallel", "parallel"),
        name="dsa_prep",
    )(z, z, z, z, ckv_g)


def _bias_kernel(t5_ref, o_ref):
    t = DSA_TILE
    j = lax.broadcasted_iota(I32, (t, t), 0)
    i = lax.broadcasted_iota(I32, (t, t), 1)
    for didx in range(2):
        n = jnp.maximum(didx * t + i - j, 0)
        nf = jnp.maximum(n, 1).astype(F32)
        large = REL_MAX_EXACT + (jnp.log(nf / REL_MAX_EXACT) / math.log(REL_MAX_DIST / REL_MAX_EXACT)
                                 * (REL_BUCKETS - REL_MAX_EXACT)).astype(I32)
        large = jnp.minimum(large, REL_BUCKETS - 1)
        bucket = jnp.where(n < REL_MAX_EXACT, n, large)
        for h in range(A_HEADS):
            val = jnp.zeros((t, t), F32)
            for k in range(REL_BUCKETS):
                val = jnp.where(bucket == k, t5_ref[k, h], val)
            o_ref[(1 - didx) * t:(2 - didx) * t, h * t:(h + 1) * t] = (val - t5_ref[REL_BUCKETS - 1, h]) * LOG2E


def _bias_tiles(t5_bias):
    t = DSA_TILE
    assert t + 1 >= FAR_DISTANCE
    return pl.pallas_call(
        _bias_kernel,
        out_shape=jax.ShapeDtypeStruct((2 * t, A_HEADS * t), F32),
        in_specs=[pl.BlockSpec(memory_space=pltpu.SMEM)],
        out_specs=pl.BlockSpec(memory_space=pltpu.VMEM),
        compiler_params=pltpu.CompilerParams(vmem_limit_bytes=VMEM_LIMIT_BYTES),
        name="t5_bias_tiles",
    )(t5_bias)


def _dsa_kernel(qt_ref, qit_ref, ckv_ref, ckvt_ref, ki_ref, wit_ref, wuk_ref, wuvt_ref, bias_ref,
                o_ref, keys_ref, khi_ref, klo_ref, acc_ref, ot_ref, *, topk, seq_len):
    t = DSA_TILE
    qb = pl.program_id(1)
    nk = qb + 1
    row = lax.broadcasted_iota(I32, (t, t), 0)
    col = lax.broadcasted_iota(I32, (t, t), 1)

    def idx_rows(start, rows, diagonal):
        sl = pl.ds(pl.multiple_of(start, rows), rows)
        kic = ki_ref[0, sl, :]
        acc = jnp.zeros((rows, t), F32)
        for h in range(IDX_HEADS):
            rel = _dot(kic, qit_ref[0, h * IDX_DIM:(h + 1) * IDX_DIM, :])
            acc = acc + jnp.maximum(rel, 0.0) * wit_ref[0, h:h + 1, :]
        acc = acc + 0.0
        bits = pltpu.bitcast(acc, I32)
        key = bits ^ ((bits >> 31) & 0x7FFFFFFF)
        if diagonal:
            key = jnp.where(row <= col, key, KEY_NEG_INF)
        keys_ref[sl, :] = key
        khi_ref[sl, :] = (key >> 16).astype(I16)
        klo_ref[sl, :] = ((key & 0xFFFF) + I16_MIN).astype(I16)

    def idx_group(chunks):
        def body(i, first_chunk):
            idx_rows(first_chunk * t, chunks * t, False)
            return first_chunk + chunks
        return body

    done = lax.fori_loop(0, qb // 4, idx_group(4), 0)
    done = lax.fori_loop(0, (qb // 2) % 2, idx_group(2), done)
    lax.fori_loop(0, qb % 2, idx_group(1), done)
    idx_rows(qb * t, t, True)

    def count16(ref, cand):
        cand16 = cand.astype(I16)

        def body(kc, accs):
            x = ref[pl.ds(pl.multiple_of(kc * t, t), t), :]
            ones = jnp.where(x >= cand16, jnp.asarray(1, BF16), jnp.asarray(0, BF16))
            ones = ones.reshape(t // PACKED_ROWS, PACKED_ROWS, t)
            accs = list(accs)
            for i in range(t // PACKED_ROWS):
                accs[i % COUNT_CHAINS] = accs[i % COUNT_CHAINS] + ones[i]
            return tuple(accs)

        accs = (jnp.zeros((PACKED_ROWS, t), BF16),) * COUNT_CHAINS
        accs = lax.fori_loop(0, nk // 2, lambda i, a: body(2 * i + 1, body(2 * i, a)), accs)
        accs = lax.fori_loop(2 * (nk // 2), nk, body, accs)
        total = sum(a.astype(F32) for a in accs)
        return jnp.sum(total, axis=0, keepdims=True).astype(I32)

    def search16(ref, base):
        def accept(cand, cur):
            return jnp.where(base + count16(ref, cand) >= topk, cand, cur)

        v = accept(jnp.zeros((1, t), I32), jnp.full((1, t), I16_MIN, I32))
        return lax.fori_loop(0, 15, lambda i, v: accept(v | (jnp.int32(1) << (14 - i)), v), v)

    hi = search16(khi_ref, 0)
    above = count16(khi_ref, jnp.minimum(hi + 1, -I16_MIN - 1))
    hi16 = hi.astype(I16)

    def keep_low(kc, carry):
        sl = pl.ds(pl.multiple_of(kc * t, t), t)
        klo_ref[sl, :] = jnp.where(khi_ref[sl, :] == hi16, klo_ref[sl, :], jnp.asarray(I16_MIN, I16))
        return carry

    lax.fori_loop(0, nk, keep_low, 0)
    lo = search16(klo_ref, above)
    thr = jnp.maximum((hi << 16) | (lo - I16_MIN), KEY_NEG_INF + 1)

    def count(pred):
        def body(kc, acc):
            k = keys_ref[pl.ds(pl.multiple_of(kc * t, t), t), :]
            m = jnp.where(pred(k, kc), 1, 0).astype(I32)
            return acc + jnp.sum(m.reshape(t // SUBLANES, SUBLANES, t), axis=0)

        acc = lax.fori_loop(0, nk, body, jnp.zeros((SUBLANES, t), I32))
        return jnp.sum(acc, axis=0, keepdims=True)

    @pl.when(jnp.max(count(lambda k, kc: k >= thr)) > topk)
    def _():
        need = topk - count(lambda k, kc: k > thr)
        nbits = (seq_len - 1).bit_length()

        def pos_bit(i, y):
            cand = y | (jnp.int32(1) << (nbits - 1 - i))
            before = count(lambda k, kc: (k == thr) & ((kc * t + row) < cand))
            return jnp.where(before < need, cand, y)

        last_kept = lax.fori_loop(0, nbits, pos_bit, jnp.zeros((1, t), I32))

        def demote(kc, carry):
            sl = pl.ds(pl.multiple_of(kc * t, t), t)
            k = keys_ref[sl, :]
            keys_ref[sl, :] = jnp.where((k == thr) & ((kc * t + row) > last_kept), k - 1, k)
            return carry

        lax.fori_loop(0, nk, demote, 0)

    scale = A_HEAD_DIM ** -0.5 * LOG2E

    hw = A_HEADS * t
    qlat = jnp.concatenate(
        [(_dot(wuk_ref[h], qt_ref[0, h * A_HEAD_DIM:(h + 1) * A_HEAD_DIM, :]) * scale).astype(BF16)
         for h in range(A_HEADS)], axis=1)
    acc_ref[...] = jnp.zeros_like(acc_ref)

    def att_chunks(kc, m, *, chunks, bias_row=None):
        rows = chunks * t
        sl = pl.ds(pl.multiple_of(kc * t, t), rows)
        raw = _dot(ckv_ref[0, sl, :], qlat)
        sel = keys_ref[sl, :] >= thr
        parts = []
        for h in range(A_HEADS):
            lg_h = raw[:, h * t:(h + 1) * t]
            if bias_row is not None:
                lg_h = lg_h + bias_ref[bias_row:bias_row + rows, h * t:(h + 1) * t]
            parts.append(jnp.where(sel, lg_h, MASKED_LOGIT).astype(BF16))
        lg = jnp.concatenate(parts, axis=1)
        m_new = jnp.maximum(m, jnp.max(lg, axis=0, keepdims=True).astype(F32))
        p = jnp.exp2(lg - m_new.astype(BF16))
        acc = acc_ref[...] * jnp.exp2(m - m_new)
        for j in range(chunks):
            acc = acc + _dot(ckvt_ref[0, kc + j], p[j * t:(j + 1) * t])
        acc_ref[...] = acc
        return m_new

    n_far = jnp.maximum(qb - 1, 0)
    has_prev = jnp.minimum(qb, 1)
    m = jnp.full((1, hw), MASKED_LOGIT, F32)
    def far_group(chunks):
        def body(i, carry):
            first_chunk, m = carry
            return first_chunk + chunks, att_chunks(first_chunk, m, chunks=chunks)
        return body

    carry = lax.fori_loop(0, n_far // 4, far_group(4), (0, m))
    carry = lax.fori_loop(0, (n_far // 2) % 2, far_group(2), carry)
    _, m = lax.fori_loop(0, n_far % 2, far_group(1), carry)
    m = lax.fori_loop(0, has_prev, lambda i, m: att_chunks(qb - 1, m, chunks=2, bias_row=0), m)
    lax.fori_loop(0, 1 - has_prev, lambda i, m: att_chunks(qb, m, chunks=1, bias_row=t), m)
    olat = (acc_ref[:KV_RANK, :] / acc_ref[KV_RANK:KV_RANK + 1, :]).astype(BF16)
    for h in range(A_HEADS):
        ot_ref[h * A_HEAD_DIM:(h + 1) * A_HEAD_DIM, :] = _dot(wuvt_ref[h], olat[:, h * t:(h + 1) * t])
    o_ref[0] = ot_ref[...].T.astype(BF16)


def _dsa(qt, qit, ckv, ckvt, ki, wit, wuk, wuvt, bias):
    b, _, s = qt.shape
    t = DSA_TILE
    nt = s // t
    topk = min(TOPK_MAX, s // 4)
    assert s // PACKED_ROWS <= BF16_EXACT_INT
    return pl.pallas_call(
        functools.partial(_dsa_kernel, topk=topk, seq_len=s),
        out_shape=jax.ShapeDtypeStruct((b, s, A_WIDTH), BF16),
        grid=(b, nt),
        in_specs=[
            pl.BlockSpec((1, A_WIDTH, t), lambda bi, i: (bi, 0, i)),
            pl.BlockSpec((1, IDX_HEADS * IDX_DIM, t), lambda bi, i: (bi, 0, i)),
            pl.BlockSpec((1, s, KV_RANK), lambda bi, i: (bi, 0, 0)),
            pl.BlockSpec((1, nt, KV_RANK + SUM_ROWS, t), lambda bi, i: (bi, 0, 0, 0)),
            pl.BlockSpec((1, s, IDX_DIM), lambda bi, i: (bi, 0, 0)),
            pl.BlockSpec((1, IDX_HEADS, t), lambda bi, i: (bi, 0, i)),
            pl.BlockSpec((A_HEADS, KV_RANK, A_HEAD_DIM), lambda bi, i: (0, 0, 0)),
            pl.BlockSpec((A_HEADS, A_HEAD_DIM, KV_RANK), lambda bi, i: (0, 0, 0)),
            pl.BlockSpec((2 * t, A_HEADS * t), lambda bi, i: (0, 0)),
        ],
        out_specs=pl.BlockSpec((1, t, A_WIDTH), lambda bi, i: (bi, i, 0)),
        scratch_shapes=[pltpu.VMEM((s, t), I32), pltpu.VMEM((s, t), I16), pltpu.VMEM((s, t), I16),
                        pltpu.VMEM((KV_RANK + SUM_ROWS, A_HEADS * t), F32), pltpu.VMEM((A_WIDTH, t), F32)],
        compiler_params=_params("parallel", "arbitrary"),
        name="dsa",
    )(qt, qit, ckv, ckvt, ki, wit, wuk, wuvt, bias)


def _split3(x):
    hi = x.astype(BF16)
    r1 = x - hi.astype(F32)
    mid = r1.astype(BF16)
    lo = (r1 - mid.astype(F32)).astype(BF16)
    return hi, mid, lo


def _rwkv_kernel(r_ref, k_ref, v_ref, gl_ref, wl_ref, al_ref,
                 mur_ref, muk_ref, muv_ref, mug_ref, muw_ref, mua_ref,
                 w0_ref, a0_ref, kk_ref, ka_ref, rk_ref, lnw_ref, lnb_ref,
                 w2_ref, a2_ref, g2_ref,
                 o_ref,
                 pr_ref, pk_ref, pv_ref, pg_ref, pw_ref, pa_ref, state_ref):
    tl = RWKV_TILE
    c = RWKV_CHUNK
    nc = tl // c
    npair = R_WIDTH // PAIR

    @pl.when(pl.program_id(1) == 0)
    def _():
        state_ref[...] = jnp.zeros_like(state_ref)
        for ref in (pr_ref, pk_ref, pv_ref, pg_ref, pw_ref, pa_ref):
            ref[...] = jnp.zeros_like(ref)

    def shift_mix(x_ref, prev_ref, mu_ref):
        x = x_ref[0]
        first = lax.broadcasted_iota(I32, x.shape, 0) == 0
        xprev = jnp.where(first, prev_ref[...], pltpu.roll(x, 1, axis=0))
        prev_ref[...] = x[tl - 1:tl, :]
        return x + (xprev - x) * mu_ref[...]

    r = shift_mix(r_ref, pr_ref, mur_ref)
    k = shift_mix(k_ref, pk_ref, muk_ref)
    v = shift_mix(v_ref, pv_ref, muv_ref)
    gl = shift_mix(gl_ref, pg_ref, mug_ref)
    wl = shift_mix(wl_ref, pw_ref, muw_ref)
    al = shift_mix(al_ref, pa_ref, mua_ref)

    w_lin = w0_ref[...] + _dot(jnp.tanh(wl).astype(BF16), w2_ref[...])
    nx = -w_lin
    softplus = jnp.maximum(nx, 0.0) + jnp.log(1.0 + jnp.exp(-jnp.abs(nx)))
    ld = -jnp.exp(-softplus - 0.5)
    a = jax.nn.sigmoid(a0_ref[...] + _dot(al.astype(BF16), a2_ref[...]))
    gate = _dot(jax.nn.sigmoid(gl).astype(BF16), g2_ref[...])

    li = lax.broadcasted_iota(I32, (LANES, LANES), 0) // R_HEAD_DIM
    lj = lax.broadcasted_iota(I32, (LANES, LANES), 1) // R_HEAD_DIM
    head_ones = jnp.where(li == lj, 1.0, 0.0).astype(BF16)

    def head_sum(x):
        parts = []
        for j in range(R_WIDTH // LANES):
            hi, mid, lo = _split3(x[:, j * LANES:(j + 1) * LANES])
            parts.append(_dot(hi, head_ones) + _dot(mid, head_ones) + _dot(lo, head_ones))
        return jnp.concatenate(parts, axis=-1)

    kk = k * kk_ref[...]
    kk = kk / jnp.maximum(jnp.sqrt(head_sum(kk * kk)), 1e-12)
    k2 = k * (1.0 + (a - 1.0) * ka_ref[...])
    bb = kk * a

    ti = lax.broadcasted_iota(I32, (tl, tl), 0)
    tj = lax.broadcasted_iota(I32, (tl, tl), 1)
    tri = jnp.where((ti // c == tj // c) & (tj <= ti), 1.0, 0.0).astype(BF16)
    hi, mid, lo = _split3(ld)
    cum = _dot(tri, hi) + _dot(tri, mid) + _dot(tri, lo)
    cum_end = jnp.broadcast_to(cum.reshape(nc, c, R_WIDTH)[:, c - 1:c, :], (nc, c, R_WIDTH)).reshape(tl, R_WIDTH)
    e_in = jnp.exp(cum)
    e_neg = jnp.exp(-cum)
    e_out = jnp.exp(cum_end - cum)

    nb = nc * npair

    def to_pairs(x):
        x3 = x.reshape(nc, c, R_WIDTH)
        xs = jnp.stack([x3[:, :, p * PAIR:(p + 1) * PAIR] for p in range(npair)], axis=1)
        return xs.reshape(nb, c, PAIR)

    lane_head = lax.broadcasted_iota(I32, (2 * c, PAIR), 1) // R_HEAD_DIM
    row_head = lax.broadcasted_iota(I32, (2 * c, PAIR), 0) // c
    same = lane_head == row_head
    rt = lax.broadcasted_iota(I32, (2 * c, PAIR), 0) % c
    ct = lax.broadcasted_iota(I32, (2 * c, PAIR), 1) % c
    strict = same & (ct < rt)
    incl = same & (ct <= rt)
    eye = jnp.where(same & (ct == rt), 1.0, 0.0).astype(F32)

    def bd(x):
        return jnp.where(same, jnp.concatenate([x, x], axis=1), 0.0)

    lane0 = lax.broadcasted_iota(I32, (c, PAIR), 1) < R_HEAD_DIM

    abar = bd(to_pairs(-kk * jnp.exp(cum - ld))).astype(BF16)
    rbar = bd(to_pairs(r * e_in)).astype(BF16)
    bt = bd(to_pairs(bb * e_neg)).astype(BF16)
    kt = bd(to_pairs(k2 * e_neg)).astype(BF16)
    bk = jnp.concatenate([bd(to_pairs(bb * e_out)), bd(to_pairs(k2 * e_out))], axis=1).astype(BF16)
    v_pl = to_pairs(v)
    vbd = bd(v_pl)
    decay_end = to_pairs(jnp.exp(cum_end))[:, 0:1, :]
    aa = _bdot_nt(jnp.concatenate([abar, rbar], axis=1), jnp.concatenate([bt, kt], axis=1))
    a_ab = jnp.where(strict, aa[:, :2 * c, :PAIR], 0.0)
    a_ak = jnp.where(strict, aa[:, :2 * c, PAIR:], 0.0).astype(BF16)
    a_rb = jnp.where(incl, aa[:, 2 * c:, :PAIR], 0.0).astype(BF16)
    a_rk = jnp.where(incl, aa[:, 2 * c:, PAIR:], 0.0).astype(BF16)
    pw = a_ab.astype(BF16)
    tinv = eye + a_ab
    for _ in range(int(math.log2(c)) - 1):
        pw = _bdot(pw, pw).astype(BF16)
        tinv = tinv + _bdot(tinv.astype(BF16), pw)
    akv = _bdot(a_ak, vbd.astype(BF16))
    wu = _bdot(tinv.astype(BF16), jnp.concatenate([abar, akv.astype(BF16)], axis=2))
    w16 = wu[:, :, :PAIR].astype(BF16)
    u_bd = wu[:, :, PAIR:]

    s_bd = state_ref[...]
    e_parts, rs_parts = [], []
    for ci in range(nc):
        sl = slice(ci * npair, (ci + 1) * npair)
        wr = _bdot_nt(jnp.concatenate([w16[sl], rbar[sl]], axis=1), s_bd.astype(BF16))
        e_bd = wr[:, :2 * c] + u_bd[sl]
        e_parts.append(e_bd)
        rs_parts.append(wr[:, 2 * c:])
        ev = jnp.concatenate([e_bd, vbd[sl]], axis=1)
        ds = _bdot(jnp.swapaxes(ev, 1, 2).astype(BF16), bk[sl])
        s_bd = s_bd * decay_end[sl] + jnp.where(same, ds, 0.0)
    state_ref[...] = s_bd

    ev_all = jnp.concatenate([jnp.concatenate(e_parts, axis=0), vbd], axis=1).astype(BF16)
    y_bd = jnp.concatenate(rs_parts, axis=0) + _bdot(jnp.concatenate([a_rb, a_rk], axis=2), ev_all)
    mean = jnp.sum(y_bd, axis=-1, keepdims=True) * (1.0 / R_HEAD_DIM)
    dev = jnp.where(same, y_bd - mean, 0.0)
    var = jnp.sum(dev * dev, axis=-1, keepdims=True) * (1.0 / R_HEAD_DIM)
    yn = dev * lax.rsqrt(var + GN_EPS)
    yn = yn[:, :c] + yn[:, c:]
    rkv = to_pairs(r * k2 * rk_ref[...])
    s0 = jnp.sum(jnp.where(lane0, rkv, 0.0), axis=-1, keepdims=True)
    s1 = jnp.sum(jnp.where(lane0, 0.0, rkv), axis=-1, keepdims=True)
    bonus = jnp.where(lane0, s0, s1) * v_pl
    def per_pair(row_ref):
        rows = [row_ref[:, p * PAIR:(p + 1) * PAIR] for p in range(npair)]
        return jnp.stack(rows * nc, axis=0)

    out = ((yn * per_pair(lnw_ref) + per_pair(lnb_ref) + bonus) * to_pairs(gate)).astype(BF16)
    for ci in range(nc):
        for p in range(npair):
            o_ref[0, ci * c:(ci + 1) * c, p * PAIR:(p + 1) * PAIR] = out[ci * npair + p]


def _rwkv(z, prm):
    b, s, _ = z.shape
    tl = RWKV_TILE

    def seg(width, off):
        return pl.BlockSpec((1, tl, width), lambda bi, i: (bi, i, off // width))

    def full(arr):
        return pl.BlockSpec(arr.shape, lambda bi, i: (0,) * arr.ndim)

    names = ("mu_r", "mu_k", "mu_v", "mu_g", "mu_w", "mu_a", "w0", "a0", "k_k", "k_a", "r_k", "ln_w", "ln_b",
             "w2", "a2", "g2")
    consts = [prm[n] for n in names]
    return pl.pallas_call(
        _rwkv_kernel,
        out_shape=jax.ShapeDtypeStruct((b, s, R_WIDTH), BF16),
        grid=(b, s // tl),
        in_specs=[seg(R_WIDTH, Z_R), seg(R_WIDTH, Z_K), seg(R_WIDTH, Z_V), seg(GATE_LORA, Z_GL),
                  seg(LORA_PAD, Z_WL), seg(LORA_PAD, Z_AL)] + [full(x) for x in consts],
        out_specs=pl.BlockSpec((1, tl, R_WIDTH), lambda bi, i: (bi, i, 0)),
        scratch_shapes=[
            pltpu.VMEM((1, R_WIDTH), F32), pltpu.VMEM((1, R_WIDTH), F32), pltpu.VMEM((1, R_WIDTH), F32),
            pltpu.VMEM((1, GATE_LORA), F32), pltpu.VMEM((1, LORA_PAD), F32), pltpu.VMEM((1, LORA_PAD), F32),
            pltpu.VMEM((R_WIDTH // PAIR, PAIR, PAIR), F32),
        ],
        compiler_params=_params("parallel", "arbitrary"),
        name="rwkv7",
    )(z, z, z, z, z, z, *consts)


def _pad_cols(w, width):
    return jnp.pad(w, ((0, 0), (0, width - w.shape[1])))


def _w_in_layout_kernel(wt_ref, o_ref):
    wt = wt_ref[0]
    cols = wt.shape[1]
    offs = [0]
    for n in (A_WIDTH, KV_RANK, IDX_HEADS * IDX_DIM, IDX_DIM + IDX_HEADS,
              3 * R_WIDTH, DECAY_LORA, AAA_LORA, GATE_LORA):
        offs.append(offs[-1] + n)
    q, ckv, qi, kiwi, rkv, wl, al, gl = (wt[offs[i]:offs[i + 1], :] for i in range(8))

    def padded(x, height):
        return jnp.concatenate([x, jnp.zeros((height - x.shape[0], cols), x.dtype)], axis=0)

    rows = [q, qi, rkv, gl, ckv, padded(kiwi, LORA_PAD), padded(wl, LORA_PAD), padded(al, LORA_PAD),
            jnp.zeros((Z_WIDTH - Z_END, cols), wt.dtype)]
    o_ref[0] = jnp.concatenate(rows, axis=0).T.astype(BF16)


def _layout_w_in(w_in):
    depth, d, p = w_in.shape
    tc = _divisor_tile(d, 256)
    return pl.pallas_call(
        _w_in_layout_kernel,
        out_shape=jax.ShapeDtypeStruct((depth, d, Z_WIDTH), BF16),
        grid=(depth, d // tc),
        in_specs=[pl.BlockSpec((1, p, tc), lambda l, i: (l, 0, i))],
        out_specs=pl.BlockSpec((1, tc, Z_WIDTH), lambda l, i: (l, i, 0)),
        compiler_params=_params("parallel", "parallel"),
        name="w_in_layout",
    )(jnp.swapaxes(w_in, 1, 2))


def _pad_rows(w, rows):
    return jnp.pad(w, ((0, rows - w.shape[0]), (0, 0)))


def _rwkv_params(l, mu, w0, w2, a0, a2, g2, k_k, k_a, r_k, ln_w, ln_b):
    m = mu[l]
    o = [0, R_WIDTH, 2 * R_WIDTH, 3 * R_WIDTH, 3 * R_WIDTH + DECAY_LORA, 3 * R_WIDTH + DECAY_LORA + AAA_LORA]
    row = lambda x: x.reshape(1, -1)
    return {
        "mu_r": row(m[o[0]:o[1]]), "mu_k": row(m[o[1]:o[2]]), "mu_v": row(m[o[2]:o[3]]),
        "mu_w": _pad_cols(row(m[o[3]:o[4]]), LORA_PAD), "mu_a": _pad_cols(row(m[o[4]:o[5]]), LORA_PAD),
        "mu_g": row(m[o[5]:]),
        "w0": row(w0[l]), "a0": row(a0[l]), "k_k": row(k_k[l]), "k_a": row(k_a[l]), "r_k": row(r_k[l]),
        "ln_w": row(ln_w[l]), "ln_b": row(ln_b[l]),
        "w2": _pad_rows(w2[l], LORA_PAD).astype(BF16), "a2": _pad_rows(a2[l], LORA_PAD).astype(BF16),
        "g2": g2[l].astype(BF16),
    }


def kernel(x, c, t5_bias, ada_w, ada_b, norm_g, ffn_w_in, ffn_w_out, w_in, ckv_norm_g, w_uk, w_uv, rwkv_mu, rwkv_w0, rwkv_w2, rwkv_a0, rwkv_a2, rwkv_g2, rwkv_k_k, rwkv_k_a, rwkv_r_k, rwkv_ln_w, rwkv_ln_b, w_out, final_norm_g):
    b, s, d = x.shape
    depth = ada_w.shape[0]
    assert s % DSA_TILE == 0 and s % RWKV_TILE == 0 and d == A_WIDTH + R_WIDTH
    mod = _ada_mod(c, ada_w, ada_b).reshape(depth, b, N_SUB, 3, 1, d)
    bias = _bias_tiles(t5_bias)
    final_g = final_norm_g.reshape(1, d)
    w_in_z = _layout_w_in(w_in)
    w_out16 = w_out.astype(BF16)
    h = x
    for l in range(depth):
        shift = lambda i: mod[l, :, i, 0]
        scale = lambda i: mod[l, :, i, 1]
        gate = lambda i: mod[l, :, i, 2]
        g = lambda i: norm_g[l, i].reshape(1, d)
        h = _ffn(h, g(0), shift(0), scale(0), gate(0), ffn_w_in, ffn_w_out, l, 0, final_g, final_norm=False)
        z = _proj(h, g(1), shift(1), scale(1), w_in_z, l)
        qt, qit, ckv, ckvt, ki, wit = _dsa_prep(z, ckv_norm_g[l].reshape(1, KV_RANK))
        wuk = jnp.transpose(w_uk[l], (1, 0, 2)).astype(BF16)
        wuvt = jnp.transpose(w_uv[l], (1, 2, 0)).astype(BF16)
        o_a = _dsa(qt, qit, ckv, ckvt, ki, wit, wuk, wuvt, bias)
        o_r = _rwkv(z, _rwkv_params(l, rwkv_mu, rwkv_w0, rwkv_w2, rwkv_a0, rwkv_a2, rwkv_g2, rwkv_k_k,
                                    rwkv_k_a, rwkv_r_k, rwkv_ln_w, rwkv_ln_b))
        h = _outproj(o_a, o_r, h, gate(1), w_out16, l)
        h = _ffn(h, g(2), shift(2), scale(2), gate(2), ffn_w_in, ffn_w_out, l, 1, final_g,
                 final_norm=(l == depth - 1))
    return h
```

```python
import functools
import math

import jax
import jax.numpy as jnp
from jax import lax
from jax.experimental import pallas as pl
from jax.experimental.pallas import tpu as pltpu

F32 = jnp.float32
BF16 = jnp.bfloat16
I32 = jnp.int32
I16 = jnp.int16

A_HEADS = 8
A_HEAD_DIM = 128
A_WIDTH = A_HEADS * A_HEAD_DIM
KV_RANK = 256
IDX_HEADS = 16
IDX_DIM = 64
TOPK_MAX = 256
REL_BUCKETS = 32
REL_MAX_EXACT = REL_BUCKETS // 2
REL_MAX_DIST = 128
R_HEAD_DIM = 64
R_WIDTH = 1024
DECAY_LORA = 96
AAA_LORA = 96
GATE_LORA = 256
GN_EPS = 64e-5
RMS_EPS = 1e-6
N_SUB = 3

LANES = 128
SUBLANES = 8
VMEM_LIMIT_BYTES = 56 * 1024 * 1024

LORA_PAD = 128
Z_Q = 0
Z_QI = Z_Q + A_WIDTH
Z_R = Z_QI + IDX_HEADS * IDX_DIM
Z_K = Z_R + R_WIDTH
Z_V = Z_K + R_WIDTH
Z_GL = Z_V + R_WIDTH
Z_CKV = Z_GL + GATE_LORA
Z_KIWI = Z_CKV + KV_RANK
Z_WL = Z_KIWI + LORA_PAD
Z_AL = Z_WL + LORA_PAD
Z_END = Z_AL + LORA_PAD
Z_WIDTH = 6144

FFN_TOKEN_TILE = 1024
FFN_HIDDEN_TILE = 512
PROLOGUE_ROWS = 256
PROJ_COLUMN_TILE = 2048
ADALN_SLAB_ROWS = 16
ADALN_UNROLL = 8
DSA_TILE = 256
RWKV_CHUNK = 64
RWKV_TILE = 256
PAIR = 2 * R_HEAD_DIM

I16_MIN = -(2 ** 15)
PACKED_ROWS = 2 * SUBLANES
SUM_ROWS = PACKED_ROWS
BF16_EXACT_INT = 256
COUNT_CHAINS = 4
KEY_NEG_INF = -2139095041
MASKED_LOGIT = -(2.0 ** 100)
LOG2E = math.log2(math.e)
FAR_DISTANCE = math.ceil(REL_MAX_EXACT * (REL_MAX_DIST / REL_MAX_EXACT)
                         ** ((REL_BUCKETS - 1 - REL_MAX_EXACT) / (REL_BUCKETS - REL_MAX_EXACT)))


def _dot(a, b):
    return jnp.dot(a, b, preferred_element_type=F32)


def _bdot(a, b):
    return lax.dot_general(a, b, (((2,), (1,)), ((0,), (0,))), preferred_element_type=F32)


def _bdot_nt(a, b):
    return lax.dot_general(a, b, (((2,), (2,)), ((0,), (0,))), preferred_element_type=F32)


def _rms(x, g, eps):
    ms = jnp.mean(x * x, axis=-1, keepdims=True)
    return x * lax.rsqrt(ms + eps) * g


def _divisor_tile(n, pref):
    if n <= pref:
        return n
    t = (pref // LANES) * LANES
    while t > LANES and n % t:
        t -= LANES
    assert n % t == 0, (n, pref)
    return t


def _params(*sem):
    return pltpu.CompilerParams(dimension_semantics=sem, vmem_limit_bytes=VMEM_LIMIT_BYTES)


def _ada_kernel(c_ref, w_ref, b_ref, o_ref):
    c = c_ref[...]
    ca = (c * jax.nn.sigmoid(c)).astype(BF16)
    o_ref[0] = _dot(ca, w_ref[0].astype(BF16)) + b_ref[0]


def _ada_mod(c, ada_w, ada_b):
    depth, d, n = ada_w.shape
    b = c.shape[0]
    bp = -(-b // SUBLANES) * SUBLANES
    cp = jnp.pad(c, ((0, bp - b), (0, 0)))
    tn = _divisor_tile(n, 1024)
    out = pl.pallas_call(
        _ada_kernel,
        out_shape=jax.ShapeDtypeStruct((depth, bp, n), F32),
        grid=(depth, n // tn),
        in_specs=[
            pl.BlockSpec((bp, d), lambda l, j: (0, 0)),
            pl.BlockSpec((1, d, tn), lambda l, j: (l, 0, j)),
            pl.BlockSpec((1, 1, tn), lambda l, j: (l, 0, j)),
        ],
        out_specs=pl.BlockSpec((1, bp, tn), lambda l, j: (l, 0, j)),
        compiler_params=_params("arbitrary", "arbitrary"),
        name="ada_mod",
    )(cp, ada_w, ada_b.reshape(depth, 1, n))
    return out[:, :b]


def _adaln_rows(h_ref, g_ref, scale_ref, shift_ref, hn_ref, base, copy_ref=None):
    rows = h_ref.shape[1]
    g = g_ref[...]
    mul = 1.0 + scale_ref[0]
    add = shift_ref[0]

    def slab(i, carry):
        src = pl.ds(pl.multiple_of(i * ADALN_SLAB_ROWS, ADALN_SLAB_ROWS), ADALN_SLAB_ROWS)
        dst = pl.ds(pl.multiple_of(base + i * ADALN_SLAB_ROWS, ADALN_SLAB_ROWS), ADALN_SLAB_ROWS)
        h = h_ref[0, src, :]
        hn_ref[dst, :] = (_rms(h, g, RMS_EPS) * mul + add).astype(BF16)
        if copy_ref is not None:
            copy_ref[0, dst, :] = h
        return carry

    lax.fori_loop(0, rows // ADALN_SLAB_ROWS, slab, 0, unroll=ADALN_UNROLL)


def _prologue_split(tm):
    rows = _divisor_tile(tm, PROLOGUE_ROWS)
    return rows, tm // rows


def _ffn_kernel(h_ref, g_ref, shift_ref, scale_ref, gate_ref, wg_ref, wu_ref, wo_ref, fg_ref,
                o_ref, hn_ref, *, final_norm, pro_steps):
    f = pl.program_id(2)

    @pl.when(f < pro_steps)
    def _():
        _adaln_rows(h_ref, g_ref, scale_ref, shift_ref, hn_ref, f * h_ref.shape[1], o_ref)

    @pl.when(f >= pro_steps)
    def _():
        hn = hn_ref[...]
        g = _dot(hn, wg_ref[...].astype(BF16))
        u = _dot(hn, wu_ref[...].astype(BF16))
        act = (g * jax.nn.sigmoid(g) * u).astype(BF16)
        o_ref[0] += (0.5 * gate_ref[0]) * _dot(act, wo_ref[...].astype(BF16))

    if final_norm:
        @pl.when(f == pl.num_programs(2) - 1)
        def _():
            o_ref[0] = _rms(o_ref[0], fg_ref[...], RMS_EPS)


def _ffn(h, g, shift, scale, gate, w_in, w_out, layer, which, final_g, *, final_norm):
    b, s, d = h.shape
    ff = w_out.shape[2]
    tm = _divisor_tile(s, FFN_TOKEN_TILE)
    tf = _divisor_tile(ff, FFN_HIDDEN_TILE)
    nf = ff // tf
    hr, pro = _prologue_split(tm)
    vec = pl.BlockSpec((1, 1, d), lambda bi, i, f: (bi, 0, 0))
    row = pl.BlockSpec((1, d), lambda bi, i, f: (0, 0))

    def wf(f):
        return jnp.maximum(f - pro, 0)

    return pl.pallas_call(
        functools.partial(_ffn_kernel, final_norm=final_norm, pro_steps=pro),
        out_shape=jax.ShapeDtypeStruct((b, s, d), F32),
        grid=(b, s // tm, pro + nf),
        in_specs=[
            pl.BlockSpec((1, hr, d), lambda bi, i, f: (bi, i * pro + jnp.minimum(f, pro - 1), 0)),
            row, vec, vec, vec,
            pl.BlockSpec((None, None, d, tf), lambda bi, i, f: (layer, which, 0, wf(f))),
            pl.BlockSpec((None, None, d, tf), lambda bi, i, f: (layer, which, 0, nf + wf(f))),
            pl.BlockSpec((None, None, tf, d), lambda bi, i, f: (layer, which, wf(f), 0)),
            row,
        ],
        out_specs=pl.BlockSpec((1, tm, d), lambda bi, i, f: (bi, i, 0)),
        scratch_shapes=[pltpu.VMEM((tm, d), BF16)],
        compiler_params=_params("parallel", "parallel", "arbitrary"),
        name="ffn",
    )(h, g, shift, scale, gate, w_in, w_in, w_out, final_g)


def _proj_kernel(h_ref, g_ref, shift_ref, scale_ref, w_ref, o_ref, hn_ref, *, pro_steps):
    n = pl.program_id(2)

    @pl.when(n < pro_steps)
    def _():
        _adaln_rows(h_ref, g_ref, scale_ref, shift_ref, hn_ref, n * h_ref.shape[1])

    @pl.when(n >= pro_steps)
    def _():
        o_ref[0] = _dot(hn_ref[...], w_ref[...])


def _proj(h, g, shift, scale, w, layer):
    b, s, d = h.shape
    p = w.shape[2]
    tm = _divisor_tile(s, 1024)
    tn = _divisor_tile(p, PROJ_COLUMN_TILE)
    hr, pro = _prologue_split(tm)
    vec = pl.BlockSpec((1, 1, d), lambda bi, i, n: (bi, 0, 0))

    def col(n):
        return jnp.maximum(n - pro, 0)

    return pl.pallas_call(
        functools.partial(_proj_kernel, pro_steps=pro),
        out_shape=jax.ShapeDtypeStruct((b, s, p), F32),
        grid=(b, s // tm, pro + p // tn),
        in_specs=[
            pl.BlockSpec((1, hr, d), lambda bi, i, n: (bi, i * pro + jnp.minimum(n, pro - 1), 0)),
            pl.BlockSpec((1, d), lambda bi, i, n: (0, 0)),
            vec, vec,
            pl.BlockSpec((None, d, tn), lambda bi, i, n: (layer, 0, col(n))),
        ],
        out_specs=pl.BlockSpec((1, tm, tn), lambda bi, i, n: (bi, i, col(n))),
        scratch_shapes=[pltpu.VMEM((tm, d), BF16)],
        compiler_params=_params("parallel", "parallel", "arbitrary"),
        name="proj",
    )(h, g, shift, scale, w)


def _outproj_kernel(oa_ref, or_ref, h_ref, gate_ref, wa_ref, wr_ref, o_ref):
    acc = _dot(oa_ref[0], wa_ref[...]) + _dot(or_ref[0], wr_ref[...])
    o_ref[0] = h_ref[0] + gate_ref[0] * acc


def _outproj(o_a, o_r, h, gate, w_out, layer):
    b, s, d = h.shape
    tm = _divisor_tile(s, 512)
    tn = d
    wa = o_a.shape[-1]
    wr = o_r.shape[-1]
    return pl.pallas_call(
        _outproj_kernel,
        out_shape=jax.ShapeDtypeStruct((b, s, d), F32),
        grid=(b, s // tm, d // tn),
        in_specs=[
            pl.BlockSpec((1, tm, wa), lambda bi, i, n: (bi, i, 0)),
            pl.BlockSpec((1, tm, wr), lambda bi, i, n: (bi, i, 0)),
            pl.BlockSpec((1, tm, tn), lambda bi, i, n: (bi, i, n)),
            pl.BlockSpec((1, 1, tn), lambda bi, i, n: (bi, 0, n)),
            pl.BlockSpec((None, wa, tn), lambda bi, i, n: (layer, 0, n)),
            pl.BlockSpec((None, wr, tn), lambda bi, i, n: (layer, wa // wr, n)),
        ],
        out_specs=pl.BlockSpec((1, tm, tn), lambda bi, i, n: (bi, i, n)),
        compiler_params=_params("parallel", "parallel", "arbitrary"),
        name="outproj",
    )(o_a, o_r, h, gate, w_out, w_out)


def _dsa_prep_kernel(q_ref, qi_ref, ckv_ref, kiwi_ref, g_ref, qt_ref, qit_ref, ckv_o, ckvt_o, ki_o, wit_o):
    qt_ref[0] = q_ref[0].T.astype(BF16)
    qit_ref[0] = qi_ref[0].T.astype(BF16)
    cn = _rms(ckv_ref[0], g_ref[...], RMS_EPS)
    ckv_o[0] = cn.astype(BF16)
    ckvt_o[0, 0] = jnp.concatenate([cn.T.astype(BF16), jnp.ones((SUM_ROWS, cn.shape[0]), BF16)], axis=0)
    kw = kiwi_ref[0]
    ki_o[0] = kw[:, :IDX_DIM].astype(BF16)
    wit_o[0] = kw.T[IDX_DIM:IDX_DIM + IDX_HEADS, :] * (IDX_HEADS * IDX_DIM) ** -0.5


def _dsa_prep(z, ckv_g):
    b, s, _ = z.shape
    t = DSA_TILE
    nt = s // t

    def seg(width, off):
        return pl.BlockSpec((1, t, width), lambda bi, i: (bi, i, off // width))

    return pl.pallas_call(
        _dsa_prep_kernel,
        out_shape=(
            jax.ShapeDtypeStruct((b, A_WIDTH, s), BF16),
            jax.ShapeDtypeStruct((b, IDX_HEADS * IDX_DIM, s), BF16),
            jax.ShapeDtypeStruct((b, s, KV_RANK), BF16),
            jax.ShapeDtypeStruct((b, nt, KV_RANK + SUM_ROWS, t), BF16),
            jax.ShapeDtypeStruct((b, s, IDX_DIM), BF16),
            jax.ShapeDtypeStruct((b, IDX_HEADS, s), F32),
        ),
        grid=(b, nt),
        in_specs=[
            seg(A_WIDTH, Z_Q), seg(IDX_HEADS * IDX_DIM, Z_QI), seg(KV_RANK, Z_CKV), seg(LORA_PAD, Z_KIWI),
            pl.BlockSpec((1, KV_RANK), lambda bi, i: (0, 0)),
        ],
        out_specs=(
            pl.BlockSpec((1, A_WIDTH, t), lambda bi, i: (bi, 0, i)),
            pl.BlockSpec((1, IDX_HEADS * IDX_DIM, t), lambda bi, i: (bi, 0, i)),
            pl.BlockSpec((1, t, KV_RANK), lambda bi, i: (bi, i, 0)),
            pl.BlockSpec((1, 1, KV_RANK + SUM_ROWS, t), lambda bi, i: (bi, i, 0, 0)),
            pl.BlockSpec((1, t, IDX_DIM), lambda bi, i: (bi, i, 0)),
            pl.BlockSpec((1, IDX_HEADS, t), lambda bi, i: (bi, 0, i)),
        ),
        compiler_params=_params("parallel", "parallel"),
        name="dsa_prep",
    )(z, z, z, z, ckv_g)


def _bias_kernel(t5_ref, o_ref):
    t = DSA_TILE
    j = lax.broadcasted_iota(I32, (t, t), 0)
    i = lax.broadcasted_iota(I32, (t, t), 1)
    for didx in range(2):
        n = jnp.maximum(didx * t + i - j, 0)
        nf = jnp.maximum(n, 1).astype(F32)
        large = REL_MAX_EXACT + (jnp.log(nf / REL_MAX_EXACT) / math.log(REL_MAX_DIST / REL_MAX_EXACT)
                                 * (REL_BUCKETS - REL_MAX_EXACT)).astype(I32)
        large = jnp.minimum(large, REL_BUCKETS - 1)
        bucket = jnp.where(n < REL_MAX_EXACT, n, large)
        for h in range(A_HEADS):
            val = jnp.zeros((t, t), F32)
            for k in range(REL_BUCKETS):
                val = jnp.where(bucket == k, t5_ref[k, h], val)
            o_ref[(1 - didx) * t:(2 - didx) * t, h * t:(h + 1) * t] = (val - t5_ref[REL_BUCKETS - 1, h]) * LOG2E


def _bias_tiles(t5_bias):
    t = DSA_TILE
    assert t + 1 >= FAR_DISTANCE
    return pl.pallas_call(
        _bias_kernel,
        out_shape=jax.ShapeDtypeStruct((2 * t, A_HEADS * t), F32),
        in_specs=[pl.BlockSpec(memory_space=pltpu.SMEM)],
        out_specs=pl.BlockSpec(memory_space=pltpu.VMEM),
        compiler_params=pltpu.CompilerParams(vmem_limit_bytes=VMEM_LIMIT_BYTES),
        name="t5_bias_tiles",
    )(t5_bias)


def _dsa_kernel(qt_ref, qit_ref, ckv_ref, ckvt_ref, ki_ref, wit_ref, wuk_ref, wuvt_ref, bias_ref,
                o_ref, keys_ref, khi_ref, klo_ref, acc_ref, ot_ref, *, topk, seq_len):
    t = DSA_TILE
    qb = pl.program_id(1)
    nk = qb + 1
    row = lax.broadcasted_iota(I32, (t, t), 0)
    col = lax.broadcasted_iota(I32, (t, t), 1)

    def idx_rows(start, rows, diagonal):
        sl = pl.ds(pl.multiple_of(start, rows), rows)
        kic = ki_ref[0, sl, :]
        acc = jnp.zeros((rows, t), F32)
        for h in range(IDX_HEADS):
            rel = _dot(kic, qit_ref[0, h * IDX_DIM:(h + 1) * IDX_DIM, :])
            acc = acc + jnp.maximum(rel, 0.0) * wit_ref[0, h:h + 1, :]
        acc = acc + 0.0
        bits = pltpu.bitcast(acc, I32)
        key = bits ^ ((bits >> 31) & 0x7FFFFFFF)
        if diagonal:
            key = jnp.where(row <= col, key, KEY_NEG_INF)
        keys_ref[sl, :] = key
        khi_ref[sl, :] = (key >> 16).astype(I16)
        klo_ref[sl, :] = ((key & 0xFFFF) + I16_MIN).astype(I16)

    def idx_group(chunks):
        def body(i, first_chunk):
            idx_rows(first_chunk * t, chunks * t, False)
            return first_chunk + chunks
        return body

    done = lax.fori_loop(0, qb // 4, idx_group(4), 0)
    done = lax.fori_loop(0, (qb // 2) % 2, idx_group(2), done)
    lax.fori_loop(0, qb % 2, idx_group(1), done)
    idx_rows(qb * t, t, True)

    def count16(ref, cand):
        cand16 = cand.astype(I16)

        def body(kc, accs):
            x = ref[pl.ds(pl.multiple_of(kc * t, t), t), :]
            ones = jnp.where(x >= cand16, jnp.asarray(1, BF16), jnp.asarray(0, BF16))
            ones = ones.reshape(t // PACKED_ROWS, PACKED_ROWS, t)
            accs = list(accs)
            for i in range(t // PACKED_ROWS):
                accs[i % COUNT_CHAINS] = accs[i % COUNT_CHAINS] + ones[i]
            return tuple(accs)

        accs = (jnp.zeros((PACKED_ROWS, t), BF16),) * COUNT_CHAINS
        accs = lax.fori_loop(0, nk // 2, lambda i, a: body(2 * i + 1, body(2 * i, a)), accs)
        accs = lax.fori_loop(2 * (nk // 2), nk, body, accs)
        total = sum(a.astype(F32) for a in accs)
        return jnp.sum(total, axis=0, keepdims=True).astype(I32)

    def search16(ref, base):
        def accept(cand, cur):
            return jnp.where(base + count16(ref, cand) >= topk, cand, cur)

        v = accept(jnp.zeros((1, t), I32), jnp.full((1, t), I16_MIN, I32))
        return lax.fori_loop(0, 15, lambda i, v: accept(v | (jnp.int32(1) << (14 - i)), v), v)

    hi = search16(khi_ref, 0)
    above = count16(khi_ref, jnp.minimum(hi + 1, -I16_MIN - 1))
    hi16 = hi.astype(I16)

    def keep_low(kc, carry):
        sl = pl.ds(pl.multiple_of(kc * t, t), t)
        klo_ref[sl, :] = jnp.where(khi_ref[sl, :] == hi16, klo_ref[sl, :], jnp.asarray(I16_MIN, I16))
        return carry

    lax.fori_loop(0, nk, keep_low, 0)
    lo = search16(klo_ref, above)
    thr = jnp.maximum((hi << 16) | (lo - I16_MIN), KEY_NEG_INF + 1)

    def count(pred):
        def body(kc, acc):
            k = keys_ref[pl.ds(pl.multiple_of(kc * t, t), t), :]
            m = jnp.where(pred(k, kc), 1, 0).astype(I32)
            return acc + jnp.sum(m.reshape(t // SUBLANES, SUBLANES, t), axis=0)

        acc = lax.fori_loop(0, nk, body, jnp.zeros((SUBLANES, t), I32))
        return jnp.sum(acc, axis=0, keepdims=True)

    @pl.when(jnp.max(count(lambda k, kc: k >= thr)) > topk)
    def _():
        need = topk - count(lambda k, kc: k > thr)
        nbits = (seq_len - 1).bit_length()

        def pos_bit(i, y):
            cand = y | (jnp.int32(1) << (nbits - 1 - i))
            before = count(lambda k, kc: (k == thr) & ((kc * t + row) < cand))
            return jnp.where(before < need, cand, y)

        last_kept = lax.fori_loop(0, nbits, pos_bit, jnp.zeros((1, t), I32))

        def demote(kc, carry):
            sl = pl.ds(pl.multiple_of(kc * t, t), t)
            k = keys_ref[sl, :]
            keys_ref[sl, :] = jnp.where((k == thr) & ((kc * t + row) > last_kept), k - 1, k)
            return carry

        lax.fori_loop(0, nk, demote, 0)

    scale = A_HEAD_DIM ** -0.5 * LOG2E

    hw = A_HEADS * t
    qlat = jnp.concatenate(
        [(_dot(wuk_ref[h], qt_ref[0, h * A_HEAD_DIM:(h + 1) * A_HEAD_DIM, :]) * scale).astype(BF16)
         for h in range(A_HEADS)], axis=1)
    acc_ref[...] = jnp.zeros_like(acc_ref)

    def att_chunks(kc, m, *, chunks, bias_row=None):
        rows = chunks * t
        sl = pl.ds(pl.multiple_of(kc * t, t), rows)
        raw = _dot(ckv_ref[0, sl, :], qlat)
        sel = keys_ref[sl, :] >= thr
        parts = []
        for h in range(A_HEADS):
            lg_h = raw[:, h * t:(h + 1) * t]
            if bias_row is not None:
                lg_h = lg_h + bias_ref[bias_row:bias_row + rows, h * t:(h + 1) * t]
            parts.append(jnp.where(sel, lg_h, MASKED_LOGIT).astype(BF16))
        lg = jnp.concatenate(parts, axis=1)
        m_new = jnp.maximum(m, jnp.max(lg, axis=0, keepdims=True).astype(F32))
        p = jnp.exp2(lg - m_new.astype(BF16))
        acc = acc_ref[...] * jnp.exp2(m - m_new)
        for j in range(chunks):
            acc = acc + _dot(ckvt_ref[0, kc + j], p[j * t:(j + 1) * t])
        acc_ref[...] = acc
        return m_new

    n_far = jnp.maximum(qb - 1, 0)
    has_prev = jnp.minimum(qb, 1)
    m = jnp.full((1, hw), MASKED_LOGIT, F32)
    def far_group(chunks):
        def body(i, carry):
            first_chunk, m = carry
            return first_chunk + chunks, att_chunks(first_chunk, m, chunks=chunks)
        return body

    carry = lax.fori_loop(0, n_far // 4, far_group(4), (0, m))
    carry = lax.fori_loop(0, (n_far // 2) % 2, far_group(2), carry)
    _, m = lax.fori_loop(0, n_far % 2, far_group(1), carry)
    m = lax.fori_loop(0, has_prev, lambda i, m: att_chunks(qb - 1, m, chunks=2, bias_row=0), m)
    lax.fori_loop(0, 1 - has_prev, lambda i, m: att_chunks(qb, m, chunks=1, bias_row=t), m)
    olat = (acc_ref[:KV_RANK, :] / acc_ref[KV_RANK:KV_RANK + 1, :]).astype(BF16)
    for h in range(A_HEADS):
        ot_ref[h * A_HEAD_DIM:(h + 1) * A_HEAD_DIM, :] = _dot(wuvt_ref[h], olat[:, h * t:(h + 1) * t])
    o_ref[0] = ot_ref[...].T.astype(BF16)


def _dsa(qt, qit, ckv, ckvt, ki, wit, wuk, wuvt, bias):
    b, _, s = qt.shape
    t = DSA_TILE
    nt = s // t
    topk = min(TOPK_MAX, s // 4)
    assert s // PACKED_ROWS <= BF16_EXACT_INT
    return pl.pallas_call(
        functools.partial(_dsa_kernel, topk=topk, seq_len=s),
        out_shape=jax.ShapeDtypeStruct((b, s, A_WIDTH), BF16),
        grid=(b, nt),
        in_specs=[
            pl.BlockSpec((1, A_WIDTH, t), lambda bi, i: (bi, 0, i)),
            pl.BlockSpec((1, IDX_HEADS * IDX_DIM, t), lambda bi, i: (bi, 0, i)),
            pl.BlockSpec((1, s, KV_RANK), lambda bi, i: (bi, 0, 0)),
            pl.BlockSpec((1, nt, KV_RANK + SUM_ROWS, t), lambda bi, i: (bi, 0, 0, 0)),
            pl.BlockSpec((1, s, IDX_DIM), lambda bi, i: (bi, 0, 0)),
            pl.BlockSpec((1, IDX_HEADS, t), lambda bi, i: (bi, 0, i)),
            pl.BlockSpec((A_HEADS, KV_RANK, A_HEAD_DIM), lambda bi, i: (0, 0, 0)),
            pl.BlockSpec((A_HEADS, A_HEAD_DIM, KV_RANK), lambda bi, i: (0, 0, 0)),
            pl.BlockSpec((2 * t, A_HEADS * t), lambda bi, i: (0, 0)),
        ],
        out_specs=pl.BlockSpec((1, t, A_WIDTH), lambda bi, i: (bi, i, 0)),
        scratch_shapes=[pltpu.VMEM((s, t), I32), pltpu.VMEM((s, t), I16), pltpu.VMEM((s, t), I16),
                        pltpu.VMEM((KV_RANK + SUM_ROWS, A_HEADS * t), F32), pltpu.VMEM((A_WIDTH, t), F32)],
        compiler_params=_params("parallel", "arbitrary"),
        name="dsa",
    )(qt, qit, ckv, ckvt, ki, wit, wuk, wuvt, bias)


def _split3(x):
    hi = x.astype(BF16)
    r1 = x - hi.astype(F32)
    mid = r1.astype(BF16)
    lo = (r1 - mid.astype(F32)).astype(BF16)
    return hi, mid, lo


def _rwkv_kernel(r_ref, k_ref, v_ref, gl_ref, wl_ref, al_ref,
                 mur_ref, muk_ref, muv_ref, mug_ref, muw_ref, mua_ref,
                 w0_ref, a0_ref, kk_ref, ka_ref, rk_ref, lnw_ref, lnb_ref,
                 w2_ref, a2_ref, g2_ref,
                 o_ref,
                 pr_ref, pk_ref, pv_ref, pg_ref, pw_ref, pa_ref, state_ref):
    tl = RWKV_TILE
    c = RWKV_CHUNK
    nc = tl // c
    npair = R_WIDTH // PAIR

    @pl.when(pl.program_id(1) == 0)
    def _():
        state_ref[...] = jnp.zeros_like(state_ref)
        for ref in (pr_ref, pk_ref, pv_ref, pg_ref, pw_ref, pa_ref):
            ref[...] = jnp.zeros_like(ref)

    def shift_mix(x_ref, prev_ref, mu_ref):
        x = x_ref[0]
        first = lax.broadcasted_iota(I32, x.shape, 0) == 0
        xprev = jnp.where(first, prev_ref[...], pltpu.roll(x, 1, axis=0))
        prev_ref[...] = x[tl - 1:tl, :]
        return x + (xprev - x) * mu_ref[...]

    r = shift_mix(r_ref, pr_ref, mur_ref)
    k = shift_mix(k_ref, pk_ref, muk_ref)
    v = shift_mix(v_ref, pv_ref, muv_ref)
    gl = shift_mix(gl_ref, pg_ref, mug_ref)
    wl = shift_mix(wl_ref, pw_ref, muw_ref)
    al = shift_mix(al_ref, pa_ref, mua_ref)

    w_lin = w0_ref[...] + _dot(jnp.tanh(wl).astype(BF16), w2_ref[...])
    nx = -w_lin
    softplus = jnp.maximum(nx, 0.0) + jnp.log(1.0 + jnp.exp(-jnp.abs(nx)))
    ld = -jnp.exp(-softplus - 0.5)
    a = jax.nn.sigmoid(a0_ref[...] + _dot(al.astype(BF16), a2_ref[...]))
    gate = _dot(jax.nn.sigmoid(gl).astype(BF16), g2_ref[...])

    li = lax.broadcasted_iota(I32, (LANES, LANES), 0) // R_HEAD_DIM
    lj = lax.broadcasted_iota(I32, (LANES, LANES), 1) // R_HEAD_DIM
    head_ones = jnp.where(li == lj, 1.0, 0.0).astype(BF16)

    def head_sum(x):
        parts = []
        for j in range(R_WIDTH // LANES):
            hi, mid, lo = _split3(x[:, j * LANES:(j + 1) * LANES])
            parts.append(_dot(hi, head_ones) + _dot(mid, head_ones) + _dot(lo, head_ones))
        return jnp.concatenate(parts, axis=-1)

    kk = k * kk_ref[...]
    kk = kk / jnp.maximum(jnp.sqrt(head_sum(kk * kk)), 1e-12)
    k2 = k * (1.0 + (a - 1.0) * ka_ref[...])
    bb = kk * a

    ti = lax.broadcasted_iota(I32, (tl, tl), 0)
    tj = lax.broadcasted_iota(I32, (tl, tl), 1)
    tri = jnp.where((ti // c == tj // c) & (tj <= ti), 1.0, 0.0).astype(BF16)
    hi, mid, lo = _split3(ld)
    cum = _dot(tri, hi) + _dot(tri, mid) + _dot(tri, lo)
    cum_end = jnp.broadcast_to(cum.reshape(nc, c, R_WIDTH)[:, c - 1:c, :], (nc, c, R_WIDTH)).reshape(tl, R_WIDTH)
    e_in = jnp.exp(cum)
    e_neg = jnp.exp(-cum)
    e_out = jnp.exp(cum_end - cum)

    nb = nc * npair

    def to_pairs(x):
        x3 = x.reshape(nc, c, R_WIDTH)
        xs = jnp.stack([x3[:, :, p * PAIR:(p + 1) * PAIR] for p in range(npair)], axis=1)
        return xs.reshape(nb, c, PAIR)

    lane_head = lax.broadcasted_iota(I32, (2 * c, PAIR), 1) // R_HEAD_DIM
    row_head = lax.broadcasted_iota(I32, (2 * c, PAIR), 0) // c
    same = lane_head == row_head
    rt = lax.broadcasted_iota(I32, (2 * c, PAIR), 0) % c
    ct = lax.broadcasted_iota(I32, (2 * c, PAIR), 1) % c
    strict = same & (ct < rt)
    incl = same & (ct <= rt)
    eye = jnp.where(same & (ct == rt), 1.0, 0.0).astype(F32)

    def bd(x):
        return jnp.where(same, jnp.concatenate([x, x], axis=1), 0.0)

    lane0 = lax.broadcasted_iota(I32, (c, PAIR), 1) < R_HEAD_DIM

    abar = bd(to_pairs(-kk * jnp.exp(cum - ld))).astype(BF16)
    rbar = bd(to_pairs(r * e_in)).astype(BF16)
    bt = bd(to_pairs(bb * e_neg)).astype(BF16)
    kt = bd(to_pairs(k2 * e_neg)).astype(BF16)
    bk = jnp.concatenate([bd(to_pairs(bb * e_out)), bd(to_pairs(k2 * e_out))], axis=1).astype(BF16)
    v_pl = to_pairs(v)
    vbd = bd(v_pl)
    decay_end = to_pairs(jnp.exp(cum_end))[:, 0:1, :]
    aa = _bdot_nt(jnp.concatenate([abar, rbar], axis=1), jnp.concatenate([bt, kt], axis=1))
    a_ab = jnp.where(strict, aa[:, :2 * c, :PAIR], 0.0)
    a_ak = jnp.where(strict, aa[:, :2 * c, PAIR:], 0.0).astype(BF16)
    a_rb = jnp.where(incl, aa[:, 2 * c:, :PAIR], 0.0).astype(BF16)
    a_rk = jnp.where(incl, aa[:, 2 * c:, PAIR:], 0.0).astype(BF16)
    pw = a_ab.astype(BF16)
    tinv = eye + a_ab
    for _ in range(int(math.log2(c)) - 1):
        pw = _bdot(pw, pw).astype(BF16)
        tinv = tinv + _bdot(tinv.astype(BF16), pw)
    akv = _bdot(a_ak, vbd.astype(BF16))
    wu = _bdot(tinv.astype(BF16), jnp.concatenate([abar, akv.astype(BF16)], axis=2))
    w16 = wu[:, :, :PAIR].astype(BF16)
    u_bd = wu[:, :, PAIR:]

    s_bd = state_ref[...]
    e_parts, rs_parts = [], []
    for ci in range(nc):
        sl = slice(ci * npair, (ci + 1) * npair)
        wr = _bdot_nt(jnp.concatenate([w16[sl], rbar[sl]], axis=1), s_bd.astype(BF16))
        e_bd = wr[:, :2 * c] + u_bd[sl]
        e_parts.append(e_bd)
        rs_parts.append(wr[:, 2 * c:])
        ev = jnp.concatenate([e_bd, vbd[sl]], axis=1)
        ds = _bdot(jnp.swapaxes(ev, 1, 2).astype(BF16), bk[sl])
        s_bd = s_bd * decay_end[sl] + jnp.where(same, ds, 0.0)
    state_ref[...] = s_bd

    ev_all = jnp.concatenate([jnp.concatenate(e_parts, axis=0), vbd], axis=1).astype(BF16)
    y_bd = jnp.concatenate(rs_parts, axis=0) + _bdot(jnp.concatenate([a_rb, a_rk], axis=2), ev_all)
    mean = jnp.sum(y_bd, axis=-1, keepdims=True) * (1.0 / R_HEAD_DIM)
    dev = jnp.where(same, y_bd - mean, 0.0)
    var = jnp.sum(dev * dev, axis=-1, keepdims=True) * (1.0 / R_HEAD_DIM)
    yn = dev * lax.rsqrt(var + GN_EPS)
    yn = yn[:, :c] + yn[:, c:]
    rkv = to_pairs(r * k2 * rk_ref[...])
    s0 = jnp.sum(jnp.where(lane0, rkv, 0.0), axis=-1, keepdims=True)
    s1 = jnp.sum(jnp.where(lane0, 0.0, rkv), axis=-1, keepdims=True)
    bonus = jnp.where(lane0, s0, s1) * v_pl
    def per_pair(row_ref):
        rows = [row_ref[:, p * PAIR:(p + 1) * PAIR] for p in range(npair)]
        return jnp.stack(rows * nc, axis=0)

    out = ((yn * per_pair(lnw_ref) + per_pair(lnb_ref) + bonus) * to_pairs(gate)).astype(BF16)
    for ci in range(nc):
        for p in range(npair):
            o_ref[0, ci * c:(ci + 1) * c, p * PAIR:(p + 1) * PAIR] = out[ci * npair + p]


def _rwkv(z, prm):
    b, s, _ = z.shape
    tl = RWKV_TILE

    def seg(width, off):
        return pl.BlockSpec((1, tl, width), lambda bi, i: (bi, i, off // width))

    def full(arr):
        return pl.BlockSpec(arr.shape, lambda bi, i: (0,) * arr.ndim)

    names = ("mu_r", "mu_k", "mu_v", "mu_g", "mu_w", "mu_a", "w0", "a0", "k_k", "k_a", "r_k", "ln_w", "ln_b",
             "w2", "a2", "g2")
    consts = [prm[n] for n in names]
    return pl.pallas_call(
        _rwkv_kernel,
        out_shape=jax.ShapeDtypeStruct((b, s, R_WIDTH), BF16),
        grid=(b, s // tl),
        in_specs=[seg(R_WIDTH, Z_R), seg(R_WIDTH, Z_K), seg(R_WIDTH, Z_V), seg(GATE_LORA, Z_GL),
                  seg(LORA_PAD, Z_WL), seg(LORA_PAD, Z_AL)] + [full(x) for x in consts],
        out_specs=pl.BlockSpec((1, tl, R_WIDTH), lambda bi, i: (bi, i, 0)),
        scratch_shapes=[
            pltpu.VMEM((1, R_WIDTH), F32), pltpu.VMEM((1, R_WIDTH), F32), pltpu.VMEM((1, R_WIDTH), F32),
            pltpu.VMEM((1, GATE_LORA), F32), pltpu.VMEM((1, LORA_PAD), F32), pltpu.VMEM((1, LORA_PAD), F32),
            pltpu.VMEM((R_WIDTH // PAIR, PAIR, PAIR), F32),
        ],
        compiler_params=_params("parallel", "arbitrary"),
        name="rwkv7",
    )(z, z, z, z, z, z, *consts)


def _pad_cols(w, width):
    return jnp.pad(w, ((0, 0), (0, width - w.shape[1])))


def _w_in_layout_kernel(wt_ref, o_ref):
    wt = wt_ref[0]
    cols = wt.shape[1]
    offs = [0]
    for n in (A_WIDTH, KV_RANK, IDX_HEADS * IDX_DIM, IDX_DIM + IDX_HEADS,
              3 * R_WIDTH, DECAY_LORA, AAA_LORA, GATE_LORA):
        offs.append(offs[-1] + n)
    q, ckv, qi, kiwi, rkv, wl, al, gl = (wt[offs[i]:offs[i + 1], :] for i in range(8))

    def padded(x, height):
        return jnp.concatenate([x, jnp.zeros((height - x.shape[0], cols), x.dtype)], axis=0)

    rows = [q, qi, rkv, gl, ckv, padded(kiwi, LORA_PAD), padded(wl, LORA_PAD), padded(al, LORA_PAD),
            jnp.zeros((Z_WIDTH - Z_END, cols), wt.dtype)]
    o_ref[0] = jnp.concatenate(rows, axis=0).T.astype(BF16)


def _layout_w_in(w_in):
    depth, d, p = w_in.shape
    tc = _divisor_tile(d, 256)
    return pl.pallas_call(
        _w_in_layout_kernel,
        out_shape=jax.ShapeDtypeStruct((depth, d, Z_WIDTH), BF16),
        grid=(depth, d // tc),
        in_specs=[pl.BlockSpec((1, p, tc), lambda l, i: (l, 0, i))],
        out_specs=pl.BlockSpec((1, tc, Z_WIDTH), lambda l, i: (l, i, 0)),
        compiler_params=_params("parallel", "parallel"),
        name="w_in_layout",
    )(jnp.swapaxes(w_in, 1, 2))


def _pad_rows(w, rows):
    return jnp.pad(w, ((0, rows - w.shape[0]), (0, 0)))


def _rwkv_params(l, mu, w0, w2, a0, a2, g2, k_k, k_a, r_k, ln_w, ln_b):
    m = mu[l]
    o = [0, R_WIDTH, 2 * R_WIDTH, 3 * R_WIDTH, 3 * R_WIDTH + DECAY_LORA, 3 * R_WIDTH + DECAY_LORA + AAA_LORA]
    row = lambda x: x.reshape(1, -1)
    return {
        "mu_r": row(m[o[0]:o[1]]), "mu_k": row(m[o[1]:o[2]]), "mu_v": row(m[o[2]:o[3]]),
        "mu_w": _pad_cols(row(m[o[3]:o[4]]), LORA_PAD), "mu_a": _pad_cols(row(m[o[4]:o[5]]), LORA_PAD),
        "mu_g": row(m[o[5]:]),
        "w0": row(w0[l]), "a0": row(a0[l]), "k_k": row(k_k[l]), "k_a": row(k_a[l]), "r_k": row(r_k[l]),
        "ln_w": row(ln_w[l]), "ln_b": row(ln_b[l]),
        "w2": _pad_rows(w2[l], LORA_PAD).astype(BF16), "a2": _pad_rows(a2[l], LORA_PAD).astype(BF16),
        "g2": g2[l].astype(BF16),
    }


def kernel(x, c, t5_bias, ada_w, ada_b, norm_g, ffn_w_in, ffn_w_out, w_in, ckv_norm_g, w_uk, w_uv, rwkv_mu, rwkv_w0, rwkv_w2, rwkv_a0, rwkv_a2, rwkv_g2, rwkv_k_k, rwkv_k_a, rwkv_r_k, rwkv_ln_w, rwkv_ln_b, w_out, final_norm_g):
    b, s, d = x.shape
    depth = ada_w.shape[0]
    assert s % DSA_TILE == 0 and s % RWKV_TILE == 0 and d == A_WIDTH + R_WIDTH
    mod = _ada_mod(c, ada_w, ada_b).reshape(depth, b, N_SUB, 3, 1, d)
    bias = _bias_tiles(t5_bias)
    final_g = final_norm_g.reshape(1, d)
    w_in_z = _layout_w_in(w_in)
    w_out16 = w_out.astype(BF16)
    h = x
    for l in range(depth):
        shift = lambda i: mod[l, :, i, 0]
        scale = lambda i: mod[l, :, i, 1]
        gate = lambda i: mod[l, :, i, 2]
        g = lambda i: norm_g[l, i].reshape(1, d)
        h = _ffn(h, g(0), shift(0), scale(0), gate(0), ffn_w_in, ffn_w_out, l, 0, final_g, final_norm=False)
        z = _proj(h, g(1), shift(1), scale(1), w_in_z, l)
        qt, qit, ckv, ckvt, ki, wit = _dsa_prep(z, ckv_norm_g[l].reshape(1, KV_RANK))
        wuk = jnp.transpose(w_uk[l], (1, 0, 2)).astype(BF16)
        wuvt = jnp.transpose(w_uv[l], (1, 2, 0)).astype(BF16)
        o_a = _dsa(qt, qit, ckv, ckvt, ki, wit, wuk, wuvt, bias)
        o_r = _rwkv(z, _rwkv_params(l, rwkv_mu, rwkv_w0, rwkv_w2, rwkv_a0, rwkv_a2, rwkv_g2, rwkv_k_k,
                                    rwkv_k_a, rwkv_r_k, rwkv_ln_w, rwkv_ln_b))
        h = _outproj(o_a, o_r, h, gate(1), w_out16, l)
        h = _ffn(h, g(2), shift(2), scale(2), gate(2), ffn_w_in, ffn_w_out, l, 1, final_g,
                 final_norm=(l == depth - 1))
    return h
```

```python
import functools
import math

import jax
import jax.numpy as jnp
from jax import lax
from jax.experimental import pallas as pl
from jax.experimental.pallas import tpu as pltpu

F32 = jnp.float32
BF16 = jnp.bfloat16
I32 = jnp.int32
I16 = jnp.int16

A_HEADS = 8
A_HEAD_DIM = 128
A_WIDTH = A_HEADS * A_HEAD_DIM
KV_RANK = 256
IDX_HEADS = 16
IDX_DIM = 64
TOPK_MAX = 256
REL_BUCKETS = 32
REL_MAX_EXACT = REL_BUCKETS // 2
REL_MAX_DIST = 128
R_HEAD_DIM = 64
R_WIDTH = 1024
DECAY_LORA = 96
AAA_LORA = 96
GATE_LORA = 256
GN_EPS = 64e-5
RMS_EPS = 1e-6
N_SUB = 3

LANES = 128
SUBLANES = 8
VMEM_LIMIT_BYTES = 56 * 1024 * 1024

LORA_PAD = 128
Z_Q = 0
Z_QI = Z_Q + A_WIDTH
Z_R = Z_QI + IDX_HEADS * IDX_DIM
Z_K = Z_R + R_WIDTH
Z_V = Z_K + R_WIDTH
Z_GL = Z_V + R_WIDTH
Z_CKV = Z_GL + GATE_LORA
Z_KIWI = Z_CKV + KV_RANK
Z_WL = Z_KIWI + LORA_PAD
Z_AL = Z_WL + LORA_PAD
Z_END = Z_AL + LORA_PAD
Z_WIDTH = 6144

FFN_TOKEN_TILE = 1024
FFN_HIDDEN_TILE = 512
PROLOGUE_ROWS = 256
ADALN_SLAB_ROWS = 16
ADALN_UNROLL = 8
DSA_TILE = 256
RWKV_CHUNK = 64
RWKV_TILE = 256
PAIR = 2 * R_HEAD_DIM

I16_MIN = -(2 ** 15)
PACKED_ROWS = 2 * SUBLANES
SUM_ROWS = PACKED_ROWS
BF16_EXACT_INT = 256
COUNT_CHAINS = 4
KEY_NEG_INF = -2139095041
MASKED_LOGIT = -(2.0 ** 100)
LOG2E = math.log2(math.e)
FAR_DISTANCE = math.ceil(REL_MAX_EXACT * (REL_MAX_DIST / REL_MAX_EXACT)
                         ** ((REL_BUCKETS - 1 - REL_MAX_EXACT) / (REL_BUCKETS - REL_MAX_EXACT)))


def _dot(a, b):
    return jnp.dot(a, b, preferred_element_type=F32)


def _bdot(a, b):
    return lax.dot_general(a, b, (((2,), (1,)), ((0,), (0,))), preferred_element_type=F32)


def _bdot_nt(a, b):
    return lax.dot_general(a, b, (((2,), (2,)), ((0,), (0,))), preferred_element_type=F32)


def _rms(x, g, eps):
    ms = jnp.mean(x * x, axis=-1, keepdims=True)
    return x * lax.rsqrt(ms + eps) * g


def _divisor_tile(n, pref):
    if n <= pref:
        return n
    t = (pref // LANES) * LANES
    while t > LANES and n % t:
        t -= LANES
    assert n % t == 0, (n, pref)
    return t


def _params(*sem):
    return pltpu.CompilerParams(dimension_semantics=sem, vmem_limit_bytes=VMEM_LIMIT_BYTES)


def _ada_kernel(c_ref, w_ref, b_ref, o_ref):
    c = c_ref[...]
    ca = (c * jax.nn.sigmoid(c)).astype(BF16)
    o_ref[0] = _dot(ca, w_ref[0].astype(BF16)) + b_ref[0]


def _ada_mod(c, ada_w, ada_b):
    depth, d, n = ada_w.shape
    b = c.shape[0]
    bp = -(-b // SUBLANES) * SUBLANES
    cp = jnp.pad(c, ((0, bp - b), (0, 0)))
    tn = _divisor_tile(n, 1024)
    out = pl.pallas_call(
        _ada_kernel,
        out_shape=jax.ShapeDtypeStruct((depth, bp, n), F32),
        grid=(depth, n // tn),
        in_specs=[
            pl.BlockSpec((bp, d), lambda l, j: (0, 0)),
            pl.BlockSpec((1, d, tn), lambda l, j: (l, 0, j)),
            pl.BlockSpec((1, 1, tn), lambda l, j: (l, 0, j)),
        ],
        out_specs=pl.BlockSpec((1, bp, tn), lambda l, j: (l, 0, j)),
        compiler_params=_params("arbitrary", "arbitrary"),
        name="ada_mod",
    )(cp, ada_w, ada_b.reshape(depth, 1, n))
    return out[:, :b]


def _adaln_rows(h_ref, g_ref, scale_ref, shift_ref, hn_ref, base, copy_ref=None):
    rows = h_ref.shape[1]
    g = g_ref[...]
    mul = 1.0 + scale_ref[0]
    add = shift_ref[0]

    def slab(i, carry):
        src = pl.ds(pl.multiple_of(i * ADALN_SLAB_ROWS, ADALN_SLAB_ROWS), ADALN_SLAB_ROWS)
        dst = pl.ds(pl.multiple_of(base + i * ADALN_SLAB_ROWS, ADALN_SLAB_ROWS), ADALN_SLAB_ROWS)
        h = h_ref[0, src, :]
        hn_ref[dst, :] = (_rms(h, g, RMS_EPS) * mul + add).astype(BF16)
        if copy_ref is not None:
            copy_ref[0, dst, :] = h
        return carry

    lax.fori_loop(0, rows // ADALN_SLAB_ROWS, slab, 0, unroll=ADALN_UNROLL)


def _prologue_split(tm):
    rows = _divisor_tile(tm, PROLOGUE_ROWS)
    return rows, tm // rows


def _ffn_kernel(h_ref, g_ref, shift_ref, scale_ref, gate_ref, wg_ref, wu_ref, wo_ref, fg_ref,
                o_ref, hn_ref, *, final_norm, pro_steps):
    f = pl.program_id(2)

    @pl.when(f < pro_steps)
    def _():
        _adaln_rows(h_ref, g_ref, scale_ref, shift_ref, hn_ref, f * h_ref.shape[1], o_ref)

    @pl.when(f >= pro_steps)
    def _():
        hn = hn_ref[...]
        g = _dot(hn, wg_ref[...].astype(BF16))
        u = _dot(hn, wu_ref[...].astype(BF16))
        act = (g * jax.nn.sigmoid(g) * u).astype(BF16)
        o_ref[0] += (0.5 * gate_ref[0]) * _dot(act, wo_ref[...].astype(BF16))

    if final_norm:
        @pl.when(f == pl.num_programs(2) - 1)
        def _():
            o_ref[0] = _rms(o_ref[0], fg_ref[...], RMS_EPS)


def _ffn(h, g, shift, scale, gate, w_in, w_out, layer, which, final_g, *, final_norm):
    b, s, d = h.shape
    ff = w_out.shape[2]
    tm = _divisor_tile(s, FFN_TOKEN_TILE)
    tf = _divisor_tile(ff, FFN_HIDDEN_TILE)
    nf = ff // tf
    hr, pro = _prologue_split(tm)
    vec = pl.BlockSpec((1, 1, d), lambda bi, i, f: (bi, 0, 0))
    row = pl.BlockSpec((1, d), lambda bi, i, f: (0, 0))

    def wf(f):
        return jnp.maximum(f - pro, 0)

    return pl.pallas_call(
        functools.partial(_ffn_kernel, final_norm=final_norm, pro_steps=pro),
        out_shape=jax.ShapeDtypeStruct((b, s, d), F32),
        grid=(b, s // tm, pro + nf),
        in_specs=[
            pl.BlockSpec((1, hr, d), lambda bi, i, f: (bi, i * pro + jnp.minimum(f, pro - 1), 0)),
            row, vec, vec, vec,
            pl.BlockSpec((None, None, d, tf), lambda bi, i, f: (layer, which, 0, wf(f))),
            pl.BlockSpec((None, None, d, tf), lambda bi, i, f: (layer, which, 0, nf + wf(f))),
            pl.BlockSpec((None, None, tf, d), lambda bi, i, f: (layer, which, wf(f), 0)),
            row,
        ],
        out_specs=pl.BlockSpec((1, tm, d), lambda bi, i, f: (bi, i, 0)),
        scratch_shapes=[pltpu.VMEM((tm, d), BF16)],
        compiler_params=_params("parallel", "parallel", "arbitrary"),
        name="ffn",
    )(h, g, shift, scale, gate, w_in, w_in, w_out, final_g)


def _proj_kernel(h_ref, g_ref, shift_ref, scale_ref, w_ref, o_ref, hn_ref):
    @pl.when(pl.program_id(2) == 0)
    def _():
        _adaln_rows(h_ref, g_ref, scale_ref, shift_ref, hn_ref, 0)

    o_ref[0] = _dot(hn_ref[...], w_ref[...])


def _proj(h, g, shift, scale, w, layer):
    b, s, d = h.shape
    p = w.shape[2]
    tm = _divisor_tile(s, 1024)
    tn = _divisor_tile(p, 1024)
    vec = pl.BlockSpec((1, 1, d), lambda bi, i, n: (bi, 0, 0))
    return pl.pallas_call(
        _proj_kernel,
        out_shape=jax.ShapeDtypeStruct((b, s, p), F32),
        grid=(b, s // tm, p // tn),
        in_specs=[
            pl.BlockSpec((1, tm, d), lambda bi, i, n: (bi, i, 0)),
            pl.BlockSpec((1, d), lambda bi, i, n: (0, 0)),
            vec, vec,
            pl.BlockSpec((None, d, tn), lambda bi, i, n: (layer, 0, n)),
        ],
        out_specs=pl.BlockSpec((1, tm, tn), lambda bi, i, n: (bi, i, n)),
        scratch_shapes=[pltpu.VMEM((tm, d), BF16)],
        compiler_params=_params("parallel", "parallel", "arbitrary"),
        name="proj",
    )(h, g, shift, scale, w)


def _outproj_kernel(oa_ref, or_ref, h_ref, gate_ref, wa_ref, wr_ref, o_ref):
    acc = _dot(oa_ref[0], wa_ref[...]) + _dot(or_ref[0], wr_ref[...])
    o_ref[0] = h_ref[0] + gate_ref[0] * acc


def _outproj(o_a, o_r, h, gate, w_out, layer):
    b, s, d = h.shape
    tm = _divisor_tile(s, 512)
    tn = d
    wa = o_a.shape[-1]
    wr = o_r.shape[-1]
    return pl.pallas_call(
        _outproj_kernel,
        out_shape=jax.ShapeDtypeStruct((b, s, d), F32),
        grid=(b, s // tm, d // tn),
        in_specs=[
            pl.BlockSpec((1, tm, wa), lambda bi, i, n: (bi, i, 0)),
            pl.BlockSpec((1, tm, wr), lambda bi, i, n: (bi, i, 0)),
            pl.BlockSpec((1, tm, tn), lambda bi, i, n: (bi, i, n)),
            pl.BlockSpec((1, 1, tn), lambda bi, i, n: (bi, 0, n)),
            pl.BlockSpec((None, wa, tn), lambda bi, i, n: (layer, 0, n)),
            pl.BlockSpec((None, wr, tn), lambda bi, i, n: (layer, wa // wr, n)),
        ],
        out_specs=pl.BlockSpec((1, tm, tn), lambda bi, i, n: (bi, i, n)),
        compiler_params=_params("parallel", "parallel", "arbitrary"),
        name="outproj",
    )(o_a, o_r, h, gate, w_out, w_out)


def _dsa_prep_kernel(q_ref, qi_ref, ckv_ref, kiwi_ref, g_ref, qt_ref, qit_ref, ckv_o, ckvt_o, ki_o, wit_o):
    qt_ref[0] = q_ref[0].T.astype(BF16)
    qit_ref[0] = qi_ref[0].T.astype(BF16)
    cn = _rms(ckv_ref[0], g_ref[...], RMS_EPS)
    ckv_o[0] = cn.astype(BF16)
    ckvt_o[0, 0] = jnp.concatenate([cn.T.astype(BF16), jnp.ones((SUM_ROWS, cn.shape[0]), BF16)], axis=0)
    kw = kiwi_ref[0]
    ki_o[0] = kw[:, :IDX_DIM].astype(BF16)
    wit_o[0] = kw.T[IDX_DIM:IDX_DIM + IDX_HEADS, :] * (IDX_HEADS * IDX_DIM) ** -0.5


def _dsa_prep(z, ckv_g):
    b, s, _ = z.shape
    t = DSA_TILE
    nt = s // t

    def seg(width, off):
        return pl.BlockSpec((1, t, width), lambda bi, i: (bi, i, off // width))

    return pl.pallas_call(
        _dsa_prep_kernel,
        out_shape=(
            jax.ShapeDtypeStruct((b, A_WIDTH, s), BF16),
            jax.ShapeDtypeStruct((b, IDX_HEADS * IDX_DIM, s), BF16),
            jax.ShapeDtypeStruct((b, s, KV_RANK), BF16),
            jax.ShapeDtypeStruct((b, nt, KV_RANK + SUM_ROWS, t), BF16),
            jax.ShapeDtypeStruct((b, s, IDX_DIM), BF16),
            jax.ShapeDtypeStruct((b, IDX_HEADS, s), F32),
        ),
        grid=(b, nt),
        in_specs=[
            seg(A_WIDTH, Z_Q), seg(IDX_HEADS * IDX_DIM, Z_QI), seg(KV_RANK, Z_CKV), seg(LORA_PAD, Z_KIWI),
            pl.BlockSpec((1, KV_RANK), lambda bi, i: (0, 0)),
        ],
        out_specs=(
            pl.BlockSpec((1, A_WIDTH, t), lambda bi, i: (bi, 0, i)),
            pl.BlockSpec((1, IDX_HEADS * IDX_DIM, t), lambda bi, i: (bi, 0, i)),
            pl.BlockSpec((1, t, KV_RANK), lambda bi, i: (bi, i, 0)),
            pl.BlockSpec((1, 1, KV_RANK + SUM_ROWS, t), lambda bi, i: (bi, i, 0, 0)),
            pl.BlockSpec((1, t, IDX_DIM), lambda bi, i: (bi, i, 0)),
            pl.BlockSpec((1, IDX_HEADS, t), lambda bi, i: (bi, 0, i)),
        ),
        compiler_params=_params("parallel", "parallel"),
        name="dsa_prep",
    )(z, z, z, z, ckv_g)


def _bias_kernel(t5_ref, o_ref):
    t = DSA_TILE
    j = lax.broadcasted_iota(I32, (t, t), 0)
    i = lax.broadcasted_iota(I32, (t, t), 1)
    for didx in range(2):
        n = jnp.maximum(didx * t + i - j, 0)
        nf = jnp.maximum(n, 1).astype(F32)
        large = REL_MAX_EXACT + (jnp.log(nf / REL_MAX_EXACT) / math.log(REL_MAX_DIST / REL_MAX_EXACT)
                                 * (REL_BUCKETS - REL_MAX_EXACT)).astype(I32)
        large = jnp.minimum(large, REL_BUCKETS - 1)
        bucket = jnp.where(n < REL_MAX_EXACT, n, large)
        for h in range(A_HEADS):
            val = jnp.zeros((t, t), F32)
            for k in range(REL_BUCKETS):
                val = jnp.where(bucket == k, t5_ref[k, h], val)
            o_ref[(1 - didx) * t:(2 - didx) * t, h * t:(h + 1) * t] = (val - t5_ref[REL_BUCKETS - 1, h]) * LOG2E


def _bias_tiles(t5_bias):
    t = DSA_TILE
    assert t + 1 >= FAR_DISTANCE
    return pl.pallas_call(
        _bias_kernel,
        out_shape=jax.ShapeDtypeStruct((2 * t, A_HEADS * t), F32),
        in_specs=[pl.BlockSpec(memory_space=pltpu.SMEM)],
        out_specs=pl.BlockSpec(memory_space=pltpu.VMEM),
        compiler_params=pltpu.CompilerParams(vmem_limit_bytes=VMEM_LIMIT_BYTES),
        name="t5_bias_tiles",
    )(t5_bias)


def _dsa_kernel(qt_ref, qit_ref, ckv_ref, ckvt_ref, ki_ref, wit_ref, wuk_ref, wuvt_ref, bias_ref,
                o_ref, keys_ref, khi_ref, klo_ref, acc_ref, ot_ref, *, topk, seq_len):
    t = DSA_TILE
    qb = pl.program_id(1)
    nk = qb + 1
    row = lax.broadcasted_iota(I32, (t, t), 0)
    col = lax.broadcasted_iota(I32, (t, t), 1)

    def idx_rows(start, rows, diagonal):
        sl = pl.ds(pl.multiple_of(start, rows), rows)
        kic = ki_ref[0, sl, :]
        acc = jnp.zeros((rows, t), F32)
        for h in range(IDX_HEADS):
            rel = _dot(kic, qit_ref[0, h * IDX_DIM:(h + 1) * IDX_DIM, :])
            acc = acc + jnp.maximum(rel, 0.0) * wit_ref[0, h:h + 1, :]
        acc = acc + 0.0
        bits = pltpu.bitcast(acc, I32)
        key = bits ^ ((bits >> 31) & 0x7FFFFFFF)
        if diagonal:
            key = jnp.where(row <= col, key, KEY_NEG_INF)
        keys_ref[sl, :] = key
        khi_ref[sl, :] = (key >> 16).astype(I16)
        klo_ref[sl, :] = ((key & 0xFFFF) + I16_MIN).astype(I16)

    def idx_group(chunks):
        def body(i, first_chunk):
            idx_rows(first_chunk * t, chunks * t, False)
            return first_chunk + chunks
        return body

    done = lax.fori_loop(0, qb // 4, idx_group(4), 0)
    done = lax.fori_loop(0, (qb // 2) % 2, idx_group(2), done)
    lax.fori_loop(0, qb % 2, idx_group(1), done)
    idx_rows(qb * t, t, True)

    def count16(ref, cand):
        cand16 = cand.astype(I16)

        def body(kc, accs):
            x = ref[pl.ds(pl.multiple_of(kc * t, t), t), :]
            ones = jnp.where(x >= cand16, jnp.asarray(1, BF16), jnp.asarray(0, BF16))
            ones = ones.reshape(t // PACKED_ROWS, PACKED_ROWS, t)
            accs = list(accs)
            for i in range(t // PACKED_ROWS):
                accs[i % COUNT_CHAINS] = accs[i % COUNT_CHAINS] + ones[i]
            return tuple(accs)

        accs = (jnp.zeros((PACKED_ROWS, t), BF16),) * COUNT_CHAINS
        accs = lax.fori_loop(0, nk // 2, lambda i, a: body(2 * i + 1, body(2 * i, a)), accs)
        accs = lax.fori_loop(2 * (nk // 2), nk, body, accs)
        total = sum(a.astype(F32) for a in accs)
        return jnp.sum(total, axis=0, keepdims=True).astype(I32)

    def search16(ref, base):
        def accept(cand, cur):
            return jnp.where(base + count16(ref, cand) >= topk, cand, cur)

        v = accept(jnp.zeros((1, t), I32), jnp.full((1, t), I16_MIN, I32))
        return lax.fori_loop(0, 15, lambda i, v: accept(v | (jnp.int32(1) << (14 - i)), v), v)

    hi = search16(khi_ref, 0)
    above = count16(khi_ref, jnp.minimum(hi + 1, -I16_MIN - 1))
    hi16 = hi.astype(I16)

    def keep_low(kc, carry):
        sl = pl.ds(pl.multiple_of(kc * t, t), t)
        klo_ref[sl, :] = jnp.where(khi_ref[sl, :] == hi16, klo_ref[sl, :], jnp.asarray(I16_MIN, I16))
        return carry

    lax.fori_loop(0, nk, keep_low, 0)
    lo = search16(klo_ref, above)
    thr = jnp.maximum((hi << 16) | (lo - I16_MIN), KEY_NEG_INF + 1)

    def count(pred):
        def body(kc, acc):
            k = keys_ref[pl.ds(pl.multiple_of(kc * t, t), t), :]
            m = jnp.where(pred(k, kc), 1, 0).astype(I32)
            return acc + jnp.sum(m.reshape(t // SUBLANES, SUBLANES, t), axis=0)

        acc = lax.fori_loop(0, nk, body, jnp.zeros((SUBLANES, t), I32))
        return jnp.sum(acc, axis=0, keepdims=True)

    @pl.when(jnp.max(count(lambda k, kc: k >= thr)) > topk)
    def _():
        need = topk - count(lambda k, kc: k > thr)
        nbits = (seq_len - 1).bit_length()

        def pos_bit(i, y):
            cand = y | (jnp.int32(1) << (nbits - 1 - i))
            before = count(lambda k, kc: (k == thr) & ((kc * t + row) < cand))
            return jnp.where(before < need, cand, y)

        last_kept = lax.fori_loop(0, nbits, pos_bit, jnp.zeros((1, t), I32))

        def demote(kc, carry):
            sl = pl.ds(pl.multiple_of(kc * t, t), t)
            k = keys_ref[sl, :]
            keys_ref[sl, :] = jnp.where((k == thr) & ((kc * t + row) > last_kept), k - 1, k)
            return carry

        lax.fori_loop(0, nk, demote, 0)

    scale = A_HEAD_DIM ** -0.5 * LOG2E

    hw = A_HEADS * t
    qlat = jnp.concatenate(
        [(_dot(wuk_ref[h], qt_ref[0, h * A_HEAD_DIM:(h + 1) * A_HEAD_DIM, :]) * scale).astype(BF16)
         for h in range(A_HEADS)], axis=1)
    acc_ref[...] = jnp.zeros_like(acc_ref)

    def att_chunks(kc, m, *, chunks, bias_row=None):
        rows = chunks * t
        sl = pl.ds(pl.multiple_of(kc * t, t), rows)
        raw = _dot(ckv_ref[0, sl, :], qlat)
        sel = keys_ref[sl, :] >= thr
        parts = []
        for h in range(A_HEADS):
            lg_h = raw[:, h * t:(h + 1) * t]
            if bias_row is not None:
                lg_h = lg_h + bias_ref[bias_row:bias_row + rows, h * t:(h + 1) * t]
            parts.append(jnp.where(sel, lg_h, MASKED_LOGIT).astype(BF16))
        lg = jnp.concatenate(parts, axis=1)
        m_new = jnp.maximum(m, jnp.max(lg, axis=0, keepdims=True).astype(F32))
        p = jnp.exp2(lg - m_new.astype(BF16))
        acc = acc_ref[...] * jnp.exp2(m - m_new)
        for j in range(chunks):
            acc = acc + _dot(ckvt_ref[0, kc + j], p[j * t:(j + 1) * t])
        acc_ref[...] = acc
        return m_new

    n_far = jnp.maximum(qb - 1, 0)
    has_prev = jnp.minimum(qb, 1)
    m = jnp.full((1, hw), MASKED_LOGIT, F32)
    def far_group(chunks):
        def body(i, carry):
            first_chunk, m = carry
            return first_chunk + chunks, att_chunks(first_chunk, m, chunks=chunks)
        return body

    carry = lax.fori_loop(0, n_far // 4, far_group(4), (0, m))
    carry = lax.fori_loop(0, (n_far // 2) % 2, far_group(2), carry)
    _, m = lax.fori_loop(0, n_far % 2, far_group(1), carry)
    m = lax.fori_loop(0, has_prev, lambda i, m: att_chunks(qb - 1, m, chunks=2, bias_row=0), m)
    lax.fori_loop(0, 1 - has_prev, lambda i, m: att_chunks(qb, m, chunks=1, bias_row=t), m)
    olat = (acc_ref[:KV_RANK, :] / acc_ref[KV_RANK:KV_RANK + 1, :]).astype(BF16)
    for h in range(A_HEADS):
        ot_ref[h * A_HEAD_DIM:(h + 1) * A_HEAD_DIM, :] = _dot(wuvt_ref[h], olat[:, h * t:(h + 1) * t])
    o_ref[0] = ot_ref[...].T.astype(BF16)


def _dsa(qt, qit, ckv, ckvt, ki, wit, wuk, wuvt, bias):
    b, _, s = qt.shape
    t = DSA_TILE
    nt = s // t
    topk = min(TOPK_MAX, s // 4)
    assert s // PACKED_ROWS <= BF16_EXACT_INT
    return pl.pallas_call(
        functools.partial(_dsa_kernel, topk=topk, seq_len=s),
        out_shape=jax.ShapeDtypeStruct((b, s, A_WIDTH), BF16),
        grid=(b, nt),
        in_specs=[
            pl.BlockSpec((1, A_WIDTH, t), lambda bi, i: (bi, 0, i)),
            pl.BlockSpec((1, IDX_HEADS * IDX_DIM, t), lambda bi, i: (bi, 0, i)),
            pl.BlockSpec((1, s, KV_RANK), lambda bi, i: (bi, 0, 0)),
            pl.BlockSpec((1, nt, KV_RANK + SUM_ROWS, t), lambda bi, i: (bi, 0, 0, 0)),
            pl.BlockSpec((1, s, IDX_DIM), lambda bi, i: (bi, 0, 0)),
            pl.BlockSpec((1, IDX_HEADS, t), lambda bi, i: (bi, 0, i)),
            pl.BlockSpec((A_HEADS, KV_RANK, A_HEAD_DIM), lambda bi, i: (0, 0, 0)),
            pl.BlockSpec((A_HEADS, A_HEAD_DIM, KV_RANK), lambda bi, i: (0, 0, 0)),
            pl.BlockSpec((2 * t, A_HEADS * t), lambda bi, i: (0, 0)),
        ],
        out_specs=pl.BlockSpec((1, t, A_WIDTH), lambda bi, i: (bi, i, 0)),
        scratch_shapes=[pltpu.VMEM((s, t), I32), pltpu.VMEM((s, t), I16), pltpu.VMEM((s, t), I16),
                        pltpu.VMEM((KV_RANK + SUM_ROWS, A_HEADS * t), F32), pltpu.VMEM((A_WIDTH, t), F32)],
        compiler_params=_params("parallel", "arbitrary"),
        name="dsa",
    )(qt, qit, ckv, ckvt, ki, wit, wuk, wuvt, bias)


def _split3(x):
    hi = x.astype(BF16)
    r1 = x - hi.astype(F32)
    mid = r1.astype(BF16)
    lo = (r1 - mid.astype(F32)).astype(BF16)
    return hi, mid, lo


def _rwkv_kernel(r_ref, k_ref, v_ref, gl_ref, wl_ref, al_ref,
                 mur_ref, muk_ref, muv_ref, mug_ref, muw_ref, mua_ref,
                 w0_ref, a0_ref, kk_ref, ka_ref, rk_ref, lnw_ref, lnb_ref,
                 w2_ref, a2_ref, g2_ref,
                 o_ref,
                 pr_ref, pk_ref, pv_ref, pg_ref, pw_ref, pa_ref, state_ref):
    tl = RWKV_TILE
    c = RWKV_CHUNK
    nc = tl // c
    npair = R_WIDTH // PAIR

    @pl.when(pl.program_id(1) == 0)
    def _():
        state_ref[...] = jnp.zeros_like(state_ref)
        for ref in (pr_ref, pk_ref, pv_ref, pg_ref, pw_ref, pa_ref):
            ref[...] = jnp.zeros_like(ref)

    def shift_mix(x_ref, prev_ref, mu_ref):
        x = x_ref[0]
        first = lax.broadcasted_iota(I32, x.shape, 0) == 0
        xprev = jnp.where(first, prev_ref[...], pltpu.roll(x, 1, axis=0))
        prev_ref[...] = x[tl - 1:tl, :]
        return x + (xprev - x) * mu_ref[...]

    r = shift_mix(r_ref, pr_ref, mur_ref)
    k = shift_mix(k_ref, pk_ref, muk_ref)
    v = shift_mix(v_ref, pv_ref, muv_ref)
    gl = shift_mix(gl_ref, pg_ref, mug_ref)
    wl = shift_mix(wl_ref, pw_ref, muw_ref)
    al = shift_mix(al_ref, pa_ref, mua_ref)

    w_lin = w0_ref[...] + _dot(jnp.tanh(wl).astype(BF16), w2_ref[...])
    nx = -w_lin
    softplus = jnp.maximum(nx, 0.0) + jnp.log(1.0 + jnp.exp(-jnp.abs(nx)))
    ld = -jnp.exp(-softplus - 0.5)
    a = jax.nn.sigmoid(a0_ref[...] + _dot(al.astype(BF16), a2_ref[...]))
    gate = _dot(jax.nn.sigmoid(gl).astype(BF16), g2_ref[...])

    li = lax.broadcasted_iota(I32, (LANES, LANES), 0) // R_HEAD_DIM
    lj = lax.broadcasted_iota(I32, (LANES, LANES), 1) // R_HEAD_DIM
    head_ones = jnp.where(li == lj, 1.0, 0.0).astype(BF16)

    def head_sum(x):
        parts = []
        for j in range(R_WIDTH // LANES):
            hi, mid, lo = _split3(x[:, j * LANES:(j + 1) * LANES])
            parts.append(_dot(hi, head_ones) + _dot(mid, head_ones) + _dot(lo, head_ones))
        return jnp.concatenate(parts, axis=-1)

    kk = k * kk_ref[...]
    kk = kk / jnp.maximum(jnp.sqrt(head_sum(kk * kk)), 1e-12)
    k2 = k * (1.0 + (a - 1.0) * ka_ref[...])
    bb = kk * a

    ti = lax.broadcasted_iota(I32, (tl, tl), 0)
    tj = lax.broadcasted_iota(I32, (tl, tl), 1)
    tri = jnp.where((ti // c == tj // c) & (tj <= ti), 1.0, 0.0).astype(BF16)
    hi, mid, lo = _split3(ld)
    cum = _dot(tri, hi) + _dot(tri, mid) + _dot(tri, lo)
    cum_end = jnp.broadcast_to(cum.reshape(nc, c, R_WIDTH)[:, c - 1:c, :], (nc, c, R_WIDTH)).reshape(tl, R_WIDTH)
    e_in = jnp.exp(cum)
    e_neg = jnp.exp(-cum)
    e_out = jnp.exp(cum_end - cum)

    nb = nc * npair

    def to_pairs(x):
        x3 = x.reshape(nc, c, R_WIDTH)
        xs = jnp.stack([x3[:, :, p * PAIR:(p + 1) * PAIR] for p in range(npair)], axis=1)
        return xs.reshape(nb, c, PAIR)

    lane_head = lax.broadcasted_iota(I32, (2 * c, PAIR), 1) // R_HEAD_DIM
    row_head = lax.broadcasted_iota(I32, (2 * c, PAIR), 0) // c
    same = lane_head == row_head
    rt = lax.broadcasted_iota(I32, (2 * c, PAIR), 0) % c
    ct = lax.broadcasted_iota(I32, (2 * c, PAIR), 1) % c
    strict = same & (ct < rt)
    incl = same & (ct <= rt)
    eye = jnp.where(same & (ct == rt), 1.0, 0.0).astype(F32)

    def bd(x):
        return jnp.where(same, jnp.concatenate([x, x], axis=1), 0.0)

    lane0 = lax.broadcasted_iota(I32, (c, PAIR), 1) < R_HEAD_DIM

    abar = bd(to_pairs(-kk * jnp.exp(cum - ld))).astype(BF16)
    rbar = bd(to_pairs(r * e_in)).astype(BF16)
    bt = bd(to_pairs(bb * e_neg)).astype(BF16)
    kt = bd(to_pairs(k2 * e_neg)).astype(BF16)
    bk = jnp.concatenate([bd(to_pairs(bb * e_out)), bd(to_pairs(k2 * e_out))], axis=1).astype(BF16)
    v_pl = to_pairs(v)
    vbd = bd(v_pl)
    decay_end = to_pairs(jnp.exp(cum_end))[:, 0:1, :]
    aa = _bdot_nt(jnp.concatenate([abar, rbar], axis=1), jnp.concatenate([bt, kt], axis=1))
    a_ab = jnp.where(strict, aa[:, :2 * c, :PAIR], 0.0)
    a_ak = jnp.where(strict, aa[:, :2 * c, PAIR:], 0.0).astype(BF16)
    a_rb = jnp.where(incl, aa[:, 2 * c:, :PAIR], 0.0).astype(BF16)
    a_rk = jnp.where(incl, aa[:, 2 * c:, PAIR:], 0.0).astype(BF16)
    pw = a_ab.astype(BF16)
    tinv = eye + a_ab
    for _ in range(int(math.log2(c)) - 1):
        pw = _bdot(pw, pw).astype(BF16)
        tinv = tinv + _bdot(tinv.astype(BF16), pw)
    akv = _bdot(a_ak, vbd.astype(BF16))
    wu = _bdot(tinv.astype(BF16), jnp.concatenate([abar, akv.astype(BF16)], axis=2))
    w16 = wu[:, :, :PAIR].astype(BF16)
    u_bd = wu[:, :, PAIR:]

    s_bd = state_ref[...]
    e_parts, rs_parts = [], []
    for ci in range(nc):
        sl = slice(ci * npair, (ci + 1) * npair)
        wr = _bdot_nt(jnp.concatenate([w16[sl], rbar[sl]], axis=1), s_bd.astype(BF16))
        e_bd = wr[:, :2 * c] + u_bd[sl]
        e_parts.append(e_bd)
        rs_parts.append(wr[:, 2 * c:])
        ev = jnp.concatenate([e_bd, vbd[sl]], axis=1)
        ds = _bdot(jnp.swapaxes(ev, 1, 2).astype(BF16), bk[sl])
        s_bd = s_bd * decay_end[sl] + jnp.where(same, ds, 0.0)
    state_ref[...] = s_bd

    ev_all = jnp.concatenate([jnp.concatenate(e_parts, axis=0), vbd], axis=1).astype(BF16)
    y_bd = jnp.concatenate(rs_parts, axis=0) + _bdot(jnp.concatenate([a_rb, a_rk], axis=2), ev_all)
    mean = jnp.sum(y_bd, axis=-1, keepdims=True) * (1.0 / R_HEAD_DIM)
    dev = jnp.where(same, y_bd - mean, 0.0)
    var = jnp.sum(dev * dev, axis=-1, keepdims=True) * (1.0 / R_HEAD_DIM)
    yn = dev * lax.rsqrt(var + GN_EPS)
    yn = yn[:, :c] + yn[:, c:]
    rkv = to_pairs(r * k2 * rk_ref[...])
    s0 = jnp.sum(jnp.where(lane0, rkv, 0.0), axis=-1, keepdims=True)
    s1 = jnp.sum(jnp.where(lane0, 0.0, rkv), axis=-1, keepdims=True)
    bonus = jnp.where(lane0, s0, s1) * v_pl
    def per_pair(row_ref):
        rows = [row_ref[:, p * PAIR:(p + 1) * PAIR] for p in range(npair)]
        return jnp.stack(rows * nc, axis=0)

    out = ((yn * per_pair(lnw_ref) + per_pair(lnb_ref) + bonus) * to_pairs(gate)).astype(BF16)
    for ci in range(nc):
        for p in range(npair):
            o_ref[0, ci * c:(ci + 1) * c, p * PAIR:(p + 1) * PAIR] = out[ci * npair + p]


def _rwkv(z, prm):
    b, s, _ = z.shape
    tl = RWKV_TILE

    def seg(width, off):
        return pl.BlockSpec((1, tl, width), lambda bi, i: (bi, i, off // width))

    def full(arr):
        return pl.BlockSpec(arr.shape, lambda bi, i: (0,) * arr.ndim)

    names = ("mu_r", "mu_k", "mu_v", "mu_g", "mu_w", "mu_a", "w0", "a0", "k_k", "k_a", "r_k", "ln_w", "ln_b",
             "w2", "a2", "g2")
    consts = [prm[n] for n in names]
    return pl.pallas_call(
        _rwkv_kernel,
        out_shape=jax.ShapeDtypeStruct((b, s, R_WIDTH), BF16),
        grid=(b, s // tl),
        in_specs=[seg(R_WIDTH, Z_R), seg(R_WIDTH, Z_K), seg(R_WIDTH, Z_V), seg(GATE_LORA, Z_GL),
                  seg(LORA_PAD, Z_WL), seg(LORA_PAD, Z_AL)] + [full(x) for x in consts],
        out_specs=pl.BlockSpec((1, tl, R_WIDTH), lambda bi, i: (bi, i, 0)),
        scratch_shapes=[
            pltpu.VMEM((1, R_WIDTH), F32), pltpu.VMEM((1, R_WIDTH), F32), pltpu.VMEM((1, R_WIDTH), F32),
            pltpu.VMEM((1, GATE_LORA), F32), pltpu.VMEM((1, LORA_PAD), F32), pltpu.VMEM((1, LORA_PAD), F32),
            pltpu.VMEM((R_WIDTH // PAIR, PAIR, PAIR), F32),
        ],
        compiler_params=_params("parallel", "arbitrary"),
        name="rwkv7",
    )(z, z, z, z, z, z, *consts)


def _pad_cols(w, width):
    return jnp.pad(w, ((0, 0), (0, width - w.shape[1])))


def _w_in_layout_kernel(wt_ref, o_ref):
    wt = wt_ref[0]
    cols = wt.shape[1]
    offs = [0]
    for n in (A_WIDTH, KV_RANK, IDX_HEADS * IDX_DIM, IDX_DIM + IDX_HEADS,
              3 * R_WIDTH, DECAY_LORA, AAA_LORA, GATE_LORA):
        offs.append(offs[-1] + n)
    q, ckv, qi, kiwi, rkv, wl, al, gl = (wt[offs[i]:offs[i + 1], :] for i in range(8))

    def padded(x, height):
        return jnp.concatenate([x, jnp.zeros((height - x.shape[0], cols), x.dtype)], axis=0)

    rows = [q, qi, rkv, gl, ckv, padded(kiwi, LORA_PAD), padded(wl, LORA_PAD), padded(al, LORA_PAD),
            jnp.zeros((Z_WIDTH - Z_END, cols), wt.dtype)]
    o_ref[0] = jnp.concatenate(rows, axis=0).T.astype(BF16)


def _layout_w_in(w_in):
    depth, d, p = w_in.shape
    tc = _divisor_tile(d, 256)
    return pl.pallas_call(
        _w_in_layout_kernel,
        out_shape=jax.ShapeDtypeStruct((depth, d, Z_WIDTH), BF16),
        grid=(depth, d // tc),
        in_specs=[pl.BlockSpec((1, p, tc), lambda l, i: (l, 0, i))],
        out_specs=pl.BlockSpec((1, tc, Z_WIDTH), lambda l, i: (l, i, 0)),
        compiler_params=_params("parallel", "parallel"),
        name="w_in_layout",
    )(jnp.swapaxes(w_in, 1, 2))


def _pad_rows(w, rows):
    return jnp.pad(w, ((0, rows - w.shape[0]), (0, 0)))


def _rwkv_params(l, mu, w0, w2, a0, a2, g2, k_k, k_a, r_k, ln_w, ln_b):
    m = mu[l]
    o = [0, R_WIDTH, 2 * R_WIDTH, 3 * R_WIDTH, 3 * R_WIDTH + DECAY_LORA, 3 * R_WIDTH + DECAY_LORA + AAA_LORA]
    row = lambda x: x.reshape(1, -1)
    return {
        "mu_r": row(m[o[0]:o[1]]), "mu_k": row(m[o[1]:o[2]]), "mu_v": row(m[o[2]:o[3]]),
        "mu_w": _pad_cols(row(m[o[3]:o[4]]), LORA_PAD), "mu_a": _pad_cols(row(m[o[4]:o[5]]), LORA_PAD),
        "mu_g": row(m[o[5]:]),
        "w0": row(w0[l]), "a0": row(a0[l]), "k_k": row(k_k[l]), "k_a": row(k_a[l]), "r_k": row(r_k[l]),
        "ln_w": row(ln_w[l]), "ln_b": row(ln_b[l]),
        "w2": _pad_rows(w2[l], LORA_PAD).astype(BF16), "a2": _pad_rows(a2[l], LORA_PAD).astype(BF16),
        "g2": g2[l].astype(BF16),
    }


def kernel(x, c, t5_bias, ada_w, ada_b, norm_g, ffn_w_in, ffn_w_out, w_in, ckv_norm_g, w_uk, w_uv, rwkv_mu, rwkv_w0, rwkv_w2, rwkv_a0, rwkv_a2, rwkv_g2, rwkv_k_k, rwkv_k_a, rwkv_r_k, rwkv_ln_w, rwkv_ln_b, w_out, final_norm_g):
    b, s, d = x.shape
    depth = ada_w.shape[0]
    assert s % DSA_TILE == 0 and s % RWKV_TILE == 0 and d == A_WIDTH + R_WIDTH
    mod = _ada_mod(c, ada_w, ada_b).reshape(depth, b, N_SUB, 3, 1, d)
    bias = _bias_tiles(t5_bias)
    final_g = final_norm_g.reshape(1, d)
    w_in_z = _layout_w_in(w_in)
    w_out16 = w_out.astype(BF16)
    h = x
    for l in range(depth):
        shift = lambda i: mod[l, :, i, 0]
        scale = lambda i: mod[l, :, i, 1]
        gate = lambda i: mod[l, :, i, 2]
        g = lambda i: norm_g[l, i].reshape(1, d)
        h = _ffn(h, g(0), shift(0), scale(0), gate(0), ffn_w_in, ffn_w_out, l, 0, final_g, final_norm=False)
        z = _proj(h, g(1), shift(1), scale(1), w_in_z, l)
        qt, qit, ckv, ckvt, ki, wit = _dsa_prep(z, ckv_norm_g[l].reshape(1, KV_RANK))
        wuk = jnp.transpose(w_uk[l], (1, 0, 2)).astype(BF16)
        wuvt = jnp.transpose(w_uv[l], (1, 2, 0)).astype(BF16)
        o_a = _dsa(qt, qit, ckv, ckvt, ki, wit, wuk, wuvt, bias)
        o_r = _rwkv(z, _rwkv_params(l, rwkv_mu, rwkv_w0, rwkv_w2, rwkv_a0, rwkv_a2, rwkv_g2, rwkv_k_k,
                                    rwkv_k_a, rwkv_r_k, rwkv_ln_w, rwkv_ln_b))
        h = _outproj(o_a, o_r, h, gate(1), w_out16, l)
        h = _ffn(h, g(2), shift(2), scale(2), gate(2), ffn_w_in, ffn_w_out, l, 1, final_g,
                 final_norm=(l == depth - 1))
    return h
```

```python
import functools
import math

import jax
import jax.numpy as jnp
from jax import lax
from jax.experimental import pallas as pl
from jax.experimental.pallas import tpu as pltpu

F32 = jnp.float32
BF16 = jnp.bfloat16
I32 = jnp.int32
I16 = jnp.int16

A_HEADS = 8
A_HEAD_DIM = 128
A_WIDTH = A_HEADS * A_HEAD_DIM
KV_RANK = 256
IDX_HEADS = 16
IDX_DIM = 64
TOPK_MAX = 256
REL_BUCKETS = 32
REL_MAX_EXACT = REL_BUCKETS // 2
REL_MAX_DIST = 128
R_HEAD_DIM = 64
R_WIDTH = 1024
DECAY_LORA = 96
AAA_LORA = 96
GATE_LORA = 256
GN_EPS = 64e-5
RMS_EPS = 1e-6
N_SUB = 3

LANES = 128
SUBLANES = 8
VMEM_LIMIT_BYTES = 56 * 1024 * 1024

LORA_PAD = 128
Z_Q = 0
Z_QI = Z_Q + A_WIDTH
Z_R = Z_QI + IDX_HEADS * IDX_DIM
Z_K = Z_R + R_WIDTH
Z_V = Z_K + R_WIDTH
Z_GL = Z_V + R_WIDTH
Z_CKV = Z_GL + GATE_LORA
Z_KIWI = Z_CKV + KV_RANK
Z_WL = Z_KIWI + LORA_PAD
Z_AL = Z_WL + LORA_PAD
Z_END = Z_AL + LORA_PAD
Z_WIDTH = 6144

FFN_TOKEN_TILE = 1024
FFN_HIDDEN_TILE = 512
PROLOGUE_ROWS = 256
ADALN_SLAB_ROWS = 16
ADALN_UNROLL = 8
DSA_TILE = 256
DSA_PREP_TILE = 512
RWKV_CHUNK = 64
RWKV_TILE = 256
PAIR = 2 * R_HEAD_DIM

I16_MIN = -(2 ** 15)
PACKED_ROWS = 2 * SUBLANES
SUM_ROWS = PACKED_ROWS
BF16_EXACT_INT = 256
COUNT_CHAINS = 4
KEY_NEG_INF = -2139095041
MASKED_LOGIT = -(2.0 ** 100)
LOG2E = math.log2(math.e)
FAR_DISTANCE = math.ceil(REL_MAX_EXACT * (REL_MAX_DIST / REL_MAX_EXACT)
                         ** ((REL_BUCKETS - 1 - REL_MAX_EXACT) / (REL_BUCKETS - REL_MAX_EXACT)))


def _dot(a, b):
    return jnp.dot(a, b, preferred_element_type=F32)


def _bdot(a, b):
    return lax.dot_general(a, b, (((2,), (1,)), ((0,), (0,))), preferred_element_type=F32)


def _bdot_nt(a, b):
    return lax.dot_general(a, b, (((2,), (2,)), ((0,), (0,))), preferred_element_type=F32)


def _rms(x, g, eps):
    ms = jnp.mean(x * x, axis=-1, keepdims=True)
    return x * lax.rsqrt(ms + eps) * g


def _divisor_tile(n, pref):
    if n <= pref:
        return n
    t = (pref // LANES) * LANES
    while t > LANES and n % t:
        t -= LANES
    assert n % t == 0, (n, pref)
    return t


def _params(*sem):
    return pltpu.CompilerParams(dimension_semantics=sem, vmem_limit_bytes=VMEM_LIMIT_BYTES)


def _ada_kernel(c_ref, w_ref, b_ref, o_ref):
    c = c_ref[...]
    ca = (c * jax.nn.sigmoid(c)).astype(BF16)
    o_ref[0] = _dot(ca, w_ref[0].astype(BF16)) + b_ref[0]


def _ada_mod(c, ada_w, ada_b):
    depth, d, n = ada_w.shape
    b = c.shape[0]
    bp = -(-b // SUBLANES) * SUBLANES
    cp = jnp.pad(c, ((0, bp - b), (0, 0)))
    tn = _divisor_tile(n, 1024)
    out = pl.pallas_call(
        _ada_kernel,
        out_shape=jax.ShapeDtypeStruct((depth, bp, n), F32),
        grid=(depth, n // tn),
        in_specs=[
            pl.BlockSpec((bp, d), lambda l, j: (0, 0)),
            pl.BlockSpec((1, d, tn), lambda l, j: (l, 0, j)),
            pl.BlockSpec((1, 1, tn), lambda l, j: (l, 0, j)),
        ],
        out_specs=pl.BlockSpec((1, bp, tn), lambda l, j: (l, 0, j)),
        compiler_params=_params("arbitrary", "arbitrary"),
        name="ada_mod",
    )(cp, ada_w, ada_b.reshape(depth, 1, n))
    return out[:, :b]


def _adaln_rows(h_ref, g_ref, scale_ref, shift_ref, hn_ref, base, copy_ref=None):
    rows = h_ref.shape[1]
    g = g_ref[...]
    mul = 1.0 + scale_ref[0]
    add = shift_ref[0]

    def slab(i, carry):
        src = pl.ds(pl.multiple_of(i * ADALN_SLAB_ROWS, ADALN_SLAB_ROWS), ADALN_SLAB_ROWS)
        dst = pl.ds(pl.multiple_of(base + i * ADALN_SLAB_ROWS, ADALN_SLAB_ROWS), ADALN_SLAB_ROWS)
        h = h_ref[0, src, :]
        hn_ref[dst, :] = (_rms(h, g, RMS_EPS) * mul + add).astype(BF16)
        if copy_ref is not None:
            copy_ref[0, dst, :] = h
        return carry

    lax.fori_loop(0, rows // ADALN_SLAB_ROWS, slab, 0, unroll=ADALN_UNROLL)


def _prologue_split(tm):
    rows = _divisor_tile(tm, PROLOGUE_ROWS)
    return rows, tm // rows


def _ffn_kernel(h_ref, g_ref, shift_ref, scale_ref, gate_ref, wg_ref, wu_ref, wo_ref, fg_ref,
                o_ref, hn_ref, *, final_norm, pro_steps):
    f = pl.program_id(2)

    @pl.when(f < pro_steps)
    def _():
        _adaln_rows(h_ref, g_ref, scale_ref, shift_ref, hn_ref, f * h_ref.shape[1], o_ref)

    @pl.when(f >= pro_steps)
    def _():
        hn = hn_ref[...]
        g = _dot(hn, wg_ref[...].astype(BF16))
        u = _dot(hn, wu_ref[...].astype(BF16))
        act = (g * jax.nn.sigmoid(g) * u).astype(BF16)
        o_ref[0] += (0.5 * gate_ref[0]) * _dot(act, wo_ref[...].astype(BF16))

    if final_norm:
        @pl.when(f == pl.num_programs(2) - 1)
        def _():
            o_ref[0] = _rms(o_ref[0], fg_ref[...], RMS_EPS)


def _ffn(h, g, shift, scale, gate, w_in, w_out, layer, which, final_g, *, final_norm):
    b, s, d = h.shape
    ff = w_out.shape[2]
    tm = _divisor_tile(s, FFN_TOKEN_TILE)
    tf = _divisor_tile(ff, FFN_HIDDEN_TILE)
    nf = ff // tf
    hr, pro = _prologue_split(tm)
    vec = pl.BlockSpec((1, 1, d), lambda bi, i, f: (bi, 0, 0))
    row = pl.BlockSpec((1, d), lambda bi, i, f: (0, 0))

    def wf(f):
        return jnp.maximum(f - pro, 0)

    return pl.pallas_call(
        functools.partial(_ffn_kernel, final_norm=final_norm, pro_steps=pro),
        out_shape=jax.ShapeDtypeStruct((b, s, d), F32),
        grid=(b, s // tm, pro + nf),
        in_specs=[
            pl.BlockSpec((1, hr, d), lambda bi, i, f: (bi, i * pro + jnp.minimum(f, pro - 1), 0)),
            row, vec, vec, vec,
            pl.BlockSpec((None, None, d, tf), lambda bi, i, f: (layer, which, 0, wf(f))),
            pl.BlockSpec((None, None, d, tf), lambda bi, i, f: (layer, which, 0, nf + wf(f))),
            pl.BlockSpec((None, None, tf, d), lambda bi, i, f: (layer, which, wf(f), 0)),
            row,
        ],
        out_specs=pl.BlockSpec((1, tm, d), lambda bi, i, f: (bi, i, 0)),
        scratch_shapes=[pltpu.VMEM((tm, d), BF16)],
        compiler_params=_params("parallel", "parallel", "arbitrary"),
        name="ffn",
    )(h, g, shift, scale, gate, w_in, w_in, w_out, final_g)


def _proj_kernel(h_ref, g_ref, shift_ref, scale_ref, w_ref, o_ref, hn_ref):
    @pl.when(pl.program_id(2) == 0)
    def _():
        _adaln_rows(h_ref, g_ref, scale_ref, shift_ref, hn_ref, 0)

    o_ref[0] = _dot(hn_ref[...], w_ref[...])


def _proj(h, g, shift, scale, w, layer):
    b, s, d = h.shape
    p = w.shape[2]
    tm = _divisor_tile(s, 1024)
    tn = _divisor_tile(p, 1024)
    vec = pl.BlockSpec((1, 1, d), lambda bi, i, n: (bi, 0, 0))
    return pl.pallas_call(
        _proj_kernel,
        out_shape=jax.ShapeDtypeStruct((b, s, p), F32),
        grid=(b, s // tm, p // tn),
        in_specs=[
            pl.BlockSpec((1, tm, d), lambda bi, i, n: (bi, i, 0)),
            pl.BlockSpec((1, d), lambda bi, i, n: (0, 0)),
            vec, vec,
            pl.BlockSpec((None, d, tn), lambda bi, i, n: (layer, 0, n)),
        ],
        out_specs=pl.BlockSpec((1, tm, tn), lambda bi, i, n: (bi, i, n)),
        scratch_shapes=[pltpu.VMEM((tm, d), BF16)],
        compiler_params=_params("parallel", "parallel", "arbitrary"),
        name="proj",
    )(h, g, shift, scale, w)


def _outproj_kernel(oa_ref, or_ref, h_ref, gate_ref, wa_ref, wr_ref, o_ref):
    acc = _dot(oa_ref[0], wa_ref[...]) + _dot(or_ref[0], wr_ref[...])
    o_ref[0] = h_ref[0] + gate_ref[0] * acc


def _outproj(o_a, o_r, h, gate, w_out, layer):
    b, s, d = h.shape
    tm = _divisor_tile(s, 512)
    tn = d
    wa = o_a.shape[-1]
    wr = o_r.shape[-1]
    return pl.pallas_call(
        _outproj_kernel,
        out_shape=jax.ShapeDtypeStruct((b, s, d), F32),
        grid=(b, s // tm, d // tn),
        in_specs=[
            pl.BlockSpec((1, tm, wa), lambda bi, i, n: (bi, i, 0)),
            pl.BlockSpec((1, tm, wr), lambda bi, i, n: (bi, i, 0)),
            pl.BlockSpec((1, tm, tn), lambda bi, i, n: (bi, i, n)),
            pl.BlockSpec((1, 1, tn), lambda bi, i, n: (bi, 0, n)),
            pl.BlockSpec((None, wa, tn), lambda bi, i, n: (layer, 0, n)),
            pl.BlockSpec((None, wr, tn), lambda bi, i, n: (layer, wa // wr, n)),
        ],
        out_specs=pl.BlockSpec((1, tm, tn), lambda bi, i, n: (bi, i, n)),
        compiler_params=_params("parallel", "parallel", "arbitrary"),
        name="outproj",
    )(o_a, o_r, h, gate, w_out, w_out)


def _dsa_prep_kernel(q_ref, qi_ref, ckv_ref, kiwi_ref, g_ref, qt_ref, qit_ref, ckv_o, ckvt_o, ki_o, wit_o):
    qt_ref[0] = q_ref[0].T.astype(BF16)
    qit_ref[0] = qi_ref[0].T.astype(BF16)
    cn = _rms(ckv_ref[0], g_ref[...], RMS_EPS)
    ckv_o[0] = cn.astype(BF16)
    t = DSA_TILE
    ones = jnp.ones((SUM_ROWS, t), BF16)
    for j in range(cn.shape[0] // t):
        ckvt_o[0, j] = jnp.concatenate([cn[j * t:(j + 1) * t].T.astype(BF16), ones], axis=0)
    kw = kiwi_ref[0]
    ki_o[0] = kw[:, :IDX_DIM].astype(BF16)
    wit_o[0] = kw.T[IDX_DIM:IDX_DIM + IDX_HEADS, :] * (IDX_HEADS * IDX_DIM) ** -0.5


def _dsa_prep(z, ckv_g):
    b, s, _ = z.shape
    t = DSA_TILE
    nt = s // t
    tp = _divisor_tile(s, DSA_PREP_TILE)
    assert tp % t == 0

    def seg(width, off):
        return pl.BlockSpec((1, tp, width), lambda bi, i: (bi, i, off // width))

    return pl.pallas_call(
        _dsa_prep_kernel,
        out_shape=(
            jax.ShapeDtypeStruct((b, A_WIDTH, s), BF16),
            jax.ShapeDtypeStruct((b, IDX_HEADS * IDX_DIM, s), BF16),
            jax.ShapeDtypeStruct((b, s, KV_RANK), BF16),
            jax.ShapeDtypeStruct((b, nt, KV_RANK + SUM_ROWS, t), BF16),
            jax.ShapeDtypeStruct((b, s, IDX_DIM), BF16),
            jax.ShapeDtypeStruct((b, IDX_HEADS, s), F32),
        ),
        grid=(b, s // tp),
        in_specs=[
            seg(A_WIDTH, Z_Q), seg(IDX_HEADS * IDX_DIM, Z_QI), seg(KV_RANK, Z_CKV), seg(LORA_PAD, Z_KIWI),
            pl.BlockSpec((1, KV_RANK), lambda bi, i: (0, 0)),
        ],
        out_specs=(
            pl.BlockSpec((1, A_WIDTH, tp), lambda bi, i: (bi, 0, i)),
            pl.BlockSpec((1, IDX_HEADS * IDX_DIM, tp), lambda bi, i: (bi, 0, i)),
            pl.BlockSpec((1, tp, KV_RANK), lambda bi, i: (bi, i, 0)),
            pl.BlockSpec((1, tp // t, KV_RANK + SUM_ROWS, t), lambda bi, i: (bi, i, 0, 0)),
            pl.BlockSpec((1, tp, IDX_DIM), lambda bi, i: (bi, i, 0)),
            pl.BlockSpec((1, IDX_HEADS, tp), lambda bi, i: (bi, 0, i)),
        ),
        compiler_params=_params("parallel", "parallel"),
        name="dsa_prep",
    )(z, z, z, z, ckv_g)


def _bias_kernel(t5_ref, o_ref):
    t = DSA_TILE
    j = lax.broadcasted_iota(I32, (t, t), 0)
    i = lax.broadcasted_iota(I32, (t, t), 1)
    for didx in range(2):
        n = jnp.maximum(didx * t + i - j, 0)
        nf = jnp.maximum(n, 1).astype(F32)
        large = REL_MAX_EXACT + (jnp.log(nf / REL_MAX_EXACT) / math.log(REL_MAX_DIST / REL_MAX_EXACT)
                                 * (REL_BUCKETS - REL_MAX_EXACT)).astype(I32)
        large = jnp.minimum(large, REL_BUCKETS - 1)
        bucket = jnp.where(n < REL_MAX_EXACT, n, large)
        for h in range(A_HEADS):
            val = jnp.zeros((t, t), F32)
            for k in range(REL_BUCKETS):
                val = jnp.where(bucket == k, t5_ref[k, h], val)
            o_ref[(1 - didx) * t:(2 - didx) * t, h * t:(h + 1) * t] = (val - t5_ref[REL_BUCKETS - 1, h]) * LOG2E


def _bias_tiles(t5_bias):
    t = DSA_TILE
    assert t + 1 >= FAR_DISTANCE
    return pl.pallas_call(
        _bias_kernel,
        out_shape=jax.ShapeDtypeStruct((2 * t, A_HEADS * t), F32),
        in_specs=[pl.BlockSpec(memory_space=pltpu.SMEM)],
        out_specs=pl.BlockSpec(memory_space=pltpu.VMEM),
        compiler_params=pltpu.CompilerParams(vmem_limit_bytes=VMEM_LIMIT_BYTES),
        name="t5_bias_tiles",
    )(t5_bias)


def _dsa_kernel(qt_ref, qit_ref, ckv_ref, ckvt_ref, ki_ref, wit_ref, wuk_ref, wuvt_ref, bias_ref,
                o_ref, keys_ref, khi_ref, klo_ref, acc_ref, ot_ref, *, topk, seq_len):
    t = DSA_TILE
    qb = pl.program_id(1)
    nk = qb + 1
    row = lax.broadcasted_iota(I32, (t, t), 0)
    col = lax.broadcasted_iota(I32, (t, t), 1)

    def idx_rows(start, rows, diagonal):
        sl = pl.ds(pl.multiple_of(start, rows), rows)
        kic = ki_ref[0, sl, :]
        acc = jnp.zeros((rows, t), F32)
        for h in range(IDX_HEADS):
            rel = _dot(kic, qit_ref[0, h * IDX_DIM:(h + 1) * IDX_DIM, :])
            acc = acc + jnp.maximum(rel, 0.0) * wit_ref[0, h:h + 1, :]
        acc = acc + 0.0
        bits = pltpu.bitcast(acc, I32)
        key = bits ^ ((bits >> 31) & 0x7FFFFFFF)
        if diagonal:
            key = jnp.where(row <= col, key, KEY_NEG_INF)
        keys_ref[sl, :] = key
        khi_ref[sl, :] = (key >> 16).astype(I16)
        klo_ref[sl, :] = ((key & 0xFFFF) + I16_MIN).astype(I16)

    def idx_group(chunks):
        def body(i, first_chunk):
            idx_rows(first_chunk * t, chunks * t, False)
            return first_chunk + chunks
        return body

    done = lax.fori_loop(0, qb // 4, idx_group(4), 0)
    done = lax.fori_loop(0, (qb // 2) % 2, idx_group(2), done)
    lax.fori_loop(0, qb % 2, idx_group(1), done)
    idx_rows(qb * t, t, True)

    def count16(ref, cand):
        cand16 = cand.astype(I16)

        def body(kc, accs):
            x = ref[pl.ds(pl.multiple_of(kc * t, t), t), :]
            ones = jnp.where(x >= cand16, jnp.asarray(1, BF16), jnp.asarray(0, BF16))
            ones = ones.reshape(t // PACKED_ROWS, PACKED_ROWS, t)
            accs = list(accs)
            for i in range(t // PACKED_ROWS):
                accs[i % COUNT_CHAINS] = accs[i % COUNT_CHAINS] + ones[i]
            return tuple(accs)

        accs = (jnp.zeros((PACKED_ROWS, t), BF16),) * COUNT_CHAINS
        accs = lax.fori_loop(0, nk // 2, lambda i, a: body(2 * i + 1, body(2 * i, a)), accs)
        accs = lax.fori_loop(2 * (nk // 2), nk, body, accs)
        total = sum(a.astype(F32) for a in accs)
        return jnp.sum(total, axis=0, keepdims=True).astype(I32)

    def search16(ref, base):
        def accept(cand, cur):
            return jnp.where(base + count16(ref, cand) >= topk, cand, cur)

        v = accept(jnp.zeros((1, t), I32), jnp.full((1, t), I16_MIN, I32))
        return lax.fori_loop(0, 15, lambda i, v: accept(v | (jnp.int32(1) << (14 - i)), v), v)

    hi = search16(khi_ref, 0)
    above = count16(khi_ref, jnp.minimum(hi + 1, -I16_MIN - 1))
    hi16 = hi.astype(I16)

    def keep_low(kc, carry):
        sl = pl.ds(pl.multiple_of(kc * t, t), t)
        klo_ref[sl, :] = jnp.where(khi_ref[sl, :] == hi16, klo_ref[sl, :], jnp.asarray(I16_MIN, I16))
        return carry

    lax.fori_loop(0, nk, keep_low, 0)
    lo = search16(klo_ref, above)
    thr = jnp.maximum((hi << 16) | (lo - I16_MIN), KEY_NEG_INF + 1)

    def count(pred):
        def body(kc, acc):
            k = keys_ref[pl.ds(pl.multiple_of(kc * t, t), t), :]
            m = jnp.where(pred(k, kc), 1, 0).astype(I32)
            return acc + jnp.sum(m.reshape(t // SUBLANES, SUBLANES, t), axis=0)

        acc = lax.fori_loop(0, nk, body, jnp.zeros((SUBLANES, t), I32))
        return jnp.sum(acc, axis=0, keepdims=True)

    @pl.when(jnp.max(count(lambda k, kc: k >= thr)) > topk)
    def _():
        need = topk - count(lambda k, kc: k > thr)
        nbits = (seq_len - 1).bit_length()

        def pos_bit(i, y):
            cand = y | (jnp.int32(1) << (nbits - 1 - i))
            before = count(lambda k, kc: (k == thr) & ((kc * t + row) < cand))
            return jnp.where(before < need, cand, y)

        last_kept = lax.fori_loop(0, nbits, pos_bit, jnp.zeros((1, t), I32))

        def demote(kc, carry):
            sl = pl.ds(pl.multiple_of(kc * t, t), t)
            k = keys_ref[sl, :]
            keys_ref[sl, :] = jnp.where((k == thr) & ((kc * t + row) > last_kept), k - 1, k)
            return carry

        lax.fori_loop(0, nk, demote, 0)

    scale = A_HEAD_DIM ** -0.5 * LOG2E

    hw = A_HEADS * t
    qlat = jnp.concatenate(
        [(_dot(wuk_ref[h], qt_ref[0, h * A_HEAD_DIM:(h + 1) * A_HEAD_DIM, :]) * scale).astype(BF16)
         for h in range(A_HEADS)], axis=1)
    acc_ref[...] = jnp.zeros_like(acc_ref)

    def att_chunks(kc, m, *, chunks, bias_row=None):
        rows = chunks * t
        sl = pl.ds(pl.multiple_of(kc * t, t), rows)
        raw = _dot(ckv_ref[0, sl, :], qlat)
        sel = keys_ref[sl, :] >= thr
        parts = []
        for h in range(A_HEADS):
            lg_h = raw[:, h * t:(h + 1) * t]
            if bias_row is not None:
                lg_h = lg_h + bias_ref[bias_row:bias_row + rows, h * t:(h + 1) * t]
            parts.append(jnp.where(sel, lg_h, MASKED_LOGIT).astype(BF16))
        lg = jnp.concatenate(parts, axis=1)
        m_new = jnp.maximum(m, jnp.max(lg, axis=0, keepdims=True).astype(F32))
        p = jnp.exp2(lg - m_new.astype(BF16))
        acc = acc_ref[...] * jnp.exp2(m - m_new)
        for j in range(chunks):
            acc = acc + _dot(ckvt_ref[0, kc + j], p[j * t:(j + 1) * t])
        acc_ref[...] = acc
        return m_new

    n_far = jnp.maximum(qb - 1, 0)
    has_prev = jnp.minimum(qb, 1)
    m = jnp.full((1, hw), MASKED_LOGIT, F32)
    def far_group(chunks):
        def body(i, carry):
            first_chunk, m = carry
            return first_chunk + chunks, att_chunks(first_chunk, m, chunks=chunks)
        return body

    carry = lax.fori_loop(0, n_far // 4, far_group(4), (0, m))
    carry = lax.fori_loop(0, (n_far // 2) % 2, far_group(2), carry)
    _, m = lax.fori_loop(0, n_far % 2, far_group(1), carry)
    m = lax.fori_loop(0, has_prev, lambda i, m: att_chunks(qb - 1, m, chunks=2, bias_row=0), m)
    lax.fori_loop(0, 1 - has_prev, lambda i, m: att_chunks(qb, m, chunks=1, bias_row=t), m)
    olat = (acc_ref[:KV_RANK, :] / acc_ref[KV_RANK:KV_RANK + 1, :]).astype(BF16)
    for h in range(A_HEADS):
        ot_ref[h * A_HEAD_DIM:(h + 1) * A_HEAD_DIM, :] = _dot(wuvt_ref[h], olat[:, h * t:(h + 1) * t])
    o_ref[0] = ot_ref[...].T.astype(BF16)


def _dsa(qt, qit, ckv, ckvt, ki, wit, wuk, wuvt, bias):
    b, _, s = qt.shape
    t = DSA_TILE
    nt = s // t
    topk = min(TOPK_MAX, s // 4)
    assert s // PACKED_ROWS <= BF16_EXACT_INT
    return pl.pallas_call(
        functools.partial(_dsa_kernel, topk=topk, seq_len=s),
        out_shape=jax.ShapeDtypeStruct((b, s, A_WIDTH), BF16),
        grid=(b, nt),
        in_specs=[
            pl.BlockSpec((1, A_WIDTH, t), lambda bi, i: (bi, 0, i)),
            pl.BlockSpec((1, IDX_HEADS * IDX_DIM, t), lambda bi, i: (bi, 0, i)),
            pl.BlockSpec((1, s, KV_RANK), lambda bi, i: (bi, 0, 0)),
            pl.BlockSpec((1, nt, KV_RANK + SUM_ROWS, t), lambda bi, i: (bi, 0, 0, 0)),
            pl.BlockSpec((1, s, IDX_DIM), lambda bi, i: (bi, 0, 0)),
            pl.BlockSpec((1, IDX_HEADS, t), lambda bi, i: (bi, 0, i)),
            pl.BlockSpec((A_HEADS, KV_RANK, A_HEAD_DIM), lambda bi, i: (0, 0, 0)),
            pl.BlockSpec((A_HEADS, A_HEAD_DIM, KV_RANK), lambda bi, i: (0, 0, 0)),
            pl.BlockSpec((2 * t, A_HEADS * t), lambda bi, i: (0, 0)),
        ],
        out_specs=pl.BlockSpec((1, t, A_WIDTH), lambda bi, i: (bi, i, 0)),
        scratch_shapes=[pltpu.VMEM((s, t), I32), pltpu.VMEM((s, t), I16), pltpu.VMEM((s, t), I16),
                        pltpu.VMEM((KV_RANK + SUM_ROWS, A_HEADS * t), F32), pltpu.VMEM((A_WIDTH, t), F32)],
        compiler_params=_params("parallel", "arbitrary"),
        name="dsa",
    )(qt, qit, ckv, ckvt, ki, wit, wuk, wuvt, bias)


def _split3(x):
    hi = x.astype(BF16)
    r1 = x - hi.astype(F32)
    mid = r1.astype(BF16)
    lo = (r1 - mid.astype(F32)).astype(BF16)
    return hi, mid, lo


def _rwkv_kernel(r_ref, k_ref, v_ref, gl_ref, wl_ref, al_ref,
                 mur_ref, muk_ref, muv_ref, mug_ref, muw_ref, mua_ref,
                 w0_ref, a0_ref, kk_ref, ka_ref, rk_ref, lnw_ref, lnb_ref,
                 w2_ref, a2_ref, g2_ref,
                 o_ref,
                 pr_ref, pk_ref, pv_ref, pg_ref, pw_ref, pa_ref, state_ref):
    tl = RWKV_TILE
    c = RWKV_CHUNK
    nc = tl // c
    npair = R_WIDTH // PAIR

    @pl.when(pl.program_id(1) == 0)
    def _():
        state_ref[...] = jnp.zeros_like(state_ref)
        for ref in (pr_ref, pk_ref, pv_ref, pg_ref, pw_ref, pa_ref):
            ref[...] = jnp.zeros_like(ref)

    def shift_mix(x_ref, prev_ref, mu_ref):
        x = x_ref[0]
        first = lax.broadcasted_iota(I32, x.shape, 0) == 0
        xprev = jnp.where(first, prev_ref[...], pltpu.roll(x, 1, axis=0))
        prev_ref[...] = x[tl - 1:tl, :]
        return x + (xprev - x) * mu_ref[...]

    r = shift_mix(r_ref, pr_ref, mur_ref)
    k = shift_mix(k_ref, pk_ref, muk_ref)
    v = shift_mix(v_ref, pv_ref, muv_ref)
    gl = shift_mix(gl_ref, pg_ref, mug_ref)
    wl = shift_mix(wl_ref, pw_ref, muw_ref)
    al = shift_mix(al_ref, pa_ref, mua_ref)

    w_lin = w0_ref[...] + _dot(jnp.tanh(wl).astype(BF16), w2_ref[...])
    nx = -w_lin
    softplus = jnp.maximum(nx, 0.0) + jnp.log(1.0 + jnp.exp(-jnp.abs(nx)))
    ld = -jnp.exp(-softplus - 0.5)
    a = jax.nn.sigmoid(a0_ref[...] + _dot(al.astype(BF16), a2_ref[...]))
    gate = _dot(jax.nn.sigmoid(gl).astype(BF16), g2_ref[...])

    li = lax.broadcasted_iota(I32, (LANES, LANES), 0) // R_HEAD_DIM
    lj = lax.broadcasted_iota(I32, (LANES, LANES), 1) // R_HEAD_DIM
    head_ones = jnp.where(li == lj, 1.0, 0.0).astype(BF16)

    def head_sum(x):
        parts = []
        for j in range(R_WIDTH // LANES):
            hi, mid, lo = _split3(x[:, j * LANES:(j + 1) * LANES])
            parts.append(_dot(hi, head_ones) + _dot(mid, head_ones) + _dot(lo, head_ones))
        return jnp.concatenate(parts, axis=-1)

    kk = k * kk_ref[...]
    kk = kk / jnp.maximum(jnp.sqrt(head_sum(kk * kk)), 1e-12)
    k2 = k * (1.0 + (a - 1.0) * ka_ref[...])
    bb = kk * a

    ti = lax.broadcasted_iota(I32, (tl, tl), 0)
    tj = lax.broadcasted_iota(I32, (tl, tl), 1)
    tri = jnp.where((ti // c == tj // c) & (tj <= ti), 1.0, 0.0).astype(BF16)
    hi, mid, lo = _split3(ld)
    cum = _dot(tri, hi) + _dot(tri, mid) + _dot(tri, lo)
    cum_end = jnp.broadcast_to(cum.reshape(nc, c, R_WIDTH)[:, c - 1:c, :], (nc, c, R_WIDTH)).reshape(tl, R_WIDTH)
    e_in = jnp.exp(cum)
    e_neg = jnp.exp(-cum)
    e_out = jnp.exp(cum_end - cum)

    nb = nc * npair

    def to_pairs(x):
        x3 = x.reshape(nc, c, R_WIDTH)
        xs = jnp.stack([x3[:, :, p * PAIR:(p + 1) * PAIR] for p in range(npair)], axis=1)
        return xs.reshape(nb, c, PAIR)

    lane_head = lax.broadcasted_iota(I32, (2 * c, PAIR), 1) // R_HEAD_DIM
    row_head = lax.broadcasted_iota(I32, (2 * c, PAIR), 0) // c
    same = lane_head == row_head
    rt = lax.broadcasted_iota(I32, (2 * c, PAIR), 0) % c
    ct = lax.broadcasted_iota(I32, (2 * c, PAIR), 1) % c
    strict = same & (ct < rt)
    incl = same & (ct <= rt)
    eye = jnp.where(same & (ct == rt), 1.0, 0.0).astype(F32)

    def bd(x):
        return jnp.where(same, jnp.concatenate([x, x], axis=1), 0.0)

    lane0 = lax.broadcasted_iota(I32, (c, PAIR), 1) < R_HEAD_DIM

    abar = bd(to_pairs(-kk * jnp.exp(cum - ld))).astype(BF16)
    rbar = bd(to_pairs(r * e_in)).astype(BF16)
    bt = bd(to_pairs(bb * e_neg)).astype(BF16)
    kt = bd(to_pairs(k2 * e_neg)).astype(BF16)
    bk = jnp.concatenate([bd(to_pairs(bb * e_out)), bd(to_pairs(k2 * e_out))], axis=1).astype(BF16)
    v_pl = to_pairs(v)
    vbd = bd(v_pl)
    decay_end = to_pairs(jnp.exp(cum_end))[:, 0:1, :]
    aa = _bdot_nt(jnp.concatenate([abar, rbar], axis=1), jnp.concatenate([bt, kt], axis=1))
    a_ab = jnp.where(strict, aa[:, :2 * c, :PAIR], 0.0)
    a_ak = jnp.where(strict, aa[:, :2 * c, PAIR:], 0.0).astype(BF16)
    a_rb = jnp.where(incl, aa[:, 2 * c:, :PAIR], 0.0).astype(BF16)
    a_rk = jnp.where(incl, aa[:, 2 * c:, PAIR:], 0.0).astype(BF16)
    pw = a_ab.astype(BF16)
    tinv = eye + a_ab
    for _ in range(int(math.log2(c)) - 1):
        pw = _bdot(pw, pw).astype(BF16)
        tinv = tinv + _bdot(tinv.astype(BF16), pw)
    akv = _bdot(a_ak, vbd.astype(BF16))
    wu = _bdot(tinv.astype(BF16), jnp.concatenate([abar, akv.astype(BF16)], axis=2))
    w16 = wu[:, :, :PAIR].astype(BF16)
    u_bd = wu[:, :, PAIR:]

    s_bd = state_ref[...]
    e_parts, rs_parts = [], []
    for ci in range(nc):
        sl = slice(ci * npair, (ci + 1) * npair)
        wr = _bdot_nt(jnp.concatenate([w16[sl], rbar[sl]], axis=1), s_bd.astype(BF16))
        e_bd = wr[:, :2 * c] + u_bd[sl]
        e_parts.append(e_bd)
        rs_parts.append(wr[:, 2 * c:])
        ev = jnp.concatenate([e_bd, vbd[sl]], axis=1)
        ds = _bdot(jnp.swapaxes(ev, 1, 2).astype(BF16), bk[sl])
        s_bd = s_bd * decay_end[sl] + jnp.where(same, ds, 0.0)
    state_ref[...] = s_bd

    ev_all = jnp.concatenate([jnp.concatenate(e_parts, axis=0), vbd], axis=1).astype(BF16)
    y_bd = jnp.concatenate(rs_parts, axis=0) + _bdot(jnp.concatenate([a_rb, a_rk], axis=2), ev_all)
    mean = jnp.sum(y_bd, axis=-1, keepdims=True) * (1.0 / R_HEAD_DIM)
    dev = jnp.where(same, y_bd - mean, 0.0)
    var = jnp.sum(dev * dev, axis=-1, keepdims=True) * (1.0 / R_HEAD_DIM)
    yn = dev * lax.rsqrt(var + GN_EPS)
    yn = yn[:, :c] + yn[:, c:]
    rkv = to_pairs(r * k2 * rk_ref[...])
    s0 = jnp.sum(jnp.where(lane0, rkv, 0.0), axis=-1, keepdims=True)
    s1 = jnp.sum(jnp.where(lane0, 0.0, rkv), axis=-1, keepdims=True)
    bonus = jnp.where(lane0, s0, s1) * v_pl
    def per_pair(row_ref):
        rows = [row_ref[:, p * PAIR:(p + 1) * PAIR] for p in range(npair)]
        return jnp.stack(rows * nc, axis=0)

    out = ((yn * per_pair(lnw_ref) + per_pair(lnb_ref) + bonus) * to_pairs(gate)).astype(BF16)
    for ci in range(nc):
        for p in range(npair):
            o_ref[0, ci * c:(ci + 1) * c, p * PAIR:(p + 1) * PAIR] = out[ci * npair + p]


def _rwkv(z, prm):
    b, s, _ = z.shape
    tl = RWKV_TILE

    def seg(width, off):
        return pl.BlockSpec((1, tl, width), lambda bi, i: (bi, i, off // width))

    def full(arr):
        return pl.BlockSpec(arr.shape, lambda bi, i: (0,) * arr.ndim)

    names = ("mu_r", "mu_k", "mu_v", "mu_g", "mu_w", "mu_a", "w0", "a0", "k_k", "k_a", "r_k", "ln_w", "ln_b",
             "w2", "a2", "g2")
    consts = [prm[n] for n in names]
    return pl.pallas_call(
        _rwkv_kernel,
        out_shape=jax.ShapeDtypeStruct((b, s, R_WIDTH), BF16),
        grid=(b, s // tl),
        in_specs=[seg(R_WIDTH, Z_R), seg(R_WIDTH, Z_K), seg(R_WIDTH, Z_V), seg(GATE_LORA, Z_GL),
                  seg(LORA_PAD, Z_WL), seg(LORA_PAD, Z_AL)] + [full(x) for x in consts],
        out_specs=pl.BlockSpec((1, tl, R_WIDTH), lambda bi, i: (bi, i, 0)),
        scratch_shapes=[
            pltpu.VMEM((1, R_WIDTH), F32), pltpu.VMEM((1, R_WIDTH), F32), pltpu.VMEM((1, R_WIDTH), F32),
            pltpu.VMEM((1, GATE_LORA), F32), pltpu.VMEM((1, LORA_PAD), F32), pltpu.VMEM((1, LORA_PAD), F32),
            pltpu.VMEM((R_WIDTH // PAIR, PAIR, PAIR), F32),
        ],
        compiler_params=_params("parallel", "arbitrary"),
        name="rwkv7",
    )(z, z, z, z, z, z, *consts)


def _pad_cols(w, width):
    return jnp.pad(w, ((0, 0), (0, width - w.shape[1])))


def _w_in_layout_kernel(wt_ref, o_ref):
    wt = wt_ref[0]
    cols = wt.shape[1]
    offs = [0]
    for n in (A_WIDTH, KV_RANK, IDX_HEADS * IDX_DIM, IDX_DIM + IDX_HEADS,
              3 * R_WIDTH, DECAY_LORA, AAA_LORA, GATE_LORA):
        offs.append(offs[-1] + n)
    q, ckv, qi, kiwi, rkv, wl, al, gl = (wt[offs[i]:offs[i + 1], :] for i in range(8))

    def padded(x, height):
        return jnp.concatenate([x, jnp.zeros((height - x.shape[0], cols), x.dtype)], axis=0)

    rows = [q, qi, rkv, gl, ckv, padded(kiwi, LORA_PAD), padded(wl, LORA_PAD), padded(al, LORA_PAD),
            jnp.zeros((Z_WIDTH - Z_END, cols), wt.dtype)]
    o_ref[0] = jnp.concatenate(rows, axis=0).T.astype(BF16)


def _layout_w_in(w_in):
    depth, d, p = w_in.shape
    tc = _divisor_tile(d, 256)
    return pl.pallas_call(
        _w_in_layout_kernel,
        out_shape=jax.ShapeDtypeStruct((depth, d, Z_WIDTH), BF16),
        grid=(depth, d // tc),
        in_specs=[pl.BlockSpec((1, p, tc), lambda l, i: (l, 0, i))],
        out_specs=pl.BlockSpec((1, tc, Z_WIDTH), lambda l, i: (l, i, 0)),
        compiler_params=_params("parallel", "parallel"),
        name="w_in_layout",
    )(jnp.swapaxes(w_in, 1, 2))


def _pad_rows(w, rows):
    return jnp.pad(w, ((0, rows - w.shape[0]), (0, 0)))


def _rwkv_params(l, mu, w0, w2, a0, a2, g2, k_k, k_a, r_k, ln_w, ln_b):
    m = mu[l]
    o = [0, R_WIDTH, 2 * R_WIDTH, 3 * R_WIDTH, 3 * R_WIDTH + DECAY_LORA, 3 * R_WIDTH + DECAY_LORA + AAA_LORA]
    row = lambda x: x.reshape(1, -1)
    return {
        "mu_r": row(m[o[0]:o[1]]), "mu_k": row(m[o[1]:o[2]]), "mu_v": row(m[o[2]:o[3]]),
        "mu_w": _pad_cols(row(m[o[3]:o[4]]), LORA_PAD), "mu_a": _pad_cols(row(m[o[4]:o[5]]), LORA_PAD),
        "mu_g": row(m[o[5]:]),
        "w0": row(w0[l]), "a0": row(a0[l]), "k_k": row(k_k[l]), "k_a": row(k_a[l]), "r_k": row(r_k[l]),
        "ln_w": row(ln_w[l]), "ln_b": row(ln_b[l]),
        "w2": _pad_rows(w2[l], LORA_PAD).astype(BF16), "a2": _pad_rows(a2[l], LORA_PAD).astype(BF16),
        "g2": g2[l].astype(BF16),
    }


def kernel(x, c, t5_bias, ada_w, ada_b, norm_g, ffn_w_in, ffn_w_out, w_in, ckv_norm_g, w_uk, w_uv, rwkv_mu, rwkv_w0, rwkv_w2, rwkv_a0, rwkv_a2, rwkv_g2, rwkv_k_k, rwkv_k_a, rwkv_r_k, rwkv_ln_w, rwkv_ln_b, w_out, final_norm_g):
    b, s, d = x.shape
    depth = ada_w.shape[0]
    assert s % DSA_TILE == 0 and s % RWKV_TILE == 0 and d == A_WIDTH + R_WIDTH
    mod = _ada_mod(c, ada_w, ada_b).reshape(depth, b, N_SUB, 3, 1, d)
    bias = _bias_tiles(t5_bias)
    final_g = final_norm_g.reshape(1, d)
    w_in_z = _layout_w_in(w_in)
    w_out16 = w_out.astype(BF16)
    h = x
    for l in range(depth):
        shift = lambda i: mod[l, :, i, 0]
        scale = lambda i: mod[l, :, i, 1]
        gate = lambda i: mod[l, :, i, 2]
        g = lambda i: norm_g[l, i].reshape(1, d)
        h = _ffn(h, g(0), shift(0), scale(0), gate(0), ffn_w_in, ffn_w_out, l, 0, final_g, final_norm=False)
        z = _proj(h, g(1), shift(1), scale(1), w_in_z, l)
        qt, qit, ckv, ckvt, ki, wit = _dsa_prep(z, ckv_norm_g[l].reshape(1, KV_RANK))
        wuk = jnp.transpose(w_uk[l], (1, 0, 2)).astype(BF16)
        wuvt = jnp.transpose(w_uv[l], (1, 2, 0)).astype(BF16)
        o_a = _dsa(qt, qit, ckv, ckvt, ki, wit, wuk, wuvt, bias)
        o_r = _rwkv(z, _rwkv_params(l, rwkv_mu, rwkv_w0, rwkv_w2, rwkv_a0, rwkv_a2, rwkv_g2, rwkv_k_k,
                                    rwkv_k_a, rwkv_r_k, rwkv_ln_w, rwkv_ln_b))
        h = _outproj(o_a, o_r, h, gate(1), w_out16, l)
        h = _ffn(h, g(2), shift(2), scale(2), gate(2), ffn_w_in, ffn_w_out, l, 1, final_g,
                 final_norm=(l == depth - 1))
    return h
```

```python
import functools
import math

import jax
import jax.numpy as jnp
from jax import lax
from jax.experimental import pallas as pl
from jax.experimental.pallas import tpu as pltpu

F32 = jnp.float32
BF16 = jnp.bfloat16
I32 = jnp.int32
I16 = jnp.int16

A_HEADS = 8
A_HEAD_DIM = 128
A_WIDTH = A_HEADS * A_HEAD_DIM
KV_RANK = 256
IDX_HEADS = 16
IDX_DIM = 64
TOPK_MAX = 256
REL_BUCKETS = 32
REL_MAX_EXACT = REL_BUCKETS // 2
REL_MAX_DIST = 128
R_HEAD_DIM = 64
R_WIDTH = 1024
DECAY_LORA = 96
AAA_LORA = 96
GATE_LORA = 256
GN_EPS = 64e-5
RMS_EPS = 1e-6
N_SUB = 3

LANES = 128
SUBLANES = 8
VMEM_LIMIT_BYTES = 56 * 1024 * 1024

LORA_PAD = 128
Z_Q = 0
Z_QI = Z_Q + A_WIDTH
Z_R = Z_QI + IDX_HEADS * IDX_DIM
Z_K = Z_R + R_WIDTH
Z_V = Z_K + R_WIDTH
Z_GL = Z_V + R_WIDTH
Z_CKV = Z_GL + GATE_LORA
Z_KIWI = Z_CKV + KV_RANK
Z_WL = Z_KIWI + LORA_PAD
Z_AL = Z_WL + LORA_PAD
Z_END = Z_AL + LORA_PAD
Z_WIDTH = 6144

FFN_TOKEN_TILE = 1024
FFN_HIDDEN_TILE = 512
PROLOGUE_ROWS = 256
ADALN_SLAB_ROWS = 16
ADALN_UNROLL = 8
DSA_TILE = 256
DSA_PREP_TILE = 512
RWKV_CHUNK = 64
RWKV_TILE = 256
PAIR = 2 * R_HEAD_DIM

I16_MIN = -(2 ** 15)
PACKED_ROWS = 2 * SUBLANES
SUM_ROWS = PACKED_ROWS
BF16_EXACT_INT = 256
COUNT_CHAINS = 4
KEY_NEG_INF = -2139095041
MASKED_LOGIT = -(2.0 ** 100)
LOG2E = math.log2(math.e)
FAR_DISTANCE = math.ceil(REL_MAX_EXACT * (REL_MAX_DIST / REL_MAX_EXACT)
                         ** ((REL_BUCKETS - 1 - REL_MAX_EXACT) / (REL_BUCKETS - REL_MAX_EXACT)))


def _dot(a, b):
    return jnp.dot(a, b, preferred_element_type=F32)


def _bdot(a, b):
    return lax.dot_general(a, b, (((2,), (1,)), ((0,), (0,))), preferred_element_type=F32)


def _bdot_nt(a, b):
    return lax.dot_general(a, b, (((2,), (2,)), ((0,), (0,))), preferred_element_type=F32)


def _rms(x, g, eps):
    ms = jnp.mean(x * x, axis=-1, keepdims=True)
    return x * lax.rsqrt(ms + eps) * g


def _divisor_tile(n, pref):
    if n <= pref:
        return n
    t = (pref // LANES) * LANES
    while t > LANES and n % t:
        t -= LANES
    assert n % t == 0, (n, pref)
    return t


def _params(*sem):
    return pltpu.CompilerParams(dimension_semantics=sem, vmem_limit_bytes=VMEM_LIMIT_BYTES)


def _ada_kernel(c_ref, w_ref, b_ref, o_ref):
    c = c_ref[...]
    ca = (c * jax.nn.sigmoid(c)).astype(BF16)
    o_ref[0] = _dot(ca, w_ref[0].astype(BF16)) + b_ref[0]


def _ada_mod(c, ada_w, ada_b):
    depth, d, n = ada_w.shape
    b = c.shape[0]
    bp = -(-b // SUBLANES) * SUBLANES
    cp = jnp.pad(c, ((0, bp - b), (0, 0)))
    tn = _divisor_tile(n, 1024)
    out = pl.pallas_call(
        _ada_kernel,
        out_shape=jax.ShapeDtypeStruct((depth, bp, n), F32),
        grid=(depth, n // tn),
        in_specs=[
            pl.BlockSpec((bp, d), lambda l, j: (0, 0)),
            pl.BlockSpec((1, d, tn), lambda l, j: (l, 0, j)),
            pl.BlockSpec((1, 1, tn), lambda l, j: (l, 0, j)),
        ],
        out_specs=pl.BlockSpec((1, bp, tn), lambda l, j: (l, 0, j)),
        compiler_params=_params("arbitrary", "arbitrary"),
        name="ada_mod",
    )(cp, ada_w, ada_b.reshape(depth, 1, n))
    return out[:, :b]


def _adaln_rows(h_ref, g_ref, scale_ref, shift_ref, hn_ref, base, copy_ref=None):
    rows = h_ref.shape[1]
    g = g_ref[...]
    mul = 1.0 + scale_ref[0]
    add = shift_ref[0]

    def slab(i, carry):
        src = pl.ds(pl.multiple_of(i * ADALN_SLAB_ROWS, ADALN_SLAB_ROWS), ADALN_SLAB_ROWS)
        dst = pl.ds(pl.multiple_of(base + i * ADALN_SLAB_ROWS, ADALN_SLAB_ROWS), ADALN_SLAB_ROWS)
        h = h_ref[0, src, :]
        hn_ref[dst, :] = (_rms(h, g, RMS_EPS) * mul + add).astype(BF16)
        if copy_ref is not None:
            copy_ref[0, dst, :] = h
        return carry

    lax.fori_loop(0, rows // ADALN_SLAB_ROWS, slab, 0, unroll=ADALN_UNROLL)


def _prologue_split(tm):
    rows = _divisor_tile(tm, PROLOGUE_ROWS)
    return rows, tm // rows


def _ffn_kernel(h_ref, g_ref, shift_ref, scale_ref, gate_ref, wg_ref, wu_ref, wo_ref, fg_ref,
                o_ref, hn_ref, *, final_norm, pro_steps):
    f = pl.program_id(2)

    @pl.when(f < pro_steps)
    def _():
        _adaln_rows(h_ref, g_ref, scale_ref, shift_ref, hn_ref, f * h_ref.shape[1], o_ref)

    @pl.when(f >= pro_steps)
    def _():
        hn = hn_ref[...]
        g = _dot(hn, wg_ref[...].astype(BF16))
        u = _dot(hn, wu_ref[...].astype(BF16))
        act = (g * jax.nn.sigmoid(g) * u).astype(BF16)
        o_ref[0] += (0.5 * gate_ref[0]) * _dot(act, wo_ref[...].astype(BF16))

    if final_norm:
        @pl.when(f == pl.num_programs(2) - 1)
        def _():
            o_ref[0] = _rms(o_ref[0], fg_ref[...], RMS_EPS)


def _ffn(h, g, shift, scale, gate, w_in, w_out, layer, which, final_g, *, final_norm):
    b, s, d = h.shape
    ff = w_out.shape[2]
    tm = _divisor_tile(s, FFN_TOKEN_TILE)
    tf = _divisor_tile(ff, FFN_HIDDEN_TILE)
    nf = ff // tf
    hr, pro = _prologue_split(tm)
    vec = pl.BlockSpec((1, 1, d), lambda bi, i, f: (bi, 0, 0))
    row = pl.BlockSpec((1, d), lambda bi, i, f: (0, 0))

    def wf(f):
        return jnp.maximum(f - pro, 0)

    return pl.pallas_call(
        functools.partial(_ffn_kernel, final_norm=final_norm, pro_steps=pro),
        out_shape=jax.ShapeDtypeStruct((b, s, d), F32),
        grid=(b, s // tm, pro + nf),
        in_specs=[
            pl.BlockSpec((1, hr, d), lambda bi, i, f: (bi, i * pro + jnp.minimum(f, pro - 1), 0)),
            row, vec, vec, vec,
            pl.BlockSpec((None, None, d, tf), lambda bi, i, f: (layer, which, 0, wf(f))),
            pl.BlockSpec((None, None, d, tf), lambda bi, i, f: (layer, which, 0, nf + wf(f))),
            pl.BlockSpec((None, None, tf, d), lambda bi, i, f: (layer, which, wf(f), 0)),
            row,
        ],
        out_specs=pl.BlockSpec((1, tm, d), lambda bi, i, f: (bi, i, 0)),
        scratch_shapes=[pltpu.VMEM((tm, d), BF16)],
        compiler_params=_params("parallel", "parallel", "arbitrary"),
        name="ffn",
    )(h, g, shift, scale, gate, w_in, w_in, w_out, final_g)


def _proj_kernel(h_ref, g_ref, shift_ref, scale_ref, w_ref, o_ref, hn_ref):
    @pl.when(pl.program_id(2) == 0)
    def _():
        _adaln_rows(h_ref, g_ref, scale_ref, shift_ref, hn_ref, 0)

    o_ref[0] = _dot(hn_ref[...], w_ref[...])


def _proj(h, g, shift, scale, w, layer):
    b, s, d = h.shape
    p = w.shape[2]
    tm = _divisor_tile(s, 1024)
    tn = _divisor_tile(p, 1024)
    vec = pl.BlockSpec((1, 1, d), lambda bi, i, n: (bi, 0, 0))
    return pl.pallas_call(
        _proj_kernel,
        out_shape=jax.ShapeDtypeStruct((b, s, p), F32),
        grid=(b, s // tm, p // tn),
        in_specs=[
            pl.BlockSpec((1, tm, d), lambda bi, i, n: (bi, i, 0)),
            pl.BlockSpec((1, d), lambda bi, i, n: (0, 0)),
            vec, vec,
            pl.BlockSpec((None, d, tn), lambda bi, i, n: (layer, 0, n)),
        ],
        out_specs=pl.BlockSpec((1, tm, tn), lambda bi, i, n: (bi, i, n)),
        scratch_shapes=[pltpu.VMEM((tm, d), BF16)],
        compiler_params=_params("parallel", "parallel", "arbitrary"),
        name="proj",
    )(h, g, shift, scale, w)


def _outproj_kernel(oa_ref, or_ref, h_ref, gate_ref, wa_ref, wr_ref, o_ref):
    acc = _dot(oa_ref[0], wa_ref[...]) + _dot(or_ref[0], wr_ref[...])
    o_ref[0] = h_ref[0] + gate_ref[0] * acc


def _outproj(o_a, o_r, h, gate, w_out, layer):
    b, s, d = h.shape
    tm = _divisor_tile(s, 512)
    tn = d
    wa = o_a.shape[-1]
    wr = o_r.shape[-1]
    return pl.pallas_call(
        _outproj_kernel,
        out_shape=jax.ShapeDtypeStruct((b, s, d), F32),
        grid=(b, s // tm, d // tn),
        in_specs=[
            pl.BlockSpec((1, tm, wa), lambda bi, i, n: (bi, i, 0)),
            pl.BlockSpec((1, tm, wr), lambda bi, i, n: (bi, i, 0)),
            pl.BlockSpec((1, tm, tn), lambda bi, i, n: (bi, i, n)),
            pl.BlockSpec((1, 1, tn), lambda bi, i, n: (bi, 0, n)),
            pl.BlockSpec((None, wa, tn), lambda bi, i, n: (layer, 0, n)),
            pl.BlockSpec((None, wr, tn), lambda bi, i, n: (layer, wa // wr, n)),
        ],
        out_specs=pl.BlockSpec((1, tm, tn), lambda bi, i, n: (bi, i, n)),
        compiler_params=_params("parallel", "parallel", "arbitrary"),
        name="outproj",
    )(o_a, o_r, h, gate, w_out, w_out)


def _dsa_prep_kernel(q_ref, qi_ref, ckv_ref, kiwi_ref, g_ref, qt_ref, qit_ref, ckv_o, ckvt_o, ki_o, wit_o):
    qt_ref[0] = q_ref[0].T.astype(BF16)
    qit_ref[0] = qi_ref[0].T.astype(BF16)
    cn = _rms(ckv_ref[0], g_ref[...], RMS_EPS)
    ckv_o[0] = cn.astype(BF16)
    t = DSA_TILE
    ones = jnp.ones((SUM_ROWS, t), BF16)
    for j in range(cn.shape[0] // t):
        ckvt_o[0, j] = jnp.concatenate([cn[j * t:(j + 1) * t].T.astype(BF16), ones], axis=0)
    kw = kiwi_ref[0]
    ki_o[0] = kw[:, :IDX_DIM].astype(BF16)
    wit_o[0] = kw.T[IDX_DIM:IDX_DIM + IDX_HEADS, :] * (IDX_HEADS * IDX_DIM) ** -0.5


def _dsa_prep(z, ckv_g):
    b, s, _ = z.shape
    t = DSA_TILE
    nt = s // t
    tp = _divisor_tile(s, DSA_PREP_TILE)
    assert tp % t == 0

    def seg(width, off):
        return pl.BlockSpec((1, tp, width), lambda bi, i: (bi, i, off // width))

    return pl.pallas_call(
        _dsa_prep_kernel,
        out_shape=(
            jax.ShapeDtypeStruct((b, A_WIDTH, s), BF16),
            jax.ShapeDtypeStruct((b, IDX_HEADS * IDX_DIM, s), BF16),
            jax.ShapeDtypeStruct((b, s, KV_RANK), BF16),
            jax.ShapeDtypeStruct((b, nt, KV_RANK + SUM_ROWS, t), BF16),
            jax.ShapeDtypeStruct((b, s, IDX_DIM), BF16),
            jax.ShapeDtypeStruct((b, IDX_HEADS, s), F32),
        ),
        grid=(b, s // tp),
        in_specs=[
            seg(A_WIDTH, Z_Q), seg(IDX_HEADS * IDX_DIM, Z_QI), seg(KV_RANK, Z_CKV), seg(LORA_PAD, Z_KIWI),
            pl.BlockSpec((1, KV_RANK), lambda bi, i: (0, 0)),
        ],
        out_specs=(
            pl.BlockSpec((1, A_WIDTH, tp), lambda bi, i: (bi, 0, i)),
            pl.BlockSpec((1, IDX_HEADS * IDX_DIM, tp), lambda bi, i: (bi, 0, i)),
            pl.BlockSpec((1, tp, KV_RANK), lambda bi, i: (bi, i, 0)),
            pl.BlockSpec((1, tp // t, KV_RANK + SUM_ROWS, t), lambda bi, i: (bi, i, 0, 0)),
            pl.BlockSpec((1, tp, IDX_DIM), lambda bi, i: (bi, i, 0)),
            pl.BlockSpec((1, IDX_HEADS, tp), lambda bi, i: (bi, 0, i)),
        ),
        compiler_params=_params("parallel", "parallel"),
        name="dsa_prep",
    )(z, z, z, z, ckv_g)


def _bias_kernel(t5_ref, o_ref):
    t = DSA_TILE
    j = lax.broadcasted_iota(I32, (t, t), 0)
    i = lax.broadcasted_iota(I32, (t, t), 1)
    for didx in range(2):
        n = jnp.maximum(didx * t + i - j, 0)
        nf = jnp.maximum(n, 1).astype(F32)
        large = REL_MAX_EXACT + (jnp.log(nf / REL_MAX_EXACT) / math.log(REL_MAX_DIST / REL_MAX_EXACT)
                                 * (REL_BUCKETS - REL_MAX_EXACT)).astype(I32)
        large = jnp.minimum(large, REL_BUCKETS - 1)
        bucket = jnp.where(n < REL_MAX_EXACT, n, large)
        for h in range(A_HEADS):
            val = jnp.zeros((t, t), F32)
            for k in range(REL_BUCKETS):
                val = jnp.where(bucket == k, t5_ref[k, h], val)
            o_ref[(1 - didx) * t:(2 - didx) * t, h * t:(h + 1) * t] = (val - t5_ref[REL_BUCKETS - 1, h]) * LOG2E


def _bias_tiles(t5_bias):
    t = DSA_TILE
    assert t + 1 >= FAR_DISTANCE
    return pl.pallas_call(
        _bias_kernel,
        out_shape=jax.ShapeDtypeStruct((2 * t, A_HEADS * t), F32),
        in_specs=[pl.BlockSpec(memory_space=pltpu.SMEM)],
        out_specs=pl.BlockSpec(memory_space=pltpu.VMEM),
        compiler_params=pltpu.CompilerParams(vmem_limit_bytes=VMEM_LIMIT_BYTES),
        name="t5_bias_tiles",
    )(t5_bias)


def _dsa_kernel(qt_ref, qit_ref, ckv_ref, ckvt_ref, ki_ref, wit_ref, wuk_ref, wuvt_ref, bias_ref,
                o_ref, keys_ref, khi_ref, klo_ref, acc_ref, ot_ref, *, topk, seq_len):
    t = DSA_TILE
    qb = pl.program_id(1)
    nk = qb + 1
    row = lax.broadcasted_iota(I32, (t, t), 0)
    col = lax.broadcasted_iota(I32, (t, t), 1)

    def idx_rows(start, rows, diagonal):
        sl = pl.ds(pl.multiple_of(start, rows), rows)
        kic = ki_ref[0, sl, :]
        acc = jnp.zeros((rows, t), F32)
        for h in range(IDX_HEADS):
            rel = _dot(kic, qit_ref[0, h * IDX_DIM:(h + 1) * IDX_DIM, :])
            acc = acc + jnp.maximum(rel, 0.0) * wit_ref[0, h:h + 1, :]
        acc = acc + 0.0
        bits = pltpu.bitcast(acc, I32)
        key = bits ^ ((bits >> 31) & 0x7FFFFFFF)
        if diagonal:
            key = jnp.where(row <= col, key, KEY_NEG_INF)
        keys_ref[sl, :] = key
        khi_ref[sl, :] = (key >> 16).astype(I16)
        klo_ref[sl, :] = ((key & 0xFFFF) + I16_MIN).astype(I16)

    def idx_group(chunks):
        def body(i, first_chunk):
            idx_rows(first_chunk * t, chunks * t, False)
            return first_chunk + chunks
        return body

    done = lax.fori_loop(0, qb // 8, idx_group(8), 0)
    done = lax.fori_loop(0, (qb // 4) % 2, idx_group(4), done)
    done = lax.fori_loop(0, (qb // 2) % 2, idx_group(2), done)
    lax.fori_loop(0, qb % 2, idx_group(1), done)
    idx_rows(qb * t, t, True)

    def count16(ref, cand):
        cand16 = cand.astype(I16)

        def body(kc, accs):
            x = ref[pl.ds(pl.multiple_of(kc * t, t), t), :]
            ones = jnp.where(x >= cand16, jnp.asarray(1, BF16), jnp.asarray(0, BF16))
            ones = ones.reshape(t // PACKED_ROWS, PACKED_ROWS, t)
            accs = list(accs)
            for i in range(t // PACKED_ROWS):
                accs[i % COUNT_CHAINS] = accs[i % COUNT_CHAINS] + ones[i]
            return tuple(accs)

        accs = (jnp.zeros((PACKED_ROWS, t), BF16),) * COUNT_CHAINS
        accs = lax.fori_loop(0, nk // 2, lambda i, a: body(2 * i + 1, body(2 * i, a)), accs)
        accs = lax.fori_loop(2 * (nk // 2), nk, body, accs)
        total = sum(a.astype(F32) for a in accs)
        return jnp.sum(total, axis=0, keepdims=True).astype(I32)

    def search16(ref, base):
        def accept(cand, cur):
            return jnp.where(base + count16(ref, cand) >= topk, cand, cur)

        v = accept(jnp.zeros((1, t), I32), jnp.full((1, t), I16_MIN, I32))
        return lax.fori_loop(0, 15, lambda i, v: accept(v | (jnp.int32(1) << (14 - i)), v), v)

    hi = search16(khi_ref, 0)
    above = count16(khi_ref, jnp.minimum(hi + 1, -I16_MIN - 1))
    hi16 = hi.astype(I16)

    def keep_low(kc, carry):
        sl = pl.ds(pl.multiple_of(kc * t, t), t)
        klo_ref[sl, :] = jnp.where(khi_ref[sl, :] == hi16, klo_ref[sl, :], jnp.asarray(I16_MIN, I16))
        return carry

    lax.fori_loop(0, nk, keep_low, 0)
    lo = search16(klo_ref, above)
    thr = jnp.maximum((hi << 16) | (lo - I16_MIN), KEY_NEG_INF + 1)

    def count(pred):
        def body(kc, acc):
            k = keys_ref[pl.ds(pl.multiple_of(kc * t, t), t), :]
            m = jnp.where(pred(k, kc), 1, 0).astype(I32)
            return acc + jnp.sum(m.reshape(t // SUBLANES, SUBLANES, t), axis=0)

        acc = lax.fori_loop(0, nk, body, jnp.zeros((SUBLANES, t), I32))
        return jnp.sum(acc, axis=0, keepdims=True)

    @pl.when(jnp.max(count(lambda k, kc: k >= thr)) > topk)
    def _():
        need = topk - count(lambda k, kc: k > thr)
        nbits = (seq_len - 1).bit_length()

        def pos_bit(i, y):
            cand = y | (jnp.int32(1) << (nbits - 1 - i))
            before = count(lambda k, kc: (k == thr) & ((kc * t + row) < cand))
            return jnp.where(before < need, cand, y)

        last_kept = lax.fori_loop(0, nbits, pos_bit, jnp.zeros((1, t), I32))

        def demote(kc, carry):
            sl = pl.ds(pl.multiple_of(kc * t, t), t)
            k = keys_ref[sl, :]
            keys_ref[sl, :] = jnp.where((k == thr) & ((kc * t + row) > last_kept), k - 1, k)
            return carry

        lax.fori_loop(0, nk, demote, 0)

    scale = A_HEAD_DIM ** -0.5 * LOG2E

    hw = A_HEADS * t
    qlat = jnp.concatenate(
        [(_dot(wuk_ref[h], qt_ref[0, h * A_HEAD_DIM:(h + 1) * A_HEAD_DIM, :]) * scale).astype(BF16)
         for h in range(A_HEADS)], axis=1)
    acc_ref[...] = jnp.zeros_like(acc_ref)

    def att_chunks(kc, m, *, chunks, bias_row=None):
        rows = chunks * t
        sl = pl.ds(pl.multiple_of(kc * t, t), rows)
        raw = _dot(ckv_ref[0, sl, :], qlat)
        sel = keys_ref[sl, :] >= thr
        parts = []
        for h in range(A_HEADS):
            lg_h = raw[:, h * t:(h + 1) * t]
            if bias_row is not None:
                lg_h = lg_h + bias_ref[bias_row:bias_row + rows, h * t:(h + 1) * t]
            parts.append(jnp.where(sel, lg_h, MASKED_LOGIT).astype(BF16))
        lg = jnp.concatenate(parts, axis=1)
        m_new = jnp.maximum(m, jnp.max(lg, axis=0, keepdims=True).astype(F32))
        p = jnp.exp2(lg - m_new.astype(BF16))
        acc = acc_ref[...] * jnp.exp2(m - m_new)
        for j in range(chunks):
            acc = acc + _dot(ckvt_ref[0, kc + j], p[j * t:(j + 1) * t])
        acc_ref[...] = acc
        return m_new

    n_far = jnp.maximum(qb - 1, 0)
    has_prev = jnp.minimum(qb, 1)
    m = jnp.full((1, hw), MASKED_LOGIT, F32)
    def far_group(chunks):
        def body(i, carry):
            first_chunk, m = carry
            return first_chunk + chunks, att_chunks(first_chunk, m, chunks=chunks)
        return body

    carry = lax.fori_loop(0, n_far // 4, far_group(4), (0, m))
    carry = lax.fori_loop(0, (n_far // 2) % 2, far_group(2), carry)
    _, m = lax.fori_loop(0, n_far % 2, far_group(1), carry)
    m = lax.fori_loop(0, has_prev, lambda i, m: att_chunks(qb - 1, m, chunks=2, bias_row=0), m)
    lax.fori_loop(0, 1 - has_prev, lambda i, m: att_chunks(qb, m, chunks=1, bias_row=t), m)
    olat = (acc_ref[:KV_RANK, :] / acc_ref[KV_RANK:KV_RANK + 1, :]).astype(BF16)
    for h in range(A_HEADS):
        ot_ref[h * A_HEAD_DIM:(h + 1) * A_HEAD_DIM, :] = _dot(wuvt_ref[h], olat[:, h * t:(h + 1) * t])
    o_ref[0] = ot_ref[...].T.astype(BF16)


def _dsa(qt, qit, ckv, ckvt, ki, wit, wuk, wuvt, bias):
    b, _, s = qt.shape
    t = DSA_TILE
    nt = s // t
    topk = min(TOPK_MAX, s // 4)
    assert s // PACKED_ROWS <= BF16_EXACT_INT
    return pl.pallas_call(
        functools.partial(_dsa_kernel, topk=topk, seq_len=s),
        out_shape=jax.ShapeDtypeStruct((b, s, A_WIDTH), BF16),
        grid=(b, nt),
        in_specs=[
            pl.BlockSpec((1, A_WIDTH, t), lambda bi, i: (bi, 0, i)),
            pl.BlockSpec((1, IDX_HEADS * IDX_DIM, t), lambda bi, i: (bi, 0, i)),
            pl.BlockSpec((1, s, KV_RANK), lambda bi, i: (bi, 0, 0)),
            pl.BlockSpec((1, nt, KV_RANK + SUM_ROWS, t), lambda bi, i: (bi, 0, 0, 0)),
            pl.BlockSpec((1, s, IDX_DIM), lambda bi, i: (bi, 0, 0)),
            pl.BlockSpec((1, IDX_HEADS, t), lambda bi, i: (bi, 0, i)),
            pl.BlockSpec((A_HEADS, KV_RANK, A_HEAD_DIM), lambda bi, i: (0, 0, 0)),
            pl.BlockSpec((A_HEADS, A_HEAD_DIM, KV_RANK), lambda bi, i: (0, 0, 0)),
            pl.BlockSpec((2 * t, A_HEADS * t), lambda bi, i: (0, 0)),
        ],
        out_specs=pl.BlockSpec((1, t, A_WIDTH), lambda bi, i: (bi, i, 0)),
        scratch_shapes=[pltpu.VMEM((s, t), I32), pltpu.VMEM((s, t), I16), pltpu.VMEM((s, t), I16),
                        pltpu.VMEM((KV_RANK + SUM_ROWS, A_HEADS * t), F32), pltpu.VMEM((A_WIDTH, t), F32)],
        compiler_params=_params("parallel", "arbitrary"),
        name="dsa",
    )(qt, qit, ckv, ckvt, ki, wit, wuk, wuvt, bias)


def _split3(x):
    hi = x.astype(BF16)
    r1 = x - hi.astype(F32)
    mid = r1.astype(BF16)
    lo = (r1 - mid.astype(F32)).astype(BF16)
    return hi, mid, lo


def _rwkv_kernel(r_ref, k_ref, v_ref, gl_ref, wl_ref, al_ref,
                 mur_ref, muk_ref, muv_ref, mug_ref, muw_ref, mua_ref,
                 w0_ref, a0_ref, kk_ref, ka_ref, rk_ref, lnw_ref, lnb_ref,
                 w2_ref, a2_ref, g2_ref,
                 o_ref,
                 pr_ref, pk_ref, pv_ref, pg_ref, pw_ref, pa_ref, state_ref):
    tl = RWKV_TILE
    c = RWKV_CHUNK
    nc = tl // c
    npair = R_WIDTH // PAIR

    @pl.when(pl.program_id(1) == 0)
    def _():
        state_ref[...] = jnp.zeros_like(state_ref)
        for ref in (pr_ref, pk_ref, pv_ref, pg_ref, pw_ref, pa_ref):
            ref[...] = jnp.zeros_like(ref)

    def shift_mix(x_ref, prev_ref, mu_ref):
        x = x_ref[0]
        first = lax.broadcasted_iota(I32, x.shape, 0) == 0
        xprev = jnp.where(first, prev_ref[...], pltpu.roll(x, 1, axis=0))
        prev_ref[...] = x[tl - 1:tl, :]
        return x + (xprev - x) * mu_ref[...]

    r = shift_mix(r_ref, pr_ref, mur_ref)
    k = shift_mix(k_ref, pk_ref, muk_ref)
    v = shift_mix(v_ref, pv_ref, muv_ref)
    gl = shift_mix(gl_ref, pg_ref, mug_ref)
    wl = shift_mix(wl_ref, pw_ref, muw_ref)
    al = shift_mix(al_ref, pa_ref, mua_ref)

    w_lin = w0_ref[...] + _dot(jnp.tanh(wl).astype(BF16), w2_ref[...])
    nx = -w_lin
    softplus = jnp.maximum(nx, 0.0) + jnp.log(1.0 + jnp.exp(-jnp.abs(nx)))
    ld = -jnp.exp(-softplus - 0.5)
    a = jax.nn.sigmoid(a0_ref[...] + _dot(al.astype(BF16), a2_ref[...]))
    gate = _dot(jax.nn.sigmoid(gl).astype(BF16), g2_ref[...])

    li = lax.broadcasted_iota(I32, (LANES, LANES), 0) // R_HEAD_DIM
    lj = lax.broadcasted_iota(I32, (LANES, LANES), 1) // R_HEAD_DIM
    head_ones = jnp.where(li == lj, 1.0, 0.0).astype(BF16)

    def head_sum(x):
        parts = []
        for j in range(R_WIDTH // LANES):
            hi, mid, lo = _split3(x[:, j * LANES:(j + 1) * LANES])
            parts.append(_dot(hi, head_ones) + _dot(mid, head_ones) + _dot(lo, head_ones))
        return jnp.concatenate(parts, axis=-1)

    kk = k * kk_ref[...]
    kk = kk / jnp.maximum(jnp.sqrt(head_sum(kk * kk)), 1e-12)
    k2 = k * (1.0 + (a - 1.0) * ka_ref[...])
    bb = kk * a

    ti = lax.broadcasted_iota(I32, (tl, tl), 0)
    tj = lax.broadcasted_iota(I32, (tl, tl), 1)
    tri = jnp.where((ti // c == tj // c) & (tj <= ti), 1.0, 0.0).astype(BF16)
    hi, mid, lo = _split3(ld)
    cum = _dot(tri, hi) + _dot(tri, mid) + _dot(tri, lo)
    cum_end = jnp.broadcast_to(cum.reshape(nc, c, R_WIDTH)[:, c - 1:c, :], (nc, c, R_WIDTH)).reshape(tl, R_WIDTH)
    e_in = jnp.exp(cum)
    e_neg = jnp.exp(-cum)
    e_out = jnp.exp(cum_end - cum)

    nb = nc * npair

    def to_pairs(x):
        x3 = x.reshape(nc, c, R_WIDTH)
        xs = jnp.stack([x3[:, :, p * PAIR:(p + 1) * PAIR] for p in range(npair)], axis=1)
        return xs.reshape(nb, c, PAIR)

    lane_head = lax.broadcasted_iota(I32, (2 * c, PAIR), 1) // R_HEAD_DIM
    row_head = lax.broadcasted_iota(I32, (2 * c, PAIR), 0) // c
    same = lane_head == row_head
    rt = lax.broadcasted_iota(I32, (2 * c, PAIR), 0) % c
    ct = lax.broadcasted_iota(I32, (2 * c, PAIR), 1) % c
    strict = same & (ct < rt)
    incl = same & (ct <= rt)
    eye = jnp.where(same & (ct == rt), 1.0, 0.0).astype(F32)

    def bd(x):
        return jnp.where(same, jnp.concatenate([x, x], axis=1), 0.0)

    lane0 = lax.broadcasted_iota(I32, (c, PAIR), 1) < R_HEAD_DIM

    abar = bd(to_pairs(-kk * jnp.exp(cum - ld))).astype(BF16)
    rbar = bd(to_pairs(r * e_in)).astype(BF16)
    bt = bd(to_pairs(bb * e_neg)).astype(BF16)
    kt = bd(to_pairs(k2 * e_neg)).astype(BF16)
    bk = jnp.concatenate([bd(to_pairs(bb * e_out)), bd(to_pairs(k2 * e_out))], axis=1).astype(BF16)
    v_pl = to_pairs(v)
    vbd = bd(v_pl)
    decay_end = to_pairs(jnp.exp(cum_end))[:, 0:1, :]
    aa = _bdot_nt(jnp.concatenate([abar, rbar], axis=1), jnp.concatenate([bt, kt], axis=1))
    a_ab = jnp.where(strict, aa[:, :2 * c, :PAIR], 0.0)
    a_ak = jnp.where(strict, aa[:, :2 * c, PAIR:], 0.0).astype(BF16)
    a_rb = jnp.where(incl, aa[:, 2 * c:, :PAIR], 0.0).astype(BF16)
    a_rk = jnp.where(incl, aa[:, 2 * c:, PAIR:], 0.0).astype(BF16)
    pw = a_ab.astype(BF16)
    tinv = eye + a_ab
    for _ in range(int(math.log2(c)) - 1):
        pw = _bdot(pw, pw).astype(BF16)
        tinv = tinv + _bdot(tinv.astype(BF16), pw)
    akv = _bdot(a_ak, vbd.astype(BF16))
    wu = _bdot(tinv.astype(BF16), jnp.concatenate([abar, akv.astype(BF16)], axis=2))
    w16 = wu[:, :, :PAIR].astype(BF16)
    u_bd = wu[:, :, PAIR:]

    s_bd = state_ref[...]
    e_parts, rs_parts = [], []
    for ci in range(nc):
        sl = slice(ci * npair, (ci + 1) * npair)
        wr = _bdot_nt(jnp.concatenate([w16[sl], rbar[sl]], axis=1), s_bd.astype(BF16))
        e_bd = wr[:, :2 * c] + u_bd[sl]
        e_parts.append(e_bd)
        rs_parts.append(wr[:, 2 * c:])
        ev = jnp.concatenate([e_bd, vbd[sl]], axis=1)
        ds = _bdot(jnp.swapaxes(ev, 1, 2).astype(BF16), bk[sl])
        s_bd = s_bd * decay_end[sl] + jnp.where(same, ds, 0.0)
    state_ref[...] = s_bd

    ev_all = jnp.concatenate([jnp.concatenate(e_parts, axis=0), vbd], axis=1).astype(BF16)
    y_bd = jnp.concatenate(rs_parts, axis=0) + _bdot(jnp.concatenate([a_rb, a_rk], axis=2), ev_all)
    mean = jnp.sum(y_bd, axis=-1, keepdims=True) * (1.0 / R_HEAD_DIM)
    dev = jnp.where(same, y_bd - mean, 0.0)
    var = jnp.sum(dev * dev, axis=-1, keepdims=True) * (1.0 / R_HEAD_DIM)
    yn = dev * lax.rsqrt(var + GN_EPS)
    yn = yn[:, :c] + yn[:, c:]
    rkv = to_pairs(r * k2 * rk_ref[...])
    s0 = jnp.sum(jnp.where(lane0, rkv, 0.0), axis=-1, keepdims=True)
    s1 = jnp.sum(jnp.where(lane0, 0.0, rkv), axis=-1, keepdims=True)
    bonus = jnp.where(lane0, s0, s1) * v_pl
    def per_pair(row_ref):
        rows = [row_ref[:, p * PAIR:(p + 1) * PAIR] for p in range(npair)]
        return jnp.stack(rows * nc, axis=0)

    out = ((yn * per_pair(lnw_ref) + per_pair(lnb_ref) + bonus) * to_pairs(gate)).astype(BF16)
    for ci in range(nc):
        for p in range(npair):
            o_ref[0, ci * c:(ci + 1) * c, p * PAIR:(p + 1) * PAIR] = out[ci * npair + p]


def _rwkv(z, prm):
    b, s, _ = z.shape
    tl = RWKV_TILE

    def seg(width, off):
        return pl.BlockSpec((1, tl, width), lambda bi, i: (bi, i, off // width))

    def full(arr):
        return pl.BlockSpec(arr.shape, lambda bi, i: (0,) * arr.ndim)

    names = ("mu_r", "mu_k", "mu_v", "mu_g", "mu_w", "mu_a", "w0", "a0", "k_k", "k_a", "r_k", "ln_w", "ln_b",
             "w2", "a2", "g2")
    consts = [prm[n] for n in names]
    return pl.pallas_call(
        _rwkv_kernel,
        out_shape=jax.ShapeDtypeStruct((b, s, R_WIDTH), BF16),
        grid=(b, s // tl),
        in_specs=[seg(R_WIDTH, Z_R), seg(R_WIDTH, Z_K), seg(R_WIDTH, Z_V), seg(GATE_LORA, Z_GL),
                  seg(LORA_PAD, Z_WL), seg(LORA_PAD, Z_AL)] + [full(x) for x in consts],
        out_specs=pl.BlockSpec((1, tl, R_WIDTH), lambda bi, i: (bi, i, 0)),
        scratch_shapes=[
            pltpu.VMEM((1, R_WIDTH), F32), pltpu.VMEM((1, R_WIDTH), F32), pltpu.VMEM((1, R_WIDTH), F32),
            pltpu.VMEM((1, GATE_LORA), F32), pltpu.VMEM((1, LORA_PAD), F32), pltpu.VMEM((1, LORA_PAD), F32),
            pltpu.VMEM((R_WIDTH // PAIR, PAIR, PAIR), F32),
        ],
        compiler_params=_params("parallel", "arbitrary"),
        name="rwkv7",
    )(z, z, z, z, z, z, *consts)


def _pad_cols(w, width):
    return jnp.pad(w, ((0, 0), (0, width - w.shape[1])))


def _w_in_layout_kernel(wt_ref, o_ref):
    wt = wt_ref[0]
    cols = wt.shape[1]
    offs = [0]
    for n in (A_WIDTH, KV_RANK, IDX_HEADS * IDX_DIM, IDX_DIM + IDX_HEADS,
              3 * R_WIDTH, DECAY_LORA, AAA_LORA, GATE_LORA):
        offs.append(offs[-1] + n)
    q, ckv, qi, kiwi, rkv, wl, al, gl = (wt[offs[i]:offs[i + 1], :] for i in range(8))

    def padded(x, height):
        return jnp.concatenate([x, jnp.zeros((height - x.shape[0], cols), x.dtype)], axis=0)

    rows = [q, qi, rkv, gl, ckv, padded(kiwi, LORA_PAD), padded(wl, LORA_PAD), padded(al, LORA_PAD),
            jnp.zeros((Z_WIDTH - Z_END, cols), wt.dtype)]
    o_ref[0] = jnp.concatenate(rows, axis=0).T.astype(BF16)


def _layout_w_in(w_in):
    depth, d, p = w_in.shape
    tc = _divisor_tile(d, 256)
    return pl.pallas_call(
        _w_in_layout_kernel,
        out_shape=jax.ShapeDtypeStruct((depth, d, Z_WIDTH), BF16),
        grid=(depth, d // tc),
        in_specs=[pl.BlockSpec((1, p, tc), lambda l, i: (l, 0, i))],
        out_specs=pl.BlockSpec((1, tc, Z_WIDTH), lambda l, i: (l, i, 0)),
        compiler_params=_params("parallel", "parallel"),
        name="w_in_layout",
    )(jnp.swapaxes(w_in, 1, 2))


def _pad_rows(w, rows):
    return jnp.pad(w, ((0, rows - w.shape[0]), (0, 0)))


def _rwkv_params(l, mu, w0, w2, a0, a2, g2, k_k, k_a, r_k, ln_w, ln_b):
    m = mu[l]
    o = [0, R_WIDTH, 2 * R_WIDTH, 3 * R_WIDTH, 3 * R_WIDTH + DECAY_LORA, 3 * R_WIDTH + DECAY_LORA + AAA_LORA]
    row = lambda x: x.reshape(1, -1)
    return {
        "mu_r": row(m[o[0]:o[1]]), "mu_k": row(m[o[1]:o[2]]), "mu_v": row(m[o[2]:o[3]]),
        "mu_w": _pad_cols(row(m[o[3]:o[4]]), LORA_PAD), "mu_a": _pad_cols(row(m[o[4]:o[5]]), LORA_PAD),
        "mu_g": row(m[o[5]:]),
        "w0": row(w0[l]), "a0": row(a0[l]), "k_k": row(k_k[l]), "k_a": row(k_a[l]), "r_k": row(r_k[l]),
        "ln_w": row(ln_w[l]), "ln_b": row(ln_b[l]),
        "w2": _pad_rows(w2[l], LORA_PAD).astype(BF16), "a2": _pad_rows(a2[l], LORA_PAD).astype(BF16),
        "g2": g2[l].astype(BF16),
    }


def kernel(x, c, t5_bias, ada_w, ada_b, norm_g, ffn_w_in, ffn_w_out, w_in, ckv_norm_g, w_uk, w_uv, rwkv_mu, rwkv_w0, rwkv_w2, rwkv_a0, rwkv_a2, rwkv_g2, rwkv_k_k, rwkv_k_a, rwkv_r_k, rwkv_ln_w, rwkv_ln_b, w_out, final_norm_g):
    b, s, d = x.shape
    depth = ada_w.shape[0]
    assert s % DSA_TILE == 0 and s % RWKV_TILE == 0 and d == A_WIDTH + R_WIDTH
    mod = _ada_mod(c, ada_w, ada_b).reshape(depth, b, N_SUB, 3, 1, d)
    bias = _bias_tiles(t5_bias)
    final_g = final_norm_g.reshape(1, d)
    w_in_z = _layout_w_in(w_in)
    w_out16 = w_out.astype(BF16)
    h = x
    for l in range(depth):
        shift = lambda i: mod[l, :, i, 0]
        scale = lambda i: mod[l, :, i, 1]
        gate = lambda i: mod[l, :, i, 2]
        g = lambda i: norm_g[l, i].reshape(1, d)
        h = _ffn(h, g(0), shift(0), scale(0), gate(0), ffn_w_in, ffn_w_out, l, 0, final_g, final_norm=False)
        z = _proj(h, g(1), shift(1), scale(1), w_in_z, l)
        qt, qit, ckv, ckvt, ki, wit = _dsa_prep(z, ckv_norm_g[l].reshape(1, KV_RANK))
        wuk = jnp.transpose(w_uk[l], (1, 0, 2)).astype(BF16)
        wuvt = jnp.transpose(w_uv[l], (1, 2, 0)).astype(BF16)
        o_a = _dsa(qt, qit, ckv, ckvt, ki, wit, wuk, wuvt, bias)
        o_r = _rwkv(z, _rwkv_params(l, rwkv_mu, rwkv_w0, rwkv_w2, rwkv_a0, rwkv_a2, rwkv_g2, rwkv_k_k,
                                    rwkv_k_a, rwkv_r_k, rwkv_ln_w, rwkv_ln_b))
        h = _outproj(o_a, o_r, h, gate(1), w_out16, l)
        h = _ffn(h, g(2), shift(2), scale(2), gate(2), ffn_w_in, ffn_w_out, l, 1, final_g,
                 final_norm=(l == depth - 1))
    return h
```

```python
import functools
import math

import jax
import jax.numpy as jnp
from jax import lax
from jax.experimental import pallas as pl
from jax.experimental.pallas import tpu as pltpu

F32 = jnp.float32
BF16 = jnp.bfloat16
I32 = jnp.int32
I16 = jnp.int16

A_HEADS = 8
A_HEAD_DIM = 128
A_WIDTH = A_HEADS * A_HEAD_DIM
KV_RANK = 256
IDX_HEADS = 16
IDX_DIM = 64
TOPK_MAX = 256
REL_BUCKETS = 32
REL_MAX_EXACT = REL_BUCKETS // 2
REL_MAX_DIST = 128
R_HEAD_DIM = 64
R_WIDTH = 1024
DECAY_LORA = 96
AAA_LORA = 96
GATE_LORA = 256
GN_EPS = 64e-5
RMS_EPS = 1e-6
N_SUB = 3

LANES = 128
SUBLANES = 8
VMEM_LIMIT_BYTES = 56 * 1024 * 1024

LORA_PAD = 128
Z_Q = 0
Z_QI = Z_Q + A_WIDTH
Z_R = Z_QI + IDX_HEADS * IDX_DIM
Z_K = Z_R + R_WIDTH
Z_V = Z_K + R_WIDTH
Z_GL = Z_V + R_WIDTH
Z_CKV = Z_GL + GATE_LORA
Z_KIWI = Z_CKV + KV_RANK
Z_WL = Z_KIWI + LORA_PAD
Z_AL = Z_WL + LORA_PAD
Z_END = Z_AL + LORA_PAD
Z_WIDTH = 6144
Z_SPLIT = Z_R

FFN_TOKEN_TILE = 1024
FFN_HIDDEN_TILE = 512
PROLOGUE_ROWS = 256
ADALN_SLAB_ROWS = 16
ADALN_UNROLL = 8
DSA_TILE = 256
DSA_PREP_TILE = 512
RWKV_CHUNK = 64
RWKV_TILE = 256
PAIR = 2 * R_HEAD_DIM

I16_MIN = -(2 ** 15)
PACKED_ROWS = 2 * SUBLANES
SUM_ROWS = PACKED_ROWS
BF16_EXACT_INT = 256
COUNT_CHAINS = 4
KEY_NEG_INF = -2139095041
MASKED_LOGIT = -(2.0 ** 100)
LOG2E = math.log2(math.e)
FAR_DISTANCE = math.ceil(REL_MAX_EXACT * (REL_MAX_DIST / REL_MAX_EXACT)
                         ** ((REL_BUCKETS - 1 - REL_MAX_EXACT) / (REL_BUCKETS - REL_MAX_EXACT)))


def _dot(a, b):
    return jnp.dot(a, b, preferred_element_type=F32)


def _bdot(a, b):
    return lax.dot_general(a, b, (((2,), (1,)), ((0,), (0,))), preferred_element_type=F32)


def _bdot_nt(a, b):
    return lax.dot_general(a, b, (((2,), (2,)), ((0,), (0,))), preferred_element_type=F32)


def _rms(x, g, eps):
    ms = jnp.mean(x * x, axis=-1, keepdims=True)
    return x * lax.rsqrt(ms + eps) * g


def _divisor_tile(n, pref):
    if n <= pref:
        return n
    t = (pref // LANES) * LANES
    while t > LANES and n % t:
        t -= LANES
    assert n % t == 0, (n, pref)
    return t


def _params(*sem):
    return pltpu.CompilerParams(dimension_semantics=sem, vmem_limit_bytes=VMEM_LIMIT_BYTES)


def _ada_kernel(c_ref, w_ref, b_ref, o_ref):
    c = c_ref[...]
    ca = (c * jax.nn.sigmoid(c)).astype(BF16)
    o_ref[0] = _dot(ca, w_ref[0].astype(BF16)) + b_ref[0]


def _ada_mod(c, ada_w, ada_b):
    depth, d, n = ada_w.shape
    b = c.shape[0]
    bp = -(-b // SUBLANES) * SUBLANES
    cp = jnp.pad(c, ((0, bp - b), (0, 0)))
    tn = _divisor_tile(n, 1024)
    out = pl.pallas_call(
        _ada_kernel,
        out_shape=jax.ShapeDtypeStruct((depth, bp, n), F32),
        grid=(depth, n // tn),
        in_specs=[
            pl.BlockSpec((bp, d), lambda l, j: (0, 0)),
            pl.BlockSpec((1, d, tn), lambda l, j: (l, 0, j)),
            pl.BlockSpec((1, 1, tn), lambda l, j: (l, 0, j)),
        ],
        out_specs=pl.BlockSpec((1, bp, tn), lambda l, j: (l, 0, j)),
        compiler_params=_params("arbitrary", "arbitrary"),
        name="ada_mod",
    )(cp, ada_w, ada_b.reshape(depth, 1, n))
    return out[:, :b]


def _adaln_rows(h_ref, g_ref, scale_ref, shift_ref, hn_ref, base, copy_ref=None):
    rows = h_ref.shape[1]
    g = g_ref[...]
    mul = 1.0 + scale_ref[0]
    add = shift_ref[0]

    def slab(i, carry):
        src = pl.ds(pl.multiple_of(i * ADALN_SLAB_ROWS, ADALN_SLAB_ROWS), ADALN_SLAB_ROWS)
        dst = pl.ds(pl.multiple_of(base + i * ADALN_SLAB_ROWS, ADALN_SLAB_ROWS), ADALN_SLAB_ROWS)
        h = h_ref[0, src, :]
        hn_ref[dst, :] = (_rms(h, g, RMS_EPS) * mul + add).astype(BF16)
        if copy_ref is not None:
            copy_ref[0, dst, :] = h
        return carry

    lax.fori_loop(0, rows // ADALN_SLAB_ROWS, slab, 0, unroll=ADALN_UNROLL)


def _prologue_split(tm):
    rows = _divisor_tile(tm, PROLOGUE_ROWS)
    return rows, tm // rows


def _ffn_kernel(h_ref, g_ref, shift_ref, scale_ref, gate_ref, wg_ref, wu_ref, wo_ref, fg_ref,
                o_ref, hn_ref, *, final_norm, pro_steps):
    f = pl.program_id(2)

    @pl.when(f < pro_steps)
    def _():
        _adaln_rows(h_ref, g_ref, scale_ref, shift_ref, hn_ref, f * h_ref.shape[1], o_ref)

    @pl.when(f >= pro_steps)
    def _():
        hn = hn_ref[...]
        g = _dot(hn, wg_ref[...].astype(BF16))
        u = _dot(hn, wu_ref[...].astype(BF16))
        act = (g * jax.nn.sigmoid(g) * u).astype(BF16)
        o_ref[0] += (0.5 * gate_ref[0]) * _dot(act, wo_ref[...].astype(BF16))

    if final_norm:
        @pl.when(f == pl.num_programs(2) - 1)
        def _():
            o_ref[0] = _rms(o_ref[0], fg_ref[...], RMS_EPS)


def _ffn(h, g, shift, scale, gate, w_in, w_out, layer, which, final_g, *, final_norm):
    b, s, d = h.shape
    ff = w_out.shape[2]
    tm = _divisor_tile(s, FFN_TOKEN_TILE)
    tf = _divisor_tile(ff, FFN_HIDDEN_TILE)
    nf = ff // tf
    hr, pro = _prologue_split(tm)
    vec = pl.BlockSpec((1, 1, d), lambda bi, i, f: (bi, 0, 0))
    row = pl.BlockSpec((1, d), lambda bi, i, f: (0, 0))

    def wf(f):
        return jnp.maximum(f - pro, 0)

    return pl.pallas_call(
        functools.partial(_ffn_kernel, final_norm=final_norm, pro_steps=pro),
        out_shape=jax.ShapeDtypeStruct((b, s, d), F32),
        grid=(b, s // tm, pro + nf),
        in_specs=[
            pl.BlockSpec((1, hr, d), lambda bi, i, f: (bi, i * pro + jnp.minimum(f, pro - 1), 0)),
            row, vec, vec, vec,
            pl.BlockSpec((None, None, d, tf), lambda bi, i, f: (layer, which, 0, wf(f))),
            pl.BlockSpec((None, None, d, tf), lambda bi, i, f: (layer, which, 0, nf + wf(f))),
            pl.BlockSpec((None, None, tf, d), lambda bi, i, f: (layer, which, wf(f), 0)),
            row,
        ],
        out_specs=pl.BlockSpec((1, tm, d), lambda bi, i, f: (bi, i, 0)),
        scratch_shapes=[pltpu.VMEM((tm, d), BF16)],
        compiler_params=_params("parallel", "parallel", "arbitrary"),
        name="ffn",
    )(h, g, shift, scale, gate, w_in, w_in, w_out, final_g)


def _proj_kernel(h_ref, g_ref, shift_ref, scale_ref, w_ref, oa_ref, ob_ref, hn_ref, *, attn_tiles):
    n = pl.program_id(2)

    @pl.when(n == 0)
    def _():
        _adaln_rows(h_ref, g_ref, scale_ref, shift_ref, hn_ref, 0)

    z = _dot(hn_ref[...], w_ref[...])

    @pl.when(n < attn_tiles)
    def _():
        oa_ref[0] = z.astype(BF16)

    @pl.when(n >= attn_tiles)
    def _():
        ob_ref[0] = z


def _proj(h, g, shift, scale, w, layer):
    b, s, d = h.shape
    p = w.shape[2]
    tm = _divisor_tile(s, 1024)
    tn = _divisor_tile(Z_SPLIT, 1024)
    assert Z_SPLIT % tn == 0 and (p - Z_SPLIT) % tn == 0
    na = Z_SPLIT // tn
    vec = pl.BlockSpec((1, 1, d), lambda bi, i, n: (bi, 0, 0))
    return pl.pallas_call(
        functools.partial(_proj_kernel, attn_tiles=na),
        out_shape=(jax.ShapeDtypeStruct((b, s, Z_SPLIT), BF16), jax.ShapeDtypeStruct((b, s, p - Z_SPLIT), F32)),
        grid=(b, s // tm, p // tn),
        in_specs=[
            pl.BlockSpec((1, tm, d), lambda bi, i, n: (bi, i, 0)),
            pl.BlockSpec((1, d), lambda bi, i, n: (0, 0)),
            vec, vec,
            pl.BlockSpec((None, d, tn), lambda bi, i, n: (layer, 0, n)),
        ],
        out_specs=(pl.BlockSpec((1, tm, tn), lambda bi, i, n: (bi, i, jnp.minimum(n, na - 1))),
                   pl.BlockSpec((1, tm, tn), lambda bi, i, n: (bi, i, jnp.maximum(n - na, 0)))),
        scratch_shapes=[pltpu.VMEM((tm, d), BF16)],
        compiler_params=_params("parallel", "parallel", "arbitrary"),
        name="proj",
    )(h, g, shift, scale, w)


def _outproj_kernel(oa_ref, or_ref, h_ref, gate_ref, wa_ref, wr_ref, o_ref):
    acc = _dot(oa_ref[0], wa_ref[...]) + _dot(or_ref[0], wr_ref[...])
    o_ref[0] = h_ref[0] + gate_ref[0] * acc


def _outproj(o_a, o_r, h, gate, w_out, layer):
    b, s, d = h.shape
    tm = _divisor_tile(s, 512)
    tn = d
    wa = o_a.shape[-1]
    wr = o_r.shape[-1]
    return pl.pallas_call(
        _outproj_kernel,
        out_shape=jax.ShapeDtypeStruct((b, s, d), F32),
        grid=(b, s // tm, d // tn),
        in_specs=[
            pl.BlockSpec((1, tm, wa), lambda bi, i, n: (bi, i, 0)),
            pl.BlockSpec((1, tm, wr), lambda bi, i, n: (bi, i, 0)),
            pl.BlockSpec((1, tm, tn), lambda bi, i, n: (bi, i, n)),
            pl.BlockSpec((1, 1, tn), lambda bi, i, n: (bi, 0, n)),
            pl.BlockSpec((None, wa, tn), lambda bi, i, n: (layer, 0, n)),
            pl.BlockSpec((None, wr, tn), lambda bi, i, n: (layer, wa // wr, n)),
        ],
        out_specs=pl.BlockSpec((1, tm, tn), lambda bi, i, n: (bi, i, n)),
        compiler_params=_params("parallel", "parallel", "arbitrary"),
        name="outproj",
    )(o_a, o_r, h, gate, w_out, w_out)


def _dsa_prep_kernel(q_ref, qi_ref, ckv_ref, kiwi_ref, g_ref, qt_ref, qit_ref, ckv_o, ckvt_o, ki_o, wit_o):
    qt_ref[0] = q_ref[0].T
    qit_ref[0] = qi_ref[0].T
    cn = _rms(ckv_ref[0], g_ref[...], RMS_EPS)
    ckv_o[0] = cn.astype(BF16)
    t = DSA_TILE
    ones = jnp.ones((SUM_ROWS, t), BF16)
    for j in range(cn.shape[0] // t):
        ckvt_o[0, j] = jnp.concatenate([cn[j * t:(j + 1) * t].T.astype(BF16), ones], axis=0)
    kw = kiwi_ref[0]
    ki_o[0] = kw[:, :IDX_DIM].astype(BF16)
    wit_o[0] = kw.T[IDX_DIM:IDX_DIM + IDX_HEADS, :] * (IDX_HEADS * IDX_DIM) ** -0.5


def _dsa_prep(za, zb, ckv_g):
    b, s, _ = za.shape
    t = DSA_TILE
    nt = s // t
    tp = _divisor_tile(s, DSA_PREP_TILE)
    assert tp % t == 0

    def seg(width, off):
        return pl.BlockSpec((1, tp, width), lambda bi, i: (bi, i, (off if off < Z_SPLIT else off - Z_SPLIT) // width))

    return pl.pallas_call(
        _dsa_prep_kernel,
        out_shape=(
            jax.ShapeDtypeStruct((b, A_WIDTH, s), BF16),
            jax.ShapeDtypeStruct((b, IDX_HEADS * IDX_DIM, s), BF16),
            jax.ShapeDtypeStruct((b, s, KV_RANK), BF16),
            jax.ShapeDtypeStruct((b, nt, KV_RANK + SUM_ROWS, t), BF16),
            jax.ShapeDtypeStruct((b, s, IDX_DIM), BF16),
            jax.ShapeDtypeStruct((b, IDX_HEADS, s), F32),
        ),
        grid=(b, s // tp),
        in_specs=[
            seg(A_WIDTH, Z_Q), seg(IDX_HEADS * IDX_DIM, Z_QI), seg(KV_RANK, Z_CKV), seg(LORA_PAD, Z_KIWI),
            pl.BlockSpec((1, KV_RANK), lambda bi, i: (0, 0)),
        ],
        out_specs=(
            pl.BlockSpec((1, A_WIDTH, tp), lambda bi, i: (bi, 0, i)),
            pl.BlockSpec((1, IDX_HEADS * IDX_DIM, tp), lambda bi, i: (bi, 0, i)),
            pl.BlockSpec((1, tp, KV_RANK), lambda bi, i: (bi, i, 0)),
            pl.BlockSpec((1, tp // t, KV_RANK + SUM_ROWS, t), lambda bi, i: (bi, i, 0, 0)),
            pl.BlockSpec((1, tp, IDX_DIM), lambda bi, i: (bi, i, 0)),
            pl.BlockSpec((1, IDX_HEADS, tp), lambda bi, i: (bi, 0, i)),
        ),
        compiler_params=_params("parallel", "parallel"),
        name="dsa_prep",
    )(za, za, zb, zb, ckv_g)


def _bias_kernel(t5_ref, o_ref):
    t = DSA_TILE
    j = lax.broadcasted_iota(I32, (t, t), 0)
    i = lax.broadcasted_iota(I32, (t, t), 1)
    for didx in range(2):
        n = jnp.maximum(didx * t + i - j, 0)
        nf = jnp.maximum(n, 1).astype(F32)
        large = REL_MAX_EXACT + (jnp.log(nf / REL_MAX_EXACT) / math.log(REL_MAX_DIST / REL_MAX_EXACT)
                                 * (REL_BUCKETS - REL_MAX_EXACT)).astype(I32)
        large = jnp.minimum(large, REL_BUCKETS - 1)
        bucket = jnp.where(n < REL_MAX_EXACT, n, large)
        for h in range(A_HEADS):
            val = jnp.zeros((t, t), F32)
            for k in range(REL_BUCKETS):
                val = jnp.where(bucket == k, t5_ref[k, h], val)
            o_ref[(1 - didx) * t:(2 - didx) * t, h * t:(h + 1) * t] = (val - t5_ref[REL_BUCKETS - 1, h]) * LOG2E


def _bias_tiles(t5_bias):
    t = DSA_TILE
    assert t + 1 >= FAR_DISTANCE
    return pl.pallas_call(
        _bias_kernel,
        out_shape=jax.ShapeDtypeStruct((2 * t, A_HEADS * t), F32),
        in_specs=[pl.BlockSpec(memory_space=pltpu.SMEM)],
        out_specs=pl.BlockSpec(memory_space=pltpu.VMEM),
        compiler_params=pltpu.CompilerParams(vmem_limit_bytes=VMEM_LIMIT_BYTES),
        name="t5_bias_tiles",
    )(t5_bias)


def _dsa_kernel(qt_ref, qit_ref, ckv_ref, ckvt_ref, ki_ref, wit_ref, wuk_ref, wuvt_ref, bias_ref,
                o_ref, keys_ref, khi_ref, klo_ref, acc_ref, ot_ref, *, topk, seq_len):
    t = DSA_TILE
    qb = pl.program_id(1)
    nk = qb + 1
    row = lax.broadcasted_iota(I32, (t, t), 0)
    col = lax.broadcasted_iota(I32, (t, t), 1)

    def idx_rows(start, rows, diagonal):
        sl = pl.ds(pl.multiple_of(start, rows), rows)
        kic = ki_ref[0, sl, :]
        acc = jnp.zeros((rows, t), F32)
        for h in range(IDX_HEADS):
            rel = _dot(kic, qit_ref[0, h * IDX_DIM:(h + 1) * IDX_DIM, :])
            acc = acc + jnp.maximum(rel, 0.0) * wit_ref[0, h:h + 1, :]
        acc = acc + 0.0
        bits = pltpu.bitcast(acc, I32)
        key = bits ^ ((bits >> 31) & 0x7FFFFFFF)
        if diagonal:
            key = jnp.where(row <= col, key, KEY_NEG_INF)
        keys_ref[sl, :] = key
        khi_ref[sl, :] = (key >> 16).astype(I16)
        klo_ref[sl, :] = ((key & 0xFFFF) + I16_MIN).astype(I16)

    def idx_group(chunks):
        def body(i, first_chunk):
            idx_rows(first_chunk * t, chunks * t, False)
            return first_chunk + chunks
        return body

    done = lax.fori_loop(0, qb // 8, idx_group(8), 0)
    done = lax.fori_loop(0, (qb // 4) % 2, idx_group(4), done)
    done = lax.fori_loop(0, (qb // 2) % 2, idx_group(2), done)
    lax.fori_loop(0, qb % 2, idx_group(1), done)
    idx_rows(qb * t, t, True)

    def count16(ref, cand):
        cand16 = cand.astype(I16)

        def body(kc, accs):
            x = ref[pl.ds(pl.multiple_of(kc * t, t), t), :]
            ones = jnp.where(x >= cand16, jnp.asarray(1, BF16), jnp.asarray(0, BF16))
            ones = ones.reshape(t // PACKED_ROWS, PACKED_ROWS, t)
            accs = list(accs)
            for i in range(t // PACKED_ROWS):
                accs[i % COUNT_CHAINS] = accs[i % COUNT_CHAINS] + ones[i]
            return tuple(accs)

        accs = (jnp.zeros((PACKED_ROWS, t), BF16),) * COUNT_CHAINS
        accs = lax.fori_loop(0, nk // 2, lambda i, a: body(2 * i + 1, body(2 * i, a)), accs)
        accs = lax.fori_loop(2 * (nk // 2), nk, body, accs)
        total = sum(a.astype(F32) for a in accs)
        return jnp.sum(total, axis=0, keepdims=True).astype(I32)

    def search16(ref, base):
        def accept(cand, cur):
            return jnp.where(base + count16(ref, cand) >= topk, cand, cur)

        v = accept(jnp.zeros((1, t), I32), jnp.full((1, t), I16_MIN, I32))
        return lax.fori_loop(0, 15, lambda i, v: accept(v | (jnp.int32(1) << (14 - i)), v), v)

    hi = search16(khi_ref, 0)
    above = count16(khi_ref, jnp.minimum(hi + 1, -I16_MIN - 1))
    hi16 = hi.astype(I16)

    def keep_low(kc, carry):
        sl = pl.ds(pl.multiple_of(kc * t, t), t)
        klo_ref[sl, :] = jnp.where(khi_ref[sl, :] == hi16, klo_ref[sl, :], jnp.asarray(I16_MIN, I16))
        return carry

    lax.fori_loop(0, nk, keep_low, 0)
    lo = search16(klo_ref, above)
    thr = jnp.maximum((hi << 16) | (lo - I16_MIN), KEY_NEG_INF + 1)

    def count(pred):
        def body(kc, acc):
            k = keys_ref[pl.ds(pl.multiple_of(kc * t, t), t), :]
            m = jnp.where(pred(k, kc), 1, 0).astype(I32)
            return acc + jnp.sum(m.reshape(t // SUBLANES, SUBLANES, t), axis=0)

        acc = lax.fori_loop(0, nk, body, jnp.zeros((SUBLANES, t), I32))
        return jnp.sum(acc, axis=0, keepdims=True)

    @pl.when(jnp.max(count(lambda k, kc: k >= thr)) > topk)
    def _():
        need = topk - count(lambda k, kc: k > thr)
        nbits = (seq_len - 1).bit_length()

        def pos_bit(i, y):
            cand = y | (jnp.int32(1) << (nbits - 1 - i))
            before = count(lambda k, kc: (k == thr) & ((kc * t + row) < cand))
            return jnp.where(before < need, cand, y)

        last_kept = lax.fori_loop(0, nbits, pos_bit, jnp.zeros((1, t), I32))

        def demote(kc, carry):
            sl = pl.ds(pl.multiple_of(kc * t, t), t)
            k = keys_ref[sl, :]
            keys_ref[sl, :] = jnp.where((k == thr) & ((kc * t + row) > last_kept), k - 1, k)
            return carry

        lax.fori_loop(0, nk, demote, 0)

    scale = A_HEAD_DIM ** -0.5 * LOG2E

    hw = A_HEADS * t
    qlat = jnp.concatenate(
        [(_dot(wuk_ref[h], qt_ref[0, h * A_HEAD_DIM:(h + 1) * A_HEAD_DIM, :]) * scale).astype(BF16)
         for h in range(A_HEADS)], axis=1)
    acc_ref[...] = jnp.zeros_like(acc_ref)

    def att_chunks(kc, m, *, chunks, bias_row=None):
        rows = chunks * t
        sl = pl.ds(pl.multiple_of(kc * t, t), rows)
        raw = _dot(ckv_ref[0, sl, :], qlat)
        sel = keys_ref[sl, :] >= thr
        parts = []
        for h in range(A_HEADS):
            lg_h = raw[:, h * t:(h + 1) * t]
            if bias_row is not None:
                lg_h = lg_h + bias_ref[bias_row:bias_row + rows, h * t:(h + 1) * t]
            parts.append(jnp.where(sel, lg_h, MASKED_LOGIT).astype(BF16))
        lg = jnp.concatenate(parts, axis=1)
        m_new = jnp.maximum(m, jnp.max(lg, axis=0, keepdims=True).astype(F32))
        p = jnp.exp2(lg - m_new.astype(BF16))
        acc = acc_ref[...] * jnp.exp2(m - m_new)
        for j in range(chunks):
            acc = acc + _dot(ckvt_ref[0, kc + j], p[j * t:(j + 1) * t])
        acc_ref[...] = acc
        return m_new

    n_far = jnp.maximum(qb - 1, 0)
    has_prev = jnp.minimum(qb, 1)
    m = jnp.full((1, hw), MASKED_LOGIT, F32)
    def far_group(chunks):
        def body(i, carry):
            first_chunk, m = carry
            return first_chunk + chunks, att_chunks(first_chunk, m, chunks=chunks)
        return body

    carry = lax.fori_loop(0, n_far // 4, far_group(4), (0, m))
    carry = lax.fori_loop(0, (n_far // 2) % 2, far_group(2), carry)
    _, m = lax.fori_loop(0, n_far % 2, far_group(1), carry)
    m = lax.fori_loop(0, has_prev, lambda i, m: att_chunks(qb - 1, m, chunks=2, bias_row=0), m)
    lax.fori_loop(0, 1 - has_prev, lambda i, m: att_chunks(qb, m, chunks=1, bias_row=t), m)
    olat = (acc_ref[:KV_RANK, :] / acc_ref[KV_RANK:KV_RANK + 1, :]).astype(BF16)
    for h in range(A_HEADS):
        ot_ref[h * A_HEAD_DIM:(h + 1) * A_HEAD_DIM, :] = _dot(wuvt_ref[h], olat[:, h * t:(h + 1) * t])
    o_ref[0] = ot_ref[...].T.astype(BF16)


def _dsa(qt, qit, ckv, ckvt, ki, wit, wuk, wuvt, bias):
    b, _, s = qt.shape
    t = DSA_TILE
    nt = s // t
    topk = min(TOPK_MAX, s // 4)
    assert s // PACKED_ROWS <= BF16_EXACT_INT
    return pl.pallas_call(
        functools.partial(_dsa_kernel, topk=topk, seq_len=s),
        out_shape=jax.ShapeDtypeStruct((b, s, A_WIDTH), BF16),
        grid=(b, nt),
        in_specs=[
            pl.BlockSpec((1, A_WIDTH, t), lambda bi, i: (bi, 0, i)),
            pl.BlockSpec((1, IDX_HEADS * IDX_DIM, t), lambda bi, i: (bi, 0, i)),
            pl.BlockSpec((1, s, KV_RANK), lambda bi, i: (bi, 0, 0)),
            pl.BlockSpec((1, nt, KV_RANK + SUM_ROWS, t), lambda bi, i: (bi, 0, 0, 0)),
            pl.BlockSpec((1, s, IDX_DIM), lambda bi, i: (bi, 0, 0)),
            pl.BlockSpec((1, IDX_HEADS, t), lambda bi, i: (bi, 0, i)),
            pl.BlockSpec((A_HEADS, KV_RANK, A_HEAD_DIM), lambda bi, i: (0, 0, 0)),
            pl.BlockSpec((A_HEADS, A_HEAD_DIM, KV_RANK), lambda bi, i: (0, 0, 0)),
            pl.BlockSpec((2 * t, A_HEADS * t), lambda bi, i: (0, 0)),
        ],
        out_specs=pl.BlockSpec((1, t, A_WIDTH), lambda bi, i: (bi, i, 0)),
        scratch_shapes=[pltpu.VMEM((s, t), I32), pltpu.VMEM((s, t), I16), pltpu.VMEM((s, t), I16),
                        pltpu.VMEM((KV_RANK + SUM_ROWS, A_HEADS * t), F32), pltpu.VMEM((A_WIDTH, t), F32)],
        compiler_params=_params("parallel", "arbitrary"),
        name="dsa",
    )(qt, qit, ckv, ckvt, ki, wit, wuk, wuvt, bias)


def _split3(x):
    hi = x.astype(BF16)
    r1 = x - hi.astype(F32)
    mid = r1.astype(BF16)
    lo = (r1 - mid.astype(F32)).astype(BF16)
    return hi, mid, lo


def _rwkv_kernel(r_ref, k_ref, v_ref, gl_ref, wl_ref, al_ref,
                 mur_ref, muk_ref, muv_ref, mug_ref, muw_ref, mua_ref,
                 w0_ref, a0_ref, kk_ref, ka_ref, rk_ref, lnw_ref, lnb_ref,
                 w2_ref, a2_ref, g2_ref,
                 o_ref,
                 pr_ref, pk_ref, pv_ref, pg_ref, pw_ref, pa_ref, state_ref):
    tl = RWKV_TILE
    c = RWKV_CHUNK
    nc = tl // c
    npair = R_WIDTH // PAIR

    @pl.when(pl.program_id(1) == 0)
    def _():
        state_ref[...] = jnp.zeros_like(state_ref)
        for ref in (pr_ref, pk_ref, pv_ref, pg_ref, pw_ref, pa_ref):
            ref[...] = jnp.zeros_like(ref)

    def shift_mix(x_ref, prev_ref, mu_ref):
        x = x_ref[0]
        first = lax.broadcasted_iota(I32, x.shape, 0) == 0
        xprev = jnp.where(first, prev_ref[...], pltpu.roll(x, 1, axis=0))
        prev_ref[...] = x[tl - 1:tl, :]
        return x + (xprev - x) * mu_ref[...]

    r = shift_mix(r_ref, pr_ref, mur_ref)
    k = shift_mix(k_ref, pk_ref, muk_ref)
    v = shift_mix(v_ref, pv_ref, muv_ref)
    gl = shift_mix(gl_ref, pg_ref, mug_ref)
    wl = shift_mix(wl_ref, pw_ref, muw_ref)
    al = shift_mix(al_ref, pa_ref, mua_ref)

    w_lin = w0_ref[...] + _dot(jnp.tanh(wl).astype(BF16), w2_ref[...])
    nx = -w_lin
    softplus = jnp.maximum(nx, 0.0) + jnp.log(1.0 + jnp.exp(-jnp.abs(nx)))
    ld = -jnp.exp(-softplus - 0.5)
    a = jax.nn.sigmoid(a0_ref[...] + _dot(al.astype(BF16), a2_ref[...]))
    gate = _dot(jax.nn.sigmoid(gl).astype(BF16), g2_ref[...])

    li = lax.broadcasted_iota(I32, (LANES, LANES), 0) // R_HEAD_DIM
    lj = lax.broadcasted_iota(I32, (LANES, LANES), 1) // R_HEAD_DIM
    head_ones = jnp.where(li == lj, 1.0, 0.0).astype(BF16)

    def head_sum(x):
        parts = []
        for j in range(R_WIDTH // LANES):
            hi, mid, lo = _split3(x[:, j * LANES:(j + 1) * LANES])
            parts.append(_dot(hi, head_ones) + _dot(mid, head_ones) + _dot(lo, head_ones))
        return jnp.concatenate(parts, axis=-1)

    kk = k * kk_ref[...]
    kk = kk / jnp.maximum(jnp.sqrt(head_sum(kk * kk)), 1e-12)
    k2 = k * (1.0 + (a - 1.0) * ka_ref[...])
    bb = kk * a

    ti = lax.broadcasted_iota(I32, (tl, tl), 0)
    tj = lax.broadcasted_iota(I32, (tl, tl), 1)
    tri = jnp.where((ti // c == tj // c) & (tj <= ti), 1.0, 0.0).astype(BF16)
    hi, mid, lo = _split3(ld)
    cum = _dot(tri, hi) + _dot(tri, mid) + _dot(tri, lo)
    cum_end = jnp.broadcast_to(cum.reshape(nc, c, R_WIDTH)[:, c - 1:c, :], (nc, c, R_WIDTH)).reshape(tl, R_WIDTH)
    e_in = jnp.exp(cum)
    e_neg = jnp.exp(-cum)
    e_out = jnp.exp(cum_end - cum)

    nb = nc * npair

    def to_pairs(x):
        x3 = x.reshape(nc, c, R_WIDTH)
        xs = jnp.stack([x3[:, :, p * PAIR:(p + 1) * PAIR] for p in range(npair)], axis=1)
        return xs.reshape(nb, c, PAIR)

    lane_head = lax.broadcasted_iota(I32, (2 * c, PAIR), 1) // R_HEAD_DIM
    row_head = lax.broadcasted_iota(I32, (2 * c, PAIR), 0) // c
    same = lane_head == row_head
    rt = lax.broadcasted_iota(I32, (2 * c, PAIR), 0) % c
    ct = lax.broadcasted_iota(I32, (2 * c, PAIR), 1) % c
    strict = same & (ct < rt)
    incl = same & (ct <= rt)
    eye = jnp.where(same & (ct == rt), 1.0, 0.0).astype(F32)

    def bd(x):
        return jnp.where(same, jnp.concatenate([x, x], axis=1), 0.0)

    lane0 = lax.broadcasted_iota(I32, (c, PAIR), 1) < R_HEAD_DIM

    abar = bd(to_pairs(-kk * jnp.exp(cum - ld))).astype(BF16)
    rbar = bd(to_pairs(r * e_in)).astype(BF16)
    bt = bd(to_pairs(bb * e_neg)).astype(BF16)
    kt = bd(to_pairs(k2 * e_neg)).astype(BF16)
    bk = jnp.concatenate([bd(to_pairs(bb * e_out)), bd(to_pairs(k2 * e_out))], axis=1).astype(BF16)
    v_pl = to_pairs(v)
    vbd = bd(v_pl)
    decay_end = to_pairs(jnp.exp(cum_end))[:, 0:1, :]
    aa = _bdot_nt(jnp.concatenate([abar, rbar], axis=1), jnp.concatenate([bt, kt], axis=1))
    a_ab = jnp.where(strict, aa[:, :2 * c, :PAIR], 0.0)
    a_ak = jnp.where(strict, aa[:, :2 * c, PAIR:], 0.0).astype(BF16)
    a_rb = jnp.where(incl, aa[:, 2 * c:, :PAIR], 0.0).astype(BF16)
    a_rk = jnp.where(incl, aa[:, 2 * c:, PAIR:], 0.0).astype(BF16)
    pw = a_ab.astype(BF16)
    tinv = eye + a_ab
    for _ in range(int(math.log2(c)) - 1):
        pw = _bdot(pw, pw).astype(BF16)
        tinv = tinv + _bdot(tinv.astype(BF16), pw)
    akv = _bdot(a_ak, vbd.astype(BF16))
    wu = _bdot(tinv.astype(BF16), jnp.concatenate([abar, akv.astype(BF16)], axis=2))
    w16 = wu[:, :, :PAIR].astype(BF16)
    u_bd = wu[:, :, PAIR:]

    s_bd = state_ref[...]
    e_parts, rs_parts = [], []
    for ci in range(nc):
        sl = slice(ci * npair, (ci + 1) * npair)
        wr = _bdot_nt(jnp.concatenate([w16[sl], rbar[sl]], axis=1), s_bd.astype(BF16))
        e_bd = wr[:, :2 * c] + u_bd[sl]
        e_parts.append(e_bd)
        rs_parts.append(wr[:, 2 * c:])
        ev = jnp.concatenate([e_bd, vbd[sl]], axis=1)
        ds = _bdot(jnp.swapaxes(ev, 1, 2).astype(BF16), bk[sl])
        s_bd = s_bd * decay_end[sl] + jnp.where(same, ds, 0.0)
    state_ref[...] = s_bd

    ev_all = jnp.concatenate([jnp.concatenate(e_parts, axis=0), vbd], axis=1).astype(BF16)
    y_bd = jnp.concatenate(rs_parts, axis=0) + _bdot(jnp.concatenate([a_rb, a_rk], axis=2), ev_all)
    mean = jnp.sum(y_bd, axis=-1, keepdims=True) * (1.0 / R_HEAD_DIM)
    dev = jnp.where(same, y_bd - mean, 0.0)
    var = jnp.sum(dev * dev, axis=-1, keepdims=True) * (1.0 / R_HEAD_DIM)
    yn = dev * lax.rsqrt(var + GN_EPS)
    yn = yn[:, :c] + yn[:, c:]
    rkv = to_pairs(r * k2 * rk_ref[...])
    s0 = jnp.sum(jnp.where(lane0, rkv, 0.0), axis=-1, keepdims=True)
    s1 = jnp.sum(jnp.where(lane0, 0.0, rkv), axis=-1, keepdims=True)
    bonus = jnp.where(lane0, s0, s1) * v_pl
    def per_pair(row_ref):
        rows = [row_ref[:, p * PAIR:(p + 1) * PAIR] for p in range(npair)]
        return jnp.stack(rows * nc, axis=0)

    out = ((yn * per_pair(lnw_ref) + per_pair(lnb_ref) + bonus) * to_pairs(gate)).astype(BF16)
    for ci in range(nc):
        for p in range(npair):
            o_ref[0, ci * c:(ci + 1) * c, p * PAIR:(p + 1) * PAIR] = out[ci * npair + p]


def _rwkv(zb, prm):
    b, s, _ = zb.shape
    tl = RWKV_TILE

    def seg(width, off):
        return pl.BlockSpec((1, tl, width), lambda bi, i: (bi, i, (off - Z_SPLIT) // width))

    def full(arr):
        return pl.BlockSpec(arr.shape, lambda bi, i: (0,) * arr.ndim)

    names = ("mu_r", "mu_k", "mu_v", "mu_g", "mu_w", "mu_a", "w0", "a0", "k_k", "k_a", "r_k", "ln_w", "ln_b",
             "w2", "a2", "g2")
    consts = [prm[n] for n in names]
    return pl.pallas_call(
        _rwkv_kernel,
        out_shape=jax.ShapeDtypeStruct((b, s, R_WIDTH), BF16),
        grid=(b, s // tl),
        in_specs=[seg(R_WIDTH, Z_R), seg(R_WIDTH, Z_K), seg(R_WIDTH, Z_V), seg(GATE_LORA, Z_GL),
                  seg(LORA_PAD, Z_WL), seg(LORA_PAD, Z_AL)] + [full(x) for x in consts],
        out_specs=pl.BlockSpec((1, tl, R_WIDTH), lambda bi, i: (bi, i, 0)),
        scratch_shapes=[
            pltpu.VMEM((1, R_WIDTH), F32), pltpu.VMEM((1, R_WIDTH), F32), pltpu.VMEM((1, R_WIDTH), F32),
            pltpu.VMEM((1, GATE_LORA), F32), pltpu.VMEM((1, LORA_PAD), F32), pltpu.VMEM((1, LORA_PAD), F32),
            pltpu.VMEM((R_WIDTH // PAIR, PAIR, PAIR), F32),
        ],
        compiler_params=_params("parallel", "arbitrary"),
        name="rwkv7",
    )(zb, zb, zb, zb, zb, zb, *consts)


def _pad_cols(w, width):
    return jnp.pad(w, ((0, 0), (0, width - w.shape[1])))


def _w_in_layout_kernel(wt_ref, o_ref):
    wt = wt_ref[0]
    cols = wt.shape[1]
    offs = [0]
    for n in (A_WIDTH, KV_RANK, IDX_HEADS * IDX_DIM, IDX_DIM + IDX_HEADS,
              3 * R_WIDTH, DECAY_LORA, AAA_LORA, GATE_LORA):
        offs.append(offs[-1] + n)
    q, ckv, qi, kiwi, rkv, wl, al, gl = (wt[offs[i]:offs[i + 1], :] for i in range(8))

    def padded(x, height):
        return jnp.concatenate([x, jnp.zeros((height - x.shape[0], cols), x.dtype)], axis=0)

    rows = [q, qi, rkv, gl, ckv, padded(kiwi, LORA_PAD), padded(wl, LORA_PAD), padded(al, LORA_PAD),
            jnp.zeros((Z_WIDTH - Z_END, cols), wt.dtype)]
    o_ref[0] = jnp.concatenate(rows, axis=0).T.astype(BF16)


def _layout_w_in(w_in):
    depth, d, p = w_in.shape
    tc = _divisor_tile(d, 256)
    return pl.pallas_call(
        _w_in_layout_kernel,
        out_shape=jax.ShapeDtypeStruct((depth, d, Z_WIDTH), BF16),
        grid=(depth, d // tc),
        in_specs=[pl.BlockSpec((1, p, tc), lambda l, i: (l, 0, i))],
        out_specs=pl.BlockSpec((1, tc, Z_WIDTH), lambda l, i: (l, i, 0)),
        compiler_params=_params("parallel", "parallel"),
        name="w_in_layout",
    )(jnp.swapaxes(w_in, 1, 2))


def _pad_rows(w, rows):
    return jnp.pad(w, ((0, rows - w.shape[0]), (0, 0)))


def _rwkv_params(l, mu, w0, w2, a0, a2, g2, k_k, k_a, r_k, ln_w, ln_b):
    m = mu[l]
    o = [0, R_WIDTH, 2 * R_WIDTH, 3 * R_WIDTH, 3 * R_WIDTH + DECAY_LORA, 3 * R_WIDTH + DECAY_LORA + AAA_LORA]
    row = lambda x: x.reshape(1, -1)
    return {
        "mu_r": row(m[o[0]:o[1]]), "mu_k": row(m[o[1]:o[2]]), "mu_v": row(m[o[2]:o[3]]),
        "mu_w": _pad_cols(row(m[o[3]:o[4]]), LORA_PAD), "mu_a": _pad_cols(row(m[o[4]:o[5]]), LORA_PAD),
        "mu_g": row(m[o[5]:]),
        "w0": row(w0[l]), "a0": row(a0[l]), "k_k": row(k_k[l]), "k_a": row(k_a[l]), "r_k": row(r_k[l]),
        "ln_w": row(ln_w[l]), "ln_b": row(ln_b[l]),
        "w2": _pad_rows(w2[l], LORA_PAD).astype(BF16), "a2": _pad_rows(a2[l], LORA_PAD).astype(BF16),
        "g2": g2[l].astype(BF16),
    }


def kernel(x, c, t5_bias, ada_w, ada_b, norm_g, ffn_w_in, ffn_w_out, w_in, ckv_norm_g, w_uk, w_uv, rwkv_mu, rwkv_w0, rwkv_w2, rwkv_a0, rwkv_a2, rwkv_g2, rwkv_k_k, rwkv_k_a, rwkv_r_k, rwkv_ln_w, rwkv_ln_b, w_out, final_norm_g):
    b, s, d = x.shape
    depth = ada_w.shape[0]
    assert s % DSA_TILE == 0 and s % RWKV_TILE == 0 and d == A_WIDTH + R_WIDTH
    mod = _ada_mod(c, ada_w, ada_b).reshape(depth, b, N_SUB, 3, 1, d)
    bias = _bias_tiles(t5_bias)
    final_g = final_norm_g.reshape(1, d)
    w_in_z = _layout_w_in(w_in)
    w_out16 = w_out.astype(BF16)
    h = x
    for l in range(depth):
        shift = lambda i: mod[l, :, i, 0]
        scale = lambda i: mod[l, :, i, 1]
        gate = lambda i: mod[l, :, i, 2]
        g = lambda i: norm_g[l, i].reshape(1, d)
        h = _ffn(h, g(0), shift(0), scale(0), gate(0), ffn_w_in, ffn_w_out, l, 0, final_g, final_norm=False)
        za, zb = _proj(h, g(1), shift(1), scale(1), w_in_z, l)
        qt, qit, ckv, ckvt, ki, wit = _dsa_prep(za, zb, ckv_norm_g[l].reshape(1, KV_RANK))
        wuk = jnp.transpose(w_uk[l], (1, 0, 2)).astype(BF16)
        wuvt = jnp.transpose(w_uv[l], (1, 2, 0)).astype(BF16)
        o_a = _dsa(qt, qit, ckv, ckvt, ki, wit, wuk, wuvt, bias)
        o_r = _rwkv(zb, _rwkv_params(l, rwkv_mu, rwkv_w0, rwkv_w2, rwkv_a0, rwkv_a2, rwkv_g2, rwkv_k_k,
                                    rwkv_k_a, rwkv_r_k, rwkv_ln_w, rwkv_ln_b))
        h = _outproj(o_a, o_r, h, gate(1), w_out16, l)
        h = _ffn(h, g(2), shift(2), scale(2), gate(2), ffn_w_in, ffn_w_out, l, 1, final_g,
                 final_norm=(l == depth - 1))
    return h
```

```python
import functools
import math

import jax
import jax.numpy as jnp
from jax import lax
from jax.experimental import pallas as pl
from jax.experimental.pallas import tpu as pltpu

F32 = jnp.float32
BF16 = jnp.bfloat16
I32 = jnp.int32
I16 = jnp.int16

A_HEADS = 8
A_HEAD_DIM = 128
A_WIDTH = A_HEADS * A_HEAD_DIM
KV_RANK = 256
IDX_HEADS = 16
IDX_DIM = 64
TOPK_MAX = 256
REL_BUCKETS = 32
REL_MAX_EXACT = REL_BUCKETS // 2
REL_MAX_DIST = 128
R_HEAD_DIM = 64
R_WIDTH = 1024
DECAY_LORA = 96
AAA_LORA = 96
GATE_LORA = 256
GN_EPS = 64e-5
RMS_EPS = 1e-6
N_SUB = 3

LANES = 128
SUBLANES = 8
VMEM_LIMIT_BYTES = 56 * 1024 * 1024

LORA_PAD = 128
Z_Q = 0
Z_QI = Z_Q + A_WIDTH
Z_R = Z_QI + IDX_HEADS * IDX_DIM
Z_K = Z_R + R_WIDTH
Z_V = Z_K + R_WIDTH
Z_GL = Z_V + R_WIDTH
Z_CKV = Z_GL + GATE_LORA
Z_KIWI = Z_CKV + KV_RANK
Z_WL = Z_KIWI + LORA_PAD
Z_AL = Z_WL + LORA_PAD
Z_END = Z_AL + LORA_PAD
Z_WIDTH = 6144
Z_SPLIT = Z_R

FFN_TOKEN_TILE = 1024
FFN_HIDDEN_TILE = 512
PROLOGUE_ROWS = 256
ADALN_SLAB_ROWS = 16
ADALN_UNROLL = 8
DSA_TILE = 256
DSA_PREP_TILE = 512
RWKV_CHUNK = 64
RWKV_TILE = 256
PAIR = 2 * R_HEAD_DIM

I16_MIN = -(2 ** 15)
PACKED_ROWS = 2 * SUBLANES
SUM_ROWS = PACKED_ROWS
BF16_EXACT_INT = 256
COUNT_CHAINS = 4
KEY_NEG_INF = -2139095041
MASKED_LOGIT = -(2.0 ** 100)
LOG2E = math.log2(math.e)
FAR_DISTANCE = math.ceil(REL_MAX_EXACT * (REL_MAX_DIST / REL_MAX_EXACT)
                         ** ((REL_BUCKETS - 1 - REL_MAX_EXACT) / (REL_BUCKETS - REL_MAX_EXACT)))


def _dot(a, b):
    return jnp.dot(a, b, preferred_element_type=F32)


def _bdot(a, b):
    return lax.dot_general(a, b, (((2,), (1,)), ((0,), (0,))), preferred_element_type=F32)


def _bdot_nt(a, b):
    return lax.dot_general(a, b, (((2,), (2,)), ((0,), (0,))), preferred_element_type=F32)


def _rms(x, g, eps):
    ms = jnp.mean(x * x, axis=-1, keepdims=True)
    return x * lax.rsqrt(ms + eps) * g


def _divisor_tile(n, pref):
    if n <= pref:
        return n
    t = (pref // LANES) * LANES
    while t > LANES and n % t:
        t -= LANES
    assert n % t == 0, (n, pref)
    return t


def _params(*sem):
    return pltpu.CompilerParams(dimension_semantics=sem, vmem_limit_bytes=VMEM_LIMIT_BYTES)


def _ada_kernel(c_ref, w_ref, b_ref, o_ref):
    c = c_ref[...]
    ca = (c * jax.nn.sigmoid(c)).astype(BF16)
    o_ref[0] = _dot(ca, w_ref[0].astype(BF16)) + b_ref[0]


def _ada_mod(c, ada_w, ada_b):
    depth, d, n = ada_w.shape
    b = c.shape[0]
    bp = -(-b // SUBLANES) * SUBLANES
    cp = jnp.pad(c, ((0, bp - b), (0, 0)))
    tn = _divisor_tile(n, 1024)
    out = pl.pallas_call(
        _ada_kernel,
        out_shape=jax.ShapeDtypeStruct((depth, bp, n), F32),
        grid=(depth, n // tn),
        in_specs=[
            pl.BlockSpec((bp, d), lambda l, j: (0, 0)),
            pl.BlockSpec((1, d, tn), lambda l, j: (l, 0, j)),
            pl.BlockSpec((1, 1, tn), lambda l, j: (l, 0, j)),
        ],
        out_specs=pl.BlockSpec((1, bp, tn), lambda l, j: (l, 0, j)),
        compiler_params=_params("arbitrary", "arbitrary"),
        name="ada_mod",
    )(cp, ada_w, ada_b.reshape(depth, 1, n))
    return out[:, :b]


def _adaln_rows(h_ref, g_ref, scale_ref, shift_ref, hn_ref, base, copy_ref=None):
    rows = h_ref.shape[1]
    g = g_ref[...]
    mul = 1.0 + scale_ref[0]
    add = shift_ref[0]

    def slab(i, carry):
        src = pl.ds(pl.multiple_of(i * ADALN_SLAB_ROWS, ADALN_SLAB_ROWS), ADALN_SLAB_ROWS)
        dst = pl.ds(pl.multiple_of(base + i * ADALN_SLAB_ROWS, ADALN_SLAB_ROWS), ADALN_SLAB_ROWS)
        h = h_ref[0, src, :]
        hn_ref[dst, :] = (_rms(h, g, RMS_EPS) * mul + add).astype(BF16)
        if copy_ref is not None:
            copy_ref[0, dst, :] = h
        return carry

    lax.fori_loop(0, rows // ADALN_SLAB_ROWS, slab, 0, unroll=ADALN_UNROLL)


def _prologue_split(tm):
    rows = _divisor_tile(tm, PROLOGUE_ROWS)
    return rows, tm // rows


def _ffn_kernel(h_ref, g_ref, shift_ref, scale_ref, gate_ref, wg_ref, wu_ref, wo_ref, fg_ref,
                o_ref, hn_ref, *, final_norm, pro_steps):
    f = pl.program_id(2)

    @pl.when(f < pro_steps)
    def _():
        _adaln_rows(h_ref, g_ref, scale_ref, shift_ref, hn_ref, f * h_ref.shape[1], o_ref)

    @pl.when(f >= pro_steps)
    def _():
        hn = hn_ref[...]
        g = _dot(hn, wg_ref[...].astype(BF16))
        u = _dot(hn, wu_ref[...].astype(BF16))
        act = (g * jax.nn.sigmoid(g) * u).astype(BF16)
        o_ref[0] += (0.5 * gate_ref[0]) * _dot(act, wo_ref[...].astype(BF16))

    if final_norm:
        @pl.when(f == pl.num_programs(2) - 1)
        def _():
            o_ref[0] = _rms(o_ref[0], fg_ref[...], RMS_EPS)


def _ffn(h, g, shift, scale, gate, w_in, w_out, layer, which, final_g, *, final_norm):
    b, s, d = h.shape
    ff = w_out.shape[2]
    tm = _divisor_tile(s, FFN_TOKEN_TILE)
    tf = _divisor_tile(ff, FFN_HIDDEN_TILE)
    nf = ff // tf
    hr, pro = _prologue_split(tm)
    vec = pl.BlockSpec((1, 1, d), lambda bi, i, f: (bi, 0, 0))
    row = pl.BlockSpec((1, d), lambda bi, i, f: (0, 0))

    def wf(f):
        return jnp.maximum(f - pro, 0)

    return pl.pallas_call(
        functools.partial(_ffn_kernel, final_norm=final_norm, pro_steps=pro),
        out_shape=jax.ShapeDtypeStruct((b, s, d), F32),
        grid=(b, s // tm, pro + nf),
        in_specs=[
            pl.BlockSpec((1, hr, d), lambda bi, i, f: (bi, i * pro + jnp.minimum(f, pro - 1), 0)),
            row, vec, vec, vec,
            pl.BlockSpec((None, None, d, tf), lambda bi, i, f: (layer, which, 0, wf(f))),
            pl.BlockSpec((None, None, d, tf), lambda bi, i, f: (layer, which, 0, nf + wf(f))),
            pl.BlockSpec((None, None, tf, d), lambda bi, i, f: (layer, which, wf(f), 0)),
            row,
        ],
        out_specs=pl.BlockSpec((1, tm, d), lambda bi, i, f: (bi, i, 0)),
        scratch_shapes=[pltpu.VMEM((tm, d), BF16)],
        compiler_params=_params("parallel", "parallel", "arbitrary"),
        name="ffn",
    )(h, g, shift, scale, gate, w_in, w_in, w_out, final_g)


def _proj_kernel(h_ref, g_ref, shift_ref, scale_ref, w_ref, oa_ref, ob_ref, hn_ref, *, attn_tiles):
    n = pl.program_id(2)

    @pl.when(n == 0)
    def _():
        _adaln_rows(h_ref, g_ref, scale_ref, shift_ref, hn_ref, 0)

    z = _dot(hn_ref[...], w_ref[...])
    ob_ref[0] = z

    @pl.when(n < attn_tiles)
    def _():
        oa_ref[0] = z.astype(BF16)


def _proj(h, g, shift, scale, w, layer):
    b, s, d = h.shape
    p = w.shape[2]
    tm = _divisor_tile(s, 1024)
    tn = _divisor_tile(Z_SPLIT, 1024)
    assert Z_SPLIT % tn == 0 and (p - Z_SPLIT) % tn == 0
    na = Z_SPLIT // tn
    vec = pl.BlockSpec((1, 1, d), lambda bi, i, n: (bi, 0, 0))
    return pl.pallas_call(
        functools.partial(_proj_kernel, attn_tiles=na),
        out_shape=(jax.ShapeDtypeStruct((b, s, Z_SPLIT), BF16), jax.ShapeDtypeStruct((b, s, p - Z_SPLIT), F32)),
        grid=(b, s // tm, p // tn),
        in_specs=[
            pl.BlockSpec((1, tm, d), lambda bi, i, n: (bi, i, 0)),
            pl.BlockSpec((1, d), lambda bi, i, n: (0, 0)),
            vec, vec,
            pl.BlockSpec((None, d, tn), lambda bi, i, n: (layer, 0, n)),
        ],
        out_specs=(pl.BlockSpec((1, tm, tn), lambda bi, i, n: (bi, i, jnp.minimum(n, na - 1))),
                   pl.BlockSpec((1, tm, tn), lambda bi, i, n: (bi, i, jnp.maximum(n - na, 0)))),
        scratch_shapes=[pltpu.VMEM((tm, d), BF16)],
        compiler_params=_params("parallel", "parallel", "arbitrary"),
        name="proj",
    )(h, g, shift, scale, w)


def _outproj_kernel(oa_ref, or_ref, h_ref, gate_ref, wa_ref, wr_ref, o_ref):
    acc = _dot(oa_ref[0], wa_ref[...]) + _dot(or_ref[0], wr_ref[...])
    o_ref[0] = h_ref[0] + gate_ref[0] * acc


def _outproj(o_a, o_r, h, gate, w_out, layer):
    b, s, d = h.shape
    tm = _divisor_tile(s, 512)
    tn = d
    wa = o_a.shape[-1]
    wr = o_r.shape[-1]
    return pl.pallas_call(
        _outproj_kernel,
        out_shape=jax.ShapeDtypeStruct((b, s, d), F32),
        grid=(b, s // tm, d // tn),
        in_specs=[
            pl.BlockSpec((1, tm, wa), lambda bi, i, n: (bi, i, 0)),
            pl.BlockSpec((1, tm, wr), lambda bi, i, n: (bi, i, 0)),
            pl.BlockSpec((1, tm, tn), lambda bi, i, n: (bi, i, n)),
            pl.BlockSpec((1, 1, tn), lambda bi, i, n: (bi, 0, n)),
            pl.BlockSpec((None, wa, tn), lambda bi, i, n: (layer, 0, n)),
            pl.BlockSpec((None, wr, tn), lambda bi, i, n: (layer, wa // wr, n)),
        ],
        out_specs=pl.BlockSpec((1, tm, tn), lambda bi, i, n: (bi, i, n)),
        compiler_params=_params("parallel", "parallel", "arbitrary"),
        name="outproj",
    )(o_a, o_r, h, gate, w_out, w_out)


def _dsa_prep_kernel(q_ref, qi_ref, ckv_ref, kiwi_ref, g_ref, qt_ref, qit_ref, ckv_o, ckvt_o, ki_o, wit_o):
    qt_ref[0] = q_ref[0].T
    qit_ref[0] = qi_ref[0].T
    cn = _rms(ckv_ref[0], g_ref[...], RMS_EPS)
    ckv_o[0] = cn.astype(BF16)
    t = DSA_TILE
    ones = jnp.ones((SUM_ROWS, t), BF16)
    for j in range(cn.shape[0] // t):
        ckvt_o[0, j] = jnp.concatenate([cn[j * t:(j + 1) * t].T.astype(BF16), ones], axis=0)
    kw = kiwi_ref[0]
    ki_o[0] = kw[:, :IDX_DIM].astype(BF16)
    wit_o[0] = kw.T[IDX_DIM:IDX_DIM + IDX_HEADS, :] * (IDX_HEADS * IDX_DIM) ** -0.5


def _dsa_prep(za, zb, ckv_g):
    b, s, _ = za.shape
    t = DSA_TILE
    nt = s // t
    tp = _divisor_tile(s, DSA_PREP_TILE)
    assert tp % t == 0

    def seg(width, off):
        return pl.BlockSpec((1, tp, width), lambda bi, i: (bi, i, (off if off < Z_SPLIT else off - Z_SPLIT) // width))

    return pl.pallas_call(
        _dsa_prep_kernel,
        out_shape=(
            jax.ShapeDtypeStruct((b, A_WIDTH, s), BF16),
            jax.ShapeDtypeStruct((b, IDX_HEADS * IDX_DIM, s), BF16),
            jax.ShapeDtypeStruct((b, s, KV_RANK), BF16),
            jax.ShapeDtypeStruct((b, nt, KV_RANK + SUM_ROWS, t), BF16),
            jax.ShapeDtypeStruct((b, s, IDX_DIM), BF16),
            jax.ShapeDtypeStruct((b, IDX_HEADS, s), F32),
        ),
        grid=(b, s // tp),
        in_specs=[
            seg(A_WIDTH, Z_Q), seg(IDX_HEADS * IDX_DIM, Z_QI), seg(KV_RANK, Z_CKV), seg(LORA_PAD, Z_KIWI),
            pl.BlockSpec((1, KV_RANK), lambda bi, i: (0, 0)),
        ],
        out_specs=(
            pl.BlockSpec((1, A_WIDTH, tp), lambda bi, i: (bi, 0, i)),
            pl.BlockSpec((1, IDX_HEADS * IDX_DIM, tp), lambda bi, i: (bi, 0, i)),
            pl.BlockSpec((1, tp, KV_RANK), lambda bi, i: (bi, i, 0)),
            pl.BlockSpec((1, tp // t, KV_RANK + SUM_ROWS, t), lambda bi, i: (bi, i, 0, 0)),
            pl.BlockSpec((1, tp, IDX_DIM), lambda bi, i: (bi, i, 0)),
            pl.BlockSpec((1, IDX_HEADS, tp), lambda bi, i: (bi, 0, i)),
        ),
        compiler_params=_params("parallel", "parallel"),
        name="dsa_prep",
    )(za, za, zb, zb, ckv_g)


def _bias_kernel(t5_ref, o_ref):
    t = DSA_TILE
    j = lax.broadcasted_iota(I32, (t, t), 0)
    i = lax.broadcasted_iota(I32, (t, t), 1)
    for didx in range(2):
        n = jnp.maximum(didx * t + i - j, 0)
        nf = jnp.maximum(n, 1).astype(F32)
        large = REL_MAX_EXACT + (jnp.log(nf / REL_MAX_EXACT) / math.log(REL_MAX_DIST / REL_MAX_EXACT)
                                 * (REL_BUCKETS - REL_MAX_EXACT)).astype(I32)
        large = jnp.minimum(large, REL_BUCKETS - 1)
        bucket = jnp.where(n < REL_MAX_EXACT, n, large)
        for h in range(A_HEADS):
            val = jnp.zeros((t, t), F32)
            for k in range(REL_BUCKETS):
                val = jnp.where(bucket == k, t5_ref[k, h], val)
            o_ref[(1 - didx) * t:(2 - didx) * t, h * t:(h + 1) * t] = (val - t5_ref[REL_BUCKETS - 1, h]) * LOG2E


def _bias_tiles(t5_bias):
    t = DSA_TILE
    assert t + 1 >= FAR_DISTANCE
    return pl.pallas_call(
        _bias_kernel,
        out_shape=jax.ShapeDtypeStruct((2 * t, A_HEADS * t), F32),
        in_specs=[pl.BlockSpec(memory_space=pltpu.SMEM)],
        out_specs=pl.BlockSpec(memory_space=pltpu.VMEM),
        compiler_params=pltpu.CompilerParams(vmem_limit_bytes=VMEM_LIMIT_BYTES),
        name="t5_bias_tiles",
    )(t5_bias)


def _dsa_kernel(qt_ref, qit_ref, ckv_ref, ckvt_ref, ki_ref, wit_ref, wuk_ref, wuvt_ref, bias_ref,
                o_ref, keys_ref, khi_ref, klo_ref, acc_ref, ot_ref, *, topk, seq_len):
    t = DSA_TILE
    qb = pl.program_id(1)
    nk = qb + 1
    row = lax.broadcasted_iota(I32, (t, t), 0)
    col = lax.broadcasted_iota(I32, (t, t), 1)

    def idx_rows(start, rows, diagonal):
        sl = pl.ds(pl.multiple_of(start, rows), rows)
        kic = ki_ref[0, sl, :]
        acc = jnp.zeros((rows, t), F32)
        for h in range(IDX_HEADS):
            rel = _dot(kic, qit_ref[0, h * IDX_DIM:(h + 1) * IDX_DIM, :])
            acc = acc + jnp.maximum(rel, 0.0) * wit_ref[0, h:h + 1, :]
        acc = acc + 0.0
        bits = pltpu.bitcast(acc, I32)
        key = bits ^ ((bits >> 31) & 0x7FFFFFFF)
        if diagonal:
            key = jnp.where(row <= col, key, KEY_NEG_INF)
        keys_ref[sl, :] = key
        khi_ref[sl, :] = (key >> 16).astype(I16)
        klo_ref[sl, :] = ((key & 0xFFFF) + I16_MIN).astype(I16)

    def idx_group(chunks):
        def body(i, first_chunk):
            idx_rows(first_chunk * t, chunks * t, False)
            return first_chunk + chunks
        return body

    done = lax.fori_loop(0, qb // 8, idx_group(8), 0)
    done = lax.fori_loop(0, (qb // 4) % 2, idx_group(4), done)
    done = lax.fori_loop(0, (qb // 2) % 2, idx_group(2), done)
    lax.fori_loop(0, qb % 2, idx_group(1), done)
    idx_rows(qb * t, t, True)

    def count16(ref, cand):
        cand16 = cand.astype(I16)

        def body(kc, accs):
            x = ref[pl.ds(pl.multiple_of(kc * t, t), t), :]
            ones = jnp.where(x >= cand16, jnp.asarray(1, BF16), jnp.asarray(0, BF16))
            ones = ones.reshape(t // PACKED_ROWS, PACKED_ROWS, t)
            accs = list(accs)
            for i in range(t // PACKED_ROWS):
                accs[i % COUNT_CHAINS] = accs[i % COUNT_CHAINS] + ones[i]
            return tuple(accs)

        accs = (jnp.zeros((PACKED_ROWS, t), BF16),) * COUNT_CHAINS
        accs = lax.fori_loop(0, nk // 2, lambda i, a: body(2 * i + 1, body(2 * i, a)), accs)
        accs = lax.fori_loop(2 * (nk // 2), nk, body, accs)
        total = sum(a.astype(F32) for a in accs)
        return jnp.sum(total, axis=0, keepdims=True).astype(I32)

    def search16(ref, base):
        def accept(cand, cur):
            return jnp.where(base + count16(ref, cand) >= topk, cand, cur)

        v = accept(jnp.zeros((1, t), I32), jnp.full((1, t), I16_MIN, I32))
        return lax.fori_loop(0, 15, lambda i, v: accept(v | (jnp.int32(1) << (14 - i)), v), v)

    hi = search16(khi_ref, 0)
    above = count16(khi_ref, jnp.minimum(hi + 1, -I16_MIN - 1))
    hi16 = hi.astype(I16)

    def keep_low(kc, carry):
        sl = pl.ds(pl.multiple_of(kc * t, t), t)
        klo_ref[sl, :] = jnp.where(khi_ref[sl, :] == hi16, klo_ref[sl, :], jnp.asarray(I16_MIN, I16))
        return carry

    lax.fori_loop(0, nk, keep_low, 0)
    lo = search16(klo_ref, above)
    thr = jnp.maximum((hi << 16) | (lo - I16_MIN), KEY_NEG_INF + 1)

    def count(pred):
        def body(kc, acc):
            k = keys_ref[pl.ds(pl.multiple_of(kc * t, t), t), :]
            m = jnp.where(pred(k, kc), 1, 0).astype(I32)
            return acc + jnp.sum(m.reshape(t // SUBLANES, SUBLANES, t), axis=0)

        acc = lax.fori_loop(0, nk, body, jnp.zeros((SUBLANES, t), I32))
        return jnp.sum(acc, axis=0, keepdims=True)

    @pl.when(jnp.max(count(lambda k, kc: k >= thr)) > topk)
    def _():
        need = topk - count(lambda k, kc: k > thr)
        nbits = (seq_len - 1).bit_length()

        def pos_bit(i, y):
            cand = y | (jnp.int32(1) << (nbits - 1 - i))
            before = count(lambda k, kc: (k == thr) & ((kc * t + row) < cand))
            return jnp.where(before < need, cand, y)

        last_kept = lax.fori_loop(0, nbits, pos_bit, jnp.zeros((1, t), I32))

        def demote(kc, carry):
            sl = pl.ds(pl.multiple_of(kc * t, t), t)
            k = keys_ref[sl, :]
            keys_ref[sl, :] = jnp.where((k == thr) & ((kc * t + row) > last_kept), k - 1, k)
            return carry

        lax.fori_loop(0, nk, demote, 0)

    scale = A_HEAD_DIM ** -0.5 * LOG2E

    hw = A_HEADS * t
    qlat = jnp.concatenate(
        [(_dot(wuk_ref[h], qt_ref[0, h * A_HEAD_DIM:(h + 1) * A_HEAD_DIM, :]) * scale).astype(BF16)
         for h in range(A_HEADS)], axis=1)
    acc_ref[...] = jnp.zeros_like(acc_ref)

    def att_chunks(kc, m, *, chunks, bias_row=None):
        rows = chunks * t
        sl = pl.ds(pl.multiple_of(kc * t, t), rows)
        raw = _dot(ckv_ref[0, sl, :], qlat)
        sel = keys_ref[sl, :] >= thr
        parts = []
        for h in range(A_HEADS):
            lg_h = raw[:, h * t:(h + 1) * t]
            if bias_row is not None:
                lg_h = lg_h + bias_ref[bias_row:bias_row + rows, h * t:(h + 1) * t]
            parts.append(jnp.where(sel, lg_h, MASKED_LOGIT).astype(BF16))
        lg = jnp.concatenate(parts, axis=1)
        m_new = jnp.maximum(m, jnp.max(lg, axis=0, keepdims=True).astype(F32))
        p = jnp.exp2(lg - m_new.astype(BF16))
        acc = acc_ref[...] * jnp.exp2(m - m_new)
        for j in range(chunks):
            acc = acc + _dot(ckvt_ref[0, kc + j], p[j * t:(j + 1) * t])
        acc_ref[...] = acc
        return m_new

    n_far = jnp.maximum(qb - 1, 0)
    has_prev = jnp.minimum(qb, 1)
    m = jnp.full((1, hw), MASKED_LOGIT, F32)
    def far_group(chunks):
        def body(i, carry):
            first_chunk, m = carry
            return first_chunk + chunks, att_chunks(first_chunk, m, chunks=chunks)
        return body

    carry = lax.fori_loop(0, n_far // 4, far_group(4), (0, m))
    carry = lax.fori_loop(0, (n_far // 2) % 2, far_group(2), carry)
    _, m = lax.fori_loop(0, n_far % 2, far_group(1), carry)
    m = lax.fori_loop(0, has_prev, lambda i, m: att_chunks(qb - 1, m, chunks=2, bias_row=0), m)
    lax.fori_loop(0, 1 - has_prev, lambda i, m: att_chunks(qb, m, chunks=1, bias_row=t), m)
    olat = (acc_ref[:KV_RANK, :] / acc_ref[KV_RANK:KV_RANK + 1, :]).astype(BF16)
    for h in range(A_HEADS):
        ot_ref[h * A_HEAD_DIM:(h + 1) * A_HEAD_DIM, :] = _dot(wuvt_ref[h], olat[:, h * t:(h + 1) * t])
    o_ref[0] = ot_ref[...].T.astype(BF16)


def _dsa(qt, qit, ckv, ckvt, ki, wit, wuk, wuvt, bias):
    b, _, s = qt.shape
    t = DSA_TILE
    nt = s // t
    topk = min(TOPK_MAX, s // 4)
    assert s // PACKED_ROWS <= BF16_EXACT_INT
    return pl.pallas_call(
        functools.partial(_dsa_kernel, topk=topk, seq_len=s),
        out_shape=jax.ShapeDtypeStruct((b, s, A_WIDTH), BF16),
        grid=(b, nt),
        in_specs=[
            pl.BlockSpec((1, A_WIDTH, t), lambda bi, i: (bi, 0, i)),
            pl.BlockSpec((1, IDX_HEADS * IDX_DIM, t), lambda bi, i: (bi, 0, i)),
            pl.BlockSpec((1, s, KV_RANK), lambda bi, i: (bi, 0, 0)),
            pl.BlockSpec((1, nt, KV_RANK + SUM_ROWS, t), lambda bi, i: (bi, 0, 0, 0)),
            pl.BlockSpec((1, s, IDX_DIM), lambda bi, i: (bi, 0, 0)),
            pl.BlockSpec((1, IDX_HEADS, t), lambda bi, i: (bi, 0, i)),
            pl.BlockSpec((A_HEADS, KV_RANK, A_HEAD_DIM), lambda bi, i: (0, 0, 0)),
            pl.BlockSpec((A_HEADS, A_HEAD_DIM, KV_RANK), lambda bi, i: (0, 0, 0)),
            pl.BlockSpec((2 * t, A_HEADS * t), lambda bi, i: (0, 0)),
        ],
        out_specs=pl.BlockSpec((1, t, A_WIDTH), lambda bi, i: (bi, i, 0)),
        scratch_shapes=[pltpu.VMEM((s, t), I32), pltpu.VMEM((s, t), I16), pltpu.VMEM((s, t), I16),
                        pltpu.VMEM((KV_RANK + SUM_ROWS, A_HEADS * t), F32), pltpu.VMEM((A_WIDTH, t), F32)],
        compiler_params=_params("parallel", "arbitrary"),
        name="dsa",
    )(qt, qit, ckv, ckvt, ki, wit, wuk, wuvt, bias)


def _split3(x):
    hi = x.astype(BF16)
    r1 = x - hi.astype(F32)
    mid = r1.astype(BF16)
    lo = (r1 - mid.astype(F32)).astype(BF16)
    return hi, mid, lo


def _rwkv_kernel(r_ref, k_ref, v_ref, gl_ref, wl_ref, al_ref,
                 mur_ref, muk_ref, muv_ref, mug_ref, muw_ref, mua_ref,
                 w0_ref, a0_ref, kk_ref, ka_ref, rk_ref, lnw_ref, lnb_ref,
                 w2_ref, a2_ref, g2_ref,
                 o_ref,
                 pr_ref, pk_ref, pv_ref, pg_ref, pw_ref, pa_ref, state_ref):
    tl = RWKV_TILE
    c = RWKV_CHUNK
    nc = tl // c
    npair = R_WIDTH // PAIR

    @pl.when(pl.program_id(1) == 0)
    def _():
        state_ref[...] = jnp.zeros_like(state_ref)
        for ref in (pr_ref, pk_ref, pv_ref, pg_ref, pw_ref, pa_ref):
            ref[...] = jnp.zeros_like(ref)

    def shift_mix(x_ref, prev_ref, mu_ref):
        x = x_ref[0]
        first = lax.broadcasted_iota(I32, x.shape, 0) == 0
        xprev = jnp.where(first, prev_ref[...], pltpu.roll(x, 1, axis=0))
        prev_ref[...] = x[tl - 1:tl, :]
        return x + (xprev - x) * mu_ref[...]

    r = shift_mix(r_ref, pr_ref, mur_ref)
    k = shift_mix(k_ref, pk_ref, muk_ref)
    v = shift_mix(v_ref, pv_ref, muv_ref)
    gl = shift_mix(gl_ref, pg_ref, mug_ref)
    wl = shift_mix(wl_ref, pw_ref, muw_ref)
    al = shift_mix(al_ref, pa_ref, mua_ref)

    w_lin = w0_ref[...] + _dot(jnp.tanh(wl).astype(BF16), w2_ref[...])
    nx = -w_lin
    softplus = jnp.maximum(nx, 0.0) + jnp.log(1.0 + jnp.exp(-jnp.abs(nx)))
    ld = -jnp.exp(-softplus - 0.5)
    a = jax.nn.sigmoid(a0_ref[...] + _dot(al.astype(BF16), a2_ref[...]))
    gate = _dot(jax.nn.sigmoid(gl).astype(BF16), g2_ref[...])

    li = lax.broadcasted_iota(I32, (LANES, LANES), 0) // R_HEAD_DIM
    lj = lax.broadcasted_iota(I32, (LANES, LANES), 1) // R_HEAD_DIM
    head_ones = jnp.where(li == lj, 1.0, 0.0).astype(BF16)

    def head_sum(x):
        parts = []
        for j in range(R_WIDTH // LANES):
            hi, mid, lo = _split3(x[:, j * LANES:(j + 1) * LANES])
            parts.append(_dot(hi, head_ones) + _dot(mid, head_ones) + _dot(lo, head_ones))
        return jnp.concatenate(parts, axis=-1)

    kk = k * kk_ref[...]
    kk = kk / jnp.maximum(jnp.sqrt(head_sum(kk * kk)), 1e-12)
    k2 = k * (1.0 + (a - 1.0) * ka_ref[...])
    bb = kk * a

    ti = lax.broadcasted_iota(I32, (tl, tl), 0)
    tj = lax.broadcasted_iota(I32, (tl, tl), 1)
    tri = jnp.where((ti // c == tj // c) & (tj <= ti), 1.0, 0.0).astype(BF16)
    hi, mid, lo = _split3(ld)
    cum = _dot(tri, hi) + _dot(tri, mid) + _dot(tri, lo)
    cum_end = jnp.broadcast_to(cum.reshape(nc, c, R_WIDTH)[:, c - 1:c, :], (nc, c, R_WIDTH)).reshape(tl, R_WIDTH)
    e_in = jnp.exp(cum)
    e_neg = jnp.exp(-cum)
    e_out = jnp.exp(cum_end - cum)

    nb = nc * npair

    def to_pairs(x):
        x3 = x.reshape(nc, c, R_WIDTH)
        xs = jnp.stack([x3[:, :, p * PAIR:(p + 1) * PAIR] for p in range(npair)], axis=1)
        return xs.reshape(nb, c, PAIR)

    lane_head = lax.broadcasted_iota(I32, (2 * c, PAIR), 1) // R_HEAD_DIM
    row_head = lax.broadcasted_iota(I32, (2 * c, PAIR), 0) // c
    same = lane_head == row_head
    rt = lax.broadcasted_iota(I32, (2 * c, PAIR), 0) % c
    ct = lax.broadcasted_iota(I32, (2 * c, PAIR), 1) % c
    strict = same & (ct < rt)
    incl = same & (ct <= rt)
    eye = jnp.where(same & (ct == rt), 1.0, 0.0).astype(F32)

    def bd(x):
        return jnp.where(same, jnp.concatenate([x, x], axis=1), 0.0)

    lane0 = lax.broadcasted_iota(I32, (c, PAIR), 1) < R_HEAD_DIM

    abar = bd(to_pairs(-kk * jnp.exp(cum - ld))).astype(BF16)
    rbar = bd(to_pairs(r * e_in)).astype(BF16)
    bt = bd(to_pairs(bb * e_neg)).astype(BF16)
    kt = bd(to_pairs(k2 * e_neg)).astype(BF16)
    bk = jnp.concatenate([bd(to_pairs(bb * e_out)), bd(to_pairs(k2 * e_out))], axis=1).astype(BF16)
    v_pl = to_pairs(v)
    vbd = bd(v_pl)
    decay_end = to_pairs(jnp.exp(cum_end))[:, 0:1, :]
    aa = _bdot_nt(jnp.concatenate([abar, rbar], axis=1), jnp.concatenate([bt, kt], axis=1))
    a_ab = jnp.where(strict, aa[:, :2 * c, :PAIR], 0.0)
    a_ak = jnp.where(strict, aa[:, :2 * c, PAIR:], 0.0).astype(BF16)
    a_rb = jnp.where(incl, aa[:, 2 * c:, :PAIR], 0.0).astype(BF16)
    a_rk = jnp.where(incl, aa[:, 2 * c:, PAIR:], 0.0).astype(BF16)
    pw = a_ab.astype(BF16)
    tinv = eye + a_ab
    for _ in range(int(math.log2(c)) - 1):
        pw = _bdot(pw, pw).astype(BF16)
        tinv = tinv + _bdot(tinv.astype(BF16), pw)
    akv = _bdot(a_ak, vbd.astype(BF16))
    wu = _bdot(tinv.astype(BF16), jnp.concatenate([abar, akv.astype(BF16)], axis=2))
    w16 = wu[:, :, :PAIR].astype(BF16)
    u_bd = wu[:, :, PAIR:]

    s_bd = state_ref[...]
    e_parts, rs_parts = [], []
    for ci in range(nc):
        sl = slice(ci * npair, (ci + 1) * npair)
        wr = _bdot_nt(jnp.concatenate([w16[sl], rbar[sl]], axis=1), s_bd.astype(BF16))
        e_bd = wr[:, :2 * c] + u_bd[sl]
        e_parts.append(e_bd)
        rs_parts.append(wr[:, 2 * c:])
        ev = jnp.concatenate([e_bd, vbd[sl]], axis=1)
        ds = _bdot(jnp.swapaxes(ev, 1, 2).astype(BF16), bk[sl])
        s_bd = s_bd * decay_end[sl] + jnp.where(same, ds, 0.0)
    state_ref[...] = s_bd

    ev_all = jnp.concatenate([jnp.concatenate(e_parts, axis=0), vbd], axis=1).astype(BF16)
    y_bd = jnp.concatenate(rs_parts, axis=0) + _bdot(jnp.concatenate([a_rb, a_rk], axis=2), ev_all)
    mean = jnp.sum(y_bd, axis=-1, keepdims=True) * (1.0 / R_HEAD_DIM)
    dev = jnp.where(same, y_bd - mean, 0.0)
    var = jnp.sum(dev * dev, axis=-1, keepdims=True) * (1.0 / R_HEAD_DIM)
    yn = dev * lax.rsqrt(var + GN_EPS)
    yn = yn[:, :c] + yn[:, c:]
    rkv = to_pairs(r * k2 * rk_ref[...])
    s0 = jnp.sum(jnp.where(lane0, rkv, 0.0), axis=-1, keepdims=True)
    s1 = jnp.sum(jnp.where(lane0, 0.0, rkv), axis=-1, keepdims=True)
    bonus = jnp.where(lane0, s0, s1) * v_pl
    def per_pair(row_ref):
        rows = [row_ref[:, p * PAIR:(p + 1) * PAIR] for p in range(npair)]
        return jnp.stack(rows * nc, axis=0)

    out = ((yn * per_pair(lnw_ref) + per_pair(lnb_ref) + bonus) * to_pairs(gate)).astype(BF16)
    for ci in range(nc):
        for p in range(npair):
            o_ref[0, ci * c:(ci + 1) * c, p * PAIR:(p + 1) * PAIR] = out[ci * npair + p]


def _rwkv(zb, prm):
    b, s, _ = zb.shape
    tl = RWKV_TILE

    def seg(width, off):
        return pl.BlockSpec((1, tl, width), lambda bi, i: (bi, i, (off - Z_SPLIT) // width))

    def full(arr):
        return pl.BlockSpec(arr.shape, lambda bi, i: (0,) * arr.ndim)

    names = ("mu_r", "mu_k", "mu_v", "mu_g", "mu_w", "mu_a", "w0", "a0", "k_k", "k_a", "r_k", "ln_w", "ln_b",
             "w2", "a2", "g2")
    consts = [prm[n] for n in names]
    return pl.pallas_call(
        _rwkv_kernel,
        out_shape=jax.ShapeDtypeStruct((b, s, R_WIDTH), BF16),
        grid=(b, s // tl),
        in_specs=[seg(R_WIDTH, Z_R), seg(R_WIDTH, Z_K), seg(R_WIDTH, Z_V), seg(GATE_LORA, Z_GL),
                  seg(LORA_PAD, Z_WL), seg(LORA_PAD, Z_AL)] + [full(x) for x in consts],
        out_specs=pl.BlockSpec((1, tl, R_WIDTH), lambda bi, i: (bi, i, 0)),
        scratch_shapes=[
            pltpu.VMEM((1, R_WIDTH), F32), pltpu.VMEM((1, R_WIDTH), F32), pltpu.VMEM((1, R_WIDTH), F32),
            pltpu.VMEM((1, GATE_LORA), F32), pltpu.VMEM((1, LORA_PAD), F32), pltpu.VMEM((1, LORA_PAD), F32),
            pltpu.VMEM((R_WIDTH // PAIR, PAIR, PAIR), F32),
        ],
        compiler_params=_params("parallel", "arbitrary"),
        name="rwkv7",
    )(zb, zb, zb, zb, zb, zb, *consts)


def _pad_cols(w, width):
    return jnp.pad(w, ((0, 0), (0, width - w.shape[1])))


def _w_in_layout_kernel(wt_ref, o_ref):
    wt = wt_ref[0]
    cols = wt.shape[1]
    offs = [0]
    for n in (A_WIDTH, KV_RANK, IDX_HEADS * IDX_DIM, IDX_DIM + IDX_HEADS,
              3 * R_WIDTH, DECAY_LORA, AAA_LORA, GATE_LORA):
        offs.append(offs[-1] + n)
    q, ckv, qi, kiwi, rkv, wl, al, gl = (wt[offs[i]:offs[i + 1], :] for i in range(8))

    def padded(x, height):
        return jnp.concatenate([x, jnp.zeros((height - x.shape[0], cols), x.dtype)], axis=0)

    rows = [q, qi, rkv, gl, ckv, padded(kiwi, LORA_PAD), padded(wl, LORA_PAD), padded(al, LORA_PAD),
            jnp.zeros((Z_WIDTH - Z_END, cols), wt.dtype)]
    o_ref[0] = jnp.concatenate(rows, axis=0).T.astype(BF16)


def _layout_w_in(w_in):
    depth, d, p = w_in.shape
    tc = _divisor_tile(d, 256)
    return pl.pallas_call(
        _w_in_layout_kernel,
        out_shape=jax.ShapeDtypeStruct((depth, d, Z_WIDTH), BF16),
        grid=(depth, d // tc),
        in_specs=[pl.BlockSpec((1, p, tc), lambda l, i: (l, 0, i))],
        out_specs=pl.BlockSpec((1, tc, Z_WIDTH), lambda l, i: (l, i, 0)),
        compiler_params=_params("parallel", "parallel"),
        name="w_in_layout",
    )(jnp.swapaxes(w_in, 1, 2))


def _pad_rows(w, rows):
    return jnp.pad(w, ((0, rows - w.shape[0]), (0, 0)))


def _rwkv_params(l, mu, w0, w2, a0, a2, g2, k_k, k_a, r_k, ln_w, ln_b):
    m = mu[l]
    o = [0, R_WIDTH, 2 * R_WIDTH, 3 * R_WIDTH, 3 * R_WIDTH + DECAY_LORA, 3 * R_WIDTH + DECAY_LORA + AAA_LORA]
    row = lambda x: x.reshape(1, -1)
    return {
        "mu_r": row(m[o[0]:o[1]]), "mu_k": row(m[o[1]:o[2]]), "mu_v": row(m[o[2]:o[3]]),
        "mu_w": _pad_cols(row(m[o[3]:o[4]]), LORA_PAD), "mu_a": _pad_cols(row(m[o[4]:o[5]]), LORA_PAD),
        "mu_g": row(m[o[5]:]),
        "w0": row(w0[l]), "a0": row(a0[l]), "k_k": row(k_k[l]), "k_a": row(k_a[l]), "r_k": row(r_k[l]),
        "ln_w": row(ln_w[l]), "ln_b": row(ln_b[l]),
        "w2": _pad_rows(w2[l], LORA_PAD).astype(BF16), "a2": _pad_rows(a2[l], LORA_PAD).astype(BF16),
        "g2": g2[l].astype(BF16),
    }


def kernel(x, c, t5_bias, ada_w, ada_b, norm_g, ffn_w_in, ffn_w_out, w_in, ckv_norm_g, w_uk, w_uv, rwkv_mu, rwkv_w0, rwkv_w2, rwkv_a0, rwkv_a2, rwkv_g2, rwkv_k_k, rwkv_k_a, rwkv_r_k, rwkv_ln_w, rwkv_ln_b, w_out, final_norm_g):
    b, s, d = x.shape
    depth = ada_w.shape[0]
    assert s % DSA_TILE == 0 and s % RWKV_TILE == 0 and d == A_WIDTH + R_WIDTH
    mod = _ada_mod(c, ada_w, ada_b).reshape(depth, b, N_SUB, 3, 1, d)
    bias = _bias_tiles(t5_bias)
    final_g = final_norm_g.reshape(1, d)
    w_in_z = _layout_w_in(w_in)
    w_out16 = w_out.astype(BF16)
    h = x
    for l in range(depth):
        shift = lambda i: mod[l, :, i, 0]
        scale = lambda i: mod[l, :, i, 1]
        gate = lambda i: mod[l, :, i, 2]
        g = lambda i: norm_g[l, i].reshape(1, d)
        h = _ffn(h, g(0), shift(0), scale(0), gate(0), ffn_w_in, ffn_w_out, l, 0, final_g, final_norm=False)
        za, zb = _proj(h, g(1), shift(1), scale(1), w_in_z, l)
        qt, qit, ckv, ckvt, ki, wit = _dsa_prep(za, zb, ckv_norm_g[l].reshape(1, KV_RANK))
        wuk = jnp.transpose(w_uk[l], (1, 0, 2)).astype(BF16)
        wuvt = jnp.transpose(w_uv[l], (1, 2, 0)).astype(BF16)
        o_a = _dsa(qt, qit, ckv, ckvt, ki, wit, wuk, wuvt, bias)
        o_r = _rwkv(zb, _rwkv_params(l, rwkv_mu, rwkv_w0, rwkv_w2, rwkv_a0, rwkv_a2, rwkv_g2, rwkv_k_k,
                                    rwkv_k_a, rwkv_r_k, rwkv_ln_w, rwkv_ln_b))
        h = _outproj(o_a, o_r, h, gate(1), w_out16, l)
        h = _ffn(h, g(2), shift(2), scale(2), gate(2), ffn_w_in, ffn_w_out, l, 1, final_g,
                 final_norm=(l == depth - 1))
    return h
```

```python
import functools
import math

import jax
import jax.numpy as jnp
from jax import lax
from jax.experimental import pallas as pl
from jax.experimental.pallas import tpu as pltpu

F32 = jnp.float32
BF16 = jnp.bfloat16
I32 = jnp.int32
I16 = jnp.int16

A_HEADS = 8
A_HEAD_DIM = 128
A_WIDTH = A_HEADS * A_HEAD_DIM
KV_RANK = 256
IDX_HEADS = 16
IDX_DIM = 64
TOPK_MAX = 256
REL_BUCKETS = 32
REL_MAX_EXACT = REL_BUCKETS // 2
REL_MAX_DIST = 128
R_HEAD_DIM = 64
R_WIDTH = 1024
DECAY_LORA = 96
AAA_LORA = 96
GATE_LORA = 256
GN_EPS = 64e-5
RMS_EPS = 1e-6
N_SUB = 3

LANES = 128
SUBLANES = 8
VMEM_LIMIT_BYTES = 56 * 1024 * 1024

LORA_PAD = 128
Z_Q = 0
Z_QI = Z_Q + A_WIDTH
Z_R = Z_QI + IDX_HEADS * IDX_DIM
Z_K = Z_R + R_WIDTH
Z_V = Z_K + R_WIDTH
Z_GL = Z_V + R_WIDTH
Z_CKV = Z_GL + GATE_LORA
Z_KIWI = Z_CKV + KV_RANK
Z_WL = Z_KIWI + LORA_PAD
Z_AL = Z_WL + LORA_PAD
Z_END = Z_AL + LORA_PAD
Z_WIDTH = 6144
Z_SPLIT = Z_R

FFN_TOKEN_TILE = 1024
FFN_HIDDEN_TILE = 512
PROLOGUE_ROWS = 256
ADALN_SLAB_ROWS = 16
ADALN_UNROLL = 8
DSA_TILE = 256
DSA_PREP_TILE = 512
RWKV_CHUNK = 64
RWKV_TILE = 256
PAIR = 2 * R_HEAD_DIM

I16_MIN = -(2 ** 15)
PACKED_ROWS = 2 * SUBLANES
SUM_ROWS = PACKED_ROWS
BF16_EXACT_INT = 256
COUNT_CHAINS = 4
KEY_NEG_INF = -2139095041
MASKED_LOGIT = -(2.0 ** 100)
LOG2E = math.log2(math.e)
FAR_DISTANCE = math.ceil(REL_MAX_EXACT * (REL_MAX_DIST / REL_MAX_EXACT)
                         ** ((REL_BUCKETS - 1 - REL_MAX_EXACT) / (REL_BUCKETS - REL_MAX_EXACT)))


def _dot(a, b):
    return jnp.dot(a, b, preferred_element_type=F32)


def _bdot(a, b):
    return lax.dot_general(a, b, (((2,), (1,)), ((0,), (0,))), preferred_element_type=F32)


def _bdot_nt(a, b):
    return lax.dot_general(a, b, (((2,), (2,)), ((0,), (0,))), preferred_element_type=F32)


def _rms(x, g, eps):
    ms = jnp.mean(x * x, axis=-1, keepdims=True)
    return x * lax.rsqrt(ms + eps) * g


def _divisor_tile(n, pref):
    if n <= pref:
        return n
    t = (pref // LANES) * LANES
    while t > LANES and n % t:
        t -= LANES
    assert n % t == 0, (n, pref)
    return t


def _params(*sem):
    return pltpu.CompilerParams(dimension_semantics=sem, vmem_limit_bytes=VMEM_LIMIT_BYTES)


def _ada_kernel(c_ref, w_ref, b_ref, o_ref):
    c = c_ref[...]
    ca = (c * jax.nn.sigmoid(c)).astype(BF16)
    o_ref[0] = _dot(ca, w_ref[0].astype(BF16)) + b_ref[0]


def _ada_mod(c, ada_w, ada_b):
    depth, d, n = ada_w.shape
    b = c.shape[0]
    bp = -(-b // SUBLANES) * SUBLANES
    cp = jnp.pad(c, ((0, bp - b), (0, 0)))
    tn = _divisor_tile(n, 1024)
    out = pl.pallas_call(
        _ada_kernel,
        out_shape=jax.ShapeDtypeStruct((depth, bp, n), F32),
        grid=(depth, n // tn),
        in_specs=[
            pl.BlockSpec((bp, d), lambda l, j: (0, 0)),
            pl.BlockSpec((1, d, tn), lambda l, j: (l, 0, j)),
            pl.BlockSpec((1, 1, tn), lambda l, j: (l, 0, j)),
        ],
        out_specs=pl.BlockSpec((1, bp, tn), lambda l, j: (l, 0, j)),
        compiler_params=_params("arbitrary", "arbitrary"),
        name="ada_mod",
    )(cp, ada_w, ada_b.reshape(depth, 1, n))
    return out[:, :b]


def _adaln_rows(h_ref, g_ref, scale_ref, shift_ref, hn_ref, base, copy_ref=None):
    rows = h_ref.shape[1]
    g = g_ref[...]
    mul = 1.0 + scale_ref[0]
    add = shift_ref[0]

    def slab(i, carry):
        src = pl.ds(pl.multiple_of(i * ADALN_SLAB_ROWS, ADALN_SLAB_ROWS), ADALN_SLAB_ROWS)
        dst = pl.ds(pl.multiple_of(base + i * ADALN_SLAB_ROWS, ADALN_SLAB_ROWS), ADALN_SLAB_ROWS)
        h = h_ref[0, src, :]
        hn_ref[dst, :] = (_rms(h, g, RMS_EPS) * mul + add).astype(BF16)
        if copy_ref is not None:
            copy_ref[0, dst, :] = h
        return carry

    lax.fori_loop(0, rows // ADALN_SLAB_ROWS, slab, 0, unroll=ADALN_UNROLL)


def _prologue_split(tm):
    rows = _divisor_tile(tm, PROLOGUE_ROWS)
    return rows, tm // rows


def _ffn_kernel(h_ref, g_ref, shift_ref, scale_ref, gate_ref, wg_ref, wu_ref, wo_ref, fg_ref,
                o_ref, hn_ref, *, final_norm, pro_steps):
    f = pl.program_id(2)

    @pl.when(f < pro_steps)
    def _():
        _adaln_rows(h_ref, g_ref, scale_ref, shift_ref, hn_ref, f * h_ref.shape[1], o_ref)

    @pl.when(f >= pro_steps)
    def _():
        hn = hn_ref[...]
        g = _dot(hn, wg_ref[...].astype(BF16))
        u = _dot(hn, wu_ref[...].astype(BF16))
        act = (g * jax.nn.sigmoid(g) * u).astype(BF16)
        o_ref[0] += (0.5 * gate_ref[0]) * _dot(act, wo_ref[...].astype(BF16))

    if final_norm:
        @pl.when(f == pl.num_programs(2) - 1)
        def _():
            o_ref[0] = _rms(o_ref[0], fg_ref[...], RMS_EPS)


def _ffn(h, g, shift, scale, gate, w_in, w_out, layer, which, final_g, *, final_norm):
    b, s, d = h.shape
    ff = w_out.shape[2]
    tm = _divisor_tile(s, FFN_TOKEN_TILE)
    tf = _divisor_tile(ff, FFN_HIDDEN_TILE)
    nf = ff // tf
    hr, pro = _prologue_split(tm)
    vec = pl.BlockSpec((1, 1, d), lambda bi, i, f: (bi, 0, 0))
    row = pl.BlockSpec((1, d), lambda bi, i, f: (0, 0))

    def wf(f):
        return jnp.maximum(f - pro, 0)

    return pl.pallas_call(
        functools.partial(_ffn_kernel, final_norm=final_norm, pro_steps=pro),
        out_shape=jax.ShapeDtypeStruct((b, s, d), F32),
        grid=(b, s // tm, pro + nf),
        in_specs=[
            pl.BlockSpec((1, hr, d), lambda bi, i, f: (bi, i * pro + jnp.minimum(f, pro - 1), 0)),
            row, vec, vec, vec,
            pl.BlockSpec((None, None, d, tf), lambda bi, i, f: (layer, which, 0, wf(f))),
            pl.BlockSpec((None, None, d, tf), lambda bi, i, f: (layer, which, 0, nf + wf(f))),
            pl.BlockSpec((None, None, tf, d), lambda bi, i, f: (layer, which, wf(f), 0)),
            row,
        ],
        out_specs=pl.BlockSpec((1, tm, d), lambda bi, i, f: (bi, i, 0)),
        scratch_shapes=[pltpu.VMEM((tm, d), BF16)],
        compiler_params=_params("parallel", "parallel", "arbitrary"),
        name="ffn",
    )(h, g, shift, scale, gate, w_in, w_in, w_out, final_g)


def _proj_kernel(h_ref, g_ref, shift_ref, scale_ref, w_ref, oa_ref, ob_ref, hn_ref, *, attn_tiles):
    n = pl.program_id(2)

    @pl.when(n == 0)
    def _():
        _adaln_rows(h_ref, g_ref, scale_ref, shift_ref, hn_ref, 0)

    z = _dot(hn_ref[...], w_ref[...])
    ob_ref[0] = z

    @pl.when(n < attn_tiles)
    def _():
        oa_ref[0] = z.astype(BF16)


def _proj(h, g, shift, scale, w, layer):
    b, s, d = h.shape
    p = w.shape[2]
    tm = _divisor_tile(s, 1024)
    tn = _divisor_tile(Z_SPLIT, 1024)
    assert Z_SPLIT % tn == 0 and (p - Z_SPLIT) % tn == 0
    na = Z_SPLIT // tn
    vec = pl.BlockSpec((1, 1, d), lambda bi, i, n: (bi, 0, 0))
    return pl.pallas_call(
        functools.partial(_proj_kernel, attn_tiles=na),
        out_shape=(jax.ShapeDtypeStruct((b, s, Z_SPLIT), BF16), jax.ShapeDtypeStruct((b, s, p - Z_SPLIT), F32)),
        grid=(b, s // tm, p // tn),
        in_specs=[
            pl.BlockSpec((1, tm, d), lambda bi, i, n: (bi, i, 0)),
            pl.BlockSpec((1, d), lambda bi, i, n: (0, 0)),
            vec, vec,
            pl.BlockSpec((None, d, tn), lambda bi, i, n: (layer, 0, n)),
        ],
        out_specs=(pl.BlockSpec((1, tm, tn), lambda bi, i, n: (bi, i, jnp.minimum(n, na - 1))),
                   pl.BlockSpec((1, tm, tn), lambda bi, i, n: (bi, i, jnp.maximum(n - na, 0)))),
        scratch_shapes=[pltpu.VMEM((tm, d), BF16)],
        compiler_params=_params("parallel", "parallel", "arbitrary"),
        name="proj",
    )(h, g, shift, scale, w)


def _outproj_kernel(oa_ref, or_ref, h_ref, gate_ref, wa_ref, wr_ref, o_ref):
    acc = _dot(oa_ref[0], wa_ref[...]) + _dot(or_ref[0], wr_ref[...])
    o_ref[0] = h_ref[0] + gate_ref[0] * acc


def _outproj(o_a, o_r, h, gate, w_out, layer):
    b, s, d = h.shape
    tm = _divisor_tile(s, 512)
    tn = d
    wa = o_a.shape[-1]
    wr = o_r.shape[-1]
    return pl.pallas_call(
        _outproj_kernel,
        out_shape=jax.ShapeDtypeStruct((b, s, d), F32),
        grid=(b, s // tm, d // tn),
        in_specs=[
            pl.BlockSpec((1, tm, wa), lambda bi, i, n: (bi, i, 0)),
            pl.BlockSpec((1, tm, wr), lambda bi, i, n: (bi, i, 0)),
            pl.BlockSpec((1, tm, tn), lambda bi, i, n: (bi, i, n)),
            pl.BlockSpec((1, 1, tn), lambda bi, i, n: (bi, 0, n)),
            pl.BlockSpec((None, wa, tn), lambda bi, i, n: (layer, 0, n)),
            pl.BlockSpec((None, wr, tn), lambda bi, i, n: (layer, wa // wr, n)),
        ],
        out_specs=pl.BlockSpec((1, tm, tn), lambda bi, i, n: (bi, i, n)),
        compiler_params=_params("parallel", "parallel", "arbitrary"),
        name="outproj",
    )(o_a, o_r, h, gate, w_out, w_out)


def _dsa_prep_kernel(q_ref, qi_ref, ckv_ref, kiwi_ref, g_ref, qt_ref, qit_ref, ckv_o, ckvt_o, ki_o, wit_o):
    qt_ref[0] = q_ref[0].T
    qit_ref[0] = qi_ref[0].T
    cn = _rms(ckv_ref[0], g_ref[...], RMS_EPS)
    ckv_o[0] = cn.astype(BF16)
    t = DSA_TILE
    ones = jnp.ones((SUM_ROWS, t), BF16)
    for j in range(cn.shape[0] // t):
        ckvt_o[0, j] = jnp.concatenate([cn[j * t:(j + 1) * t].T.astype(BF16), ones], axis=0)
    kw = kiwi_ref[0]
    ki_o[0] = kw[:, :IDX_DIM].astype(BF16)
    wit_o[0] = kw.T[IDX_DIM:IDX_DIM + IDX_HEADS, :] * (IDX_HEADS * IDX_DIM) ** -0.5


def _dsa_prep(za, zb, ckv_g):
    b, s, _ = za.shape
    t = DSA_TILE
    nt = s // t
    tp = _divisor_tile(s, DSA_PREP_TILE)
    assert tp % t == 0

    def seg(width, off):
        return pl.BlockSpec((1, tp, width), lambda bi, i: (bi, i, (off if off < Z_SPLIT else off - Z_SPLIT) // width))

    return pl.pallas_call(
        _dsa_prep_kernel,
        out_shape=(
            jax.ShapeDtypeStruct((b, A_WIDTH, s), BF16),
            jax.ShapeDtypeStruct((b, IDX_HEADS * IDX_DIM, s), BF16),
            jax.ShapeDtypeStruct((b, s, KV_RANK), BF16),
            jax.ShapeDtypeStruct((b, nt, KV_RANK + SUM_ROWS, t), BF16),
            jax.ShapeDtypeStruct((b, s, IDX_DIM), BF16),
            jax.ShapeDtypeStruct((b, IDX_HEADS, s), F32),
        ),
        grid=(b, s // tp),
        in_specs=[
            seg(A_WIDTH, Z_Q), seg(IDX_HEADS * IDX_DIM, Z_QI), seg(KV_RANK, Z_CKV), seg(LORA_PAD, Z_KIWI),
            pl.BlockSpec((1, KV_RANK), lambda bi, i: (0, 0)),
        ],
        out_specs=(
            pl.BlockSpec((1, A_WIDTH, tp), lambda bi, i: (bi, 0, i)),
            pl.BlockSpec((1, IDX_HEADS * IDX_DIM, tp), lambda bi, i: (bi, 0, i)),
            pl.BlockSpec((1, tp, KV_RANK), lambda bi, i: (bi, i, 0)),
            pl.BlockSpec((1, tp // t, KV_RANK + SUM_ROWS, t), lambda bi, i: (bi, i, 0, 0)),
            pl.BlockSpec((1, tp, IDX_DIM), lambda bi, i: (bi, i, 0)),
            pl.BlockSpec((1, IDX_HEADS, tp), lambda bi, i: (bi, 0, i)),
        ),
        compiler_params=_params("parallel", "parallel"),
        name="dsa_prep",
    )(za, za, zb, zb, ckv_g)


def _bias_kernel(t5_ref, o_ref):
    t = DSA_TILE
    j = lax.broadcasted_iota(I32, (t, t), 0)
    i = lax.broadcasted_iota(I32, (t, t), 1)
    for didx in range(2):
        n = jnp.maximum(didx * t + i - j, 0)
        nf = jnp.maximum(n, 1).astype(F32)
        large = REL_MAX_EXACT + (jnp.log(nf / REL_MAX_EXACT) / math.log(REL_MAX_DIST / REL_MAX_EXACT)
                                 * (REL_BUCKETS - REL_MAX_EXACT)).astype(I32)
        large = jnp.minimum(large, REL_BUCKETS - 1)
        bucket = jnp.where(n < REL_MAX_EXACT, n, large)
        for h in range(A_HEADS):
            val = jnp.zeros((t, t), F32)
            for k in range(REL_BUCKETS):
                val = jnp.where(bucket == k, t5_ref[k, h], val)
            o_ref[(1 - didx) * t:(2 - didx) * t, h * t:(h + 1) * t] = (val - t5_ref[REL_BUCKETS - 1, h]) * LOG2E


def _bias_tiles(t5_bias):
    t = DSA_TILE
    assert t + 1 >= FAR_DISTANCE
    return pl.pallas_call(
        _bias_kernel,
        out_shape=jax.ShapeDtypeStruct((2 * t, A_HEADS * t), F32),
        in_specs=[pl.BlockSpec(memory_space=pltpu.SMEM)],
        out_specs=pl.BlockSpec(memory_space=pltpu.VMEM),
        compiler_params=pltpu.CompilerParams(vmem_limit_bytes=VMEM_LIMIT_BYTES),
        name="t5_bias_tiles",
    )(t5_bias)


def _dsa_kernel(qt_ref, qit_ref, ckv_ref, ckvt_ref, ki_ref, wit_ref, wuk_ref, wuvt_ref, bias_ref,
                o_ref, keys_ref, khi_ref, klo_ref, acc_ref, ot_ref, *, topk, seq_len):
    t = DSA_TILE
    qb = pl.program_id(1)
    nk = qb + 1
    row = lax.broadcasted_iota(I32, (t, t), 0)
    col = lax.broadcasted_iota(I32, (t, t), 1)

    def idx_rows(start, rows, diagonal):
        sl = pl.ds(pl.multiple_of(start, rows), rows)
        kic = ki_ref[0, sl, :]
        acc = jnp.zeros((rows, t), F32)
        for h in range(IDX_HEADS):
            rel = _dot(kic, qit_ref[0, h * IDX_DIM:(h + 1) * IDX_DIM, :])
            acc = acc + jnp.maximum(rel, 0.0) * wit_ref[0, h:h + 1, :]
        acc = acc + 0.0
        bits = pltpu.bitcast(acc, I32)
        key = bits ^ ((bits >> 31) & 0x7FFFFFFF)
        if diagonal:
            key = jnp.where(row <= col, key, KEY_NEG_INF)
        keys_ref[sl, :] = key
        khi_ref[sl, :] = (key >> 16).astype(I16)
        klo_ref[sl, :] = ((key & 0xFFFF) + I16_MIN).astype(I16)

    def idx_group(chunks):
        def body(i, first_chunk):
            idx_rows(first_chunk * t, chunks * t, False)
            return first_chunk + chunks
        return body

    done = lax.fori_loop(0, qb // 8, idx_group(8), 0)
    done = lax.fori_loop(0, (qb // 4) % 2, idx_group(4), done)
    done = lax.fori_loop(0, (qb // 2) % 2, idx_group(2), done)
    lax.fori_loop(0, qb % 2, idx_group(1), done)
    idx_rows(qb * t, t, True)

    def count16(ref, cand):
        cand16 = cand.astype(I16)

        def body(kc, accs):
            x = ref[pl.ds(pl.multiple_of(kc * t, t), t), :]
            ones = jnp.where(x >= cand16, jnp.asarray(1, BF16), jnp.asarray(0, BF16))
            ones = ones.reshape(t // PACKED_ROWS, PACKED_ROWS, t)
            accs = list(accs)
            for i in range(t // PACKED_ROWS):
                accs[i % COUNT_CHAINS] = accs[i % COUNT_CHAINS] + ones[i]
            return tuple(accs)

        accs = (jnp.zeros((PACKED_ROWS, t), BF16),) * COUNT_CHAINS
        accs = lax.fori_loop(0, nk // 2, lambda i, a: body(2 * i + 1, body(2 * i, a)), accs)
        accs = lax.fori_loop(2 * (nk // 2), nk, body, accs)
        total = sum(a.astype(F32) for a in accs)
        return jnp.sum(total, axis=0, keepdims=True).astype(I32)

    def search16(ref, base):
        def accept(cand, carry):
            cur, cur_count = carry
            n = base + count16(ref, cand)
            ok = n >= topk
            return jnp.where(ok, cand, cur), jnp.where(ok, n, cur_count)

        everything = jnp.zeros((1, t), I32) + base + nk * t
        carry = accept(jnp.zeros((1, t), I32), (jnp.full((1, t), I16_MIN, I32), everything))
        return lax.fori_loop(0, 15, lambda i, c: accept(c[0] | (jnp.int32(1) << (14 - i)), c), carry)

    hi, _ = search16(khi_ref, 0)
    above = count16(khi_ref, jnp.minimum(hi + 1, -I16_MIN - 1))
    hi16 = hi.astype(I16)

    def keep_low(kc, carry):
        sl = pl.ds(pl.multiple_of(kc * t, t), t)
        klo_ref[sl, :] = jnp.where(khi_ref[sl, :] == hi16, klo_ref[sl, :], jnp.asarray(I16_MIN, I16))
        return carry

    lax.fori_loop(0, nk, keep_low, 0)
    lo, n_selected = search16(klo_ref, above)
    thr = jnp.maximum((hi << 16) | (lo - I16_MIN), KEY_NEG_INF + 1)

    def count(pred):
        def body(kc, acc):
            k = keys_ref[pl.ds(pl.multiple_of(kc * t, t), t), :]
            m = jnp.where(pred(k, kc), 1, 0).astype(I32)
            return acc + jnp.sum(m.reshape(t // SUBLANES, SUBLANES, t), axis=0)

        acc = lax.fori_loop(0, nk, body, jnp.zeros((SUBLANES, t), I32))
        return jnp.sum(acc, axis=0, keepdims=True)

    @pl.when(jnp.max(n_selected) > topk)
    def _():
        need = topk - count(lambda k, kc: k > thr)
        nbits = (seq_len - 1).bit_length()

        def pos_bit(i, y):
            cand = y | (jnp.int32(1) << (nbits - 1 - i))
            before = count(lambda k, kc: (k == thr) & ((kc * t + row) < cand))
            return jnp.where(before < need, cand, y)

        last_kept = lax.fori_loop(0, nbits, pos_bit, jnp.zeros((1, t), I32))

        def demote(kc, carry):
            sl = pl.ds(pl.multiple_of(kc * t, t), t)
            k = keys_ref[sl, :]
            keys_ref[sl, :] = jnp.where((k == thr) & ((kc * t + row) > last_kept), k - 1, k)
            return carry

        lax.fori_loop(0, nk, demote, 0)

    scale = A_HEAD_DIM ** -0.5 * LOG2E

    hw = A_HEADS * t
    qlat = jnp.concatenate(
        [(_dot(wuk_ref[h], qt_ref[0, h * A_HEAD_DIM:(h + 1) * A_HEAD_DIM, :]) * scale).astype(BF16)
         for h in range(A_HEADS)], axis=1)
    acc_ref[...] = jnp.zeros_like(acc_ref)

    def att_chunks(kc, m, *, chunks, bias_row=None):
        rows = chunks * t
        sl = pl.ds(pl.multiple_of(kc * t, t), rows)
        raw = _dot(ckv_ref[0, sl, :], qlat)
        sel = keys_ref[sl, :] >= thr
        parts = []
        for h in range(A_HEADS):
            lg_h = raw[:, h * t:(h + 1) * t]
            if bias_row is not None:
                lg_h = lg_h + bias_ref[bias_row:bias_row + rows, h * t:(h + 1) * t]
            parts.append(jnp.where(sel, lg_h, MASKED_LOGIT).astype(BF16))
        lg = jnp.concatenate(parts, axis=1)
        m_new = jnp.maximum(m, jnp.max(lg, axis=0, keepdims=True).astype(F32))
        p = jnp.exp2(lg - m_new.astype(BF16))
        acc = acc_ref[...] * jnp.exp2(m - m_new)
        for j in range(chunks):
            acc = acc + _dot(ckvt_ref[0, kc + j], p[j * t:(j + 1) * t])
        acc_ref[...] = acc
        return m_new

    n_far = jnp.maximum(qb - 1, 0)
    has_prev = jnp.minimum(qb, 1)
    m = jnp.full((1, hw), MASKED_LOGIT, F32)
    def far_group(chunks):
        def body(i, carry):
            first_chunk, m = carry
            return first_chunk + chunks, att_chunks(first_chunk, m, chunks=chunks)
        return body

    carry = lax.fori_loop(0, n_far // 4, far_group(4), (0, m))
    carry = lax.fori_loop(0, (n_far // 2) % 2, far_group(2), carry)
    _, m = lax.fori_loop(0, n_far % 2, far_group(1), carry)
    m = lax.fori_loop(0, has_prev, lambda i, m: att_chunks(qb - 1, m, chunks=2, bias_row=0), m)
    lax.fori_loop(0, 1 - has_prev, lambda i, m: att_chunks(qb, m, chunks=1, bias_row=t), m)
    olat = (acc_ref[:KV_RANK, :] / acc_ref[KV_RANK:KV_RANK + 1, :]).astype(BF16)
    for h in range(A_HEADS):
        ot_ref[h * A_HEAD_DIM:(h + 1) * A_HEAD_DIM, :] = _dot(wuvt_ref[h], olat[:, h * t:(h + 1) * t])
    o_ref[0] = ot_ref[...].T.astype(BF16)


def _dsa(qt, qit, ckv, ckvt, ki, wit, wuk, wuvt, bias):
    b, _, s = qt.shape
    t = DSA_TILE
    nt = s // t
    topk = min(TOPK_MAX, s // 4)
    assert s // PACKED_ROWS <= BF16_EXACT_INT
    return pl.pallas_call(
        functools.partial(_dsa_kernel, topk=topk, seq_len=s),
        out_shape=jax.ShapeDtypeStruct((b, s, A_WIDTH), BF16),
        grid=(b, nt),
        in_specs=[
            pl.BlockSpec((1, A_WIDTH, t), lambda bi, i: (bi, 0, i)),
            pl.BlockSpec((1, IDX_HEADS * IDX_DIM, t), lambda bi, i: (bi, 0, i)),
            pl.BlockSpec((1, s, KV_RANK), lambda bi, i: (bi, 0, 0)),
            pl.BlockSpec((1, nt, KV_RANK + SUM_ROWS, t), lambda bi, i: (bi, 0, 0, 0)),
            pl.BlockSpec((1, s, IDX_DIM), lambda bi, i: (bi, 0, 0)),
            pl.BlockSpec((1, IDX_HEADS, t), lambda bi, i: (bi, 0, i)),
            pl.BlockSpec((A_HEADS, KV_RANK, A_HEAD_DIM), lambda bi, i: (0, 0, 0)),
            pl.BlockSpec((A_HEADS, A_HEAD_DIM, KV_RANK), lambda bi, i: (0, 0, 0)),
            pl.BlockSpec((2 * t, A_HEADS * t), lambda bi, i: (0, 0)),
        ],
        out_specs=pl.BlockSpec((1, t, A_WIDTH), lambda bi, i: (bi, i, 0)),
        scratch_shapes=[pltpu.VMEM((s, t), I32), pltpu.VMEM((s, t), I16), pltpu.VMEM((s, t), I16),
                        pltpu.VMEM((KV_RANK + SUM_ROWS, A_HEADS * t), F32), pltpu.VMEM((A_WIDTH, t), F32)],
        compiler_params=_params("parallel", "arbitrary"),
        name="dsa",
    )(qt, qit, ckv, ckvt, ki, wit, wuk, wuvt, bias)


def _split3(x):
    hi = x.astype(BF16)
    r1 = x - hi.astype(F32)
    mid = r1.astype(BF16)
    lo = (r1 - mid.astype(F32)).astype(BF16)
    return hi, mid, lo


def _rwkv_kernel(r_ref, k_ref, v_ref, gl_ref, wl_ref, al_ref,
                 mur_ref, muk_ref, muv_ref, mug_ref, muw_ref, mua_ref,
                 w0_ref, a0_ref, kk_ref, ka_ref, rk_ref, lnw_ref, lnb_ref,
                 w2_ref, a2_ref, g2_ref,
                 o_ref,
                 pr_ref, pk_ref, pv_ref, pg_ref, pw_ref, pa_ref, state_ref):
    tl = RWKV_TILE
    c = RWKV_CHUNK
    nc = tl // c
    npair = R_WIDTH // PAIR

    @pl.when(pl.program_id(1) == 0)
    def _():
        state_ref[...] = jnp.zeros_like(state_ref)
        for ref in (pr_ref, pk_ref, pv_ref, pg_ref, pw_ref, pa_ref):
            ref[...] = jnp.zeros_like(ref)

    def shift_mix(x_ref, prev_ref, mu_ref):
        x = x_ref[0]
        first = lax.broadcasted_iota(I32, x.shape, 0) == 0
        xprev = jnp.where(first, prev_ref[...], pltpu.roll(x, 1, axis=0))
        prev_ref[...] = x[tl - 1:tl, :]
        return x + (xprev - x) * mu_ref[...]

    r = shift_mix(r_ref, pr_ref, mur_ref)
    k = shift_mix(k_ref, pk_ref, muk_ref)
    v = shift_mix(v_ref, pv_ref, muv_ref)
    gl = shift_mix(gl_ref, pg_ref, mug_ref)
    wl = shift_mix(wl_ref, pw_ref, muw_ref)
    al = shift_mix(al_ref, pa_ref, mua_ref)

    w_lin = w0_ref[...] + _dot(jnp.tanh(wl).astype(BF16), w2_ref[...])
    nx = -w_lin
    softplus = jnp.maximum(nx, 0.0) + jnp.log(1.0 + jnp.exp(-jnp.abs(nx)))
    ld = -jnp.exp(-softplus - 0.5)
    a = jax.nn.sigmoid(a0_ref[...] + _dot(al.astype(BF16), a2_ref[...]))
    gate = _dot(jax.nn.sigmoid(gl).astype(BF16), g2_ref[...])

    li = lax.broadcasted_iota(I32, (LANES, LANES), 0) // R_HEAD_DIM
    lj = lax.broadcasted_iota(I32, (LANES, LANES), 1) // R_HEAD_DIM
    head_ones = jnp.where(li == lj, 1.0, 0.0).astype(BF16)

    def head_sum(x):
        parts = []
        for j in range(R_WIDTH // LANES):
            hi, mid, lo = _split3(x[:, j * LANES:(j + 1) * LANES])
            parts.append(_dot(hi, head_ones) + _dot(mid, head_ones) + _dot(lo, head_ones))
        return jnp.concatenate(parts, axis=-1)

    kk = k * kk_ref[...]
    kk = kk / jnp.maximum(jnp.sqrt(head_sum(kk * kk)), 1e-12)
    k2 = k * (1.0 + (a - 1.0) * ka_ref[...])
    bb = kk * a

    ti = lax.broadcasted_iota(I32, (tl, tl), 0)
    tj = lax.broadcasted_iota(I32, (tl, tl), 1)
    tri = jnp.where((ti // c == tj // c) & (tj <= ti), 1.0, 0.0).astype(BF16)
    hi, mid, lo = _split3(ld)
    cum = _dot(tri, hi) + _dot(tri, mid) + _dot(tri, lo)
    cum_end = jnp.broadcast_to(cum.reshape(nc, c, R_WIDTH)[:, c - 1:c, :], (nc, c, R_WIDTH)).reshape(tl, R_WIDTH)
    e_in = jnp.exp(cum)
    e_neg = jnp.exp(-cum)
    e_out = jnp.exp(cum_end - cum)

    nb = nc * npair

    def to_pairs(x):
        x3 = x.reshape(nc, c, R_WIDTH)
        xs = jnp.stack([x3[:, :, p * PAIR:(p + 1) * PAIR] for p in range(npair)], axis=1)
        return xs.reshape(nb, c, PAIR)

    lane_head = lax.broadcasted_iota(I32, (2 * c, PAIR), 1) // R_HEAD_DIM
    row_head = lax.broadcasted_iota(I32, (2 * c, PAIR), 0) // c
    same = lane_head == row_head
    rt = lax.broadcasted_iota(I32, (2 * c, PAIR), 0) % c
    ct = lax.broadcasted_iota(I32, (2 * c, PAIR), 1) % c
    strict = same & (ct < rt)
    incl = same & (ct <= rt)
    eye = jnp.where(same & (ct == rt), 1.0, 0.0).astype(F32)

    def bd(x):
        return jnp.where(same, jnp.concatenate([x, x], axis=1), 0.0)

    lane0 = lax.broadcasted_iota(I32, (c, PAIR), 1) < R_HEAD_DIM

    abar = bd(to_pairs(-kk * jnp.exp(cum - ld))).astype(BF16)
    rbar = bd(to_pairs(r * e_in)).astype(BF16)
    bt = bd(to_pairs(bb * e_neg)).astype(BF16)
    kt = bd(to_pairs(k2 * e_neg)).astype(BF16)
    bk = jnp.concatenate([bd(to_pairs(bb * e_out)), bd(to_pairs(k2 * e_out))], axis=1).astype(BF16)
    v_pl = to_pairs(v)
    vbd = bd(v_pl)
    decay_end = to_pairs(jnp.exp(cum_end))[:, 0:1, :]
    aa = _bdot_nt(jnp.concatenate([abar, rbar], axis=1), jnp.concatenate([bt, kt], axis=1))
    a_ab = jnp.where(strict, aa[:, :2 * c, :PAIR], 0.0)
    a_ak = jnp.where(strict, aa[:, :2 * c, PAIR:], 0.0).astype(BF16)
    a_rb = jnp.where(incl, aa[:, 2 * c:, :PAIR], 0.0).astype(BF16)
    a_rk = jnp.where(incl, aa[:, 2 * c:, PAIR:], 0.0).astype(BF16)
    pw = a_ab.astype(BF16)
    tinv = eye + a_ab
    for _ in range(int(math.log2(c)) - 1):
        pw = _bdot(pw, pw).astype(BF16)
        tinv = tinv + _bdot(tinv.astype(BF16), pw)
    akv = _bdot(a_ak, vbd.astype(BF16))
    wu = _bdot(tinv.astype(BF16), jnp.concatenate([abar, akv.astype(BF16)], axis=2))
    w16 = wu[:, :, :PAIR].astype(BF16)
    u_bd = wu[:, :, PAIR:]

    s_bd = state_ref[...]
    e_parts, rs_parts = [], []
    for ci in range(nc):
        sl = slice(ci * npair, (ci + 1) * npair)
        wr = _bdot_nt(jnp.concatenate([w16[sl], rbar[sl]], axis=1), s_bd.astype(BF16))
        e_bd = wr[:, :2 * c] + u_bd[sl]
        e_parts.append(e_bd)
        rs_parts.append(wr[:, 2 * c:])
        ev = jnp.concatenate([e_bd, vbd[sl]], axis=1)
        ds = _bdot(jnp.swapaxes(ev, 1, 2).astype(BF16), bk[sl])
        s_bd = s_bd * decay_end[sl] + jnp.where(same, ds, 0.0)
    state_ref[...] = s_bd

    ev_all = jnp.concatenate([jnp.concatenate(e_parts, axis=0), vbd], axis=1).astype(BF16)
    y_bd = jnp.concatenate(rs_parts, axis=0) + _bdot(jnp.concatenate([a_rb, a_rk], axis=2), ev_all)
    mean = jnp.sum(y_bd, axis=-1, keepdims=True) * (1.0 / R_HEAD_DIM)
    dev = jnp.where(same, y_bd - mean, 0.0)
    var = jnp.sum(dev * dev, axis=-1, keepdims=True) * (1.0 / R_HEAD_DIM)
    yn = dev * lax.rsqrt(var + GN_EPS)
    yn = yn[:, :c] + yn[:, c:]
    rkv = to_pairs(r * k2 * rk_ref[...])
    s0 = jnp.sum(jnp.where(lane0, rkv, 0.0), axis=-1, keepdims=True)
    s1 = jnp.sum(jnp.where(lane0, 0.0, rkv), axis=-1, keepdims=True)
    bonus = jnp.where(lane0, s0, s1) * v_pl
    def per_pair(row_ref):
        rows = [row_ref[:, p * PAIR:(p + 1) * PAIR] for p in range(npair)]
        return jnp.stack(rows * nc, axis=0)

    out = ((yn * per_pair(lnw_ref) + per_pair(lnb_ref) + bonus) * to_pairs(gate)).astype(BF16)
    for ci in range(nc):
        for p in range(npair):
            o_ref[0, ci * c:(ci + 1) * c, p * PAIR:(p + 1) * PAIR] = out[ci * npair + p]


def _rwkv(zb, prm):
    b, s, _ = zb.shape
    tl = RWKV_TILE

    def seg(width, off):
        return pl.BlockSpec((1, tl, width), lambda bi, i: (bi, i, (off - Z_SPLIT) // width))

    def full(arr):
        return pl.BlockSpec(arr.shape, lambda bi, i: (0,) * arr.ndim)

    names = ("mu_r", "mu_k", "mu_v", "mu_g", "mu_w", "mu_a", "w0", "a0", "k_k", "k_a", "r_k", "ln_w", "ln_b",
             "w2", "a2", "g2")
    consts = [prm[n] for n in names]
    return pl.pallas_call(
        _rwkv_kernel,
        out_shape=jax.ShapeDtypeStruct((b, s, R_WIDTH), BF16),
        grid=(b, s // tl),
        in_specs=[seg(R_WIDTH, Z_R), seg(R_WIDTH, Z_K), seg(R_WIDTH, Z_V), seg(GATE_LORA, Z_GL),
                  seg(LORA_PAD, Z_WL), seg(LORA_PAD, Z_AL)] + [full(x) for x in consts],
        out_specs=pl.BlockSpec((1, tl, R_WIDTH), lambda bi, i: (bi, i, 0)),
        scratch_shapes=[
            pltpu.VMEM((1, R_WIDTH), F32), pltpu.VMEM((1, R_WIDTH), F32), pltpu.VMEM((1, R_WIDTH), F32),
            pltpu.VMEM((1, GATE_LORA), F32), pltpu.VMEM((1, LORA_PAD), F32), pltpu.VMEM((1, LORA_PAD), F32),
            pltpu.VMEM((R_WIDTH // PAIR, PAIR, PAIR), F32),
        ],
        compiler_params=_params("parallel", "arbitrary"),
        name="rwkv7",
    )(zb, zb, zb, zb, zb, zb, *consts)


def _pad_cols(w, width):
    return jnp.pad(w, ((0, 0), (0, width - w.shape[1])))


def _w_in_layout_kernel(wt_ref, o_ref):
    wt = wt_ref[0]
    cols = wt.shape[1]
    offs = [0]
    for n in (A_WIDTH, KV_RANK, IDX_HEADS * IDX_DIM, IDX_DIM + IDX_HEADS,
              3 * R_WIDTH, DECAY_LORA, AAA_LORA, GATE_LORA):
        offs.append(offs[-1] + n)
    q, ckv, qi, kiwi, rkv, wl, al, gl = (wt[offs[i]:offs[i + 1], :] for i in range(8))

    def padded(x, height):
        return jnp.concatenate([x, jnp.zeros((height - x.shape[0], cols), x.dtype)], axis=0)

    rows = [q, qi, rkv, gl, ckv, padded(kiwi, LORA_PAD), padded(wl, LORA_PAD), padded(al, LORA_PAD),
            jnp.zeros((Z_WIDTH - Z_END, cols), wt.dtype)]
    o_ref[0] = jnp.concatenate(rows, axis=0).T.astype(BF16)


def _layout_w_in(w_in):
    depth, d, p = w_in.shape
    tc = _divisor_tile(d, 256)
    return pl.pallas_call(
        _w_in_layout_kernel,
        out_shape=jax.ShapeDtypeStruct((depth, d, Z_WIDTH), BF16),
        grid=(depth, d // tc),
        in_specs=[pl.BlockSpec((1, p, tc), lambda l, i: (l, 0, i))],
        out_specs=pl.BlockSpec((1, tc, Z_WIDTH), lambda l, i: (l, i, 0)),
        compiler_params=_params("parallel", "parallel"),
        name="w_in_layout",
    )(jnp.swapaxes(w_in, 1, 2))


def _pad_rows(w, rows):
    return jnp.pad(w, ((0, rows - w.shape[0]), (0, 0)))


def _rwkv_params(l, mu, w0, w2, a0, a2, g2, k_k, k_a, r_k, ln_w, ln_b):
    m = mu[l]
    o = [0, R_WIDTH, 2 * R_WIDTH, 3 * R_WIDTH, 3 * R_WIDTH + DECAY_LORA, 3 * R_WIDTH + DECAY_LORA + AAA_LORA]
    row = lambda x: x.reshape(1, -1)
    return {
        "mu_r": row(m[o[0]:o[1]]), "mu_k": row(m[o[1]:o[2]]), "mu_v": row(m[o[2]:o[3]]),
        "mu_w": _pad_cols(row(m[o[3]:o[4]]), LORA_PAD), "mu_a": _pad_cols(row(m[o[4]:o[5]]), LORA_PAD),
        "mu_g": row(m[o[5]:]),
        "w0": row(w0[l]), "a0": row(a0[l]), "k_k": row(k_k[l]), "k_a": row(k_a[l]), "r_k": row(r_k[l]),
        "ln_w": row(ln_w[l]), "ln_b": row(ln_b[l]),
        "w2": _pad_rows(w2[l], LORA_PAD).astype(BF16), "a2": _pad_rows(a2[l], LORA_PAD).astype(BF16),
        "g2": g2[l].astype(BF16),
    }


def kernel(x, c, t5_bias, ada_w, ada_b, norm_g, ffn_w_in, ffn_w_out, w_in, ckv_norm_g, w_uk, w_uv, rwkv_mu, rwkv_w0, rwkv_w2, rwkv_a0, rwkv_a2, rwkv_g2, rwkv_k_k, rwkv_k_a, rwkv_r_k, rwkv_ln_w, rwkv_ln_b, w_out, final_norm_g):
    b, s, d = x.shape
    depth = ada_w.shape[0]
    assert s % DSA_TILE == 0 and s % RWKV_TILE == 0 and d == A_WIDTH + R_WIDTH
    mod = _ada_mod(c, ada_w, ada_b).reshape(depth, b, N_SUB, 3, 1, d)
    bias = _bias_tiles(t5_bias)
    final_g = final_norm_g.reshape(1, d)
    w_in_z = _layout_w_in(w_in)
    w_out16 = w_out.astype(BF16)
    h = x
    for l in range(depth):
        shift = lambda i: mod[l, :, i, 0]
        scale = lambda i: mod[l, :, i, 1]
        gate = lambda i: mod[l, :, i, 2]
        g = lambda i: norm_g[l, i].reshape(1, d)
        h = _ffn(h, g(0), shift(0), scale(0), gate(0), ffn_w_in, ffn_w_out, l, 0, final_g, final_norm=False)
        za, zb = _proj(h, g(1), shift(1), scale(1), w_in_z, l)
        qt, qit, ckv, ckvt, ki, wit = _dsa_prep(za, zb, ckv_norm_g[l].reshape(1, KV_RANK))
        wuk = jnp.transpose(w_uk[l], (1, 0, 2)).astype(BF16)
        wuvt = jnp.transpose(w_uv[l], (1, 2, 0)).astype(BF16)
        o_a = _dsa(qt, qit, ckv, ckvt, ki, wit, wuk, wuvt, bias)
        o_r = _rwkv(zb, _rwkv_params(l, rwkv_mu, rwkv_w0, rwkv_w2, rwkv_a0, rwkv_a2, rwkv_g2, rwkv_k_k,
                                    rwkv_k_a, rwkv_r_k, rwkv_ln_w, rwkv_ln_b))
        h = _outproj(o_a, o_r, h, gate(1), w_out16, l)
        h = _ffn(h, g(2), shift(2), scale(2), gate(2), ffn_w_in, ffn_w_out, l, 1, final_g,
                 final_norm=(l == depth - 1))
    return h
```

```python
import functools
import math

import jax
import jax.numpy as jnp
from jax import lax
from jax.experimental import pallas as pl
from jax.experimental.pallas import tpu as pltpu

F32 = jnp.float32
BF16 = jnp.bfloat16
I32 = jnp.int32
I16 = jnp.int16

A_HEADS = 8
A_HEAD_DIM = 128
A_WIDTH = A_HEADS * A_HEAD_DIM
KV_RANK = 256
IDX_HEADS = 16
IDX_DIM = 64
TOPK_MAX = 256
REL_BUCKETS = 32
REL_MAX_EXACT = REL_BUCKETS // 2
REL_MAX_DIST = 128
R_HEAD_DIM = 64
R_WIDTH = 1024
DECAY_LORA = 96
AAA_LORA = 96
GATE_LORA = 256
GN_EPS = 64e-5
RMS_EPS = 1e-6
N_SUB = 3

LANES = 128
SUBLANES = 8
VMEM_LIMIT_BYTES = 56 * 1024 * 1024

LORA_PAD = 128
Z_Q = 0
Z_QI = Z_Q + A_WIDTH
Z_R = Z_QI + IDX_HEADS * IDX_DIM
Z_K = Z_R + R_WIDTH
Z_V = Z_K + R_WIDTH
Z_GL = Z_V + R_WIDTH
Z_CKV = Z_GL + GATE_LORA
Z_KIWI = Z_CKV + KV_RANK
Z_WL = Z_KIWI + LORA_PAD
Z_AL = Z_WL + LORA_PAD
Z_END = Z_AL + LORA_PAD
Z_WIDTH = 6144
Z_SPLIT = Z_R

FFN_TOKEN_TILE = 1024
FFN_HIDDEN_TILE = 512
PROLOGUE_ROWS = 256
ADALN_SLAB_ROWS = 16
ADALN_UNROLL = 8
DSA_TILE = 256
DSA_PREP_TILE = 512
RWKV_CHUNK = 64
RWKV_TILE = 256
PAIR = 2 * R_HEAD_DIM

I16_MIN = -(2 ** 15)
PACKED_ROWS = 2 * SUBLANES
SUM_ROWS = PACKED_ROWS
BF16_EXACT_INT = 256
COUNT_CHAINS = 4
KEY_NEG_INF = -2139095041
MASKED_LOGIT = -(2.0 ** 100)
LOG2E = math.log2(math.e)
FAR_DISTANCE = math.ceil(REL_MAX_EXACT * (REL_MAX_DIST / REL_MAX_EXACT)
                         ** ((REL_BUCKETS - 1 - REL_MAX_EXACT) / (REL_BUCKETS - REL_MAX_EXACT)))


def _dot(a, b):
    return jnp.dot(a, b, preferred_element_type=F32)


def _bdot(a, b):
    return lax.dot_general(a, b, (((2,), (1,)), ((0,), (0,))), preferred_element_type=F32)


def _bdot_nt(a, b):
    return lax.dot_general(a, b, (((2,), (2,)), ((0,), (0,))), preferred_element_type=F32)


def _rms(x, g, eps):
    ms = jnp.mean(x * x, axis=-1, keepdims=True)
    return x * lax.rsqrt(ms + eps) * g


def _divisor_tile(n, pref):
    if n <= pref:
        return n
    t = (pref // LANES) * LANES
    while t > LANES and n % t:
        t -= LANES
    assert n % t == 0, (n, pref)
    return t


def _params(*sem):
    return pltpu.CompilerParams(dimension_semantics=sem, vmem_limit_bytes=VMEM_LIMIT_BYTES)


def _ada_kernel(c_ref, w_ref, b_ref, o_ref):
    c = c_ref[...]
    ca = (c * jax.nn.sigmoid(c)).astype(BF16)
    o_ref[0] = _dot(ca, w_ref[0].astype(BF16)) + b_ref[0]


def _ada_mod(c, ada_w, ada_b):
    depth, d, n = ada_w.shape
    b = c.shape[0]
    bp = -(-b // SUBLANES) * SUBLANES
    cp = jnp.pad(c, ((0, bp - b), (0, 0)))
    tn = _divisor_tile(n, 1024)
    out = pl.pallas_call(
        _ada_kernel,
        out_shape=jax.ShapeDtypeStruct((depth, bp, n), F32),
        grid=(depth, n // tn),
        in_specs=[
            pl.BlockSpec((bp, d), lambda l, j: (0, 0)),
            pl.BlockSpec((1, d, tn), lambda l, j: (l, 0, j)),
            pl.BlockSpec((1, 1, tn), lambda l, j: (l, 0, j)),
        ],
        out_specs=pl.BlockSpec((1, bp, tn), lambda l, j: (l, 0, j)),
        compiler_params=_params("arbitrary", "arbitrary"),
        name="ada_mod",
    )(cp, ada_w, ada_b.reshape(depth, 1, n))
    return out[:, :b]


def _adaln_rows(h_ref, g_ref, scale_ref, shift_ref, hn_ref, base, copy_ref=None):
    rows = h_ref.shape[1]
    g = g_ref[...]
    mul = 1.0 + scale_ref[0]
    add = shift_ref[0]

    def slab(i, carry):
        src = pl.ds(pl.multiple_of(i * ADALN_SLAB_ROWS, ADALN_SLAB_ROWS), ADALN_SLAB_ROWS)
        dst = pl.ds(pl.multiple_of(base + i * ADALN_SLAB_ROWS, ADALN_SLAB_ROWS), ADALN_SLAB_ROWS)
        h = h_ref[0, src, :]
        hn_ref[dst, :] = (_rms(h, g, RMS_EPS) * mul + add).astype(BF16)
        if copy_ref is not None:
            copy_ref[0, dst, :] = h
        return carry

    lax.fori_loop(0, rows // ADALN_SLAB_ROWS, slab, 0, unroll=ADALN_UNROLL)


def _prologue_split(tm):
    rows = _divisor_tile(tm, PROLOGUE_ROWS)
    return rows, tm // rows


def _ffn_kernel(h_ref, g_ref, shift_ref, scale_ref, gate_ref, wg_ref, wu_ref, wo_ref, fg_ref,
                o_ref, hn_ref, *, final_norm, pro_steps):
    f = pl.program_id(2)

    @pl.when(f < pro_steps)
    def _():
        _adaln_rows(h_ref, g_ref, scale_ref, shift_ref, hn_ref, f * h_ref.shape[1], o_ref)

    @pl.when(f >= pro_steps)
    def _():
        hn = hn_ref[...]
        g = _dot(hn, wg_ref[...].astype(BF16))
        u = _dot(hn, wu_ref[...].astype(BF16))
        act = (g * jax.nn.sigmoid(g) * u).astype(BF16)
        o_ref[0] += (0.5 * gate_ref[0]) * _dot(act, wo_ref[...].astype(BF16))

    if final_norm:
        @pl.when(f == pl.num_programs(2) - 1)
        def _():
            o_ref[0] = _rms(o_ref[0], fg_ref[...], RMS_EPS)


def _ffn(h, g, shift, scale, gate, w_in, w_out, layer, which, final_g, *, final_norm):
    b, s, d = h.shape
    ff = w_out.shape[2]
    tm = _divisor_tile(s, FFN_TOKEN_TILE)
    tf = _divisor_tile(ff, FFN_HIDDEN_TILE)
    nf = ff // tf
    hr, pro = _prologue_split(tm)
    vec = pl.BlockSpec((1, 1, d), lambda bi, i, f: (bi, 0, 0))
    row = pl.BlockSpec((1, d), lambda bi, i, f: (0, 0))

    def wf(f):
        return jnp.maximum(f - pro, 0)

    return pl.pallas_call(
        functools.partial(_ffn_kernel, final_norm=final_norm, pro_steps=pro),
        out_shape=jax.ShapeDtypeStruct((b, s, d), F32),
        grid=(b, s // tm, pro + nf),
        in_specs=[
            pl.BlockSpec((1, hr, d), lambda bi, i, f: (bi, i * pro + jnp.minimum(f, pro - 1), 0)),
            row, vec, vec, vec,
            pl.BlockSpec((None, None, d, tf), lambda bi, i, f: (layer, which, 0, wf(f))),
            pl.BlockSpec((None, None, d, tf), lambda bi, i, f: (layer, which, 0, nf + wf(f))),
            pl.BlockSpec((None, None, tf, d), lambda bi, i, f: (layer, which, wf(f), 0)),
            row,
        ],
        out_specs=pl.BlockSpec((1, tm, d), lambda bi, i, f: (bi, i, 0)),
        scratch_shapes=[pltpu.VMEM((tm, d), BF16)],
        compiler_params=_params("parallel", "parallel", "arbitrary"),
        name="ffn",
    )(h, g, shift, scale, gate, w_in, w_in, w_out, final_g)


def _proj_kernel(h_ref, g_ref, shift_ref, scale_ref, w_ref, oa_ref, ob_ref, hn_ref, *, attn_tiles):
    n = pl.program_id(2)

    @pl.when(n == 0)
    def _():
        _adaln_rows(h_ref, g_ref, scale_ref, shift_ref, hn_ref, 0)

    z = _dot(hn_ref[...], w_ref[...])
    ob_ref[0] = z

    @pl.when(n < attn_tiles)
    def _():
        oa_ref[0] = z.astype(BF16)


def _proj(h, g, shift, scale, w, layer):
    b, s, d = h.shape
    p = w.shape[2]
    tm = _divisor_tile(s, 1024)
    tn = _divisor_tile(Z_SPLIT, 1024)
    assert Z_SPLIT % tn == 0 and (p - Z_SPLIT) % tn == 0
    na = Z_SPLIT // tn
    vec = pl.BlockSpec((1, 1, d), lambda bi, i, n: (bi, 0, 0))
    return pl.pallas_call(
        functools.partial(_proj_kernel, attn_tiles=na),
        out_shape=(jax.ShapeDtypeStruct((b, s, Z_SPLIT), BF16), jax.ShapeDtypeStruct((b, s, p - Z_SPLIT), F32)),
        grid=(b, s // tm, p // tn),
        in_specs=[
            pl.BlockSpec((1, tm, d), lambda bi, i, n: (bi, i, 0)),
            pl.BlockSpec((1, d), lambda bi, i, n: (0, 0)),
            vec, vec,
            pl.BlockSpec((None, d, tn), lambda bi, i, n: (layer, 0, n)),
        ],
        out_specs=(pl.BlockSpec((1, tm, tn), lambda bi, i, n: (bi, i, jnp.minimum(n, na - 1))),
                   pl.BlockSpec((1, tm, tn), lambda bi, i, n: (bi, i, jnp.maximum(n - na, 0)))),
        scratch_shapes=[pltpu.VMEM((tm, d), BF16)],
        compiler_params=_params("parallel", "parallel", "arbitrary"),
        name="proj",
    )(h, g, shift, scale, w)


def _outproj_kernel(oa_ref, or_ref, h_ref, gate_ref, wa_ref, wr_ref, o_ref):
    acc = _dot(oa_ref[0], wa_ref[...]) + _dot(or_ref[0], wr_ref[...])
    o_ref[0] = h_ref[0] + gate_ref[0] * acc


def _outproj(o_a, o_r, h, gate, w_out, layer):
    b, s, d = h.shape
    tm = _divisor_tile(s, 512)
    tn = d
    wa = o_a.shape[-1]
    wr = o_r.shape[-1]
    return pl.pallas_call(
        _outproj_kernel,
        out_shape=jax.ShapeDtypeStruct((b, s, d), F32),
        grid=(b, s // tm, d // tn),
        in_specs=[
            pl.BlockSpec((1, tm, wa), lambda bi, i, n: (bi, i, 0)),
            pl.BlockSpec((1, tm, wr), lambda bi, i, n: (bi, i, 0)),
            pl.BlockSpec((1, tm, tn), lambda bi, i, n: (bi, i, n)),
            pl.BlockSpec((1, 1, tn), lambda bi, i, n: (bi, 0, n)),
            pl.BlockSpec((None, wa, tn), lambda bi, i, n: (layer, 0, n)),
            pl.BlockSpec((None, wr, tn), lambda bi, i, n: (layer, wa // wr, n)),
        ],
        out_specs=pl.BlockSpec((1, tm, tn), lambda bi, i, n: (bi, i, n)),
        compiler_params=_params("parallel", "parallel", "arbitrary"),
        name="outproj",
    )(o_a, o_r, h, gate, w_out, w_out)


def _dsa_prep_kernel(q_ref, qi_ref, ckv_ref, kiwi_ref, g_ref, qt_ref, qit_ref, ckv_o, ckvt_o, ki_o, wit_o):
    qt_ref[0] = q_ref[0].T
    qit_ref[0] = qi_ref[0].T
    cn = _rms(ckv_ref[0], g_ref[...], RMS_EPS)
    ckv_o[0] = cn.astype(BF16)
    t = DSA_TILE
    ones = jnp.ones((SUM_ROWS, t), BF16)
    for j in range(cn.shape[0] // t):
        ckvt_o[0, j] = jnp.concatenate([cn[j * t:(j + 1) * t].T.astype(BF16), ones], axis=0)
    kw = kiwi_ref[0]
    ki_o[0] = kw[:, :IDX_DIM].astype(BF16)
    wit_o[0] = kw.T[IDX_DIM:IDX_DIM + IDX_HEADS, :] * (IDX_HEADS * IDX_DIM) ** -0.5


def _dsa_prep(za, zb, ckv_g):
    b, s, _ = za.shape
    t = DSA_TILE
    nt = s // t
    tp = _divisor_tile(s, DSA_PREP_TILE)
    assert tp % t == 0

    def seg(width, off):
        return pl.BlockSpec((1, tp, width), lambda bi, i: (bi, i, (off if off < Z_SPLIT else off - Z_SPLIT) // width))

    return pl.pallas_call(
        _dsa_prep_kernel,
        out_shape=(
            jax.ShapeDtypeStruct((b, A_WIDTH, s), BF16),
            jax.ShapeDtypeStruct((b, IDX_HEADS * IDX_DIM, s), BF16),
            jax.ShapeDtypeStruct((b, s, KV_RANK), BF16),
            jax.ShapeDtypeStruct((b, nt, KV_RANK + SUM_ROWS, t), BF16),
            jax.ShapeDtypeStruct((b, s, IDX_DIM), BF16),
            jax.ShapeDtypeStruct((b, IDX_HEADS, s), F32),
        ),
        grid=(b, s // tp),
        in_specs=[
            seg(A_WIDTH, Z_Q), seg(IDX_HEADS * IDX_DIM, Z_QI), seg(KV_RANK, Z_CKV), seg(LORA_PAD, Z_KIWI),
            pl.BlockSpec((1, KV_RANK), lambda bi, i: (0, 0)),
        ],
        out_specs=(
            pl.BlockSpec((1, A_WIDTH, tp), lambda bi, i: (bi, 0, i)),
            pl.BlockSpec((1, IDX_HEADS * IDX_DIM, tp), lambda bi, i: (bi, 0, i)),
            pl.BlockSpec((1, tp, KV_RANK), lambda bi, i: (bi, i, 0)),
            pl.BlockSpec((1, tp // t, KV_RANK + SUM_ROWS, t), lambda bi, i: (bi, i, 0, 0)),
            pl.BlockSpec((1, tp, IDX_DIM), lambda bi, i: (bi, i, 0)),
            pl.BlockSpec((1, IDX_HEADS, tp), lambda bi, i: (bi, 0, i)),
        ),
        compiler_params=_params("parallel", "parallel"),
        name="dsa_prep",
    )(za, za, zb, zb, ckv_g)


def _bias_kernel(t5_ref, o_ref):
    t = DSA_TILE
    j = lax.broadcasted_iota(I32, (t, t), 0)
    i = lax.broadcasted_iota(I32, (t, t), 1)
    for didx in range(2):
        n = jnp.maximum(didx * t + i - j, 0)
        nf = jnp.maximum(n, 1).astype(F32)
        large = REL_MAX_EXACT + (jnp.log(nf / REL_MAX_EXACT) / math.log(REL_MAX_DIST / REL_MAX_EXACT)
                                 * (REL_BUCKETS - REL_MAX_EXACT)).astype(I32)
        large = jnp.minimum(large, REL_BUCKETS - 1)
        bucket = jnp.where(n < REL_MAX_EXACT, n, large)
        for h in range(A_HEADS):
            val = jnp.zeros((t, t), F32)
            for k in range(REL_BUCKETS):
                val = jnp.where(bucket == k, t5_ref[k, h], val)
            o_ref[(1 - didx) * t:(2 - didx) * t, h * t:(h + 1) * t] = (val - t5_ref[REL_BUCKETS - 1, h]) * LOG2E


def _bias_tiles(t5_bias):
    t = DSA_TILE
    assert t + 1 >= FAR_DISTANCE
    return pl.pallas_call(
        _bias_kernel,
        out_shape=jax.ShapeDtypeStruct((2 * t, A_HEADS * t), F32),
        in_specs=[pl.BlockSpec(memory_space=pltpu.SMEM)],
        out_specs=pl.BlockSpec(memory_space=pltpu.VMEM),
        compiler_params=pltpu.CompilerParams(vmem_limit_bytes=VMEM_LIMIT_BYTES),
        name="t5_bias_tiles",
    )(t5_bias)


def _dsa_kernel(qt_ref, qit_ref, ckv_ref, ckvt_ref, ki_ref, wit_ref, wuk_ref, wuvt_ref, bias_ref,
                o_ref, keys_ref, khi_ref, klo_ref, acc_ref, ot_ref, *, topk, seq_len):
    t = DSA_TILE
    qb = pl.program_id(1)
    nk = qb + 1
    row = lax.broadcasted_iota(I32, (t, t), 0)
    col = lax.broadcasted_iota(I32, (t, t), 1)

    def idx_rows(start, rows, diagonal):
        sl = pl.ds(pl.multiple_of(start, rows), rows)
        kic = ki_ref[0, sl, :]
        acc = jnp.zeros((rows, t), F32)
        for h in range(IDX_HEADS):
            rel = _dot(kic, qit_ref[0, h * IDX_DIM:(h + 1) * IDX_DIM, :])
            acc = acc + jnp.maximum(rel, 0.0) * wit_ref[0, h:h + 1, :]
        acc = acc + 0.0
        bits = pltpu.bitcast(acc, I32)
        key = bits ^ ((bits >> 31) & 0x7FFFFFFF)
        if diagonal:
            key = jnp.where(row <= col, key, KEY_NEG_INF)
        keys_ref[sl, :] = key
        khi_ref[sl, :] = (key >> 16).astype(I16)
        klo_ref[sl, :] = ((key & 0xFFFF) + I16_MIN).astype(I16)

    def idx_group(chunks):
        def body(i, first_chunk):
            idx_rows(first_chunk * t, chunks * t, False)
            return first_chunk + chunks
        return body

    done = lax.fori_loop(0, qb // 8, idx_group(8), 0)
    done = lax.fori_loop(0, (qb // 4) % 2, idx_group(4), done)
    done = lax.fori_loop(0, (qb // 2) % 2, idx_group(2), done)
    lax.fori_loop(0, qb % 2, idx_group(1), done)
    idx_rows(qb * t, t, True)

    def count16(ref, cand):
        cand16 = cand.astype(I16)

        def body(kc, accs):
            x = ref[pl.ds(pl.multiple_of(kc * t, t), t), :]
            ones = jnp.where(x >= cand16, jnp.asarray(1, BF16), jnp.asarray(0, BF16))
            ones = ones.reshape(t // PACKED_ROWS, PACKED_ROWS, t)
            accs = list(accs)
            for i in range(t // PACKED_ROWS):
                accs[i % COUNT_CHAINS] = accs[i % COUNT_CHAINS] + ones[i]
            return tuple(accs)

        accs = (jnp.zeros((PACKED_ROWS, t), BF16),) * COUNT_CHAINS
        accs = lax.fori_loop(0, nk // 2, lambda i, a: body(2 * i + 1, body(2 * i, a)), accs)
        accs = lax.fori_loop(2 * (nk // 2), nk, body, accs)
        total = sum(a.astype(F32) for a in accs)
        return jnp.sum(total, axis=0, keepdims=True).astype(I32)

    def search16(ref, base):
        def accept(cand, carry):
            cur, cur_count, next_count = carry
            n = base + count16(ref, cand)
            ok = n >= topk
            return jnp.where(ok, cand, cur), jnp.where(ok, n, cur_count), jnp.where(ok, next_count, n)

        everything = jnp.zeros((1, t), I32) + base + nk * t
        start = (jnp.full((1, t), I16_MIN, I32), everything, jnp.zeros((1, t), I32))
        carry = accept(jnp.zeros((1, t), I32), start)
        return lax.fori_loop(0, 15, lambda i, c: accept(c[0] | (jnp.int32(1) << (14 - i)), c), carry)

    hi, _, above = search16(khi_ref, 0)
    hi16 = hi.astype(I16)

    def keep_low(kc, carry):
        sl = pl.ds(pl.multiple_of(kc * t, t), t)
        klo_ref[sl, :] = jnp.where(khi_ref[sl, :] == hi16, klo_ref[sl, :], jnp.asarray(I16_MIN, I16))
        return carry

    lax.fori_loop(0, nk, keep_low, 0)
    lo, n_selected, _ = search16(klo_ref, above)
    thr = jnp.maximum((hi << 16) | (lo - I16_MIN), KEY_NEG_INF + 1)

    def count(pred):
        def body(kc, acc):
            k = keys_ref[pl.ds(pl.multiple_of(kc * t, t), t), :]
            m = jnp.where(pred(k, kc), 1, 0).astype(I32)
            return acc + jnp.sum(m.reshape(t // SUBLANES, SUBLANES, t), axis=0)

        acc = lax.fori_loop(0, nk, body, jnp.zeros((SUBLANES, t), I32))
        return jnp.sum(acc, axis=0, keepdims=True)

    @pl.when(jnp.max(n_selected) > topk)
    def _():
        need = topk - count(lambda k, kc: k > thr)
        nbits = (seq_len - 1).bit_length()

        def pos_bit(i, y):
            cand = y | (jnp.int32(1) << (nbits - 1 - i))
            before = count(lambda k, kc: (k == thr) & ((kc * t + row) < cand))
            return jnp.where(before < need, cand, y)

        last_kept = lax.fori_loop(0, nbits, pos_bit, jnp.zeros((1, t), I32))

        def demote(kc, carry):
            sl = pl.ds(pl.multiple_of(kc * t, t), t)
            k = keys_ref[sl, :]
            keys_ref[sl, :] = jnp.where((k == thr) & ((kc * t + row) > last_kept), k - 1, k)
            return carry

        lax.fori_loop(0, nk, demote, 0)

    scale = A_HEAD_DIM ** -0.5 * LOG2E

    hw = A_HEADS * t
    qlat = jnp.concatenate(
        [(_dot(wuk_ref[h], qt_ref[0, h * A_HEAD_DIM:(h + 1) * A_HEAD_DIM, :]) * scale).astype(BF16)
         for h in range(A_HEADS)], axis=1)
    acc_ref[...] = jnp.zeros_like(acc_ref)

    def att_chunks(kc, m, *, chunks, bias_row=None):
        rows = chunks * t
        sl = pl.ds(pl.multiple_of(kc * t, t), rows)
        raw = _dot(ckv_ref[0, sl, :], qlat)
        sel = keys_ref[sl, :] >= thr
        parts = []
        for h in range(A_HEADS):
            lg_h = raw[:, h * t:(h + 1) * t]
            if bias_row is not None:
                lg_h = lg_h + bias_ref[bias_row:bias_row + rows, h * t:(h + 1) * t]
            parts.append(jnp.where(sel, lg_h, MASKED_LOGIT).astype(BF16))
        lg = jnp.concatenate(parts, axis=1)
        m_new = jnp.maximum(m, jnp.max(lg, axis=0, keepdims=True).astype(F32))
        p = jnp.exp2(lg - m_new.astype(BF16))
        acc = acc_ref[...] * jnp.exp2(m - m_new)
        for j in range(chunks):
            acc = acc + _dot(ckvt_ref[0, kc + j], p[j * t:(j + 1) * t])
        acc_ref[...] = acc
        return m_new

    n_far = jnp.maximum(qb - 1, 0)
    has_prev = jnp.minimum(qb, 1)
    m = jnp.full((1, hw), MASKED_LOGIT, F32)
    def far_group(chunks):
        def body(i, carry):
            first_chunk, m = carry
            return first_chunk + chunks, att_chunks(first_chunk, m, chunks=chunks)
        return body

    carry = lax.fori_loop(0, n_far // 4, far_group(4), (0, m))
    carry = lax.fori_loop(0, (n_far // 2) % 2, far_group(2), carry)
    _, m = lax.fori_loop(0, n_far % 2, far_group(1), carry)
    m = lax.fori_loop(0, has_prev, lambda i, m: att_chunks(qb - 1, m, chunks=2, bias_row=0), m)
    lax.fori_loop(0, 1 - has_prev, lambda i, m: att_chunks(qb, m, chunks=1, bias_row=t), m)
    olat = (acc_ref[:KV_RANK, :] / acc_ref[KV_RANK:KV_RANK + 1, :]).astype(BF16)
    for h in range(A_HEADS):
        ot_ref[h * A_HEAD_DIM:(h + 1) * A_HEAD_DIM, :] = _dot(wuvt_ref[h], olat[:, h * t:(h + 1) * t])
    o_ref[0] = ot_ref[...].T.astype(BF16)


def _dsa(qt, qit, ckv, ckvt, ki, wit, wuk, wuvt, bias):
    b, _, s = qt.shape
    t = DSA_TILE
    nt = s // t
    topk = min(TOPK_MAX, s // 4)
    assert s // PACKED_ROWS <= BF16_EXACT_INT
    return pl.pallas_call(
        functools.partial(_dsa_kernel, topk=topk, seq_len=s),
        out_shape=jax.ShapeDtypeStruct((b, s, A_WIDTH), BF16),
        grid=(b, nt),
        in_specs=[
            pl.BlockSpec((1, A_WIDTH, t), lambda bi, i: (bi, 0, i)),
            pl.BlockSpec((1, IDX_HEADS * IDX_DIM, t), lambda bi, i: (bi, 0, i)),
            pl.BlockSpec((1, s, KV_RANK), lambda bi, i: (bi, 0, 0)),
            pl.BlockSpec((1, nt, KV_RANK + SUM_ROWS, t), lambda bi, i: (bi, 0, 0, 0)),
            pl.BlockSpec((1, s, IDX_DIM), lambda bi, i: (bi, 0, 0)),
            pl.BlockSpec((1, IDX_HEADS, t), lambda bi, i: (bi, 0, i)),
            pl.BlockSpec((A_HEADS, KV_RANK, A_HEAD_DIM), lambda bi, i: (0, 0, 0)),
            pl.BlockSpec((A_HEADS, A_HEAD_DIM, KV_RANK), lambda bi, i: (0, 0, 0)),
            pl.BlockSpec((2 * t, A_HEADS * t), lambda bi, i: (0, 0)),
        ],
        out_specs=pl.BlockSpec((1, t, A_WIDTH), lambda bi, i: (bi, i, 0)),
        scratch_shapes=[pltpu.VMEM((s, t), I32), pltpu.VMEM((s, t), I16), pltpu.VMEM((s, t), I16),
                        pltpu.VMEM((KV_RANK + SUM_ROWS, A_HEADS * t), F32), pltpu.VMEM((A_WIDTH, t), F32)],
        compiler_params=_params("parallel", "arbitrary"),
        name="dsa",
    )(qt, qit, ckv, ckvt, ki, wit, wuk, wuvt, bias)


def _split3(x):
    hi = x.astype(BF16)
    r1 = x - hi.astype(F32)
    mid = r1.astype(BF16)
    lo = (r1 - mid.astype(F32)).astype(BF16)
    return hi, mid, lo


def _rwkv_kernel(r_ref, k_ref, v_ref, gl_ref, wl_ref, al_ref,
                 mur_ref, muk_ref, muv_ref, mug_ref, muw_ref, mua_ref,
                 w0_ref, a0_ref, kk_ref, ka_ref, rk_ref, lnw_ref, lnb_ref,
                 w2_ref, a2_ref, g2_ref,
                 o_ref,
                 pr_ref, pk_ref, pv_ref, pg_ref, pw_ref, pa_ref, state_ref):
    tl = RWKV_TILE
    c = RWKV_CHUNK
    nc = tl // c
    npair = R_WIDTH // PAIR

    @pl.when(pl.program_id(1) == 0)
    def _():
        state_ref[...] = jnp.zeros_like(state_ref)
        for ref in (pr_ref, pk_ref, pv_ref, pg_ref, pw_ref, pa_ref):
            ref[...] = jnp.zeros_like(ref)

    def shift_mix(x_ref, prev_ref, mu_ref):
        x = x_ref[0]
        first = lax.broadcasted_iota(I32, x.shape, 0) == 0
        xprev = jnp.where(first, prev_ref[...], pltpu.roll(x, 1, axis=0))
        prev_ref[...] = x[tl - 1:tl, :]
        return x + (xprev - x) * mu_ref[...]

    r = shift_mix(r_ref, pr_ref, mur_ref)
    k = shift_mix(k_ref, pk_ref, muk_ref)
    v = shift_mix(v_ref, pv_ref, muv_ref)
    gl = shift_mix(gl_ref, pg_ref, mug_ref)
    wl = shift_mix(wl_ref, pw_ref, muw_ref)
    al = shift_mix(al_ref, pa_ref, mua_ref)

    w_lin = w0_ref[...] + _dot(jnp.tanh(wl).astype(BF16), w2_ref[...])
    nx = -w_lin
    softplus = jnp.maximum(nx, 0.0) + jnp.log(1.0 + jnp.exp(-jnp.abs(nx)))
    ld = -jnp.exp(-softplus - 0.5)
    a = jax.nn.sigmoid(a0_ref[...] + _dot(al.astype(BF16), a2_ref[...]))
    gate = _dot(jax.nn.sigmoid(gl).astype(BF16), g2_ref[...])

    li = lax.broadcasted_iota(I32, (LANES, LANES), 0) // R_HEAD_DIM
    lj = lax.broadcasted_iota(I32, (LANES, LANES), 1) // R_HEAD_DIM
    head_ones = jnp.where(li == lj, 1.0, 0.0).astype(BF16)

    def head_sum(x):
        parts = []
        for j in range(R_WIDTH // LANES):
            hi, mid, lo = _split3(x[:, j * LANES:(j + 1) * LANES])
            parts.append(_dot(hi, head_ones) + _dot(mid, head_ones) + _dot(lo, head_ones))
        return jnp.concatenate(parts, axis=-1)

    kk = k * kk_ref[...]
    kk = kk / jnp.maximum(jnp.sqrt(head_sum(kk * kk)), 1e-12)
    k2 = k * (1.0 + (a - 1.0) * ka_ref[...])
    bb = kk * a

    ti = lax.broadcasted_iota(I32, (tl, tl), 0)
    tj = lax.broadcasted_iota(I32, (tl, tl), 1)
    tri = jnp.where((ti // c == tj // c) & (tj <= ti), 1.0, 0.0).astype(BF16)
    hi, mid, lo = _split3(ld)
    cum = _dot(tri, hi) + _dot(tri, mid) + _dot(tri, lo)
    cum_end = jnp.broadcast_to(cum.reshape(nc, c, R_WIDTH)[:, c - 1:c, :], (nc, c, R_WIDTH)).reshape(tl, R_WIDTH)
    e_in = jnp.exp(cum)
    e_neg = jnp.exp(-cum)
    e_out = jnp.exp(cum_end - cum)

    nb = nc * npair

    def to_pairs(x):
        x3 = x.reshape(nc, c, R_WIDTH)
        xs = jnp.stack([x3[:, :, p * PAIR:(p + 1) * PAIR] for p in range(npair)], axis=1)
        return xs.reshape(nb, c, PAIR)

    lane_head = lax.broadcasted_iota(I32, (2 * c, PAIR), 1) // R_HEAD_DIM
    row_head = lax.broadcasted_iota(I32, (2 * c, PAIR), 0) // c
    same = lane_head == row_head
    rt = lax.broadcasted_iota(I32, (2 * c, PAIR), 0) % c
    ct = lax.broadcasted_iota(I32, (2 * c, PAIR), 1) % c
    strict = same & (ct < rt)
    incl = same & (ct <= rt)
    eye = jnp.where(same & (ct == rt), 1.0, 0.0).astype(F32)

    def bd(x):
        return jnp.where(same, jnp.concatenate([x, x], axis=1), 0.0)

    lane0 = lax.broadcasted_iota(I32, (c, PAIR), 1) < R_HEAD_DIM

    abar = bd(to_pairs(-kk * jnp.exp(cum - ld))).astype(BF16)
    rbar = bd(to_pairs(r * e_in)).astype(BF16)
    bt = bd(to_pairs(bb * e_neg)).astype(BF16)
    kt = bd(to_pairs(k2 * e_neg)).astype(BF16)
    bk = jnp.concatenate([bd(to_pairs(bb * e_out)), bd(to_pairs(k2 * e_out))], axis=1).astype(BF16)
    v_pl = to_pairs(v)
    vbd = bd(v_pl)
    decay_end = to_pairs(jnp.exp(cum_end))[:, 0:1, :]
    aa = _bdot_nt(jnp.concatenate([abar, rbar], axis=1), jnp.concatenate([bt, kt], axis=1))
    a_ab = jnp.where(strict, aa[:, :2 * c, :PAIR], 0.0)
    a_ak = jnp.where(strict, aa[:, :2 * c, PAIR:], 0.0).astype(BF16)
    a_rb = jnp.where(incl, aa[:, 2 * c:, :PAIR], 0.0).astype(BF16)
    a_rk = jnp.where(incl, aa[:, 2 * c:, PAIR:], 0.0).astype(BF16)
    pw = a_ab.astype(BF16)
    tinv = eye + a_ab
    for _ in range(int(math.log2(c)) - 1):
        pw = _bdot(pw, pw).astype(BF16)
        tinv = tinv + _bdot(tinv.astype(BF16), pw)
    akv = _bdot(a_ak, vbd.astype(BF16))
    wu = _bdot(tinv.astype(BF16), jnp.concatenate([abar, akv.astype(BF16)], axis=2))
    w16 = wu[:, :, :PAIR].astype(BF16)
    u_bd = wu[:, :, PAIR:]

    s_bd = state_ref[...]
    e_parts, rs_parts = [], []
    for ci in range(nc):
        sl = slice(ci * npair, (ci + 1) * npair)
        wr = _bdot_nt(jnp.concatenate([w16[sl], rbar[sl]], axis=1), s_bd.astype(BF16))
        e_bd = wr[:, :2 * c] + u_bd[sl]
        e_parts.append(e_bd)
        rs_parts.append(wr[:, 2 * c:])
        ev = jnp.concatenate([e_bd, vbd[sl]], axis=1)
        ds = _bdot(jnp.swapaxes(ev, 1, 2).astype(BF16), bk[sl])
        s_bd = s_bd * decay_end[sl] + jnp.where(same, ds, 0.0)
    state_ref[...] = s_bd

    ev_all = jnp.concatenate([jnp.concatenate(e_parts, axis=0), vbd], axis=1).astype(BF16)
    y_bd = jnp.concatenate(rs_parts, axis=0) + _bdot(jnp.concatenate([a_rb, a_rk], axis=2), ev_all)
    mean = jnp.sum(y_bd, axis=-1, keepdims=True) * (1.0 / R_HEAD_DIM)
    dev = jnp.where(same, y_bd - mean, 0.0)
    var = jnp.sum(dev * dev, axis=-1, keepdims=True) * (1.0 / R_HEAD_DIM)
    yn = dev * lax.rsqrt(var + GN_EPS)
    yn = yn[:, :c] + yn[:, c:]
    rkv = to_pairs(r * k2 * rk_ref[...])
    s0 = jnp.sum(jnp.where(lane0, rkv, 0.0), axis=-1, keepdims=True)
    s1 = jnp.sum(jnp.where(lane0, 0.0, rkv), axis=-1, keepdims=True)
    bonus = jnp.where(lane0, s0, s1) * v_pl
    def per_pair(row_ref):
        rows = [row_ref[:, p * PAIR:(p + 1) * PAIR] for p in range(npair)]
        return jnp.stack(rows * nc, axis=0)

    out = ((yn * per_pair(lnw_ref) + per_pair(lnb_ref) + bonus) * to_pairs(gate)).astype(BF16)
    for ci in range(nc):
        for p in range(npair):
            o_ref[0, ci * c:(ci + 1) * c, p * PAIR:(p + 1) * PAIR] = out[ci * npair + p]


def _rwkv(zb, prm):
    b, s, _ = zb.shape
    tl = RWKV_TILE

    def seg(width, off):
        return pl.BlockSpec((1, tl, width), lambda bi, i: (bi, i, (off - Z_SPLIT) // width))

    def full(arr):
        return pl.BlockSpec(arr.shape, lambda bi, i: (0,) * arr.ndim)

    names = ("mu_r", "mu_k", "mu_v", "mu_g", "mu_w", "mu_a", "w0", "a0", "k_k", "k_a", "r_k", "ln_w", "ln_b",
             "w2", "a2", "g2")
    consts = [prm[n] for n in names]
    return pl.pallas_call(
        _rwkv_kernel,
        out_shape=jax.ShapeDtypeStruct((b, s, R_WIDTH), BF16),
        grid=(b, s // tl),
        in_specs=[seg(R_WIDTH, Z_R), seg(R_WIDTH, Z_K), seg(R_WIDTH, Z_V), seg(GATE_LORA, Z_GL),
                  seg(LORA_PAD, Z_WL), seg(LORA_PAD, Z_AL)] + [full(x) for x in consts],
        out_specs=pl.BlockSpec((1, tl, R_WIDTH), lambda bi, i: (bi, i, 0)),
        scratch_shapes=[
            pltpu.VMEM((1, R_WIDTH), F32), pltpu.VMEM((1, R_WIDTH), F32), pltpu.VMEM((1, R_WIDTH), F32),
            pltpu.VMEM((1, GATE_LORA), F32), pltpu.VMEM((1, LORA_PAD), F32), pltpu.VMEM((1, LORA_PAD), F32),
            pltpu.VMEM((R_WIDTH // PAIR, PAIR, PAIR), F32),
        ],
        compiler_params=_params("parallel", "arbitrary"),
        name="rwkv7",
    )(zb, zb, zb, zb, zb, zb, *consts)


def _pad_cols(w, width):
    return jnp.pad(w, ((0, 0), (0, width - w.shape[1])))


def _w_in_layout_kernel(wt_ref, o_ref):
    wt = wt_ref[0]
    cols = wt.shape[1]
    offs = [0]
    for n in (A_WIDTH, KV_RANK, IDX_HEADS * IDX_DIM, IDX_DIM + IDX_HEADS,
              3 * R_WIDTH, DECAY_LORA, AAA_LORA, GATE_LORA):
        offs.append(offs[-1] + n)
    q, ckv, qi, kiwi, rkv, wl, al, gl = (wt[offs[i]:offs[i + 1], :] for i in range(8))

    def padded(x, height):
        return jnp.concatenate([x, jnp.zeros((height - x.shape[0], cols), x.dtype)], axis=0)

    rows = [q, qi, rkv, gl, ckv, padded(kiwi, LORA_PAD), padded(wl, LORA_PAD), padded(al, LORA_PAD),
            jnp.zeros((Z_WIDTH - Z_END, cols), wt.dtype)]
    o_ref[0] = jnp.concatenate(rows, axis=0).T.astype(BF16)


def _layout_w_in(w_in):
    depth, d, p = w_in.shape
    tc = _divisor_tile(d, 256)
    return pl.pallas_call(
        _w_in_layout_kernel,
        out_shape=jax.ShapeDtypeStruct((depth, d, Z_WIDTH), BF16),
        grid=(depth, d // tc),
        in_specs=[pl.BlockSpec((1, p, tc), lambda l, i: (l, 0, i))],
        out_specs=pl.BlockSpec((1, tc, Z_WIDTH), lambda l, i: (l, i, 0)),
        compiler_params=_params("parallel", "parallel"),
        name="w_in_layout",
    )(jnp.swapaxes(w_in, 1, 2))


def _pad_rows(w, rows):
    return jnp.pad(w, ((0, rows - w.shape[0]), (0, 0)))


def _rwkv_params(l, mu, w0, w2, a0, a2, g2, k_k, k_a, r_k, ln_w, ln_b):
    m = mu[l]
    o = [0, R_WIDTH, 2 * R_WIDTH, 3 * R_WIDTH, 3 * R_WIDTH + DECAY_LORA, 3 * R_WIDTH + DECAY_LORA + AAA_LORA]
    row = lambda x: x.reshape(1, -1)
    return {
        "mu_r": row(m[o[0]:o[1]]), "mu_k": row(m[o[1]:o[2]]), "mu_v": row(m[o[2]:o[3]]),
        "mu_w": _pad_cols(row(m[o[3]:o[4]]), LORA_PAD), "mu_a": _pad_cols(row(m[o[4]:o[5]]), LORA_PAD),
        "mu_g": row(m[o[5]:]),
        "w0": row(w0[l]), "a0": row(a0[l]), "k_k": row(k_k[l]), "k_a": row(k_a[l]), "r_k": row(r_k[l]),
        "ln_w": row(ln_w[l]), "ln_b": row(ln_b[l]),
        "w2": _pad_rows(w2[l], LORA_PAD).astype(BF16), "a2": _pad_rows(a2[l], LORA_PAD).astype(BF16),
        "g2": g2[l].astype(BF16),
    }


def kernel(x, c, t5_bias, ada_w, ada_b, norm_g, ffn_w_in, ffn_w_out, w_in, ckv_norm_g, w_uk, w_uv, rwkv_mu, rwkv_w0, rwkv_w2, rwkv_a0, rwkv_a2, rwkv_g2, rwkv_k_k, rwkv_k_a, rwkv_r_k, rwkv_ln_w, rwkv_ln_b, w_out, final_norm_g):
    b, s, d = x.shape
    depth = ada_w.shape[0]
    assert s % DSA_TILE == 0 and s % RWKV_TILE == 0 and d == A_WIDTH + R_WIDTH
    mod = _ada_mod(c, ada_w, ada_b).reshape(depth, b, N_SUB, 3, 1, d)
    bias = _bias_tiles(t5_bias)
    final_g = final_norm_g.reshape(1, d)
    w_in_z = _layout_w_in(w_in)
    w_out16 = w_out.astype(BF16)
    h = x
    for l in range(depth):
        shift = lambda i: mod[l, :, i, 0]
        scale = lambda i: mod[l, :, i, 1]
        gate = lambda i: mod[l, :, i, 2]
        g = lambda i: norm_g[l, i].reshape(1, d)
        h = _ffn(h, g(0), shift(0), scale(0), gate(0), ffn_w_in, ffn_w_out, l, 0, final_g, final_norm=False)
        za, zb = _proj(h, g(1), shift(1), scale(1), w_in_z, l)
        qt, qit, ckv, ckvt, ki, wit = _dsa_prep(za, zb, ckv_norm_g[l].reshape(1, KV_RANK))
        wuk = jnp.transpose(w_uk[l], (1, 0, 2)).astype(BF16)
        wuvt = jnp.transpose(w_uv[l], (1, 2, 0)).astype(BF16)
        o_a = _dsa(qt, qit, ckv, ckvt, ki, wit, wuk, wuvt, bias)
        o_r = _rwkv(zb, _rwkv_params(l, rwkv_mu, rwkv_w0, rwkv_w2, rwkv_a0, rwkv_a2, rwkv_g2, rwkv_k_k,
                                    rwkv_k_a, rwkv_r_k, rwkv_ln_w, rwkv_ln_b))
        h = _outproj(o_a, o_r, h, gate(1), w_out16, l)
        h = _ffn(h, g(2), shift(2), scale(2), gate(2), ffn_w_in, ffn_w_out, l, 1, final_g,
                 final_norm=(l == depth - 1))
    return h
```

```python
import functools
import math

import jax
import jax.numpy as jnp
from jax import lax
from jax.experimental import pallas as pl
from jax.experimental.pallas import tpu as pltpu

F32 = jnp.float32
BF16 = jnp.bfloat16
I32 = jnp.int32
I16 = jnp.int16

A_HEADS = 8
A_HEAD_DIM = 128
A_WIDTH = A_HEADS * A_HEAD_DIM
KV_RANK = 256
IDX_HEADS = 16
IDX_DIM = 64
TOPK_MAX = 256
REL_BUCKETS = 32
REL_MAX_EXACT = REL_BUCKETS // 2
REL_MAX_DIST = 128
R_HEAD_DIM = 64
R_WIDTH = 1024
DECAY_LORA = 96
AAA_LORA = 96
GATE_LORA = 256
GN_EPS = 64e-5
RMS_EPS = 1e-6
N_SUB = 3

LANES = 128
SUBLANES = 8
VMEM_LIMIT_BYTES = 56 * 1024 * 1024

LORA_PAD = 128
Z_Q = 0
Z_QI = Z_Q + A_WIDTH
Z_R = Z_QI + IDX_HEADS * IDX_DIM
Z_K = Z_R + R_WIDTH
Z_V = Z_K + R_WIDTH
Z_GL = Z_V + R_WIDTH
Z_CKV = Z_GL + GATE_LORA
Z_KIWI = Z_CKV + KV_RANK
Z_WL = Z_KIWI + LORA_PAD
Z_AL = Z_WL + LORA_PAD
Z_END = Z_AL + LORA_PAD
Z_WIDTH = 6144
Z_SPLIT = Z_R

FFN_TOKEN_TILE = 1024
FFN_HIDDEN_TILE = 512
PROLOGUE_ROWS = 256
ADALN_SLAB_ROWS = 16
ADALN_UNROLL = 8
DSA_TILE = 256
DSA_PREP_TILE = 512
RWKV_CHUNK = 64
RWKV_TILE = 256
PAIR = 2 * R_HEAD_DIM

I16_MIN = -(2 ** 15)
PACKED_ROWS = 2 * SUBLANES
SUM_ROWS = PACKED_ROWS
BF16_EXACT_INT = 256
COUNT_CHAINS = 4
KEY_NEG_INF = -2139095041
MASKED_LOGIT = -(2.0 ** 100)
LOG2E = math.log2(math.e)
FAR_DISTANCE = math.ceil(REL_MAX_EXACT * (REL_MAX_DIST / REL_MAX_EXACT)
                         ** ((REL_BUCKETS - 1 - REL_MAX_EXACT) / (REL_BUCKETS - REL_MAX_EXACT)))


def _dot(a, b):
    return jnp.dot(a, b, preferred_element_type=F32)


def _bdot(a, b):
    return lax.dot_general(a, b, (((2,), (1,)), ((0,), (0,))), preferred_element_type=F32)


def _bdot_nt(a, b):
    return lax.dot_general(a, b, (((2,), (2,)), ((0,), (0,))), preferred_element_type=F32)


def _rms(x, g, eps):
    ms = jnp.mean(x * x, axis=-1, keepdims=True)
    return x * lax.rsqrt(ms + eps) * g


def _divisor_tile(n, pref):
    if n <= pref:
        return n
    t = (pref // LANES) * LANES
    while t > LANES and n % t:
        t -= LANES
    assert n % t == 0, (n, pref)
    return t


def _params(*sem):
    return pltpu.CompilerParams(dimension_semantics=sem, vmem_limit_bytes=VMEM_LIMIT_BYTES)


def _ada_kernel(c_ref, w_ref, b_ref, o_ref):
    c = c_ref[...]
    ca = (c * jax.nn.sigmoid(c)).astype(BF16)
    o_ref[0] = _dot(ca, w_ref[0].astype(BF16)) + b_ref[0]


def _ada_mod(c, ada_w, ada_b):
    depth, d, n = ada_w.shape
    b = c.shape[0]
    bp = -(-b // SUBLANES) * SUBLANES
    cp = jnp.pad(c, ((0, bp - b), (0, 0)))
    tn = _divisor_tile(n, 1024)
    out = pl.pallas_call(
        _ada_kernel,
        out_shape=jax.ShapeDtypeStruct((depth, bp, n), F32),
        grid=(depth, n // tn),
        in_specs=[
            pl.BlockSpec((bp, d), lambda l, j: (0, 0)),
            pl.BlockSpec((1, d, tn), lambda l, j: (l, 0, j)),
            pl.BlockSpec((1, 1, tn), lambda l, j: (l, 0, j)),
        ],
        out_specs=pl.BlockSpec((1, bp, tn), lambda l, j: (l, 0, j)),
        compiler_params=_params("arbitrary", "arbitrary"),
        name="ada_mod",
    )(cp, ada_w, ada_b.reshape(depth, 1, n))
    return out[:, :b]


def _adaln_rows(h_ref, g_ref, scale_ref, shift_ref, hn_ref, base, copy_ref=None):
    rows = h_ref.shape[1]
    g = g_ref[...]
    mul = 1.0 + scale_ref[0]
    add = shift_ref[0]

    def slab(i, carry):
        src = pl.ds(pl.multiple_of(i * ADALN_SLAB_ROWS, ADALN_SLAB_ROWS), ADALN_SLAB_ROWS)
        dst = pl.ds(pl.multiple_of(base + i * ADALN_SLAB_ROWS, ADALN_SLAB_ROWS), ADALN_SLAB_ROWS)
        h = h_ref[0, src, :]
        hn_ref[dst, :] = (_rms(h, g, RMS_EPS) * mul + add).astype(BF16)
        if copy_ref is not None:
            copy_ref[0, dst, :] = h
        return carry

    lax.fori_loop(0, rows // ADALN_SLAB_ROWS, slab, 0, unroll=ADALN_UNROLL)


def _prologue_split(tm):
    rows = _divisor_tile(tm, PROLOGUE_ROWS)
    return rows, tm // rows


def _ffn_kernel(h_ref, g_ref, shift_ref, scale_ref, gate_ref, wg_ref, wu_ref, wo_ref, fg_ref,
                o_ref, hn_ref, *, final_norm, pro_steps):
    f = pl.program_id(2)

    @pl.when(f < pro_steps)
    def _():
        _adaln_rows(h_ref, g_ref, scale_ref, shift_ref, hn_ref, f * h_ref.shape[1], o_ref)

    @pl.when(f >= pro_steps)
    def _():
        hn = hn_ref[...]
        g = _dot(hn, wg_ref[...].astype(BF16))
        u = _dot(hn, wu_ref[...].astype(BF16))
        act = (g * jax.nn.sigmoid(g) * u).astype(BF16)
        o_ref[0] += (0.5 * gate_ref[0]) * _dot(act, wo_ref[...].astype(BF16))

    if final_norm:
        @pl.when(f == pl.num_programs(2) - 1)
        def _():
            o_ref[0] = _rms(o_ref[0], fg_ref[...], RMS_EPS)


def _ffn(h, g, shift, scale, gate, w_in, w_out, layer, which, final_g, *, final_norm):
    b, s, d = h.shape
    ff = w_out.shape[2]
    tm = _divisor_tile(s, FFN_TOKEN_TILE)
    tf = _divisor_tile(ff, FFN_HIDDEN_TILE)
    nf = ff // tf
    hr, pro = _prologue_split(tm)
    vec = pl.BlockSpec((1, 1, d), lambda bi, i, f: (bi, 0, 0))
    row = pl.BlockSpec((1, d), lambda bi, i, f: (0, 0))

    def wf(f):
        return jnp.maximum(f - pro, 0)

    return pl.pallas_call(
        functools.partial(_ffn_kernel, final_norm=final_norm, pro_steps=pro),
        out_shape=jax.ShapeDtypeStruct((b, s, d), F32),
        grid=(b, s // tm, pro + nf),
        in_specs=[
            pl.BlockSpec((1, hr, d), lambda bi, i, f: (bi, i * pro + jnp.minimum(f, pro - 1), 0)),
            row, vec, vec, vec,
            pl.BlockSpec((None, None, d, tf), lambda bi, i, f: (layer, which, 0, wf(f))),
            pl.BlockSpec((None, None, d, tf), lambda bi, i, f: (layer, which, 0, nf + wf(f))),
            pl.BlockSpec((None, None, tf, d), lambda bi, i, f: (layer, which, wf(f), 0)),
            row,
        ],
        out_specs=pl.BlockSpec((1, tm, d), lambda bi, i, f: (bi, i, 0)),
        scratch_shapes=[pltpu.VMEM((tm, d), BF16)],
        compiler_params=_params("parallel", "parallel", "arbitrary"),
        name="ffn",
    )(h, g, shift, scale, gate, w_in, w_in, w_out, final_g)


def _proj_kernel(h_ref, g_ref, shift_ref, scale_ref, w_ref, oa_ref, ob_ref, hn_ref, *, attn_tiles):
    n = pl.program_id(2)

    @pl.when(n == 0)
    def _():
        _adaln_rows(h_ref, g_ref, scale_ref, shift_ref, hn_ref, 0)

    z = _dot(hn_ref[...], w_ref[...])
    ob_ref[0] = z

    @pl.when(n < attn_tiles)
    def _():
        oa_ref[0] = z.astype(BF16)


def _proj(h, g, shift, scale, w, layer):
    b, s, d = h.shape
    p = w.shape[2]
    tm = _divisor_tile(s, 1024)
    tn = _divisor_tile(Z_SPLIT, 1024)
    assert Z_SPLIT % tn == 0 and (p - Z_SPLIT) % tn == 0
    na = Z_SPLIT // tn
    vec = pl.BlockSpec((1, 1, d), lambda bi, i, n: (bi, 0, 0))
    return pl.pallas_call(
        functools.partial(_proj_kernel, attn_tiles=na),
        out_shape=(jax.ShapeDtypeStruct((b, s, Z_SPLIT), BF16), jax.ShapeDtypeStruct((b, s, p - Z_SPLIT), F32)),
        grid=(b, s // tm, p // tn),
        in_specs=[
            pl.BlockSpec((1, tm, d), lambda bi, i, n: (bi, i, 0)),
            pl.BlockSpec((1, d), lambda bi, i, n: (0, 0)),
            vec, vec,
            pl.BlockSpec((None, d, tn), lambda bi, i, n: (layer, 0, n)),
        ],
        out_specs=(pl.BlockSpec((1, tm, tn), lambda bi, i, n: (bi, i, jnp.minimum(n, na - 1))),
                   pl.BlockSpec((1, tm, tn), lambda bi, i, n: (bi, i, jnp.maximum(n - na, 0)))),
        scratch_shapes=[pltpu.VMEM((tm, d), BF16)],
        compiler_params=_params("parallel", "parallel", "arbitrary"),
        name="proj",
    )(h, g, shift, scale, w)


def _outproj_kernel(oa_ref, or_ref, h_ref, gate_ref, wa_ref, wr_ref, o_ref):
    acc = _dot(oa_ref[0], wa_ref[...]) + _dot(or_ref[0], wr_ref[...])
    o_ref[0] = h_ref[0] + gate_ref[0] * acc


def _outproj(o_a, o_r, h, gate, w_out, layer):
    b, s, d = h.shape
    tm = _divisor_tile(s, 512)
    tn = d
    wa = o_a.shape[-1]
    wr = o_r.shape[-1]
    return pl.pallas_call(
        _outproj_kernel,
        out_shape=jax.ShapeDtypeStruct((b, s, d), F32),
        grid=(b, s // tm, d // tn),
        in_specs=[
            pl.BlockSpec((1, tm, wa), lambda bi, i, n: (bi, i, 0)),
            pl.BlockSpec((1, tm, wr), lambda bi, i, n: (bi, i, 0)),
            pl.BlockSpec((1, tm, tn), lambda bi, i, n: (bi, i, n)),
            pl.BlockSpec((1, 1, tn), lambda bi, i, n: (bi, 0, n)),
            pl.BlockSpec((None, wa, tn), lambda bi, i, n: (layer, 0, n)),
            pl.BlockSpec((None, wr, tn), lambda bi, i, n: (layer, wa // wr, n)),
        ],
        out_specs=pl.BlockSpec((1, tm, tn), lambda bi, i, n: (bi, i, n)),
        compiler_params=_params("parallel", "parallel", "arbitrary"),
        name="outproj",
    )(o_a, o_r, h, gate, w_out, w_out)


def _dsa_prep_kernel(q_ref, qi_ref, ckv_ref, kiwi_ref, g_ref, qt_ref, qit_ref, ckv_o, ckvt_o, ki_o, wit_o):
    qt_ref[0] = q_ref[0].T
    qit_ref[0] = qi_ref[0].T
    cn = _rms(ckv_ref[0], g_ref[...], RMS_EPS)
    ckv_o[0] = cn.astype(BF16)
    t = DSA_TILE
    ones = jnp.ones((SUM_ROWS, t), BF16)
    for j in range(cn.shape[0] // t):
        ckvt_o[0, j] = jnp.concatenate([cn[j * t:(j + 1) * t].T.astype(BF16), ones], axis=0)
    kw = kiwi_ref[0]
    ki_o[0] = kw[:, :IDX_DIM].astype(BF16)
    wit_o[0] = kw.T[IDX_DIM:IDX_DIM + IDX_HEADS, :] * (IDX_HEADS * IDX_DIM) ** -0.5


def _dsa_prep(za, zb, ckv_g):
    b, s, _ = za.shape
    t = DSA_TILE
    nt = s // t
    tp = _divisor_tile(s, DSA_PREP_TILE)
    assert tp % t == 0

    def seg(width, off):
        return pl.BlockSpec((1, tp, width), lambda bi, i: (bi, i, (off if off < Z_SPLIT else off - Z_SPLIT) // width))

    return pl.pallas_call(
        _dsa_prep_kernel,
        out_shape=(
            jax.ShapeDtypeStruct((b, A_WIDTH, s), BF16),
            jax.ShapeDtypeStruct((b, IDX_HEADS * IDX_DIM, s), BF16),
            jax.ShapeDtypeStruct((b, s, KV_RANK), BF16),
            jax.ShapeDtypeStruct((b, nt, KV_RANK + SUM_ROWS, t), BF16),
            jax.ShapeDtypeStruct((b, s, IDX_DIM), BF16),
            jax.ShapeDtypeStruct((b, IDX_HEADS, s), F32),
        ),
        grid=(b, s // tp),
        in_specs=[
            seg(A_WIDTH, Z_Q), seg(IDX_HEADS * IDX_DIM, Z_QI), seg(KV_RANK, Z_CKV), seg(LORA_PAD, Z_KIWI),
            pl.BlockSpec((1, KV_RANK), lambda bi, i: (0, 0)),
        ],
        out_specs=(
            pl.BlockSpec((1, A_WIDTH, tp), lambda bi, i: (bi, 0, i)),
            pl.BlockSpec((1, IDX_HEADS * IDX_DIM, tp), lambda bi, i: (bi, 0, i)),
            pl.BlockSpec((1, tp, KV_RANK), lambda bi, i: (bi, i, 0)),
            pl.BlockSpec((1, tp // t, KV_RANK + SUM_ROWS, t), lambda bi, i: (bi, i, 0, 0)),
            pl.BlockSpec((1, tp, IDX_DIM), lambda bi, i: (bi, i, 0)),
            pl.BlockSpec((1, IDX_HEADS, tp), lambda bi, i: (bi, 0, i)),
        ),
        compiler_params=_params("parallel", "parallel"),
        name="dsa_prep",
    )(za, za, zb, zb, ckv_g)


def _bias_kernel(t5_ref, o_ref):
    t = DSA_TILE
    j = lax.broadcasted_iota(I32, (t, t), 0)
    i = lax.broadcasted_iota(I32, (t, t), 1)
    for didx in range(2):
        n = jnp.maximum(didx * t + i - j, 0)
        nf = jnp.maximum(n, 1).astype(F32)
        large = REL_MAX_EXACT + (jnp.log(nf / REL_MAX_EXACT) / math.log(REL_MAX_DIST / REL_MAX_EXACT)
                                 * (REL_BUCKETS - REL_MAX_EXACT)).astype(I32)
        large = jnp.minimum(large, REL_BUCKETS - 1)
        bucket = jnp.where(n < REL_MAX_EXACT, n, large)
        for h in range(A_HEADS):
            val = jnp.zeros((t, t), F32)
            for k in range(REL_BUCKETS):
                val = jnp.where(bucket == k, t5_ref[k, h], val)
            o_ref[(1 - didx) * t:(2 - didx) * t, h * t:(h + 1) * t] = (val - t5_ref[REL_BUCKETS - 1, h]) * LOG2E


def _bias_tiles(t5_bias):
    t = DSA_TILE
    assert t + 1 >= FAR_DISTANCE
    return pl.pallas_call(
        _bias_kernel,
        out_shape=jax.ShapeDtypeStruct((2 * t, A_HEADS * t), F32),
        in_specs=[pl.BlockSpec(memory_space=pltpu.SMEM)],
        out_specs=pl.BlockSpec(memory_space=pltpu.VMEM),
        compiler_params=pltpu.CompilerParams(vmem_limit_bytes=VMEM_LIMIT_BYTES),
        name="t5_bias_tiles",
    )(t5_bias)


def _dsa_kernel(qt_ref, qit_ref, ckv_ref, ckvt_ref, ki_ref, wit_ref, wuk_ref, wuvt_ref, bias_ref,
                o_ref, keys_ref, khi_ref, klo_ref, acc_ref, ot_ref, *, topk, seq_len):
    t = DSA_TILE
    qb = pl.program_id(1)
    nk = qb + 1
    row = lax.broadcasted_iota(I32, (t, t), 0)
    col = lax.broadcasted_iota(I32, (t, t), 1)

    def idx_rows(start, rows, diagonal):
        sl = pl.ds(pl.multiple_of(start, rows), rows)
        kic = ki_ref[0, sl, :]
        acc = jnp.zeros((rows, t), F32)
        for h in range(IDX_HEADS):
            rel = _dot(kic, qit_ref[0, h * IDX_DIM:(h + 1) * IDX_DIM, :])
            acc = acc + jnp.maximum(rel, 0.0) * wit_ref[0, h:h + 1, :]
        acc = acc + 0.0
        bits = pltpu.bitcast(acc, I32)
        key = bits ^ ((bits >> 31) & 0x7FFFFFFF)
        if diagonal:
            key = jnp.where(row <= col, key, KEY_NEG_INF)
        keys_ref[sl, :] = key
        khi_ref[sl, :] = (key >> 16).astype(I16)
        klo_ref[sl, :] = ((key & 0xFFFF) + I16_MIN).astype(I16)

    def idx_group(chunks):
        def body(i, first_chunk):
            idx_rows(first_chunk * t, chunks * t, False)
            return first_chunk + chunks
        return body

    done = lax.fori_loop(0, qb // 8, idx_group(8), 0)
    done = lax.fori_loop(0, (qb // 4) % 2, idx_group(4), done)
    done = lax.fori_loop(0, (qb // 2) % 2, idx_group(2), done)
    lax.fori_loop(0, qb % 2, idx_group(1), done)
    idx_rows(qb * t, t, True)

    def count16(ref, cand):
        cand16 = cand.astype(I16)

        def body(kc, accs):
            x = ref[pl.ds(pl.multiple_of(kc * t, t), t), :]
            ones = jnp.where(x >= cand16, jnp.asarray(1, BF16), jnp.asarray(0, BF16))
            ones = ones.reshape(t // PACKED_ROWS, PACKED_ROWS, t)
            accs = list(accs)
            for i in range(t // PACKED_ROWS):
                accs[i % COUNT_CHAINS] = accs[i % COUNT_CHAINS] + ones[i]
            return tuple(accs)

        accs = (jnp.zeros((PACKED_ROWS, t), BF16),) * COUNT_CHAINS
        accs = lax.fori_loop(0, nk // 2, lambda i, a: body(2 * i + 1, body(2 * i, a)), accs)
        accs = lax.fori_loop(2 * (nk // 2), nk, body, accs)
        total = sum(a.astype(F32) for a in accs)
        return jnp.sum(total, axis=0, keepdims=True).astype(I32)

    def search16(ref, base):
        def accept(cand, carry):
            cur, cur_count, next_count = carry
            n = base + count16(ref, cand)
            ok = n >= topk
            return jnp.where(ok, cand, cur), jnp.where(ok, n, cur_count), jnp.where(ok, next_count, n)

        everything = jnp.zeros((1, t), I32) + base + nk * t
        start = (jnp.full((1, t), I16_MIN, I32), everything, jnp.zeros((1, t), I32))
        carry = accept(jnp.zeros((1, t), I32), start)
        return lax.fori_loop(0, 15, lambda i, c: accept(c[0] | (jnp.int32(1) << (14 - i)), c), carry)

    hi, n_hi, above = search16(khi_ref, 0)
    hi16 = hi.astype(I16)

    def keep_low(kc, carry):
        sl = pl.ds(pl.multiple_of(kc * t, t), t)
        klo_ref[sl, :] = jnp.where(khi_ref[sl, :] == hi16, klo_ref[sl, :], jnp.asarray(I16_MIN, I16))
        return carry

    lax.fori_loop(0, nk, keep_low, 0)
    lo, n_lo, _ = search16(klo_ref, above)
    thr = jnp.maximum((hi << 16) | (lo - I16_MIN), KEY_NEG_INF + 1)
    n_selected = jnp.where(lo == I16_MIN, n_hi, n_lo)

    def count(pred):
        def body(kc, acc):
            k = keys_ref[pl.ds(pl.multiple_of(kc * t, t), t), :]
            m = jnp.where(pred(k, kc), 1, 0).astype(I32)
            return acc + jnp.sum(m.reshape(t // SUBLANES, SUBLANES, t), axis=0)

        acc = lax.fori_loop(0, nk, body, jnp.zeros((SUBLANES, t), I32))
        return jnp.sum(acc, axis=0, keepdims=True)

    @pl.when(jnp.max(n_selected) > topk)
    def _():
        need = topk - count(lambda k, kc: k > thr)
        nbits = (seq_len - 1).bit_length()

        def pos_bit(i, y):
            cand = y | (jnp.int32(1) << (nbits - 1 - i))
            before = count(lambda k, kc: (k == thr) & ((kc * t + row) < cand))
            return jnp.where(before < need, cand, y)

        last_kept = lax.fori_loop(0, nbits, pos_bit, jnp.zeros((1, t), I32))

        def demote(kc, carry):
            sl = pl.ds(pl.multiple_of(kc * t, t), t)
            k = keys_ref[sl, :]
            keys_ref[sl, :] = jnp.where((k == thr) & ((kc * t + row) > last_kept), k - 1, k)
            return carry

        lax.fori_loop(0, nk, demote, 0)

    scale = A_HEAD_DIM ** -0.5 * LOG2E

    hw = A_HEADS * t
    qlat = jnp.concatenate(
        [(_dot(wuk_ref[h], qt_ref[0, h * A_HEAD_DIM:(h + 1) * A_HEAD_DIM, :]) * scale).astype(BF16)
         for h in range(A_HEADS)], axis=1)
    acc_ref[...] = jnp.zeros_like(acc_ref)

    def att_chunks(kc, m, *, chunks, bias_row=None):
        rows = chunks * t
        sl = pl.ds(pl.multiple_of(kc * t, t), rows)
        raw = _dot(ckv_ref[0, sl, :], qlat)
        sel = keys_ref[sl, :] >= thr
        parts = []
        for h in range(A_HEADS):
            lg_h = raw[:, h * t:(h + 1) * t]
            if bias_row is not None:
                lg_h = lg_h + bias_ref[bias_row:bias_row + rows, h * t:(h + 1) * t]
            parts.append(jnp.where(sel, lg_h, MASKED_LOGIT).astype(BF16))
        lg = jnp.concatenate(parts, axis=1)
        m_new = jnp.maximum(m, jnp.max(lg, axis=0, keepdims=True).astype(F32))
        p = jnp.exp2(lg - m_new.astype(BF16))
        acc = acc_ref[...] * jnp.exp2(m - m_new)
        for j in range(chunks):
            acc = acc + _dot(ckvt_ref[0, kc + j], p[j * t:(j + 1) * t])
        acc_ref[...] = acc
        return m_new

    n_far = jnp.maximum(qb - 1, 0)
    has_prev = jnp.minimum(qb, 1)
    m = jnp.full((1, hw), MASKED_LOGIT, F32)
    def far_group(chunks):
        def body(i, carry):
            first_chunk, m = carry
            return first_chunk + chunks, att_chunks(first_chunk, m, chunks=chunks)
        return body

    carry = lax.fori_loop(0, n_far // 4, far_group(4), (0, m))
    carry = lax.fori_loop(0, (n_far // 2) % 2, far_group(2), carry)
    _, m = lax.fori_loop(0, n_far % 2, far_group(1), carry)
    m = lax.fori_loop(0, has_prev, lambda i, m: att_chunks(qb - 1, m, chunks=2, bias_row=0), m)
    lax.fori_loop(0, 1 - has_prev, lambda i, m: att_chunks(qb, m, chunks=1, bias_row=t), m)
    olat = (acc_ref[:KV_RANK, :] / acc_ref[KV_RANK:KV_RANK + 1, :]).astype(BF16)
    for h in range(A_HEADS):
        ot_ref[h * A_HEAD_DIM:(h + 1) * A_HEAD_DIM, :] = _dot(wuvt_ref[h], olat[:, h * t:(h + 1) * t])
    o_ref[0] = ot_ref[...].T.astype(BF16)


def _dsa(qt, qit, ckv, ckvt, ki, wit, wuk, wuvt, bias):
    b, _, s = qt.shape
    t = DSA_TILE
    nt = s // t
    topk = min(TOPK_MAX, s // 4)
    assert s // PACKED_ROWS <= BF16_EXACT_INT
    return pl.pallas_call(
        functools.partial(_dsa_kernel, topk=topk, seq_len=s),
        out_shape=jax.ShapeDtypeStruct((b, s, A_WIDTH), BF16),
        grid=(b, nt),
        in_specs=[
            pl.BlockSpec((1, A_WIDTH, t), lambda bi, i: (bi, 0, i)),
            pl.BlockSpec((1, IDX_HEADS * IDX_DIM, t), lambda bi, i: (bi, 0, i)),
            pl.BlockSpec((1, s, KV_RANK), lambda bi, i: (bi, 0, 0)),
            pl.BlockSpec((1, nt, KV_RANK + SUM_ROWS, t), lambda bi, i: (bi, 0, 0, 0)),
            pl.BlockSpec((1, s, IDX_DIM), lambda bi, i: (bi, 0, 0)),
            pl.BlockSpec((1, IDX_HEADS, t), lambda bi, i: (bi, 0, i)),
            pl.BlockSpec((A_HEADS, KV_RANK, A_HEAD_DIM), lambda bi, i: (0, 0, 0)),
            pl.BlockSpec((A_HEADS, A_HEAD_DIM, KV_RANK), lambda bi, i: (0, 0, 0)),
            pl.BlockSpec((2 * t, A_HEADS * t), lambda bi, i: (0, 0)),
        ],
        out_specs=pl.BlockSpec((1, t, A_WIDTH), lambda bi, i: (bi, i, 0)),
        scratch_shapes=[pltpu.VMEM((s, t), I32), pltpu.VMEM((s, t), I16), pltpu.VMEM((s, t), I16),
                        pltpu.VMEM((KV_RANK + SUM_ROWS, A_HEADS * t), F32), pltpu.VMEM((A_WIDTH, t), F32)],
        compiler_params=_params("parallel", "arbitrary"),
        name="dsa",
    )(qt, qit, ckv, ckvt, ki, wit, wuk, wuvt, bias)


def _split3(x):
    hi = x.astype(BF16)
    r1 = x - hi.astype(F32)
    mid = r1.astype(BF16)
    lo = (r1 - mid.astype(F32)).astype(BF16)
    return hi, mid, lo


def _rwkv_kernel(r_ref, k_ref, v_ref, gl_ref, wl_ref, al_ref,
                 mur_ref, muk_ref, muv_ref, mug_ref, muw_ref, mua_ref,
                 w0_ref, a0_ref, kk_ref, ka_ref, rk_ref, lnw_ref, lnb_ref,
                 w2_ref, a2_ref, g2_ref,
                 o_ref,
                 pr_ref, pk_ref, pv_ref, pg_ref, pw_ref, pa_ref, state_ref):
    tl = RWKV_TILE
    c = RWKV_CHUNK
    nc = tl // c
    npair = R_WIDTH // PAIR

    @pl.when(pl.program_id(1) == 0)
    def _():
        state_ref[...] = jnp.zeros_like(state_ref)
        for ref in (pr_ref, pk_ref, pv_ref, pg_ref, pw_ref, pa_ref):
            ref[...] = jnp.zeros_like(ref)

    def shift_mix(x_ref, prev_ref, mu_ref):
        x = x_ref[0]
        first = lax.broadcasted_iota(I32, x.shape, 0) == 0
        xprev = jnp.where(first, prev_ref[...], pltpu.roll(x, 1, axis=0))
        prev_ref[...] = x[tl - 1:tl, :]
        return x + (xprev - x) * mu_ref[...]

    r = shift_mix(r_ref, pr_ref, mur_ref)
    k = shift_mix(k_ref, pk_ref, muk_ref)
    v = shift_mix(v_ref, pv_ref, muv_ref)
    gl = shift_mix(gl_ref, pg_ref, mug_ref)
    wl = shift_mix(wl_ref, pw_ref, muw_ref)
    al = shift_mix(al_ref, pa_ref, mua_ref)

    w_lin = w0_ref[...] + _dot(jnp.tanh(wl).astype(BF16), w2_ref[...])
    nx = -w_lin
    softplus = jnp.maximum(nx, 0.0) + jnp.log(1.0 + jnp.exp(-jnp.abs(nx)))
    ld = -jnp.exp(-softplus - 0.5)
    a = jax.nn.sigmoid(a0_ref[...] + _dot(al.astype(BF16), a2_ref[...]))
    gate = _dot(jax.nn.sigmoid(gl).astype(BF16), g2_ref[...])

    li = lax.broadcasted_iota(I32, (LANES, LANES), 0) // R_HEAD_DIM
    lj = lax.broadcasted_iota(I32, (LANES, LANES), 1) // R_HEAD_DIM
    head_ones = jnp.where(li == lj, 1.0, 0.0).astype(BF16)

    def head_sum(x):
        parts = []
        for j in range(R_WIDTH // LANES):
            hi, mid, lo = _split3(x[:, j * LANES:(j + 1) * LANES])
            parts.append(_dot(hi, head_ones) + _dot(mid, head_ones) + _dot(lo, head_ones))
        return jnp.concatenate(parts, axis=-1)

    kk = k * kk_ref[...]
    kk = kk / jnp.maximum(jnp.sqrt(head_sum(kk * kk)), 1e-12)
    k2 = k * (1.0 + (a - 1.0) * ka_ref[...])
    bb = kk * a

    ti = lax.broadcasted_iota(I32, (tl, tl), 0)
    tj = lax.broadcasted_iota(I32, (tl, tl), 1)
    tri = jnp.where((ti // c == tj // c) & (tj <= ti), 1.0, 0.0).astype(BF16)
    hi, mid, lo = _split3(ld)
    cum = _dot(tri, hi) + _dot(tri, mid) + _dot(tri, lo)
    cum_end = jnp.broadcast_to(cum.reshape(nc, c, R_WIDTH)[:, c - 1:c, :], (nc, c, R_WIDTH)).reshape(tl, R_WIDTH)
    e_in = jnp.exp(cum)
    e_neg = jnp.exp(-cum)
    e_out = jnp.exp(cum_end - cum)

    nb = nc * npair

    def to_pairs(x):
        x3 = x.reshape(nc, c, R_WIDTH)
        xs = jnp.stack([x3[:, :, p * PAIR:(p + 1) * PAIR] for p in range(npair)], axis=1)
        return xs.reshape(nb, c, PAIR)

    lane_head = lax.broadcasted_iota(I32, (2 * c, PAIR), 1) // R_HEAD_DIM
    row_head = lax.broadcasted_iota(I32, (2 * c, PAIR), 0) // c
    same = lane_head == row_head
    rt = lax.broadcasted_iota(I32, (2 * c, PAIR), 0) % c
    ct = lax.broadcasted_iota(I32, (2 * c, PAIR), 1) % c
    strict = same & (ct < rt)
    incl = same & (ct <= rt)
    eye = jnp.where(same & (ct == rt), 1.0, 0.0).astype(F32)

    def bd(x):
        return jnp.where(same, jnp.concatenate([x, x], axis=1), 0.0)

    lane0 = lax.broadcasted_iota(I32, (c, PAIR), 1) < R_HEAD_DIM

    abar = bd(to_pairs(-kk * jnp.exp(cum - ld))).astype(BF16)
    rbar = bd(to_pairs(r * e_in)).astype(BF16)
    bt = bd(to_pairs(bb * e_neg)).astype(BF16)
    kt = bd(to_pairs(k2 * e_neg)).astype(BF16)
    bk = jnp.concatenate([bd(to_pairs(bb * e_out)), bd(to_pairs(k2 * e_out))], axis=1).astype(BF16)
    v_pl = to_pairs(v)
    vbd = bd(v_pl)
    decay_end = to_pairs(jnp.exp(cum_end))[:, 0:1, :]
    aa = _bdot_nt(jnp.concatenate([abar, rbar], axis=1), jnp.concatenate([bt, kt], axis=1))
    a_ab = jnp.where(strict, aa[:, :2 * c, :PAIR], 0.0)
    a_ak = jnp.where(strict, aa[:, :2 * c, PAIR:], 0.0).astype(BF16)
    a_rb = jnp.where(incl, aa[:, 2 * c:, :PAIR], 0.0).astype(BF16)
    a_rk = jnp.where(incl, aa[:, 2 * c:, PAIR:], 0.0).astype(BF16)
    pw = a_ab.astype(BF16)
    tinv = eye + a_ab
    for _ in range(int(math.log2(c)) - 1):
        pw = _bdot(pw, pw).astype(BF16)
        tinv = tinv + _bdot(tinv.astype(BF16), pw)
    akv = _bdot(a_ak, vbd.astype(BF16))
    wu = _bdot(tinv.astype(BF16), jnp.concatenate([abar, akv.astype(BF16)], axis=2))
    w16 = wu[:, :, :PAIR].astype(BF16)
    u_bd = wu[:, :, PAIR:]

    s_bd = state_ref[...]
    e_parts, rs_parts = [], []
    for ci in range(nc):
        sl = slice(ci * npair, (ci + 1) * npair)
        wr = _bdot_nt(jnp.concatenate([w16[sl], rbar[sl]], axis=1), s_bd.astype(BF16))
        e_bd = wr[:, :2 * c] + u_bd[sl]
        e_parts.append(e_bd)
        rs_parts.append(wr[:, 2 * c:])
        ev = jnp.concatenate([e_bd, vbd[sl]], axis=1)
        ds = _bdot(jnp.swapaxes(ev, 1, 2).astype(BF16), bk[sl])
        s_bd = s_bd * decay_end[sl] + jnp.where(same, ds, 0.0)
    state_ref[...] = s_bd

    ev_all = jnp.concatenate([jnp.concatenate(e_parts, axis=0), vbd], axis=1).astype(BF16)
    y_bd = jnp.concatenate(rs_parts, axis=0) + _bdot(jnp.concatenate([a_rb, a_rk], axis=2), ev_all)
    mean = jnp.sum(y_bd, axis=-1, keepdims=True) * (1.0 / R_HEAD_DIM)
    dev = jnp.where(same, y_bd - mean, 0.0)
    var = jnp.sum(dev * dev, axis=-1, keepdims=True) * (1.0 / R_HEAD_DIM)
    yn = dev * lax.rsqrt(var + GN_EPS)
    yn = yn[:, :c] + yn[:, c:]
    rkv = to_pairs(r * k2 * rk_ref[...])
    s0 = jnp.sum(jnp.where(lane0, rkv, 0.0), axis=-1, keepdims=True)
    s1 = jnp.sum(jnp.where(lane0, 0.0, rkv), axis=-1, keepdims=True)
    bonus = jnp.where(lane0, s0, s1) * v_pl
    def per_pair(row_ref):
        rows = [row_ref[:, p * PAIR:(p + 1) * PAIR] for p in range(npair)]
        return jnp.stack(rows * nc, axis=0)

    out = ((yn * per_pair(lnw_ref) + per_pair(lnb_ref) + bonus) * to_pairs(gate)).astype(BF16)
    for ci in range(nc):
        for p in range(npair):
            o_ref[0, ci * c:(ci + 1) * c, p * PAIR:(p + 1) * PAIR] = out[ci * npair + p]


def _rwkv(zb, prm):
    b, s, _ = zb.shape
    tl = RWKV_TILE

    def seg(width, off):
        return pl.BlockSpec((1, tl, width), lambda bi, i: (bi, i, (off - Z_SPLIT) // width))

    def full(arr):
        return pl.BlockSpec(arr.shape, lambda bi, i: (0,) * arr.ndim)

    names = ("mu_r", "mu_k", "mu_v", "mu_g", "mu_w", "mu_a", "w0", "a0", "k_k", "k_a", "r_k", "ln_w", "ln_b",
             "w2", "a2", "g2")
    consts = [prm[n] for n in names]
    return pl.pallas_call(
        _rwkv_kernel,
        out_shape=jax.ShapeDtypeStruct((b, s, R_WIDTH), BF16),
        grid=(b, s // tl),
        in_specs=[seg(R_WIDTH, Z_R), seg(R_WIDTH, Z_K), seg(R_WIDTH, Z_V), seg(GATE_LORA, Z_GL),
                  seg(LORA_PAD, Z_WL), seg(LORA_PAD, Z_AL)] + [full(x) for x in consts],
        out_specs=pl.BlockSpec((1, tl, R_WIDTH), lambda bi, i: (bi, i, 0)),
        scratch_shapes=[
            pltpu.VMEM((1, R_WIDTH), F32), pltpu.VMEM((1, R_WIDTH), F32), pltpu.VMEM((1, R_WIDTH), F32),
            pltpu.VMEM((1, GATE_LORA), F32), pltpu.VMEM((1, LORA_PAD), F32), pltpu.VMEM((1, LORA_PAD), F32),
            pltpu.VMEM((R_WIDTH // PAIR, PAIR, PAIR), F32),
        ],
        compiler_params=_params("parallel", "arbitrary"),
        name="rwkv7",
    )(zb, zb, zb, zb, zb, zb, *consts)


def _pad_cols(w, width):
    return jnp.pad(w, ((0, 0), (0, width - w.shape[1])))


def _w_in_layout_kernel(wt_ref, o_ref):
    wt = wt_ref[0]
    cols = wt.shape[1]
    offs = [0]
    for n in (A_WIDTH, KV_RANK, IDX_HEADS * IDX_DIM, IDX_DIM + IDX_HEADS,
              3 * R_WIDTH, DECAY_LORA, AAA_LORA, GATE_LORA):
        offs.append(offs[-1] + n)
    q, ckv, qi, kiwi, rkv, wl, al, gl = (wt[offs[i]:offs[i + 1], :] for i in range(8))

    def padded(x, height):
        return jnp.concatenate([x, jnp.zeros((height - x.shape[0], cols), x.dtype)], axis=0)

    rows = [q, qi, rkv, gl, ckv, padded(kiwi, LORA_PAD), padded(wl, LORA_PAD), padded(al, LORA_PAD),
            jnp.zeros((Z_WIDTH - Z_END, cols), wt.dtype)]
    o_ref[0] = jnp.concatenate(rows, axis=0).T.astype(BF16)


def _layout_w_in(w_in):
    depth, d, p = w_in.shape
    tc = _divisor_tile(d, 256)
    return pl.pallas_call(
        _w_in_layout_kernel,
        out_shape=jax.ShapeDtypeStruct((depth, d, Z_WIDTH), BF16),
        grid=(depth, d // tc),
        in_specs=[pl.BlockSpec((1, p, tc), lambda l, i: (l, 0, i))],
        out_specs=pl.BlockSpec((1, tc, Z_WIDTH), lambda l, i: (l, i, 0)),
        compiler_params=_params("parallel", "parallel"),
        name="w_in_layout",
    )(jnp.swapaxes(w_in, 1, 2))


def _pad_rows(w, rows):
    return jnp.pad(w, ((0, rows - w.shape[0]), (0, 0)))


def _rwkv_params(l, mu, w0, w2, a0, a2, g2, k_k, k_a, r_k, ln_w, ln_b):
    m = mu[l]
    o = [0, R_WIDTH, 2 * R_WIDTH, 3 * R_WIDTH, 3 * R_WIDTH + DECAY_LORA, 3 * R_WIDTH + DECAY_LORA + AAA_LORA]
    row = lambda x: x.reshape(1, -1)
    return {
        "mu_r": row(m[o[0]:o[1]]), "mu_k": row(m[o[1]:o[2]]), "mu_v": row(m[o[2]:o[3]]),
        "mu_w": _pad_cols(row(m[o[3]:o[4]]), LORA_PAD), "mu_a": _pad_cols(row(m[o[4]:o[5]]), LORA_PAD),
        "mu_g": row(m[o[5]:]),
        "w0": row(w0[l]), "a0": row(a0[l]), "k_k": row(k_k[l]), "k_a": row(k_a[l]), "r_k": row(r_k[l]),
        "ln_w": row(ln_w[l]), "ln_b": row(ln_b[l]),
        "w2": _pad_rows(w2[l], LORA_PAD).astype(BF16), "a2": _pad_rows(a2[l], LORA_PAD).astype(BF16),
        "g2": g2[l].astype(BF16),
    }


def kernel(x, c, t5_bias, ada_w, ada_b, norm_g, ffn_w_in, ffn_w_out, w_in, ckv_norm_g, w_uk, w_uv, rwkv_mu, rwkv_w0, rwkv_w2, rwkv_a0, rwkv_a2, rwkv_g2, rwkv_k_k, rwkv_k_a, rwkv_r_k, rwkv_ln_w, rwkv_ln_b, w_out, final_norm_g):
    b, s, d = x.shape
    depth = ada_w.shape[0]
    assert s % DSA_TILE == 0 and s % RWKV_TILE == 0 and d == A_WIDTH + R_WIDTH
    mod = _ada_mod(c, ada_w, ada_b).reshape(depth, b, N_SUB, 3, 1, d)
    bias = _bias_tiles(t5_bias)
    final_g = final_norm_g.reshape(1, d)
    w_in_z = _layout_w_in(w_in)
    w_out16 = w_out.astype(BF16)
    h = x
    for l in range(depth):
        shift = lambda i: mod[l, :, i, 0]
        scale = lambda i: mod[l, :, i, 1]
        gate = lambda i: mod[l, :, i, 2]
        g = lambda i: norm_g[l, i].reshape(1, d)
        h = _ffn(h, g(0), shift(0), scale(0), gate(0), ffn_w_in, ffn_w_out, l, 0, final_g, final_norm=False)
        za, zb = _proj(h, g(1), shift(1), scale(1), w_in_z, l)
        qt, qit, ckv, ckvt, ki, wit = _dsa_prep(za, zb, ckv_norm_g[l].reshape(1, KV_RANK))
        wuk = jnp.transpose(w_uk[l], (1, 0, 2)).astype(BF16)
        wuvt = jnp.transpose(w_uv[l], (1, 2, 0)).astype(BF16)
        o_a = _dsa(qt, qit, ckv, ckvt, ki, wit, wuk, wuvt, bias)
        o_r = _rwkv(zb, _rwkv_params(l, rwkv_mu, rwkv_w0, rwkv_w2, rwkv_a0, rwkv_a2, rwkv_g2, rwkv_k_k,
                                    rwkv_k_a, rwkv_r_k, rwkv_ln_w, rwkv_ln_b))
        h = _outproj(o_a, o_r, h, gate(1), w_out16, l)
        h = _ffn(h, g(2), shift(2), scale(2), gate(2), ffn_w_in, ffn_w_out, l, 1, final_g,
                 final_norm=(l == depth - 1))
    return h
```

```python
import functools
import math

import jax
import jax.numpy as jnp
from jax import lax
from jax.experimental import pallas as pl
from jax.experimental.pallas import tpu as pltpu

F32 = jnp.float32
BF16 = jnp.bfloat16
I32 = jnp.int32
I16 = jnp.int16

A_HEADS = 8
A_HEAD_DIM = 128
A_WIDTH = A_HEADS * A_HEAD_DIM
KV_RANK = 256
IDX_HEADS = 16
IDX_DIM = 64
TOPK_MAX = 256
REL_BUCKETS = 32
REL_MAX_EXACT = REL_BUCKETS // 2
REL_MAX_DIST = 128
R_HEAD_DIM = 64
R_WIDTH = 1024
DECAY_LORA = 96
AAA_LORA = 96
GATE_LORA = 256
GN_EPS = 64e-5
RMS_EPS = 1e-6
N_SUB = 3

LANES = 128
SUBLANES = 8
VMEM_LIMIT_BYTES = 56 * 1024 * 1024

LORA_PAD = 128
Z_Q = 0
Z_QI = Z_Q + A_WIDTH
Z_R = Z_QI + IDX_HEADS * IDX_DIM
Z_K = Z_R + R_WIDTH
Z_V = Z_K + R_WIDTH
Z_GL = Z_V + R_WIDTH
Z_CKV = Z_GL + GATE_LORA
Z_KIWI = Z_CKV + KV_RANK
Z_WL = Z_KIWI + LORA_PAD
Z_AL = Z_WL + LORA_PAD
Z_END = Z_AL + LORA_PAD
Z_WIDTH = 6144
Z_SPLIT = Z_R

FFN_TOKEN_TILE = 1024
FFN_HIDDEN_TILE = 512
PROLOGUE_ROWS = 256
ADALN_SLAB_ROWS = 16
ADALN_UNROLL = 8
DSA_TILE = 256
DSA_PREP_TILE = 1024
RWKV_CHUNK = 64
RWKV_TILE = 256
PAIR = 2 * R_HEAD_DIM

I16_MIN = -(2 ** 15)
PACKED_ROWS = 2 * SUBLANES
SUM_ROWS = PACKED_ROWS
BF16_EXACT_INT = 256
COUNT_CHAINS = 4
KEY_NEG_INF = -2139095041
MASKED_LOGIT = -(2.0 ** 100)
LOG2E = math.log2(math.e)
FAR_DISTANCE = math.ceil(REL_MAX_EXACT * (REL_MAX_DIST / REL_MAX_EXACT)
                         ** ((REL_BUCKETS - 1 - REL_MAX_EXACT) / (REL_BUCKETS - REL_MAX_EXACT)))


def _dot(a, b):
    return jnp.dot(a, b, preferred_element_type=F32)


def _bdot(a, b):
    return lax.dot_general(a, b, (((2,), (1,)), ((0,), (0,))), preferred_element_type=F32)


def _bdot_nt(a, b):
    return lax.dot_general(a, b, (((2,), (2,)), ((0,), (0,))), preferred_element_type=F32)


def _rms(x, g, eps):
    ms = jnp.mean(x * x, axis=-1, keepdims=True)
    return x * lax.rsqrt(ms + eps) * g


def _divisor_tile(n, pref):
    if n <= pref:
        return n
    t = (pref // LANES) * LANES
    while t > LANES and n % t:
        t -= LANES
    assert n % t == 0, (n, pref)
    return t


def _params(*sem):
    return pltpu.CompilerParams(dimension_semantics=sem, vmem_limit_bytes=VMEM_LIMIT_BYTES)


def _ada_kernel(c_ref, w_ref, b_ref, o_ref):
    c = c_ref[...]
    ca = (c * jax.nn.sigmoid(c)).astype(BF16)
    o_ref[0] = _dot(ca, w_ref[0].astype(BF16)) + b_ref[0]


def _ada_mod(c, ada_w, ada_b):
    depth, d, n = ada_w.shape
    b = c.shape[0]
    bp = -(-b // SUBLANES) * SUBLANES
    cp = jnp.pad(c, ((0, bp - b), (0, 0)))
    tn = _divisor_tile(n, 1024)
    out = pl.pallas_call(
        _ada_kernel,
        out_shape=jax.ShapeDtypeStruct((depth, bp, n), F32),
        grid=(depth, n // tn),
        in_specs=[
            pl.BlockSpec((bp, d), lambda l, j: (0, 0)),
            pl.BlockSpec((1, d, tn), lambda l, j: (l, 0, j)),
            pl.BlockSpec((1, 1, tn), lambda l, j: (l, 0, j)),
        ],
        out_specs=pl.BlockSpec((1, bp, tn), lambda l, j: (l, 0, j)),
        compiler_params=_params("arbitrary", "arbitrary"),
        name="ada_mod",
    )(cp, ada_w, ada_b.reshape(depth, 1, n))
    return out[:, :b]


def _adaln_rows(h_ref, g_ref, scale_ref, shift_ref, hn_ref, base, copy_ref=None):
    rows = h_ref.shape[1]
    g = g_ref[...]
    mul = 1.0 + scale_ref[0]
    add = shift_ref[0]

    def slab(i, carry):
        src = pl.ds(pl.multiple_of(i * ADALN_SLAB_ROWS, ADALN_SLAB_ROWS), ADALN_SLAB_ROWS)
        dst = pl.ds(pl.multiple_of(base + i * ADALN_SLAB_ROWS, ADALN_SLAB_ROWS), ADALN_SLAB_ROWS)
        h = h_ref[0, src, :]
        hn_ref[dst, :] = (_rms(h, g, RMS_EPS) * mul + add).astype(BF16)
        if copy_ref is not None:
            copy_ref[0, dst, :] = h
        return carry

    lax.fori_loop(0, rows // ADALN_SLAB_ROWS, slab, 0, unroll=ADALN_UNROLL)


def _prologue_split(tm):
    rows = _divisor_tile(tm, PROLOGUE_ROWS)
    return rows, tm // rows


def _ffn_kernel(h_ref, g_ref, shift_ref, scale_ref, gate_ref, wg_ref, wu_ref, wo_ref, fg_ref,
                o_ref, hn_ref, *, final_norm, pro_steps):
    f = pl.program_id(2)

    @pl.when(f < pro_steps)
    def _():
        _adaln_rows(h_ref, g_ref, scale_ref, shift_ref, hn_ref, f * h_ref.shape[1], o_ref)

    @pl.when(f >= pro_steps)
    def _():
        hn = hn_ref[...]
        g = _dot(hn, wg_ref[...].astype(BF16))
        u = _dot(hn, wu_ref[...].astype(BF16))
        act = (g * jax.nn.sigmoid(g) * u).astype(BF16)
        o_ref[0] += (0.5 * gate_ref[0]) * _dot(act, wo_ref[...].astype(BF16))

    if final_norm:
        @pl.when(f == pl.num_programs(2) - 1)
        def _():
            o_ref[0] = _rms(o_ref[0], fg_ref[...], RMS_EPS)


def _ffn(h, g, shift, scale, gate, w_in, w_out, layer, which, final_g, *, final_norm):
    b, s, d = h.shape
    ff = w_out.shape[2]
    tm = _divisor_tile(s, FFN_TOKEN_TILE)
    tf = _divisor_tile(ff, FFN_HIDDEN_TILE)
    nf = ff // tf
    hr, pro = _prologue_split(tm)
    vec = pl.BlockSpec((1, 1, d), lambda bi, i, f: (bi, 0, 0))
    row = pl.BlockSpec((1, d), lambda bi, i, f: (0, 0))

    def wf(f):
        return jnp.maximum(f - pro, 0)

    return pl.pallas_call(
        functools.partial(_ffn_kernel, final_norm=final_norm, pro_steps=pro),
        out_shape=jax.ShapeDtypeStruct((b, s, d), F32),
        grid=(b, s // tm, pro + nf),
        in_specs=[
            pl.BlockSpec((1, hr, d), lambda bi, i, f: (bi, i * pro + jnp.minimum(f, pro - 1), 0)),
            row, vec, vec, vec,
            pl.BlockSpec((None, None, d, tf), lambda bi, i, f: (layer, which, 0, wf(f))),
            pl.BlockSpec((None, None, d, tf), lambda bi, i, f: (layer, which, 0, nf + wf(f))),
            pl.BlockSpec((None, None, tf, d), lambda bi, i, f: (layer, which, wf(f), 0)),
            row,
        ],
        out_specs=pl.BlockSpec((1, tm, d), lambda bi, i, f: (bi, i, 0)),
        scratch_shapes=[pltpu.VMEM((tm, d), BF16)],
        compiler_params=_params("parallel", "parallel", "arbitrary"),
        name="ffn",
    )(h, g, shift, scale, gate, w_in, w_in, w_out, final_g)


def _proj_kernel(h_ref, g_ref, shift_ref, scale_ref, w_ref, oa_ref, ob_ref, hn_ref, *, attn_tiles):
    n = pl.program_id(2)

    @pl.when(n == 0)
    def _():
        _adaln_rows(h_ref, g_ref, scale_ref, shift_ref, hn_ref, 0)

    z = _dot(hn_ref[...], w_ref[...])
    ob_ref[0] = z

    @pl.when(n < attn_tiles)
    def _():
        oa_ref[0] = z.astype(BF16)


def _proj(h, g, shift, scale, w, layer):
    b, s, d = h.shape
    p = w.shape[2]
    tm = _divisor_tile(s, 1024)
    tn = _divisor_tile(Z_SPLIT, 1024)
    assert Z_SPLIT % tn == 0 and (p - Z_SPLIT) % tn == 0
    na = Z_SPLIT // tn
    vec = pl.BlockSpec((1, 1, d), lambda bi, i, n: (bi, 0, 0))
    return pl.pallas_call(
        functools.partial(_proj_kernel, attn_tiles=na),
        out_shape=(jax.ShapeDtypeStruct((b, s, Z_SPLIT), BF16), jax.ShapeDtypeStruct((b, s, p - Z_SPLIT), F32)),
        grid=(b, s // tm, p // tn),
        in_specs=[
            pl.BlockSpec((1, tm, d), lambda bi, i, n: (bi, i, 0)),
            pl.BlockSpec((1, d), lambda bi, i, n: (0, 0)),
            vec, vec,
            pl.BlockSpec((None, d, tn), lambda bi, i, n: (layer, 0, n)),
        ],
        out_specs=(pl.BlockSpec((1, tm, tn), lambda bi, i, n: (bi, i, jnp.minimum(n, na - 1))),
                   pl.BlockSpec((1, tm, tn), lambda bi, i, n: (bi, i, jnp.maximum(n - na, 0)))),
        scratch_shapes=[pltpu.VMEM((tm, d), BF16)],
        compiler_params=_params("parallel", "parallel", "arbitrary"),
        name="proj",
    )(h, g, shift, scale, w)


def _outproj_kernel(oa_ref, or_ref, h_ref, gate_ref, wa_ref, wr_ref, o_ref):
    acc = _dot(oa_ref[0], wa_ref[...]) + _dot(or_ref[0], wr_ref[...])
    o_ref[0] = h_ref[0] + gate_ref[0] * acc


def _outproj(o_a, o_r, h, gate, w_out, layer):
    b, s, d = h.shape
    tm = _divisor_tile(s, 512)
    tn = d
    wa = o_a.shape[-1]
    wr = o_r.shape[-1]
    return pl.pallas_call(
        _outproj_kernel,
        out_shape=jax.ShapeDtypeStruct((b, s, d), F32),
        grid=(b, s // tm, d // tn),
        in_specs=[
            pl.BlockSpec((1, tm, wa), lambda bi, i, n: (bi, i, 0)),
            pl.BlockSpec((1, tm, wr), lambda bi, i, n: (bi, i, 0)),
            pl.BlockSpec((1, tm, tn), lambda bi, i, n: (bi, i, n)),
            pl.BlockSpec((1, 1, tn), lambda bi, i, n: (bi, 0, n)),
            pl.BlockSpec((None, wa, tn), lambda bi, i, n: (layer, 0, n)),
            pl.BlockSpec((None, wr, tn), lambda bi, i, n: (layer, wa // wr, n)),
        ],
        out_specs=pl.BlockSpec((1, tm, tn), lambda bi, i, n: (bi, i, n)),
        compiler_params=_params("parallel", "parallel", "arbitrary"),
        name="outproj",
    )(o_a, o_r, h, gate, w_out, w_out)


def _dsa_prep_kernel(q_ref, qi_ref, ckv_ref, kiwi_ref, g_ref, qt_ref, qit_ref, ckv_o, ckvt_o, ki_o, wit_o):
    qt_ref[0] = q_ref[0].T
    qit_ref[0] = qi_ref[0].T
    cn = _rms(ckv_ref[0], g_ref[...], RMS_EPS)
    ckv_o[0] = cn.astype(BF16)
    t = DSA_TILE
    ones = jnp.ones((SUM_ROWS, t), BF16)
    for j in range(cn.shape[0] // t):
        ckvt_o[0, j] = jnp.concatenate([cn[j * t:(j + 1) * t].T.astype(BF16), ones], axis=0)
    kw = kiwi_ref[0]
    ki_o[0] = kw[:, :IDX_DIM].astype(BF16)
    wit_o[0] = kw.T[IDX_DIM:IDX_DIM + IDX_HEADS, :] * (IDX_HEADS * IDX_DIM) ** -0.5


def _dsa_prep(za, zb, ckv_g):
    b, s, _ = za.shape
    t = DSA_TILE
    nt = s // t
    tp = _divisor_tile(s, DSA_PREP_TILE)
    assert tp % t == 0

    def seg(width, off):
        return pl.BlockSpec((1, tp, width), lambda bi, i: (bi, i, (off if off < Z_SPLIT else off - Z_SPLIT) // width))

    return pl.pallas_call(
        _dsa_prep_kernel,
        out_shape=(
            jax.ShapeDtypeStruct((b, A_WIDTH, s), BF16),
            jax.ShapeDtypeStruct((b, IDX_HEADS * IDX_DIM, s), BF16),
            jax.ShapeDtypeStruct((b, s, KV_RANK), BF16),
            jax.ShapeDtypeStruct((b, nt, KV_RANK + SUM_ROWS, t), BF16),
            jax.ShapeDtypeStruct((b, s, IDX_DIM), BF16),
            jax.ShapeDtypeStruct((b, IDX_HEADS, s), F32),
        ),
        grid=(b, s // tp),
        in_specs=[
            seg(A_WIDTH, Z_Q), seg(IDX_HEADS * IDX_DIM, Z_QI), seg(KV_RANK, Z_CKV), seg(LORA_PAD, Z_KIWI),
            pl.BlockSpec((1, KV_RANK), lambda bi, i: (0, 0)),
        ],
        out_specs=(
            pl.BlockSpec((1, A_WIDTH, tp), lambda bi, i: (bi, 0, i)),
            pl.BlockSpec((1, IDX_HEADS * IDX_DIM, tp), lambda bi, i: (bi, 0, i)),
            pl.BlockSpec((1, tp, KV_RANK), lambda bi, i: (bi, i, 0)),
            pl.BlockSpec((1, tp // t, KV_RANK + SUM_ROWS, t), lambda bi, i: (bi, i, 0, 0)),
            pl.BlockSpec((1, tp, IDX_DIM), lambda bi, i: (bi, i, 0)),
            pl.BlockSpec((1, IDX_HEADS, tp), lambda bi, i: (bi, 0, i)),
        ),
        compiler_params=_params("parallel", "parallel"),
        name="dsa_prep",
    )(za, za, zb, zb, ckv_g)


def _bias_kernel(t5_ref, o_ref):
    t = DSA_TILE
    j = lax.broadcasted_iota(I32, (t, t), 0)
    i = lax.broadcasted_iota(I32, (t, t), 1)
    for didx in range(2):
        n = jnp.maximum(didx * t + i - j, 0)
        nf = jnp.maximum(n, 1).astype(F32)
        large = REL_MAX_EXACT + (jnp.log(nf / REL_MAX_EXACT) / math.log(REL_MAX_DIST / REL_MAX_EXACT)
                                 * (REL_BUCKETS - REL_MAX_EXACT)).astype(I32)
        large = jnp.minimum(large, REL_BUCKETS - 1)
        bucket = jnp.where(n < REL_MAX_EXACT, n, large)
        for h in range(A_HEADS):
            val = jnp.zeros((t, t), F32)
            for k in range(REL_BUCKETS):
                val = jnp.where(bucket == k, t5_ref[k, h], val)
            o_ref[(1 - didx) * t:(2 - didx) * t, h * t:(h + 1) * t] = (val - t5_ref[REL_BUCKETS - 1, h]) * LOG2E


def _bias_tiles(t5_bias):
    t = DSA_TILE
    assert t + 1 >= FAR_DISTANCE
    return pl.pallas_call(
        _bias_kernel,
        out_shape=jax.ShapeDtypeStruct((2 * t, A_HEADS * t), F32),
        in_specs=[pl.BlockSpec(memory_space=pltpu.SMEM)],
        out_specs=pl.BlockSpec(memory_space=pltpu.VMEM),
        compiler_params=pltpu.CompilerParams(vmem_limit_bytes=VMEM_LIMIT_BYTES),
        name="t5_bias_tiles",
    )(t5_bias)


def _dsa_kernel(qt_ref, qit_ref, ckv_ref, ckvt_ref, ki_ref, wit_ref, wuk_ref, wuvt_ref, bias_ref,
                o_ref, keys_ref, khi_ref, klo_ref, acc_ref, ot_ref, *, topk, seq_len):
    t = DSA_TILE
    qb = pl.program_id(1)
    nk = qb + 1
    row = lax.broadcasted_iota(I32, (t, t), 0)
    col = lax.broadcasted_iota(I32, (t, t), 1)

    def idx_rows(start, rows, diagonal):
        sl = pl.ds(pl.multiple_of(start, rows), rows)
        kic = ki_ref[0, sl, :]
        acc = jnp.zeros((rows, t), F32)
        for h in range(IDX_HEADS):
            rel = _dot(kic, qit_ref[0, h * IDX_DIM:(h + 1) * IDX_DIM, :])
            acc = acc + jnp.maximum(rel, 0.0) * wit_ref[0, h:h + 1, :]
        acc = acc + 0.0
        bits = pltpu.bitcast(acc, I32)
        key = bits ^ ((bits >> 31) & 0x7FFFFFFF)
        if diagonal:
            key = jnp.where(row <= col, key, KEY_NEG_INF)
        keys_ref[sl, :] = key
        khi_ref[sl, :] = (key >> 16).astype(I16)
        klo_ref[sl, :] = ((key & 0xFFFF) + I16_MIN).astype(I16)

    def idx_group(chunks):
        def body(i, first_chunk):
            idx_rows(first_chunk * t, chunks * t, False)
            return first_chunk + chunks
        return body

    done = lax.fori_loop(0, qb // 8, idx_group(8), 0)
    done = lax.fori_loop(0, (qb // 4) % 2, idx_group(4), done)
    done = lax.fori_loop(0, (qb // 2) % 2, idx_group(2), done)
    lax.fori_loop(0, qb % 2, idx_group(1), done)
    idx_rows(qb * t, t, True)

    def count16(ref, cand):
        cand16 = cand.astype(I16)

        def body(kc, accs):
            x = ref[pl.ds(pl.multiple_of(kc * t, t), t), :]
            ones = jnp.where(x >= cand16, jnp.asarray(1, BF16), jnp.asarray(0, BF16))
            ones = ones.reshape(t // PACKED_ROWS, PACKED_ROWS, t)
            accs = list(accs)
            for i in range(t // PACKED_ROWS):
                accs[i % COUNT_CHAINS] = accs[i % COUNT_CHAINS] + ones[i]
            return tuple(accs)

        accs = (jnp.zeros((PACKED_ROWS, t), BF16),) * COUNT_CHAINS
        accs = lax.fori_loop(0, nk // 2, lambda i, a: body(2 * i + 1, body(2 * i, a)), accs)
        accs = lax.fori_loop(2 * (nk // 2), nk, body, accs)
        total = sum(a.astype(F32) for a in accs)
        return jnp.sum(total, axis=0, keepdims=True).astype(I32)

    def search16(ref, base):
        def accept(cand, carry):
            cur, cur_count, next_count = carry
            n = base + count16(ref, cand)
            ok = n >= topk
            return jnp.where(ok, cand, cur), jnp.where(ok, n, cur_count), jnp.where(ok, next_count, n)

        everything = jnp.zeros((1, t), I32) + base + nk * t
        start = (jnp.full((1, t), I16_MIN, I32), everything, jnp.zeros((1, t), I32))
        carry = accept(jnp.zeros((1, t), I32), start)
        return lax.fori_loop(0, 15, lambda i, c: accept(c[0] | (jnp.int32(1) << (14 - i)), c), carry)

    hi, n_hi, above = search16(khi_ref, 0)
    hi16 = hi.astype(I16)

    def keep_low(kc, carry):
        sl = pl.ds(pl.multiple_of(kc * t, t), t)
        klo_ref[sl, :] = jnp.where(khi_ref[sl, :] == hi16, klo_ref[sl, :], jnp.asarray(I16_MIN, I16))
        return carry

    lax.fori_loop(0, nk, keep_low, 0)
    lo, n_lo, _ = search16(klo_ref, above)
    thr = jnp.maximum((hi << 16) | (lo - I16_MIN), KEY_NEG_INF + 1)
    n_selected = jnp.where(lo == I16_MIN, n_hi, n_lo)

    def count(pred):
        def body(kc, acc):
            k = keys_ref[pl.ds(pl.multiple_of(kc * t, t), t), :]
            m = jnp.where(pred(k, kc), 1, 0).astype(I32)
            return acc + jnp.sum(m.reshape(t // SUBLANES, SUBLANES, t), axis=0)

        acc = lax.fori_loop(0, nk, body, jnp.zeros((SUBLANES, t), I32))
        return jnp.sum(acc, axis=0, keepdims=True)

    @pl.when(jnp.max(n_selected) > topk)
    def _():
        need = topk - count(lambda k, kc: k > thr)
        nbits = (seq_len - 1).bit_length()

        def pos_bit(i, y):
            cand = y | (jnp.int32(1) << (nbits - 1 - i))
            before = count(lambda k, kc: (k == thr) & ((kc * t + row) < cand))
            return jnp.where(before < need, cand, y)

        last_kept = lax.fori_loop(0, nbits, pos_bit, jnp.zeros((1, t), I32))

        def demote(kc, carry):
            sl = pl.ds(pl.multiple_of(kc * t, t), t)
            k = keys_ref[sl, :]
            keys_ref[sl, :] = jnp.where((k == thr) & ((kc * t + row) > last_kept), k - 1, k)
            return carry

        lax.fori_loop(0, nk, demote, 0)

    scale = A_HEAD_DIM ** -0.5 * LOG2E

    hw = A_HEADS * t
    qlat = jnp.concatenate(
        [(_dot(wuk_ref[h], qt_ref[0, h * A_HEAD_DIM:(h + 1) * A_HEAD_DIM, :]) * scale).astype(BF16)
         for h in range(A_HEADS)], axis=1)
    acc_ref[...] = jnp.zeros_like(acc_ref)

    def att_chunks(kc, m, *, chunks, bias_row=None):
        rows = chunks * t
        sl = pl.ds(pl.multiple_of(kc * t, t), rows)
        raw = _dot(ckv_ref[0, sl, :], qlat)
        sel = keys_ref[sl, :] >= thr
        parts = []
        for h in range(A_HEADS):
            lg_h = raw[:, h * t:(h + 1) * t]
            if bias_row is not None:
                lg_h = lg_h + bias_ref[bias_row:bias_row + rows, h * t:(h + 1) * t]
            parts.append(jnp.where(sel, lg_h, MASKED_LOGIT).astype(BF16))
        lg = jnp.concatenate(parts, axis=1)
        m_new = jnp.maximum(m, jnp.max(lg, axis=0, keepdims=True).astype(F32))
        p = jnp.exp2(lg - m_new.astype(BF16))
        acc = acc_ref[...] * jnp.exp2(m - m_new)
        for j in range(chunks):
            acc = acc + _dot(ckvt_ref[0, kc + j], p[j * t:(j + 1) * t])
        acc_ref[...] = acc
        return m_new

    n_far = jnp.maximum(qb - 1, 0)
    has_prev = jnp.minimum(qb, 1)
    m = jnp.full((1, hw), MASKED_LOGIT, F32)
    def far_group(chunks):
        def body(i, carry):
            first_chunk, m = carry
            return first_chunk + chunks, att_chunks(first_chunk, m, chunks=chunks)
        return body

    carry = lax.fori_loop(0, n_far // 4, far_group(4), (0, m))
    carry = lax.fori_loop(0, (n_far // 2) % 2, far_group(2), carry)
    _, m = lax.fori_loop(0, n_far % 2, far_group(1), carry)
    m = lax.fori_loop(0, has_prev, lambda i, m: att_chunks(qb - 1, m, chunks=2, bias_row=0), m)
    lax.fori_loop(0, 1 - has_prev, lambda i, m: att_chunks(qb, m, chunks=1, bias_row=t), m)
    olat = (acc_ref[:KV_RANK, :] / acc_ref[KV_RANK:KV_RANK + 1, :]).astype(BF16)
    for h in range(A_HEADS):
        ot_ref[h * A_HEAD_DIM:(h + 1) * A_HEAD_DIM, :] = _dot(wuvt_ref[h], olat[:, h * t:(h + 1) * t])
    o_ref[0] = ot_ref[...].T.astype(BF16)


def _dsa(qt, qit, ckv, ckvt, ki, wit, wuk, wuvt, bias):
    b, _, s = qt.shape
    t = DSA_TILE
    nt = s // t
    topk = min(TOPK_MAX, s // 4)
    assert s // PACKED_ROWS <= BF16_EXACT_INT
    return pl.pallas_call(
        functools.partial(_dsa_kernel, topk=topk, seq_len=s),
        out_shape=jax.ShapeDtypeStruct((b, s, A_WIDTH), BF16),
        grid=(b, nt),
        in_specs=[
            pl.BlockSpec((1, A_WIDTH, t), lambda bi, i: (bi, 0, i)),
            pl.BlockSpec((1, IDX_HEADS * IDX_DIM, t), lambda bi, i: (bi, 0, i)),
            pl.BlockSpec((1, s, KV_RANK), lambda bi, i: (bi, 0, 0)),
            pl.BlockSpec((1, nt, KV_RANK + SUM_ROWS, t), lambda bi, i: (bi, 0, 0, 0)),
            pl.BlockSpec((1, s, IDX_DIM), lambda bi, i: (bi, 0, 0)),
            pl.BlockSpec((1, IDX_HEADS, t), lambda bi, i: (bi, 0, i)),
            pl.BlockSpec((A_HEADS, KV_RANK, A_HEAD_DIM), lambda bi, i: (0, 0, 0)),
            pl.BlockSpec((A_HEADS, A_HEAD_DIM, KV_RANK), lambda bi, i: (0, 0, 0)),
            pl.BlockSpec((2 * t, A_HEADS * t), lambda bi, i: (0, 0)),
        ],
        out_specs=pl.BlockSpec((1, t, A_WIDTH), lambda bi, i: (bi, i, 0)),
        scratch_shapes=[pltpu.VMEM((s, t), I32), pltpu.VMEM((s, t), I16), pltpu.VMEM((s, t), I16),
                        pltpu.VMEM((KV_RANK + SUM_ROWS, A_HEADS * t), F32), pltpu.VMEM((A_WIDTH, t), F32)],
        compiler_params=_params("parallel", "arbitrary"),
        name="dsa",
    )(qt, qit, ckv, ckvt, ki, wit, wuk, wuvt, bias)


def _split3(x):
    hi = x.astype(BF16)
    r1 = x - hi.astype(F32)
    mid = r1.astype(BF16)
    lo = (r1 - mid.astype(F32)).astype(BF16)
    return hi, mid, lo


def _rwkv_kernel(r_ref, k_ref, v_ref, gl_ref, wl_ref, al_ref,
                 mur_ref, muk_ref, muv_ref, mug_ref, muw_ref, mua_ref,
                 w0_ref, a0_ref, kk_ref, ka_ref, rk_ref, lnw_ref, lnb_ref,
                 w2_ref, a2_ref, g2_ref,
                 o_ref,
                 pr_ref, pk_ref, pv_ref, pg_ref, pw_ref, pa_ref, state_ref):
    tl = RWKV_TILE
    c = RWKV_CHUNK
    nc = tl // c
    npair = R_WIDTH // PAIR

    @pl.when(pl.program_id(1) == 0)
    def _():
        state_ref[...] = jnp.zeros_like(state_ref)
        for ref in (pr_ref, pk_ref, pv_ref, pg_ref, pw_ref, pa_ref):
            ref[...] = jnp.zeros_like(ref)

    def shift_mix(x_ref, prev_ref, mu_ref):
        x = x_ref[0]
        first = lax.broadcasted_iota(I32, x.shape, 0) == 0
        xprev = jnp.where(first, prev_ref[...], pltpu.roll(x, 1, axis=0))
        prev_ref[...] = x[tl - 1:tl, :]
        return x + (xprev - x) * mu_ref[...]

    r = shift_mix(r_ref, pr_ref, mur_ref)
    k = shift_mix(k_ref, pk_ref, muk_ref)
    v = shift_mix(v_ref, pv_ref, muv_ref)
    gl = shift_mix(gl_ref, pg_ref, mug_ref)
    wl = shift_mix(wl_ref, pw_ref, muw_ref)
    al = shift_mix(al_ref, pa_ref, mua_ref)

    w_lin = w0_ref[...] + _dot(jnp.tanh(wl).astype(BF16), w2_ref[...])
    nx = -w_lin
    softplus = jnp.maximum(nx, 0.0) + jnp.log(1.0 + jnp.exp(-jnp.abs(nx)))
    ld = -jnp.exp(-softplus - 0.5)
    a = jax.nn.sigmoid(a0_ref[...] + _dot(al.astype(BF16), a2_ref[...]))
    gate = _dot(jax.nn.sigmoid(gl).astype(BF16), g2_ref[...])

    li = lax.broadcasted_iota(I32, (LANES, LANES), 0) // R_HEAD_DIM
    lj = lax.broadcasted_iota(I32, (LANES, LANES), 1) // R_HEAD_DIM
    head_ones = jnp.where(li == lj, 1.0, 0.0).astype(BF16)

    def head_sum(x):
        parts = []
        for j in range(R_WIDTH // LANES):
            hi, mid, lo = _split3(x[:, j * LANES:(j + 1) * LANES])
            parts.append(_dot(hi, head_ones) + _dot(mid, head_ones) + _dot(lo, head_ones))
        return jnp.concatenate(parts, axis=-1)

    kk = k * kk_ref[...]
    kk = kk / jnp.maximum(jnp.sqrt(head_sum(kk * kk)), 1e-12)
    k2 = k * (1.0 + (a - 1.0) * ka_ref[...])
    bb = kk * a

    ti = lax.broadcasted_iota(I32, (tl, tl), 0)
    tj = lax.broadcasted_iota(I32, (tl, tl), 1)
    tri = jnp.where((ti // c == tj // c) & (tj <= ti), 1.0, 0.0).astype(BF16)
    hi, mid, lo = _split3(ld)
    cum = _dot(tri, hi) + _dot(tri, mid) + _dot(tri, lo)
    cum_end = jnp.broadcast_to(cum.reshape(nc, c, R_WIDTH)[:, c - 1:c, :], (nc, c, R_WIDTH)).reshape(tl, R_WIDTH)
    e_in = jnp.exp(cum)
    e_neg = jnp.exp(-cum)
    e_out = jnp.exp(cum_end - cum)

    nb = nc * npair

    def to_pairs(x):
        x3 = x.reshape(nc, c, R_WIDTH)
        xs = jnp.stack([x3[:, :, p * PAIR:(p + 1) * PAIR] for p in range(npair)], axis=1)
        return xs.reshape(nb, c, PAIR)

    lane_head = lax.broadcasted_iota(I32, (2 * c, PAIR), 1) // R_HEAD_DIM
    row_head = lax.broadcasted_iota(I32, (2 * c, PAIR), 0) // c
    same = lane_head == row_head
    rt = lax.broadcasted_iota(I32, (2 * c, PAIR), 0) % c
    ct = lax.broadcasted_iota(I32, (2 * c, PAIR), 1) % c
    strict = same & (ct < rt)
    incl = same & (ct <= rt)
    eye = jnp.where(same & (ct == rt), 1.0, 0.0).astype(F32)

    def bd(x):
        return jnp.where(same, jnp.concatenate([x, x], axis=1), 0.0)

    lane0 = lax.broadcasted_iota(I32, (c, PAIR), 1) < R_HEAD_DIM

    abar = bd(to_pairs(-kk * jnp.exp(cum - ld))).astype(BF16)
    rbar = bd(to_pairs(r * e_in)).astype(BF16)
    bt = bd(to_pairs(bb * e_neg)).astype(BF16)
    kt = bd(to_pairs(k2 * e_neg)).astype(BF16)
    bk = jnp.concatenate([bd(to_pairs(bb * e_out)), bd(to_pairs(k2 * e_out))], axis=1).astype(BF16)
    v_pl = to_pairs(v)
    vbd = bd(v_pl)
    decay_end = to_pairs(jnp.exp(cum_end))[:, 0:1, :]
    aa = _bdot_nt(jnp.concatenate([abar, rbar], axis=1), jnp.concatenate([bt, kt], axis=1))
    a_ab = jnp.where(strict, aa[:, :2 * c, :PAIR], 0.0)
    a_ak = jnp.where(strict, aa[:, :2 * c, PAIR:], 0.0).astype(BF16)
    a_rb = jnp.where(incl, aa[:, 2 * c:, :PAIR], 0.0).astype(BF16)
    a_rk = jnp.where(incl, aa[:, 2 * c:, PAIR:], 0.0).astype(BF16)
    pw = a_ab.astype(BF16)
    tinv = eye + a_ab
    for _ in range(int(math.log2(c)) - 1):
        pw = _bdot(pw, pw).astype(BF16)
        tinv = tinv + _bdot(tinv.astype(BF16), pw)
    akv = _bdot(a_ak, vbd.astype(BF16))
    wu = _bdot(tinv.astype(BF16), jnp.concatenate([abar, akv.astype(BF16)], axis=2))
    w16 = wu[:, :, :PAIR].astype(BF16)
    u_bd = wu[:, :, PAIR:]

    s_bd = state_ref[...]
    e_parts, rs_parts = [], []
    for ci in range(nc):
        sl = slice(ci * npair, (ci + 1) * npair)
        wr = _bdot_nt(jnp.concatenate([w16[sl], rbar[sl]], axis=1), s_bd.astype(BF16))
        e_bd = wr[:, :2 * c] + u_bd[sl]
        e_parts.append(e_bd)
        rs_parts.append(wr[:, 2 * c:])
        ev = jnp.concatenate([e_bd, vbd[sl]], axis=1)
        ds = _bdot(jnp.swapaxes(ev, 1, 2).astype(BF16), bk[sl])
        s_bd = s_bd * decay_end[sl] + jnp.where(same, ds, 0.0)
    state_ref[...] = s_bd

    ev_all = jnp.concatenate([jnp.concatenate(e_parts, axis=0), vbd], axis=1).astype(BF16)
    y_bd = jnp.concatenate(rs_parts, axis=0) + _bdot(jnp.concatenate([a_rb, a_rk], axis=2), ev_all)
    mean = jnp.sum(y_bd, axis=-1, keepdims=True) * (1.0 / R_HEAD_DIM)
    dev = jnp.where(same, y_bd - mean, 0.0)
    var = jnp.sum(dev * dev, axis=-1, keepdims=True) * (1.0 / R_HEAD_DIM)
    yn = dev * lax.rsqrt(var + GN_EPS)
    yn = yn[:, :c] + yn[:, c:]
    rkv = to_pairs(r * k2 * rk_ref[...])
    s0 = jnp.sum(jnp.where(lane0, rkv, 0.0), axis=-1, keepdims=True)
    s1 = jnp.sum(jnp.where(lane0, 0.0, rkv), axis=-1, keepdims=True)
    bonus = jnp.where(lane0, s0, s1) * v_pl
    def per_pair(row_ref):
        rows = [row_ref[:, p * PAIR:(p + 1) * PAIR] for p in range(npair)]
        return jnp.stack(rows * nc, axis=0)

    out = ((yn * per_pair(lnw_ref) + per_pair(lnb_ref) + bonus) * to_pairs(gate)).astype(BF16)
    for ci in range(nc):
        for p in range(npair):
            o_ref[0, ci * c:(ci + 1) * c, p * PAIR:(p + 1) * PAIR] = out[ci * npair + p]


def _rwkv(zb, prm):
    b, s, _ = zb.shape
    tl = RWKV_TILE

    def seg(width, off):
        return pl.BlockSpec((1, tl, width), lambda bi, i: (bi, i, (off - Z_SPLIT) // width))

    def full(arr):
        return pl.BlockSpec(arr.shape, lambda bi, i: (0,) * arr.ndim)

    names = ("mu_r", "mu_k", "mu_v", "mu_g", "mu_w", "mu_a", "w0", "a0", "k_k", "k_a", "r_k", "ln_w", "ln_b",
             "w2", "a2", "g2")
    consts = [prm[n] for n in names]
    return pl.pallas_call(
        _rwkv_kernel,
        out_shape=jax.ShapeDtypeStruct((b, s, R_WIDTH), BF16),
        grid=(b, s // tl),
        in_specs=[seg(R_WIDTH, Z_R), seg(R_WIDTH, Z_K), seg(R_WIDTH, Z_V), seg(GATE_LORA, Z_GL),
                  seg(LORA_PAD, Z_WL), seg(LORA_PAD, Z_AL)] + [full(x) for x in consts],
        out_specs=pl.BlockSpec((1, tl, R_WIDTH), lambda bi, i: (bi, i, 0)),
        scratch_shapes=[
            pltpu.VMEM((1, R_WIDTH), F32), pltpu.VMEM((1, R_WIDTH), F32), pltpu.VMEM((1, R_WIDTH), F32),
            pltpu.VMEM((1, GATE_LORA), F32), pltpu.VMEM((1, LORA_PAD), F32), pltpu.VMEM((1, LORA_PAD), F32),
            pltpu.VMEM((R_WIDTH // PAIR, PAIR, PAIR), F32),
        ],
        compiler_params=_params("parallel", "arbitrary"),
        name="rwkv7",
    )(zb, zb, zb, zb, zb, zb, *consts)


def _pad_cols(w, width):
    return jnp.pad(w, ((0, 0), (0, width - w.shape[1])))


def _w_in_layout_kernel(wt_ref, o_ref):
    wt = wt_ref[0]
    cols = wt.shape[1]
    offs = [0]
    for n in (A_WIDTH, KV_RANK, IDX_HEADS * IDX_DIM, IDX_DIM + IDX_HEADS,
              3 * R_WIDTH, DECAY_LORA, AAA_LORA, GATE_LORA):
        offs.append(offs[-1] + n)
    q, ckv, qi, kiwi, rkv, wl, al, gl = (wt[offs[i]:offs[i + 1], :] for i in range(8))

    def padded(x, height):
        return jnp.concatenate([x, jnp.zeros((height - x.shape[0], cols), x.dtype)], axis=0)

    rows = [q, qi, rkv, gl, ckv, padded(kiwi, LORA_PAD), padded(wl, LORA_PAD), padded(al, LORA_PAD),
            jnp.zeros((Z_WIDTH - Z_END, cols), wt.dtype)]
    o_ref[0] = jnp.concatenate(rows, axis=0).T.astype(BF16)


def _layout_w_in(w_in):
    depth, d, p = w_in.shape
    tc = _divisor_tile(d, 256)
    return pl.pallas_call(
        _w_in_layout_kernel,
        out_shape=jax.ShapeDtypeStruct((depth, d, Z_WIDTH), BF16),
        grid=(depth, d // tc),
        in_specs=[pl.BlockSpec((1, p, tc), lambda l, i: (l, 0, i))],
        out_specs=pl.BlockSpec((1, tc, Z_WIDTH), lambda l, i: (l, i, 0)),
        compiler_params=_params("parallel", "parallel"),
        name="w_in_layout",
    )(jnp.swapaxes(w_in, 1, 2))


def _pad_rows(w, rows):
    return jnp.pad(w, ((0, rows - w.shape[0]), (0, 0)))


def _rwkv_params(l, mu, w0, w2, a0, a2, g2, k_k, k_a, r_k, ln_w, ln_b):
    m = mu[l]
    o = [0, R_WIDTH, 2 * R_WIDTH, 3 * R_WIDTH, 3 * R_WIDTH + DECAY_LORA, 3 * R_WIDTH + DECAY_LORA + AAA_LORA]
    row = lambda x: x.reshape(1, -1)
    return {
        "mu_r": row(m[o[0]:o[1]]), "mu_k": row(m[o[1]:o[2]]), "mu_v": row(m[o[2]:o[3]]),
        "mu_w": _pad_cols(row(m[o[3]:o[4]]), LORA_PAD), "mu_a": _pad_cols(row(m[o[4]:o[5]]), LORA_PAD),
        "mu_g": row(m[o[5]:]),
        "w0": row(w0[l]), "a0": row(a0[l]), "k_k": row(k_k[l]), "k_a": row(k_a[l]), "r_k": row(r_k[l]),
        "ln_w": row(ln_w[l]), "ln_b": row(ln_b[l]),
        "w2": _pad_rows(w2[l], LORA_PAD).astype(BF16), "a2": _pad_rows(a2[l], LORA_PAD).astype(BF16),
        "g2": g2[l].astype(BF16),
    }


def kernel(x, c, t5_bias, ada_w, ada_b, norm_g, ffn_w_in, ffn_w_out, w_in, ckv_norm_g, w_uk, w_uv, rwkv_mu, rwkv_w0, rwkv_w2, rwkv_a0, rwkv_a2, rwkv_g2, rwkv_k_k, rwkv_k_a, rwkv_r_k, rwkv_ln_w, rwkv_ln_b, w_out, final_norm_g):
    b, s, d = x.shape
    depth = ada_w.shape[0]
    assert s % DSA_TILE == 0 and s % RWKV_TILE == 0 and d == A_WIDTH + R_WIDTH
    mod = _ada_mod(c, ada_w, ada_b).reshape(depth, b, N_SUB, 3, 1, d)
    bias = _bias_tiles(t5_bias)
    final_g = final_norm_g.reshape(1, d)
    w_in_z = _layout_w_in(w_in)
    w_out16 = w_out.astype(BF16)
    h = x
    for l in range(depth):
        shift = lambda i: mod[l, :, i, 0]
        scale = lambda i: mod[l, :, i, 1]
        gate = lambda i: mod[l, :, i, 2]
        g = lambda i: norm_g[l, i].reshape(1, d)
        h = _ffn(h, g(0), shift(0), scale(0), gate(0), ffn_w_in, ffn_w_out, l, 0, final_g, final_norm=False)
        za, zb = _proj(h, g(1), shift(1), scale(1), w_in_z, l)
        qt, qit, ckv, ckvt, ki, wit = _dsa_prep(za, zb, ckv_norm_g[l].reshape(1, KV_RANK))
        wuk = jnp.transpose(w_uk[l], (1, 0, 2)).astype(BF16)
        wuvt = jnp.transpose(w_uv[l], (1, 2, 0)).astype(BF16)
        o_a = _dsa(qt, qit, ckv, ckvt, ki, wit, wuk, wuvt, bias)
        o_r = _rwkv(zb, _rwkv_params(l, rwkv_mu, rwkv_w0, rwkv_w2, rwkv_a0, rwkv_a2, rwkv_g2, rwkv_k_k,
                                    rwkv_k_a, rwkv_r_k, rwkv_ln_w, rwkv_ln_b))
        h = _outproj(o_a, o_r, h, gate(1), w_out16, l)
        h = _ffn(h, g(2), shift(2), scale(2), gate(2), ffn_w_in, ffn_w_out, l, 1, final_g,
                 final_norm=(l == depth - 1))
    return h
```
